```python
import math
import jax, jax.numpy as jnp
from jax import lax
import numpy as np

D_MODEL = 1024
BATCH = 8
SEQ = 4096
DEPTH = 2

PLE_DIM = 256
D_FF = 2816
HEAD_DIM = 64
SB_HEADS = 8
SWA_HEADS = 8
SWA_KV_HEADS = 2
WINDOW = 128
Q_BLOCK = 128
GDN_K_HEADS = 8
GDN_V_HEADS = 16
GDN_HEAD_DIM = 128
GDN_CONV = 4
GDN_CHUNK = 64
EPS = 1e-6
N_EVEN = (DEPTH + 1) // 2
N_ODD = DEPTH // 2

SB_W = SB_HEADS * HEAD_DIM
SWA_QW = SWA_HEADS * HEAD_DIM
SWA_KVW = SWA_KV_HEADS * HEAD_DIM
ATT_IN = 3 * SB_W + SWA_QW + 2 * SWA_KVW
ATT_OUT = SB_W + SWA_QW
GDN_KW = GDN_K_HEADS * GDN_HEAD_DIM
GDN_VW = GDN_V_HEADS * GDN_HEAD_DIM
GDN_CONV_W = 2 * GDN_KW + GDN_VW
GDN_IN = GDN_CONV_W + GDN_VW + 2 * GDN_V_HEADS

kernel_name = 'hybrid_stickbreak_swa_gdn_macaron'


def rmsnorm(x, g):
    xf = x.astype(jnp.float32)
    y = xf * lax.rsqrt(jnp.mean(xf * xf, axis=-1, keepdims=True) + EPS) * g.astype(jnp.float32)
    return y.astype(x.dtype)


def l2norm(x):
    xf = x.astype(jnp.float32)
    return xf * lax.rsqrt(jnp.sum(xf * xf, axis=-1, keepdims=True) + EPS)


def swiglu(h, w_gate, w_up, w_down):
    return (jax.nn.silu(h @ w_gate) * (h @ w_up)) @ w_down


def alibi_slopes(n):
    return jnp.asarray(2.0 ** (-8.0 * (np.arange(n) + 1) / n), dtype=jnp.float32)


def stick_breaking_attention(q, k, v):
    b, h, t, d = q.shape
    nblk = t // Q_BLOCK
    qb = q.reshape(b, h, nblk, Q_BLOCK, d).transpose(2, 0, 1, 3, 4)
    key_pos = jnp.arange(t)
    scale = d ** -0.5

    def block(args):
        qi, i = args
        z = jnp.einsum('bhqd,bhkd->bhqk', qi, k).astype(jnp.float32) * scale
        q_pos = i * Q_BLOCK + jnp.arange(Q_BLOCK)
        causal = key_pos[None, :] < q_pos[:, None]
        log_keep = jnp.where(causal, jax.nn.log_sigmoid(-z), 0.0)
        log_between = lax.cumsum(log_keep, axis=3, reverse=True) - log_keep
        w = jnp.where(causal, jnp.exp(jax.nn.log_sigmoid(z) + log_between), 0.0)
        return jnp.einsum('bhqk,bhkd->bhqd', w.astype(v.dtype), v)

    out = lax.map(block, (qb, jnp.arange(nblk)))
    return out.transpose(1, 2, 0, 3, 4).reshape(b, h, t, d)


def swa_sink_attention(q, k, v, q_gain, k_gain, sinks, slopes):
    q = rmsnorm(q, q_gain)
    k = rmsnorm(k, k_gain)
    b, t, hq, d = q.shape
    hkv = k.shape[2]
    g = hq // hkv
    nblk = t // WINDOW
    qb = q.reshape(b, nblk, WINDOW, hkv, g, d)
    kpad = jnp.pad(k, ((0, 0), (WINDOW, 0), (0, 0), (0, 0)))
    vpad = jnp.pad(v, ((0, 0), (WINDOW, 0), (0, 0), (0, 0)))
    kb = jnp.concatenate([kpad[:, :t].reshape(b, nblk, WINDOW, hkv, d),
                          k.reshape(b, nblk, WINDOW, hkv, d)], axis=2)
    vb = jnp.concatenate([vpad[:, :t].reshape(b, nblk, WINDOW, hkv, d),
                          v.reshape(b, nblk, WINDOW, hkv, d)], axis=2)
    s = jnp.einsum('bnqhgd,bnkhd->bnhgqk', qb, kb).astype(jnp.float32) * (d ** -0.5)
    qi = jnp.arange(WINDOW)[:, None]
    kj = jnp.arange(2 * WINDOW)[None, :]
    dist = (qi + WINDOW - kj)
    band = (dist >= 0) & (dist < WINDOW)
    valid = band[None] & ((jnp.arange(nblk)[:, None, None] > 0) | (kj >= WINDOW)[None])
    bias = -slopes.reshape(hkv, g)[:, :, None, None] * dist.astype(jnp.float32)
    s = jnp.where(valid[None, :, None, None], s + bias[None, None], -jnp.inf)
    sink = jnp.broadcast_to(sinks.astype(jnp.float32).reshape(hkv, g)[None, None, :, :, None, None],
                            s.shape[:-1] + (1,))
    probs = jax.nn.softmax(jnp.concatenate([s, sink], axis=-1), axis=-1)[..., :-1]
    o = jnp.einsum('bnhgqk,bnkhd->bnqhgd', probs.astype(v.dtype), vb)
    return o.reshape(b, t, hq, d)


def attention_mixer(h, w_in, q_gain, k_gain, sinks, w_out):
    b, t, _ = h.shape
    proj = h @ w_in
    cuts = [SB_W, 2 * SB_W, 3 * SB_W, 3 * SB_W + SWA_QW, 3 * SB_W + SWA_QW + SWA_KVW]
    sq, sk, sv, bq, bk, bv = jnp.split(proj, cuts, axis=-1)
    heads = lambda z, n: z.reshape(b, t, n, HEAD_DIM)
    a_out = stick_breaking_attention(heads(sq, SB_HEADS).transpose(0, 2, 1, 3),
                                     heads(sk, SB_HEADS).transpose(0, 2, 1, 3),
                                     heads(sv, SB_HEADS).transpose(0, 2, 1, 3))
    b_out = swa_sink_attention(heads(bq, SWA_HEADS), heads(bk, SWA_KV_HEADS), heads(bv, SWA_KV_HEADS),
                               q_gain, k_gain, sinks, alibi_slopes(SWA_HEADS))
    o = jnp.concatenate([a_out.transpose(0, 2, 1, 3).reshape(b, t, SB_W),
                         b_out.reshape(b, t, SWA_QW)], axis=-1)
    return o @ w_out


def causal_depthwise_conv(x, w):
    kk, c = w.shape
    return lax.conv_general_dilated(x, w[:, None, :].astype(x.dtype), window_strides=(1,),
                                    padding=[(kk - 1, 0)], dimension_numbers=('NWC', 'WIO', 'NWC'),
                                    feature_group_count=c)


def chunk_gated_delta_rule(q, k, v, g, beta):
    b, t, h, dk = q.shape
    dv = v.shape[-1]
    c = GDN_CHUNK
    n = t // c
    chunks = lambda z: z.astype(jnp.float32).reshape(b, n, c, h, -1).transpose(0, 3, 1, 2, 4)
    q, k, v = chunks(q), chunks(k), chunks(v)
    g = g.astype(jnp.float32).reshape(b, n, c, h).transpose(0, 3, 1, 2)
    beta = beta.astype(jnp.float32).reshape(b, n, c, h).transpose(0, 3, 1, 2)
    gc = jnp.cumsum(g, axis=-1)
    idx = jnp.arange(c)
    lower_incl = idx[:, None] >= idx[None, :]
    strict = idx[:, None] > idx[None, :]
    decay = jnp.exp(jnp.where(lower_incl, gc[..., :, None] - gc[..., None, :], -jnp.inf))
    kbeta = k * beta[..., None]
    lmat = jnp.where(strict, jnp.einsum('bhncd,bhnsd->bhncs', kbeta, k) * decay, 0.0)
    tmat = lmat + jnp.eye(c, dtype=jnp.float32)
    rhs = jnp.concatenate([v * beta[..., None], kbeta * jnp.exp(gc)[..., None]], axis=-1)
    sol = lax.linalg.triangular_solve(tmat, rhs, left_side=True, lower=True, unit_diagonal=True)
    u, w = sol[..., :dv], sol[..., dv:]
    attn = jnp.where(lower_incl, jnp.einsum('bhncd,bhnsd->bhncs', q, k) * decay, 0.0)
    q_dec = q * jnp.exp(gc)[..., None]
    k_tail = k * jnp.exp(gc[..., -1:] - gc)[..., None]
    chunk_dec = jnp.exp(gc[..., -1])

    def step(state, inp):
        u_c, w_c, qd_c, a_c, kt_c, dec_c = inp
        v_new = u_c - jnp.einsum('bhcd,bhdv->bhcv', w_c, state)
        o_c = jnp.einsum('bhcd,bhdv->bhcv', qd_c, state) + jnp.einsum('bhcs,bhsv->bhcv', a_c, v_new)
        state = state * dec_c[..., None, None] + jnp.einsum('bhcd,bhcv->bhdv', kt_c, v_new)
        return state, o_c

    xs = tuple(jnp.moveaxis(z, 2, 0) for z in (u, w, q_dec, attn, k_tail, chunk_dec))
    s0 = jnp.zeros((b, h, dk, dv), jnp.float32)
    _, o = lax.scan(step, s0, xs)
    return o.transpose(1, 0, 3, 2, 4).reshape(b, t, h, dv)


def gdn_mixer(h, w_in, conv_w, a_log, dt_bias, out_gain, w_out):
    b, t, _ = h.shape
    proj = h @ w_in
    qkv, z, beta_logit, a = jnp.split(
        proj, [GDN_CONV_W, GDN_CONV_W + GDN_VW, GDN_CONV_W + GDN_VW + GDN_V_HEADS], axis=-1)
    qkv = jax.nn.silu(causal_depthwise_conv(qkv, conv_w))
    q, k, v = jnp.split(qkv, [GDN_KW, 2 * GDN_KW], axis=-1)
    q = l2norm(q.reshape(b, t, GDN_K_HEADS, GDN_HEAD_DIM)) * (GDN_HEAD_DIM ** -0.5)
    k = l2norm(k.reshape(b, t, GDN_K_HEADS, GDN_HEAD_DIM))
    rep = GDN_V_HEADS // GDN_K_HEADS
    q = jnp.repeat(q, rep, axis=2)
    k = jnp.repeat(k, rep, axis=2)
    v = v.reshape(b, t, GDN_V_HEADS, GDN_HEAD_DIM)
    beta = jax.nn.sigmoid(beta_logit.astype(jnp.float32))
    g = -jnp.exp(a_log.astype(jnp.float32)) * jax.nn.softplus(a.astype(jnp.float32) + dt_bias.astype(jnp.float32))
    o = chunk_gated_delta_rule(q, k, v, g, beta).astype(h.dtype)
    o = rmsnorm(o, out_gain) * jax.nn.silu(z.reshape(b, t, GDN_V_HEADS, GDN_HEAD_DIM))
    return o.reshape(b, t, GDN_VW) @ w_out


def _fwd_setup_inputs(seed: int = 0) -> dict:
    key = jax.random.key(seed)
    ks = jax.random.split(key, 24)
    f32 = jnp.float32
    nrm = lambda kk, shape, fan_in: jax.random.normal(kk, shape, f32) * (fan_in ** -0.5)
    gain = lambda kk, shape: 1.0 + 0.02 * jax.random.normal(kk, shape, f32)
    dt = jnp.exp(jax.random.uniform(ks[15], (N_ODD, GDN_V_HEADS), f32, math.log(1e-3), math.log(0.1)))
    return {
        'x': jax.random.normal(ks[0], (BATCH, SEQ, D_MODEL), f32),
        'p': jax.random.normal(ks[1], (DEPTH, BATCH, SEQ, PLE_DIM), f32),
        'ffn_norm': gain(ks[2], (DEPTH, 2, D_MODEL)),
        'ffn_w_gate': nrm(ks[3], (DEPTH, 2, D_MODEL, D_FF), D_MODEL),
        'ffn_w_up': nrm(ks[4], (DEPTH, 2, D_MODEL, D_FF), D_MODEL),
        'ffn_w_down': nrm(ks[5], (DEPTH, 2, D_FF, D_MODEL), D_FF),
        'mix_norm': gain(ks[6], (DEPTH, D_MODEL)),
        'att_w_in': nrm(ks[7], (N_EVEN, D_MODEL, ATT_IN), D_MODEL),
        'att_q_norm': gain(ks[8], (N_EVEN, HEAD_DIM)),
        'att_k_norm': gain(ks[9], (N_EVEN, HEAD_DIM)),
        'att_sinks': 0.5 * jax.random.normal(ks[10], (N_EVEN, SWA_HEADS), f32),
        'att_w_out': nrm(ks[11], (N_EVEN, ATT_OUT, D_MODEL), ATT_OUT),
        'gdn_w_in': nrm(ks[12], (N_ODD, D_MODEL, GDN_IN), D_MODEL),
        'gdn_conv_w': nrm(ks[13], (N_ODD, GDN_CONV, GDN_CONV_W), GDN_CONV),
        'gdn_a_log': jnp.log(jax.random.uniform(ks[14], (N_ODD, GDN_V_HEADS), f32, 1.0, 16.0)),
        'gdn_dt_bias': dt + jnp.log(-jnp.expm1(-dt)),
        'gdn_out_norm': gain(ks[16], (N_ODD, GDN_HEAD_DIM)),
        'gdn_w_out': nrm(ks[17], (N_ODD, GDN_VW, D_MODEL), GDN_VW),
        'ple_norm': gain(ks[18], (DEPTH, D_MODEL)),
        'ple_w_gate': nrm(ks[19], (DEPTH, D_MODEL, D_MODEL), D_MODEL),
        'ple_w_proj': nrm(ks[20], (DEPTH, PLE_DIM, D_MODEL), PLE_DIM),
    }


def _fwd_reference(x, p, ffn_norm, ffn_w_gate, ffn_w_up, ffn_w_down, mix_norm,
              att_w_in, att_q_norm, att_k_norm, att_sinks, att_w_out,
              gdn_w_in, gdn_conv_w, gdn_a_log, gdn_dt_bias, gdn_out_norm, gdn_w_out,
              ple_norm, ple_w_gate, ple_w_proj):
    h = x
    for i in range(DEPTH):
        h = h + 0.5 * swiglu(rmsnorm(h, ffn_norm[i, 0]), ffn_w_gate[i, 0], ffn_w_up[i, 0], ffn_w_down[i, 0])
        hn = rmsnorm(h, mix_norm[i])
        j = i // 2
        if i % 2 == 0:
            h = h + attention_mixer(hn, att_w_in[j], att_q_norm[j], att_k_norm[j], att_sinks[j], att_w_out[j])
        else:
            h = h + gdn_mixer(hn, gdn_w_in[j], gdn_conv_w[j], gdn_a_log[j], gdn_dt_bias[j],
                              gdn_out_norm[j], gdn_w_out[j])
        h = h + 0.5 * swiglu(rmsnorm(h, ffn_norm[i, 1]), ffn_w_gate[i, 1], ffn_w_up[i, 1], ffn_w_down[i, 1])
        gate = jax.nn.sigmoid(rmsnorm(h, ple_norm[i]) @ ple_w_gate[i])
        h = h + gate * (p[i] @ ple_w_proj[i])
    return h


import jax as _jax
import jax.numpy as _jnp

TWIN_FORMAT = 'train_step'
FWD_PARAMS = ['x', 'p', 'ffn_norm', 'ffn_w_gate', 'ffn_w_up', 'ffn_w_down', 'mix_norm', 'att_w_in', 'att_q_norm', 'att_k_norm', 'att_sinks', 'att_w_out', 'gdn_w_in', 'gdn_conv_w', 'gdn_a_log', 'gdn_dt_bias', 'gdn_out_norm', 'gdn_w_out', 'ple_norm', 'ple_w_gate', 'ple_w_proj']
TWIN_WEIGHTS = ['ffn_norm', 'ffn_w_gate', 'ffn_w_up', 'ffn_w_down', 'mix_norm', 'att_w_in', 'att_q_norm', 'att_k_norm', 'att_sinks', 'att_w_out', 'gdn_w_in', 'gdn_conv_w', 'gdn_a_log', 'gdn_dt_bias', 'gdn_out_norm', 'gdn_w_out', 'ple_norm', 'ple_w_gate', 'ple_w_proj']
TWIN_DIFF_INPUT = 'x'
TWIN_INPUTS = ['x', 'p', 'ffn_norm', 'ffn_w_gate', 'ffn_w_up', 'ffn_w_down', 'mix_norm', 'att_w_in', 'att_q_norm', 'att_k_norm', 'att_sinks', 'att_w_out', 'gdn_w_in', 'gdn_conv_w', 'gdn_a_log', 'gdn_dt_bias', 'gdn_out_norm', 'gdn_w_out', 'ple_norm', 'ple_w_gate', 'ple_w_proj', 'loss_target', 'm_ffn_norm', 'm_ffn_w_gate', 'm_ffn_w_up', 'm_ffn_w_down', 'm_mix_norm', 'm_att_w_in', 'm_att_q_norm', 'm_att_k_norm', 'm_att_sinks', 'm_att_w_out', 'm_gdn_w_in', 'm_gdn_conv_w', 'm_gdn_a_log', 'm_gdn_dt_bias', 'm_gdn_out_norm', 'm_gdn_w_out', 'm_ple_norm', 'm_ple_w_gate', 'm_ple_w_proj', 'v_ffn_norm', 'v_ffn_w_gate', 'v_ffn_w_up', 'v_ffn_w_down', 'v_mix_norm', 'v_att_w_in', 'v_att_q_norm', 'v_att_k_norm', 'v_att_sinks', 'v_att_w_out', 'v_gdn_w_in', 'v_gdn_conv_w', 'v_gdn_a_log', 'v_gdn_dt_bias', 'v_gdn_out_norm', 'v_gdn_w_out', 'v_ple_norm', 'v_ple_w_gate', 'v_ple_w_proj']
TWIN_OUTPUTS = ['loss', 'grad_x', 'grad_ffn_norm', 'grad_ffn_w_gate', 'grad_ffn_w_up', 'grad_ffn_w_down', 'grad_mix_norm', 'grad_att_w_in', 'grad_att_q_norm', 'grad_att_k_norm', 'grad_att_sinks', 'grad_att_w_out', 'grad_gdn_w_in', 'grad_gdn_conv_w', 'grad_gdn_a_log', 'grad_gdn_dt_bias', 'grad_gdn_out_norm', 'grad_gdn_w_out', 'grad_ple_norm', 'grad_ple_w_gate', 'grad_ple_w_proj', 'delta_ffn_norm', 'delta_ffn_w_gate', 'delta_ffn_w_up', 'delta_ffn_w_down', 'delta_mix_norm', 'delta_att_w_in', 'delta_att_q_norm', 'delta_att_k_norm', 'delta_att_sinks', 'delta_att_w_out', 'delta_gdn_w_in', 'delta_gdn_conv_w', 'delta_gdn_a_log', 'delta_gdn_dt_bias', 'delta_gdn_out_norm', 'delta_gdn_w_out', 'delta_ple_norm', 'delta_ple_w_gate', 'delta_ple_w_proj', 'new_m_ffn_norm', 'new_m_ffn_w_gate', 'new_m_ffn_w_up', 'new_m_ffn_w_down', 'new_m_mix_norm', 'new_m_att_w_in', 'new_m_att_q_norm', 'new_m_att_k_norm', 'new_m_att_sinks', 'new_m_att_w_out', 'new_m_gdn_w_in', 'new_m_gdn_conv_w', 'new_m_gdn_a_log', 'new_m_gdn_dt_bias', 'new_m_gdn_out_norm', 'new_m_gdn_w_out', 'new_m_ple_norm', 'new_m_ple_w_gate', 'new_m_ple_w_proj', 'new_v_ffn_norm', 'new_v_ffn_w_gate', 'new_v_ffn_w_up', 'new_v_ffn_w_down', 'new_v_mix_norm', 'new_v_att_w_in', 'new_v_att_q_norm', 'new_v_att_k_norm', 'new_v_att_sinks', 'new_v_att_w_out', 'new_v_gdn_w_in', 'new_v_gdn_conv_w', 'new_v_gdn_a_log', 'new_v_gdn_dt_bias', 'new_v_gdn_out_norm', 'new_v_gdn_w_out', 'new_v_ple_norm', 'new_v_ple_w_gate', 'new_v_ple_w_proj']
TWIN_LEAF_KINDS = {'loss': 'loss', 'grad_x': 'grad_x', 'grad_ffn_norm': 'grad_w', 'grad_ffn_w_gate': 'grad_w', 'grad_ffn_w_up': 'grad_w', 'grad_ffn_w_down': 'grad_w', 'grad_mix_norm': 'grad_w', 'grad_att_w_in': 'grad_w', 'grad_att_q_norm': 'grad_w', 'grad_att_k_norm': 'grad_w', 'grad_att_sinks': 'grad_w', 'grad_att_w_out': 'grad_w', 'grad_gdn_w_in': 'grad_w', 'grad_gdn_conv_w': 'grad_w', 'grad_gdn_a_log': 'grad_w', 'grad_gdn_dt_bias': 'grad_w', 'grad_gdn_out_norm': 'grad_w', 'grad_gdn_w_out': 'grad_w', 'grad_ple_norm': 'grad_w', 'grad_ple_w_gate': 'grad_w', 'grad_ple_w_proj': 'grad_w', 'delta_ffn_norm': 'delta_w', 'delta_ffn_w_gate': 'delta_w', 'delta_ffn_w_up': 'delta_w', 'delta_ffn_w_down': 'delta_w', 'delta_mix_norm': 'delta_w', 'delta_att_w_in': 'delta_w', 'delta_att_q_norm': 'delta_w', 'delta_att_k_norm': 'delta_w', 'delta_att_sinks': 'delta_w', 'delta_att_w_out': 'delta_w', 'delta_gdn_w_in': 'delta_w', 'delta_gdn_conv_w': 'delta_w', 'delta_gdn_a_log': 'delta_w', 'delta_gdn_dt_bias': 'delta_w', 'delta_gdn_out_norm': 'delta_w', 'delta_gdn_w_out': 'delta_w', 'delta_ple_norm': 'delta_w', 'delta_ple_w_gate': 'delta_w', 'delta_ple_w_proj': 'delta_w', 'new_m_ffn_norm': 'new_m', 'new_m_ffn_w_gate': 'new_m', 'new_m_ffn_w_up': 'new_m', 'new_m_ffn_w_down': 'new_m', 'new_m_mix_norm': 'new_m', 'new_m_att_w_in': 'new_m', 'new_m_att_q_norm': 'new_m', 'new_m_att_k_norm': 'new_m', 'new_m_att_sinks': 'new_m', 'new_m_att_w_out': 'new_m', 'new_m_gdn_w_in': 'new_m', 'new_m_gdn_conv_w': 'new_m', 'new_m_gdn_a_log': 'new_m', 'new_m_gdn_dt_bias': 'new_m', 'new_m_gdn_out_norm': 'new_m', 'new_m_gdn_w_out': 'new_m', 'new_m_ple_norm': 'new_m', 'new_m_ple_w_gate': 'new_m', 'new_m_ple_w_proj': 'new_m', 'new_v_ffn_norm': 'new_v', 'new_v_ffn_w_gate': 'new_v', 'new_v_ffn_w_up': 'new_v', 'new_v_ffn_w_down': 'new_v', 'new_v_mix_norm': 'new_v', 'new_v_att_w_in': 'new_v', 'new_v_att_q_norm': 'new_v', 'new_v_att_k_norm': 'new_v', 'new_v_att_sinks': 'new_v', 'new_v_att_w_out': 'new_v', 'new_v_gdn_w_in': 'new_v', 'new_v_gdn_conv_w': 'new_v', 'new_v_gdn_a_log': 'new_v', 'new_v_gdn_dt_bias': 'new_v', 'new_v_gdn_out_norm': 'new_v', 'new_v_gdn_w_out': 'new_v', 'new_v_ple_norm': 'new_v', 'new_v_ple_w_gate': 'new_v', 'new_v_ple_w_proj': 'new_v'}


def _forward(args):
    return _fwd_reference(*[args[k] for k in FWD_PARAMS])


def _output_shape():
    def fwd():
        inp = _fwd_setup_inputs(0)
        return _fwd_reference(*[inp[k] for k in FWD_PARAMS])
    out = _jax.eval_shape(fwd)
    return out.shape, out.dtype

N_MICROBATCH = 1
ADAM_LR = 0.001
ADAM_B1 = 0.9
ADAM_B2 = 0.999
ADAM_EPS = 1e-08
ADAM_WD = 0.01
ADAM_STEP = 10
PER_EXAMPLE_BATCH_AXIS = {'x': 0, 'p': 1, 'loss_target': 0}
SHARED_INPUTS = []
_WEIGHT_DTYPES = {'ffn_norm': _jnp.float32, 'ffn_w_gate': _jnp.float32, 'ffn_w_up': _jnp.float32, 'ffn_w_down': _jnp.float32, 'mix_norm': _jnp.float32, 'att_w_in': _jnp.float32, 'att_q_norm': _jnp.float32, 'att_k_norm': _jnp.float32, 'att_sinks': _jnp.float32, 'att_w_out': _jnp.float32, 'gdn_w_in': _jnp.float32, 'gdn_conv_w': _jnp.float32, 'gdn_a_log': _jnp.float32, 'gdn_dt_bias': _jnp.float32, 'gdn_out_norm': _jnp.float32, 'gdn_w_out': _jnp.float32, 'ple_norm': _jnp.float32, 'ple_w_gate': _jnp.float32, 'ple_w_proj': _jnp.float32}
MOMENT_SCALE = {'ffn_norm': 6.175561e+00, 'ffn_w_gate': 1.094266e-01, 'ffn_w_up': 1.205506e-01, 'ffn_w_down': 1.980989e-01, 'mix_norm': 1.176580e+01, 'att_w_in': 3.245316e-01, 'att_q_norm': 9.639875e+00, 'att_k_norm': 9.619077e+00, 'att_sinks': 2.986353e+01, 'att_w_out': 4.113304e-01, 'gdn_w_in': 2.648752e-01, 'gdn_conv_w': 3.379729e-01, 'gdn_a_log': 1.474570e+01, 'gdn_dt_bias': 1.416570e+01, 'gdn_out_norm': 8.971456e+01, 'gdn_w_out': 7.495206e-01, 'ple_norm': 9.925686e-01, 'ple_w_gate': 1.800180e-01, 'ple_w_proj': 5.283885e-01}


def _to_microbatches(a, axis):
    t = _jnp.moveaxis(a, axis, 0)
    t = t.reshape((N_MICROBATCH, t.shape[0] // N_MICROBATCH) + t.shape[1:])
    return _jnp.moveaxis(t, 1, axis + 1)


def setup_inputs(seed: int = 0) -> dict:
    inp = _fwd_setup_inputs(seed)
    key = _jax.random.fold_in(_jax.random.key(seed), 7919)
    shape, _ = _output_shape()
    out = dict(inp)
    out["loss_target"] = _jax.random.normal(_jax.random.fold_in(key, 0), shape, _jnp.float32)
    for i, name in enumerate(TWIN_WEIGHTS):
        w = inp[name].astype(_jnp.float32)
        if MOMENT_SCALE is None:
            s = _jnp.sqrt(_jnp.mean(_jnp.square(w)) + 1e-30)
        else:
            s = MOMENT_SCALE[name]
        km, kv = _jax.random.split(_jax.random.fold_in(key, i + 1))
        out[name] = w
        out["m_" + name] = s * _jax.random.normal(km, w.shape, _jnp.float32)
        out["v_" + name] = (s * s) * _jax.random.uniform(kv, w.shape, _jnp.float32, 0.5, 1.5)
    if N_MICROBATCH > 1:
        for name, axis in PER_EXAMPLE_BATCH_AXIS.items():
            out[name] = _to_microbatches(out[name], axis)
    return {'x': out['x'], 'p': out['p'], 'ffn_norm': out['ffn_norm'], 'ffn_w_gate': out['ffn_w_gate'], 'ffn_w_up': out['ffn_w_up'], 'ffn_w_down': out['ffn_w_down'], 'mix_norm': out['mix_norm'], 'att_w_in': out['att_w_in'], 'att_q_norm': out['att_q_norm'], 'att_k_norm': out['att_k_norm'], 'att_sinks': out['att_sinks'], 'att_w_out': out['att_w_out'], 'gdn_w_in': out['gdn_w_in'], 'gdn_conv_w': out['gdn_conv_w'], 'gdn_a_log': out['gdn_a_log'], 'gdn_dt_bias': out['gdn_dt_bias'], 'gdn_out_norm': out['gdn_out_norm'], 'gdn_w_out': out['gdn_w_out'], 'ple_norm': out['ple_norm'], 'ple_w_gate': out['ple_w_gate'], 'ple_w_proj': out['ple_w_proj'], 'loss_target': out['loss_target'], 'm_ffn_norm': out['m_ffn_norm'], 'm_ffn_w_gate': out['m_ffn_w_gate'], 'm_ffn_w_up': out['m_ffn_w_up'], 'm_ffn_w_down': out['m_ffn_w_down'], 'm_mix_norm': out['m_mix_norm'], 'm_att_w_in': out['m_att_w_in'], 'm_att_q_norm': out['m_att_q_norm'], 'm_att_k_norm': out['m_att_k_norm'], 'm_att_sinks': out['m_att_sinks'], 'm_att_w_out': out['m_att_w_out'], 'm_gdn_w_in': out['m_gdn_w_in'], 'm_gdn_conv_w': out['m_gdn_conv_w'], 'm_gdn_a_log': out['m_gdn_a_log'], 'm_gdn_dt_bias': out['m_gdn_dt_bias'], 'm_gdn_out_norm': out['m_gdn_out_norm'], 'm_gdn_w_out': out['m_gdn_w_out'], 'm_ple_norm': out['m_ple_norm'], 'm_ple_w_gate': out['m_ple_w_gate'], 'm_ple_w_proj': out['m_ple_w_proj'], 'v_ffn_norm': out['v_ffn_norm'], 'v_ffn_w_gate': out['v_ffn_w_gate'], 'v_ffn_w_up': out['v_ffn_w_up'], 'v_ffn_w_down': out['v_ffn_w_down'], 'v_mix_norm': out['v_mix_norm'], 'v_att_w_in': out['v_att_w_in'], 'v_att_q_norm': out['v_att_q_norm'], 'v_att_k_norm': out['v_att_k_norm'], 'v_att_sinks': out['v_att_sinks'], 'v_att_w_out': out['v_att_w_out'], 'v_gdn_w_in': out['v_gdn_w_in'], 'v_gdn_conv_w': out['v_gdn_conv_w'], 'v_gdn_a_log': out['v_gdn_a_log'], 'v_gdn_dt_bias': out['v_gdn_dt_bias'], 'v_gdn_out_norm': out['v_gdn_out_norm'], 'v_gdn_w_out': out['v_gdn_w_out'], 'v_ple_norm': out['v_ple_norm'], 'v_ple_w_gate': out['v_ple_w_gate'], 'v_ple_w_proj': out['v_ple_w_proj']}


def _loss(weights, diff, rest, loss_target):
    with _jax.named_scope("forward"):
        args = {**rest, TWIN_DIFF_INPUT: diff, **{k: w.astype(_WEIGHT_DTYPES[k]) for k, w in weights.items()}}
        y = _forward(args)
    with _jax.named_scope("loss_head"):
        err = _jnp.square(y.astype(_jnp.float32) - loss_target)
        return 0.5 * _jnp.sum(_jnp.mean(err, axis=-1)) if err.ndim else 0.5 * err


def _adamw(w, g, m, v):
    m = ADAM_B1 * m + (1.0 - ADAM_B1) * g
    v = ADAM_B2 * v + (1.0 - ADAM_B2) * _jnp.square(g)
    m_hat = m / (1.0 - ADAM_B1 ** ADAM_STEP)
    v_hat = v / (1.0 - ADAM_B2 ** ADAM_STEP)
    delta = -ADAM_LR * (m_hat / (_jnp.sqrt(v_hat) + ADAM_EPS) + ADAM_WD * w)
    return delta, m, v


def reference(x, p, ffn_norm, ffn_w_gate, ffn_w_up, ffn_w_down, mix_norm, att_w_in, att_q_norm, att_k_norm, att_sinks, att_w_out, gdn_w_in, gdn_conv_w, gdn_a_log, gdn_dt_bias, gdn_out_norm, gdn_w_out, ple_norm, ple_w_gate, ple_w_proj, loss_target, m_ffn_norm, m_ffn_w_gate, m_ffn_w_up, m_ffn_w_down, m_mix_norm, m_att_w_in, m_att_q_norm, m_att_k_norm, m_att_sinks, m_att_w_out, m_gdn_w_in, m_gdn_conv_w, m_gdn_a_log, m_gdn_dt_bias, m_gdn_out_norm, m_gdn_w_out, m_ple_norm, m_ple_w_gate, m_ple_w_proj, v_ffn_norm, v_ffn_w_gate, v_ffn_w_up, v_ffn_w_down, v_mix_norm, v_att_w_in, v_att_q_norm, v_att_k_norm, v_att_sinks, v_att_w_out, v_gdn_w_in, v_gdn_conv_w, v_gdn_a_log, v_gdn_dt_bias, v_gdn_out_norm, v_gdn_w_out, v_ple_norm, v_ple_w_gate, v_ple_w_proj):
    given = dict(x=x, p=p, ffn_norm=ffn_norm, ffn_w_gate=ffn_w_gate, ffn_w_up=ffn_w_up, ffn_w_down=ffn_w_down, mix_norm=mix_norm, att_w_in=att_w_in, att_q_norm=att_q_norm, att_k_norm=att_k_norm, att_sinks=att_sinks, att_w_out=att_w_out, gdn_w_in=gdn_w_in, gdn_conv_w=gdn_conv_w, gdn_a_log=gdn_a_log, gdn_dt_bias=gdn_dt_bias, gdn_out_norm=gdn_out_norm, gdn_w_out=gdn_w_out, ple_norm=ple_norm, ple_w_gate=ple_w_gate, ple_w_proj=ple_w_proj, loss_target=loss_target, m_ffn_norm=m_ffn_norm, m_ffn_w_gate=m_ffn_w_gate, m_ffn_w_up=m_ffn_w_up, m_ffn_w_down=m_ffn_w_down, m_mix_norm=m_mix_norm, m_att_w_in=m_att_w_in, m_att_q_norm=m_att_q_norm, m_att_k_norm=m_att_k_norm, m_att_sinks=m_att_sinks, m_att_w_out=m_att_w_out, m_gdn_w_in=m_gdn_w_in, m_gdn_conv_w=m_gdn_conv_w, m_gdn_a_log=m_gdn_a_log, m_gdn_dt_bias=m_gdn_dt_bias, m_gdn_out_norm=m_gdn_out_norm, m_gdn_w_out=m_gdn_w_out, m_ple_norm=m_ple_norm, m_ple_w_gate=m_ple_w_gate, m_ple_w_proj=m_ple_w_proj, v_ffn_norm=v_ffn_norm, v_ffn_w_gate=v_ffn_w_gate, v_ffn_w_up=v_ffn_w_up, v_ffn_w_down=v_ffn_w_down, v_mix_norm=v_mix_norm, v_att_w_in=v_att_w_in, v_att_q_norm=v_att_q_norm, v_att_k_norm=v_att_k_norm, v_att_sinks=v_att_sinks, v_att_w_out=v_att_w_out, v_gdn_w_in=v_gdn_w_in, v_gdn_conv_w=v_gdn_conv_w, v_gdn_a_log=v_gdn_a_log, v_gdn_dt_bias=v_gdn_dt_bias, v_gdn_out_norm=v_gdn_out_norm, v_gdn_w_out=v_gdn_w_out, v_ple_norm=v_ple_norm, v_ple_w_gate=v_ple_w_gate, v_ple_w_proj=v_ple_w_proj)
    weights = {n: given[n] for n in TWIN_WEIGHTS}
    shared = {n: given[n] for n in SHARED_INPUTS}
    per_example = {n: given[n] for n in ['x', 'p']}
    grad_fn = _jax.value_and_grad(_loss, argnums=(0, 1))

    def one_microbatch(ex, loss_target):
        ex = dict(ex)
        diff = ex.pop(TWIN_DIFF_INPUT)
        return grad_fn(weights, diff, {**shared, **ex}, loss_target)

    if N_MICROBATCH == 1:
        loss, (grad_w, grad_x) = one_microbatch(per_example, given["loss_target"])
    else:
        def body(carry, xs):
            loss_sum, grad_sum = carry
            l_k, (gw_k, gx_k) = one_microbatch(xs[0], xs[1])
            with _jax.named_scope("update"):
                return (loss_sum + l_k, _jax.tree.map(_jnp.add, grad_sum, gw_k)), gx_k

        init = (_jnp.zeros((), _jnp.float32), _jax.tree.map(_jnp.zeros_like, weights))
        (loss, grad_w), grad_x = _jax.lax.scan(body, init, (per_example, given["loss_target"]))
    with _jax.named_scope("update"):
        delta_w, new_m, new_v = {}, {}, {}
        for n in TWIN_WEIGHTS:
            delta_w[n], new_m[n], new_v[n] = _adamw(weights[n], grad_w[n], given["m_" + n], given["v_" + n])
    return (loss, grad_x, *[grad_w[n] for n in TWIN_WEIGHTS], *[delta_w[n] for n in TWIN_WEIGHTS],
            *[new_m[n] for n in TWIN_WEIGHTS], *[new_v[n] for n in TWIN_WEIGHTS])
```

```python
import functools
import math

import jax
import jax.numpy as jnp
from jax import lax
from jax.experimental import pallas as pl
from jax.experimental.pallas import tpu as pltpu

F32 = jnp.float32
BF16 = jnp.bfloat16
MESH = pl.DeviceIdType.MESH

LANES = 128
VMEM_LIMIT_BYTES = 56 * 1024 * 1024

EPS = 1e-6
D_MODEL = 1024
HEAD_DIM = 64
SB_HEADS = 8
SWA_HEADS = 8
SWA_KV_HEADS = 2
WINDOW = 128
GDN_K_HEADS = 8
GDN_V_HEADS = 16
GDN_HEAD_DIM = 128
GDN_CONV = 4
GDN_CHUNK = 64
SB_W = SB_HEADS * HEAD_DIM
SWA_QW = SWA_HEADS * HEAD_DIM
SWA_KVW = SWA_KV_HEADS * HEAD_DIM
GDN_KW = GDN_K_HEADS * GDN_HEAD_DIM
GDN_VW = GDN_V_HEADS * GDN_HEAD_DIM
GDN_CONV_W = 2 * GDN_KW + GDN_VW

ADAM_LR = 0.001
ADAM_B1 = 0.9
ADAM_B2 = 0.999
ADAM_EPS = 1e-08
ADAM_WD = 0.01
ADAM_STEP = 10

NEG = -1e30


def _params(*sem):
    return pltpu.CompilerParams(dimension_semantics=sem or None, vmem_limit_bytes=VMEM_LIMIT_BYTES)


def _tile(n, cap, align=LANES):
    if n <= cap:
        return n
    for t in range(cap - cap % align, 0, -align):
        if n % t == 0:
            return t
    return n


def _mm(a, b, *, name, ta=False, tb=False, out_dtype=F32, res=None, scale=1.0, tm=512, tn=1024, tk=1024):
    (k_a, m) = a.shape if ta else a.shape[::-1]
    (n, k_b) = b.shape if tb else b.shape[::-1]
    assert k_a == k_b, (a.shape, b.shape, ta, tb)
    kdim = k_a
    tm, tn, tk = _tile(m, tm), _tile(n, tn), _tile(kdim, tk)
    nk = kdim // tk
    dims = (((0 if ta else 1,), (1 if tb else 0,)), ((), ()))
    has_res = res is not None

    def body(*refs):
        a_ref, b_ref = refs[0], refs[1]
        o_ref, acc_ref = refs[-2], refs[-1]
        k = pl.program_id(2)

        @pl.when(k == 0)
        def _():
            acc_ref[...] = jnp.zeros_like(acc_ref)

        acc_ref[...] += lax.dot_general(a_ref[...].astype(BF16), b_ref[...].astype(BF16), dims,
                                        preferred_element_type=F32)

        @pl.when(k == nk - 1)
        def _():
            r = acc_ref[...]
            if scale != 1.0:
                r = r * scale
            if has_res:
                r = r + refs[2][...].astype(F32)
            o_ref[...] = r.astype(out_dtype)

    a_spec = pl.BlockSpec((tk, tm), lambda i, j, k: (k, i)) if ta else pl.BlockSpec((tm, tk), lambda i, j, k: (i, k))
    b_spec = pl.BlockSpec((tn, tk), lambda i, j, k: (j, k)) if tb else pl.BlockSpec((tk, tn), lambda i, j, k: (k, j))
    o_spec = pl.BlockSpec((tm, tn), lambda i, j, k: (i, j))
    in_specs = [a_spec, b_spec] + ([o_spec] if has_res else [])
    args = (a, b) + ((res,) if has_res else ())
    return pl.pallas_call(
        body, name=name, grid=(m // tm, n // tn, nk), in_specs=in_specs, out_specs=o_spec,
        out_shape=jax.ShapeDtypeStruct((m, n), out_dtype),
        scratch_shapes=[pltpu.VMEM((tm, tn), F32)],
        compiler_params=_params("parallel", "parallel", "arbitrary"),
    )(*args)


def _row_spec(r, tm):
    if isinstance(r, tuple):
        arr, width, cb = r
        return arr, pl.BlockSpec((tm, width), lambda i, cb=cb: (i, cb))
    return r, pl.BlockSpec((tm, r.shape[1]), lambda i: (i, 0))


def _const_spec(c):
    return pl.BlockSpec(c.shape, lambda i: (0,) * c.ndim)


def _row_fwd(fn, rows, consts, outs, *, name, tm=256):
    tm = _tile(_row_spec(rows[0], tm)[0].shape[0], tm, 8)
    arrs, specs = zip(*[_row_spec(r, tm) for r in rows])
    t = arrs[0].shape[0]
    nr, nc = len(rows), len(consts)

    def body(*refs):
        vals = [r[...].astype(F32) for r in refs[:nr + nc]]
        res = fn(*vals)
        for o_ref, v in zip(refs[nr + nc:], res):
            o_ref[...] = v.astype(o_ref.dtype)

    out = pl.pallas_call(
        body, name=name, grid=(t // tm,),
        in_specs=list(specs) + [_const_spec(c) for c in consts],
        out_specs=[pl.BlockSpec((tm, w), lambda i: (i, 0)) for w, _ in outs],
        out_shape=[jax.ShapeDtypeStruct((t, w), dt) for w, dt in outs],
        compiler_params=_params("parallel"),
    )(*arrs, *consts)
    return list(out)


def _row_bwd(fn, rows, consts, cts, row_grads, const_grads, *, name, tm=256):
    tm = _tile(_row_spec(rows[0], tm)[0].shape[0], tm, 8)
    arrs, specs = zip(*[_row_spec(r, tm) for r in rows])
    ct_arrs, ct_specs = zip(*[_row_spec(r, tm) for r in cts])
    t = arrs[0].shape[0]
    nr, nc, nt = len(rows), len(consts), len(cts)
    n_in = nr + nc + nt

    def body(*refs):
        vals = [r[...].astype(F32) for r in refs[:nr + nc]]
        ctv = tuple(r[...].astype(F32) for r in refs[nr + nc:n_in])
        _, vjp = jax.vjp(fn, *vals)
        g = vjp(ctv)
        outs = refs[n_in:]
        for (idx, _), o_ref in zip(row_grads, outs[:len(row_grads)]):
            o_ref[...] = g[idx].astype(o_ref.dtype)
        first = pl.program_id(0) == 0
        for ci, o_ref in zip(const_grads, outs[len(row_grads):]):
            @pl.when(first)
            def _(o_ref=o_ref):
                o_ref[...] = jnp.zeros_like(o_ref)

            o_ref[...] += g[nr + ci]

    widths = [(_row_spec(rows[idx], tm)[1].block_shape[1], dt) for idx, dt in row_grads]
    out = pl.pallas_call(
        body, name=name, grid=(t // tm,),
        in_specs=list(specs) + [_const_spec(c) for c in consts] + list(ct_specs),
        out_specs=[pl.BlockSpec((tm, w), lambda i: (i, 0)) for w, _ in widths]
        + [_const_spec(consts[ci]) for ci in const_grads],
        out_shape=[jax.ShapeDtypeStruct((t, w), dt) for w, dt in widths]
        + [jax.ShapeDtypeStruct(consts[ci].shape, F32) for ci in const_grads],
        compiler_params=_params("arbitrary"),
    )(*arrs, *consts, *ct_arrs)
    return list(out)


def _rms(x, g):
    return x * lax.rsqrt(jnp.mean(x * x, axis=-1, keepdims=True) + EPS) * g


def _f_rms(h, g):
    return (_rms(h, g),)


def _f_rms_res(h, g):
    return (_rms(h, g), h)


def _f_swiglu(g, u):
    return (g * jax.nn.sigmoid(g) * u,)


def _f_ple(h, gl, pp):
    return (h + jax.nn.sigmoid(gl) * pp,)


def _f_gdn_out(o, z, gain):
    outs = []
    for hd in range(GDN_V_HEADS):
        sl = slice(hd * GDN_HEAD_DIM, (hd + 1) * GDN_HEAD_DIM)
        oh, zh = o[:, sl], z[:, sl]
        outs.append(_rms(oh, gain) * (zh * jax.nn.sigmoid(zh)))
    return (jnp.concatenate(outs, axis=1),)


def _loss_head(y, target, *, name, tm=512):
    t, d = y.shape
    tm = _tile(t, tm, 8)

    def body(y_ref, t_ref, dy_ref, l_ref):
        @pl.when(pl.program_id(0) == 0)
        def _():
            l_ref[...] = jnp.zeros_like(l_ref)

        e = y_ref[...] - t_ref[...]
        dy_ref[...] = e * (1.0 / d)
        l_ref[...] += jnp.sum(e * e) * (0.5 / d)

    dy, l = pl.pallas_call(
        body, name=name, grid=(t // tm,),
        in_specs=[pl.BlockSpec((tm, d), lambda i: (i, 0))] * 2,
        out_specs=[pl.BlockSpec((tm, d), lambda i: (i, 0)), pl.BlockSpec((8, LANES), lambda i: (0, 0))],
        out_shape=[jax.ShapeDtypeStruct((t, d), F32), jax.ShapeDtypeStruct((8, LANES), F32)],
        compiler_params=_params("arbitrary"),
    )(y, target)
    return dy, l[0, 0]


def _dg(a, b, ca, cb, precision=None):
    return lax.dot_general(a, b, (((ca,), (cb,)), ((), ())), preferred_element_type=F32, precision=precision)


def _b(x):
    return x.astype(BF16)


@jax.custom_vjp
def _bdot(a, b):
    return _dg(_b(a), _b(b), 1, 0)


def _bdot_fwd(a, b):
    return _bdot(a, b), (a, b)


def _bdot_bwd(r, ct):
    a, b = r
    return _dg(_b(ct), _b(b), 1, 1), _dg(_b(a), _b(ct), 0, 0)


_bdot.defvjp(_bdot_fwd, _bdot_bwd)


@jax.custom_vjp
def _bdot_nt(a, b):
    return _dg(_b(a), _b(b), 1, 1)


def _bdot_nt_fwd(a, b):
    return _bdot_nt(a, b), (a, b)


def _bdot_nt_bwd(r, ct):
    a, b = r
    return _dg(_b(ct), _b(b), 1, 0), _dg(_b(ct), _b(a), 0, 0)


_bdot_nt.defvjp(_bdot_nt_fwd, _bdot_nt_bwd)


@jax.custom_vjp
def _bdot_tn(a, b):
    return _dg(_b(a), _b(b), 0, 0)


def _bdot_tn_fwd(a, b):
    return _bdot_tn(a, b), (a, b)


def _bdot_tn_bwd(r, ct):
    a, b = r
    return _dg(_b(b), _b(ct), 1, 1), _dg(_b(a), _b(ct), 1, 0)


_bdot_tn.defvjp(_bdot_tn_fwd, _bdot_tn_bwd)

_HI = lax.Precision.HIGHEST


@jax.custom_vjp
def _hdot(a, b):
    return _dg(a, b, 1, 0, _HI)


def _hdot_fwd(a, b):
    return _hdot(a, b), (a, b)


def _hdot_bwd(r, ct):
    a, b = r
    return _dg(ct, b, 1, 1, _HI), _dg(a, ct, 0, 0, _HI)


_hdot.defvjp(_hdot_fwd, _hdot_bwd)


def _split_dot(x, u):
    hi = x.astype(BF16)
    lo = (x - hi.astype(F32)).astype(BF16)
    return _dg(hi, u, 1, 0) + _dg(lo, u, 1, 0)


SB_BLK = 128
SB_SCALE = HEAD_DIM ** -0.5


def _sb_masks():
    lane = lax.broadcasted_iota(jnp.int32, (SB_BLK, SB_BLK), 1)
    row = lax.broadcasted_iota(jnp.int32, (SB_BLK, SB_BLK), 0)
    return lane, row, lane < HEAD_DIM


def _sb_fwd(proj, *, name):
    t = proj.shape[0]
    nb = t // SB_BLK
    npair = SB_W // LANES

    def body(q_ref, k_ref, v_ref, o_ref, r_ref):
        i = pl.program_id(1)
        lane, row, head0 = _sb_masks()
        causal = lane < row
        u_suffix = (row >= lane).astype(BF16)
        q = q_ref[...]
        qh = [_b(jnp.where(head0, q, 0.0)), _b(jnp.where(head0, 0.0, q))]

        def block(j, carry, masked):
            rows = pl.ds(pl.multiple_of(j * SB_BLK, SB_BLK), SB_BLK)
            kj, vj = _b(k_ref[rows, :]), _b(v_ref[rows, :])
            out = []
            for h in range(2):
                acc, car = carry[h]
                z = _dg(qh[h], kj, 1, 1) * SB_SCALE
                ls = jax.nn.log_sigmoid(z)
                lk = ls - z
                if masked:
                    lk = jnp.where(causal, lk, 0.0)
                suf = _split_dot(lk, u_suffix) + car
                w = jnp.exp(ls + (suf - lk))
                if masked:
                    w = jnp.where(causal, w, 0.0)
                out.append((acc + _dg(_b(w), vj, 1, 0), suf[:, 0:1]))
            return tuple(out)

        zero = (jnp.zeros((SB_BLK, LANES), F32), jnp.zeros((SB_BLK, 1), F32))
        carry = block(i, (zero, zero), True)
        carry = lax.fori_loop(0, i, lambda s, c: block(i - 1 - s, c, False), carry)
        o_ref[...] = jnp.where(head0, carry[0][0], carry[1][0])
        r_ref[...] = jnp.where(head0, carry[0][1], carry[1][1])

    return pl.pallas_call(
        body, name=name, grid=(npair, nb),
        in_specs=[pl.BlockSpec((SB_BLK, LANES), lambda p, i: (i, p)),
                  pl.BlockSpec((t, LANES), lambda p, i: (0, npair + p)),
                  pl.BlockSpec((t, LANES), lambda p, i: (0, 2 * npair + p))],
        out_specs=[pl.BlockSpec((SB_BLK, LANES), lambda p, i: (i, p)),
                   pl.BlockSpec((None, SB_BLK, LANES), lambda p, i: (p, i, 0))],
        out_shape=[jax.ShapeDtypeStruct((t, SB_W), F32), jax.ShapeDtypeStruct((npair, t, LANES), F32)],
        compiler_params=_params("parallel", "arbitrary"),
    )(proj, proj, proj)


def _sb_bwd(proj, rtot, dout, *, name):
    t = proj.shape[0]
    nb = t // SB_BLK
    npair = SB_W // LANES

    def body(q_ref, k_ref, v_ref, r_ref, do_ref, dq_ref, dk_ref, dv_ref):
        i = pl.program_id(1)
        lane, row, head0 = _sb_masks()
        causal = lane < row
        u_incl = (row <= lane).astype(BF16)
        u_excl = (row < lane).astype(BF16)
        q, do, rt = q_ref[...], do_ref[...], r_ref[...]
        qb, dob = _b(q), _b(do)
        qh = [_b(jnp.where(head0, q, 0.0)), _b(jnp.where(head0, 0.0, q))]
        doh = [_b(jnp.where(head0, do, 0.0)), _b(jnp.where(head0, 0.0, do))]
        rh = [rt[:, 0:1], rt[:, HEAD_DIM:HEAD_DIM + 1]]

        @pl.when(i == 0)
        def _():
            dk_ref[...] = jnp.zeros_like(dk_ref)
            dv_ref[...] = jnp.zeros_like(dv_ref)

        def block(j, carry, masked):
            rows = pl.ds(pl.multiple_of(j * SB_BLK, SB_BLK), SB_BLK)
            kj, vj = _b(k_ref[rows, :]), _b(v_ref[rows, :])
            out, dks, dvs = [], [], []
            for h in range(2):
                dq_acc, clk, ce = carry[h]
                z = _dg(qh[h], kj, 1, 1) * SB_SCALE
                ls = jax.nn.log_sigmoid(z)
                lk = ls - z
                if masked:
                    lk = jnp.where(causal, lk, 0.0)
                pre = _split_dot(lk, u_incl) + clk
                w = jnp.exp(ls + (rh[h] - pre))
                if masked:
                    w = jnp.where(causal, w, 0.0)
                e = _dg(doh[h], vj, 1, 1) * w
                pre_e = _split_dot(e, u_excl) + ce
                sig = jnp.exp(ls)
                dz = (e * (1.0 - sig) - sig * pre_e) * SB_SCALE
                if masked:
                    dz = jnp.where(causal, dz, 0.0)
                dzb = _b(dz)
                dks.append(_dg(dzb, qb, 0, 0))
                dvs.append(_dg(_b(w), dob, 0, 0))
                last = SB_BLK - 1
                out.append((dq_acc + _dg(dzb, kj, 1, 0), pre[:, last:], pre_e[:, last:] + e[:, last:]))
            dk_ref[rows, :] += jnp.where(head0, dks[0], dks[1])
            dv_ref[rows, :] += jnp.where(head0, dvs[0], dvs[1])
            return tuple(out)

        zero = (jnp.zeros((SB_BLK, LANES), F32), jnp.zeros((SB_BLK, 1), F32), jnp.zeros((SB_BLK, 1), F32))
        carry = lax.fori_loop(0, i, lambda j, c: block(j, c, False), (zero, zero))
        carry = block(i, carry, True)
        dq_ref[...] = jnp.where(head0, carry[0][0], carry[1][0])

    blk = pl.BlockSpec((SB_BLK, LANES), lambda p, i: (i, p))
    whole = pl.BlockSpec((t, LANES), lambda p, i: (0, p))
    return pl.pallas_call(
        body, name=name, grid=(npair, nb),
        in_specs=[blk,
                  pl.BlockSpec((t, LANES), lambda p, i: (0, npair + p)),
                  pl.BlockSpec((t, LANES), lambda p, i: (0, 2 * npair + p)),
                  pl.BlockSpec((None, SB_BLK, LANES), lambda p, i: (p, i, 0)),
                  blk],
        out_specs=[blk, whole, whole],
        out_shape=[jax.ShapeDtypeStruct((t, SB_W), F32)] * 3,
        compiler_params=_params("arbitrary", "arbitrary"),
    )(proj, proj, proj, rtot, dout)


SWA_G = SWA_HEADS // SWA_KV_HEADS


def _swa_heads(first, qs, ks, vs, qg, kg, sinks):
    qi = lax.broadcasted_iota(jnp.int32, (WINDOW, 2 * WINDOW), 0)
    kj = lax.broadcasted_iota(jnp.int32, (WINDOW, 2 * WINDOW), 1)
    dist = qi + WINDOW - kj
    valid = (dist >= 0) & (dist < WINDOW) & (jnp.logical_not(first) | (kj >= WINDOW))
    distf = dist.astype(F32)
    outs = []
    for hk in range(SWA_KV_HEADS):
        kn = _rms(ks[hk], kg)
        for g in range(SWA_G):
            h = hk * SWA_G + g
            slope = 2.0 ** (-8.0 * (h + 1) / SWA_HEADS)
            s = _bdot_nt(_rms(qs[h], qg), kn) * (HEAD_DIM ** -0.5)
            s = jnp.where(valid, s - slope * distf, NEG)
            m = lax.stop_gradient(jnp.maximum(jnp.max(s, axis=1, keepdims=True), sinks[h]))
            p = jnp.exp(s - m)
            den = jnp.sum(p, axis=1, keepdims=True) + jnp.exp(sinks[h] - m)
            outs.append(_bdot(p / den, vs[hk]))
    return tuple(outs)


def _swa_split(q, kp, kc, vp, vc, sk):
    qs = [q[:, h * HEAD_DIM:(h + 1) * HEAD_DIM] for h in range(SWA_HEADS)]
    k2, v2 = jnp.concatenate([kp, kc], axis=0), jnp.concatenate([vp, vc], axis=0)
    ks = [k2[:, h * HEAD_DIM:(h + 1) * HEAD_DIM] for h in range(SWA_KV_HEADS)]
    vs = [v2[:, h * HEAD_DIM:(h + 1) * HEAD_DIM] for h in range(SWA_KV_HEADS)]
    sinks = [sk[:, h:h + 1] for h in range(SWA_HEADS)]
    return qs, ks, vs, sinks


def _swa_specs(t):
    qcb = (3 * SB_W) // SWA_QW
    kcb = (3 * SB_W + SWA_QW) // SWA_KVW
    prev = lambda i: jnp.maximum(i - 1, 0)
    return [pl.BlockSpec((WINDOW, SWA_QW), lambda i: (i, qcb)),
            pl.BlockSpec((WINDOW, SWA_KVW), lambda i: (prev(i), kcb)),
            pl.BlockSpec((WINDOW, SWA_KVW), lambda i: (i, kcb)),
            pl.BlockSpec((WINDOW, SWA_KVW), lambda i: (prev(i), kcb + 1)),
            pl.BlockSpec((WINDOW, SWA_KVW), lambda i: (i, kcb + 1)),
            pl.BlockSpec((1, HEAD_DIM), lambda i: (0, 0)),
            pl.BlockSpec((1, HEAD_DIM), lambda i: (0, 0)),
            pl.BlockSpec((1, SWA_HEADS), lambda i: (0, 0))]


def _swa_fwd(proj, qg, kg, sinks, *, name):
    t = proj.shape[0]

    def body(q_ref, kp_ref, kc_ref, vp_ref, vc_ref, qg_ref, kg_ref, sk_ref, o_ref):
        first = pl.program_id(0) == 0
        qs, ks, vs, sk = _swa_split(q_ref[...], kp_ref[...], kc_ref[...], vp_ref[...], vc_ref[...], sk_ref[...])
        o_ref[...] = jnp.concatenate(_swa_heads(first, qs, ks, vs, qg_ref[...], kg_ref[...], sk), axis=1)

    return pl.pallas_call(
        body, name=name, grid=(t // WINDOW,), in_specs=_swa_specs(t),
        out_specs=pl.BlockSpec((WINDOW, SWA_QW), lambda i: (i, 0)),
        out_shape=jax.ShapeDtypeStruct((t, SWA_QW), F32),
        compiler_params=_params("parallel"),
    )(proj, proj, proj, proj, proj, qg, kg, sinks)


def _swa_bwd(proj, qg, kg, sinks, dout, *, name):
    t = proj.shape[0]

    def body(q_ref, kp_ref, kc_ref, vp_ref, vc_ref, qg_ref, kg_ref, sk_ref, do_ref,
             dq_ref, dk_ref, dv_ref, dqg_ref, dkg_ref, dsk_ref):
        i = pl.program_id(0)
        first = i == 0

        @pl.when(first)
        def _():
            for r in (dk_ref, dv_ref, dqg_ref, dkg_ref, dsk_ref):
                r[...] = jnp.zeros_like(r)

        qs, ks, vs, sk = _swa_split(q_ref[...], kp_ref[...], kc_ref[...], vp_ref[...], vc_ref[...], sk_ref[...])
        do = do_ref[...]
        cts = tuple(do[:, h * HEAD_DIM:(h + 1) * HEAD_DIM] for h in range(SWA_HEADS))
        _, vjp = jax.vjp(functools.partial(_swa_heads, first), qs, ks, vs, qg_ref[...], kg_ref[...], sk)
        dqs, dks, dvs, dqg, dkg, dsk = vjp(cts)
        dq_ref[...] = jnp.concatenate(dqs, axis=1)
        dk2, dv2 = jnp.concatenate(dks, axis=1), jnp.concatenate(dvs, axis=1)
        cur = pl.ds(pl.multiple_of(i * WINDOW, WINDOW), WINDOW)
        prv = pl.ds(pl.multiple_of(jnp.maximum(i - 1, 0) * WINDOW, WINDOW), WINDOW)
        dk_ref[prv, :] += dk2[:WINDOW]
        dv_ref[prv, :] += dv2[:WINDOW]
        dk_ref[cur, :] += dk2[WINDOW:]
        dv_ref[cur, :] += dv2[WINDOW:]
        dqg_ref[...] += dqg
        dkg_ref[...] += dkg
        dsk_ref[...] += jnp.concatenate(dsk, axis=1)

    whole = lambda shape: pl.BlockSpec(shape, lambda i: (0, 0))
    return pl.pallas_call(
        body, name=name, grid=(t // WINDOW,),
        in_specs=_swa_specs(t) + [pl.BlockSpec((WINDOW, SWA_QW), lambda i: (i, 0))],
        out_specs=[pl.BlockSpec((WINDOW, SWA_QW), lambda i: (i, 0)), whole((t, SWA_KVW)), whole((t, SWA_KVW)),
                   whole((1, HEAD_DIM)), whole((1, HEAD_DIM)), whole((1, SWA_HEADS))],
        out_shape=[jax.ShapeDtypeStruct((t, SWA_QW), F32), jax.ShapeDtypeStruct((t, SWA_KVW), F32),
                   jax.ShapeDtypeStruct((t, SWA_KVW), F32), jax.ShapeDtypeStruct((1, HEAD_DIM), F32),
                   jax.ShapeDtypeStruct((1, HEAD_DIM), F32), jax.ShapeDtypeStruct((1, SWA_HEADS), F32)],
        compiler_params=_params("arbitrary"),
    )(proj, proj, proj, proj, proj, qg, kg, sinks, dout)


CONV_CB = 512
CONV_TM = 512
HALO = 8


def _conv_pre(x_ref, h_ref, w_ref, i):
    halo = jnp.where(i > 0, h_ref[...], 0.0)
    xe = jnp.concatenate([halo, x_ref[...]], axis=0)
    tm = x_ref.shape[0]
    w = w_ref[...]
    c = sum(w[k:k + 1, :] * xe[HALO - (GDN_CONV - 1) + k:HALO - (GDN_CONV - 1) + k + tm] for k in range(GDN_CONV))
    return c, xe


def _conv_specs(tm, cb):
    return [pl.BlockSpec((tm, cb), lambda c, i: (i, c)),
            pl.BlockSpec((HALO, cb), lambda c, i: (jnp.maximum(i * (tm // HALO) - 1, 0), c)),
            pl.BlockSpec((GDN_CONV, cb), lambda c, i: (0, c))]


def _conv_fwd(x, w, dact=None, *, name):
    t, ch = x.shape
    tm, cb = _tile(t, CONV_TM), _tile(ch, CONV_CB)

    def body(*refs):
        x_ref, h_ref, w_ref = refs[:3]
        c, _ = _conv_pre(x_ref, h_ref, w_ref, pl.program_id(1))
        sig = jax.nn.sigmoid(c)
        if dact is None:
            refs[3][...] = c * sig
        else:
            refs[4][...] = refs[3][...] * (sig * (1.0 + c * (1.0 - sig)))

    tile = pl.BlockSpec((tm, cb), lambda c, i: (i, c))
    extra = () if dact is None else (dact,)
    return pl.pallas_call(
        body, name=name, grid=(ch // cb, t // tm),
        in_specs=_conv_specs(tm, cb) + [tile] * len(extra), out_specs=tile,
        out_shape=jax.ShapeDtypeStruct((t, ch), F32),
        compiler_params=_params("parallel", "parallel"),
    )(x, x, w, *extra)


def _conv_bwd(x, w, dc, *, name):
    t, ch = x.shape
    tm, cb = _tile(t, CONV_TM), _tile(ch, CONV_CB)
    nt = t // tm

    def body(x_ref, h_ref, w_ref, dc_ref, nh_ref, dx_ref, dw_ref):
        i = pl.program_id(1)

        @pl.when(i == 0)
        def _():
            dw_ref[...] = jnp.zeros_like(dw_ref)

        halo = jnp.where(i > 0, h_ref[...], 0.0)
        xe = jnp.concatenate([halo, x_ref[...]], axis=0)
        dc = dc_ref[...]
        dce = jnp.concatenate([dc, jnp.where(i < nt - 1, nh_ref[...], 0.0)], axis=0)
        w = w_ref[...]
        last = GDN_CONV - 1
        dx_ref[...] = sum(w[k:k + 1, :] * dce[last - k:last - k + tm] for k in range(GDN_CONV))
        dw_ref[...] += jnp.concatenate(
            [jnp.sum(dc * xe[HALO - last + k:HALO - last + k + tm], axis=0, keepdims=True) for k in range(GDN_CONV)],
            axis=0)

    tile = pl.BlockSpec((tm, cb), lambda c, i: (i, c))
    nxt = pl.BlockSpec((HALO, cb), lambda c, i: (jnp.minimum((i + 1) * (tm // HALO), t // HALO - 1), c))
    return pl.pallas_call(
        body, name=name, grid=(ch // cb, nt),
        in_specs=_conv_specs(tm, cb) + [tile, nxt],
        out_specs=[tile, pl.BlockSpec((GDN_CONV, cb), lambda c, i: (0, c))],
        out_shape=[jax.ShapeDtypeStruct((t, ch), F32), jax.ShapeDtypeStruct((GDN_CONV, ch), F32)],
        compiler_params=_params("parallel", "arbitrary"),
    )(x, x, w, dc, dc)


def _gdn_chunk(qraw, kraw, v, bl, a, alog, dtb, state):
    c, d = GDN_CHUNK, GDN_HEAD_DIM
    ri = lax.broadcasted_iota(jnp.int32, (c, c), 0)
    ci = lax.broadcasted_iota(jnp.int32, (c, c), 1)
    incl, strict = ri >= ci, ri > ci
    q = qraw * lax.rsqrt(jnp.sum(qraw * qraw, axis=-1, keepdims=True) + EPS) * (d ** -0.5)
    k = kraw * lax.rsqrt(jnp.sum(kraw * kraw, axis=-1, keepdims=True) + EPS)
    beta = jax.nn.sigmoid(bl)
    g = -jnp.exp(alog) * jax.nn.softplus(a + dtb)
    gc = _hdot(incl.astype(F32), jnp.broadcast_to(g, (c, d)))
    gcm = gc[:, :c]
    decay = jnp.exp(jnp.where(incl, gcm - gcm.T, NEG))
    eg = jnp.exp(gc)
    kbeta = k * beta
    x = -jnp.where(strict, _bdot_nt(kbeta, k) * decay, 0.0)
    tinv = (ri == ci).astype(F32) + x
    pw = x
    for _ in range(int(math.log2(c)) - 1):
        pw = _hdot(pw, pw)
        tinv = tinv + _hdot(tinv, pw)
    u = _hdot(tinv, v * beta)
    w = _hdot(tinv, kbeta * eg)
    attn = jnp.where(incl, _bdot_nt(q, k) * decay, 0.0)
    glast = gc[c - 1:c, :]
    v_new = u - _bdot(w, state)
    o = _bdot(q * eg, state) + _bdot(attn, v_new)
    state = state * jnp.exp(glast) + _bdot_tn(k * jnp.exp(glast - gc), v_new)
    return o, state


GDN_REP = GDN_V_HEADS // GDN_K_HEADS


def _gdn_pick(vals, j, r):
    ba, alog, dtb = vals
    lane = lax.broadcasted_iota(jnp.int32, ba.shape, 1)
    hv = j * GDN_REP + r
    bl = jnp.sum(jnp.where(lane == hv, ba, 0.0), axis=1, keepdims=True)
    a = jnp.sum(jnp.where(lane == GDN_V_HEADS + hv, ba, 0.0), axis=1, keepdims=True)
    lane1 = lax.broadcasted_iota(jnp.int32, alog.shape, 1)
    al = jnp.sum(jnp.where(lane1 == hv, alog, 0.0), axis=1, keepdims=True)
    db = jnp.sum(jnp.where(lane1 == hv, dtb, 0.0), axis=1, keepdims=True)
    return bl, a, al, db


def _gdn_specs(nchunk, rev):
    c, d = GDN_CHUNK, GDN_HEAD_DIM
    at = (lambda n: nchunk - 1 - n) if rev else (lambda n: n)
    kh = GDN_K_HEADS
    return at, [pl.BlockSpec((c, d), lambda n, j: (at(n), j)),
                pl.BlockSpec((c, d), lambda n, j: (at(n), kh + j)),
                pl.BlockSpec((c, GDN_REP * d), lambda n, j: (at(n), kh + j)),
                pl.BlockSpec((c, 2 * GDN_V_HEADS), lambda n, j: (at(n), 0)),
                pl.BlockSpec((1, GDN_V_HEADS), lambda n, j: (0, 0)),
                pl.BlockSpec((1, GDN_V_HEADS), lambda n, j: (0, 0))]


def _gdn_fwd(act, ba, alog, dtb, *, name):
    t = act.shape[0]
    c, d = GDN_CHUNK, GDN_HEAD_DIM
    nchunk = t // c
    at, specs = _gdn_specs(nchunk, False)

    def body(q_ref, k_ref, v_ref, ba_ref, al_ref, db_ref, o_ref, s_ref, state):
        n, j = pl.program_id(0), pl.program_id(1)

        @pl.when(n == 0)
        def _():
            state[j] = jnp.zeros((GDN_REP, d, d), F32)

        s_ref[...] = state[j]
        q, k, v = q_ref[...], k_ref[...], v_ref[...]
        small = (ba_ref[...], al_ref[...], db_ref[...])
        outs = []
        for r in range(GDN_REP):
            bl, a, al, db = _gdn_pick(small, j, r)
            o, s_new = _gdn_chunk(q, k, v[:, r * d:(r + 1) * d], bl, a, al, db, state[j, r])
            state[j, r] = s_new
            outs.append(o)
        o_ref[...] = jnp.concatenate(outs, axis=1)

    return pl.pallas_call(
        body, name=name, grid=(nchunk, GDN_K_HEADS), in_specs=specs,
        out_specs=[pl.BlockSpec((c, GDN_REP * d), lambda n, j: (n, j)),
                   pl.BlockSpec((None, None, GDN_REP, d, d), lambda n, j: (n, j, 0, 0, 0))],
        out_shape=[jax.ShapeDtypeStruct((t, GDN_VW), F32),
                   jax.ShapeDtypeStruct((nchunk, GDN_K_HEADS, GDN_REP, d, d), F32)],
        scratch_shapes=[pltpu.VMEM((GDN_K_HEADS, GDN_REP, d, d), F32)],
        compiler_params=_params("arbitrary", "arbitrary"),
    )(act, act, act, ba, alog, dtb)


def _gdn_bwd(act, ba, alog, dtb, states, dout, *, name):
    t = act.shape[0]
    c, d = GDN_CHUNK, GDN_HEAD_DIM
    nchunk = t // c
    at, specs = _gdn_specs(nchunk, True)

    def body(q_ref, k_ref, v_ref, ba_ref, al_ref, db_ref, s_ref, do_ref,
             dq_ref, dk_ref, dv_ref, dba_ref, dal_ref, ddb_ref, dstate):
        n, j = pl.program_id(0), pl.program_id(1)

        @pl.when(n == 0)
        def _():
            dstate[j] = jnp.zeros((GDN_REP, d, d), F32)

        @pl.when((n == 0) & (j == 0))
        def _():
            dal_ref[...] = jnp.zeros_like(dal_ref)
            ddb_ref[...] = jnp.zeros_like(ddb_ref)

        @pl.when(j == 0)
        def _():
            dba_ref[...] = jnp.zeros_like(dba_ref)

        q, k, v, do = q_ref[...], k_ref[...], v_ref[...], do_ref[...]
        small = (ba_ref[...], al_ref[...], db_ref[...])
        lane = lax.broadcasted_iota(jnp.int32, (c, 2 * GDN_V_HEADS), 1)
        lane1 = lax.broadcasted_iota(jnp.int32, (1, GDN_V_HEADS), 1)
        dq = jnp.zeros((c, d), F32)
        dk = jnp.zeros((c, d), F32)
        dvs = []
        dba = jnp.zeros((c, 2 * GDN_V_HEADS), F32)
        dal = jnp.zeros((1, GDN_V_HEADS), F32)
        ddb = jnp.zeros((1, GDN_V_HEADS), F32)
        for r in range(GDN_REP):
            hv = j * GDN_REP + r
            bl, a, al, db = _gdn_pick(small, j, r)
            _, vjp = jax.vjp(_gdn_chunk, q, k, v[:, r * d:(r + 1) * d], bl, a, al, db, s_ref[r])
            gq, gk, gv, gbl, ga, gal, gdb, gs = vjp((do[:, r * d:(r + 1) * d], dstate[j, r]))
            dstate[j, r] = gs
            dq, dk = dq + gq, dk + gk
            dvs.append(gv)
            dba = dba + jnp.where(lane == hv, gbl, 0.0) + jnp.where(lane == GDN_V_HEADS + hv, ga, 0.0)
            dal = dal + jnp.where(lane1 == hv, gal, 0.0)
            ddb = ddb + jnp.where(lane1 == hv, gdb, 0.0)
        dq_ref[...] = dq
        dk_ref[...] = dk
        dv_ref[...] = jnp.concatenate(dvs, axis=1)
        dba_ref[...] += dba
        dal_ref[...] += dal
        ddb_ref[...] += ddb

    small = pl.BlockSpec((1, GDN_V_HEADS), lambda n, j: (0, 0))
    return pl.pallas_call(
        body, name=name, grid=(nchunk, GDN_K_HEADS),
        in_specs=specs + [pl.BlockSpec((None, None, GDN_REP, d, d), lambda n, j: (at(n), j, 0, 0, 0)),
                          pl.BlockSpec((c, GDN_REP * d), lambda n, j: (at(n), j))],
        out_specs=[pl.BlockSpec((c, d), lambda n, j: (at(n), j)),
                   pl.BlockSpec((c, d), lambda n, j: (at(n), j)),
                   pl.BlockSpec((c, GDN_REP * d), lambda n, j: (at(n), j)),
                   pl.BlockSpec((c, 2 * GDN_V_HEADS), lambda n, j: (at(n), 0)),
                   small, small],
        out_shape=[jax.ShapeDtypeStruct((t, GDN_KW), F32), jax.ShapeDtypeStruct((t, GDN_KW), F32),
                   jax.ShapeDtypeStruct((t, GDN_VW), F32), jax.ShapeDtypeStruct((t, 2 * GDN_V_HEADS), F32),
                   jax.ShapeDtypeStruct((1, GDN_V_HEADS), F32), jax.ShapeDtypeStruct((1, GDN_V_HEADS), F32)],
        scratch_shapes=[pltpu.VMEM((GDN_K_HEADS, GDN_REP, d, d), F32)],
        compiler_params=_params("arbitrary", "arbitrary"),
    )(act, act, act, ba, alog, dtb, states, dout)


N_CHIPS = 4
N_DEV = 8
ANY = pl.BlockSpec(memory_space=pl.ANY)


def _coords():
    return lax.axis_index("x"), lax.axis_index("y"), lax.axis_index("c")


def _other_chips(x, y):
    return [(1 - x, y), (x, 1 - y), (1 - x, 1 - y)]


def _remote(src, dst, send_sems, recv_sems, k, to):
    return pltpu.make_async_remote_copy(src_ref=src, dst_ref=dst, send_sem=send_sems.at[k], recv_sem=recv_sems.at[k],
                                        device_id=to, device_id_type=MESH)


def _gather_quarters(flat, *, name):
    _, rows, w = flat.shape

    def body(x_ref, out_ref, send_sems, recv_sems, local_sem):
        x, y, c = _coords()
        sibling = (x, y, 1 - c)
        chips = _other_chips(x, y)

        def slot(px, py, half):
            return out_ref.at[2 * px + py, half]

        mine = pltpu.make_async_copy(x_ref, out_ref.at[2 * x + y], local_sem)
        mine.start()
        first = [_remote(x_ref.at[c], slot(x, y, c), send_sems, recv_sems, j, (*chip, c)) for j, chip in enumerate(chips)]
        for cp in first:
            cp.start()
        passed = [_remote(slot(*chip, c), slot(*chip, c), send_sems, recv_sems, 3 + j, sibling)
                  for j, chip in enumerate(chips)]
        for j, chip in enumerate(chips):
            _remote(x_ref.at[c], slot(*chip, c), send_sems, recv_sems, j, (*chip, c)).wait_recv()
            passed[j].start()
        for j, chip in enumerate(chips):
            _remote(x_ref.at[c], slot(*chip, 1 - c), send_sems, recv_sems, 3 + j, sibling).wait_recv()
        for cp in first + passed:
            cp.wait_send()
        mine.wait()

    return pl.pallas_call(
        body, name=name, in_specs=[ANY], out_specs=ANY,
        out_shape=jax.ShapeDtypeStruct((N_CHIPS, 2, rows, w), flat.dtype),
        scratch_shapes=[pltpu.SemaphoreType.DMA((6,)), pltpu.SemaphoreType.DMA((6,)), pltpu.SemaphoreType.DMA(())],
    )(flat)


def _swap_halves(g, *, name):
    _, _, rows, w = g.shape

    def body(g_ref, keep_ref, recv_ref, send_sems, recv_sems, local_sems):
        x, y, c = _coords()
        sibling = (x, y, 1 - c)
        local = [pltpu.make_async_copy(g_ref.at[j, c], keep_ref.at[j], local_sems.at[j]) for j in range(N_CHIPS)]
        sends = [_remote(g_ref.at[j, 1 - c], recv_ref.at[j], send_sems, recv_sems, j, sibling) for j in range(N_CHIPS)]
        for cp in local + sends:
            cp.start()
        for cp in sends:
            cp.wait()
        for cp in local:
            cp.wait()

    out = jax.ShapeDtypeStruct((N_CHIPS, rows, w), g.dtype)
    return pl.pallas_call(
        body, name=name, in_specs=[ANY], out_specs=[ANY, ANY], out_shape=[out, out],
        scratch_shapes=[pltpu.SemaphoreType.DMA((N_CHIPS,))] * 3,
    )(g)


def _scatter_quarters(part, *, name):
    _, rows, w = part.shape

    def body(p_ref, own_ref, r0_ref, r1_ref, r2_ref, send_sems, recv_sems, local_sem):
        x, y, c = _coords()
        landing = [r0_ref, r1_ref, r2_ref]
        mine = pltpu.make_async_copy(p_ref.at[2 * x + y], own_ref, local_sem)
        mine.start()
        sends = [_remote(p_ref.at[2 * cx + cy], landing[j], send_sems, recv_sems, j, (cx, cy, c))
                 for j, (cx, cy) in enumerate(_other_chips(x, y))]
        for cp in sends:
            cp.start()
        for cp in sends:
            cp.wait()
        mine.wait()

    out = jax.ShapeDtypeStruct((rows, w), part.dtype)
    return pl.pallas_call(
        body, name=name, in_specs=[ANY], out_specs=[ANY] * 4, out_shape=[out] * 4,
        scratch_shapes=[pltpu.SemaphoreType.DMA((3,)), pltpu.SemaphoreType.DMA((3,)), pltpu.SemaphoreType.DMA(())],
    )(part)


def _share_half(tot, *, name):
    rows, w = tot.shape

    def body(t_ref, out_ref, send_sems, recv_sems, local_sem):
        x, y, c = _coords()
        sibling = (x, y, 1 - c)
        mine = pltpu.make_async_copy(t_ref, out_ref.at[c], local_sem)
        mine.start()
        cp = _remote(t_ref, out_ref.at[c], send_sems, recv_sems, 0, sibling)
        cp.start()
        _remote(t_ref, out_ref.at[1 - c], send_sems, recv_sems, 0, sibling).wait_recv()
        cp.wait_send()
        mine.wait()

    return pl.pallas_call(
        body, name=name, in_specs=[ANY], out_specs=ANY, out_shape=jax.ShapeDtypeStruct((2, rows, w), tot.dtype),
        scratch_shapes=[pltpu.SemaphoreType.DMA((1,)), pltpu.SemaphoreType.DMA((1,)), pltpu.SemaphoreType.DMA(())],
    )(tot)


def _gather_all(vec, *, name):
    m, w = vec.shape

    def body(x_ref, out_ref, send_sems, recv_sems, local_sem):
        x, y, c = _coords()
        me, sibling = (x, y, c), (x, y, 1 - c)
        chips = _other_chips(x, y)

        def rows(px, py, pc):
            return out_ref.at[pl.ds((4 * px + 2 * py + pc) * m, m), :]

        def copy(k, block, to, src=None):
            return _remote(rows(*block) if src is None else src, rows(*block), send_sems, recv_sems, k, to)

        mine = pltpu.make_async_copy(x_ref, rows(*me), local_sem)
        mine.start()
        first = [copy(0, me, sibling, src=x_ref)]
        first += [copy(1 + j, me, (*chip, c), src=x_ref) for j, chip in enumerate(chips)]
        for cp in first:
            cp.start()
        passed = [copy(4 + j, (*chip, c), sibling) for j, chip in enumerate(chips)]
        for j, chip in enumerate(chips):
            copy(1 + j, (*chip, c), me).wait_recv()
            passed[j].start()
        copy(0, sibling, me).wait_recv()
        for j, chip in enumerate(chips):
            copy(4 + j, (*chip, 1 - c), me).wait_recv()
        for cp in first + passed:
            cp.wait_send()
        mine.wait()

    vm = pl.BlockSpec(memory_space=pltpu.VMEM)
    return pl.pallas_call(
        body, name=name, in_specs=[vm], out_specs=vm, out_shape=jax.ShapeDtypeStruct((N_DEV * m, w), vec.dtype),
        scratch_shapes=[pltpu.SemaphoreType.DMA((7,)), pltpu.SemaphoreType.DMA((7,)), pltpu.SemaphoreType.DMA(())],
    )(vec)


def _sum_blocks(allv, n, *, name):
    m = allv.shape[0] // n

    def body(a_ref, o_ref):
        acc = a_ref[0:m, :]
        for d in range(1, n):
            acc = acc + a_ref[d * m:(d + 1) * m, :]
        o_ref[...] = acc

    return pl.pallas_call(body, name=name, out_shape=jax.ShapeDtypeStruct((m, allv.shape[1]), allv.dtype))(allv)


def _add(arrs, *, name):
    def fn(*vals):
        acc = vals[0]
        for v in vals[1:]:
            acc = acc + v
        return (acc,)

    return _row_fwd(fn, list(arrs), [], [(arrs[0].shape[1], arrs[0].dtype)], name=name, tm=256)[0]


def _adamw(w, g, m, v, *, name):
    shape = w.shape
    last = shape[-1]
    w2, g2, m2, v2 = (a.reshape(-1, last) for a in (w, g, m, v))
    rows = w2.shape[0]
    tm = _tile(rows, max(8, ((1 << 18) // last) // 8 * 8), 8)

    def body(w_ref, g_ref, m_ref, v_ref, d_ref, nm_ref, nv_ref):
        gv = g_ref[...]
        nm = ADAM_B1 * m_ref[...] + (1.0 - ADAM_B1) * gv
        nv = ADAM_B2 * v_ref[...] + (1.0 - ADAM_B2) * (gv * gv)
        m_hat = nm / (1.0 - ADAM_B1 ** ADAM_STEP)
        v_hat = nv / (1.0 - ADAM_B2 ** ADAM_STEP)
        d_ref[...] = -ADAM_LR * (m_hat / (jnp.sqrt(v_hat) + ADAM_EPS) + ADAM_WD * w_ref[...])
        nm_ref[...] = nm
        nv_ref[...] = nv

    spec = pl.BlockSpec((tm, last), lambda i: (i, 0))
    out = jax.ShapeDtypeStruct((rows, last), F32)
    d, nm, nv = pl.pallas_call(
        body, name=name, grid=(rows // tm,), in_specs=[spec] * 4, out_specs=[spec] * 3, out_shape=[out] * 3,
        compiler_params=_params("parallel"),
    )(w2, g2, m2, v2)
    return d.reshape(shape), nm.reshape(shape), nv.reshape(shape)


_WEIGHTS = ['ffn_norm', 'ffn_w_gate', 'ffn_w_up', 'ffn_w_down', 'mix_norm', 'att_w_in', 'att_q_norm', 'att_k_norm',
            'att_sinks', 'att_w_out', 'gdn_w_in', 'gdn_conv_w', 'gdn_a_log', 'gdn_dt_bias', 'gdn_out_norm', 'gdn_w_out',
            'ple_norm', 'ple_w_gate', 'ple_w_proj']
_BIG_CUT = {'ffn_w_gate': 3, 'ffn_w_up': 3, 'ffn_w_down': 2, 'att_w_in': 2, 'att_w_out': 1, 'gdn_w_in': 2,
            'gdn_w_out': 1, 'ple_w_gate': 1, 'ple_w_proj': 2}
_SMALL_CUT = {'ffn_norm': 2, 'gdn_conv_w': 2}
_WHOLE = ['mix_norm', 'att_q_norm', 'att_k_norm', 'att_sinks', 'gdn_a_log', 'gdn_dt_bias', 'gdn_out_norm', 'ple_norm']
PACK_W = 1024
BIG_ROW_MULT = 512
SMALL_ROW_MULT = 8


def _from_quarters(blk, axis):
    full = jnp.moveaxis(blk, 0, axis)
    shp = list(full.shape)
    shp[axis:axis + 2] = [shp[axis] * shp[axis + 1]]
    return full.reshape(shp)


def _to_quarters(full, axis):
    shp = list(full.shape)
    shp[axis:axis + 1] = [N_CHIPS, shp[axis] // N_CHIPS]
    return jnp.moveaxis(full.reshape(shp), axis, 0)


def _pack(parts, row_mult):
    flat = jnp.concatenate(parts, axis=-1)
    n = flat.shape[-1]
    rows = -(-n // (PACK_W * row_mult)) * row_mult
    pad = [(0, 0)] * (flat.ndim - 1) + [(0, rows * PACK_W - n)]
    return jnp.pad(flat, pad).reshape(flat.shape[:-1] + (rows, PACK_W))


def _unpack(flat, shapes):
    lead = flat.shape[:-2]
    flat = flat.reshape(lead + (-1,))
    out, off = [], 0
    for shp in shapes:
        n = math.prod(shp)
        out.append(flat[..., off:off + n].reshape(lead + tuple(shp)))
        off += n
    return out


def _ffn_fwd(h, gain, wgu, wd, tag):
    ff = wd.shape[0]
    hn, = _row_fwd(_f_rms, [h], [gain], [(D_MODEL, BF16)], name=f"{tag}_norm")
    gu = _mm(hn, wgu, name=f"{tag}_gate_up")

    def act(guv):
        return _f_swiglu(guv[:, :ff], guv[:, ff:])

    a, = _row_fwd(act, [gu], [], [(ff, BF16)], name=f"{tag}_act")
    out = _mm(a, wd, res=h, scale=0.5, name=f"{tag}_down")
    return out, (h, hn, gu, a, act)


def _ffn_bwd(dout, saved, gain, wgu, wd, tag):
    h, hn, gu, a, act = saved
    da = _mm(dout, wd, tb=True, scale=0.5, name=f"{tag}_d_act")
    dgu, = _row_bwd(act, [gu], [], [da], [(0, BF16)], [], name=f"{tag}_d_gate_up")
    dwd = _mm(a, dout, ta=True, scale=0.5, name=f"{tag}_dw_down")
    dwgu = _mm(hn, dgu, ta=True, name=f"{tag}_dw_gate_up")
    dhn = _mm(dgu, wgu, tb=True, name=f"{tag}_d_norm")
    dh, dgain = _row_bwd(_f_rms_res, [h], [gain], [dhn, dout], [(0, F32)], [0], name=f"{tag}_d_in")
    return dh, dgain, dwgu, dwd


def _att_fwd(h, gain, w_in, qg, kg, sinks, w_out):
    hn, = _row_fwd(_f_rms, [h], [gain], [(D_MODEL, BF16)], name="att_norm")
    proj = _mm(hn, w_in, name="att_in")
    a, rtot = _sb_fwd(proj, name="att_sb")
    b = _swa_fwd(proj, qg, kg, sinks, name="att_swa")
    out = _mm(a, w_out[:SB_W], res=h, name="att_out_sb")
    out = _mm(b, w_out[SB_W:], res=out, name="att_out_swa")
    return out, (h, hn, proj, a, rtot, b)


def _att_bwd(dout, saved, gain, w_in, qg, kg, sinks, w_out):
    h, hn, proj, a, rtot, b = saved
    da = _mm(dout, w_out[:SB_W], tb=True, name="att_d_sb")
    db = _mm(dout, w_out[SB_W:], tb=True, name="att_d_swa")
    dw_out = jnp.concatenate([_mm(a, dout, ta=True, name="att_dw_out_sb"),
                              _mm(b, dout, ta=True, name="att_dw_out_swa")], axis=0)
    dq, dk, dv = _sb_bwd(proj, rtot, da, name="att_sb_bwd")
    dqb, dkb, dvb, dqg, dkg, dsk = _swa_bwd(proj, qg, kg, sinks, db, name="att_swa_bwd")
    dproj = jnp.concatenate([dq, dk, dv, dqb, dkb, dvb], axis=1)
    dw_in = _mm(hn, dproj, ta=True, name="att_dw_in")
    dhn = _mm(dproj, w_in, tb=True, name="att_d_norm")
    dh, dgain = _row_bwd(_f_rms_res, [h], [gain], [dhn, dout], [(0, F32)], [0], name="att_d_in")
    return dh, dgain, dw_in, dqg, dkg, dsk, dw_out


def _gdn_layer_fwd(h, gain, w_in, conv_w, alog, dtb, out_gain, w_out):
    w_qkv, w_z, w_ba = w_in[:, :GDN_CONV_W], w_in[:, GDN_CONV_W:GDN_CONV_W + GDN_VW], w_in[:, GDN_CONV_W + GDN_VW:]
    hn, = _row_fwd(_f_rms, [h], [gain], [(D_MODEL, BF16)], name="gdn_norm")
    pq = _mm(hn, w_qkv, name="gdn_in_qkv")
    pz = _mm(hn, w_z, name="gdn_in_z")
    ba = _mm(hn, w_ba, name="gdn_in_ba")
    act = _conv_fwd(pq, conv_w, name="gdn_conv")
    o, states = _gdn_fwd(act, ba, alog, dtb, name="gdn_rule")
    y, = _row_fwd(_f_gdn_out, [o, pz], [out_gain], [(GDN_VW, BF16)], name="gdn_gate")
    out = _mm(y, w_out, res=h, name="gdn_out")
    return out, (h, hn, pq, pz, ba, act, o, states, y, (w_qkv, w_z, w_ba))


def _gdn_layer_bwd(dout, saved, gain, conv_w, alog, dtb, out_gain, w_out):
    h, hn, pq, pz, ba, act, o, states, y, (w_qkv, w_z, w_ba) = saved
    dy = _mm(dout, w_out, tb=True, name="gdn_d_gate")
    dw_out = _mm(y, dout, ta=True, name="gdn_dw_out")
    do, dpz, dout_gain = _row_bwd(_f_gdn_out, [o, pz], [out_gain], [dy], [(0, F32), (1, F32)], [0], name="gdn_gate_bwd")
    dq, dk, dv, dba, dal, ddb = _gdn_bwd(act, ba, alog, dtb, states, do, name="gdn_rule_bwd")
    dact = jnp.concatenate([dq, dk, dv], axis=1)
    dc = _conv_fwd(pq, conv_w, dact, name="gdn_conv_d_pre")
    dpq, dconv = _conv_bwd(pq, conv_w, dc, name="gdn_conv_bwd")
    dw_in = jnp.concatenate([_mm(hn, dpq, ta=True, name="gdn_dw_qkv"), _mm(hn, dpz, ta=True, name="gdn_dw_z"),
                             _mm(hn, dba, ta=True, name="gdn_dw_ba")], axis=1)
    dhn = _mm(dpq, w_qkv, tb=True, name="gdn_d_norm_qkv")
    dhn = _mm(dpz, w_z, tb=True, res=dhn, name="gdn_d_norm_z")
    dhn = _mm(dba, w_ba, tb=True, res=dhn, name="gdn_d_norm_ba")
    dh, dgain = _row_bwd(_f_rms_res, [h], [gain], [dhn, dout], [(0, F32)], [0], name="gdn_d_in")
    return dh, dgain, dw_in, dconv, dal, ddb, dout_gain, dw_out


def _ple_fwd(h, gain, w_gate, w_proj, pe, tag):
    hn, = _row_fwd(_f_rms, [h], [gain], [(D_MODEL, BF16)], name=f"{tag}_norm")
    gl = _mm(hn, w_gate, name=f"{tag}_gate")
    pp = _mm(pe, w_proj, name=f"{tag}_proj")
    out, = _row_fwd(_f_ple, [h, gl, pp], [], [(D_MODEL, F32)], name=f"{tag}_mix")
    return out, (h, hn, gl, pp)


def _ple_bwd(dout, saved, gain, w_gate, pe, tag):
    h, hn, gl, pp = saved
    dha, dgl, dpp = _row_bwd(_f_ple, [h, gl, pp], [], [dout], [(0, F32), (1, BF16), (2, BF16)], [], name=f"{tag}_mix_bwd")
    dw_gate = _mm(hn, dgl, ta=True, name=f"{tag}_dw_gate")
    dw_proj = _mm(pe, dpp, ta=True, name=f"{tag}_dw_proj")
    dhn = _mm(dgl, w_gate, tb=True, name=f"{tag}_d_norm")
    dh, dgain = _row_bwd(_f_rms_res, [h], [gain], [dhn, dha], [(0, F32)], [0], name=f"{tag}_d_in")
    return dh, dgain, dw_gate, dw_proj


def kernel(x, p, ffn_norm, ffn_w_gate, ffn_w_up, ffn_w_down, mix_norm, att_w_in, att_q_norm, att_k_norm, att_sinks, att_w_out, gdn_w_in, gdn_conv_w, gdn_a_log, gdn_dt_bias, gdn_out_norm, gdn_w_out, ple_norm, ple_w_gate, ple_w_proj, loss_target, m_ffn_norm, m_ffn_w_gate, m_ffn_w_up, m_ffn_w_down, m_mix_norm, m_att_w_in, m_att_q_norm, m_att_k_norm, m_att_sinks, m_att_w_out, m_gdn_w_in, m_gdn_conv_w, m_gdn_a_log, m_gdn_dt_bias, m_gdn_out_norm, m_gdn_w_out, m_ple_norm, m_ple_w_gate, m_ple_w_proj, v_ffn_norm, v_ffn_w_gate, v_ffn_w_up, v_ffn_w_down, v_mix_norm, v_att_w_in, v_att_q_norm, v_att_k_norm, v_att_sinks, v_att_w_out, v_gdn_w_in, v_gdn_conv_w, v_gdn_a_log, v_gdn_dt_bias, v_gdn_out_norm, v_gdn_w_out, v_ple_norm, v_ple_w_gate, v_ple_w_proj):
    arg = dict(locals())
    t = x.shape[1]
    chip = 2 * lax.axis_index("x") + lax.axis_index("y")
    n_layers = ffn_norm.shape[0]

    big_names = list(_BIG_CUT)
    big_shapes = [arg[n].shape for n in big_names]
    flat = _pack([arg[n].astype(BF16).reshape(-1) for n in big_names], BIG_ROW_MULT)
    rows = flat.shape[0]
    gathered = _gather_quarters(flat.reshape(2, rows // 2, PACK_W), name="gather_weights")
    quarters = _unpack(gathered.reshape(N_CHIPS, rows, PACK_W), big_shapes)
    wt = {n: _from_quarters(q, _BIG_CUT[n]) for n, q in zip(big_names, quarters)}

    small_names = list(_SMALL_CUT)
    small_shapes = [arg[n].shape for n in small_names]
    svec = _pack([arg[n].reshape(-1) for n in small_names], SMALL_ROW_MULT)
    srows = svec.shape[0]
    sall = _gather_all(svec, name="gather_gains").reshape(N_CHIPS, 2, srows, PACK_W)[:, 0]
    for n, q in zip(small_names, _unpack(sall, small_shapes)):
        wt[n] = _from_quarters(q, _SMALL_CUT[n])
    for n in _WHOLE:
        wt[n] = arg[n]

    wgu = [[jnp.concatenate([wt['ffn_w_gate'][i, k], wt['ffn_w_up'][i, k]], axis=1) for k in range(2)]
           for i in range(n_layers)]
    ff = wt['ffn_w_down'].shape[2]
    row = lambda v: v.reshape(1, -1)

    h = x[0]
    tape = []
    for i in range(n_layers):
        j = i // 2
        h, s0 = _ffn_fwd(h, row(wt['ffn_norm'][i, 0]), wgu[i][0], wt['ffn_w_down'][i, 0], f"ffn{i}a")
        if i % 2 == 0:
            h, sm = _att_fwd(h, row(mix_norm[i]), wt['att_w_in'][j], att_q_norm[j:j + 1], att_k_norm[j:j + 1],
                             att_sinks[j:j + 1], wt['att_w_out'][j])
        else:
            h, sm = _gdn_layer_fwd(h, row(mix_norm[i]), wt['gdn_w_in'][j], wt['gdn_conv_w'][j], gdn_a_log[j:j + 1],
                                   gdn_dt_bias[j:j + 1], gdn_out_norm[j:j + 1], wt['gdn_w_out'][j])
        h, s1 = _ffn_fwd(h, row(wt['ffn_norm'][i, 1]), wgu[i][1], wt['ffn_w_down'][i, 1], f"ffn{i}b")
        h, sp = _ple_fwd(h, row(ple_norm[i]), wt['ple_w_gate'][i], wt['ple_w_proj'][i], p[i, 0], f"ple{i}")
        tape.append((s0, sm, s1, sp))

    dh, loss_local = _loss_head(h, loss_target[0], name="loss_head")
    loss = lax.psum(loss_local, ("x", "y", "c"))

    gr = {}
    d_ffn_norm = [[None, None] for _ in range(n_layers)]
    d_gu = [[None, None] for _ in range(n_layers)]
    d_down = [[None, None] for _ in range(n_layers)]
    d_mix, d_ple_norm, d_ple_gate, d_ple_proj = [None] * n_layers, [None] * n_layers, [None] * n_layers, [None] * n_layers
    for i in reversed(range(n_layers)):
        j = i // 2
        s0, sm, s1, sp = tape[i]
        dh, d_ple_norm[i], d_ple_gate[i], d_ple_proj[i] = _ple_bwd(dh, sp, row(ple_norm[i]), wt['ple_w_gate'][i], p[i, 0],
                                                                   f"ple{i}")
        dh, d_ffn_norm[i][1], d_gu[i][1], d_down[i][1] = _ffn_bwd(dh, s1, row(wt['ffn_norm'][i, 1]), wgu[i][1],
                                                                  wt['ffn_w_down'][i, 1], f"ffn{i}b")
        if i % 2 == 0:
            (dh, d_mix[i], gr['att_w_in'], gr['att_q_norm'], gr['att_k_norm'], gr['att_sinks'],
             gr['att_w_out']) = _att_bwd(dh, sm, row(mix_norm[i]), wt['att_w_in'][j], att_q_norm[j:j + 1],
                                         att_k_norm[j:j + 1], att_sinks[j:j + 1], wt['att_w_out'][j])
        else:
            (dh, d_mix[i], gr['gdn_w_in'], gr['gdn_conv_w'], gr['gdn_a_log'], gr['gdn_dt_bias'], gr['gdn_out_norm'],
             gr['gdn_w_out']) = _gdn_layer_bwd(dh, sm, row(mix_norm[i]), wt['gdn_conv_w'][j], gdn_a_log[j:j + 1],
                                               gdn_dt_bias[j:j + 1], gdn_out_norm[j:j + 1], wt['gdn_w_out'][j])
        dh, d_ffn_norm[i][0], d_gu[i][0], d_down[i][0] = _ffn_bwd(dh, s0, row(wt['ffn_norm'][i, 0]), wgu[i][0],
                                                                  wt['ffn_w_down'][i, 0], f"ffn{i}a")
    grad_x = dh[None]

    stack2 = lambda tbl, f: jnp.stack([jnp.stack([f(tbl[i][k]) for k in range(2)]) for i in range(n_layers)])
    gr['ffn_norm'] = stack2(d_ffn_norm, lambda a: a[0])
    gr['ffn_w_gate'] = stack2(d_gu, lambda a: a[:, :ff])
    gr['ffn_w_up'] = stack2(d_gu, lambda a: a[:, ff:])
    gr['ffn_w_down'] = stack2(d_down, lambda a: a)
    gr['mix_norm'] = jnp.concatenate(d_mix, axis=0)
    gr['ple_norm'] = jnp.concatenate(d_ple_norm, axis=0)
    gr['ple_w_gate'] = jnp.stack(d_ple_gate)
    gr['ple_w_proj'] = jnp.stack(d_ple_proj)
    for n in ('att_w_in', 'att_w_out', 'gdn_w_in', 'gdn_conv_w', 'gdn_w_out'):
        gr[n] = gr[n][None]

    gflat = _pack([_to_quarters(gr[n], _BIG_CUT[n]).reshape(N_CHIPS, -1) for n in big_names], BIG_ROW_MULT)
    half = rows // 2
    kept, got = _swap_halves(gflat.reshape(N_CHIPS, 2, half, PACK_W), name="grad_swap_halves")
    pair = _add([kept.reshape(N_CHIPS * half, PACK_W), got.reshape(N_CHIPS * half, PACK_W)], name="grad_add_pair")
    own, r0, r1, r2 = _scatter_quarters(pair.reshape(N_CHIPS, half, PACK_W), name="grad_scatter")
    tot = _add([own, r0, r1, r2], name="grad_add_chips")
    gbig = _unpack(_share_half(tot, name="grad_share").reshape(rows, PACK_W), big_shapes)
    gsum = dict(zip(big_names, gbig))

    whole_shapes = [arg[n].shape for n in _WHOLE]
    cut_full_shapes = [gr[n].shape for n in small_names]
    gvec = _pack([gr[n].reshape(-1) for n in _WHOLE + small_names], SMALL_ROW_MULT)
    gall = _sum_blocks(_gather_all(gvec, name="gather_small_grads"), N_DEV, name="sum_small_grads")
    parts = _unpack(gall, whole_shapes + cut_full_shapes)
    for n, g in zip(_WHOLE, parts):
        gsum[n] = g
    for n, g in zip(small_names, parts[len(_WHOLE):]):
        gsum[n] = lax.dynamic_index_in_dim(_to_quarters(g, _SMALL_CUT[n]), chip, axis=0, keepdims=False)

    delta, new_m, new_v = {}, {}, {}
    for n in _WEIGHTS:
        delta[n], new_m[n], new_v[n] = _adamw(arg[n], gsum[n], arg["m_" + n], arg["v_" + n], name=f"adamw_{n}")
    return (loss, grad_x, *[gsum[n] for n in _WEIGHTS], *[delta[n] for n in _WEIGHTS],
            *[new_m[n] for n in _WEIGHTS], *[new_v[n] for n in _WEIGHTS])
```

```python
import functools
import math

import jax
import jax.numpy as jnp
from jax import lax
from jax.experimental import pallas as pl
from jax.experimental.pallas import tpu as pltpu

F32 = jnp.float32
BF16 = jnp.bfloat16
MESH = pl.DeviceIdType.MESH

LANES = 128
VMEM_LIMIT_BYTES = 56 * 1024 * 1024

EPS = 1e-6
D_MODEL = 1024
HEAD_DIM = 64
SB_HEADS = 8
SWA_HEADS = 8
SWA_KV_HEADS = 2
WINDOW = 128
GDN_K_HEADS = 8
GDN_V_HEADS = 16
GDN_HEAD_DIM = 128
GDN_CONV = 4
GDN_CHUNK = 64
SB_W = SB_HEADS * HEAD_DIM
SWA_QW = SWA_HEADS * HEAD_DIM
SWA_KVW = SWA_KV_HEADS * HEAD_DIM
GDN_KW = GDN_K_HEADS * GDN_HEAD_DIM
GDN_VW = GDN_V_HEADS * GDN_HEAD_DIM
GDN_CONV_W = 2 * GDN_KW + GDN_VW

ADAM_LR = 0.001
ADAM_B1 = 0.9
ADAM_B2 = 0.999
ADAM_EPS = 1e-08
ADAM_WD = 0.01
ADAM_STEP = 10

NEG = -1e30


def _params(*sem):
    return pltpu.CompilerParams(dimension_semantics=sem or None, vmem_limit_bytes=VMEM_LIMIT_BYTES)


def _tile(n, cap, align=LANES):
    if n <= cap:
        return n
    for t in range(cap - cap % align, 0, -align):
        if n % t == 0:
            return t
    return n


N_CHIPS = 4
Q = "q"


def _opnd(x):
    return x if isinstance(x, tuple) else (x, ())


def _mm(a, b, *, name, ta=False, tb=False, out_dtype=F32, res=None, scale=1.0, out_q=False, into=None,
        tm=512, tn=1024, tk=1024):
    (a_arr, a_lead), (b_arr, b_lead) = _opnd(a), _opnd(b)
    (k_a, m) = a_arr.shape[-2:] if ta else a_arr.shape[-2:][::-1]
    (n, k_b) = b_arr.shape[-2:] if tb else b_arr.shape[-2:][::-1]
    if into is not None:
        out_arr, out_lead = into
        out_q, out_dtype = Q in out_lead, out_arr.dtype
    else:
        out_lead = (Q,) if out_q else ()
    red_q = (Q in a_lead or Q in b_lead) and not out_q
    kq = min(k_a, k_b)
    assert (k_a == k_b) or (red_q and max(k_a, k_b) == N_CHIPS * kq), (a_arr.shape, b_arr.shape)
    tm, tn, tk = _tile(m, tm), _tile(n, tn), _tile(kq, tk)
    nk = kq // tk
    ksteps = nk * (N_CHIPS if red_q else 1)
    dims = (((0 if ta else 1,), (1 if tb else 0,)), ((), ()))
    has_res = res is not None

    def body(*refs):
        a_ref, b_ref = refs[0], refs[1]
        o_ref, acc_ref = refs[-2], refs[-1]
        k = pl.program_id(3)

        @pl.when(k == 0)
        def _():
            acc_ref[...] = jnp.zeros_like(acc_ref)

        acc_ref[...] += lax.dot_general(a_ref[...].astype(BF16), b_ref[...].astype(BF16), dims,
                                        preferred_element_type=F32)

        @pl.when(k == ksteps - 1)
        def _():
            r = acc_ref[...]
            if scale != 1.0:
                r = r * scale
            if has_res:
                r = r + refs[2][...].astype(F32)
            o_ref[...] = r.astype(o_ref.dtype)

    def spec(lead, blk, pos):
        def index(s, i, j, k):
            kk = k % nk if (red_q and Q in lead) else k
            quarter = s if out_q else k // nk
            return tuple(quarter if l == Q else l for l in lead) + pos(i, j, kk)
        return pl.BlockSpec((None,) * len(lead) + blk, index)

    a_spec = spec(a_lead, (tk, tm), lambda i, j, k: (k, i)) if ta else spec(a_lead, (tm, tk), lambda i, j, k: (i, k))
    b_spec = spec(b_lead, (tn, tk), lambda i, j, k: (j, k)) if tb else spec(b_lead, (tk, tn), lambda i, j, k: (k, j))
    o_spec = spec(out_lead, (tm, tn), lambda i, j, k: (i, j))
    in_specs, args = [a_spec, b_spec], [a_arr, b_arr]
    if has_res:
        r_arr, r_lead = _opnd(res)
        in_specs.append(spec(r_lead, (tm, tn), lambda i, j, k: (i, j)))
        args.append(r_arr)
    aliases = {}
    if into is not None:
        in_specs.append(pl.BlockSpec(memory_space=pl.ANY))
        args.append(out_arr)
        aliases = {len(args) - 1: 0}
        out_shape = jax.ShapeDtypeStruct(out_arr.shape, out_arr.dtype)
    else:
        out_shape = jax.ShapeDtypeStruct(((N_CHIPS,) if out_q else ()) + (m, n), out_dtype)
    return pl.pallas_call(
        body, name=name, grid=(N_CHIPS if out_q else 1, m // tm, n // tn, ksteps), in_specs=in_specs, out_specs=o_spec,
        out_shape=out_shape, scratch_shapes=[pltpu.VMEM((tm, tn), F32)], input_output_aliases=aliases,
        compiler_params=_params("parallel", "parallel", "parallel", "arbitrary"),
    )(*args)


def _row_spec(r, tm):
    if isinstance(r, tuple):
        arr, width, cb = r
        return arr, pl.BlockSpec((tm, width), lambda i, cb=cb: (i, cb))
    return r, pl.BlockSpec((tm, r.shape[1]), lambda i: (i, 0))


def _const_spec(c):
    return pl.BlockSpec(c.shape, lambda i: (0,) * c.ndim)


def _row_fwd(fn, rows, consts, outs, *, name, tm=256):
    tm = _tile(_row_spec(rows[0], tm)[0].shape[0], tm, 8)
    arrs, specs = zip(*[_row_spec(r, tm) for r in rows])
    t = arrs[0].shape[0]
    nr, nc = len(rows), len(consts)

    def body(*refs):
        vals = [r[...].astype(F32) for r in refs[:nr + nc]]
        res = fn(*vals)
        for o_ref, v in zip(refs[nr + nc:], res):
            o_ref[...] = v.astype(o_ref.dtype)

    out = pl.pallas_call(
        body, name=name, grid=(t // tm,),
        in_specs=list(specs) + [_const_spec(c) for c in consts],
        out_specs=[pl.BlockSpec((tm, w), lambda i: (i, 0)) for w, _ in outs],
        out_shape=[jax.ShapeDtypeStruct((t, w), dt) for w, dt in outs],
        compiler_params=_params("parallel"),
    )(*arrs, *consts)
    return list(out)


def _row_bwd(fn, rows, consts, cts, row_grads, const_grads, *, name, tm=256):
    tm = _tile(_row_spec(rows[0], tm)[0].shape[0], tm, 8)
    arrs, specs = zip(*[_row_spec(r, tm) for r in rows])
    ct_arrs, ct_specs = zip(*[_row_spec(r, tm) for r in cts])
    t = arrs[0].shape[0]
    nr, nc, nt = len(rows), len(consts), len(cts)
    n_in = nr + nc + nt

    def body(*refs):
        vals = [r[...].astype(F32) for r in refs[:nr + nc]]
        ctv = tuple(r[...].astype(F32) for r in refs[nr + nc:n_in])
        _, vjp = jax.vjp(fn, *vals)
        g = vjp(ctv)
        outs = refs[n_in:]
        for (idx, _), o_ref in zip(row_grads, outs[:len(row_grads)]):
            o_ref[...] = g[idx].astype(o_ref.dtype)
        first = pl.program_id(0) == 0
        for ci, o_ref in zip(const_grads, outs[len(row_grads):]):
            @pl.when(first)
            def _(o_ref=o_ref):
                o_ref[...] = jnp.zeros_like(o_ref)

            o_ref[...] += g[nr + ci]

    widths = [(_row_spec(rows[idx], tm)[1].block_shape[1], dt) for idx, dt in row_grads]
    out = pl.pallas_call(
        body, name=name, grid=(t // tm,),
        in_specs=list(specs) + [_const_spec(c) for c in consts] + list(ct_specs),
        out_specs=[pl.BlockSpec((tm, w), lambda i: (i, 0)) for w, _ in widths]
        + [_const_spec(consts[ci]) for ci in const_grads],
        out_shape=[jax.ShapeDtypeStruct((t, w), dt) for w, dt in widths]
        + [jax.ShapeDtypeStruct(consts[ci].shape, F32) for ci in const_grads],
        compiler_params=_params("arbitrary"),
    )(*arrs, *consts, *ct_arrs)
    return list(out)


def _rms(x, g):
    return x * lax.rsqrt(jnp.mean(x * x, axis=-1, keepdims=True) + EPS) * g


def _f_rms(h, g):
    return (_rms(h, g),)


def _f_rms_res(h, g):
    return (_rms(h, g), h)


def _f_swiglu(g, u):
    return (g * jax.nn.sigmoid(g) * u,)


def _f_ple(h, gl, pp):
    return (h + jax.nn.sigmoid(gl) * pp,)


def _f_gdn_out(o, z, gain):
    outs = []
    for hd in range(GDN_V_HEADS):
        sl = slice(hd * GDN_HEAD_DIM, (hd + 1) * GDN_HEAD_DIM)
        oh, zh = o[:, sl], z[:, sl]
        outs.append(_rms(oh, gain) * (zh * jax.nn.sigmoid(zh)))
    return (jnp.concatenate(outs, axis=1),)


def _loss_head(y, target, *, name, tm=512):
    t, d = y.shape
    tm = _tile(t, tm, 8)

    def body(y_ref, t_ref, dy_ref, l_ref):
        @pl.when(pl.program_id(0) == 0)
        def _():
            l_ref[...] = jnp.zeros_like(l_ref)

        e = y_ref[...] - t_ref[...]
        dy_ref[...] = e * (1.0 / d)
        l_ref[...] += jnp.sum(e * e) * (0.5 / d)

    dy, l = pl.pallas_call(
        body, name=name, grid=(t // tm,),
        in_specs=[pl.BlockSpec((tm, d), lambda i: (i, 0))] * 2,
        out_specs=[pl.BlockSpec((tm, d), lambda i: (i, 0)), pl.BlockSpec((8, LANES), lambda i: (0, 0))],
        out_shape=[jax.ShapeDtypeStruct((t, d), F32), jax.ShapeDtypeStruct((8, LANES), F32)],
        compiler_params=_params("arbitrary"),
    )(y, target)
    return dy, l[0, 0]


def _dg(a, b, ca, cb, precision=None):
    return lax.dot_general(a, b, (((ca,), (cb,)), ((), ())), preferred_element_type=F32, precision=precision)


def _b(x):
    return x.astype(BF16)


@jax.custom_vjp
def _bdot(a, b):
    return _dg(_b(a), _b(b), 1, 0)


def _bdot_fwd(a, b):
    return _bdot(a, b), (a, b)


def _bdot_bwd(r, ct):
    a, b = r
    return _dg(_b(ct), _b(b), 1, 1), _dg(_b(a), _b(ct), 0, 0)


_bdot.defvjp(_bdot_fwd, _bdot_bwd)


@jax.custom_vjp
def _bdot_nt(a, b):
    return _dg(_b(a), _b(b), 1, 1)


def _bdot_nt_fwd(a, b):
    return _bdot_nt(a, b), (a, b)


def _bdot_nt_bwd(r, ct):
    a, b = r
    return _dg(_b(ct), _b(b), 1, 0), _dg(_b(ct), _b(a), 0, 0)


_bdot_nt.defvjp(_bdot_nt_fwd, _bdot_nt_bwd)


@jax.custom_vjp
def _bdot_tn(a, b):
    return _dg(_b(a), _b(b), 0, 0)


def _bdot_tn_fwd(a, b):
    return _bdot_tn(a, b), (a, b)


def _bdot_tn_bwd(r, ct):
    a, b = r
    return _dg(_b(b), _b(ct), 1, 1), _dg(_b(a), _b(ct), 1, 0)


_bdot_tn.defvjp(_bdot_tn_fwd, _bdot_tn_bwd)


def _two(x):
    hi = x.astype(BF16)
    return hi, (x - hi.astype(F32)).astype(BF16)


def _dg3(a, b, ca, cb):
    (ah, al), (bh, bl) = _two(a), _two(b)
    return _dg(ah, bh, ca, cb) + (_dg(ah, bl, ca, cb) + _dg(al, bh, ca, cb))


@jax.custom_vjp
def _hdot(a, b):
    return _dg3(a, b, 1, 0)


def _hdot_fwd(a, b):
    return _hdot(a, b), (a, b)


def _hdot_bwd(r, ct):
    a, b = r
    return _dg3(ct, b, 1, 1), _dg3(a, ct, 0, 0)


_hdot.defvjp(_hdot_fwd, _hdot_bwd)


def _split_dot(x, u):
    hi, lo = _two(x)
    return _dg(hi, u, 1, 0) + _dg(lo, u, 1, 0)


@jax.custom_vjp
def _ldot(l01, x):
    hi, lo = _two(x)
    l01 = l01.astype(BF16)
    return _dg(l01, hi, 1, 0) + _dg(l01, lo, 1, 0)


def _ldot_fwd(l01, x):
    return _ldot(l01, x), l01


def _ldot_bwd(l01, ct):
    hi, lo = _two(ct)
    l01b = l01.astype(BF16)
    return jnp.zeros_like(l01), _dg(l01b, hi, 0, 0) + _dg(l01b, lo, 0, 0)


_ldot.defvjp(_ldot_fwd, _ldot_bwd)


SB_BLK = 128
SB_KEYS = 512
SB_SCALE = HEAD_DIM ** -0.5


def _sb_consts(t):
    kb = min(SB_KEYS, t)
    lane = lax.broadcasted_iota(jnp.int32, (SB_BLK, kb), 1)
    row = lax.broadcasted_iota(jnp.int32, (SB_BLK, kb), 0)
    ur = lax.broadcasted_iota(jnp.int32, (kb, kb), 0)
    uc = lax.broadcasted_iota(jnp.int32, (kb, kb), 1)
    head0 = lax.broadcasted_iota(jnp.int32, (SB_BLK, LANES), 1) < HEAD_DIM
    return kb, lane, row, ur, uc, head0


def _sb_fwd(proj, *, name):
    t = proj.shape[0]
    nb = t // SB_BLK
    npair = SB_W // LANES

    def body(q_ref, k_ref, v_ref, o_ref, r_ref):
        i = pl.program_id(1)
        kb, lane, row, ur, uc, head0 = _sb_consts(t)
        u_suffix = (ur >= uc).astype(BF16)
        q = q_ref[...]
        qh = [_b(jnp.where(head0, q, 0.0)), _b(jnp.where(head0, 0.0, q))]
        diag = (i * SB_BLK) // kb

        def block(j, carry, masked):
            keys = pl.ds(pl.multiple_of(j * kb, kb), kb)
            kj, vj = _b(k_ref[keys, :]), _b(v_ref[keys, :])
            causal = (j * kb + lane) < (i * SB_BLK + row)
            out = []
            for h in range(2):
                acc, car = carry[h]
                z = _dg(qh[h], kj, 1, 1) * SB_SCALE
                ls = jax.nn.log_sigmoid(z)
                lk = ls - z
                if masked:
                    lk = jnp.where(causal, lk, 0.0)
                suf = _split_dot(lk, u_suffix) + car
                w = jnp.exp(ls + (suf - lk))
                if masked:
                    w = jnp.where(causal, w, 0.0)
                out.append((acc + _dg(_b(w), vj, 1, 0), suf[:, 0:1]))
            return tuple(out)

        zero = (jnp.zeros((SB_BLK, LANES), F32), jnp.zeros((SB_BLK, 1), F32))
        carry = block(diag, (zero, zero), True)
        carry = lax.fori_loop(0, diag, lambda s, c: block(diag - 1 - s, c, False), carry)
        o_ref[...] = jnp.where(head0, carry[0][0], carry[1][0])
        r_ref[...] = jnp.where(head0, carry[0][1], carry[1][1])

    return pl.pallas_call(
        body, name=name, grid=(npair, nb),
        in_specs=[pl.BlockSpec((SB_BLK, LANES), lambda p, i: (i, p)),
                  pl.BlockSpec((t, LANES), lambda p, i: (0, npair + p)),
                  pl.BlockSpec((t, LANES), lambda p, i: (0, 2 * npair + p))],
        out_specs=[pl.BlockSpec((SB_BLK, LANES), lambda p, i: (i, p)),
                   pl.BlockSpec((None, SB_BLK, LANES), lambda p, i: (p, i, 0))],
        out_shape=[jax.ShapeDtypeStruct((t, SB_W), F32), jax.ShapeDtypeStruct((npair, t, LANES), F32)],
        compiler_params=_params("parallel", "arbitrary"),
    )(proj, proj, proj)


def _sb_bwd(proj, rtot, dout, *, name):
    t = proj.shape[0]
    nb = t // SB_BLK
    npair = SB_W // LANES

    def body(q_ref, k_ref, v_ref, r_ref, do_ref, dq_ref, dk_ref, dv_ref):
        i = pl.program_id(1)
        kb, lane, row, ur, uc, head0 = _sb_consts(t)
        u_incl = (ur <= uc).astype(BF16)
        u_excl = (ur < uc).astype(BF16)
        q, do, rt = q_ref[...], do_ref[...], r_ref[...]
        qb, dob = _b(q), _b(do)
        qh = [_b(jnp.where(head0, q, 0.0)), _b(jnp.where(head0, 0.0, q))]
        doh = [_b(jnp.where(head0, do, 0.0)), _b(jnp.where(head0, 0.0, do))]
        rh = [rt[:, 0:1], rt[:, HEAD_DIM:HEAD_DIM + 1]]
        diag = (i * SB_BLK) // kb

        @pl.when(i == 0)
        def _():
            dk_ref[...] = jnp.zeros_like(dk_ref)
            dv_ref[...] = jnp.zeros_like(dv_ref)

        def block(j, carry, masked):
            keys = pl.ds(pl.multiple_of(j * kb, kb), kb)
            kj, vj = _b(k_ref[keys, :]), _b(v_ref[keys, :])
            causal = (j * kb + lane) < (i * SB_BLK + row)
            out, dks, dvs = [], [], []
            for h in range(2):
                dq_acc, clk, ce = carry[h]
                z = _dg(qh[h], kj, 1, 1) * SB_SCALE
                ls = jax.nn.log_sigmoid(z)
                lk = ls - z
                if masked:
                    lk = jnp.where(causal, lk, 0.0)
                pre = _split_dot(lk, u_incl) + clk
                w = jnp.exp(ls + (rh[h] - pre))
                if masked:
                    w = jnp.where(causal, w, 0.0)
                e = _dg(doh[h], vj, 1, 1) * w
                pre_e = _split_dot(e, u_excl) + ce
                sig = jnp.exp(ls)
                dz = (e * (1.0 - sig) - sig * pre_e) * SB_SCALE
                if masked:
                    dz = jnp.where(causal, dz, 0.0)
                dzb = _b(dz)
                dks.append(_dg(dzb, qb, 0, 0))
                dvs.append(_dg(_b(w), dob, 0, 0))
                out.append((dq_acc + _dg(dzb, kj, 1, 0), pre[:, kb - 1:], pre_e[:, kb - 1:] + e[:, kb - 1:]))
            head0k = lax.broadcasted_iota(jnp.int32, (kb, LANES), 1) < HEAD_DIM
            dk_ref[keys, :] += jnp.where(head0k, dks[0], dks[1])
            dv_ref[keys, :] += jnp.where(head0k, dvs[0], dvs[1])
            return tuple(out)

        zero = (jnp.zeros((SB_BLK, LANES), F32), jnp.zeros((SB_BLK, 1), F32), jnp.zeros((SB_BLK, 1), F32))
        carry = lax.fori_loop(0, diag, lambda j, c: block(j, c, False), (zero, zero))
        carry = block(diag, carry, True)
        dq_ref[...] = jnp.where(head0, carry[0][0], carry[1][0])

    blk = pl.BlockSpec((SB_BLK, LANES), lambda p, i: (i, p))
    whole = pl.BlockSpec((t, LANES), lambda p, i: (0, p))
    return pl.pallas_call(
        body, name=name, grid=(npair, nb),
        in_specs=[blk,
                  pl.BlockSpec((t, LANES), lambda p, i: (0, npair + p)),
                  pl.BlockSpec((t, LANES), lambda p, i: (0, 2 * npair + p)),
                  pl.BlockSpec((None, SB_BLK, LANES), lambda p, i: (p, i, 0)),
                  blk],
        out_specs=[blk, whole, whole],
        out_shape=[jax.ShapeDtypeStruct((t, SB_W), F32)] * 3,
        compiler_params=_params("arbitrary", "arbitrary"),
    )(proj, proj, proj, rtot, dout)


SWA_G = SWA_HEADS // SWA_KV_HEADS


def _swa_heads(first, qs, ks, vs, qg, kg, sinks):
    qi = lax.broadcasted_iota(jnp.int32, (WINDOW, 2 * WINDOW), 0)
    kj = lax.broadcasted_iota(jnp.int32, (WINDOW, 2 * WINDOW), 1)
    dist = qi + WINDOW - kj
    valid = (dist >= 0) & (dist < WINDOW) & (jnp.logical_not(first) | (kj >= WINDOW))
    distf = dist.astype(F32)
    outs = []
    for hk in range(SWA_KV_HEADS):
        kn = _rms(ks[hk], kg)
        for g in range(SWA_G):
            h = hk * SWA_G + g
            slope = 2.0 ** (-8.0 * (h + 1) / SWA_HEADS)
            s = _bdot_nt(_rms(qs[h], qg), kn) * (HEAD_DIM ** -0.5)
            s = jnp.where(valid, s - slope * distf, NEG)
            m = lax.stop_gradient(jnp.maximum(jnp.max(s, axis=1, keepdims=True), sinks[h]))
            p = jnp.exp(s - m)
            den = jnp.sum(p, axis=1, keepdims=True) + jnp.exp(sinks[h] - m)
            outs.append(_bdot(p / den, vs[hk]))
    return tuple(outs)


def _swa_split(q, kp, kc, vp, vc, sk):
    qs = [q[:, h * HEAD_DIM:(h + 1) * HEAD_DIM] for h in range(SWA_HEADS)]
    k2, v2 = jnp.concatenate([kp, kc], axis=0), jnp.concatenate([vp, vc], axis=0)
    ks = [k2[:, h * HEAD_DIM:(h + 1) * HEAD_DIM] for h in range(SWA_KV_HEADS)]
    vs = [v2[:, h * HEAD_DIM:(h + 1) * HEAD_DIM] for h in range(SWA_KV_HEADS)]
    sinks = [sk[:, h:h + 1] for h in range(SWA_HEADS)]
    return qs, ks, vs, sinks


def _swa_specs(t):
    qcb = (3 * SB_W) // SWA_QW
    kcb = (3 * SB_W + SWA_QW) // SWA_KVW
    prev = lambda i: jnp.maximum(i - 1, 0)
    return [pl.BlockSpec((WINDOW, SWA_QW), lambda i: (i, qcb)),
            pl.BlockSpec((WINDOW, SWA_KVW), lambda i: (prev(i), kcb)),
            pl.BlockSpec((WINDOW, SWA_KVW), lambda i: (i, kcb)),
            pl.BlockSpec((WINDOW, SWA_KVW), lambda i: (prev(i), kcb + 1)),
            pl.BlockSpec((WINDOW, SWA_KVW), lambda i: (i, kcb + 1)),
            pl.BlockSpec((1, HEAD_DIM), lambda i: (0, 0)),
            pl.BlockSpec((1, HEAD_DIM), lambda i: (0, 0)),
            pl.BlockSpec((1, SWA_HEADS), lambda i: (0, 0))]


def _swa_fwd(proj, qg, kg, sinks, *, name):
    t = proj.shape[0]

    def body(q_ref, kp_ref, kc_ref, vp_ref, vc_ref, qg_ref, kg_ref, sk_ref, o_ref):
        first = pl.program_id(0) == 0
        qs, ks, vs, sk = _swa_split(q_ref[...], kp_ref[...], kc_ref[...], vp_ref[...], vc_ref[...], sk_ref[...])
        o_ref[...] = jnp.concatenate(_swa_heads(first, qs, ks, vs, qg_ref[...], kg_ref[...], sk), axis=1)

    return pl.pallas_call(
        body, name=name, grid=(t // WINDOW,), in_specs=_swa_specs(t),
        out_specs=pl.BlockSpec((WINDOW, SWA_QW), lambda i: (i, 0)),
        out_shape=jax.ShapeDtypeStruct((t, SWA_QW), F32),
        compiler_params=_params("parallel"),
    )(proj, proj, proj, proj, proj, qg, kg, sinks)


def _swa_bwd(proj, qg, kg, sinks, dout, *, name):
    t = proj.shape[0]

    def body(q_ref, kp_ref, kc_ref, vp_ref, vc_ref, qg_ref, kg_ref, sk_ref, do_ref,
             dq_ref, dk_ref, dv_ref, dqg_ref, dkg_ref, dsk_ref):
        i = pl.program_id(0)
        first = i == 0

        @pl.when(first)
        def _():
            for r in (dk_ref, dv_ref, dqg_ref, dkg_ref, dsk_ref):
                r[...] = jnp.zeros_like(r)

        qs, ks, vs, sk = _swa_split(q_ref[...], kp_ref[...], kc_ref[...], vp_ref[...], vc_ref[...], sk_ref[...])
        do = do_ref[...]
        cts = tuple(do[:, h * HEAD_DIM:(h + 1) * HEAD_DIM] for h in range(SWA_HEADS))
        _, vjp = jax.vjp(functools.partial(_swa_heads, first), qs, ks, vs, qg_ref[...], kg_ref[...], sk)
        dqs, dks, dvs, dqg, dkg, dsk = vjp(cts)
        dq_ref[...] = jnp.concatenate(dqs, axis=1)
        dk2, dv2 = jnp.concatenate(dks, axis=1), jnp.concatenate(dvs, axis=1)
        cur = pl.ds(pl.multiple_of(i * WINDOW, WINDOW), WINDOW)
        prv = pl.ds(pl.multiple_of(jnp.maximum(i - 1, 0) * WINDOW, WINDOW), WINDOW)
        dk_ref[prv, :] += dk2[:WINDOW]
        dv_ref[prv, :] += dv2[:WINDOW]
        dk_ref[cur, :] += dk2[WINDOW:]
        dv_ref[cur, :] += dv2[WINDOW:]
        dqg_ref[...] += dqg
        dkg_ref[...] += dkg
        dsk_ref[...] += jnp.concatenate(dsk, axis=1)

    whole = lambda shape: pl.BlockSpec(shape, lambda i: (0, 0))
    return pl.pallas_call(
        body, name=name, grid=(t // WINDOW,),
        in_specs=_swa_specs(t) + [pl.BlockSpec((WINDOW, SWA_QW), lambda i: (i, 0))],
        out_specs=[pl.BlockSpec((WINDOW, SWA_QW), lambda i: (i, 0)), whole((t, SWA_KVW)), whole((t, SWA_KVW)),
                   whole((1, HEAD_DIM)), whole((1, HEAD_DIM)), whole((1, SWA_HEADS))],
        out_shape=[jax.ShapeDtypeStruct((t, SWA_QW), F32), jax.ShapeDtypeStruct((t, SWA_KVW), F32),
                   jax.ShapeDtypeStruct((t, SWA_KVW), F32), jax.ShapeDtypeStruct((1, HEAD_DIM), F32),
                   jax.ShapeDtypeStruct((1, HEAD_DIM), F32), jax.ShapeDtypeStruct((1, SWA_HEADS), F32)],
        compiler_params=_params("arbitrary"),
    )(proj, proj, proj, proj, proj, qg, kg, sinks, dout)


CONV_CB = 512
CONV_TM = 512
HALO = 8


def _conv_pre(x_ref, h_ref, w_ref, i):
    halo = jnp.where(i > 0, h_ref[...], 0.0)
    xe = jnp.concatenate([halo, x_ref[...]], axis=0)
    tm = x_ref.shape[0]
    w = w_ref[...]
    c = sum(w[k:k + 1, :] * xe[HALO - (GDN_CONV - 1) + k:HALO - (GDN_CONV - 1) + k + tm] for k in range(GDN_CONV))
    return c, xe


def _conv_specs(tm, cb):
    return [pl.BlockSpec((tm, cb), lambda c, i: (i, c)),
            pl.BlockSpec((HALO, cb), lambda c, i: (jnp.maximum(i * (tm // HALO) - 1, 0), c)),
            pl.BlockSpec((GDN_CONV, cb), lambda c, i: (0, c))]


def _conv_fwd(x, w, dact=None, *, name):
    t, ch = x.shape
    tm, cb = _tile(t, CONV_TM), _tile(ch, CONV_CB)

    def body(*refs):
        x_ref, h_ref, w_ref = refs[:3]
        c, _ = _conv_pre(x_ref, h_ref, w_ref, pl.program_id(1))
        sig = jax.nn.sigmoid(c)
        if dact is None:
            refs[3][...] = c * sig
        else:
            refs[4][...] = refs[3][...] * (sig * (1.0 + c * (1.0 - sig)))

    tile = pl.BlockSpec((tm, cb), lambda c, i: (i, c))
    extra = () if dact is None else (dact,)
    return pl.pallas_call(
        body, name=name, grid=(ch // cb, t // tm),
        in_specs=_conv_specs(tm, cb) + [tile] * len(extra), out_specs=tile,
        out_shape=jax.ShapeDtypeStruct((t, ch), F32),
        compiler_params=_params("parallel", "parallel"),
    )(x, x, w, *extra)


def _conv_bwd(x, w, dc, *, name):
    t, ch = x.shape
    tm, cb = _tile(t, CONV_TM), _tile(ch, CONV_CB)
    nt = t // tm

    def body(x_ref, h_ref, w_ref, dc_ref, nh_ref, dx_ref, dw_ref):
        i = pl.program_id(1)

        @pl.when(i == 0)
        def _():
            dw_ref[...] = jnp.zeros_like(dw_ref)

        halo = jnp.where(i > 0, h_ref[...], 0.0)
        xe = jnp.concatenate([halo, x_ref[...]], axis=0)
        dc = dc_ref[...]
        dce = jnp.concatenate([dc, jnp.where(i < nt - 1, nh_ref[...], 0.0)], axis=0)
        w = w_ref[...]
        last = GDN_CONV - 1
        dx_ref[...] = sum(w[k:k + 1, :] * dce[last - k:last - k + tm] for k in range(GDN_CONV))
        dw_ref[...] += jnp.concatenate(
            [jnp.sum(dc * xe[HALO - last + k:HALO - last + k + tm], axis=0, keepdims=True) for k in range(GDN_CONV)],
            axis=0)

    tile = pl.BlockSpec((tm, cb), lambda c, i: (i, c))
    nxt = pl.BlockSpec((HALO, cb), lambda c, i: (jnp.minimum((i + 1) * (tm // HALO), t // HALO - 1), c))
    return pl.pallas_call(
        body, name=name, grid=(ch // cb, nt),
        in_specs=_conv_specs(tm, cb) + [tile, nxt],
        out_specs=[tile, pl.BlockSpec((GDN_CONV, cb), lambda c, i: (0, c))],
        out_shape=[jax.ShapeDtypeStruct((t, ch), F32), jax.ShapeDtypeStruct((GDN_CONV, ch), F32)],
        compiler_params=_params("parallel", "arbitrary"),
    )(x, x, w, dc, dc)


def _gdn_chunk(qraw, kraw, v, bl, a, alog, dtb, state):
    c, d = GDN_CHUNK, GDN_HEAD_DIM
    ri = lax.broadcasted_iota(jnp.int32, (c, c), 0)
    ci = lax.broadcasted_iota(jnp.int32, (c, c), 1)
    incl, strict = ri >= ci, ri > ci
    q = qraw * lax.rsqrt(jnp.sum(qraw * qraw, axis=-1, keepdims=True) + EPS) * (d ** -0.5)
    k = kraw * lax.rsqrt(jnp.sum(kraw * kraw, axis=-1, keepdims=True) + EPS)
    beta = jax.nn.sigmoid(bl)
    g = -jnp.exp(alog) * jax.nn.softplus(a + dtb)
    gc = _ldot(incl.astype(F32), jnp.broadcast_to(g, (c, d)))
    gcm = gc[:, :c]
    decay = jnp.exp(jnp.where(incl, gcm - gcm.T, NEG))
    eg = jnp.exp(gc)
    kbeta = k * beta
    x = -jnp.where(strict, _bdot_nt(kbeta, k) * decay, 0.0)
    tinv = (ri == ci).astype(F32) + x
    pw = x
    for _ in range(int(math.log2(c)) - 1):
        pw = _hdot(pw, pw)
        tinv = tinv + _hdot(tinv, pw)
    u = _hdot(tinv, v * beta)
    w = _hdot(tinv, kbeta * eg)
    attn = jnp.where(incl, _bdot_nt(q, k) * decay, 0.0)
    glast = gc[c - 1:c, :]
    v_new = u - _bdot(w, state)
    o = _bdot(q * eg, state) + _bdot(attn, v_new)
    state = state * jnp.exp(glast) + _bdot_tn(k * jnp.exp(glast - gc), v_new)
    return o, state


GDN_REP = GDN_V_HEADS // GDN_K_HEADS
GDN_HB = 4


def _gdn_pick(vals, kh, r):
    ba, alog, dtb = vals
    lane = lax.broadcasted_iota(jnp.int32, ba.shape, 1)
    hv = kh * GDN_REP + r
    bl = jnp.sum(jnp.where(lane == hv, ba, 0.0), axis=1, keepdims=True)
    a = jnp.sum(jnp.where(lane == GDN_V_HEADS + hv, ba, 0.0), axis=1, keepdims=True)
    lane1 = lax.broadcasted_iota(jnp.int32, alog.shape, 1)
    al = jnp.sum(jnp.where(lane1 == hv, alog, 0.0), axis=1, keepdims=True)
    db = jnp.sum(jnp.where(lane1 == hv, dtb, 0.0), axis=1, keepdims=True)
    return bl, a, al, db


def _gdn_specs(nchunk, rev):
    c, d = GDN_CHUNK, GDN_HEAD_DIM
    at = (lambda n: nchunk - 1 - n) if rev else (lambda n: n)
    ng = GDN_K_HEADS // GDN_HB
    return at, [pl.BlockSpec((c, GDN_HB * d), lambda n, j: (at(n), j)),
                pl.BlockSpec((c, GDN_HB * d), lambda n, j: (at(n), ng + j)),
                pl.BlockSpec((c, GDN_HB * GDN_REP * d), lambda n, j: (at(n), ng + j)),
                pl.BlockSpec((c, 2 * GDN_V_HEADS), lambda n, j: (at(n), 0)),
                pl.BlockSpec((1, GDN_V_HEADS), lambda n, j: (0, 0)),
                pl.BlockSpec((1, GDN_V_HEADS), lambda n, j: (0, 0))]


def _gdn_fwd(act, ba, alog, dtb, *, name):
    t = act.shape[0]
    c, d = GDN_CHUNK, GDN_HEAD_DIM
    nchunk = t // c
    at, specs = _gdn_specs(nchunk, False)

    def body(q_ref, k_ref, v_ref, ba_ref, al_ref, db_ref, o_ref, s_ref, state):
        n, j = pl.program_id(0), pl.program_id(1)
        heads = pl.ds(j * GDN_HB, GDN_HB)

        @pl.when(n == 0)
        def _():
            state[heads] = jnp.zeros((GDN_HB, GDN_REP, d, d), F32)

        s_ref[...] = state[heads]
        small = (ba_ref[...], al_ref[...], db_ref[...])
        for hh in range(GDN_HB):
            kh = j * GDN_HB + hh
            q, k = q_ref[:, hh * d:(hh + 1) * d], k_ref[:, hh * d:(hh + 1) * d]
            for r in range(GDN_REP):
                col = (hh * GDN_REP + r) * d
                bl, a, al, db = _gdn_pick(small, kh, r)
                o, s_new = _gdn_chunk(q, k, v_ref[:, col:col + d], bl, a, al, db, state[kh, r])
                state[kh, r] = s_new
                o_ref[:, col:col + d] = o

    return pl.pallas_call(
        body, name=name, grid=(nchunk, GDN_K_HEADS // GDN_HB), in_specs=specs,
        out_specs=[pl.BlockSpec((c, GDN_HB * GDN_REP * d), lambda n, j: (n, j)),
                   pl.BlockSpec((None, GDN_HB, GDN_REP, d, d), lambda n, j: (n, j, 0, 0, 0))],
        out_shape=[jax.ShapeDtypeStruct((t, GDN_VW), F32),
                   jax.ShapeDtypeStruct((nchunk, GDN_K_HEADS, GDN_REP, d, d), F32)],
        scratch_shapes=[pltpu.VMEM((GDN_K_HEADS, GDN_REP, d, d), F32)],
        compiler_params=_params("arbitrary", "arbitrary"),
    )(act, act, act, ba, alog, dtb)


def _gdn_bwd(act, ba, alog, dtb, states, dout, *, name):
    t = act.shape[0]
    c, d = GDN_CHUNK, GDN_HEAD_DIM
    nchunk = t // c
    at, specs = _gdn_specs(nchunk, True)

    def body(q_ref, k_ref, v_ref, ba_ref, al_ref, db_ref, s_ref, do_ref,
             dq_ref, dk_ref, dv_ref, dba_ref, dal_ref, ddb_ref, dstate):
        n, j = pl.program_id(0), pl.program_id(1)

        @pl.when(n == 0)
        def _():
            dstate[pl.ds(j * GDN_HB, GDN_HB)] = jnp.zeros((GDN_HB, GDN_REP, d, d), F32)

        @pl.when((n == 0) & (j == 0))
        def _():
            dal_ref[...] = jnp.zeros_like(dal_ref)
            ddb_ref[...] = jnp.zeros_like(ddb_ref)

        @pl.when(j == 0)
        def _():
            dba_ref[...] = jnp.zeros_like(dba_ref)

        small = (ba_ref[...], al_ref[...], db_ref[...])
        lane = lax.broadcasted_iota(jnp.int32, (c, 2 * GDN_V_HEADS), 1)
        lane1 = lax.broadcasted_iota(jnp.int32, (1, GDN_V_HEADS), 1)
        dba = jnp.zeros((c, 2 * GDN_V_HEADS), F32)
        dal = jnp.zeros((1, GDN_V_HEADS), F32)
        ddb = jnp.zeros((1, GDN_V_HEADS), F32)
        for hh in range(GDN_HB):
            kh = j * GDN_HB + hh
            q, k = q_ref[:, hh * d:(hh + 1) * d], k_ref[:, hh * d:(hh + 1) * d]
            dq = jnp.zeros((c, d), F32)
            dk = jnp.zeros((c, d), F32)
            for r in range(GDN_REP):
                hv = kh * GDN_REP + r
                col = (hh * GDN_REP + r) * d
                bl, a, al, db = _gdn_pick(small, kh, r)
                _, vjp = jax.vjp(_gdn_chunk, q, k, v_ref[:, col:col + d], bl, a, al, db, s_ref[hh, r])
                gq, gk, gv, gbl, ga, gal, gdb, gs = vjp((do_ref[:, col:col + d], dstate[kh, r]))
                dstate[kh, r] = gs
                dq, dk = dq + gq, dk + gk
                dv_ref[:, col:col + d] = gv
                dba = dba + jnp.where(lane == hv, gbl, 0.0) + jnp.where(lane == GDN_V_HEADS + hv, ga, 0.0)
                dal = dal + jnp.where(lane1 == hv, gal, 0.0)
                ddb = ddb + jnp.where(lane1 == hv, gdb, 0.0)
            dq_ref[:, hh * d:(hh + 1) * d] = dq
            dk_ref[:, hh * d:(hh + 1) * d] = dk
        dba_ref[...] += dba
        dal_ref[...] += dal
        ddb_ref[...] += ddb

    small = pl.BlockSpec((1, GDN_V_HEADS), lambda n, j: (0, 0))
    return pl.pallas_call(
        body, name=name, grid=(nchunk, GDN_K_HEADS // GDN_HB),
        in_specs=specs + [pl.BlockSpec((None, GDN_HB, GDN_REP, d, d), lambda n, j: (at(n), j, 0, 0, 0)),
                          pl.BlockSpec((c, GDN_HB * GDN_REP * d), lambda n, j: (at(n), j))],
        out_specs=[pl.BlockSpec((c, GDN_HB * d), lambda n, j: (at(n), j)),
                   pl.BlockSpec((c, GDN_HB * d), lambda n, j: (at(n), j)),
                   pl.BlockSpec((c, GDN_HB * GDN_REP * d), lambda n, j: (at(n), j)),
                   pl.BlockSpec((c, 2 * GDN_V_HEADS), lambda n, j: (at(n), 0)),
                   small, small],
        out_shape=[jax.ShapeDtypeStruct((t, GDN_KW), F32), jax.ShapeDtypeStruct((t, GDN_KW), F32),
                   jax.ShapeDtypeStruct((t, GDN_VW), F32), jax.ShapeDtypeStruct((t, 2 * GDN_V_HEADS), F32),
                   jax.ShapeDtypeStruct((1, GDN_V_HEADS), F32), jax.ShapeDtypeStruct((1, GDN_V_HEADS), F32)],
        scratch_shapes=[pltpu.VMEM((GDN_K_HEADS, GDN_REP, d, d), F32)],
        compiler_params=_params("arbitrary", "arbitrary"),
    )(act, act, act, ba, alog, dtb, states, dout)


N_DEV = 8
ANY = pl.BlockSpec(memory_space=pl.ANY)


def _coords():
    return lax.axis_index("x"), lax.axis_index("y"), lax.axis_index("c")


def _other_chips(x, y):
    return [(1 - x, y), (x, 1 - y), (1 - x, 1 - y)]


def _remote(src, dst, send_sems, recv_sems, k, to):
    return pltpu.make_async_remote_copy(src_ref=src, dst_ref=dst, send_sem=send_sems.at[k], recv_sem=recv_sems.at[k],
                                        device_id=to, device_id_type=MESH)


def _dma_sems(n):
    return [pltpu.SemaphoreType.DMA((n,)), pltpu.SemaphoreType.DMA((n,))]


def _gather_quarters(parts, *, name):
    na = len(parts)

    def body(*refs):
        ins, outs = refs[:na], refs[na:2 * na]
        send_sems, recv_sems, local_sems = refs[2 * na:]
        x, y, c = _coords()
        sibling = (x, y, 1 - c)
        chips = _other_chips(x, y)
        mine, first, passed = [], [], []
        for a, (x_ref, out_ref) in enumerate(zip(ins, outs)):
            mine.append(pltpu.make_async_copy(x_ref, out_ref.at[2 * x + y], local_sems.at[a]))
            for j, (cx, cy) in enumerate(chips):
                first.append(_remote(x_ref.at[c], out_ref.at[2 * x + y, c], send_sems, recv_sems, 6 * a + j, (cx, cy, c)))
                passed.append(_remote(out_ref.at[2 * cx + cy, c], out_ref.at[2 * cx + cy, c], send_sems, recv_sems,
                                      6 * a + 3 + j, sibling))
        for cp in mine + first:
            cp.start()
        for a, (x_ref, out_ref) in enumerate(zip(ins, outs)):
            for j, (cx, cy) in enumerate(chips):
                _remote(x_ref.at[c], out_ref.at[2 * cx + cy, c], send_sems, recv_sems, 6 * a + j, (cx, cy, c)).wait_recv()
                passed[3 * a + j].start()
        for a, (x_ref, out_ref) in enumerate(zip(ins, outs)):
            for j, (cx, cy) in enumerate(chips):
                _remote(x_ref.at[c], out_ref.at[2 * cx + cy, 1 - c], send_sems, recv_sems, 6 * a + 3 + j,
                        sibling).wait_recv()
        for cp in first + passed:
            cp.wait_send()
        for cp in mine:
            cp.wait()

    return pl.pallas_call(
        body, name=name, in_specs=[ANY] * na, out_specs=[ANY] * na,
        out_shape=[jax.ShapeDtypeStruct((N_CHIPS,) + p.shape, p.dtype) for p in parts],
        scratch_shapes=_dma_sems(6 * na) + [pltpu.SemaphoreType.DMA((na,))],
    )(*parts)


def _swap_halves(grads, *, name):
    na = len(grads)

    def body(*refs):
        ins, outs = refs[:na], refs[na:2 * na]
        send_sems, recv_sems = refs[2 * na:]
        x, y, c = _coords()
        sends = [_remote(g_ref.at[j, 1 - c], o_ref.at[j], send_sems, recv_sems, N_CHIPS * a + j, (x, y, 1 - c))
                 for a, (g_ref, o_ref) in enumerate(zip(ins, outs)) for j in range(N_CHIPS)]
        for cp in sends:
            cp.start()
        for cp in sends:
            cp.wait()

    return pl.pallas_call(
        body, name=name, in_specs=[ANY] * na, out_specs=[ANY] * na,
        out_shape=[jax.ShapeDtypeStruct((N_CHIPS,) + g.shape[2:], g.dtype) for g in grads],
        scratch_shapes=_dma_sems(N_CHIPS * na),
    )(*grads)


def _scatter_quarters(pairs, *, name):
    na = len(pairs)

    def body(*refs):
        ins, outs = refs[:na], refs[na:4 * na]
        send_sems, recv_sems = refs[4 * na:]
        x, y, c = _coords()
        sends = [_remote(p_ref.at[2 * cx + cy], outs[3 * a + j], send_sems, recv_sems, 3 * a + j, (cx, cy, c))
                 for a, p_ref in enumerate(ins) for j, (cx, cy) in enumerate(_other_chips(x, y))]
        for cp in sends:
            cp.start()
        for cp in sends:
            cp.wait()

    out = pl.pallas_call(
        body, name=name, in_specs=[ANY] * na, out_specs=[ANY] * (3 * na),
        out_shape=[jax.ShapeDtypeStruct(p.shape[1:], p.dtype) for p in pairs for _ in range(3)],
        scratch_shapes=_dma_sems(3 * na),
    )(*pairs)
    return [out[3 * a:3 * a + 3] for a in range(na)]


def _share_halves(tots, *, name):
    na = len(tots)

    def body(*refs):
        ins, outs = refs[:na], refs[na:2 * na]
        send_sems, recv_sems = refs[2 * na:]
        x, y, c = _coords()
        sends = [_remote(t_ref, o_ref, send_sems, recv_sems, a, (x, y, 1 - c))
                 for a, (t_ref, o_ref) in enumerate(zip(ins, outs))]
        for cp in sends:
            cp.start()
        for cp in sends:
            cp.wait()

    return pl.pallas_call(
        body, name=name, in_specs=[ANY] * na, out_specs=[ANY] * na,
        out_shape=[jax.ShapeDtypeStruct(t.shape, t.dtype) for t in tots],
        scratch_shapes=_dma_sems(na),
    )(*tots)


def _gather_all(vec, *, name):
    m, w = vec.shape

    def body(x_ref, out_ref, send_sems, recv_sems, local_sem):
        x, y, c = _coords()
        me, sibling = (x, y, c), (x, y, 1 - c)
        chips = _other_chips(x, y)

        def rows(px, py, pc):
            return out_ref.at[pl.ds((4 * px + 2 * py + pc) * m, m), :]

        def copy(k, block, to, src=None):
            return _remote(rows(*block) if src is None else src, rows(*block), send_sems, recv_sems, k, to)

        mine = pltpu.make_async_copy(x_ref, rows(*me), local_sem)
        mine.start()
        first = [copy(0, me, sibling, src=x_ref)]
        first += [copy(1 + j, me, (*chip, c), src=x_ref) for j, chip in enumerate(chips)]
        for cp in first:
            cp.start()
        passed = [copy(4 + j, (*chip, c), sibling) for j, chip in enumerate(chips)]
        for j, chip in enumerate(chips):
            copy(1 + j, (*chip, c), me).wait_recv()
            passed[j].start()
        copy(0, sibling, me).wait_recv()
        for j, chip in enumerate(chips):
            copy(4 + j, (*chip, 1 - c), me).wait_recv()
        for cp in first + passed:
            cp.wait_send()
        mine.wait()

    vm = pl.BlockSpec(memory_space=pltpu.VMEM)
    return pl.pallas_call(
        body, name=name, in_specs=[vm], out_specs=vm, out_shape=jax.ShapeDtypeStruct((N_DEV * m, w), vec.dtype),
        scratch_shapes=_dma_sems(7) + [pltpu.SemaphoreType.DMA(())],
    )(vec)


def _sum_blocks(allv, n, *, name):
    m = allv.shape[0] // n

    def body(a_ref, o_ref):
        acc = a_ref[0:m, :]
        for d in range(1, n):
            acc = acc + a_ref[d * m:(d + 1) * m, :]
        o_ref[...] = acc

    return pl.pallas_call(body, name=name, out_shape=jax.ShapeDtypeStruct((m, allv.shape[1]), allv.dtype))(allv)


EW_BLOCK_BYTES = 1 << 20


def _ew_rows(rows, w):
    return _tile(rows, max(8, (EW_BLOCK_BYTES // (4 * w)) // 8 * 8), 8)


def _add_pair(g, got, c, *, name):
    _, _, rows, w = g.shape
    tr = _ew_rows(rows, w)

    def body(c_ref, g_ref, got_ref, o_ref):
        o_ref[...] = g_ref[...] + got_ref[...]

    blk = pl.BlockSpec((None, tr, w), lambda q, i, c_ref: (q, i, 0))
    return pl.pallas_call(
        body, name=name,
        grid_spec=pltpu.PrefetchScalarGridSpec(
            num_scalar_prefetch=1, grid=(N_CHIPS, rows // tr),
            in_specs=[pl.BlockSpec((None, None, tr, w), lambda q, i, c_ref: (q, c_ref[0], i, 0)), blk], out_specs=blk),
        out_shape=jax.ShapeDtypeStruct(got.shape, got.dtype),
        compiler_params=_params("parallel", "parallel"),
    )(c, g, got)


def _add_chips(pair, recv, chip, *, name):
    _, rows, w = pair.shape
    tr = _ew_rows(rows, w)

    def body(chip_ref, p_ref, r0_ref, r1_ref, r2_ref, o_ref):
        o_ref[...] = ((p_ref[...] + r0_ref[...]) + r1_ref[...]) + r2_ref[...]

    blk = pl.BlockSpec((tr, w), lambda i, chip_ref: (i, 0))
    return pl.pallas_call(
        body, name=name,
        grid_spec=pltpu.PrefetchScalarGridSpec(
            num_scalar_prefetch=1, grid=(rows // tr,),
            in_specs=[pl.BlockSpec((None, tr, w), lambda i, chip_ref: (chip_ref[0], i, 0)), blk, blk, blk], out_specs=blk),
        out_shape=jax.ShapeDtypeStruct((rows, w), pair.dtype),
        compiler_params=_params("parallel"),
    )(chip, pair, *recv)


def _adamw_math(w, g, m, v):
    nm = ADAM_B1 * m + (1.0 - ADAM_B1) * g
    nv = ADAM_B2 * v + (1.0 - ADAM_B2) * (g * g)
    m_hat = nm / (1.0 - ADAM_B1 ** ADAM_STEP)
    v_hat = nv / (1.0 - ADAM_B2 ** ADAM_STEP)
    return -ADAM_LR * (m_hat / (jnp.sqrt(v_hat) + ADAM_EPS) + ADAM_WD * w), nm, nv


def _adamw(w, g, m, v, *, name):
    shape = w.shape
    last = shape[-1]
    w2, g2, m2, v2 = (a.reshape(-1, last) for a in (w, g, m, v))
    rows = w2.shape[0]
    tm = _ew_rows(rows, last)

    def body(w_ref, g_ref, m_ref, v_ref, d_ref, nm_ref, nv_ref):
        d_ref[...], nm_ref[...], nv_ref[...] = _adamw_math(w_ref[...], g_ref[...], m_ref[...], v_ref[...])

    spec = pl.BlockSpec((tm, last), lambda i: (i, 0))
    out = jax.ShapeDtypeStruct((rows, last), F32)
    d, nm, nv = pl.pallas_call(
        body, name=name, grid=(rows // tm,), in_specs=[spec] * 4, out_specs=[spec] * 3, out_shape=[out] * 3,
        compiler_params=_params("parallel"),
    )(w2, g2, m2, v2)
    return d.reshape(shape), nm.reshape(shape), nv.reshape(shape)


def _adamw_halves(w, m, v, mine, theirs, c, *, name):
    _, rows, wd = w.shape
    tr = _ew_rows(rows, wd)

    def body(c_ref, w_ref, m_ref, v_ref, a_ref, b_ref, g_ref, d_ref, nm_ref, nv_ref):
        g = jnp.where(pl.program_id(0) == c_ref[0], a_ref[...], b_ref[...])
        g_ref[...] = g
        d_ref[...], nm_ref[...], nv_ref[...] = _adamw_math(w_ref[...], g, m_ref[...], v_ref[...])

    full = pl.BlockSpec((None, tr, wd), lambda hf, i, c_ref: (hf, i, 0))
    half = pl.BlockSpec((tr, wd), lambda hf, i, c_ref: (i, 0))
    out = jax.ShapeDtypeStruct(w.shape, F32)
    return pl.pallas_call(
        body, name=name,
        grid_spec=pltpu.PrefetchScalarGridSpec(num_scalar_prefetch=1, grid=(2, rows // tr),
                                               in_specs=[full] * 3 + [half] * 2, out_specs=[full] * 4),
        out_shape=[out] * 4,
        compiler_params=_params("parallel", "parallel"),
    )(c, w, m, v, mine, theirs)


def _join_quarters(q, *, name):
    _, rows, n = q.shape
    tr = _tile(rows, 256, 16)

    def body(q_ref, o_ref):
        o_ref[...] = jnp.concatenate([q_ref[s] for s in range(N_CHIPS)], axis=1)

    return pl.pallas_call(
        body, name=name, grid=(rows // tr,),
        in_specs=[pl.BlockSpec((N_CHIPS, tr, n), lambda i: (0, i, 0))],
        out_specs=pl.BlockSpec((tr, N_CHIPS * n), lambda i: (i, 0)),
        out_shape=jax.ShapeDtypeStruct((rows, N_CHIPS * n), q.dtype),
        compiler_params=_params("parallel"),
    )(q)


def _split_quarters(full, *, name):
    rows, n4 = full.shape
    n = n4 // N_CHIPS
    tr = _tile(rows, 256, 16)

    def body(x_ref, o_ref):
        x = x_ref[...]
        for s in range(N_CHIPS):
            o_ref[s] = x[:, s * n:(s + 1) * n]

    return pl.pallas_call(
        body, name=name, grid=(rows // tr,),
        in_specs=[pl.BlockSpec((tr, n4), lambda i: (i, 0))],
        out_specs=pl.BlockSpec((N_CHIPS, tr, n), lambda i: (0, i, 0)),
        out_shape=jax.ShapeDtypeStruct((N_CHIPS, rows, n), full.dtype),
        compiler_params=_params("parallel"),
    )(full)


_WEIGHTS = ['ffn_norm', 'ffn_w_gate', 'ffn_w_up', 'ffn_w_down', 'mix_norm', 'att_w_in', 'att_q_norm', 'att_k_norm',
            'att_sinks', 'att_w_out', 'gdn_w_in', 'gdn_conv_w', 'gdn_a_log', 'gdn_dt_bias', 'gdn_out_norm', 'gdn_w_out',
            'ple_norm', 'ple_w_gate', 'ple_w_proj']
_BIG = ['ffn_w_gate', 'ffn_w_up', 'ffn_w_down', 'att_w_in', 'att_w_out', 'gdn_w_in', 'gdn_w_out', 'ple_w_gate',
        'ple_w_proj']
_SMALL_CUT = {'ffn_norm': 2, 'gdn_conv_w': 2}
_WHOLE = ['mix_norm', 'att_q_norm', 'att_k_norm', 'att_sinks', 'gdn_a_log', 'gdn_dt_bias', 'gdn_out_norm', 'ple_norm']
PACK_W = 1024
SMALL_ROW_MULT = 8


def _halves(a):
    return a.reshape(2, -1, a.shape[-1])


def _from_quarters(blk, axis):
    full = jnp.moveaxis(blk, 0, axis)
    shp = list(full.shape)
    shp[axis:axis + 2] = [shp[axis] * shp[axis + 1]]
    return full.reshape(shp)


def _to_quarters(full, axis):
    shp = list(full.shape)
    shp[axis:axis + 1] = [N_CHIPS, shp[axis] // N_CHIPS]
    return jnp.moveaxis(full.reshape(shp), axis, 0)


def _pack(parts, row_mult):
    flat = jnp.concatenate(parts, axis=-1)
    n = flat.shape[-1]
    rows = -(-n // (PACK_W * row_mult)) * row_mult
    return jnp.pad(flat, [(0, rows * PACK_W - n)]).reshape(rows, PACK_W)


def _unpack(flat, shapes):
    lead = flat.shape[:-2]
    flat = flat.reshape(lead + (-1,))
    out, off = [], 0
    for shp in shapes:
        n = math.prod(shp)
        out.append(flat[..., off:off + n].reshape(lead + tuple(shp)))
        off += n
    return out


def _ffn_fwd(h, gain, wg, wu, wd, at, tag):
    t = h.shape[0]
    fq = wd.shape[-2]
    lead = (Q,) + at
    hn, = _row_fwd(_f_rms, [h], [gain], [(D_MODEL, BF16)], name=f"{tag}_norm")
    g = _mm(hn, (wg, lead), out_q=True, name=f"{tag}_gate").reshape(N_CHIPS * t, fq)
    u = _mm(hn, (wu, lead), out_q=True, name=f"{tag}_up").reshape(N_CHIPS * t, fq)
    a, = _row_fwd(_f_swiglu, [g, u], [], [(fq, BF16)], name=f"{tag}_act")
    a = a.reshape(N_CHIPS, t, fq)
    out = _mm((a, (Q,)), (wd, lead), res=h, scale=0.5, name=f"{tag}_down")
    return out, (h, hn, g, u, a)


def _ffn_bwd(dout, saved, gain, wg, wu, wd, at, grads, tag):
    h, hn, g, u, a = saved
    t = h.shape[0]
    fq = wd.shape[-2]
    lead = (Q,) + at
    da = _mm(dout, (wd, lead), tb=True, scale=0.5, out_q=True, name=f"{tag}_d_act").reshape(N_CHIPS * t, fq)
    dg, du = _row_bwd(_f_swiglu, [g, u], [], [da], [(0, BF16), (1, BF16)], [], name=f"{tag}_d_gate_up")
    dg, du = dg.reshape(N_CHIPS, t, fq), du.reshape(N_CHIPS, t, fq)
    g_gate, g_up, g_down = grads
    g_down = _mm((a, (Q,)), dout, ta=True, scale=0.5, into=(g_down, lead), name=f"{tag}_dw_down")
    g_gate = _mm(hn, (dg, (Q,)), ta=True, into=(g_gate, lead), name=f"{tag}_dw_gate")
    g_up = _mm(hn, (du, (Q,)), ta=True, into=(g_up, lead), name=f"{tag}_dw_up")
    dhn = _mm((dg, (Q,)), (wg, lead), tb=True, name=f"{tag}_d_norm_gate")
    dhn = _mm((du, (Q,)), (wu, lead), tb=True, res=dhn, name=f"{tag}_d_norm_up")
    dh, dgain = _row_bwd(_f_rms_res, [h], [gain], [dhn, dout], [(0, F32)], [0], name=f"{tag}_d_in")
    return dh, dgain, (g_gate, g_up, g_down)


def _att_fwd(h, gain, w_in, qg, kg, sinks, w_out):
    hn, = _row_fwd(_f_rms, [h], [gain], [(D_MODEL, BF16)], name="att_norm")
    proj = _mm(hn, w_in, name="att_in")
    a, rtot = _sb_fwd(proj, name="att_sb")
    b = _swa_fwd(proj, qg, kg, sinks, name="att_swa")
    out = _mm(a, (w_out, (0,)), res=h, name="att_out_sb")
    out = _mm(b, (w_out, (1,)), res=out, name="att_out_swa")
    return out, (h, hn, proj, a, rtot, b)


def _att_bwd(dout, saved, gain, w_in, qg, kg, sinks, w_out):
    h, hn, proj, a, rtot, b = saved
    da = _mm(dout, (w_out, (0,)), tb=True, name="att_d_sb")
    db = _mm(dout, (w_out, (1,)), tb=True, name="att_d_swa")
    dw_out = jnp.zeros(w_out.shape, F32)
    dw_out = _mm(a, dout, ta=True, into=(dw_out, (0,)), name="att_dw_out_sb")
    dw_out = _mm(b, dout, ta=True, into=(dw_out, (1,)), name="att_dw_out_swa")
    dq, dk, dv = _sb_bwd(proj, rtot, da, name="att_sb_bwd")
    dqb, dkb, dvb, dqg, dkg, dsk = _swa_bwd(proj, qg, kg, sinks, db, name="att_swa_bwd")
    dproj = jnp.concatenate([dq, dk, dv, dqb, dkb, dvb], axis=1)
    dw_in = _mm(hn, dproj, ta=True, name="att_dw_in")
    dhn = _mm(dproj, w_in, tb=True, name="att_d_norm")
    dh, dgain = _row_bwd(_f_rms_res, [h], [gain], [dhn, dout], [(0, F32)], [0], name="att_d_in")
    return dh, dgain, dw_in, dqg, dkg, dsk, dw_out


def _gdn_layer_fwd(h, gain, w_in, conv_w, alog, dtb, out_gain, w_out):
    w_qkv, w_z, w_ba = w_in[:, :GDN_CONV_W], w_in[:, GDN_CONV_W:GDN_CONV_W + GDN_VW], w_in[:, GDN_CONV_W + GDN_VW:]
    hn, = _row_fwd(_f_rms, [h], [gain], [(D_MODEL, BF16)], name="gdn_norm")
    pq = _mm(hn, w_qkv, name="gdn_in_qkv")
    pz = _mm(hn, w_z, name="gdn_in_z")
    ba = _mm(hn, w_ba, name="gdn_in_ba")
    act = _conv_fwd(pq, conv_w, name="gdn_conv")
    o, states = _gdn_fwd(act, ba, alog, dtb, name="gdn_rule")
    y, = _row_fwd(_f_gdn_out, [o, pz], [out_gain], [(GDN_VW, BF16)], name="gdn_gate")
    out = _mm(y, w_out, res=h, name="gdn_out")
    return out, (h, hn, pq, pz, ba, act, o, states, y, (w_qkv, w_z, w_ba))


def _gdn_layer_bwd(dout, saved, gain, conv_w, alog, dtb, out_gain, w_out):
    h, hn, pq, pz, ba, act, o, states, y, (w_qkv, w_z, w_ba) = saved
    dy = _mm(dout, w_out, tb=True, name="gdn_d_gate")
    dw_out = _mm(y, dout, ta=True, name="gdn_dw_out")
    do, dpz, dout_gain = _row_bwd(_f_gdn_out, [o, pz], [out_gain], [dy], [(0, F32), (1, F32)], [0], name="gdn_gate_bwd")
    dq, dk, dv, dba, dal, ddb = _gdn_bwd(act, ba, alog, dtb, states, do, name="gdn_rule_bwd")
    dact = jnp.concatenate([dq, dk, dv], axis=1)
    dc = _conv_fwd(pq, conv_w, dact, name="gdn_conv_d_pre")
    dpq, dconv = _conv_bwd(pq, conv_w, dc, name="gdn_conv_bwd")
    dw_in = jnp.concatenate([_mm(hn, dpq, ta=True, name="gdn_dw_qkv"), _mm(hn, dpz, ta=True, name="gdn_dw_z"),
                             _mm(hn, dba, ta=True, name="gdn_dw_ba")], axis=1)
    dhn = _mm(dpq, w_qkv, tb=True, name="gdn_d_norm_qkv")
    dhn = _mm(dpz, w_z, tb=True, res=dhn, name="gdn_d_norm_z")
    dhn = _mm(dba, w_ba, tb=True, res=dhn, name="gdn_d_norm_ba")
    dh, dgain = _row_bwd(_f_rms_res, [h], [gain], [dhn, dout], [(0, F32)], [0], name="gdn_d_in")
    return dh, dgain, dw_in, dconv, dal, ddb, dout_gain, dw_out


def _ple_fwd(h, gain, w_gate, w_proj, pe, tag):
    hn, = _row_fwd(_f_rms, [h], [gain], [(D_MODEL, BF16)], name=f"{tag}_norm")
    gl = _mm(hn, w_gate, name=f"{tag}_gate")
    pp = _mm(pe, w_proj, name=f"{tag}_proj")
    out, = _row_fwd(_f_ple, [h, gl, pp], [], [(D_MODEL, F32)], name=f"{tag}_mix")
    return out, (h, hn, gl, pp)


def _ple_bwd(dout, saved, gain, w_gate, pe, tag):
    h, hn, gl, pp = saved
    dha, dgl, dpp = _row_bwd(_f_ple, [h, gl, pp], [], [dout], [(0, F32), (1, BF16), (2, BF16)], [], name=f"{tag}_mix_bwd")
    dw_gate = _mm(hn, dgl, ta=True, name=f"{tag}_dw_gate")
    dw_proj = _mm(pe, dpp, ta=True, name=f"{tag}_dw_proj")
    dhn = _mm(dgl, w_gate, tb=True, name=f"{tag}_d_norm")
    dh, dgain = _row_bwd(_f_rms_res, [h], [gain], [dhn, dha], [(0, F32)], [0], name=f"{tag}_d_in")
    return dh, dgain, dw_gate, dw_proj


def kernel(x, p, ffn_norm, ffn_w_gate, ffn_w_up, ffn_w_down, mix_norm, att_w_in, att_q_norm, att_k_norm, att_sinks, att_w_out, gdn_w_in, gdn_conv_w, gdn_a_log, gdn_dt_bias, gdn_out_norm, gdn_w_out, ple_norm, ple_w_gate, ple_w_proj, loss_target, m_ffn_norm, m_ffn_w_gate, m_ffn_w_up, m_ffn_w_down, m_mix_norm, m_att_w_in, m_att_q_norm, m_att_k_norm, m_att_sinks, m_att_w_out, m_gdn_w_in, m_gdn_conv_w, m_gdn_a_log, m_gdn_dt_bias, m_gdn_out_norm, m_gdn_w_out, m_ple_norm, m_ple_w_gate, m_ple_w_proj, v_ffn_norm, v_ffn_w_gate, v_ffn_w_up, v_ffn_w_down, v_mix_norm, v_att_w_in, v_att_q_norm, v_att_k_norm, v_att_sinks, v_att_w_out, v_gdn_w_in, v_gdn_conv_w, v_gdn_a_log, v_gdn_dt_bias, v_gdn_out_norm, v_gdn_w_out, v_ple_norm, v_ple_w_gate, v_ple_w_proj):
    arg = dict(locals())
    cx, cy, cc = _coords()
    chip = (2 * cx + cy).astype(jnp.int32).reshape(1)
    core = cc.astype(jnp.int32).reshape(1)
    n_layers = ffn_norm.shape[0]

    gathered = _gather_quarters([_halves(arg[n].astype(BF16)) for n in _BIG], name="gather_weights")
    wq = {n: g.reshape((N_CHIPS,) + arg[n].shape) for n, g in zip(_BIG, gathered)}
    wt = {}
    wt['att_w_in'] = _join_quarters(wq['att_w_in'][:, 0], name="att_w_in_join")
    wt['gdn_w_in'] = _join_quarters(wq['gdn_w_in'][:, 0], name="gdn_w_in_join")
    wt['att_w_out'] = wq['att_w_out'].reshape(2, SB_W, D_MODEL)
    wt['gdn_w_out'] = wq['gdn_w_out'].reshape(GDN_VW, D_MODEL)
    wt['ple_w_gate'] = _from_quarters(wq['ple_w_gate'], 1)
    wt['ple_w_proj'] = _from_quarters(wq['ple_w_proj'], 2)

    small_names = list(_SMALL_CUT)
    small_shapes = [arg[n].shape for n in small_names]
    svec = _pack([arg[n].reshape(-1) for n in small_names], SMALL_ROW_MULT)
    srows = svec.shape[0]
    sall = _gather_all(svec, name="gather_gains").reshape(N_CHIPS, 2, srows, PACK_W)[:, 0]
    for n, q in zip(small_names, _unpack(sall, small_shapes)):
        wt[n] = _from_quarters(q, _SMALL_CUT[n])
    row = lambda v: v.reshape(1, -1)

    h = x[0]
    tape = []
    ffn_w = (wq['ffn_w_gate'], wq['ffn_w_up'], wq['ffn_w_down'])
    for i in range(n_layers):
        j = i // 2
        h, s0 = _ffn_fwd(h, row(wt['ffn_norm'][i, 0]), *ffn_w, (i, 0), f"ffn{i}a")
        if i % 2 == 0:
            h, sm = _att_fwd(h, row(mix_norm[i]), wt['att_w_in'], att_q_norm[j:j + 1], att_k_norm[j:j + 1],
                             att_sinks[j:j + 1], wt['att_w_out'])
        else:
            h, sm = _gdn_layer_fwd(h, row(mix_norm[i]), wt['gdn_w_in'], wt['gdn_conv_w'][j], gdn_a_log[j:j + 1],
                                   gdn_dt_bias[j:j + 1], gdn_out_norm[j:j + 1], wt['gdn_w_out'])
        h, s1 = _ffn_fwd(h, row(wt['ffn_norm'][i, 1]), *ffn_w, (i, 1), f"ffn{i}b")
        h, sp = _ple_fwd(h, row(ple_norm[i]), wt['ple_w_gate'][i], wt['ple_w_proj'][i], p[i, 0], f"ple{i}")
        tape.append((s0, sm, s1, sp))

    dh, loss_local = _loss_head(h, loss_target[0], name="loss_head")
    loss = lax.psum(loss_local, ("x", "y", "c"))

    gr = {}
    ffn_g = tuple(jnp.zeros(w.shape, F32) for w in ffn_w)
    d_ffn_norm = [[None, None] for _ in range(n_layers)]
    d_mix, d_ple_norm, d_ple_gate, d_ple_proj = [None] * n_layers, [None] * n_layers, [None] * n_layers, [None] * n_layers
    for i in reversed(range(n_layers)):
        j = i // 2
        s0, sm, s1, sp = tape[i]
        dh, d_ple_norm[i], d_ple_gate[i], d_ple_proj[i] = _ple_bwd(dh, sp, row(ple_norm[i]), wt['ple_w_gate'][i], p[i, 0],
                                                                   f"ple{i}")
        dh, d_ffn_norm[i][1], ffn_g = _ffn_bwd(dh, s1, row(wt['ffn_norm'][i, 1]), *ffn_w, (i, 1), ffn_g, f"ffn{i}b")
        if i % 2 == 0:
            (dh, d_mix[i], dw_in, gr['att_q_norm'], gr['att_k_norm'], gr['att_sinks'],
             dw_out) = _att_bwd(dh, sm, row(mix_norm[i]), wt['att_w_in'], att_q_norm[j:j + 1],
                                att_k_norm[j:j + 1], att_sinks[j:j + 1], wt['att_w_out'])
            gr['att_w_in'] = _split_quarters(dw_in, name="att_dw_in_split")
            gr['att_w_out'] = dw_out
        else:
            (dh, d_mix[i], dw_in, dconv, gr['gdn_a_log'], gr['gdn_dt_bias'], gr['gdn_out_norm'],
             dw_out) = _gdn_layer_bwd(dh, sm, row(mix_norm[i]), wt['gdn_conv_w'][j], gdn_a_log[j:j + 1],
                                      gdn_dt_bias[j:j + 1], gdn_out_norm[j:j + 1], wt['gdn_w_out'])
            gr['gdn_w_in'] = _split_quarters(dw_in, name="gdn_dw_in_split")
            gr['gdn_w_out'] = dw_out
            gr['gdn_conv_w'] = dconv[None]
        dh, d_ffn_norm[i][0], ffn_g = _ffn_bwd(dh, s0, row(wt['ffn_norm'][i, 0]), *ffn_w, (i, 0), ffn_g, f"ffn{i}a")
    grad_x = dh[None]

    gr['ffn_w_gate'], gr['ffn_w_up'], gr['ffn_w_down'] = ffn_g
    gr['ple_w_gate'] = _to_quarters(jnp.stack(d_ple_gate), 1)
    gr['ple_w_proj'] = _to_quarters(jnp.stack(d_ple_proj), 2)
    gr['ffn_norm'] = jnp.stack([jnp.stack([d_ffn_norm[i][k][0] for k in range(2)]) for i in range(n_layers)])
    gr['mix_norm'] = jnp.concatenate(d_mix, axis=0)
    gr['ple_norm'] = jnp.concatenate(d_ple_norm, axis=0)

    gq = [gr[n].reshape((N_CHIPS, 2, -1, arg[n].shape[-1])) for n in _BIG]
    got = _swap_halves(gq, name="grad_swap_halves")
    pairs = [_add_pair(g, o, core, name=f"grad_add_pair_{n}") for n, g, o in zip(_BIG, gq, got)]
    recv = _scatter_quarters(pairs, name="grad_scatter")
    tots = [_add_chips(pr, rc, chip, name=f"grad_add_chips_{n}") for n, pr, rc in zip(_BIG, pairs, recv)]
    theirs = _share_halves(tots, name="grad_share")

    whole_shapes = [arg[n].shape for n in _WHOLE]
    cut_full_shapes = [gr[n].shape for n in small_names]
    gvec = _pack([gr[n].reshape(-1) for n in _WHOLE + small_names], SMALL_ROW_MULT)
    gall = _sum_blocks(_gather_all(gvec, name="gather_small_grads"), N_DEV, name="sum_small_grads")
    parts = _unpack(gall, whole_shapes + cut_full_shapes)
    gsum = dict(zip(_WHOLE, parts))
    for n, g in zip(small_names, parts[len(_WHOLE):]):
        gsum[n] = lax.dynamic_index_in_dim(_to_quarters(g, _SMALL_CUT[n]), chip[0], axis=0, keepdims=False)

    delta, new_m, new_v = {}, {}, {}
    for n, mine, other in zip(_BIG, tots, theirs):
        res = _adamw_halves(_halves(arg[n]), _halves(arg["m_" + n]), _halves(arg["v_" + n]), mine, other, core,
                            name=f"adamw_{n}")
        gsum[n], delta[n], new_m[n], new_v[n] = (r.reshape(arg[n].shape) for r in res)
    for n in _WHOLE + small_names:
        delta[n], new_m[n], new_v[n] = _adamw(arg[n], gsum[n], arg["m_" + n], arg["v_" + n], name=f"adamw_{n}")
    return (loss, grad_x, *[gsum[n] for n in _WEIGHTS], *[delta[n] for n in _WEIGHTS],
            *[new_m[n] for n in _WEIGHTS], *[new_v[n] for n in _WEIGHTS])
```

```python
import functools
import math

import jax
import jax.numpy as jnp
from jax import lax
from jax.experimental import pallas as pl
from jax.experimental.pallas import tpu as pltpu

F32 = jnp.float32
BF16 = jnp.bfloat16
MESH = pl.DeviceIdType.MESH

LANES = 128
VMEM_LIMIT_BYTES = 56 * 1024 * 1024

EPS = 1e-6
D_MODEL = 1024
HEAD_DIM = 64
SB_HEADS = 8
SWA_HEADS = 8
SWA_KV_HEADS = 2
WINDOW = 128
GDN_K_HEADS = 8
GDN_V_HEADS = 16
GDN_HEAD_DIM = 128
GDN_CONV = 4
GDN_CHUNK = 64
SB_W = SB_HEADS * HEAD_DIM
SWA_QW = SWA_HEADS * HEAD_DIM
SWA_KVW = SWA_KV_HEADS * HEAD_DIM
GDN_KW = GDN_K_HEADS * GDN_HEAD_DIM
GDN_VW = GDN_V_HEADS * GDN_HEAD_DIM
GDN_CONV_W = 2 * GDN_KW + GDN_VW

ADAM_LR = 0.001
ADAM_B1 = 0.9
ADAM_B2 = 0.999
ADAM_EPS = 1e-08
ADAM_WD = 0.01
ADAM_STEP = 10

NEG = -1e30


def _params(*sem):
    return pltpu.CompilerParams(dimension_semantics=sem or None, vmem_limit_bytes=VMEM_LIMIT_BYTES)


def _tile(n, cap, align=LANES):
    if n <= cap:
        return n
    for t in range(cap - cap % align, 0, -align):
        if n % t == 0:
            return t
    return n


N_CHIPS = 4
Q = "q"


def _opnd(x):
    return x if isinstance(x, tuple) else (x, ())


def _mm(a, b, *, name, ta=False, tb=False, out_dtype=F32, res=None, scale=1.0, out_q=False, into=None,
        tm=512, tn=1024, tk=1024):
    (a_arr, a_lead), (b_arr, b_lead) = _opnd(a), _opnd(b)
    (k_a, m) = a_arr.shape[-2:] if ta else a_arr.shape[-2:][::-1]
    (n, k_b) = b_arr.shape[-2:] if tb else b_arr.shape[-2:][::-1]
    if into is not None:
        out_arr, out_lead = into
        out_q, out_dtype = Q in out_lead, out_arr.dtype
    else:
        out_lead = (Q,) if out_q else ()
    red_q = (Q in a_lead or Q in b_lead) and not out_q
    kq = min(k_a, k_b)
    assert (k_a == k_b) or (red_q and max(k_a, k_b) == N_CHIPS * kq), (a_arr.shape, b_arr.shape)
    tm, tn, tk = _tile(m, tm), _tile(n, tn), _tile(kq, tk)
    nk = kq // tk
    ksteps = nk * (N_CHIPS if red_q else 1)
    dims = (((0 if ta else 1,), (1 if tb else 0,)), ((), ()))
    has_res = res is not None

    def body(*refs):
        a_ref, b_ref = refs[0], refs[1]
        o_ref, acc_ref = refs[-2], refs[-1]
        k = pl.program_id(3)

        @pl.when(k == 0)
        def _():
            acc_ref[...] = jnp.zeros_like(acc_ref)

        acc_ref[...] += lax.dot_general(a_ref[...].astype(BF16), b_ref[...].astype(BF16), dims,
                                        preferred_element_type=F32)

        @pl.when(k == ksteps - 1)
        def _():
            r = acc_ref[...]
            if scale != 1.0:
                r = r * scale
            if has_res:
                r = r + refs[2][...].astype(F32)
            o_ref[...] = r.astype(o_ref.dtype)

    def spec(lead, blk, pos):
        def index(s, i, j, k):
            kk = k % nk if (red_q and Q in lead) else k
            quarter = s if out_q else k // nk
            return tuple(quarter if l == Q else l for l in lead) + pos(i, j, kk)
        return pl.BlockSpec((None,) * len(lead) + blk, index)

    a_spec = spec(a_lead, (tk, tm), lambda i, j, k: (k, i)) if ta else spec(a_lead, (tm, tk), lambda i, j, k: (i, k))
    b_spec = spec(b_lead, (tn, tk), lambda i, j, k: (j, k)) if tb else spec(b_lead, (tk, tn), lambda i, j, k: (k, j))
    o_spec = spec(out_lead, (tm, tn), lambda i, j, k: (i, j))
    in_specs, args = [a_spec, b_spec], [a_arr, b_arr]
    if has_res:
        r_arr, r_lead = _opnd(res)
        in_specs.append(spec(r_lead, (tm, tn), lambda i, j, k: (i, j)))
        args.append(r_arr)
    aliases = {}
    if into is not None:
        in_specs.append(pl.BlockSpec(memory_space=pl.ANY))
        args.append(out_arr)
        aliases = {len(args) - 1: 0}
        out_shape = jax.ShapeDtypeStruct(out_arr.shape, out_arr.dtype)
    else:
        out_shape = jax.ShapeDtypeStruct(((N_CHIPS,) if out_q else ()) + (m, n), out_dtype)
    return pl.pallas_call(
        body, name=name, grid=(N_CHIPS if out_q else 1, m // tm, n // tn, ksteps), in_specs=in_specs, out_specs=o_spec,
        out_shape=out_shape, scratch_shapes=[pltpu.VMEM((tm, tn), F32)], input_output_aliases=aliases,
        compiler_params=_params("parallel", "parallel", "parallel", "arbitrary"),
    )(*args)


def _row_spec(r, tm):
    if isinstance(r, tuple):
        arr, width, cb = r
        return arr, pl.BlockSpec((tm, width), lambda i, cb=cb: (i, cb))
    return r, pl.BlockSpec((tm, r.shape[1]), lambda i: (i, 0))


def _const_spec(c):
    return pl.BlockSpec(c.shape, lambda i: (0,) * c.ndim)


def _row_fwd(fn, rows, consts, outs, *, name, tm=256):
    tm = _tile(_row_spec(rows[0], tm)[0].shape[0], tm, 8)
    arrs, specs = zip(*[_row_spec(r, tm) for r in rows])
    t = arrs[0].shape[0]
    nr, nc = len(rows), len(consts)

    def body(*refs):
        vals = [r[...].astype(F32) for r in refs[:nr + nc]]
        res = fn(*vals)
        for o_ref, v in zip(refs[nr + nc:], res):
            o_ref[...] = v.astype(o_ref.dtype)

    out = pl.pallas_call(
        body, name=name, grid=(t // tm,),
        in_specs=list(specs) + [_const_spec(c) for c in consts],
        out_specs=[pl.BlockSpec((tm, w), lambda i: (i, 0)) for w, _ in outs],
        out_shape=[jax.ShapeDtypeStruct((t, w), dt) for w, dt in outs],
        compiler_params=_params("parallel"),
    )(*arrs, *consts)
    return list(out)


def _row_bwd(fn, rows, consts, cts, row_grads, const_grads, *, name, tm=256):
    tm = _tile(_row_spec(rows[0], tm)[0].shape[0], tm, 8)
    arrs, specs = zip(*[_row_spec(r, tm) for r in rows])
    ct_arrs, ct_specs = zip(*[_row_spec(r, tm) for r in cts])
    t = arrs[0].shape[0]
    nr, nc, nt = len(rows), len(consts), len(cts)
    n_in = nr + nc + nt

    def body(*refs):
        vals = [r[...].astype(F32) for r in refs[:nr + nc]]
        ctv = tuple(r[...].astype(F32) for r in refs[nr + nc:n_in])
        _, vjp = jax.vjp(fn, *vals)
        g = vjp(ctv)
        outs = refs[n_in:]
        for (idx, _), o_ref in zip(row_grads, outs[:len(row_grads)]):
            o_ref[...] = g[idx].astype(o_ref.dtype)
        first = pl.program_id(0) == 0
        for ci, o_ref in zip(const_grads, outs[len(row_grads):]):
            @pl.when(first)
            def _(o_ref=o_ref):
                o_ref[...] = jnp.zeros_like(o_ref)

            o_ref[...] += g[nr + ci]

    widths = [(_row_spec(rows[idx], tm)[1].block_shape[1], dt) for idx, dt in row_grads]
    out = pl.pallas_call(
        body, name=name, grid=(t // tm,),
        in_specs=list(specs) + [_const_spec(c) for c in consts] + list(ct_specs),
        out_specs=[pl.BlockSpec((tm, w), lambda i: (i, 0)) for w, _ in widths]
        + [_const_spec(consts[ci]) for ci in const_grads],
        out_shape=[jax.ShapeDtypeStruct((t, w), dt) for w, dt in widths]
        + [jax.ShapeDtypeStruct(consts[ci].shape, F32) for ci in const_grads],
        compiler_params=_params("arbitrary"),
    )(*arrs, *consts, *ct_arrs)
    return list(out)


def _rms(x, g):
    return x * lax.rsqrt(jnp.mean(x * x, axis=-1, keepdims=True) + EPS) * g


def _f_rms(h, g):
    return (_rms(h, g),)


def _f_rms_res(h, g):
    return (_rms(h, g), h)


def _f_swiglu(g, u):
    return (g * jax.nn.sigmoid(g) * u,)


def _f_ple(h, gl, pp):
    return (h + jax.nn.sigmoid(gl) * pp,)


def _f_gdn_out(o, z, gain):
    outs = []
    for hd in range(GDN_V_HEADS):
        sl = slice(hd * GDN_HEAD_DIM, (hd + 1) * GDN_HEAD_DIM)
        oh, zh = o[:, sl], z[:, sl]
        outs.append(_rms(oh, gain) * (zh * jax.nn.sigmoid(zh)))
    return (jnp.concatenate(outs, axis=1),)


def _loss_head(y, target, *, name, tm=512):
    t, d = y.shape
    tm = _tile(t, tm, 8)

    def body(y_ref, t_ref, dy_ref, l_ref):
        @pl.when(pl.program_id(0) == 0)
        def _():
            l_ref[...] = jnp.zeros_like(l_ref)

        e = y_ref[...] - t_ref[...]
        dy_ref[...] = e * (1.0 / d)
        l_ref[...] += jnp.sum(e * e) * (0.5 / d)

    dy, l = pl.pallas_call(
        body, name=name, grid=(t // tm,),
        in_specs=[pl.BlockSpec((tm, d), lambda i: (i, 0))] * 2,
        out_specs=[pl.BlockSpec((tm, d), lambda i: (i, 0)), pl.BlockSpec((8, LANES), lambda i: (0, 0))],
        out_shape=[jax.ShapeDtypeStruct((t, d), F32), jax.ShapeDtypeStruct((8, LANES), F32)],
        compiler_params=_params("arbitrary"),
    )(y, target)
    return dy, l[0, 0]


def _dg(a, b, ca, cb):
    nb = a.ndim - 2
    batch = tuple(range(nb))
    return lax.dot_general(a, b, (((ca + nb,), (cb + nb,)), (batch, batch)), preferred_element_type=F32)


def _b(x):
    return x.astype(BF16)


@jax.custom_vjp
def _bdot(a, b):
    return _dg(_b(a), _b(b), 1, 0)


def _bdot_fwd(a, b):
    return _bdot(a, b), (a, b)


def _bdot_bwd(r, ct):
    a, b = r
    return _dg(_b(ct), _b(b), 1, 1), _dg(_b(a), _b(ct), 0, 0)


_bdot.defvjp(_bdot_fwd, _bdot_bwd)


@jax.custom_vjp
def _bdot_nt(a, b):
    return _dg(_b(a), _b(b), 1, 1)


def _bdot_nt_fwd(a, b):
    return _bdot_nt(a, b), (a, b)


def _bdot_nt_bwd(r, ct):
    a, b = r
    return _dg(_b(ct), _b(b), 1, 0), _dg(_b(ct), _b(a), 0, 0)


_bdot_nt.defvjp(_bdot_nt_fwd, _bdot_nt_bwd)


@jax.custom_vjp
def _bdot_tn(a, b):
    return _dg(_b(a), _b(b), 0, 0)


def _bdot_tn_fwd(a, b):
    return _bdot_tn(a, b), (a, b)


def _bdot_tn_bwd(r, ct):
    a, b = r
    return _dg(_b(b), _b(ct), 1, 1), _dg(_b(a), _b(ct), 1, 0)


_bdot_tn.defvjp(_bdot_tn_fwd, _bdot_tn_bwd)


def _two(x):
    hi = x.astype(BF16)
    return hi, (x - hi.astype(F32)).astype(BF16)


def _dg3(a, b, ca, cb):
    (ah, al), (bh, bl) = _two(a), _two(b)
    return _dg(ah, bh, ca, cb) + (_dg(ah, bl, ca, cb) + _dg(al, bh, ca, cb))


@jax.custom_vjp
def _hdot(a, b):
    return _dg3(a, b, 1, 0)


def _hdot_fwd(a, b):
    return _hdot(a, b), (a, b)


def _hdot_bwd(r, ct):
    a, b = r
    return _dg3(ct, b, 1, 1), _dg3(a, ct, 0, 0)


_hdot.defvjp(_hdot_fwd, _hdot_bwd)


def _split_dot(x, u):
    hi, lo = _two(x)
    return _dg(hi, u, 1, 0) + _dg(lo, u, 1, 0)


@jax.custom_vjp
def _ldot(l01, x):
    hi, lo = _two(x)
    l01 = l01.astype(BF16)
    return _dg(l01, hi, 1, 0) + _dg(l01, lo, 1, 0)


def _ldot_fwd(l01, x):
    return _ldot(l01, x), l01


def _ldot_bwd(l01, ct):
    hi, lo = _two(ct)
    l01b = l01.astype(BF16)
    return jnp.zeros_like(l01), _dg(l01b, hi, 0, 0) + _dg(l01b, lo, 0, 0)


_ldot.defvjp(_ldot_fwd, _ldot_bwd)


SB_BLK = 128
SB_KEYS = 512
SB_SCALE = HEAD_DIM ** -0.5


def _sb_consts(t):
    kb = min(SB_KEYS, t)
    lane = lax.broadcasted_iota(jnp.int32, (SB_BLK, kb), 1)
    row = lax.broadcasted_iota(jnp.int32, (SB_BLK, kb), 0)
    ur = lax.broadcasted_iota(jnp.int32, (kb, kb), 0)
    uc = lax.broadcasted_iota(jnp.int32, (kb, kb), 1)
    head0 = lax.broadcasted_iota(jnp.int32, (SB_BLK, LANES), 1) < HEAD_DIM
    return kb, lane, row, ur, uc, head0


def _sb_fwd(proj, *, name):
    t = proj.shape[0]
    nb = t // SB_BLK
    npair = SB_W // LANES

    def body(q_ref, k_ref, v_ref, o_ref, r_ref):
        i = pl.program_id(1)
        kb, lane, row, ur, uc, head0 = _sb_consts(t)
        u_suffix = (ur >= uc).astype(BF16)
        q = q_ref[...]
        qh = [_b(jnp.where(head0, q, 0.0)), _b(jnp.where(head0, 0.0, q))]
        diag = (i * SB_BLK) // kb

        def block(j, carry, masked):
            keys = pl.ds(pl.multiple_of(j * kb, kb), kb)
            kj, vj = _b(k_ref[keys, :]), _b(v_ref[keys, :])
            causal = (j * kb + lane) < (i * SB_BLK + row)
            out = []
            for h in range(2):
                acc, car = carry[h]
                z = _dg(qh[h], kj, 1, 1) * SB_SCALE
                ls = jax.nn.log_sigmoid(z)
                lk = ls - z
                if masked:
                    lk = jnp.where(causal, lk, 0.0)
                suf = _split_dot(lk, u_suffix) + car
                w = jnp.exp(ls + (suf - lk))
                if masked:
                    w = jnp.where(causal, w, 0.0)
                out.append((acc + _dg(_b(w), vj, 1, 0), suf[:, 0:1]))
            return tuple(out)

        zero = (jnp.zeros((SB_BLK, LANES), F32), jnp.zeros((SB_BLK, 1), F32))
        carry = block(diag, (zero, zero), True)
        carry = lax.fori_loop(0, diag, lambda s, c: block(diag - 1 - s, c, False), carry)
        o_ref[...] = jnp.where(head0, carry[0][0], carry[1][0])
        r_ref[...] = jnp.where(head0, carry[0][1], carry[1][1])

    return pl.pallas_call(
        body, name=name, grid=(npair, nb),
        in_specs=[pl.BlockSpec((SB_BLK, LANES), lambda p, i: (i, p)),
                  pl.BlockSpec((t, LANES), lambda p, i: (0, npair + p)),
                  pl.BlockSpec((t, LANES), lambda p, i: (0, 2 * npair + p))],
        out_specs=[pl.BlockSpec((SB_BLK, LANES), lambda p, i: (i, p)),
                   pl.BlockSpec((None, SB_BLK, LANES), lambda p, i: (p, i, 0))],
        out_shape=[jax.ShapeDtypeStruct((t, SB_W), F32), jax.ShapeDtypeStruct((npair, t, LANES), F32)],
        compiler_params=_params("parallel", "arbitrary"),
    )(proj, proj, proj)


def _sb_bwd(proj, rtot, dout, *, name):
    t = proj.shape[0]
    nb = t // SB_BLK
    npair = SB_W // LANES

    def body(q_ref, k_ref, v_ref, r_ref, do_ref, dq_ref, dk_ref, dv_ref):
        i = pl.program_id(1)
        kb, lane, row, ur, uc, head0 = _sb_consts(t)
        u_incl = (ur <= uc).astype(BF16)
        u_excl = (ur < uc).astype(BF16)
        q, do, rt = q_ref[...], do_ref[...], r_ref[...]
        qb, dob = _b(q), _b(do)
        qh = [_b(jnp.where(head0, q, 0.0)), _b(jnp.where(head0, 0.0, q))]
        doh = [_b(jnp.where(head0, do, 0.0)), _b(jnp.where(head0, 0.0, do))]
        rh = [rt[:, 0:1], rt[:, HEAD_DIM:HEAD_DIM + 1]]
        diag = (i * SB_BLK) // kb

        @pl.when(i == 0)
        def _():
            dk_ref[...] = jnp.zeros_like(dk_ref)
            dv_ref[...] = jnp.zeros_like(dv_ref)

        def block(j, carry, masked):
            keys = pl.ds(pl.multiple_of(j * kb, kb), kb)
            kj, vj = _b(k_ref[keys, :]), _b(v_ref[keys, :])
            causal = (j * kb + lane) < (i * SB_BLK + row)
            out, dks, dvs = [], [], []
            for h in range(2):
                dq_acc, clk, ce = carry[h]
                z = _dg(qh[h], kj, 1, 1) * SB_SCALE
                ls = jax.nn.log_sigmoid(z)
                lk = ls - z
                if masked:
                    lk = jnp.where(causal, lk, 0.0)
                pre = _split_dot(lk, u_incl) + clk
                w = jnp.exp(ls + (rh[h] - pre))
                if masked:
                    w = jnp.where(causal, w, 0.0)
                e = _dg(doh[h], vj, 1, 1) * w
                pre_e = _split_dot(e, u_excl) + ce
                sig = jnp.exp(ls)
                dz = (e * (1.0 - sig) - sig * pre_e) * SB_SCALE
                if masked:
                    dz = jnp.where(causal, dz, 0.0)
                dzb = _b(dz)
                dks.append(_dg(dzb, qb, 0, 0))
                dvs.append(_dg(_b(w), dob, 0, 0))
                out.append((dq_acc + _dg(dzb, kj, 1, 0), pre[:, kb - 1:], pre_e[:, kb - 1:] + e[:, kb - 1:]))
            head0k = lax.broadcasted_iota(jnp.int32, (kb, LANES), 1) < HEAD_DIM
            dk_ref[keys, :] += jnp.where(head0k, dks[0], dks[1])
            dv_ref[keys, :] += jnp.where(head0k, dvs[0], dvs[1])
            return tuple(out)

        zero = (jnp.zeros((SB_BLK, LANES), F32), jnp.zeros((SB_BLK, 1), F32), jnp.zeros((SB_BLK, 1), F32))
        carry = lax.fori_loop(0, diag, lambda j, c: block(j, c, False), (zero, zero))
        carry = block(diag, carry, True)
        dq_ref[...] = jnp.where(head0, carry[0][0], carry[1][0])

    blk = pl.BlockSpec((SB_BLK, LANES), lambda p, i: (i, p))
    whole = pl.BlockSpec((t, LANES), lambda p, i: (0, p))
    return pl.pallas_call(
        body, name=name, grid=(npair, nb),
        in_specs=[blk,
                  pl.BlockSpec((t, LANES), lambda p, i: (0, npair + p)),
                  pl.BlockSpec((t, LANES), lambda p, i: (0, 2 * npair + p)),
                  pl.BlockSpec((None, SB_BLK, LANES), lambda p, i: (p, i, 0)),
                  blk],
        out_specs=[blk, whole, whole],
        out_shape=[jax.ShapeDtypeStruct((t, SB_W), F32)] * 3,
        compiler_params=_params("arbitrary", "arbitrary"),
    )(proj, proj, proj, rtot, dout)


SWA_G = SWA_HEADS // SWA_KV_HEADS


def _swa_heads(first, qs, ks, vs, qg, kg, sinks):
    qi = lax.broadcasted_iota(jnp.int32, (WINDOW, 2 * WINDOW), 0)
    kj = lax.broadcasted_iota(jnp.int32, (WINDOW, 2 * WINDOW), 1)
    dist = qi + WINDOW - kj
    valid = (dist >= 0) & (dist < WINDOW) & (jnp.logical_not(first) | (kj >= WINDOW))
    distf = dist.astype(F32)
    outs = []
    for hk in range(SWA_KV_HEADS):
        kn = _rms(ks[hk], kg)
        for g in range(SWA_G):
            h = hk * SWA_G + g
            slope = 2.0 ** (-8.0 * (h + 1) / SWA_HEADS)
            s = _bdot_nt(_rms(qs[h], qg), kn) * (HEAD_DIM ** -0.5)
            s = jnp.where(valid, s - slope * distf, NEG)
            m = lax.stop_gradient(jnp.maximum(jnp.max(s, axis=1, keepdims=True), sinks[h]))
            p = jnp.exp(s - m)
            den = jnp.sum(p, axis=1, keepdims=True) + jnp.exp(sinks[h] - m)
            outs.append(_bdot(p / den, vs[hk]))
    return tuple(outs)


def _swa_split(q, kp, kc, vp, vc, sk):
    qs = [q[:, h * HEAD_DIM:(h + 1) * HEAD_DIM] for h in range(SWA_HEADS)]
    k2, v2 = jnp.concatenate([kp, kc], axis=0), jnp.concatenate([vp, vc], axis=0)
    ks = [k2[:, h * HEAD_DIM:(h + 1) * HEAD_DIM] for h in range(SWA_KV_HEADS)]
    vs = [v2[:, h * HEAD_DIM:(h + 1) * HEAD_DIM] for h in range(SWA_KV_HEADS)]
    sinks = [sk[:, h:h + 1] for h in range(SWA_HEADS)]
    return qs, ks, vs, sinks


def _swa_specs(t):
    qcb = (3 * SB_W) // SWA_QW
    kcb = (3 * SB_W + SWA_QW) // SWA_KVW
    prev = lambda i: jnp.maximum(i - 1, 0)
    return [pl.BlockSpec((WINDOW, SWA_QW), lambda i: (i, qcb)),
            pl.BlockSpec((WINDOW, SWA_KVW), lambda i: (prev(i), kcb)),
            pl.BlockSpec((WINDOW, SWA_KVW), lambda i: (i, kcb)),
            pl.BlockSpec((WINDOW, SWA_KVW), lambda i: (prev(i), kcb + 1)),
            pl.BlockSpec((WINDOW, SWA_KVW), lambda i: (i, kcb + 1)),
            pl.BlockSpec((1, HEAD_DIM), lambda i: (0, 0)),
            pl.BlockSpec((1, HEAD_DIM), lambda i: (0, 0)),
            pl.BlockSpec((1, SWA_HEADS), lambda i: (0, 0))]


def _swa_fwd(proj, qg, kg, sinks, *, name):
    t = proj.shape[0]

    def body(q_ref, kp_ref, kc_ref, vp_ref, vc_ref, qg_ref, kg_ref, sk_ref, o_ref):
        first = pl.program_id(0) == 0
        qs, ks, vs, sk = _swa_split(q_ref[...], kp_ref[...], kc_ref[...], vp_ref[...], vc_ref[...], sk_ref[...])
        o_ref[...] = jnp.concatenate(_swa_heads(first, qs, ks, vs, qg_ref[...], kg_ref[...], sk), axis=1)

    return pl.pallas_call(
        body, name=name, grid=(t // WINDOW,), in_specs=_swa_specs(t),
        out_specs=pl.BlockSpec((WINDOW, SWA_QW), lambda i: (i, 0)),
        out_shape=jax.ShapeDtypeStruct((t, SWA_QW), F32),
        compiler_params=_params("parallel"),
    )(proj, proj, proj, proj, proj, qg, kg, sinks)


def _swa_bwd(proj, qg, kg, sinks, dout, *, name):
    t = proj.shape[0]

    def body(q_ref, kp_ref, kc_ref, vp_ref, vc_ref, qg_ref, kg_ref, sk_ref, do_ref,
             dq_ref, dk_ref, dv_ref, dqg_ref, dkg_ref, dsk_ref):
        i = pl.program_id(0)
        first = i == 0

        @pl.when(first)
        def _():
            for r in (dk_ref, dv_ref, dqg_ref, dkg_ref, dsk_ref):
                r[...] = jnp.zeros_like(r)

        qs, ks, vs, sk = _swa_split(q_ref[...], kp_ref[...], kc_ref[...], vp_ref[...], vc_ref[...], sk_ref[...])
        do = do_ref[...]
        cts = tuple(do[:, h * HEAD_DIM:(h + 1) * HEAD_DIM] for h in range(SWA_HEADS))
        _, vjp = jax.vjp(functools.partial(_swa_heads, first), qs, ks, vs, qg_ref[...], kg_ref[...], sk)
        dqs, dks, dvs, dqg, dkg, dsk = vjp(cts)
        dq_ref[...] = jnp.concatenate(dqs, axis=1)
        dk2, dv2 = jnp.concatenate(dks, axis=1), jnp.concatenate(dvs, axis=1)
        cur = pl.ds(pl.multiple_of(i * WINDOW, WINDOW), WINDOW)
        prv = pl.ds(pl.multiple_of(jnp.maximum(i - 1, 0) * WINDOW, WINDOW), WINDOW)
        dk_ref[prv, :] += dk2[:WINDOW]
        dv_ref[prv, :] += dv2[:WINDOW]
        dk_ref[cur, :] += dk2[WINDOW:]
        dv_ref[cur, :] += dv2[WINDOW:]
        dqg_ref[...] += dqg
        dkg_ref[...] += dkg
        dsk_ref[...] += jnp.concatenate(dsk, axis=1)

    whole = lambda shape: pl.BlockSpec(shape, lambda i: (0, 0))
    return pl.pallas_call(
        body, name=name, grid=(t // WINDOW,),
        in_specs=_swa_specs(t) + [pl.BlockSpec((WINDOW, SWA_QW), lambda i: (i, 0))],
        out_specs=[pl.BlockSpec((WINDOW, SWA_QW), lambda i: (i, 0)), whole((t, SWA_KVW)), whole((t, SWA_KVW)),
                   whole((1, HEAD_DIM)), whole((1, HEAD_DIM)), whole((1, SWA_HEADS))],
        out_shape=[jax.ShapeDtypeStruct((t, SWA_QW), F32), jax.ShapeDtypeStruct((t, SWA_KVW), F32),
                   jax.ShapeDtypeStruct((t, SWA_KVW), F32), jax.ShapeDtypeStruct((1, HEAD_DIM), F32),
                   jax.ShapeDtypeStruct((1, HEAD_DIM), F32), jax.ShapeDtypeStruct((1, SWA_HEADS), F32)],
        compiler_params=_params("arbitrary"),
    )(proj, proj, proj, proj, proj, qg, kg, sinks, dout)


CONV_CB = 512
CONV_TM = 512
HALO = 8


def _conv_pre(x_ref, h_ref, w_ref, i):
    halo = jnp.where(i > 0, h_ref[...], 0.0)
    xe = jnp.concatenate([halo, x_ref[...]], axis=0)
    tm = x_ref.shape[0]
    w = w_ref[...]
    c = sum(w[k:k + 1, :] * xe[HALO - (GDN_CONV - 1) + k:HALO - (GDN_CONV - 1) + k + tm] for k in range(GDN_CONV))
    return c, xe


def _conv_specs(tm, cb):
    return [pl.BlockSpec((tm, cb), lambda c, i: (i, c)),
            pl.BlockSpec((HALO, cb), lambda c, i: (jnp.maximum(i * (tm // HALO) - 1, 0), c)),
            pl.BlockSpec((GDN_CONV, cb), lambda c, i: (0, c))]


def _conv_fwd(x, w, dact=None, *, name):
    t, ch = x.shape
    tm, cb = _tile(t, CONV_TM), _tile(ch, CONV_CB)

    def body(*refs):
        x_ref, h_ref, w_ref = refs[:3]
        c, _ = _conv_pre(x_ref, h_ref, w_ref, pl.program_id(1))
        sig = jax.nn.sigmoid(c)
        if dact is None:
            refs[3][...] = c * sig
        else:
            refs[4][...] = refs[3][...] * (sig * (1.0 + c * (1.0 - sig)))

    tile = pl.BlockSpec((tm, cb), lambda c, i: (i, c))
    extra = () if dact is None else (dact,)
    return pl.pallas_call(
        body, name=name, grid=(ch // cb, t // tm),
        in_specs=_conv_specs(tm, cb) + [tile] * len(extra), out_specs=tile,
        out_shape=jax.ShapeDtypeStruct((t, ch), F32),
        compiler_params=_params("parallel", "parallel"),
    )(x, x, w, *extra)


def _conv_bwd(x, w, dc, *, name):
    t, ch = x.shape
    tm, cb = _tile(t, CONV_TM), _tile(ch, CONV_CB)
    nt = t // tm

    def body(x_ref, h_ref, w_ref, dc_ref, nh_ref, dx_ref, dw_ref):
        i = pl.program_id(1)

        @pl.when(i == 0)
        def _():
            dw_ref[...] = jnp.zeros_like(dw_ref)

        halo = jnp.where(i > 0, h_ref[...], 0.0)
        xe = jnp.concatenate([halo, x_ref[...]], axis=0)
        dc = dc_ref[...]
        dce = jnp.concatenate([dc, jnp.where(i < nt - 1, nh_ref[...], 0.0)], axis=0)
        w = w_ref[...]
        last = GDN_CONV - 1
        dx_ref[...] = sum(w[k:k + 1, :] * dce[last - k:last - k + tm] for k in range(GDN_CONV))
        dw_ref[...] += jnp.concatenate(
            [jnp.sum(dc * xe[HALO - last + k:HALO - last + k + tm], axis=0, keepdims=True) for k in range(GDN_CONV)],
            axis=0)

    tile = pl.BlockSpec((tm, cb), lambda c, i: (i, c))
    nxt = pl.BlockSpec((HALO, cb), lambda c, i: (jnp.minimum((i + 1) * (tm // HALO), t // HALO - 1), c))
    return pl.pallas_call(
        body, name=name, grid=(ch // cb, nt),
        in_specs=_conv_specs(tm, cb) + [tile, nxt],
        out_specs=[tile, pl.BlockSpec((GDN_CONV, cb), lambda c, i: (0, c))],
        out_shape=[jax.ShapeDtypeStruct((t, ch), F32), jax.ShapeDtypeStruct((GDN_CONV, ch), F32)],
        compiler_params=_params("parallel", "arbitrary"),
    )(x, x, w, dc, dc)


def _gdn_chunk(qraw, kraw, v, bl, a, alog, dtb, state):
    c, d = GDN_CHUNK, GDN_HEAD_DIM
    nh = qraw.shape[0]
    ri = lax.broadcasted_iota(jnp.int32, (nh, c, c), 1)
    ci = lax.broadcasted_iota(jnp.int32, (nh, c, c), 2)
    incl, strict = ri >= ci, ri > ci
    q = qraw * lax.rsqrt(jnp.sum(qraw * qraw, axis=-1, keepdims=True) + EPS) * (d ** -0.5)
    k = kraw * lax.rsqrt(jnp.sum(kraw * kraw, axis=-1, keepdims=True) + EPS)
    beta = jax.nn.sigmoid(bl)
    g = -jnp.exp(alog) * jax.nn.softplus(a + dtb)
    gc = _ldot(incl.astype(F32), jnp.broadcast_to(g, (nh, c, d)))
    gcm = gc[:, :, :c]
    decay = jnp.exp(jnp.where(incl, gcm - jnp.swapaxes(gcm, 1, 2), NEG))
    eg = jnp.exp(gc)
    kbeta = k * beta
    x = -jnp.where(strict, _bdot_nt(kbeta, k) * decay, 0.0)
    tinv = (ri == ci).astype(F32) + x
    pw = x
    for _ in range(int(math.log2(c)) - 1):
        pw = _hdot(pw, pw)
        tinv = tinv + _hdot(tinv, pw)
    u = _hdot(tinv, v * beta)
    w = _hdot(tinv, kbeta * eg)
    attn = jnp.where(incl, _bdot_nt(q, k) * decay, 0.0)
    glast = gc[:, c - 1:c, :]
    v_new = u - _bdot(w, state)
    o = _bdot(q * eg, state) + _bdot(attn, v_new)
    state = state * jnp.exp(glast) + _bdot_tn(k * jnp.exp(glast - gc), v_new)
    return o, state


GDN_REP = GDN_V_HEADS // GDN_K_HEADS
GDN_HB = 4


def _gdn_pick(vals, kh, r):
    ba, alog, dtb = vals
    lane = lax.broadcasted_iota(jnp.int32, ba.shape, 1)
    hv = kh * GDN_REP + r
    bl = jnp.sum(jnp.where(lane == hv, ba, 0.0), axis=1, keepdims=True)
    a = jnp.sum(jnp.where(lane == GDN_V_HEADS + hv, ba, 0.0), axis=1, keepdims=True)
    lane1 = lax.broadcasted_iota(jnp.int32, alog.shape, 1)
    al = jnp.sum(jnp.where(lane1 == hv, alog, 0.0), axis=1, keepdims=True)
    db = jnp.sum(jnp.where(lane1 == hv, dtb, 0.0), axis=1, keepdims=True)
    return bl, a, al, db


def _gdn_stack(qs, ks, vs, small, j):
    d = GDN_HEAD_DIM
    per = [[], [], [], [], [], [], []]
    for hh in range(GDN_HB):
        q, k = qs[:, hh * d:(hh + 1) * d], ks[:, hh * d:(hh + 1) * d]
        for r in range(GDN_REP):
            col = (hh * GDN_REP + r) * d
            for lst, val in zip(per, (q, k, vs[:, col:col + d]) + _gdn_pick(small, j * GDN_HB + hh, r)):
                lst.append(val)
    return tuple(jnp.stack(lst) for lst in per)


def _gdn_specs(nchunk, rev):
    c, d = GDN_CHUNK, GDN_HEAD_DIM
    at = (lambda n: nchunk - 1 - n) if rev else (lambda n: n)
    ng = GDN_K_HEADS // GDN_HB
    return at, [pl.BlockSpec((c, GDN_HB * d), lambda n, j: (at(n), j)),
                pl.BlockSpec((c, GDN_HB * d), lambda n, j: (at(n), ng + j)),
                pl.BlockSpec((c, GDN_HB * GDN_REP * d), lambda n, j: (at(n), ng + j)),
                pl.BlockSpec((c, 2 * GDN_V_HEADS), lambda n, j: (at(n), 0)),
                pl.BlockSpec((1, GDN_V_HEADS), lambda n, j: (0, 0)),
                pl.BlockSpec((1, GDN_V_HEADS), lambda n, j: (0, 0))]


def _gdn_fwd(act, ba, alog, dtb, *, name):
    t = act.shape[0]
    c, d = GDN_CHUNK, GDN_HEAD_DIM
    nchunk = t // c
    at, specs = _gdn_specs(nchunk, False)

    def body(q_ref, k_ref, v_ref, ba_ref, al_ref, db_ref, o_ref, s_ref, state):
        n, j = pl.program_id(0), pl.program_id(1)
        heads = pl.ds(j * GDN_HB, GDN_HB)

        @pl.when(n == 0)
        def _():
            state[heads] = jnp.zeros((GDN_HB, GDN_REP, d, d), F32)

        s_in = state[heads]
        s_ref[...] = s_in
        args = _gdn_stack(q_ref[...], k_ref[...], v_ref[...], (ba_ref[...], al_ref[...], db_ref[...]), j)
        o, s_new = _gdn_chunk(*args, s_in.reshape(GDN_HB * GDN_REP, d, d))
        o_ref[...] = jnp.concatenate([o[b] for b in range(GDN_HB * GDN_REP)], axis=1)
        state[heads] = s_new.reshape(GDN_HB, GDN_REP, d, d)

    return pl.pallas_call(
        body, name=name, grid=(nchunk, GDN_K_HEADS // GDN_HB), in_specs=specs,
        out_specs=[pl.BlockSpec((c, GDN_HB * GDN_REP * d), lambda n, j: (n, j)),
                   pl.BlockSpec((None, GDN_HB, GDN_REP, d, d), lambda n, j: (n, j, 0, 0, 0))],
        out_shape=[jax.ShapeDtypeStruct((t, GDN_VW), F32),
                   jax.ShapeDtypeStruct((nchunk, GDN_K_HEADS, GDN_REP, d, d), F32)],
        scratch_shapes=[pltpu.VMEM((GDN_K_HEADS, GDN_REP, d, d), F32)],
        compiler_params=_params("arbitrary", "arbitrary"),
    )(act, act, act, ba, alog, dtb)


def _gdn_bwd(act, ba, alog, dtb, states, dout, *, name):
    t = act.shape[0]
    c, d = GDN_CHUNK, GDN_HEAD_DIM
    nchunk = t // c
    at, specs = _gdn_specs(nchunk, True)

    def body(q_ref, k_ref, v_ref, ba_ref, al_ref, db_ref, s_ref, do_ref,
             dq_ref, dk_ref, dv_ref, dba_ref, dal_ref, ddb_ref, dstate):
        n, j = pl.program_id(0), pl.program_id(1)

        @pl.when(n == 0)
        def _():
            dstate[pl.ds(j * GDN_HB, GDN_HB)] = jnp.zeros((GDN_HB, GDN_REP, d, d), F32)

        @pl.when((n == 0) & (j == 0))
        def _():
            dal_ref[...] = jnp.zeros_like(dal_ref)
            ddb_ref[...] = jnp.zeros_like(ddb_ref)

        @pl.when(j == 0)
        def _():
            dba_ref[...] = jnp.zeros_like(dba_ref)

        heads = pl.ds(j * GDN_HB, GDN_HB)
        nh = GDN_HB * GDN_REP
        args = _gdn_stack(q_ref[...], k_ref[...], v_ref[...], (ba_ref[...], al_ref[...], db_ref[...]), j)
        _, vjp = jax.vjp(_gdn_chunk, *args, s_ref[...].reshape(nh, d, d))
        do = do_ref[...]
        do = jnp.stack([do[:, b * d:(b + 1) * d] for b in range(nh)])
        gq, gk, gv, gbl, ga, gal, gdb, gs = vjp((do, dstate[heads].reshape(nh, d, d)))
        dstate[heads] = gs.reshape(GDN_HB, GDN_REP, d, d)
        dq_ref[...] = jnp.concatenate([gq[GDN_REP * hh] + gq[GDN_REP * hh + 1] for hh in range(GDN_HB)], axis=1)
        dk_ref[...] = jnp.concatenate([gk[GDN_REP * hh] + gk[GDN_REP * hh + 1] for hh in range(GDN_HB)], axis=1)
        dv_ref[...] = jnp.concatenate([gv[b] for b in range(nh)], axis=1)
        lane = lax.broadcasted_iota(jnp.int32, (c, 2 * GDN_V_HEADS), 1)
        lane1 = lax.broadcasted_iota(jnp.int32, (1, GDN_V_HEADS), 1)
        dba = jnp.zeros((c, 2 * GDN_V_HEADS), F32)
        dal = jnp.zeros((1, GDN_V_HEADS), F32)
        ddb = jnp.zeros((1, GDN_V_HEADS), F32)
        for b in range(nh):
            hv = j * nh + b
            dba = dba + jnp.where(lane == hv, gbl[b], 0.0) + jnp.where(lane == GDN_V_HEADS + hv, ga[b], 0.0)
            dal = dal + jnp.where(lane1 == hv, gal[b], 0.0)
            ddb = ddb + jnp.where(lane1 == hv, gdb[b], 0.0)
        dba_ref[...] += dba
        dal_ref[...] += dal
        ddb_ref[...] += ddb

    small = pl.BlockSpec((1, GDN_V_HEADS), lambda n, j: (0, 0))
    return pl.pallas_call(
        body, name=name, grid=(nchunk, GDN_K_HEADS // GDN_HB),
        in_specs=specs + [pl.BlockSpec((None, GDN_HB, GDN_REP, d, d), lambda n, j: (at(n), j, 0, 0, 0)),
                          pl.BlockSpec((c, GDN_HB * GDN_REP * d), lambda n, j: (at(n), j))],
        out_specs=[pl.BlockSpec((c, GDN_HB * d), lambda n, j: (at(n), j)),
                   pl.BlockSpec((c, GDN_HB * d), lambda n, j: (at(n), j)),
                   pl.BlockSpec((c, GDN_HB * GDN_REP * d), lambda n, j: (at(n), j)),
                   pl.BlockSpec((c, 2 * GDN_V_HEADS), lambda n, j: (at(n), 0)),
                   small, small],
        out_shape=[jax.ShapeDtypeStruct((t, GDN_KW), F32), jax.ShapeDtypeStruct((t, GDN_KW), F32),
                   jax.ShapeDtypeStruct((t, GDN_VW), F32), jax.ShapeDtypeStruct((t, 2 * GDN_V_HEADS), F32),
                   jax.ShapeDtypeStruct((1, GDN_V_HEADS), F32), jax.ShapeDtypeStruct((1, GDN_V_HEADS), F32)],
        scratch_shapes=[pltpu.VMEM((GDN_K_HEADS, GDN_REP, d, d), F32)],
        compiler_params=_params("arbitrary", "arbitrary"),
    )(act, act, act, ba, alog, dtb, states, dout)


N_DEV = 8
ANY = pl.BlockSpec(memory_space=pl.ANY)


def _coords():
    return lax.axis_index("x"), lax.axis_index("y"), lax.axis_index("c")


def _other_chips(x, y):
    return [(1 - x, y), (x, 1 - y), (1 - x, 1 - y)]


def _remote(src, dst, send_sems, recv_sems, k, to):
    return pltpu.make_async_remote_copy(src_ref=src, dst_ref=dst, send_sem=send_sems.at[k], recv_sem=recv_sems.at[k],
                                        device_id=to, device_id_type=MESH)


def _dma_sems(n):
    return [pltpu.SemaphoreType.DMA((n,)), pltpu.SemaphoreType.DMA((n,))]


def _gather_quarters(parts, *, name):
    na = len(parts)

    def body(*refs):
        ins, outs = refs[:na], refs[na:2 * na]
        send_sems, recv_sems, local_sems = refs[2 * na:]
        x, y, c = _coords()
        sibling = (x, y, 1 - c)
        chips = _other_chips(x, y)
        mine, first, passed = [], [], []
        for a, (x_ref, out_ref) in enumerate(zip(ins, outs)):
            mine.append(pltpu.make_async_copy(x_ref, out_ref.at[2 * x + y], local_sems.at[a]))
            for j, (cx, cy) in enumerate(chips):
                first.append(_remote(x_ref.at[c], out_ref.at[2 * x + y, c], send_sems, recv_sems, 6 * a + j, (cx, cy, c)))
                passed.append(_remote(out_ref.at[2 * cx + cy, c], out_ref.at[2 * cx + cy, c], send_sems, recv_sems,
                                      6 * a + 3 + j, sibling))
        for cp in mine + first:
            cp.start()
        for a, (x_ref, out_ref) in enumerate(zip(ins, outs)):
            for j, (cx, cy) in enumerate(chips):
                _remote(x_ref.at[c], out_ref.at[2 * cx + cy, c], send_sems, recv_sems, 6 * a + j, (cx, cy, c)).wait_recv()
                passed[3 * a + j].start()
        for a, (x_ref, out_ref) in enumerate(zip(ins, outs)):
            for j, (cx, cy) in enumerate(chips):
                _remote(x_ref.at[c], out_ref.at[2 * cx + cy, 1 - c], send_sems, recv_sems, 6 * a + 3 + j,
                        sibling).wait_recv()
        for cp in first + passed:
            cp.wait_send()
        for cp in mine:
            cp.wait()

    return pl.pallas_call(
        body, name=name, in_specs=[ANY] * na, out_specs=[ANY] * na,
        out_shape=[jax.ShapeDtypeStruct((N_CHIPS,) + p.shape, p.dtype) for p in parts],
        scratch_shapes=_dma_sems(6 * na) + [pltpu.SemaphoreType.DMA((na,))],
    )(*parts)


def _swap_halves(grads, *, name):
    na = len(grads)

    def body(*refs):
        ins, outs = refs[:na], refs[na:2 * na]
        send_sems, recv_sems = refs[2 * na:]
        x, y, c = _coords()
        sends = [_remote(g_ref.at[j, 1 - c], o_ref.at[j], send_sems, recv_sems, N_CHIPS * a + j, (x, y, 1 - c))
                 for a, (g_ref, o_ref) in enumerate(zip(ins, outs)) for j in range(N_CHIPS)]
        for cp in sends:
            cp.start()
        for cp in sends:
            cp.wait()

    return pl.pallas_call(
        body, name=name, in_specs=[ANY] * na, out_specs=[ANY] * na,
        out_shape=[jax.ShapeDtypeStruct((N_CHIPS,) + g.shape[2:], g.dtype) for g in grads],
        scratch_shapes=_dma_sems(N_CHIPS * na),
    )(*grads)


def _scatter_quarters(pairs, *, name):
    na = len(pairs)

    def body(*refs):
        ins, outs = refs[:na], refs[na:4 * na]
        send_sems, recv_sems = refs[4 * na:]
        x, y, c = _coords()
        sends = [_remote(p_ref.at[2 * cx + cy], outs[3 * a + j], send_sems, recv_sems, 3 * a + j, (cx, cy, c))
                 for a, p_ref in enumerate(ins) for j, (cx, cy) in enumerate(_other_chips(x, y))]
        for cp in sends:
            cp.start()
        for cp in sends:
            cp.wait()

    out = pl.pallas_call(
        body, name=name, in_specs=[ANY] * na, out_specs=[ANY] * (3 * na),
        out_shape=[jax.ShapeDtypeStruct(p.shape[1:], p.dtype) for p in pairs for _ in range(3)],
        scratch_shapes=_dma_sems(3 * na),
    )(*pairs)
    return [out[3 * a:3 * a + 3] for a in range(na)]


def _share_halves(tots, *, name):
    na = len(tots)

    def body(*refs):
        ins, outs = refs[:na], refs[na:2 * na]
        send_sems, recv_sems = refs[2 * na:]
        x, y, c = _coords()
        sends = [_remote(t_ref, o_ref, send_sems, recv_sems, a, (x, y, 1 - c))
                 for a, (t_ref, o_ref) in enumerate(zip(ins, outs))]
        for cp in sends:
            cp.start()
        for cp in sends:
            cp.wait()

    return pl.pallas_call(
        body, name=name, in_specs=[ANY] * na, out_specs=[ANY] * na,
        out_shape=[jax.ShapeDtypeStruct(t.shape, t.dtype) for t in tots],
        scratch_shapes=_dma_sems(na),
    )(*tots)


def _gather_all(vec, *, name):
    m, w = vec.shape

    def body(x_ref, out_ref, send_sems, recv_sems, local_sem):
        x, y, c = _coords()
        me, sibling = (x, y, c), (x, y, 1 - c)
        chips = _other_chips(x, y)

        def rows(px, py, pc):
            return out_ref.at[pl.ds((4 * px + 2 * py + pc) * m, m), :]

        def copy(k, block, to, src=None):
            return _remote(rows(*block) if src is None else src, rows(*block), send_sems, recv_sems, k, to)

        mine = pltpu.make_async_copy(x_ref, rows(*me), local_sem)
        mine.start()
        first = [copy(0, me, sibling, src=x_ref)]
        first += [copy(1 + j, me, (*chip, c), src=x_ref) for j, chip in enumerate(chips)]
        for cp in first:
            cp.start()
        passed = [copy(4 + j, (*chip, c), sibling) for j, chip in enumerate(chips)]
        for j, chip in enumerate(chips):
            copy(1 + j, (*chip, c), me).wait_recv()
            passed[j].start()
        copy(0, sibling, me).wait_recv()
        for j, chip in enumerate(chips):
            copy(4 + j, (*chip, 1 - c), me).wait_recv()
        for cp in first + passed:
            cp.wait_send()
        mine.wait()

    vm = pl.BlockSpec(memory_space=pltpu.VMEM)
    return pl.pallas_call(
        body, name=name, in_specs=[vm], out_specs=vm, out_shape=jax.ShapeDtypeStruct((N_DEV * m, w), vec.dtype),
        scratch_shapes=_dma_sems(7) + [pltpu.SemaphoreType.DMA(())],
    )(vec)


def _sum_blocks(allv, n, *, name):
    m = allv.shape[0] // n

    def body(a_ref, o_ref):
        acc = a_ref[0:m, :]
        for d in range(1, n):
            acc = acc + a_ref[d * m:(d + 1) * m, :]
        o_ref[...] = acc

    return pl.pallas_call(body, name=name, out_shape=jax.ShapeDtypeStruct((m, allv.shape[1]), allv.dtype))(allv)


EW_BLOCK_BYTES = 1 << 20


def _ew_rows(rows, w):
    return _tile(rows, max(8, (EW_BLOCK_BYTES // (4 * w)) // 8 * 8), 8)


def _add_pair(g, got, c, *, name):
    _, _, rows, w = g.shape
    tr = _ew_rows(rows, w)

    def body(c_ref, g_ref, got_ref, o_ref):
        o_ref[...] = (g_ref[...] + got_ref[...]).astype(o_ref.dtype)

    blk = pl.BlockSpec((None, tr, w), lambda q, i, c_ref: (q, i, 0))
    return pl.pallas_call(
        body, name=name,
        grid_spec=pltpu.PrefetchScalarGridSpec(
            num_scalar_prefetch=1, grid=(N_CHIPS, rows // tr),
            in_specs=[pl.BlockSpec((None, None, tr, w), lambda q, i, c_ref: (q, c_ref[0], i, 0)), blk], out_specs=blk),
        out_shape=jax.ShapeDtypeStruct(got.shape, BF16),
        compiler_params=_params("parallel", "parallel"),
    )(c, g, got)


def _add_chips(pair, recv, chip, *, name):
    _, rows, w = pair.shape
    tr = _ew_rows(rows, w)

    def body(chip_ref, p_ref, r0_ref, r1_ref, r2_ref, o_ref):
        f = lambda r: r[...].astype(F32)
        o_ref[...] = ((f(p_ref) + f(r0_ref)) + f(r1_ref)) + f(r2_ref)

    blk = pl.BlockSpec((tr, w), lambda i, chip_ref: (i, 0))
    return pl.pallas_call(
        body, name=name,
        grid_spec=pltpu.PrefetchScalarGridSpec(
            num_scalar_prefetch=1, grid=(rows // tr,),
            in_specs=[pl.BlockSpec((None, tr, w), lambda i, chip_ref: (chip_ref[0], i, 0)), blk, blk, blk], out_specs=blk),
        out_shape=jax.ShapeDtypeStruct((rows, w), F32),
        compiler_params=_params("parallel"),
    )(chip, pair, *recv)


def _adamw_math(w, g, m, v):
    nm = ADAM_B1 * m + (1.0 - ADAM_B1) * g
    nv = ADAM_B2 * v + (1.0 - ADAM_B2) * (g * g)
    m_hat = nm / (1.0 - ADAM_B1 ** ADAM_STEP)
    v_hat = nv / (1.0 - ADAM_B2 ** ADAM_STEP)
    return -ADAM_LR * (m_hat / (jnp.sqrt(v_hat) + ADAM_EPS) + ADAM_WD * w), nm, nv


def _adamw(w, g, m, v, *, name):
    shape = w.shape
    last = shape[-1]
    w2, g2, m2, v2 = (a.reshape(-1, last) for a in (w, g, m, v))
    rows = w2.shape[0]
    tm = _ew_rows(rows, last)

    def body(w_ref, g_ref, m_ref, v_ref, d_ref, nm_ref, nv_ref):
        d_ref[...], nm_ref[...], nv_ref[...] = _adamw_math(w_ref[...], g_ref[...], m_ref[...], v_ref[...])

    spec = pl.BlockSpec((tm, last), lambda i: (i, 0))
    out = jax.ShapeDtypeStruct((rows, last), F32)
    d, nm, nv = pl.pallas_call(
        body, name=name, grid=(rows // tm,), in_specs=[spec] * 4, out_specs=[spec] * 3, out_shape=[out] * 3,
        compiler_params=_params("parallel"),
    )(w2, g2, m2, v2)
    return d.reshape(shape), nm.reshape(shape), nv.reshape(shape)


def _adamw_halves(w, m, v, mine, theirs, c, *, name):
    _, rows, wd = w.shape
    tr = _ew_rows(rows, wd)

    def body(c_ref, w_ref, m_ref, v_ref, a_ref, b_ref, g_ref, d_ref, nm_ref, nv_ref):
        g = jnp.where(pl.program_id(0) == c_ref[0], a_ref[...], b_ref[...])
        g_ref[...] = g
        d_ref[...], nm_ref[...], nv_ref[...] = _adamw_math(w_ref[...], g, m_ref[...], v_ref[...])

    full = pl.BlockSpec((None, tr, wd), lambda hf, i, c_ref: (hf, i, 0))
    half = pl.BlockSpec((tr, wd), lambda hf, i, c_ref: (i, 0))
    out = jax.ShapeDtypeStruct(w.shape, F32)
    return pl.pallas_call(
        body, name=name,
        grid_spec=pltpu.PrefetchScalarGridSpec(num_scalar_prefetch=1, grid=(2, rows // tr),
                                               in_specs=[full] * 3 + [half] * 2, out_specs=[full] * 4),
        out_shape=[out] * 4,
        compiler_params=_params("parallel", "parallel"),
    )(c, w, m, v, mine, theirs)


def _join_quarters(q, *, name):
    _, rows, n = q.shape
    tr = _tile(rows, 256, 16)

    def body(q_ref, o_ref):
        o_ref[...] = jnp.concatenate([q_ref[s] for s in range(N_CHIPS)], axis=1)

    return pl.pallas_call(
        body, name=name, grid=(rows // tr,),
        in_specs=[pl.BlockSpec((N_CHIPS, tr, n), lambda i: (0, i, 0))],
        out_specs=pl.BlockSpec((tr, N_CHIPS * n), lambda i: (i, 0)),
        out_shape=jax.ShapeDtypeStruct((rows, N_CHIPS * n), q.dtype),
        compiler_params=_params("parallel"),
    )(q)


def _split_quarters(full, *, name):
    rows, n4 = full.shape
    n = n4 // N_CHIPS
    tr = _tile(rows, 256, 16)

    def body(x_ref, o_ref):
        x = x_ref[...]
        for s in range(N_CHIPS):
            o_ref[s] = x[:, s * n:(s + 1) * n]

    return pl.pallas_call(
        body, name=name, grid=(rows // tr,),
        in_specs=[pl.BlockSpec((tr, n4), lambda i: (i, 0))],
        out_specs=pl.BlockSpec((N_CHIPS, tr, n), lambda i: (0, i, 0)),
        out_shape=jax.ShapeDtypeStruct((N_CHIPS, rows, n), full.dtype),
        compiler_params=_params("parallel"),
    )(full)


_WEIGHTS = ['ffn_norm', 'ffn_w_gate', 'ffn_w_up', 'ffn_w_down', 'mix_norm', 'att_w_in', 'att_q_norm', 'att_k_norm',
            'att_sinks', 'att_w_out', 'gdn_w_in', 'gdn_conv_w', 'gdn_a_log', 'gdn_dt_bias', 'gdn_out_norm', 'gdn_w_out',
            'ple_norm', 'ple_w_gate', 'ple_w_proj']
_BIG = ['ffn_w_gate', 'ffn_w_up', 'ffn_w_down', 'att_w_in', 'att_w_out', 'gdn_w_in', 'gdn_w_out', 'ple_w_gate',
        'ple_w_proj']
_SMALL_CUT = {'ffn_norm': 2, 'gdn_conv_w': 2}
_WHOLE = ['mix_norm', 'att_q_norm', 'att_k_norm', 'att_sinks', 'gdn_a_log', 'gdn_dt_bias', 'gdn_out_norm', 'ple_norm']
PACK_W = 1024
SMALL_ROW_MULT = 8


def _halves(a):
    return a.reshape(2, -1, a.shape[-1])


def _from_quarters(blk, axis):
    full = jnp.moveaxis(blk, 0, axis)
    shp = list(full.shape)
    shp[axis:axis + 2] = [shp[axis] * shp[axis + 1]]
    return full.reshape(shp)


def _to_quarters(full, axis):
    shp = list(full.shape)
    shp[axis:axis + 1] = [N_CHIPS, shp[axis] // N_CHIPS]
    return jnp.moveaxis(full.reshape(shp), axis, 0)


def _pack(parts, row_mult):
    flat = jnp.concatenate(parts, axis=-1)
    n = flat.shape[-1]
    rows = -(-n // (PACK_W * row_mult)) * row_mult
    return jnp.pad(flat, [(0, rows * PACK_W - n)]).reshape(rows, PACK_W)


def _unpack(flat, shapes):
    lead = flat.shape[:-2]
    flat = flat.reshape(lead + (-1,))
    out, off = [], 0
    for shp in shapes:
        n = math.prod(shp)
        out.append(flat[..., off:off + n].reshape(lead + tuple(shp)))
        off += n
    return out


def _ffn_fwd(h, gain, wg, wu, wd, at, tag):
    t = h.shape[0]
    fq = wd.shape[-2]
    lead = (Q,) + at
    hn, = _row_fwd(_f_rms, [h], [gain], [(D_MODEL, BF16)], name=f"{tag}_norm")
    g = _mm(hn, (wg, lead), out_q=True, name=f"{tag}_gate").reshape(N_CHIPS * t, fq)
    u = _mm(hn, (wu, lead), out_q=True, name=f"{tag}_up").reshape(N_CHIPS * t, fq)
    a, = _row_fwd(_f_swiglu, [g, u], [], [(fq, BF16)], name=f"{tag}_act")
    a = a.reshape(N_CHIPS, t, fq)
    out = _mm((a, (Q,)), (wd, lead), res=h, scale=0.5, name=f"{tag}_down")
    return out, (h, hn, g, u, a)


def _ffn_bwd(dout, saved, gain, wg, wu, wd, at, grads, tag):
    h, hn, g, u, a = saved
    t = h.shape[0]
    fq = wd.shape[-2]
    lead = (Q,) + at
    da = _mm(dout, (wd, lead), tb=True, scale=0.5, out_q=True, name=f"{tag}_d_act").reshape(N_CHIPS * t, fq)
    dg, du = _row_bwd(_f_swiglu, [g, u], [], [da], [(0, BF16), (1, BF16)], [], name=f"{tag}_d_gate_up")
    dg, du = dg.reshape(N_CHIPS, t, fq), du.reshape(N_CHIPS, t, fq)
    g_gate, g_up, g_down = grads
    g_down = _mm((a, (Q,)), dout, ta=True, scale=0.5, into=(g_down, lead), name=f"{tag}_dw_down")
    g_gate = _mm(hn, (dg, (Q,)), ta=True, into=(g_gate, lead), name=f"{tag}_dw_gate")
    g_up = _mm(hn, (du, (Q,)), ta=True, into=(g_up, lead), name=f"{tag}_dw_up")
    dhn = _mm((dg, (Q,)), (wg, lead), tb=True, name=f"{tag}_d_norm_gate")
    dhn = _mm((du, (Q,)), (wu, lead), tb=True, res=dhn, name=f"{tag}_d_norm_up")
    dh, dgain = _row_bwd(_f_rms_res, [h], [gain], [dhn, dout], [(0, F32)], [0], name=f"{tag}_d_in")
    return dh, dgain, (g_gate, g_up, g_down)


def _att_fwd(h, gain, w_in, qg, kg, sinks, w_out):
    hn, = _row_fwd(_f_rms, [h], [gain], [(D_MODEL, BF16)], name="att_norm")
    proj = _mm(hn, w_in, name="att_in")
    a, rtot = _sb_fwd(proj, name="att_sb")
    b = _swa_fwd(proj, qg, kg, sinks, name="att_swa")
    out = _mm(a, (w_out, (0,)), res=h, name="att_out_sb")
    out = _mm(b, (w_out, (1,)), res=out, name="att_out_swa")
    return out, (h, hn, proj, a, rtot, b)


def _att_bwd(dout, saved, gain, w_in, qg, kg, sinks, w_out):
    h, hn, proj, a, rtot, b = saved
    da = _mm(dout, (w_out, (0,)), tb=True, name="att_d_sb")
    db = _mm(dout, (w_out, (1,)), tb=True, name="att_d_swa")
    dw_out = jnp.zeros(w_out.shape, F32)
    dw_out = _mm(a, dout, ta=True, into=(dw_out, (0,)), name="att_dw_out_sb")
    dw_out = _mm(b, dout, ta=True, into=(dw_out, (1,)), name="att_dw_out_swa")
    dq, dk, dv = _sb_bwd(proj, rtot, da, name="att_sb_bwd")
    dqb, dkb, dvb, dqg, dkg, dsk = _swa_bwd(proj, qg, kg, sinks, db, name="att_swa_bwd")
    dproj = jnp.concatenate([dq, dk, dv, dqb, dkb, dvb], axis=1)
    dw_in = _mm(hn, dproj, ta=True, name="att_dw_in")
    dhn = _mm(dproj, w_in, tb=True, name="att_d_norm")
    dh, dgain = _row_bwd(_f_rms_res, [h], [gain], [dhn, dout], [(0, F32)], [0], name="att_d_in")
    return dh, dgain, dw_in, dqg, dkg, dsk, dw_out


def _gdn_layer_fwd(h, gain, w_in, conv_w, alog, dtb, out_gain, w_out):
    w_qkv, w_z, w_ba = w_in[:, :GDN_CONV_W], w_in[:, GDN_CONV_W:GDN_CONV_W + GDN_VW], w_in[:, GDN_CONV_W + GDN_VW:]
    hn, = _row_fwd(_f_rms, [h], [gain], [(D_MODEL, BF16)], name="gdn_norm")
    pq = _mm(hn, w_qkv, name="gdn_in_qkv")
    pz = _mm(hn, w_z, name="gdn_in_z")
    ba = _mm(hn, w_ba, name="gdn_in_ba")
    act = _conv_fwd(pq, conv_w, name="gdn_conv")
    o, states = _gdn_fwd(act, ba, alog, dtb, name="gdn_rule")
    y, = _row_fwd(_f_gdn_out, [o, pz], [out_gain], [(GDN_VW, BF16)], name="gdn_gate")
    out = _mm(y, w_out, res=h, name="gdn_out")
    return out, (h, hn, pq, pz, ba, act, o, states, y, (w_qkv, w_z, w_ba))


def _gdn_layer_bwd(dout, saved, gain, conv_w, alog, dtb, out_gain, w_out):
    h, hn, pq, pz, ba, act, o, states, y, (w_qkv, w_z, w_ba) = saved
    dy = _mm(dout, w_out, tb=True, name="gdn_d_gate")
    dw_out = _mm(y, dout, ta=True, name="gdn_dw_out")
    do, dpz, dout_gain = _row_bwd(_f_gdn_out, [o, pz], [out_gain], [dy], [(0, F32), (1, F32)], [0], name="gdn_gate_bwd")
    dq, dk, dv, dba, dal, ddb = _gdn_bwd(act, ba, alog, dtb, states, do, name="gdn_rule_bwd")
    dact = jnp.concatenate([dq, dk, dv], axis=1)
    dc = _conv_fwd(pq, conv_w, dact, name="gdn_conv_d_pre")
    dpq, dconv = _conv_bwd(pq, conv_w, dc, name="gdn_conv_bwd")
    dw_in = jnp.concatenate([_mm(hn, dpq, ta=True, name="gdn_dw_qkv"), _mm(hn, dpz, ta=True, name="gdn_dw_z"),
                             _mm(hn, dba, ta=True, name="gdn_dw_ba")], axis=1)
    dhn = _mm(dpq, w_qkv, tb=True, name="gdn_d_norm_qkv")
    dhn = _mm(dpz, w_z, tb=True, res=dhn, name="gdn_d_norm_z")
    dhn = _mm(dba, w_ba, tb=True, res=dhn, name="gdn_d_norm_ba")
    dh, dgain = _row_bwd(_f_rms_res, [h], [gain], [dhn, dout], [(0, F32)], [0], name="gdn_d_in")
    return dh, dgain, dw_in, dconv, dal, ddb, dout_gain, dw_out


def _ple_fwd(h, gain, w_gate, w_proj, pe, tag):
    hn, = _row_fwd(_f_rms, [h], [gain], [(D_MODEL, BF16)], name=f"{tag}_norm")
    gl = _mm(hn, w_gate, name=f"{tag}_gate")
    pp = _mm(pe, w_proj, name=f"{tag}_proj")
    out, = _row_fwd(_f_ple, [h, gl, pp], [], [(D_MODEL, F32)], name=f"{tag}_mix")
    return out, (h, hn, gl, pp)


def _ple_bwd(dout, saved, gain, w_gate, pe, tag):
    h, hn, gl, pp = saved
    dha, dgl, dpp = _row_bwd(_f_ple, [h, gl, pp], [], [dout], [(0, F32), (1, BF16), (2, BF16)], [], name=f"{tag}_mix_bwd")
    dw_gate = _mm(hn, dgl, ta=True, name=f"{tag}_dw_gate")
    dw_proj = _mm(pe, dpp, ta=True, name=f"{tag}_dw_proj")
    dhn = _mm(dgl, w_gate, tb=True, name=f"{tag}_d_norm")
    dh, dgain = _row_bwd(_f_rms_res, [h], [gain], [dhn, dha], [(0, F32)], [0], name=f"{tag}_d_in")
    return dh, dgain, dw_gate, dw_proj


def kernel(x, p, ffn_norm, ffn_w_gate, ffn_w_up, ffn_w_down, mix_norm, att_w_in, att_q_norm, att_k_norm, att_sinks, att_w_out, gdn_w_in, gdn_conv_w, gdn_a_log, gdn_dt_bias, gdn_out_norm, gdn_w_out, ple_norm, ple_w_gate, ple_w_proj, loss_target, m_ffn_norm, m_ffn_w_gate, m_ffn_w_up, m_ffn_w_down, m_mix_norm, m_att_w_in, m_att_q_norm, m_att_k_norm, m_att_sinks, m_att_w_out, m_gdn_w_in, m_gdn_conv_w, m_gdn_a_log, m_gdn_dt_bias, m_gdn_out_norm, m_gdn_w_out, m_ple_norm, m_ple_w_gate, m_ple_w_proj, v_ffn_norm, v_ffn_w_gate, v_ffn_w_up, v_ffn_w_down, v_mix_norm, v_att_w_in, v_att_q_norm, v_att_k_norm, v_att_sinks, v_att_w_out, v_gdn_w_in, v_gdn_conv_w, v_gdn_a_log, v_gdn_dt_bias, v_gdn_out_norm, v_gdn_w_out, v_ple_norm, v_ple_w_gate, v_ple_w_proj):
    arg = dict(locals())
    cx, cy, cc = _coords()
    chip = (2 * cx + cy).astype(jnp.int32).reshape(1)
    core = cc.astype(jnp.int32).reshape(1)
    n_layers = ffn_norm.shape[0]

    gathered = _gather_quarters([_halves(arg[n].astype(BF16)) for n in _BIG], name="gather_weights")
    wq = {n: g.reshape((N_CHIPS,) + arg[n].shape) for n, g in zip(_BIG, gathered)}
    wt = {}
    wt['att_w_in'] = _join_quarters(wq['att_w_in'][:, 0], name="att_w_in_join")
    wt['gdn_w_in'] = _join_quarters(wq['gdn_w_in'][:, 0], name="gdn_w_in_join")
    wt['att_w_out'] = wq['att_w_out'].reshape(2, SB_W, D_MODEL)
    wt['gdn_w_out'] = wq['gdn_w_out'].reshape(GDN_VW, D_MODEL)
    wt['ple_w_gate'] = _from_quarters(wq['ple_w_gate'], 1)
    wt['ple_w_proj'] = _from_quarters(wq['ple_w_proj'], 2)

    small_names = list(_SMALL_CUT)
    small_shapes = [arg[n].shape for n in small_names]
    svec = _pack([arg[n].reshape(-1) for n in small_names], SMALL_ROW_MULT)
    srows = svec.shape[0]
    sall = _gather_all(svec, name="gather_gains").reshape(N_CHIPS, 2, srows, PACK_W)[:, 0]
    for n, q in zip(small_names, _unpack(sall, small_shapes)):
        wt[n] = _from_quarters(q, _SMALL_CUT[n])
    row = lambda v: v.reshape(1, -1)

    h = x[0]
    tape = []
    ffn_w = (wq['ffn_w_gate'], wq['ffn_w_up'], wq['ffn_w_down'])
    for i in range(n_layers):
        j = i // 2
        h, s0 = _ffn_fwd(h, row(wt['ffn_norm'][i, 0]), *ffn_w, (i, 0), f"ffn{i}a")
        if i % 2 == 0:
            h, sm = _att_fwd(h, row(mix_norm[i]), wt['att_w_in'], att_q_norm[j:j + 1], att_k_norm[j:j + 1],
                             att_sinks[j:j + 1], wt['att_w_out'])
        else:
            h, sm = _gdn_layer_fwd(h, row(mix_norm[i]), wt['gdn_w_in'], wt['gdn_conv_w'][j], gdn_a_log[j:j + 1],
                                   gdn_dt_bias[j:j + 1], gdn_out_norm[j:j + 1], wt['gdn_w_out'])
        h, s1 = _ffn_fwd(h, row(wt['ffn_norm'][i, 1]), *ffn_w, (i, 1), f"ffn{i}b")
        h, sp = _ple_fwd(h, row(ple_norm[i]), wt['ple_w_gate'][i], wt['ple_w_proj'][i], p[i, 0], f"ple{i}")
        tape.append((s0, sm, s1, sp))

    dh, loss_local = _loss_head(h, loss_target[0], name="loss_head")
    loss = lax.psum(loss_local, ("x", "y", "c"))

    gr = {}
    ffn_g = tuple(jnp.zeros(w.shape, F32) for w in ffn_w)
    d_ffn_norm = [[None, None] for _ in range(n_layers)]
    d_mix, d_ple_norm, d_ple_gate, d_ple_proj = [None] * n_layers, [None] * n_layers, [None] * n_layers, [None] * n_layers
    for i in reversed(range(n_layers)):
        j = i // 2
        s0, sm, s1, sp = tape[i]
        dh, d_ple_norm[i], d_ple_gate[i], d_ple_proj[i] = _ple_bwd(dh, sp, row(ple_norm[i]), wt['ple_w_gate'][i], p[i, 0],
                                                                   f"ple{i}")
        dh, d_ffn_norm[i][1], ffn_g = _ffn_bwd(dh, s1, row(wt['ffn_norm'][i, 1]), *ffn_w, (i, 1), ffn_g, f"ffn{i}b")
        if i % 2 == 0:
            (dh, d_mix[i], dw_in, gr['att_q_norm'], gr['att_k_norm'], gr['att_sinks'],
             dw_out) = _att_bwd(dh, sm, row(mix_norm[i]), wt['att_w_in'], att_q_norm[j:j + 1],
                                att_k_norm[j:j + 1], att_sinks[j:j + 1], wt['att_w_out'])
            gr['att_w_in'] = _split_quarters(dw_in, name="att_dw_in_split")
            gr['att_w_out'] = dw_out
        else:
            (dh, d_mix[i], dw_in, dconv, gr['gdn_a_log'], gr['gdn_dt_bias'], gr['gdn_out_norm'],
             dw_out) = _gdn_layer_bwd(dh, sm, row(mix_norm[i]), wt['gdn_conv_w'][j], gdn_a_log[j:j + 1],
                                      gdn_dt_bias[j:j + 1], gdn_out_norm[j:j + 1], wt['gdn_w_out'])
            gr['gdn_w_in'] = _split_quarters(dw_in, name="gdn_dw_in_split")
            gr['gdn_w_out'] = dw_out
            gr['gdn_conv_w'] = dconv[None]
        dh, d_ffn_norm[i][0], ffn_g = _ffn_bwd(dh, s0, row(wt['ffn_norm'][i, 0]), *ffn_w, (i, 0), ffn_g, f"ffn{i}a")
    grad_x = dh[None]

    gr['ffn_w_gate'], gr['ffn_w_up'], gr['ffn_w_down'] = ffn_g
    gr['ple_w_gate'] = _to_quarters(jnp.stack(d_ple_gate), 1)
    gr['ple_w_proj'] = _to_quarters(jnp.stack(d_ple_proj), 2)
    gr['ffn_norm'] = jnp.stack([jnp.stack([d_ffn_norm[i][k][0] for k in range(2)]) for i in range(n_layers)])
    gr['mix_norm'] = jnp.concatenate(d_mix, axis=0)
    gr['ple_norm'] = jnp.concatenate(d_ple_norm, axis=0)

    gq = [gr[n].reshape((N_CHIPS, 2, -1, arg[n].shape[-1])) for n in _BIG]
    got = _swap_halves(gq, name="grad_swap_halves")
    pairs = [_add_pair(g, o, core, name=f"grad_add_pair_{n}") for n, g, o in zip(_BIG, gq, got)]
    recv = _scatter_quarters(pairs, name="grad_scatter")
    tots = [_add_chips(pr, rc, chip, name=f"grad_add_chips_{n}") for n, pr, rc in zip(_BIG, pairs, recv)]
    theirs = _share_halves(tots, name="grad_share")

    whole_shapes = [arg[n].shape for n in _WHOLE]
    cut_full_shapes = [gr[n].shape for n in small_names]
    gvec = _pack([gr[n].reshape(-1) for n in _WHOLE + small_names], SMALL_ROW_MULT)
    gall = _sum_blocks(_gather_all(gvec, name="gather_small_grads"), N_DEV, name="sum_small_grads")
    parts = _unpack(gall, whole_shapes + cut_full_shapes)
    gsum = dict(zip(_WHOLE, parts))
    for n, g in zip(small_names, parts[len(_WHOLE):]):
        gsum[n] = lax.dynamic_index_in_dim(_to_quarters(g, _SMALL_CUT[n]), chip[0], axis=0, keepdims=False)

    delta, new_m, new_v = {}, {}, {}
    for n, mine, other in zip(_BIG, tots, theirs):
        res = _adamw_halves(_halves(arg[n]), _halves(arg["m_" + n]), _halves(arg["v_" + n]), mine, other, core,
                            name=f"adamw_{n}")
        gsum[n], delta[n], new_m[n], new_v[n] = (r.reshape(arg[n].shape) for r in res)
    for n in _WHOLE + small_names:
        delta[n], new_m[n], new_v[n] = _adamw(arg[n], gsum[n], arg["m_" + n], arg["v_" + n], name=f"adamw_{n}")
    return (loss, grad_x, *[gsum[n] for n in _WEIGHTS], *[delta[n] for n in _WEIGHTS],
            *[new_m[n] for n in _WEIGHTS], *[new_v[n] for n in _WEIGHTS])
```

```python
import functools
import math

import jax
import jax.numpy as jnp
from jax import lax
from jax.experimental import pallas as pl
from jax.experimental.pallas import tpu as pltpu

F32 = jnp.float32
BF16 = jnp.bfloat16
MESH = pl.DeviceIdType.MESH

LANES = 128
VMEM_LIMIT_BYTES = 56 * 1024 * 1024

EPS = 1e-6
D_MODEL = 1024
HEAD_DIM = 64
SB_HEADS = 8
SWA_HEADS = 8
SWA_KV_HEADS = 2
WINDOW = 128
GDN_K_HEADS = 8
GDN_V_HEADS = 16
GDN_HEAD_DIM = 128
GDN_CONV = 4
GDN_CHUNK = 64
SB_W = SB_HEADS * HEAD_DIM
SWA_QW = SWA_HEADS * HEAD_DIM
SWA_KVW = SWA_KV_HEADS * HEAD_DIM
GDN_KW = GDN_K_HEADS * GDN_HEAD_DIM
GDN_VW = GDN_V_HEADS * GDN_HEAD_DIM
GDN_CONV_W = 2 * GDN_KW + GDN_VW

ADAM_LR = 0.001
ADAM_B1 = 0.9
ADAM_B2 = 0.999
ADAM_EPS = 1e-08
ADAM_WD = 0.01
ADAM_STEP = 10

NEG = -1e30


def _params(*sem):
    return pltpu.CompilerParams(dimension_semantics=sem or None, vmem_limit_bytes=VMEM_LIMIT_BYTES)


def _tile(n, cap, align=LANES):
    if n <= cap:
        return n
    for t in range(cap - cap % align, 0, -align):
        if n % t == 0:
            return t
    return n


N_CHIPS = 4
Q = "q"


def _opnd(x):
    return x if isinstance(x, tuple) else (x, ())


def _mm(a, b, *, name, ta=False, tb=False, out_dtype=F32, res=None, scale=1.0, out_q=False, into=None,
        tm=512, tn=1024, tk=1024):
    (a_arr, a_lead), (b_arr, b_lead) = _opnd(a), _opnd(b)
    (k_a, m) = a_arr.shape[-2:] if ta else a_arr.shape[-2:][::-1]
    (n, k_b) = b_arr.shape[-2:] if tb else b_arr.shape[-2:][::-1]
    if into is not None:
        out_arr, out_lead = into
        out_q, out_dtype = Q in out_lead, out_arr.dtype
    else:
        out_lead = (Q,) if out_q else ()
    red_q = (Q in a_lead or Q in b_lead) and not out_q
    kq = min(k_a, k_b)
    assert (k_a == k_b) or (red_q and max(k_a, k_b) == N_CHIPS * kq), (a_arr.shape, b_arr.shape)
    tm, tn, tk = _tile(m, tm), _tile(n, tn), _tile(kq, tk)
    nk = kq // tk
    ksteps = nk * (N_CHIPS if red_q else 1)
    dims = (((0 if ta else 1,), (1 if tb else 0,)), ((), ()))
    has_res = res is not None

    def body(*refs):
        a_ref, b_ref = refs[0], refs[1]
        o_ref, acc_ref = refs[-2], refs[-1]
        k = pl.program_id(3)

        @pl.when(k == 0)
        def _():
            acc_ref[...] = jnp.zeros_like(acc_ref)

        acc_ref[...] += lax.dot_general(a_ref[...].astype(BF16), b_ref[...].astype(BF16), dims,
                                        preferred_element_type=F32)

        @pl.when(k == ksteps - 1)
        def _():
            r = acc_ref[...]
            if scale != 1.0:
                r = r * scale
            if has_res:
                r = r + refs[2][...].astype(F32)
            o_ref[...] = r.astype(o_ref.dtype)

    def spec(lead, blk, pos):
        def index(s, i, j, k):
            kk = k % nk if (red_q and Q in lead) else k
            quarter = s if out_q else k // nk
            return tuple(quarter if l == Q else l for l in lead) + pos(i, j, kk)
        return pl.BlockSpec((None,) * len(lead) + blk, index)

    a_spec = spec(a_lead, (tk, tm), lambda i, j, k: (k, i)) if ta else spec(a_lead, (tm, tk), lambda i, j, k: (i, k))
    b_spec = spec(b_lead, (tn, tk), lambda i, j, k: (j, k)) if tb else spec(b_lead, (tk, tn), lambda i, j, k: (k, j))
    o_spec = spec(out_lead, (tm, tn), lambda i, j, k: (i, j))
    in_specs, args = [a_spec, b_spec], [a_arr, b_arr]
    if has_res:
        r_arr, r_lead = _opnd(res)
        in_specs.append(spec(r_lead, (tm, tn), lambda i, j, k: (i, j)))
        args.append(r_arr)
    aliases = {}
    if into is not None:
        in_specs.append(pl.BlockSpec(memory_space=pl.ANY))
        args.append(out_arr)
        aliases = {len(args) - 1: 0}
        out_shape = jax.ShapeDtypeStruct(out_arr.shape, out_arr.dtype)
    else:
        out_shape = jax.ShapeDtypeStruct(((N_CHIPS,) if out_q else ()) + (m, n), out_dtype)
    return pl.pallas_call(
        body, name=name, grid=(N_CHIPS if out_q else 1, m // tm, n // tn, ksteps), in_specs=in_specs, out_specs=o_spec,
        out_shape=out_shape, scratch_shapes=[pltpu.VMEM((tm, tn), F32)], input_output_aliases=aliases,
        compiler_params=_params("parallel", "parallel", "parallel", "arbitrary"),
    )(*args)


def _row_spec(r, tm):
    if isinstance(r, tuple):
        arr, width, cb = r
        return arr, pl.BlockSpec((tm, width), lambda i, cb=cb: (i, cb))
    return r, pl.BlockSpec((tm, r.shape[1]), lambda i: (i, 0))


def _const_spec(c):
    return pl.BlockSpec(c.shape, lambda i: (0,) * c.ndim)


def _row_fwd(fn, rows, consts, outs, *, name, tm=256):
    tm = _tile(_row_spec(rows[0], tm)[0].shape[0], tm, 8)
    arrs, specs = zip(*[_row_spec(r, tm) for r in rows])
    t = arrs[0].shape[0]
    nr, nc = len(rows), len(consts)

    def body(*refs):
        vals = [r[...].astype(F32) for r in refs[:nr + nc]]
        res = fn(*vals)
        for o_ref, v in zip(refs[nr + nc:], res):
            o_ref[...] = v.astype(o_ref.dtype)

    out = pl.pallas_call(
        body, name=name, grid=(t // tm,),
        in_specs=list(specs) + [_const_spec(c) for c in consts],
        out_specs=[pl.BlockSpec((tm, w), lambda i: (i, 0)) for w, _ in outs],
        out_shape=[jax.ShapeDtypeStruct((t, w), dt) for w, dt in outs],
        compiler_params=_params("parallel"),
    )(*arrs, *consts)
    return list(out)


def _row_bwd(fn, rows, consts, cts, row_grads, const_grads, *, name, tm=256):
    tm = _tile(_row_spec(rows[0], tm)[0].shape[0], tm, 8)
    arrs, specs = zip(*[_row_spec(r, tm) for r in rows])
    ct_arrs, ct_specs = zip(*[_row_spec(r, tm) for r in cts])
    t = arrs[0].shape[0]
    nr, nc, nt = len(rows), len(consts), len(cts)
    n_in = nr + nc + nt

    def body(*refs):
        vals = [r[...].astype(F32) for r in refs[:nr + nc]]
        ctv = tuple(r[...].astype(F32) for r in refs[nr + nc:n_in])
        _, vjp = jax.vjp(fn, *vals)
        g = vjp(ctv)
        outs = refs[n_in:]
        for (idx, _), o_ref in zip(row_grads, outs[:len(row_grads)]):
            o_ref[...] = g[idx].astype(o_ref.dtype)
        first = pl.program_id(0) == 0
        for ci, o_ref in zip(const_grads, outs[len(row_grads):]):
            @pl.when(first)
            def _(o_ref=o_ref):
                o_ref[...] = jnp.zeros_like(o_ref)

            o_ref[...] += g[nr + ci]

    widths = [(_row_spec(rows[idx], tm)[1].block_shape[1], dt) for idx, dt in row_grads]
    out = pl.pallas_call(
        body, name=name, grid=(t // tm,),
        in_specs=list(specs) + [_const_spec(c) for c in consts] + list(ct_specs),
        out_specs=[pl.BlockSpec((tm, w), lambda i: (i, 0)) for w, _ in widths]
        + [_const_spec(consts[ci]) for ci in const_grads],
        out_shape=[jax.ShapeDtypeStruct((t, w), dt) for w, dt in widths]
        + [jax.ShapeDtypeStruct(consts[ci].shape, F32) for ci in const_grads],
        compiler_params=_params("arbitrary"),
    )(*arrs, *consts, *ct_arrs)
    return list(out)


def _rms(x, g):
    return x * lax.rsqrt(jnp.mean(x * x, axis=-1, keepdims=True) + EPS) * g


def _f_rms(h, g):
    return (_rms(h, g),)


def _f_rms_res(h, g):
    return (_rms(h, g), h)


def _f_swiglu(g, u):
    return (g * jax.nn.sigmoid(g) * u,)


def _f_ple(h, gl, pp):
    return (h + jax.nn.sigmoid(gl) * pp,)


def _f_gdn_out(o, z, gain):
    outs = []
    for hd in range(GDN_V_HEADS):
        sl = slice(hd * GDN_HEAD_DIM, (hd + 1) * GDN_HEAD_DIM)
        oh, zh = o[:, sl], z[:, sl]
        outs.append(_rms(oh, gain) * (zh * jax.nn.sigmoid(zh)))
    return (jnp.concatenate(outs, axis=1),)


def _loss_head(y, target, *, name, tm=512):
    t, d = y.shape
    tm = _tile(t, tm, 8)

    def body(y_ref, t_ref, dy_ref, l_ref):
        @pl.when(pl.program_id(0) == 0)
        def _():
            l_ref[...] = jnp.zeros_like(l_ref)

        e = y_ref[...] - t_ref[...]
        dy_ref[...] = e * (1.0 / d)
        l_ref[...] += jnp.sum(e * e) * (0.5 / d)

    dy, l = pl.pallas_call(
        body, name=name, grid=(t // tm,),
        in_specs=[pl.BlockSpec((tm, d), lambda i: (i, 0))] * 2,
        out_specs=[pl.BlockSpec((tm, d), lambda i: (i, 0)), pl.BlockSpec((8, LANES), lambda i: (0, 0))],
        out_shape=[jax.ShapeDtypeStruct((t, d), F32), jax.ShapeDtypeStruct((8, LANES), F32)],
        compiler_params=_params("arbitrary"),
    )(y, target)
    return dy, l[0, 0]


def _dg(a, b, ca, cb):
    nb = a.ndim - 2
    batch = tuple(range(nb))
    return lax.dot_general(a, b, (((ca + nb,), (cb + nb,)), (batch, batch)), preferred_element_type=F32)


def _b(x):
    return x.astype(BF16)


@jax.custom_vjp
def _bdot(a, b):
    return _dg(_b(a), _b(b), 1, 0)


def _bdot_fwd(a, b):
    return _bdot(a, b), (a, b)


def _bdot_bwd(r, ct):
    a, b = r
    return _dg(_b(ct), _b(b), 1, 1), _dg(_b(a), _b(ct), 0, 0)


_bdot.defvjp(_bdot_fwd, _bdot_bwd)


@jax.custom_vjp
def _bdot_nt(a, b):
    return _dg(_b(a), _b(b), 1, 1)


def _bdot_nt_fwd(a, b):
    return _bdot_nt(a, b), (a, b)


def _bdot_nt_bwd(r, ct):
    a, b = r
    return _dg(_b(ct), _b(b), 1, 0), _dg(_b(ct), _b(a), 0, 0)


_bdot_nt.defvjp(_bdot_nt_fwd, _bdot_nt_bwd)


@jax.custom_vjp
def _bdot_tn(a, b):
    return _dg(_b(a), _b(b), 0, 0)


def _bdot_tn_fwd(a, b):
    return _bdot_tn(a, b), (a, b)


def _bdot_tn_bwd(r, ct):
    a, b = r
    return _dg(_b(b), _b(ct), 1, 1), _dg(_b(a), _b(ct), 1, 0)


_bdot_tn.defvjp(_bdot_tn_fwd, _bdot_tn_bwd)


def _two(x):
    hi = x.astype(BF16)
    return hi, (x - hi.astype(F32)).astype(BF16)


def _dg3(a, b, ca, cb):
    (ah, al), (bh, bl) = _two(a), _two(b)
    return _dg(ah, bh, ca, cb) + (_dg(ah, bl, ca, cb) + _dg(al, bh, ca, cb))


@jax.custom_vjp
def _hdot(a, b):
    return _dg3(a, b, 1, 0)


def _hdot_fwd(a, b):
    return _hdot(a, b), (a, b)


def _hdot_bwd(r, ct):
    a, b = r
    return _dg3(ct, b, 1, 1), _dg3(a, ct, 0, 0)


_hdot.defvjp(_hdot_fwd, _hdot_bwd)


def _split_dot(x, u):
    hi, lo = _two(x)
    return _dg(hi, u, 1, 0) + _dg(lo, u, 1, 0)


@jax.custom_vjp
def _ldot(l01, x):
    hi, lo = _two(x)
    l01 = l01.astype(BF16)
    return _dg(l01, hi, 1, 0) + _dg(l01, lo, 1, 0)


def _ldot_fwd(l01, x):
    return _ldot(l01, x), l01


def _ldot_bwd(l01, ct):
    hi, lo = _two(ct)
    l01b = l01.astype(BF16)
    return jnp.zeros_like(l01), _dg(l01b, hi, 0, 0) + _dg(l01b, lo, 0, 0)


_ldot.defvjp(_ldot_fwd, _ldot_bwd)


SB_BLK = 128
SB_KEYS = 512
SB_PAIRS = 2
SB_SCALE = HEAD_DIM ** -0.5


def _sb_consts(t):
    kb = min(SB_KEYS, t)
    nh = 2 * SB_PAIRS
    lane = lax.broadcasted_iota(jnp.int32, (nh, SB_BLK, kb), 2)
    row = lax.broadcasted_iota(jnp.int32, (nh, SB_BLK, kb), 1)
    ur = lax.broadcasted_iota(jnp.int32, (kb, kb), 0)
    uc = lax.broadcasted_iota(jnp.int32, (kb, kb), 1)
    return kb, nh, lane, row, ur, uc


def _sb_heads(x):
    head0 = lax.broadcasted_iota(jnp.int32, (x.shape[0], LANES), 1) < HEAD_DIM
    out = []
    for p in range(SB_PAIRS):
        blk = x[:, p * LANES:(p + 1) * LANES]
        out += [jnp.where(head0, blk, 0.0), jnp.where(head0, 0.0, blk)]
    return jnp.stack(out)


def _sb_pairs(x):
    return jnp.stack([x[:, (h // 2) * LANES:(h // 2 + 1) * LANES] for h in range(2 * SB_PAIRS)])


def _sb_merge(x):
    head0 = lax.broadcasted_iota(jnp.int32, (x.shape[1], LANES), 1) < HEAD_DIM
    return jnp.concatenate([jnp.where(head0, x[2 * p], x[2 * p + 1]) for p in range(SB_PAIRS)], axis=1)


def _sb_rows_dot(x, u):
    nh, rows, k = x.shape
    return _split_dot(x.reshape(nh * rows, k), u).reshape(nh, rows, k)


def _sb_fwd(proj, *, name):
    t = proj.shape[0]
    nb = t // SB_BLK
    width = SB_PAIRS * LANES
    ng = SB_W // width

    def body(q_ref, k_ref, v_ref, o_ref, r_ref):
        i = pl.program_id(1)
        kb, nh, lane, row, ur, uc = _sb_consts(t)
        u_suffix = (ur >= uc).astype(BF16)
        qh = _b(_sb_heads(q_ref[...]))
        diag = (i * SB_BLK) // kb

        def block(j, carry, masked):
            acc, car = carry
            keys = pl.ds(pl.multiple_of(j * kb, kb), kb)
            kj, vj = _b(_sb_pairs(k_ref[keys, :])), _b(_sb_pairs(v_ref[keys, :]))
            z = _dg(qh, kj, 1, 1) * SB_SCALE
            ls = jax.nn.log_sigmoid(z)
            lk = ls - z
            if masked:
                causal = (j * kb + lane) < (i * SB_BLK + row)
                lk = jnp.where(causal, lk, 0.0)
            suf = _sb_rows_dot(lk, u_suffix) + car
            w = jnp.exp(ls + (suf - lk))
            if masked:
                w = jnp.where(causal, w, 0.0)
            return acc + _dg(_b(w), vj, 1, 0), suf[:, :, 0:1]

        zero = (jnp.zeros((nh, SB_BLK, LANES), F32), jnp.zeros((nh, SB_BLK, 1), F32))
        carry = block(diag, zero, True)
        acc, car = lax.fori_loop(0, diag, lambda s, c: block(diag - 1 - s, c, False), carry)
        o_ref[...] = _sb_merge(acc)
        r_ref[...] = _sb_merge(jnp.broadcast_to(car, (nh, SB_BLK, LANES)))

    return pl.pallas_call(
        body, name=name, grid=(ng, nb),
        in_specs=[pl.BlockSpec((SB_BLK, width), lambda p, i: (i, p)),
                  pl.BlockSpec((t, width), lambda p, i: (0, ng + p)),
                  pl.BlockSpec((t, width), lambda p, i: (0, 2 * ng + p))],
        out_specs=[pl.BlockSpec((SB_BLK, width), lambda p, i: (i, p))] * 2,
        out_shape=[jax.ShapeDtypeStruct((t, SB_W), F32)] * 2,
        compiler_params=_params("parallel", "arbitrary"),
    )(proj, proj, proj)


def _sb_bwd(proj, rtot, dout, *, name):
    t = proj.shape[0]
    nb = t // SB_BLK
    width = SB_PAIRS * LANES
    ng = SB_W // width

    def body(q_ref, k_ref, v_ref, r_ref, do_ref, dq_ref, dk_ref, dv_ref):
        i = pl.program_id(1)
        kb, nh, lane, row, ur, uc = _sb_consts(t)
        u_incl = (ur <= uc).astype(BF16)
        u_excl = (ur < uc).astype(BF16)
        q, do = q_ref[...], do_ref[...]
        qh, doh = _b(_sb_heads(q)), _b(_sb_heads(do))
        qb, dob = _b(_sb_pairs(q)), _b(_sb_pairs(do))
        rh = jnp.min(_sb_heads(r_ref[...]), axis=2, keepdims=True)
        diag = (i * SB_BLK) // kb

        @pl.when(i == 0)
        def _():
            dk_ref[...] = jnp.zeros_like(dk_ref)
            dv_ref[...] = jnp.zeros_like(dv_ref)

        def block(j, carry, masked):
            dq_acc, clk, ce = carry
            keys = pl.ds(pl.multiple_of(j * kb, kb), kb)
            kj, vj = _b(_sb_pairs(k_ref[keys, :])), _b(_sb_pairs(v_ref[keys, :]))
            z = _dg(qh, kj, 1, 1) * SB_SCALE
            ls = jax.nn.log_sigmoid(z)
            lk = ls - z
            if masked:
                causal = (j * kb + lane) < (i * SB_BLK + row)
                lk = jnp.where(causal, lk, 0.0)
            pre = _sb_rows_dot(lk, u_incl) + clk
            w = jnp.exp(ls + (rh - pre))
            if masked:
                w = jnp.where(causal, w, 0.0)
            e = _dg(doh, vj, 1, 1) * w
            pre_e = _sb_rows_dot(e, u_excl) + ce
            sig = jnp.exp(ls)
            dz = (e * (1.0 - sig) - sig * pre_e) * SB_SCALE
            if masked:
                dz = jnp.where(causal, dz, 0.0)
            dzb = _b(dz)
            dk_ref[keys, :] += _sb_merge(_dg(dzb, qb, 0, 0))
            dv_ref[keys, :] += _sb_merge(_dg(_b(w), dob, 0, 0))
            return dq_acc + _dg(dzb, kj, 1, 0), pre[:, :, kb - 1:], pre_e[:, :, kb - 1:] + e[:, :, kb - 1:]

        zero = (jnp.zeros((nh, SB_BLK, LANES), F32), jnp.zeros((nh, SB_BLK, 1), F32), jnp.zeros((nh, SB_BLK, 1), F32))
        carry = lax.fori_loop(0, diag, lambda j, c: block(j, c, False), zero)
        dq_acc, _, _ = block(diag, carry, True)
        dq_ref[...] = _sb_merge(dq_acc)

    blk = pl.BlockSpec((SB_BLK, width), lambda p, i: (i, p))
    whole = pl.BlockSpec((t, width), lambda p, i: (0, p))
    return pl.pallas_call(
        body, name=name, grid=(ng, nb),
        in_specs=[blk,
                  pl.BlockSpec((t, width), lambda p, i: (0, ng + p)),
                  pl.BlockSpec((t, width), lambda p, i: (0, 2 * ng + p)),
                  blk, blk],
        out_specs=[blk, whole, whole],
        out_shape=[jax.ShapeDtypeStruct((t, SB_W), F32)] * 3,
        compiler_params=_params("arbitrary", "arbitrary"),
    )(proj, proj, proj, rtot, dout)


SWA_G = SWA_HEADS // SWA_KV_HEADS


def _swa_heads(first, qs, ks, vs, qg, kg, sinks):
    qi = lax.broadcasted_iota(jnp.int32, (WINDOW, 2 * WINDOW), 0)
    kj = lax.broadcasted_iota(jnp.int32, (WINDOW, 2 * WINDOW), 1)
    dist = qi + WINDOW - kj
    valid = (dist >= 0) & (dist < WINDOW) & (jnp.logical_not(first) | (kj >= WINDOW))
    distf = dist.astype(F32)
    outs = []
    for hk in range(SWA_KV_HEADS):
        kn = _rms(ks[hk], kg)
        for g in range(SWA_G):
            h = hk * SWA_G + g
            slope = 2.0 ** (-8.0 * (h + 1) / SWA_HEADS)
            s = _bdot_nt(_rms(qs[h], qg), kn) * (HEAD_DIM ** -0.5)
            s = jnp.where(valid, s - slope * distf, NEG)
            m = lax.stop_gradient(jnp.maximum(jnp.max(s, axis=1, keepdims=True), sinks[h]))
            p = jnp.exp(s - m)
            den = jnp.sum(p, axis=1, keepdims=True) + jnp.exp(sinks[h] - m)
            outs.append(_bdot(p / den, vs[hk]))
    return tuple(outs)


def _swa_split(q, kp, kc, vp, vc, sk):
    qs = [q[:, h * HEAD_DIM:(h + 1) * HEAD_DIM] for h in range(SWA_HEADS)]
    k2, v2 = jnp.concatenate([kp, kc], axis=0), jnp.concatenate([vp, vc], axis=0)
    ks = [k2[:, h * HEAD_DIM:(h + 1) * HEAD_DIM] for h in range(SWA_KV_HEADS)]
    vs = [v2[:, h * HEAD_DIM:(h + 1) * HEAD_DIM] for h in range(SWA_KV_HEADS)]
    sinks = [sk[:, h:h + 1] for h in range(SWA_HEADS)]
    return qs, ks, vs, sinks


def _swa_specs(t):
    qcb = (3 * SB_W) // SWA_QW
    kcb = (3 * SB_W + SWA_QW) // SWA_KVW
    prev = lambda i: jnp.maximum(i - 1, 0)
    return [pl.BlockSpec((WINDOW, SWA_QW), lambda i: (i, qcb)),
            pl.BlockSpec((WINDOW, SWA_KVW), lambda i: (prev(i), kcb)),
            pl.BlockSpec((WINDOW, SWA_KVW), lambda i: (i, kcb)),
            pl.BlockSpec((WINDOW, SWA_KVW), lambda i: (prev(i), kcb + 1)),
            pl.BlockSpec((WINDOW, SWA_KVW), lambda i: (i, kcb + 1)),
            pl.BlockSpec((1, HEAD_DIM), lambda i: (0, 0)),
            pl.BlockSpec((1, HEAD_DIM), lambda i: (0, 0)),
            pl.BlockSpec((1, SWA_HEADS), lambda i: (0, 0))]


def _swa_fwd(proj, qg, kg, sinks, *, name):
    t = proj.shape[0]

    def body(q_ref, kp_ref, kc_ref, vp_ref, vc_ref, qg_ref, kg_ref, sk_ref, o_ref):
        first = pl.program_id(0) == 0
        qs, ks, vs, sk = _swa_split(q_ref[...], kp_ref[...], kc_ref[...], vp_ref[...], vc_ref[...], sk_ref[...])
        o_ref[...] = jnp.concatenate(_swa_heads(first, qs, ks, vs, qg_ref[...], kg_ref[...], sk), axis=1)

    return pl.pallas_call(
        body, name=name, grid=(t // WINDOW,), in_specs=_swa_specs(t),
        out_specs=pl.BlockSpec((WINDOW, SWA_QW), lambda i: (i, 0)),
        out_shape=jax.ShapeDtypeStruct((t, SWA_QW), F32),
        compiler_params=_params("parallel"),
    )(proj, proj, proj, proj, proj, qg, kg, sinks)


def _swa_bwd(proj, qg, kg, sinks, dout, *, name):
    t = proj.shape[0]

    def body(q_ref, kp_ref, kc_ref, vp_ref, vc_ref, qg_ref, kg_ref, sk_ref, do_ref,
             dq_ref, dk_ref, dv_ref, dqg_ref, dkg_ref, dsk_ref):
        i = pl.program_id(0)
        first = i == 0

        @pl.when(first)
        def _():
            for r in (dk_ref, dv_ref, dqg_ref, dkg_ref, dsk_ref):
                r[...] = jnp.zeros_like(r)

        qs, ks, vs, sk = _swa_split(q_ref[...], kp_ref[...], kc_ref[...], vp_ref[...], vc_ref[...], sk_ref[...])
        do = do_ref[...]
        cts = tuple(do[:, h * HEAD_DIM:(h + 1) * HEAD_DIM] for h in range(SWA_HEADS))
        _, vjp = jax.vjp(functools.partial(_swa_heads, first), qs, ks, vs, qg_ref[...], kg_ref[...], sk)
        dqs, dks, dvs, dqg, dkg, dsk = vjp(cts)
        dq_ref[...] = jnp.concatenate(dqs, axis=1)
        dk2, dv2 = jnp.concatenate(dks, axis=1), jnp.concatenate(dvs, axis=1)
        cur = pl.ds(pl.multiple_of(i * WINDOW, WINDOW), WINDOW)
        prv = pl.ds(pl.multiple_of(jnp.maximum(i - 1, 0) * WINDOW, WINDOW), WINDOW)
        dk_ref[prv, :] += dk2[:WINDOW]
        dv_ref[prv, :] += dv2[:WINDOW]
        dk_ref[cur, :] += dk2[WINDOW:]
        dv_ref[cur, :] += dv2[WINDOW:]
        dqg_ref[...] += dqg
        dkg_ref[...] += dkg
        dsk_ref[...] += jnp.concatenate(dsk, axis=1)

    whole = lambda shape: pl.BlockSpec(shape, lambda i: (0, 0))
    return pl.pallas_call(
        body, name=name, grid=(t // WINDOW,),
        in_specs=_swa_specs(t) + [pl.BlockSpec((WINDOW, SWA_QW), lambda i: (i, 0))],
        out_specs=[pl.BlockSpec((WINDOW, SWA_QW), lambda i: (i, 0)), whole((t, SWA_KVW)), whole((t, SWA_KVW)),
                   whole((1, HEAD_DIM)), whole((1, HEAD_DIM)), whole((1, SWA_HEADS))],
        out_shape=[jax.ShapeDtypeStruct((t, SWA_QW), F32), jax.ShapeDtypeStruct((t, SWA_KVW), F32),
                   jax.ShapeDtypeStruct((t, SWA_KVW), F32), jax.ShapeDtypeStruct((1, HEAD_DIM), F32),
                   jax.ShapeDtypeStruct((1, HEAD_DIM), F32), jax.ShapeDtypeStruct((1, SWA_HEADS), F32)],
        compiler_params=_params("arbitrary"),
    )(proj, proj, proj, proj, proj, qg, kg, sinks, dout)


CONV_CB = 512
CONV_TM = 512
HALO = 8


def _conv_pre(x_ref, h_ref, w_ref, i):
    halo = jnp.where(i > 0, h_ref[...], 0.0)
    xe = jnp.concatenate([halo, x_ref[...]], axis=0)
    tm = x_ref.shape[0]
    w = w_ref[...]
    c = sum(w[k:k + 1, :] * xe[HALO - (GDN_CONV - 1) + k:HALO - (GDN_CONV - 1) + k + tm] for k in range(GDN_CONV))
    return c, xe


def _conv_specs(tm, cb):
    return [pl.BlockSpec((tm, cb), lambda c, i: (i, c)),
            pl.BlockSpec((HALO, cb), lambda c, i: (jnp.maximum(i * (tm // HALO) - 1, 0), c)),
            pl.BlockSpec((GDN_CONV, cb), lambda c, i: (0, c))]


def _conv_fwd(x, w, dact=None, *, name):
    t, ch = x.shape
    tm, cb = _tile(t, CONV_TM), _tile(ch, CONV_CB)

    def body(*refs):
        x_ref, h_ref, w_ref = refs[:3]
        c, _ = _conv_pre(x_ref, h_ref, w_ref, pl.program_id(1))
        sig = jax.nn.sigmoid(c)
        if dact is None:
            refs[3][...] = c * sig
        else:
            refs[4][...] = refs[3][...] * (sig * (1.0 + c * (1.0 - sig)))

    tile = pl.BlockSpec((tm, cb), lambda c, i: (i, c))
    extra = () if dact is None else (dact,)
    return pl.pallas_call(
        body, name=name, grid=(ch // cb, t // tm),
        in_specs=_conv_specs(tm, cb) + [tile] * len(extra), out_specs=tile,
        out_shape=jax.ShapeDtypeStruct((t, ch), F32),
        compiler_params=_params("parallel", "parallel"),
    )(x, x, w, *extra)


def _conv_bwd(x, w, dc, *, name):
    t, ch = x.shape
    tm, cb = _tile(t, CONV_TM), _tile(ch, CONV_CB)
    nt = t // tm

    def body(x_ref, h_ref, w_ref, dc_ref, nh_ref, dx_ref, dw_ref):
        i = pl.program_id(1)

        @pl.when(i == 0)
        def _():
            dw_ref[...] = jnp.zeros_like(dw_ref)

        halo = jnp.where(i > 0, h_ref[...], 0.0)
        xe = jnp.concatenate([halo, x_ref[...]], axis=0)
        dc = dc_ref[...]
        dce = jnp.concatenate([dc, jnp.where(i < nt - 1, nh_ref[...], 0.0)], axis=0)
        w = w_ref[...]
        last = GDN_CONV - 1
        dx_ref[...] = sum(w[k:k + 1, :] * dce[last - k:last - k + tm] for k in range(GDN_CONV))
        dw_ref[...] += jnp.concatenate(
            [jnp.sum(dc * xe[HALO - last + k:HALO - last + k + tm], axis=0, keepdims=True) for k in range(GDN_CONV)],
            axis=0)

    tile = pl.BlockSpec((tm, cb), lambda c, i: (i, c))
    nxt = pl.BlockSpec((HALO, cb), lambda c, i: (jnp.minimum((i + 1) * (tm // HALO), t // HALO - 1), c))
    return pl.pallas_call(
        body, name=name, grid=(ch // cb, nt),
        in_specs=_conv_specs(tm, cb) + [tile, nxt],
        out_specs=[tile, pl.BlockSpec((GDN_CONV, cb), lambda c, i: (0, c))],
        out_shape=[jax.ShapeDtypeStruct((t, ch), F32), jax.ShapeDtypeStruct((GDN_CONV, ch), F32)],
        compiler_params=_params("parallel", "arbitrary"),
    )(x, x, w, dc, dc)


def _gdn_chunk(qraw, kraw, v, bl, a, alog, dtb, state):
    c, d = GDN_CHUNK, GDN_HEAD_DIM
    nh = qraw.shape[0]
    ri = lax.broadcasted_iota(jnp.int32, (nh, c, c), 1)
    ci = lax.broadcasted_iota(jnp.int32, (nh, c, c), 2)
    incl, strict = ri >= ci, ri > ci
    q = qraw * lax.rsqrt(jnp.sum(qraw * qraw, axis=-1, keepdims=True) + EPS) * (d ** -0.5)
    k = kraw * lax.rsqrt(jnp.sum(kraw * kraw, axis=-1, keepdims=True) + EPS)
    beta = jax.nn.sigmoid(bl)
    g = -jnp.exp(alog) * jax.nn.softplus(a + dtb)
    gc = _ldot(incl.astype(F32), jnp.broadcast_to(g, (nh, c, d)))
    gcm = gc[:, :, :c]
    decay = jnp.exp(jnp.where(incl, gcm - jnp.swapaxes(gcm, 1, 2), NEG))
    eg = jnp.exp(gc)
    kbeta = k * beta
    x = -jnp.where(strict, _bdot_nt(kbeta, k) * decay, 0.0)
    tinv = (ri == ci).astype(F32) + x
    pw = x
    for _ in range(int(math.log2(c)) - 1):
        pw = _hdot(pw, pw)
        tinv = tinv + _hdot(tinv, pw)
    u = _hdot(tinv, v * beta)
    w = _hdot(tinv, kbeta * eg)
    attn = jnp.where(incl, _bdot_nt(q, k) * decay, 0.0)
    glast = gc[:, c - 1:c, :]
    v_new = u - _bdot(w, state)
    o = _bdot(q * eg, state) + _bdot(attn, v_new)
    state = state * jnp.exp(glast) + _bdot_tn(k * jnp.exp(glast - gc), v_new)
    return o, state


GDN_REP = GDN_V_HEADS // GDN_K_HEADS
GDN_HB = 4


def _gdn_pick(vals, kh, r):
    ba, alog, dtb = vals
    lane = lax.broadcasted_iota(jnp.int32, ba.shape, 1)
    hv = kh * GDN_REP + r
    bl = jnp.sum(jnp.where(lane == hv, ba, 0.0), axis=1, keepdims=True)
    a = jnp.sum(jnp.where(lane == GDN_V_HEADS + hv, ba, 0.0), axis=1, keepdims=True)
    lane1 = lax.broadcasted_iota(jnp.int32, alog.shape, 1)
    al = jnp.sum(jnp.where(lane1 == hv, alog, 0.0), axis=1, keepdims=True)
    db = jnp.sum(jnp.where(lane1 == hv, dtb, 0.0), axis=1, keepdims=True)
    return bl, a, al, db


def _gdn_stack(qs, ks, vs, small, j):
    d = GDN_HEAD_DIM
    per = [[], [], [], [], [], [], []]
    for hh in range(GDN_HB):
        q, k = qs[:, hh * d:(hh + 1) * d], ks[:, hh * d:(hh + 1) * d]
        for r in range(GDN_REP):
            col = (hh * GDN_REP + r) * d
            for lst, val in zip(per, (q, k, vs[:, col:col + d]) + _gdn_pick(small, j * GDN_HB + hh, r)):
                lst.append(val)
    return tuple(jnp.stack(lst) for lst in per)


def _gdn_specs(nchunk, rev):
    c, d = GDN_CHUNK, GDN_HEAD_DIM
    at = (lambda n: nchunk - 1 - n) if rev else (lambda n: n)
    ng = GDN_K_HEADS // GDN_HB
    return at, [pl.BlockSpec((c, GDN_HB * d), lambda n, j: (at(n), j)),
                pl.BlockSpec((c, GDN_HB * d), lambda n, j: (at(n), ng + j)),
                pl.BlockSpec((c, GDN_HB * GDN_REP * d), lambda n, j: (at(n), ng + j)),
                pl.BlockSpec((c, 2 * GDN_V_HEADS), lambda n, j: (at(n), 0)),
                pl.BlockSpec((1, GDN_V_HEADS), lambda n, j: (0, 0)),
                pl.BlockSpec((1, GDN_V_HEADS), lambda n, j: (0, 0))]


def _gdn_fwd(act, ba, alog, dtb, *, name):
    t = act.shape[0]
    c, d = GDN_CHUNK, GDN_HEAD_DIM
    nchunk = t // c
    at, specs = _gdn_specs(nchunk, False)

    def body(q_ref, k_ref, v_ref, ba_ref, al_ref, db_ref, o_ref, s_ref, state):
        n, j = pl.program_id(0), pl.program_id(1)
        heads = pl.ds(j * GDN_HB, GDN_HB)

        @pl.when(n == 0)
        def _():
            state[heads] = jnp.zeros((GDN_HB, GDN_REP, d, d), F32)

        s_in = state[heads]
        s_ref[...] = s_in
        args = _gdn_stack(q_ref[...], k_ref[...], v_ref[...], (ba_ref[...], al_ref[...], db_ref[...]), j)
        o, s_new = _gdn_chunk(*args, s_in.reshape(GDN_HB * GDN_REP, d, d))
        o_ref[...] = jnp.concatenate([o[b] for b in range(GDN_HB * GDN_REP)], axis=1)
        state[heads] = s_new.reshape(GDN_HB, GDN_REP, d, d)

    return pl.pallas_call(
        body, name=name, grid=(nchunk, GDN_K_HEADS // GDN_HB), in_specs=specs,
        out_specs=[pl.BlockSpec((c, GDN_HB * GDN_REP * d), lambda n, j: (n, j)),
                   pl.BlockSpec((None, GDN_HB, GDN_REP, d, d), lambda n, j: (n, j, 0, 0, 0))],
        out_shape=[jax.ShapeDtypeStruct((t, GDN_VW), F32),
                   jax.ShapeDtypeStruct((nchunk, GDN_K_HEADS, GDN_REP, d, d), F32)],
        scratch_shapes=[pltpu.VMEM((GDN_K_HEADS, GDN_REP, d, d), F32)],
        compiler_params=_params("arbitrary", "arbitrary"),
    )(act, act, act, ba, alog, dtb)


def _gdn_bwd(act, ba, alog, dtb, states, dout, *, name):
    t = act.shape[0]
    c, d = GDN_CHUNK, GDN_HEAD_DIM
    nchunk = t // c
    at, specs = _gdn_specs(nchunk, True)

    def body(q_ref, k_ref, v_ref, ba_ref, al_ref, db_ref, s_ref, do_ref,
             dq_ref, dk_ref, dv_ref, dba_ref, dal_ref, ddb_ref, dstate):
        n, j = pl.program_id(0), pl.program_id(1)

        @pl.when(n == 0)
        def _():
            dstate[pl.ds(j * GDN_HB, GDN_HB)] = jnp.zeros((GDN_HB, GDN_REP, d, d), F32)

        @pl.when((n == 0) & (j == 0))
        def _():
            dal_ref[...] = jnp.zeros_like(dal_ref)
            ddb_ref[...] = jnp.zeros_like(ddb_ref)

        @pl.when(j == 0)
        def _():
            dba_ref[...] = jnp.zeros_like(dba_ref)

        heads = pl.ds(j * GDN_HB, GDN_HB)
        nh = GDN_HB * GDN_REP
        args = _gdn_stack(q_ref[...], k_ref[...], v_ref[...], (ba_ref[...], al_ref[...], db_ref[...]), j)
        _, vjp = jax.vjp(_gdn_chunk, *args, s_ref[...].reshape(nh, d, d))
        do = do_ref[...]
        do = jnp.stack([do[:, b * d:(b + 1) * d] for b in range(nh)])
        gq, gk, gv, gbl, ga, gal, gdb, gs = vjp((do, dstate[heads].reshape(nh, d, d)))
        dstate[heads] = gs.reshape(GDN_HB, GDN_REP, d, d)
        dq_ref[...] = jnp.concatenate([gq[GDN_REP * hh] + gq[GDN_REP * hh + 1] for hh in range(GDN_HB)], axis=1)
        dk_ref[...] = jnp.concatenate([gk[GDN_REP * hh] + gk[GDN_REP * hh + 1] for hh in range(GDN_HB)], axis=1)
        dv_ref[...] = jnp.concatenate([gv[b] for b in range(nh)], axis=1)
        lane = lax.broadcasted_iota(jnp.int32, (c, 2 * GDN_V_HEADS), 1)
        lane1 = lax.broadcasted_iota(jnp.int32, (1, GDN_V_HEADS), 1)
        dba = jnp.zeros((c, 2 * GDN_V_HEADS), F32)
        dal = jnp.zeros((1, GDN_V_HEADS), F32)
        ddb = jnp.zeros((1, GDN_V_HEADS), F32)
        for b in range(nh):
            hv = j * nh + b
            dba = dba + jnp.where(lane == hv, gbl[b], 0.0) + jnp.where(lane == GDN_V_HEADS + hv, ga[b], 0.0)
            dal = dal + jnp.where(lane1 == hv, gal[b], 0.0)
            ddb = ddb + jnp.where(lane1 == hv, gdb[b], 0.0)
        dba_ref[...] += dba
        dal_ref[...] += dal
        ddb_ref[...] += ddb

    small = pl.BlockSpec((1, GDN_V_HEADS), lambda n, j: (0, 0))
    return pl.pallas_call(
        body, name=name, grid=(nchunk, GDN_K_HEADS // GDN_HB),
        in_specs=specs + [pl.BlockSpec((None, GDN_HB, GDN_REP, d, d), lambda n, j: (at(n), j, 0, 0, 0)),
                          pl.BlockSpec((c, GDN_HB * GDN_REP * d), lambda n, j: (at(n), j))],
        out_specs=[pl.BlockSpec((c, GDN_HB * d), lambda n, j: (at(n), j)),
                   pl.BlockSpec((c, GDN_HB * d), lambda n, j: (at(n), j)),
                   pl.BlockSpec((c, GDN_HB * GDN_REP * d), lambda n, j: (at(n), j)),
                   pl.BlockSpec((c, 2 * GDN_V_HEADS), lambda n, j: (at(n), 0)),
                   small, small],
        out_shape=[jax.ShapeDtypeStruct((t, GDN_KW), F32), jax.ShapeDtypeStruct((t, GDN_KW), F32),
                   jax.ShapeDtypeStruct((t, GDN_VW), F32), jax.ShapeDtypeStruct((t, 2 * GDN_V_HEADS), F32),
                   jax.ShapeDtypeStruct((1, GDN_V_HEADS), F32), jax.ShapeDtypeStruct((1, GDN_V_HEADS), F32)],
        scratch_shapes=[pltpu.VMEM((GDN_K_HEADS, GDN_REP, d, d), F32)],
        compiler_params=_params("arbitrary", "arbitrary"),
    )(act, act, act, ba, alog, dtb, states, dout)


N_DEV = 8
ANY = pl.BlockSpec(memory_space=pl.ANY)


def _coords():
    return lax.axis_index("x"), lax.axis_index("y"), lax.axis_index("c")


def _other_chips(x, y):
    return [(1 - x, y), (x, 1 - y), (1 - x, 1 - y)]


def _remote(src, dst, send_sems, recv_sems, k, to):
    return pltpu.make_async_remote_copy(src_ref=src, dst_ref=dst, send_sem=send_sems.at[k], recv_sem=recv_sems.at[k],
                                        device_id=to, device_id_type=MESH)


def _dma_sems(n):
    return [pltpu.SemaphoreType.DMA((n,)), pltpu.SemaphoreType.DMA((n,))]


def _gather_quarters(parts, *, name):
    na = len(parts)

    def body(*refs):
        ins, outs = refs[:na], refs[na:2 * na]
        send_sems, recv_sems, local_sems = refs[2 * na:]
        x, y, c = _coords()
        sibling = (x, y, 1 - c)
        chips = _other_chips(x, y)
        mine, first, passed = [], [], []
        for a, (x_ref, out_ref) in enumerate(zip(ins, outs)):
            mine.append(pltpu.make_async_copy(x_ref, out_ref.at[2 * x + y], local_sems.at[a]))
            for j, (cx, cy) in enumerate(chips):
                first.append(_remote(x_ref.at[c], out_ref.at[2 * x + y, c], send_sems, recv_sems, 6 * a + j, (cx, cy, c)))
                passed.append(_remote(out_ref.at[2 * cx + cy, c], out_ref.at[2 * cx + cy, c], send_sems, recv_sems,
                                      6 * a + 3 + j, sibling))
        for cp in mine + first:
            cp.start()
        for a, (x_ref, out_ref) in enumerate(zip(ins, outs)):
            for j, (cx, cy) in enumerate(chips):
                _remote(x_ref.at[c], out_ref.at[2 * cx + cy, c], send_sems, recv_sems, 6 * a + j, (cx, cy, c)).wait_recv()
                passed[3 * a + j].start()
        for a, (x_ref, out_ref) in enumerate(zip(ins, outs)):
            for j, (cx, cy) in enumerate(chips):
                _remote(x_ref.at[c], out_ref.at[2 * cx + cy, 1 - c], send_sems, recv_sems, 6 * a + 3 + j,
                        sibling).wait_recv()
        for cp in first + passed:
            cp.wait_send()
        for cp in mine:
            cp.wait()

    return pl.pallas_call(
        body, name=name, in_specs=[ANY] * na, out_specs=[ANY] * na,
        out_shape=[jax.ShapeDtypeStruct((N_CHIPS,) + p.shape, p.dtype) for p in parts],
        scratch_shapes=_dma_sems(6 * na) + [pltpu.SemaphoreType.DMA((na,))],
    )(*parts)


def _swap_halves(grads, *, name):
    na = len(grads)

    def body(*refs):
        ins, outs = refs[:na], refs[na:2 * na]
        send_sems, recv_sems = refs[2 * na:]
        x, y, c = _coords()
        sends = [_remote(g_ref.at[j, 1 - c], o_ref.at[j], send_sems, recv_sems, N_CHIPS * a + j, (x, y, 1 - c))
                 for a, (g_ref, o_ref) in enumerate(zip(ins, outs)) for j in range(N_CHIPS)]
        for cp in sends:
            cp.start()
        for cp in sends:
            cp.wait()

    return pl.pallas_call(
        body, name=name, in_specs=[ANY] * na, out_specs=[ANY] * na,
        out_shape=[jax.ShapeDtypeStruct((N_CHIPS,) + g.shape[2:], g.dtype) for g in grads],
        scratch_shapes=_dma_sems(N_CHIPS * na),
    )(*grads)


def _scatter_quarters(pairs, *, name):
    na = len(pairs)

    def body(*refs):
        ins, outs = refs[:na], refs[na:4 * na]
        send_sems, recv_sems = refs[4 * na:]
        x, y, c = _coords()
        sends = [_remote(p_ref.at[2 * cx + cy], outs[3 * a + j], send_sems, recv_sems, 3 * a + j, (cx, cy, c))
                 for a, p_ref in enumerate(ins) for j, (cx, cy) in enumerate(_other_chips(x, y))]
        for cp in sends:
            cp.start()
        for cp in sends:
            cp.wait()

    out = pl.pallas_call(
        body, name=name, in_specs=[ANY] * na, out_specs=[ANY] * (3 * na),
        out_shape=[jax.ShapeDtypeStruct(p.shape[1:], p.dtype) for p in pairs for _ in range(3)],
        scratch_shapes=_dma_sems(3 * na),
    )(*pairs)
    return [out[3 * a:3 * a + 3] for a in range(na)]


def _share_halves(tots, *, name):
    na = len(tots)

    def body(*refs):
        ins, outs = refs[:na], refs[na:2 * na]
        send_sems, recv_sems = refs[2 * na:]
        x, y, c = _coords()
        sends = [_remote(t_ref, o_ref, send_sems, recv_sems, a, (x, y, 1 - c))
                 for a, (t_ref, o_ref) in enumerate(zip(ins, outs))]
        for cp in sends:
            cp.start()
        for cp in sends:
            cp.wait()

    return pl.pallas_call(
        body, name=name, in_specs=[ANY] * na, out_specs=[ANY] * na,
        out_shape=[jax.ShapeDtypeStruct(t.shape, t.dtype) for t in tots],
        scratch_shapes=_dma_sems(na),
    )(*tots)


def _gather_all(vec, *, name):
    m, w = vec.shape

    def body(x_ref, out_ref, send_sems, recv_sems, local_sem):
        x, y, c = _coords()
        me, sibling = (x, y, c), (x, y, 1 - c)
        chips = _other_chips(x, y)

        def rows(px, py, pc):
            return out_ref.at[pl.ds((4 * px + 2 * py + pc) * m, m), :]

        def copy(k, block, to, src=None):
            return _remote(rows(*block) if src is None else src, rows(*block), send_sems, recv_sems, k, to)

        mine = pltpu.make_async_copy(x_ref, rows(*me), local_sem)
        mine.start()
        first = [copy(0, me, sibling, src=x_ref)]
        first += [copy(1 + j, me, (*chip, c), src=x_ref) for j, chip in enumerate(chips)]
        for cp in first:
            cp.start()
        passed = [copy(4 + j, (*chip, c), sibling) for j, chip in enumerate(chips)]
        for j, chip in enumerate(chips):
            copy(1 + j, (*chip, c), me).wait_recv()
            passed[j].start()
        copy(0, sibling, me).wait_recv()
        for j, chip in enumerate(chips):
            copy(4 + j, (*chip, 1 - c), me).wait_recv()
        for cp in first + passed:
            cp.wait_send()
        mine.wait()

    vm = pl.BlockSpec(memory_space=pltpu.VMEM)
    return pl.pallas_call(
        body, name=name, in_specs=[vm], out_specs=vm, out_shape=jax.ShapeDtypeStruct((N_DEV * m, w), vec.dtype),
        scratch_shapes=_dma_sems(7) + [pltpu.SemaphoreType.DMA(())],
    )(vec)


def _sum_blocks(allv, n, *, name):
    m = allv.shape[0] // n

    def body(a_ref, o_ref):
        acc = a_ref[0:m, :]
        for d in range(1, n):
            acc = acc + a_ref[d * m:(d + 1) * m, :]
        o_ref[...] = acc

    return pl.pallas_call(body, name=name, out_shape=jax.ShapeDtypeStruct((m, allv.shape[1]), allv.dtype))(allv)


EW_BLOCK_BYTES = 1 << 20


def _ew_rows(rows, w):
    return _tile(rows, max(8, (EW_BLOCK_BYTES // (4 * w)) // 8 * 8), 8)


def _add_pair(g, got, c, *, name):
    _, _, rows, w = g.shape
    tr = _ew_rows(rows, w)

    def body(c_ref, g_ref, got_ref, o_ref):
        o_ref[...] = (g_ref[...] + got_ref[...]).astype(o_ref.dtype)

    blk = pl.BlockSpec((None, tr, w), lambda q, i, c_ref: (q, i, 0))
    return pl.pallas_call(
        body, name=name,
        grid_spec=pltpu.PrefetchScalarGridSpec(
            num_scalar_prefetch=1, grid=(N_CHIPS, rows // tr),
            in_specs=[pl.BlockSpec((None, None, tr, w), lambda q, i, c_ref: (q, c_ref[0], i, 0)), blk], out_specs=blk),
        out_shape=jax.ShapeDtypeStruct(got.shape, BF16),
        compiler_params=_params("parallel", "parallel"),
    )(c, g, got)


def _add_chips(pair, recv, chip, *, name):
    _, rows, w = pair.shape
    tr = _ew_rows(rows, w)

    def body(chip_ref, p_ref, r0_ref, r1_ref, r2_ref, o_ref):
        f = lambda r: r[...].astype(F32)
        o_ref[...] = ((f(p_ref) + f(r0_ref)) + f(r1_ref)) + f(r2_ref)

    blk = pl.BlockSpec((tr, w), lambda i, chip_ref: (i, 0))
    return pl.pallas_call(
        body, name=name,
        grid_spec=pltpu.PrefetchScalarGridSpec(
            num_scalar_prefetch=1, grid=(rows // tr,),
            in_specs=[pl.BlockSpec((None, tr, w), lambda i, chip_ref: (chip_ref[0], i, 0)), blk, blk, blk], out_specs=blk),
        out_shape=jax.ShapeDtypeStruct((rows, w), F32),
        compiler_params=_params("parallel"),
    )(chip, pair, *recv)


def _adamw_math(w, g, m, v):
    nm = ADAM_B1 * m + (1.0 - ADAM_B1) * g
    nv = ADAM_B2 * v + (1.0 - ADAM_B2) * (g * g)
    m_hat = nm / (1.0 - ADAM_B1 ** ADAM_STEP)
    v_hat = nv / (1.0 - ADAM_B2 ** ADAM_STEP)
    return -ADAM_LR * (m_hat / (jnp.sqrt(v_hat) + ADAM_EPS) + ADAM_WD * w), nm, nv


def _adamw(w, g, m, v, *, name):
    shape = w.shape
    last = shape[-1]
    w2, g2, m2, v2 = (a.reshape(-1, last) for a in (w, g, m, v))
    rows = w2.shape[0]
    tm = _ew_rows(rows, last)

    def body(w_ref, g_ref, m_ref, v_ref, d_ref, nm_ref, nv_ref):
        d_ref[...], nm_ref[...], nv_ref[...] = _adamw_math(w_ref[...], g_ref[...], m_ref[...], v_ref[...])

    spec = pl.BlockSpec((tm, last), lambda i: (i, 0))
    out = jax.ShapeDtypeStruct((rows, last), F32)
    d, nm, nv = pl.pallas_call(
        body, name=name, grid=(rows // tm,), in_specs=[spec] * 4, out_specs=[spec] * 3, out_shape=[out] * 3,
        compiler_params=_params("parallel"),
    )(w2, g2, m2, v2)
    return d.reshape(shape), nm.reshape(shape), nv.reshape(shape)


def _adamw_halves(w, m, v, mine, theirs, c, *, name):
    _, rows, wd = w.shape
    tr = _ew_rows(rows, wd)

    def body(c_ref, w_ref, m_ref, v_ref, a_ref, b_ref, g_ref, d_ref, nm_ref, nv_ref):
        g = jnp.where(pl.program_id(0) == c_ref[0], a_ref[...], b_ref[...])
        g_ref[...] = g
        d_ref[...], nm_ref[...], nv_ref[...] = _adamw_math(w_ref[...], g, m_ref[...], v_ref[...])

    full = pl.BlockSpec((None, tr, wd), lambda hf, i, c_ref: (hf, i, 0))
    half = pl.BlockSpec((tr, wd), lambda hf, i, c_ref: (i, 0))
    out = jax.ShapeDtypeStruct(w.shape, F32)
    return pl.pallas_call(
        body, name=name,
        grid_spec=pltpu.PrefetchScalarGridSpec(num_scalar_prefetch=1, grid=(2, rows // tr),
                                               in_specs=[full] * 3 + [half] * 2, out_specs=[full] * 4),
        out_shape=[out] * 4,
        compiler_params=_params("parallel", "parallel"),
    )(c, w, m, v, mine, theirs)


def _join_quarters(q, *, name):
    _, rows, n = q.shape
    tr = _tile(rows, 256, 16)

    def body(q_ref, o_ref):
        o_ref[...] = jnp.concatenate([q_ref[s] for s in range(N_CHIPS)], axis=1)

    return pl.pallas_call(
        body, name=name, grid=(rows // tr,),
        in_specs=[pl.BlockSpec((N_CHIPS, tr, n), lambda i: (0, i, 0))],
        out_specs=pl.BlockSpec((tr, N_CHIPS * n), lambda i: (i, 0)),
        out_shape=jax.ShapeDtypeStruct((rows, N_CHIPS * n), q.dtype),
        compiler_params=_params("parallel"),
    )(q)


def _split_quarters(full, *, name):
    rows, n4 = full.shape
    n = n4 // N_CHIPS
    tr = _tile(rows, 256, 16)

    def body(x_ref, o_ref):
        x = x_ref[...]
        for s in range(N_CHIPS):
            o_ref[s] = x[:, s * n:(s + 1) * n]

    return pl.pallas_call(
        body, name=name, grid=(rows // tr,),
        in_specs=[pl.BlockSpec((tr, n4), lambda i: (i, 0))],
        out_specs=pl.BlockSpec((N_CHIPS, tr, n), lambda i: (0, i, 0)),
        out_shape=jax.ShapeDtypeStruct((N_CHIPS, rows, n), full.dtype),
        compiler_params=_params("parallel"),
    )(full)


_WEIGHTS = ['ffn_norm', 'ffn_w_gate', 'ffn_w_up', 'ffn_w_down', 'mix_norm', 'att_w_in', 'att_q_norm', 'att_k_norm',
            'att_sinks', 'att_w_out', 'gdn_w_in', 'gdn_conv_w', 'gdn_a_log', 'gdn_dt_bias', 'gdn_out_norm', 'gdn_w_out',
            'ple_norm', 'ple_w_gate', 'ple_w_proj']
_BIG = ['ffn_w_gate', 'ffn_w_up', 'ffn_w_down', 'att_w_in', 'att_w_out', 'gdn_w_in', 'gdn_w_out', 'ple_w_gate',
        'ple_w_proj']
_SMALL_CUT = {'ffn_norm': 2, 'gdn_conv_w': 2}
_WHOLE = ['mix_norm', 'att_q_norm', 'att_k_norm', 'att_sinks', 'gdn_a_log', 'gdn_dt_bias', 'gdn_out_norm', 'ple_norm']
PACK_W = 1024
SMALL_ROW_MULT = 8


def _halves(a):
    return a.reshape(2, -1, a.shape[-1])


def _from_quarters(blk, axis):
    full = jnp.moveaxis(blk, 0, axis)
    shp = list(full.shape)
    shp[axis:axis + 2] = [shp[axis] * shp[axis + 1]]
    return full.reshape(shp)


def _to_quarters(full, axis):
    shp = list(full.shape)
    shp[axis:axis + 1] = [N_CHIPS, shp[axis] // N_CHIPS]
    return jnp.moveaxis(full.reshape(shp), axis, 0)


def _pack(parts, row_mult):
    flat = jnp.concatenate(parts, axis=-1)
    n = flat.shape[-1]
    rows = -(-n // (PACK_W * row_mult)) * row_mult
    return jnp.pad(flat, [(0, rows * PACK_W - n)]).reshape(rows, PACK_W)


def _unpack(flat, shapes):
    lead = flat.shape[:-2]
    flat = flat.reshape(lead + (-1,))
    out, off = [], 0
    for shp in shapes:
        n = math.prod(shp)
        out.append(flat[..., off:off + n].reshape(lead + tuple(shp)))
        off += n
    return out


def _ffn_fwd(h, gain, wg, wu, wd, at, tag):
    t = h.shape[0]
    fq = wd.shape[-2]
    lead = (Q,) + at
    hn, = _row_fwd(_f_rms, [h], [gain], [(D_MODEL, BF16)], name=f"{tag}_norm")
    g = _mm(hn, (wg, lead), out_q=True, name=f"{tag}_gate").reshape(N_CHIPS * t, fq)
    u = _mm(hn, (wu, lead), out_q=True, name=f"{tag}_up").reshape(N_CHIPS * t, fq)
    a, = _row_fwd(_f_swiglu, [g, u], [], [(fq, BF16)], name=f"{tag}_act")
    a = a.reshape(N_CHIPS, t, fq)
    out = _mm((a, (Q,)), (wd, lead), res=h, scale=0.5, name=f"{tag}_down")
    return out, (h, hn, g, u, a)


def _ffn_bwd(dout, saved, gain, wg, wu, wd, at, grads, tag):
    h, hn, g, u, a = saved
    t = h.shape[0]
    fq = wd.shape[-2]
    lead = (Q,) + at
    da = _mm(dout, (wd, lead), tb=True, scale=0.5, out_q=True, name=f"{tag}_d_act").reshape(N_CHIPS * t, fq)
    dg, du = _row_bwd(_f_swiglu, [g, u], [], [da], [(0, BF16), (1, BF16)], [], name=f"{tag}_d_gate_up")
    dg, du = dg.reshape(N_CHIPS, t, fq), du.reshape(N_CHIPS, t, fq)
    g_gate, g_up, g_down = grads
    g_down = _mm((a, (Q,)), dout, ta=True, scale=0.5, into=(g_down, lead), name=f"{tag}_dw_down")
    g_gate = _mm(hn, (dg, (Q,)), ta=True, into=(g_gate, lead), name=f"{tag}_dw_gate")
    g_up = _mm(hn, (du, (Q,)), ta=True, into=(g_up, lead), name=f"{tag}_dw_up")
    dhn = _mm((dg, (Q,)), (wg, lead), tb=True, name=f"{tag}_d_norm_gate")
    dhn = _mm((du, (Q,)), (wu, lead), tb=True, res=dhn, name=f"{tag}_d_norm_up")
    dh, dgain = _row_bwd(_f_rms_res, [h], [gain], [dhn, dout], [(0, F32)], [0], name=f"{tag}_d_in")
    return dh, dgain, (g_gate, g_up, g_down)


def _att_fwd(h, gain, w_in, qg, kg, sinks, w_out):
    hn, = _row_fwd(_f_rms, [h], [gain], [(D_MODEL, BF16)], name="att_norm")
    proj = _mm(hn, w_in, name="att_in")
    a, rtot = _sb_fwd(proj, name="att_sb")
    b = _swa_fwd(proj, qg, kg, sinks, name="att_swa")
    out = _mm(a, (w_out, (0,)), res=h, name="att_out_sb")
    out = _mm(b, (w_out, (1,)), res=out, name="att_out_swa")
    return out, (h, hn, proj, a, rtot, b)


def _att_bwd(dout, saved, gain, w_in, qg, kg, sinks, w_out):
    h, hn, proj, a, rtot, b = saved
    da = _mm(dout, (w_out, (0,)), tb=True, name="att_d_sb")
    db = _mm(dout, (w_out, (1,)), tb=True, name="att_d_swa")
    dw_out = jnp.zeros(w_out.shape, F32)
    dw_out = _mm(a, dout, ta=True, into=(dw_out, (0,)), name="att_dw_out_sb")
    dw_out = _mm(b, dout, ta=True, into=(dw_out, (1,)), name="att_dw_out_swa")
    dq, dk, dv = _sb_bwd(proj, rtot, da, name="att_sb_bwd")
    dqb, dkb, dvb, dqg, dkg, dsk = _swa_bwd(proj, qg, kg, sinks, db, name="att_swa_bwd")
    dproj = jnp.concatenate([dq, dk, dv, dqb, dkb, dvb], axis=1)
    dw_in = _mm(hn, dproj, ta=True, name="att_dw_in")
    dhn = _mm(dproj, w_in, tb=True, name="att_d_norm")
    dh, dgain = _row_bwd(_f_rms_res, [h], [gain], [dhn, dout], [(0, F32)], [0], name="att_d_in")
    return dh, dgain, dw_in, dqg, dkg, dsk, dw_out


def _gdn_layer_fwd(h, gain, w_in, conv_w, alog, dtb, out_gain, w_out):
    w_qkv, w_z, w_ba = w_in[:, :GDN_CONV_W], w_in[:, GDN_CONV_W:GDN_CONV_W + GDN_VW], w_in[:, GDN_CONV_W + GDN_VW:]
    hn, = _row_fwd(_f_rms, [h], [gain], [(D_MODEL, BF16)], name="gdn_norm")
    pq = _mm(hn, w_qkv, name="gdn_in_qkv")
    pz = _mm(hn, w_z, name="gdn_in_z")
    ba = _mm(hn, w_ba, name="gdn_in_ba")
    act = _conv_fwd(pq, conv_w, name="gdn_conv")
    o, states = _gdn_fwd(act, ba, alog, dtb, name="gdn_rule")
    y, = _row_fwd(_f_gdn_out, [o, pz], [out_gain], [(GDN_VW, BF16)], name="gdn_gate")
    out = _mm(y, w_out, res=h, name="gdn_out")
    return out, (h, hn, pq, pz, ba, act, o, states, y, (w_qkv, w_z, w_ba))


def _gdn_layer_bwd(dout, saved, gain, conv_w, alog, dtb, out_gain, w_out):
    h, hn, pq, pz, ba, act, o, states, y, (w_qkv, w_z, w_ba) = saved
    dy = _mm(dout, w_out, tb=True, name="gdn_d_gate")
    dw_out = _mm(y, dout, ta=True, name="gdn_dw_out")
    do, dpz, dout_gain = _row_bwd(_f_gdn_out, [o, pz], [out_gain], [dy], [(0, F32), (1, F32)], [0], name="gdn_gate_bwd")
    dq, dk, dv, dba, dal, ddb = _gdn_bwd(act, ba, alog, dtb, states, do, name="gdn_rule_bwd")
    dact = jnp.concatenate([dq, dk, dv], axis=1)
    dc = _conv_fwd(pq, conv_w, dact, name="gdn_conv_d_pre")
    dpq, dconv = _conv_bwd(pq, conv_w, dc, name="gdn_conv_bwd")
    dw_in = jnp.concatenate([_mm(hn, dpq, ta=True, name="gdn_dw_qkv"), _mm(hn, dpz, ta=True, name="gdn_dw_z"),
                             _mm(hn, dba, ta=True, name="gdn_dw_ba")], axis=1)
    dhn = _mm(dpq, w_qkv, tb=True, name="gdn_d_norm_qkv")
    dhn = _mm(dpz, w_z, tb=True, res=dhn, name="gdn_d_norm_z")
    dhn = _mm(dba, w_ba, tb=True, res=dhn, name="gdn_d_norm_ba")
    dh, dgain = _row_bwd(_f_rms_res, [h], [gain], [dhn, dout], [(0, F32)], [0], name="gdn_d_in")
    return dh, dgain, dw_in, dconv, dal, ddb, dout_gain, dw_out


def _ple_fwd(h, gain, w_gate, w_proj, pe, tag):
    hn, = _row_fwd(_f_rms, [h], [gain], [(D_MODEL, BF16)], name=f"{tag}_norm")
    gl = _mm(hn, w_gate, name=f"{tag}_gate")
    pp = _mm(pe, w_proj, name=f"{tag}_proj")
    out, = _row_fwd(_f_ple, [h, gl, pp], [], [(D_MODEL, F32)], name=f"{tag}_mix")
    return out, (h, hn, gl, pp)


def _ple_bwd(dout, saved, gain, w_gate, pe, tag):
    h, hn, gl, pp = saved
    dha, dgl, dpp = _row_bwd(_f_ple, [h, gl, pp], [], [dout], [(0, F32), (1, BF16), (2, BF16)], [], name=f"{tag}_mix_bwd")
    dw_gate = _mm(hn, dgl, ta=True, name=f"{tag}_dw_gate")
    dw_proj = _mm(pe, dpp, ta=True, name=f"{tag}_dw_proj")
    dhn = _mm(dgl, w_gate, tb=True, name=f"{tag}_d_norm")
    dh, dgain = _row_bwd(_f_rms_res, [h], [gain], [dhn, dha], [(0, F32)], [0], name=f"{tag}_d_in")
    return dh, dgain, dw_gate, dw_proj


def kernel(x, p, ffn_norm, ffn_w_gate, ffn_w_up, ffn_w_down, mix_norm, att_w_in, att_q_norm, att_k_norm, att_sinks, att_w_out, gdn_w_in, gdn_conv_w, gdn_a_log, gdn_dt_bias, gdn_out_norm, gdn_w_out, ple_norm, ple_w_gate, ple_w_proj, loss_target, m_ffn_norm, m_ffn_w_gate, m_ffn_w_up, m_ffn_w_down, m_mix_norm, m_att_w_in, m_att_q_norm, m_att_k_norm, m_att_sinks, m_att_w_out, m_gdn_w_in, m_gdn_conv_w, m_gdn_a_log, m_gdn_dt_bias, m_gdn_out_norm, m_gdn_w_out, m_ple_norm, m_ple_w_gate, m_ple_w_proj, v_ffn_norm, v_ffn_w_gate, v_ffn_w_up, v_ffn_w_down, v_mix_norm, v_att_w_in, v_att_q_norm, v_att_k_norm, v_att_sinks, v_att_w_out, v_gdn_w_in, v_gdn_conv_w, v_gdn_a_log, v_gdn_dt_bias, v_gdn_out_norm, v_gdn_w_out, v_ple_norm, v_ple_w_gate, v_ple_w_proj):
    arg = dict(locals())
    cx, cy, cc = _coords()
    chip = (2 * cx + cy).astype(jnp.int32).reshape(1)
    core = cc.astype(jnp.int32).reshape(1)
    n_layers = ffn_norm.shape[0]

    gathered = _gather_quarters([_halves(arg[n].astype(BF16)) for n in _BIG], name="gather_weights")
    wq = {n: g.reshape((N_CHIPS,) + arg[n].shape) for n, g in zip(_BIG, gathered)}
    wt = {}
    wt['att_w_in'] = _join_quarters(wq['att_w_in'][:, 0], name="att_w_in_join")
    wt['gdn_w_in'] = _join_quarters(wq['gdn_w_in'][:, 0], name="gdn_w_in_join")
    wt['att_w_out'] = wq['att_w_out'].reshape(2, SB_W, D_MODEL)
    wt['gdn_w_out'] = wq['gdn_w_out'].reshape(GDN_VW, D_MODEL)
    wt['ple_w_gate'] = _from_quarters(wq['ple_w_gate'], 1)
    wt['ple_w_proj'] = _from_quarters(wq['ple_w_proj'], 2)

    small_names = list(_SMALL_CUT)
    small_shapes = [arg[n].shape for n in small_names]
    svec = _pack([arg[n].reshape(-1) for n in small_names], SMALL_ROW_MULT)
    srows = svec.shape[0]
    sall = _gather_all(svec, name="gather_gains").reshape(N_CHIPS, 2, srows, PACK_W)[:, 0]
    for n, q in zip(small_names, _unpack(sall, small_shapes)):
        wt[n] = _from_quarters(q, _SMALL_CUT[n])
    row = lambda v: v.reshape(1, -1)

    h = x[0]
    tape = []
    ffn_w = (wq['ffn_w_gate'], wq['ffn_w_up'], wq['ffn_w_down'])
    for i in range(n_layers):
        j = i // 2
        h, s0 = _ffn_fwd(h, row(wt['ffn_norm'][i, 0]), *ffn_w, (i, 0), f"ffn{i}a")
        if i % 2 == 0:
            h, sm = _att_fwd(h, row(mix_norm[i]), wt['att_w_in'], att_q_norm[j:j + 1], att_k_norm[j:j + 1],
                             att_sinks[j:j + 1], wt['att_w_out'])
        else:
            h, sm = _gdn_layer_fwd(h, row(mix_norm[i]), wt['gdn_w_in'], wt['gdn_conv_w'][j], gdn_a_log[j:j + 1],
                                   gdn_dt_bias[j:j + 1], gdn_out_norm[j:j + 1], wt['gdn_w_out'])
        h, s1 = _ffn_fwd(h, row(wt['ffn_norm'][i, 1]), *ffn_w, (i, 1), f"ffn{i}b")
        h, sp = _ple_fwd(h, row(ple_norm[i]), wt['ple_w_gate'][i], wt['ple_w_proj'][i], p[i, 0], f"ple{i}")
        tape.append((s0, sm, s1, sp))

    dh, loss_local = _loss_head(h, loss_target[0], name="loss_head")
    loss = lax.psum(loss_local, ("x", "y", "c"))

    gr = {}
    ffn_g = tuple(jnp.zeros(w.shape, F32) for w in ffn_w)
    d_ffn_norm = [[None, None] for _ in range(n_layers)]
    d_mix, d_ple_norm, d_ple_gate, d_ple_proj = [None] * n_layers, [None] * n_layers, [None] * n_layers, [None] * n_layers
    for i in reversed(range(n_layers)):
        j = i // 2
        s0, sm, s1, sp = tape[i]
        dh, d_ple_norm[i], d_ple_gate[i], d_ple_proj[i] = _ple_bwd(dh, sp, row(ple_norm[i]), wt['ple_w_gate'][i], p[i, 0],
                                                                   f"ple{i}")
        dh, d_ffn_norm[i][1], ffn_g = _ffn_bwd(dh, s1, row(wt['ffn_norm'][i, 1]), *ffn_w, (i, 1), ffn_g, f"ffn{i}b")
        if i % 2 == 0:
            (dh, d_mix[i], dw_in, gr['att_q_norm'], gr['att_k_norm'], gr['att_sinks'],
             dw_out) = _att_bwd(dh, sm, row(mix_norm[i]), wt['att_w_in'], att_q_norm[j:j + 1],
                                att_k_norm[j:j + 1], att_sinks[j:j + 1], wt['att_w_out'])
            gr['att_w_in'] = _split_quarters(dw_in, name="att_dw_in_split")
            gr['att_w_out'] = dw_out
        else:
            (dh, d_mix[i], dw_in, dconv, gr['gdn_a_log'], gr['gdn_dt_bias'], gr['gdn_out_norm'],
             dw_out) = _gdn_layer_bwd(dh, sm, row(mix_norm[i]), wt['gdn_conv_w'][j], gdn_a_log[j:j + 1],
                                      gdn_dt_bias[j:j + 1], gdn_out_norm[j:j + 1], wt['gdn_w_out'])
            gr['gdn_w_in'] = _split_quarters(dw_in, name="gdn_dw_in_split")
            gr['gdn_w_out'] = dw_out
            gr['gdn_conv_w'] = dconv[None]
        dh, d_ffn_norm[i][0], ffn_g = _ffn_bwd(dh, s0, row(wt['ffn_norm'][i, 0]), *ffn_w, (i, 0), ffn_g, f"ffn{i}a")
    grad_x = dh[None]

    gr['ffn_w_gate'], gr['ffn_w_up'], gr['ffn_w_down'] = ffn_g
    gr['ple_w_gate'] = _to_quarters(jnp.stack(d_ple_gate), 1)
    gr['ple_w_proj'] = _to_quarters(jnp.stack(d_ple_proj), 2)
    gr['ffn_norm'] = jnp.stack([jnp.stack([d_ffn_norm[i][k][0] for k in range(2)]) for i in range(n_layers)])
    gr['mix_norm'] = jnp.concatenate(d_mix, axis=0)
    gr['ple_norm'] = jnp.concatenate(d_ple_norm, axis=0)

    gq = [gr[n].reshape((N_CHIPS, 2, -1, arg[n].shape[-1])) for n in _BIG]
    got = _swap_halves(gq, name="grad_swap_halves")
    pairs = [_add_pair(g, o, core, name=f"grad_add_pair_{n}") for n, g, o in zip(_BIG, gq, got)]
    recv = _scatter_quarters(pairs, name="grad_scatter")
    tots = [_add_chips(pr, rc, chip, name=f"grad_add_chips_{n}") for n, pr, rc in zip(_BIG, pairs, recv)]
    theirs = _share_halves(tots, name="grad_share")

    whole_shapes = [arg[n].shape for n in _WHOLE]
    cut_full_shapes = [gr[n].shape for n in small_names]
    gvec = _pack([gr[n].reshape(-1) for n in _WHOLE + small_names], SMALL_ROW_MULT)
    gall = _sum_blocks(_gather_all(gvec, name="gather_small_grads"), N_DEV, name="sum_small_grads")
    parts = _unpack(gall, whole_shapes + cut_full_shapes)
    gsum = dict(zip(_WHOLE, parts))
    for n, g in zip(small_names, parts[len(_WHOLE):]):
        gsum[n] = lax.dynamic_index_in_dim(_to_quarters(g, _SMALL_CUT[n]), chip[0], axis=0, keepdims=False)

    delta, new_m, new_v = {}, {}, {}
    for n, mine, other in zip(_BIG, tots, theirs):
        res = _adamw_halves(_halves(arg[n]), _halves(arg["m_" + n]), _halves(arg["v_" + n]), mine, other, core,
                            name=f"adamw_{n}")
        gsum[n], delta[n], new_m[n], new_v[n] = (r.reshape(arg[n].shape) for r in res)
    for n in _WHOLE + small_names:
        delta[n], new_m[n], new_v[n] = _adamw(arg[n], gsum[n], arg["m_" + n], arg["v_" + n], name=f"adamw_{n}")
    return (loss, grad_x, *[gsum[n] for n in _WEIGHTS], *[delta[n] for n in _WEIGHTS],
            *[new_m[n] for n in _WEIGHTS], *[new_v[n] for n in _WEIGHTS])
```

```python
import functools
import math

import jax
import jax.numpy as jnp
from jax import lax
from jax.experimental import pallas as pl
from jax.experimental.pallas import tpu as pltpu

F32 = jnp.float32
BF16 = jnp.bfloat16
MESH = pl.DeviceIdType.MESH

LANES = 128
VMEM_LIMIT_BYTES = 56 * 1024 * 1024

EPS = 1e-6
D_MODEL = 1024
HEAD_DIM = 64
SB_HEADS = 8
SWA_HEADS = 8
SWA_KV_HEADS = 2
WINDOW = 128
GDN_K_HEADS = 8
GDN_V_HEADS = 16
GDN_HEAD_DIM = 128
GDN_CONV = 4
GDN_CHUNK = 64
SB_W = SB_HEADS * HEAD_DIM
SWA_QW = SWA_HEADS * HEAD_DIM
SWA_KVW = SWA_KV_HEADS * HEAD_DIM
GDN_KW = GDN_K_HEADS * GDN_HEAD_DIM
GDN_VW = GDN_V_HEADS * GDN_HEAD_DIM
GDN_CONV_W = 2 * GDN_KW + GDN_VW

ADAM_LR = 0.001
ADAM_B1 = 0.9
ADAM_B2 = 0.999
ADAM_EPS = 1e-08
ADAM_WD = 0.01
ADAM_STEP = 10

NEG = -1e30


def _params(*sem):
    return pltpu.CompilerParams(dimension_semantics=sem or None, vmem_limit_bytes=VMEM_LIMIT_BYTES)


def _tile(n, cap, align=LANES):
    if n <= cap:
        return n
    for t in range(cap - cap % align, 0, -align):
        if n % t == 0:
            return t
    return n


N_CHIPS = 4
Q = "q"


def _opnd(x):
    return x if isinstance(x, tuple) else (x, ())


def _mm(a, b, *, name, ta=False, tb=False, out_dtype=F32, res=None, scale=1.0, out_q=False, into=None,
        tm=1024, tn=1024, tk=1024):
    (a_arr, a_lead), (b_arr, b_lead) = _opnd(a), _opnd(b)
    (k_a, m) = a_arr.shape[-2:] if ta else a_arr.shape[-2:][::-1]
    (n, k_b) = b_arr.shape[-2:] if tb else b_arr.shape[-2:][::-1]
    if into is not None:
        out_arr, out_lead = into
        out_q, out_dtype = Q in out_lead, out_arr.dtype
    else:
        out_lead = (Q,) if out_q else ()
    red_q = (Q in a_lead or Q in b_lead) and not out_q
    kq = min(k_a, k_b)
    assert (k_a == k_b) or (red_q and max(k_a, k_b) == N_CHIPS * kq), (a_arr.shape, b_arr.shape)
    tm, tn, tk = _tile(m, tm), _tile(n, tn), _tile(kq, tk)
    nk = kq // tk
    ksteps = nk * (N_CHIPS if red_q else 1)
    dims = (((0 if ta else 1,), (1 if tb else 0,)), ((), ()))
    has_res = res is not None

    def body(*refs):
        a_ref, b_ref = refs[0], refs[1]
        o_ref, acc_ref = refs[-2], refs[-1]
        k = pl.program_id(3)

        @pl.when(k == 0)
        def _():
            acc_ref[...] = jnp.zeros_like(acc_ref)

        acc_ref[...] += lax.dot_general(a_ref[...].astype(BF16), b_ref[...].astype(BF16), dims,
                                        preferred_element_type=F32)

        @pl.when(k == ksteps - 1)
        def _():
            r = acc_ref[...]
            if scale != 1.0:
                r = r * scale
            if has_res:
                r = r + refs[2][...].astype(F32)
            o_ref[...] = r.astype(o_ref.dtype)

    def spec(lead, blk, pos):
        def index(s, i, j, k):
            kk = k % nk if (red_q and Q in lead) else k
            quarter = s if out_q else k // nk
            return tuple(quarter if l == Q else l for l in lead) + pos(i, j, kk)
        return pl.BlockSpec((None,) * len(lead) + blk, index)

    a_spec = spec(a_lead, (tk, tm), lambda i, j, k: (k, i)) if ta else spec(a_lead, (tm, tk), lambda i, j, k: (i, k))
    b_spec = spec(b_lead, (tn, tk), lambda i, j, k: (j, k)) if tb else spec(b_lead, (tk, tn), lambda i, j, k: (k, j))
    o_spec = spec(out_lead, (tm, tn), lambda i, j, k: (i, j))
    in_specs, args = [a_spec, b_spec], [a_arr, b_arr]
    if has_res:
        r_arr, r_lead = _opnd(res)
        in_specs.append(spec(r_lead, (tm, tn), lambda i, j, k: (i, j)))
        args.append(r_arr)
    aliases = {}
    if into is not None:
        in_specs.append(pl.BlockSpec(memory_space=pl.ANY))
        args.append(out_arr)
        aliases = {len(args) - 1: 0}
        out_shape = jax.ShapeDtypeStruct(out_arr.shape, out_arr.dtype)
    else:
        out_shape = jax.ShapeDtypeStruct(((N_CHIPS,) if out_q else ()) + (m, n), out_dtype)
    return pl.pallas_call(
        body, name=name, grid=(N_CHIPS if out_q else 1, m // tm, n // tn, ksteps), in_specs=in_specs, out_specs=o_spec,
        out_shape=out_shape, scratch_shapes=[pltpu.VMEM((tm, tn), F32)], input_output_aliases=aliases,
        compiler_params=_params("parallel", "parallel", "parallel", "arbitrary"),
    )(*args)


def _row_spec(r, tm):
    if isinstance(r, tuple):
        arr, width, cb = r
        return arr, pl.BlockSpec((tm, width), lambda i, cb=cb: (i, cb))
    return r, pl.BlockSpec((tm, r.shape[1]), lambda i: (i, 0))


def _const_spec(c):
    return pl.BlockSpec(c.shape, lambda i: (0,) * c.ndim)


def _row_fwd(fn, rows, consts, outs, *, name, tm=256):
    tm = _tile(_row_spec(rows[0], tm)[0].shape[0], tm, 8)
    arrs, specs = zip(*[_row_spec(r, tm) for r in rows])
    t = arrs[0].shape[0]
    nr, nc = len(rows), len(consts)

    def body(*refs):
        vals = [r[...].astype(F32) for r in refs[:nr + nc]]
        res = fn(*vals)
        for o_ref, v in zip(refs[nr + nc:], res):
            o_ref[...] = v.astype(o_ref.dtype)

    out = pl.pallas_call(
        body, name=name, grid=(t // tm,),
        in_specs=list(specs) + [_const_spec(c) for c in consts],
        out_specs=[pl.BlockSpec((tm, w), lambda i: (i, 0)) for w, _ in outs],
        out_shape=[jax.ShapeDtypeStruct((t, w), dt) for w, dt in outs],
        compiler_params=_params("parallel"),
    )(*arrs, *consts)
    return list(out)


def _row_bwd(fn, rows, consts, cts, row_grads, const_grads, *, name, tm=256):
    tm = _tile(_row_spec(rows[0], tm)[0].shape[0], tm, 8)
    arrs, specs = zip(*[_row_spec(r, tm) for r in rows])
    ct_arrs, ct_specs = zip(*[_row_spec(r, tm) for r in cts])
    t = arrs[0].shape[0]
    nr, nc, nt = len(rows), len(consts), len(cts)
    n_in = nr + nc + nt

    def body(*refs):
        vals = [r[...].astype(F32) for r in refs[:nr + nc]]
        ctv = tuple(r[...].astype(F32) for r in refs[nr + nc:n_in])
        _, vjp = jax.vjp(fn, *vals)
        g = vjp(ctv)
        outs = refs[n_in:]
        for (idx, _), o_ref in zip(row_grads, outs[:len(row_grads)]):
            o_ref[...] = g[idx].astype(o_ref.dtype)
        first = pl.program_id(0) == 0
        for ci, o_ref in zip(const_grads, outs[len(row_grads):]):
            @pl.when(first)
            def _(o_ref=o_ref):
                o_ref[...] = jnp.zeros_like(o_ref)

            o_ref[...] += g[nr + ci]

    widths = [(_row_spec(rows[idx], tm)[1].block_shape[1], dt) for idx, dt in row_grads]
    out = pl.pallas_call(
        body, name=name, grid=(t // tm,),
        in_specs=list(specs) + [_const_spec(c) for c in consts] + list(ct_specs),
        out_specs=[pl.BlockSpec((tm, w), lambda i: (i, 0)) for w, _ in widths]
        + [_const_spec(consts[ci]) for ci in const_grads],
        out_shape=[jax.ShapeDtypeStruct((t, w), dt) for w, dt in widths]
        + [jax.ShapeDtypeStruct(consts[ci].shape, F32) for ci in const_grads],
        compiler_params=_params("arbitrary"),
    )(*arrs, *consts, *ct_arrs)
    return list(out)


def _rms(x, g):
    return x * lax.rsqrt(jnp.mean(x * x, axis=-1, keepdims=True) + EPS) * g


def _f_rms(h, g):
    return (_rms(h, g),)


def _f_rms_res(h, g):
    return (_rms(h, g), h)


def _f_swiglu(g, u):
    return (g * jax.nn.sigmoid(g) * u,)


def _f_ple(h, gl, pp):
    return (h + jax.nn.sigmoid(gl) * pp,)


def _f_gdn_out(o, z, gain):
    outs = []
    for hd in range(GDN_V_HEADS):
        sl = slice(hd * GDN_HEAD_DIM, (hd + 1) * GDN_HEAD_DIM)
        oh, zh = o[:, sl], z[:, sl]
        outs.append(_rms(oh, gain) * (zh * jax.nn.sigmoid(zh)))
    return (jnp.concatenate(outs, axis=1),)


def _loss_head(y, target, *, name, tm=512):
    t, d = y.shape
    tm = _tile(t, tm, 8)

    def body(y_ref, t_ref, dy_ref, l_ref):
        @pl.when(pl.program_id(0) == 0)
        def _():
            l_ref[...] = jnp.zeros_like(l_ref)

        e = y_ref[...] - t_ref[...]
        dy_ref[...] = e * (1.0 / d)
        l_ref[...] += jnp.sum(e * e) * (0.5 / d)

    dy, l = pl.pallas_call(
        body, name=name, grid=(t // tm,),
        in_specs=[pl.BlockSpec((tm, d), lambda i: (i, 0))] * 2,
        out_specs=[pl.BlockSpec((tm, d), lambda i: (i, 0)), pl.BlockSpec((8, LANES), lambda i: (0, 0))],
        out_shape=[jax.ShapeDtypeStruct((t, d), F32), jax.ShapeDtypeStruct((8, LANES), F32)],
        compiler_params=_params("arbitrary"),
    )(y, target)
    return dy, l[0, 0]


def _dg(a, b, ca, cb):
    nb = a.ndim - 2
    batch = tuple(range(nb))
    return lax.dot_general(a, b, (((ca + nb,), (cb + nb,)), (batch, batch)), preferred_element_type=F32)


def _b(x):
    return x.astype(BF16)


@jax.custom_vjp
def _bdot(a, b):
    return _dg(_b(a), _b(b), 1, 0)


def _bdot_fwd(a, b):
    return _bdot(a, b), (a, b)


def _bdot_bwd(r, ct):
    a, b = r
    return _dg(_b(ct), _b(b), 1, 1), _dg(_b(a), _b(ct), 0, 0)


_bdot.defvjp(_bdot_fwd, _bdot_bwd)


@jax.custom_vjp
def _bdot_nt(a, b):
    return _dg(_b(a), _b(b), 1, 1)


def _bdot_nt_fwd(a, b):
    return _bdot_nt(a, b), (a, b)


def _bdot_nt_bwd(r, ct):
    a, b = r
    return _dg(_b(ct), _b(b), 1, 0), _dg(_b(ct), _b(a), 0, 0)


_bdot_nt.defvjp(_bdot_nt_fwd, _bdot_nt_bwd)


@jax.custom_vjp
def _bdot_tn(a, b):
    return _dg(_b(a), _b(b), 0, 0)


def _bdot_tn_fwd(a, b):
    return _bdot_tn(a, b), (a, b)


def _bdot_tn_bwd(r, ct):
    a, b = r
    return _dg(_b(b), _b(ct), 1, 1), _dg(_b(a), _b(ct), 1, 0)


_bdot_tn.defvjp(_bdot_tn_fwd, _bdot_tn_bwd)


def _two(x):
    hi = x.astype(BF16)
    return hi, (x - hi.astype(F32)).astype(BF16)


def _dg3(a, b, ca, cb):
    (ah, al), (bh, bl) = _two(a), _two(b)
    return _dg(ah, bh, ca, cb) + (_dg(ah, bl, ca, cb) + _dg(al, bh, ca, cb))


@jax.custom_vjp
def _hdot(a, b):
    return _dg3(a, b, 1, 0)


def _hdot_fwd(a, b):
    return _hdot(a, b), (a, b)


def _hdot_bwd(r, ct):
    a, b = r
    return _dg3(ct, b, 1, 1), _dg3(a, ct, 0, 0)


_hdot.defvjp(_hdot_fwd, _hdot_bwd)


def _split_dot(x, u):
    hi, lo = _two(x)
    return _dg(hi, u, 1, 0) + _dg(lo, u, 1, 0)


@jax.custom_vjp
def _ldot(l01, x):
    hi, lo = _two(x)
    l01 = l01.astype(BF16)
    return _dg(l01, hi, 1, 0) + _dg(l01, lo, 1, 0)


def _ldot_fwd(l01, x):
    return _ldot(l01, x), l01


def _ldot_bwd(l01, ct):
    hi, lo = _two(ct)
    l01b = l01.astype(BF16)
    return jnp.zeros_like(l01), _dg(l01b, hi, 0, 0) + _dg(l01b, lo, 0, 0)


_ldot.defvjp(_ldot_fwd, _ldot_bwd)


SB_BLK = 128
SB_KEYS = 512
SB_PAIRS = 2
SB_SCALE = HEAD_DIM ** -0.5


def _log_sigmoid(z):
    return jnp.minimum(z, 0.0) - jnp.log(1.0 + jnp.exp(-jnp.abs(z)))


def _sb_consts(t):
    kb = min(SB_KEYS, t)
    nh = 2 * SB_PAIRS
    lane = lax.broadcasted_iota(jnp.int32, (nh, SB_BLK, kb), 2)
    row = lax.broadcasted_iota(jnp.int32, (nh, SB_BLK, kb), 1)
    ur = lax.broadcasted_iota(jnp.int32, (kb, kb), 0)
    uc = lax.broadcasted_iota(jnp.int32, (kb, kb), 1)
    return kb, nh, lane, row, ur, uc


def _sb_heads(x):
    head0 = lax.broadcasted_iota(jnp.int32, (x.shape[0], LANES), 1) < HEAD_DIM
    out = []
    for p in range(SB_PAIRS):
        blk = x[:, p * LANES:(p + 1) * LANES]
        out += [jnp.where(head0, blk, 0.0), jnp.where(head0, 0.0, blk)]
    return jnp.stack(out)


def _sb_pairs(x):
    return jnp.stack([x[:, (h // 2) * LANES:(h // 2 + 1) * LANES] for h in range(2 * SB_PAIRS)])


def _sb_merge(x):
    head0 = lax.broadcasted_iota(jnp.int32, (x.shape[1], LANES), 1) < HEAD_DIM
    return jnp.concatenate([jnp.where(head0, x[2 * p], x[2 * p + 1]) for p in range(SB_PAIRS)], axis=1)


def _sb_rows_dot(x, u):
    nh, rows, k = x.shape
    return _split_dot(x.reshape(nh * rows, k), u).reshape(nh, rows, k)


def _sb_fwd(proj, *, name):
    t = proj.shape[0]
    nb = t // SB_BLK
    width = SB_PAIRS * LANES
    ng = SB_W // width

    def body(q_ref, k_ref, v_ref, o_ref, r_ref):
        i = pl.program_id(1)
        kb, nh, lane, row, ur, uc = _sb_consts(t)
        u_suffix = (ur >= uc).astype(BF16)
        qh = _b(_sb_heads(q_ref[...]))
        diag = (i * SB_BLK) // kb

        def block(j, carry, masked):
            acc, car = carry
            keys = pl.ds(pl.multiple_of(j * kb, kb), kb)
            kj, vj = _b(_sb_pairs(k_ref[keys, :])), _b(_sb_pairs(v_ref[keys, :]))
            z = _dg(qh, kj, 1, 1) * SB_SCALE
            ls = _log_sigmoid(z)
            lk = ls - z
            if masked:
                causal = (j * kb + lane) < (i * SB_BLK + row)
                lk = jnp.where(causal, lk, 0.0)
            suf = _sb_rows_dot(lk, u_suffix) + car
            w = jnp.exp(ls + (suf - lk))
            if masked:
                w = jnp.where(causal, w, 0.0)
            return acc + _dg(_b(w), vj, 1, 0), suf[:, :, 0:1]

        zero = (jnp.zeros((nh, SB_BLK, LANES), F32), jnp.zeros((nh, SB_BLK, 1), F32))
        carry = block(diag, zero, True)
        acc, car = lax.fori_loop(0, diag, lambda s, c: block(diag - 1 - s, c, False), carry)
        o_ref[...] = _sb_merge(acc)
        r_ref[...] = _sb_merge(jnp.broadcast_to(car, (nh, SB_BLK, LANES)))

    return pl.pallas_call(
        body, name=name, grid=(ng, nb),
        in_specs=[pl.BlockSpec((SB_BLK, width), lambda p, i: (i, p)),
                  pl.BlockSpec((t, width), lambda p, i: (0, ng + p)),
                  pl.BlockSpec((t, width), lambda p, i: (0, 2 * ng + p))],
        out_specs=[pl.BlockSpec((SB_BLK, width), lambda p, i: (i, p))] * 2,
        out_shape=[jax.ShapeDtypeStruct((t, SB_W), F32)] * 2,
        compiler_params=_params("parallel", "arbitrary"),
    )(proj, proj, proj)


def _sb_bwd(proj, rtot, dout, *, name):
    t = proj.shape[0]
    nb = t // SB_BLK
    width = SB_PAIRS * LANES
    ng = SB_W // width

    def body(q_ref, k_ref, v_ref, r_ref, do_ref, dq_ref, dk_ref, dv_ref):
        i = pl.program_id(1)
        kb, nh, lane, row, ur, uc = _sb_consts(t)
        u_incl = (ur <= uc).astype(BF16)
        u_excl = (ur < uc).astype(BF16)
        q, do = q_ref[...], do_ref[...]
        qh, doh = _b(_sb_heads(q)), _b(_sb_heads(do))
        qb, dob = _b(_sb_pairs(q)), _b(_sb_pairs(do))
        rh = jnp.min(_sb_heads(r_ref[...]), axis=2, keepdims=True)
        diag = (i * SB_BLK) // kb

        @pl.when(i == 0)
        def _():
            dk_ref[...] = jnp.zeros_like(dk_ref)
            dv_ref[...] = jnp.zeros_like(dv_ref)

        def block(j, carry, masked):
            dq_acc, clk, ce = carry
            keys = pl.ds(pl.multiple_of(j * kb, kb), kb)
            kj, vj = _b(_sb_pairs(k_ref[keys, :])), _b(_sb_pairs(v_ref[keys, :]))
            z = _dg(qh, kj, 1, 1) * SB_SCALE
            ls = _log_sigmoid(z)
            lk = ls - z
            if masked:
                causal = (j * kb + lane) < (i * SB_BLK + row)
                lk = jnp.where(causal, lk, 0.0)
            pre = _sb_rows_dot(lk, u_incl) + clk
            w = jnp.exp(ls + (rh - pre))
            if masked:
                w = jnp.where(causal, w, 0.0)
            e = _dg(doh, vj, 1, 1) * w
            pre_e = _sb_rows_dot(e, u_excl) + ce
            sig = jnp.exp(ls)
            dz = (e * (1.0 - sig) - sig * pre_e) * SB_SCALE
            if masked:
                dz = jnp.where(causal, dz, 0.0)
            dzb = _b(dz)
            dk_ref[keys, :] += _sb_merge(_dg(dzb, qb, 0, 0))
            dv_ref[keys, :] += _sb_merge(_dg(_b(w), dob, 0, 0))
            return dq_acc + _dg(dzb, kj, 1, 0), pre[:, :, kb - 1:], pre_e[:, :, kb - 1:] + e[:, :, kb - 1:]

        zero = (jnp.zeros((nh, SB_BLK, LANES), F32), jnp.zeros((nh, SB_BLK, 1), F32), jnp.zeros((nh, SB_BLK, 1), F32))
        carry = lax.fori_loop(0, diag, lambda j, c: block(j, c, False), zero)
        dq_acc, _, _ = block(diag, carry, True)
        dq_ref[...] = _sb_merge(dq_acc)

    blk = pl.BlockSpec((SB_BLK, width), lambda p, i: (i, p))
    whole = pl.BlockSpec((t, width), lambda p, i: (0, p))
    return pl.pallas_call(
        body, name=name, grid=(ng, nb),
        in_specs=[blk,
                  pl.BlockSpec((t, width), lambda p, i: (0, ng + p)),
                  pl.BlockSpec((t, width), lambda p, i: (0, 2 * ng + p)),
                  blk, blk],
        out_specs=[blk, whole, whole],
        out_shape=[jax.ShapeDtypeStruct((t, SB_W), F32)] * 3,
        compiler_params=_params("arbitrary", "arbitrary"),
    )(proj, proj, proj, rtot, dout)


SWA_G = SWA_HEADS // SWA_KV_HEADS


def _swa_heads(first, qs, ks, vs, qg, kg, sinks):
    qi = lax.broadcasted_iota(jnp.int32, (WINDOW, 2 * WINDOW), 0)
    kj = lax.broadcasted_iota(jnp.int32, (WINDOW, 2 * WINDOW), 1)
    dist = qi + WINDOW - kj
    valid = (dist >= 0) & (dist < WINDOW) & (jnp.logical_not(first) | (kj >= WINDOW))
    distf = dist.astype(F32)
    outs = []
    for hk in range(SWA_KV_HEADS):
        kn = _rms(ks[hk], kg)
        for g in range(SWA_G):
            h = hk * SWA_G + g
            slope = 2.0 ** (-8.0 * (h + 1) / SWA_HEADS)
            s = _bdot_nt(_rms(qs[h], qg), kn) * (HEAD_DIM ** -0.5)
            s = jnp.where(valid, s - slope * distf, NEG)
            m = lax.stop_gradient(jnp.maximum(jnp.max(s, axis=1, keepdims=True), sinks[h]))
            p = jnp.exp(s - m)
            den = jnp.sum(p, axis=1, keepdims=True) + jnp.exp(sinks[h] - m)
            outs.append(_bdot(p / den, vs[hk]))
    return tuple(outs)


def _swa_split(q, kp, kc, vp, vc, sk):
    qs = [q[:, h * HEAD_DIM:(h + 1) * HEAD_DIM] for h in range(SWA_HEADS)]
    k2, v2 = jnp.concatenate([kp, kc], axis=0), jnp.concatenate([vp, vc], axis=0)
    ks = [k2[:, h * HEAD_DIM:(h + 1) * HEAD_DIM] for h in range(SWA_KV_HEADS)]
    vs = [v2[:, h * HEAD_DIM:(h + 1) * HEAD_DIM] for h in range(SWA_KV_HEADS)]
    sinks = [sk[:, h:h + 1] for h in range(SWA_HEADS)]
    return qs, ks, vs, sinks


def _swa_specs(t):
    qcb = (3 * SB_W) // SWA_QW
    kcb = (3 * SB_W + SWA_QW) // SWA_KVW
    prev = lambda i: jnp.maximum(i - 1, 0)
    return [pl.BlockSpec((WINDOW, SWA_QW), lambda i: (i, qcb)),
            pl.BlockSpec((WINDOW, SWA_KVW), lambda i: (prev(i), kcb)),
            pl.BlockSpec((WINDOW, SWA_KVW), lambda i: (i, kcb)),
            pl.BlockSpec((WINDOW, SWA_KVW), lambda i: (prev(i), kcb + 1)),
            pl.BlockSpec((WINDOW, SWA_KVW), lambda i: (i, kcb + 1)),
            pl.BlockSpec((1, HEAD_DIM), lambda i: (0, 0)),
            pl.BlockSpec((1, HEAD_DIM), lambda i: (0, 0)),
            pl.BlockSpec((1, SWA_HEADS), lambda i: (0, 0))]


def _swa_fwd(proj, qg, kg, sinks, *, name):
    t = proj.shape[0]

    def body(q_ref, kp_ref, kc_ref, vp_ref, vc_ref, qg_ref, kg_ref, sk_ref, o_ref):
        first = pl.program_id(0) == 0
        qs, ks, vs, sk = _swa_split(q_ref[...], kp_ref[...], kc_ref[...], vp_ref[...], vc_ref[...], sk_ref[...])
        o_ref[...] = jnp.concatenate(_swa_heads(first, qs, ks, vs, qg_ref[...], kg_ref[...], sk), axis=1)

    return pl.pallas_call(
        body, name=name, grid=(t // WINDOW,), in_specs=_swa_specs(t),
        out_specs=pl.BlockSpec((WINDOW, SWA_QW), lambda i: (i, 0)),
        out_shape=jax.ShapeDtypeStruct((t, SWA_QW), F32),
        compiler_params=_params("parallel"),
    )(proj, proj, proj, proj, proj, qg, kg, sinks)


def _swa_bwd(proj, qg, kg, sinks, dout, *, name):
    t = proj.shape[0]

    def body(q_ref, kp_ref, kc_ref, vp_ref, vc_ref, qg_ref, kg_ref, sk_ref, do_ref,
             dq_ref, dk_ref, dv_ref, dqg_ref, dkg_ref, dsk_ref):
        i = pl.program_id(0)
        first = i == 0

        @pl.when(first)
        def _():
            for r in (dk_ref, dv_ref, dqg_ref, dkg_ref, dsk_ref):
                r[...] = jnp.zeros_like(r)

        qs, ks, vs, sk = _swa_split(q_ref[...], kp_ref[...], kc_ref[...], vp_ref[...], vc_ref[...], sk_ref[...])
        do = do_ref[...]
        cts = tuple(do[:, h * HEAD_DIM:(h + 1) * HEAD_DIM] for h in range(SWA_HEADS))
        _, vjp = jax.vjp(functools.partial(_swa_heads, first), qs, ks, vs, qg_ref[...], kg_ref[...], sk)
        dqs, dks, dvs, dqg, dkg, dsk = vjp(cts)
        dq_ref[...] = jnp.concatenate(dqs, axis=1)
        dk2, dv2 = jnp.concatenate(dks, axis=1), jnp.concatenate(dvs, axis=1)
        cur = pl.ds(pl.multiple_of(i * WINDOW, WINDOW), WINDOW)
        prv = pl.ds(pl.multiple_of(jnp.maximum(i - 1, 0) * WINDOW, WINDOW), WINDOW)
        dk_ref[prv, :] += dk2[:WINDOW]
        dv_ref[prv, :] += dv2[:WINDOW]
        dk_ref[cur, :] += dk2[WINDOW:]
        dv_ref[cur, :] += dv2[WINDOW:]
        dqg_ref[...] += dqg
        dkg_ref[...] += dkg
        dsk_ref[...] += jnp.concatenate(dsk, axis=1)

    whole = lambda shape: pl.BlockSpec(shape, lambda i: (0, 0))
    return pl.pallas_call(
        body, name=name, grid=(t // WINDOW,),
        in_specs=_swa_specs(t) + [pl.BlockSpec((WINDOW, SWA_QW), lambda i: (i, 0))],
        out_specs=[pl.BlockSpec((WINDOW, SWA_QW), lambda i: (i, 0)), whole((t, SWA_KVW)), whole((t, SWA_KVW)),
                   whole((1, HEAD_DIM)), whole((1, HEAD_DIM)), whole((1, SWA_HEADS))],
        out_shape=[jax.ShapeDtypeStruct((t, SWA_QW), F32), jax.ShapeDtypeStruct((t, SWA_KVW), F32),
                   jax.ShapeDtypeStruct((t, SWA_KVW), F32), jax.ShapeDtypeStruct((1, HEAD_DIM), F32),
                   jax.ShapeDtypeStruct((1, HEAD_DIM), F32), jax.ShapeDtypeStruct((1, SWA_HEADS), F32)],
        compiler_params=_params("arbitrary"),
    )(proj, proj, proj, proj, proj, qg, kg, sinks, dout)


CONV_CB = 512
CONV_TM = 512
HALO = 8


def _conv_pre(x_ref, h_ref, w_ref, i):
    halo = jnp.where(i > 0, h_ref[...], 0.0)
    xe = jnp.concatenate([halo, x_ref[...]], axis=0)
    tm = x_ref.shape[0]
    w = w_ref[...]
    c = sum(w[k:k + 1, :] * xe[HALO - (GDN_CONV - 1) + k:HALO - (GDN_CONV - 1) + k + tm] for k in range(GDN_CONV))
    return c, xe


def _conv_specs(tm, cb):
    return [pl.BlockSpec((tm, cb), lambda c, i: (i, c)),
            pl.BlockSpec((HALO, cb), lambda c, i: (jnp.maximum(i * (tm // HALO) - 1, 0), c)),
            pl.BlockSpec((GDN_CONV, cb), lambda c, i: (0, c))]


def _conv_fwd(x, w, dact=None, *, name):
    t, ch = x.shape
    tm, cb = _tile(t, CONV_TM), _tile(ch, CONV_CB)

    def body(*refs):
        x_ref, h_ref, w_ref = refs[:3]
        c, _ = _conv_pre(x_ref, h_ref, w_ref, pl.program_id(1))
        sig = jax.nn.sigmoid(c)
        if dact is None:
            refs[3][...] = c * sig
        else:
            refs[4][...] = refs[3][...] * (sig * (1.0 + c * (1.0 - sig)))

    tile = pl.BlockSpec((tm, cb), lambda c, i: (i, c))
    extra = () if dact is None else (dact,)
    return pl.pallas_call(
        body, name=name, grid=(ch // cb, t // tm),
        in_specs=_conv_specs(tm, cb) + [tile] * len(extra), out_specs=tile,
        out_shape=jax.ShapeDtypeStruct((t, ch), F32),
        compiler_params=_params("parallel", "parallel"),
    )(x, x, w, *extra)


def _conv_bwd(x, w, dc, *, name):
    t, ch = x.shape
    tm, cb = _tile(t, CONV_TM), _tile(ch, CONV_CB)
    nt = t // tm

    def body(x_ref, h_ref, w_ref, dc_ref, nh_ref, dx_ref, dw_ref):
        i = pl.program_id(1)

        @pl.when(i == 0)
        def _():
            dw_ref[...] = jnp.zeros_like(dw_ref)

        halo = jnp.where(i > 0, h_ref[...], 0.0)
        xe = jnp.concatenate([halo, x_ref[...]], axis=0)
        dc = dc_ref[...]
        dce = jnp.concatenate([dc, jnp.where(i < nt - 1, nh_ref[...], 0.0)], axis=0)
        w = w_ref[...]
        last = GDN_CONV - 1
        dx_ref[...] = sum(w[k:k + 1, :] * dce[last - k:last - k + tm] for k in range(GDN_CONV))
        dw_ref[...] += jnp.concatenate(
            [jnp.sum(dc * xe[HALO - last + k:HALO - last + k + tm], axis=0, keepdims=True) for k in range(GDN_CONV)],
            axis=0)

    tile = pl.BlockSpec((tm, cb), lambda c, i: (i, c))
    nxt = pl.BlockSpec((HALO, cb), lambda c, i: (jnp.minimum((i + 1) * (tm // HALO), t // HALO - 1), c))
    return pl.pallas_call(
        body, name=name, grid=(ch // cb, nt),
        in_specs=_conv_specs(tm, cb) + [tile, nxt],
        out_specs=[tile, pl.BlockSpec((GDN_CONV, cb), lambda c, i: (0, c))],
        out_shape=[jax.ShapeDtypeStruct((t, ch), F32), jax.ShapeDtypeStruct((GDN_CONV, ch), F32)],
        compiler_params=_params("parallel", "arbitrary"),
    )(x, x, w, dc, dc)


def _gdn_chunk(qraw, kraw, v, bl, a, alog, dtb, state):
    c, d = GDN_CHUNK, GDN_HEAD_DIM
    nh = qraw.shape[0]
    ri = lax.broadcasted_iota(jnp.int32, (nh, c, c), 1)
    ci = lax.broadcasted_iota(jnp.int32, (nh, c, c), 2)
    incl, strict = ri >= ci, ri > ci
    q = qraw * lax.rsqrt(jnp.sum(qraw * qraw, axis=-1, keepdims=True) + EPS) * (d ** -0.5)
    k = kraw * lax.rsqrt(jnp.sum(kraw * kraw, axis=-1, keepdims=True) + EPS)
    beta = jax.nn.sigmoid(bl)
    g = -jnp.exp(alog) * jax.nn.softplus(a + dtb)
    gc = _ldot(incl.astype(F32), jnp.broadcast_to(g, (nh, c, d)))
    gcm = gc[:, :, :c]
    decay = jnp.exp(jnp.where(incl, gcm - jnp.swapaxes(gcm, 1, 2), NEG))
    eg = jnp.exp(gc)
    kbeta = k * beta
    x = -jnp.where(strict, _bdot_nt(kbeta, k) * decay, 0.0)
    tinv = (ri == ci).astype(F32) + x
    pw = x
    for _ in range(int(math.log2(c)) - 1):
        pw = _hdot(pw, pw)
        tinv = tinv + _hdot(tinv, pw)
    u = _hdot(tinv, v * beta)
    w = _hdot(tinv, kbeta * eg)
    attn = jnp.where(incl, _bdot_nt(q, k) * decay, 0.0)
    glast = gc[:, c - 1:c, :]
    v_new = u - _bdot(w, state)
    o = _bdot(q * eg, state) + _bdot(attn, v_new)
    state = state * jnp.exp(glast) + _bdot_tn(k * jnp.exp(glast - gc), v_new)
    return o, state


GDN_REP = GDN_V_HEADS // GDN_K_HEADS
GDN_HB = 8


def _gdn_pick(vals, kh, r):
    ba, alog, dtb = vals
    lane = lax.broadcasted_iota(jnp.int32, ba.shape, 1)
    hv = kh * GDN_REP + r
    bl = jnp.sum(jnp.where(lane == hv, ba, 0.0), axis=1, keepdims=True)
    a = jnp.sum(jnp.where(lane == GDN_V_HEADS + hv, ba, 0.0), axis=1, keepdims=True)
    lane1 = lax.broadcasted_iota(jnp.int32, alog.shape, 1)
    al = jnp.sum(jnp.where(lane1 == hv, alog, 0.0), axis=1, keepdims=True)
    db = jnp.sum(jnp.where(lane1 == hv, dtb, 0.0), axis=1, keepdims=True)
    return bl, a, al, db


def _gdn_stack(qs, ks, vs, small, j):
    d = GDN_HEAD_DIM
    per = [[], [], [], [], [], [], []]
    for hh in range(GDN_HB):
        q, k = qs[:, hh * d:(hh + 1) * d], ks[:, hh * d:(hh + 1) * d]
        for r in range(GDN_REP):
            col = (hh * GDN_REP + r) * d
            for lst, val in zip(per, (q, k, vs[:, col:col + d]) + _gdn_pick(small, j * GDN_HB + hh, r)):
                lst.append(val)
    return tuple(jnp.stack(lst) for lst in per)


def _gdn_specs(nchunk, rev):
    c, d = GDN_CHUNK, GDN_HEAD_DIM
    at = (lambda n: nchunk - 1 - n) if rev else (lambda n: n)
    ng = GDN_K_HEADS // GDN_HB
    return at, [pl.BlockSpec((c, GDN_HB * d), lambda n, j: (at(n), j)),
                pl.BlockSpec((c, GDN_HB * d), lambda n, j: (at(n), ng + j)),
                pl.BlockSpec((c, GDN_HB * GDN_REP * d), lambda n, j: (at(n), ng + j)),
                pl.BlockSpec((c, 2 * GDN_V_HEADS), lambda n, j: (at(n), 0)),
                pl.BlockSpec((1, GDN_V_HEADS), lambda n, j: (0, 0)),
                pl.BlockSpec((1, GDN_V_HEADS), lambda n, j: (0, 0))]


def _gdn_fwd(act, ba, alog, dtb, *, name):
    t = act.shape[0]
    c, d = GDN_CHUNK, GDN_HEAD_DIM
    nchunk = t // c
    at, specs = _gdn_specs(nchunk, False)

    def body(q_ref, k_ref, v_ref, ba_ref, al_ref, db_ref, o_ref, s_ref, state):
        n, j = pl.program_id(0), pl.program_id(1)
        heads = pl.ds(j * GDN_HB, GDN_HB)

        @pl.when(n == 0)
        def _():
            state[heads] = jnp.zeros((GDN_HB, GDN_REP, d, d), F32)

        s_in = state[heads]
        s_ref[...] = s_in
        args = _gdn_stack(q_ref[...], k_ref[...], v_ref[...], (ba_ref[...], al_ref[...], db_ref[...]), j)
        o, s_new = _gdn_chunk(*args, s_in.reshape(GDN_HB * GDN_REP, d, d))
        o_ref[...] = jnp.concatenate([o[b] for b in range(GDN_HB * GDN_REP)], axis=1)
        state[heads] = s_new.reshape(GDN_HB, GDN_REP, d, d)

    return pl.pallas_call(
        body, name=name, grid=(nchunk, GDN_K_HEADS // GDN_HB), in_specs=specs,
        out_specs=[pl.BlockSpec((c, GDN_HB * GDN_REP * d), lambda n, j: (n, j)),
                   pl.BlockSpec((None, GDN_HB, GDN_REP, d, d), lambda n, j: (n, j, 0, 0, 0))],
        out_shape=[jax.ShapeDtypeStruct((t, GDN_VW), F32),
                   jax.ShapeDtypeStruct((nchunk, GDN_K_HEADS, GDN_REP, d, d), F32)],
        scratch_shapes=[pltpu.VMEM((GDN_K_HEADS, GDN_REP, d, d), F32)],
        compiler_params=_params("arbitrary", "arbitrary"),
    )(act, act, act, ba, alog, dtb)


def _gdn_bwd(act, ba, alog, dtb, states, dout, *, name):
    t = act.shape[0]
    c, d = GDN_CHUNK, GDN_HEAD_DIM
    nchunk = t // c
    at, specs = _gdn_specs(nchunk, True)

    def body(q_ref, k_ref, v_ref, ba_ref, al_ref, db_ref, s_ref, do_ref,
             dq_ref, dk_ref, dv_ref, dba_ref, dal_ref, ddb_ref, dstate):
        n, j = pl.program_id(0), pl.program_id(1)

        @pl.when(n == 0)
        def _():
            dstate[pl.ds(j * GDN_HB, GDN_HB)] = jnp.zeros((GDN_HB, GDN_REP, d, d), F32)

        @pl.when((n == 0) & (j == 0))
        def _():
            dal_ref[...] = jnp.zeros_like(dal_ref)
            ddb_ref[...] = jnp.zeros_like(ddb_ref)

        @pl.when(j == 0)
        def _():
            dba_ref[...] = jnp.zeros_like(dba_ref)

        heads = pl.ds(j * GDN_HB, GDN_HB)
        nh = GDN_HB * GDN_REP
        args = _gdn_stack(q_ref[...], k_ref[...], v_ref[...], (ba_ref[...], al_ref[...], db_ref[...]), j)
        _, vjp = jax.vjp(_gdn_chunk, *args, s_ref[...].reshape(nh, d, d))
        do = do_ref[...]
        do = jnp.stack([do[:, b * d:(b + 1) * d] for b in range(nh)])
        gq, gk, gv, gbl, ga, gal, gdb, gs = vjp((do, dstate[heads].reshape(nh, d, d)))
        dstate[heads] = gs.reshape(GDN_HB, GDN_REP, d, d)
        dq_ref[...] = jnp.concatenate([gq[GDN_REP * hh] + gq[GDN_REP * hh + 1] for hh in range(GDN_HB)], axis=1)
        dk_ref[...] = jnp.concatenate([gk[GDN_REP * hh] + gk[GDN_REP * hh + 1] for hh in range(GDN_HB)], axis=1)
        dv_ref[...] = jnp.concatenate([gv[b] for b in range(nh)], axis=1)
        lane = lax.broadcasted_iota(jnp.int32, (c, 2 * GDN_V_HEADS), 1)
        lane1 = lax.broadcasted_iota(jnp.int32, (1, GDN_V_HEADS), 1)
        dba = jnp.zeros((c, 2 * GDN_V_HEADS), F32)
        dal = jnp.zeros((1, GDN_V_HEADS), F32)
        ddb = jnp.zeros((1, GDN_V_HEADS), F32)
        for b in range(nh):
            hv = j * nh + b
            dba = dba + jnp.where(lane == hv, gbl[b], 0.0) + jnp.where(lane == GDN_V_HEADS + hv, ga[b], 0.0)
            dal = dal + jnp.where(lane1 == hv, gal[b], 0.0)
            ddb = ddb + jnp.where(lane1 == hv, gdb[b], 0.0)
        dba_ref[...] += dba
        dal_ref[...] += dal
        ddb_ref[...] += ddb

    small = pl.BlockSpec((1, GDN_V_HEADS), lambda n, j: (0, 0))
    return pl.pallas_call(
        body, name=name, grid=(nchunk, GDN_K_HEADS // GDN_HB),
        in_specs=specs + [pl.BlockSpec((None, GDN_HB, GDN_REP, d, d), lambda n, j: (at(n), j, 0, 0, 0)),
                          pl.BlockSpec((c, GDN_HB * GDN_REP * d), lambda n, j: (at(n), j))],
        out_specs=[pl.BlockSpec((c, GDN_HB * d), lambda n, j: (at(n), j)),
                   pl.BlockSpec((c, GDN_HB * d), lambda n, j: (at(n), j)),
                   pl.BlockSpec((c, GDN_HB * GDN_REP * d), lambda n, j: (at(n), j)),
                   pl.BlockSpec((c, 2 * GDN_V_HEADS), lambda n, j: (at(n), 0)),
                   small, small],
        out_shape=[jax.ShapeDtypeStruct((t, GDN_KW), F32), jax.ShapeDtypeStruct((t, GDN_KW), F32),
                   jax.ShapeDtypeStruct((t, GDN_VW), F32), jax.ShapeDtypeStruct((t, 2 * GDN_V_HEADS), F32),
                   jax.ShapeDtypeStruct((1, GDN_V_HEADS), F32), jax.ShapeDtypeStruct((1, GDN_V_HEADS), F32)],
        scratch_shapes=[pltpu.VMEM((GDN_K_HEADS, GDN_REP, d, d), F32)],
        compiler_params=_params("arbitrary", "arbitrary"),
    )(act, act, act, ba, alog, dtb, states, dout)


N_DEV = 8
ANY = pl.BlockSpec(memory_space=pl.ANY)


def _coords():
    return lax.axis_index("x"), lax.axis_index("y"), lax.axis_index("c")


def _other_chips(x, y):
    return [(1 - x, y), (x, 1 - y), (1 - x, 1 - y)]


def _remote(src, dst, send_sems, recv_sems, k, to):
    return pltpu.make_async_remote_copy(src_ref=src, dst_ref=dst, send_sem=send_sems.at[k], recv_sem=recv_sems.at[k],
                                        device_id=to, device_id_type=MESH)


def _dma_sems(n):
    return [pltpu.SemaphoreType.DMA((n,)), pltpu.SemaphoreType.DMA((n,))]


def _gather_quarters(parts, *, name):
    na = len(parts)

    def body(*refs):
        ins, outs = refs[:na], refs[na:2 * na]
        send_sems, recv_sems, local_sems = refs[2 * na:]
        x, y, c = _coords()
        sibling = (x, y, 1 - c)
        chips = _other_chips(x, y)
        mine, first, passed = [], [], []
        for a, (x_ref, out_ref) in enumerate(zip(ins, outs)):
            mine.append(pltpu.make_async_copy(x_ref, out_ref.at[2 * x + y], local_sems.at[a]))
            for j, (cx, cy) in enumerate(chips):
                first.append(_remote(x_ref.at[c], out_ref.at[2 * x + y, c], send_sems, recv_sems, 6 * a + j, (cx, cy, c)))
                passed.append(_remote(out_ref.at[2 * cx + cy, c], out_ref.at[2 * cx + cy, c], send_sems, recv_sems,
                                      6 * a + 3 + j, sibling))
        for cp in mine + first:
            cp.start()
        for a, (x_ref, out_ref) in enumerate(zip(ins, outs)):
            for j, (cx, cy) in enumerate(chips):
                _remote(x_ref.at[c], out_ref.at[2 * cx + cy, c], send_sems, recv_sems, 6 * a + j, (cx, cy, c)).wait_recv()
                passed[3 * a + j].start()
        for a, (x_ref, out_ref) in enumerate(zip(ins, outs)):
            for j, (cx, cy) in enumerate(chips):
                _remote(x_ref.at[c], out_ref.at[2 * cx + cy, 1 - c], send_sems, recv_sems, 6 * a + 3 + j,
                        sibling).wait_recv()
        for cp in first + passed:
            cp.wait_send()
        for cp in mine:
            cp.wait()

    return pl.pallas_call(
        body, name=name, in_specs=[ANY] * na, out_specs=[ANY] * na,
        out_shape=[jax.ShapeDtypeStruct((N_CHIPS,) + p.shape, p.dtype) for p in parts],
        scratch_shapes=_dma_sems(6 * na) + [pltpu.SemaphoreType.DMA((na,))],
    )(*parts)


def _swap_halves(grads, *, name):
    na = len(grads)

    def body(*refs):
        ins, outs = refs[:na], refs[na:2 * na]
        send_sems, recv_sems = refs[2 * na:]
        x, y, c = _coords()
        sends = [_remote(g_ref.at[j, 1 - c], o_ref.at[j], send_sems, recv_sems, N_CHIPS * a + j, (x, y, 1 - c))
                 for a, (g_ref, o_ref) in enumerate(zip(ins, outs)) for j in range(N_CHIPS)]
        for cp in sends:
            cp.start()
        for cp in sends:
            cp.wait()

    return pl.pallas_call(
        body, name=name, in_specs=[ANY] * na, out_specs=[ANY] * na,
        out_shape=[jax.ShapeDtypeStruct((N_CHIPS,) + g.shape[2:], g.dtype) for g in grads],
        scratch_shapes=_dma_sems(N_CHIPS * na),
    )(*grads)


def _scatter_quarters(pairs, *, name):
    na = len(pairs)

    def body(*refs):
        ins, outs = refs[:na], refs[na:4 * na]
        send_sems, recv_sems = refs[4 * na:]
        x, y, c = _coords()
        sends = [_remote(p_ref.at[2 * cx + cy], outs[3 * a + j], send_sems, recv_sems, 3 * a + j, (cx, cy, c))
                 for a, p_ref in enumerate(ins) for j, (cx, cy) in enumerate(_other_chips(x, y))]
        for cp in sends:
            cp.start()
        for cp in sends:
            cp.wait()

    out = pl.pallas_call(
        body, name=name, in_specs=[ANY] * na, out_specs=[ANY] * (3 * na),
        out_shape=[jax.ShapeDtypeStruct(p.shape[1:], p.dtype) for p in pairs for _ in range(3)],
        scratch_shapes=_dma_sems(3 * na),
    )(*pairs)
    return [out[3 * a:3 * a + 3] for a in range(na)]


def _share_halves(tots, *, name):
    na = len(tots)

    def body(*refs):
        ins, outs = refs[:na], refs[na:2 * na]
        send_sems, recv_sems = refs[2 * na:]
        x, y, c = _coords()
        sends = [_remote(t_ref, o_ref, send_sems, recv_sems, a, (x, y, 1 - c))
                 for a, (t_ref, o_ref) in enumerate(zip(ins, outs))]
        for cp in sends:
            cp.start()
        for cp in sends:
            cp.wait()

    return pl.pallas_call(
        body, name=name, in_specs=[ANY] * na, out_specs=[ANY] * na,
        out_shape=[jax.ShapeDtypeStruct(t.shape, t.dtype) for t in tots],
        scratch_shapes=_dma_sems(na),
    )(*tots)


def _gather_all(vec, *, name):
    m, w = vec.shape

    def body(x_ref, out_ref, send_sems, recv_sems, local_sem):
        x, y, c = _coords()
        me, sibling = (x, y, c), (x, y, 1 - c)
        chips = _other_chips(x, y)

        def rows(px, py, pc):
            return out_ref.at[pl.ds((4 * px + 2 * py + pc) * m, m), :]

        def copy(k, block, to, src=None):
            return _remote(rows(*block) if src is None else src, rows(*block), send_sems, recv_sems, k, to)

        mine = pltpu.make_async_copy(x_ref, rows(*me), local_sem)
        mine.start()
        first = [copy(0, me, sibling, src=x_ref)]
        first += [copy(1 + j, me, (*chip, c), src=x_ref) for j, chip in enumerate(chips)]
        for cp in first:
            cp.start()
        passed = [copy(4 + j, (*chip, c), sibling) for j, chip in enumerate(chips)]
        for j, chip in enumerate(chips):
            copy(1 + j, (*chip, c), me).wait_recv()
            passed[j].start()
        copy(0, sibling, me).wait_recv()
        for j, chip in enumerate(chips):
            copy(4 + j, (*chip, 1 - c), me).wait_recv()
        for cp in first + passed:
            cp.wait_send()
        mine.wait()

    vm = pl.BlockSpec(memory_space=pltpu.VMEM)
    return pl.pallas_call(
        body, name=name, in_specs=[vm], out_specs=vm, out_shape=jax.ShapeDtypeStruct((N_DEV * m, w), vec.dtype),
        scratch_shapes=_dma_sems(7) + [pltpu.SemaphoreType.DMA(())],
    )(vec)


def _sum_blocks(allv, n, *, name):
    m = allv.shape[0] // n

    def body(a_ref, o_ref):
        acc = a_ref[0:m, :]
        for d in range(1, n):
            acc = acc + a_ref[d * m:(d + 1) * m, :]
        o_ref[...] = acc

    return pl.pallas_call(body, name=name, out_shape=jax.ShapeDtypeStruct((m, allv.shape[1]), allv.dtype))(allv)


EW_BLOCK_BYTES = 1 << 20


def _ew_rows(rows, w):
    return _tile(rows, max(8, (EW_BLOCK_BYTES // (4 * w)) // 8 * 8), 8)


def _add_pair(g, got, c, *, name):
    _, _, rows, w = g.shape
    tr = _ew_rows(rows, w)

    def body(c_ref, g_ref, got_ref, o_ref):
        o_ref[...] = (g_ref[...] + got_ref[...]).astype(o_ref.dtype)

    blk = pl.BlockSpec((None, tr, w), lambda q, i, c_ref: (q, i, 0))
    return pl.pallas_call(
        body, name=name,
        grid_spec=pltpu.PrefetchScalarGridSpec(
            num_scalar_prefetch=1, grid=(N_CHIPS, rows // tr),
            in_specs=[pl.BlockSpec((None, None, tr, w), lambda q, i, c_ref: (q, c_ref[0], i, 0)), blk], out_specs=blk),
        out_shape=jax.ShapeDtypeStruct(got.shape, BF16),
        compiler_params=_params("parallel", "parallel"),
    )(c, g, got)


def _add_chips(pair, recv, chip, *, name):
    _, rows, w = pair.shape
    tr = _ew_rows(rows, w)

    def body(chip_ref, p_ref, r0_ref, r1_ref, r2_ref, o_ref):
        f = lambda r: r[...].astype(F32)
        o_ref[...] = ((f(p_ref) + f(r0_ref)) + f(r1_ref)) + f(r2_ref)

    blk = pl.BlockSpec((tr, w), lambda i, chip_ref: (i, 0))
    return pl.pallas_call(
        body, name=name,
        grid_spec=pltpu.PrefetchScalarGridSpec(
            num_scalar_prefetch=1, grid=(rows // tr,),
            in_specs=[pl.BlockSpec((None, tr, w), lambda i, chip_ref: (chip_ref[0], i, 0)), blk, blk, blk], out_specs=blk),
        out_shape=jax.ShapeDtypeStruct((rows, w), F32),
        compiler_params=_params("parallel"),
    )(chip, pair, *recv)


def _adamw_math(w, g, m, v):
    nm = ADAM_B1 * m + (1.0 - ADAM_B1) * g
    nv = ADAM_B2 * v + (1.0 - ADAM_B2) * (g * g)
    m_hat = nm / (1.0 - ADAM_B1 ** ADAM_STEP)
    v_hat = nv / (1.0 - ADAM_B2 ** ADAM_STEP)
    return -ADAM_LR * (m_hat / (jnp.sqrt(v_hat) + ADAM_EPS) + ADAM_WD * w), nm, nv


def _adamw(w, g, m, v, *, name):
    shape = w.shape
    last = shape[-1]
    w2, g2, m2, v2 = (a.reshape(-1, last) for a in (w, g, m, v))
    rows = w2.shape[0]
    tm = _ew_rows(rows, last)

    def body(w_ref, g_ref, m_ref, v_ref, d_ref, nm_ref, nv_ref):
        d_ref[...], nm_ref[...], nv_ref[...] = _adamw_math(w_ref[...], g_ref[...], m_ref[...], v_ref[...])

    spec = pl.BlockSpec((tm, last), lambda i: (i, 0))
    out = jax.ShapeDtypeStruct((rows, last), F32)
    d, nm, nv = pl.pallas_call(
        body, name=name, grid=(rows // tm,), in_specs=[spec] * 4, out_specs=[spec] * 3, out_shape=[out] * 3,
        compiler_params=_params("parallel"),
    )(w2, g2, m2, v2)
    return d.reshape(shape), nm.reshape(shape), nv.reshape(shape)


def _adamw_halves(w, m, v, mine, theirs, c, *, name):
    _, rows, wd = w.shape
    tr = _ew_rows(rows, wd)

    def body(c_ref, w_ref, m_ref, v_ref, a_ref, b_ref, g_ref, d_ref, nm_ref, nv_ref):
        g = jnp.where(pl.program_id(0) == c_ref[0], a_ref[...], b_ref[...])
        g_ref[...] = g
        d_ref[...], nm_ref[...], nv_ref[...] = _adamw_math(w_ref[...], g, m_ref[...], v_ref[...])

    full = pl.BlockSpec((None, tr, wd), lambda hf, i, c_ref: (hf, i, 0))
    half = pl.BlockSpec((tr, wd), lambda hf, i, c_ref: (i, 0))
    out = jax.ShapeDtypeStruct(w.shape, F32)
    return pl.pallas_call(
        body, name=name,
        grid_spec=pltpu.PrefetchScalarGridSpec(num_scalar_prefetch=1, grid=(2, rows // tr),
                                               in_specs=[full] * 3 + [half] * 2, out_specs=[full] * 4),
        out_shape=[out] * 4,
        compiler_params=_params("parallel", "parallel"),
    )(c, w, m, v, mine, theirs)


def _join_quarters(q, *, name):
    _, rows, n = q.shape
    tr = _tile(rows, 256, 16)

    def body(q_ref, o_ref):
        o_ref[...] = jnp.concatenate([q_ref[s] for s in range(N_CHIPS)], axis=1)

    return pl.pallas_call(
        body, name=name, grid=(rows // tr,),
        in_specs=[pl.BlockSpec((N_CHIPS, tr, n), lambda i: (0, i, 0))],
        out_specs=pl.BlockSpec((tr, N_CHIPS * n), lambda i: (i, 0)),
        out_shape=jax.ShapeDtypeStruct((rows, N_CHIPS * n), q.dtype),
        compiler_params=_params("parallel"),
    )(q)


def _split_quarters(full, *, name):
    rows, n4 = full.shape
    n = n4 // N_CHIPS
    tr = _tile(rows, 256, 16)

    def body(x_ref, o_ref):
        x = x_ref[...]
        for s in range(N_CHIPS):
            o_ref[s] = x[:, s * n:(s + 1) * n]

    return pl.pallas_call(
        body, name=name, grid=(rows // tr,),
        in_specs=[pl.BlockSpec((tr, n4), lambda i: (i, 0))],
        out_specs=pl.BlockSpec((N_CHIPS, tr, n), lambda i: (0, i, 0)),
        out_shape=jax.ShapeDtypeStruct((N_CHIPS, rows, n), full.dtype),
        compiler_params=_params("parallel"),
    )(full)


_WEIGHTS = ['ffn_norm', 'ffn_w_gate', 'ffn_w_up', 'ffn_w_down', 'mix_norm', 'att_w_in', 'att_q_norm', 'att_k_norm',
            'att_sinks', 'att_w_out', 'gdn_w_in', 'gdn_conv_w', 'gdn_a_log', 'gdn_dt_bias', 'gdn_out_norm', 'gdn_w_out',
            'ple_norm', 'ple_w_gate', 'ple_w_proj']
_BIG = ['ffn_w_gate', 'ffn_w_up', 'ffn_w_down', 'att_w_in', 'att_w_out', 'gdn_w_in', 'gdn_w_out', 'ple_w_gate',
        'ple_w_proj']
_SMALL_CUT = {'ffn_norm': 2, 'gdn_conv_w': 2}
_WHOLE = ['mix_norm', 'att_q_norm', 'att_k_norm', 'att_sinks', 'gdn_a_log', 'gdn_dt_bias', 'gdn_out_norm', 'ple_norm']
PACK_W = 1024
SMALL_ROW_MULT = 8


def _halves(a):
    return a.reshape(2, -1, a.shape[-1])


def _from_quarters(blk, axis):
    full = jnp.moveaxis(blk, 0, axis)
    shp = list(full.shape)
    shp[axis:axis + 2] = [shp[axis] * shp[axis + 1]]
    return full.reshape(shp)


def _to_quarters(full, axis):
    shp = list(full.shape)
    shp[axis:axis + 1] = [N_CHIPS, shp[axis] // N_CHIPS]
    return jnp.moveaxis(full.reshape(shp), axis, 0)


def _pack(parts, row_mult):
    flat = jnp.concatenate(parts, axis=-1)
    n = flat.shape[-1]
    rows = -(-n // (PACK_W * row_mult)) * row_mult
    return jnp.pad(flat, [(0, rows * PACK_W - n)]).reshape(rows, PACK_W)


def _unpack(flat, shapes):
    lead = flat.shape[:-2]
    flat = flat.reshape(lead + (-1,))
    out, off = [], 0
    for shp in shapes:
        n = math.prod(shp)
        out.append(flat[..., off:off + n].reshape(lead + tuple(shp)))
        off += n
    return out


def _ffn_fwd(h, gain, wg, wu, wd, at, tag):
    t = h.shape[0]
    fq = wd.shape[-2]
    lead = (Q,) + at
    hn, = _row_fwd(_f_rms, [h], [gain], [(D_MODEL, BF16)], name=f"{tag}_norm")
    g = _mm(hn, (wg, lead), out_q=True, out_dtype=BF16, name=f"{tag}_gate").reshape(N_CHIPS * t, fq)
    u = _mm(hn, (wu, lead), out_q=True, out_dtype=BF16, name=f"{tag}_up").reshape(N_CHIPS * t, fq)
    a, = _row_fwd(_f_swiglu, [g, u], [], [(fq, BF16)], name=f"{tag}_act")
    a = a.reshape(N_CHIPS, t, fq)
    out = _mm((a, (Q,)), (wd, lead), res=h, scale=0.5, name=f"{tag}_down")
    return out, (h, hn, g, u, a)


def _ffn_bwd(dout, saved, gain, wg, wu, wd, at, grads, tag):
    h, hn, g, u, a = saved
    t = h.shape[0]
    fq = wd.shape[-2]
    lead = (Q,) + at
    da = _mm(dout, (wd, lead), tb=True, scale=0.5, out_q=True, out_dtype=BF16, name=f"{tag}_d_act").reshape(N_CHIPS * t, fq)
    dg, du = _row_bwd(_f_swiglu, [g, u], [], [da], [(0, BF16), (1, BF16)], [], name=f"{tag}_d_gate_up")
    dg, du = dg.reshape(N_CHIPS, t, fq), du.reshape(N_CHIPS, t, fq)
    g_gate, g_up, g_down = grads
    g_down = _mm((a, (Q,)), dout, ta=True, scale=0.5, into=(g_down, lead), name=f"{tag}_dw_down")
    g_gate = _mm(hn, (dg, (Q,)), ta=True, into=(g_gate, lead), name=f"{tag}_dw_gate")
    g_up = _mm(hn, (du, (Q,)), ta=True, into=(g_up, lead), name=f"{tag}_dw_up")
    dhn = _mm((dg, (Q,)), (wg, lead), tb=True, name=f"{tag}_d_norm_gate")
    dhn = _mm((du, (Q,)), (wu, lead), tb=True, res=dhn, name=f"{tag}_d_norm_up")
    dh, dgain = _row_bwd(_f_rms_res, [h], [gain], [dhn, dout], [(0, F32)], [0], name=f"{tag}_d_in")
    return dh, dgain, (g_gate, g_up, g_down)


def _att_fwd(h, gain, w_in, qg, kg, sinks, w_out):
    hn, = _row_fwd(_f_rms, [h], [gain], [(D_MODEL, BF16)], name="att_norm")
    proj = _mm(hn, w_in, name="att_in")
    a, rtot = _sb_fwd(proj, name="att_sb")
    b = _swa_fwd(proj, qg, kg, sinks, name="att_swa")
    out = _mm(a, (w_out, (0,)), res=h, name="att_out_sb")
    out = _mm(b, (w_out, (1,)), res=out, name="att_out_swa")
    return out, (h, hn, proj, a, rtot, b)


def _att_bwd(dout, saved, gain, w_in, qg, kg, sinks, w_out):
    h, hn, proj, a, rtot, b = saved
    da = _mm(dout, (w_out, (0,)), tb=True, name="att_d_sb")
    db = _mm(dout, (w_out, (1,)), tb=True, name="att_d_swa")
    dw_out = jnp.zeros(w_out.shape, F32)
    dw_out = _mm(a, dout, ta=True, into=(dw_out, (0,)), name="att_dw_out_sb")
    dw_out = _mm(b, dout, ta=True, into=(dw_out, (1,)), name="att_dw_out_swa")
    dq, dk, dv = _sb_bwd(proj, rtot, da, name="att_sb_bwd")
    dqb, dkb, dvb, dqg, dkg, dsk = _swa_bwd(proj, qg, kg, sinks, db, name="att_swa_bwd")
    dproj = jnp.concatenate([dq, dk, dv, dqb, dkb, dvb], axis=1)
    dw_in = _mm(hn, dproj, ta=True, name="att_dw_in")
    dhn = _mm(dproj, w_in, tb=True, name="att_d_norm")
    dh, dgain = _row_bwd(_f_rms_res, [h], [gain], [dhn, dout], [(0, F32)], [0], name="att_d_in")
    return dh, dgain, dw_in, dqg, dkg, dsk, dw_out


def _gdn_layer_fwd(h, gain, w_in, conv_w, alog, dtb, out_gain, w_out):
    w_qkv, w_z, w_ba = w_in[:, :GDN_CONV_W], w_in[:, GDN_CONV_W:GDN_CONV_W + GDN_VW], w_in[:, GDN_CONV_W + GDN_VW:]
    hn, = _row_fwd(_f_rms, [h], [gain], [(D_MODEL, BF16)], name="gdn_norm")
    pq = _mm(hn, w_qkv, name="gdn_in_qkv")
    pz = _mm(hn, w_z, name="gdn_in_z")
    ba = _mm(hn, w_ba, name="gdn_in_ba")
    act = _conv_fwd(pq, conv_w, name="gdn_conv")
    o, states = _gdn_fwd(act, ba, alog, dtb, name="gdn_rule")
    y, = _row_fwd(_f_gdn_out, [o, pz], [out_gain], [(GDN_VW, BF16)], name="gdn_gate")
    out = _mm(y, w_out, res=h, name="gdn_out")
    return out, (h, hn, pq, pz, ba, act, o, states, y, (w_qkv, w_z, w_ba))


def _gdn_layer_bwd(dout, saved, gain, conv_w, alog, dtb, out_gain, w_out):
    h, hn, pq, pz, ba, act, o, states, y, (w_qkv, w_z, w_ba) = saved
    dy = _mm(dout, w_out, tb=True, name="gdn_d_gate")
    dw_out = _mm(y, dout, ta=True, name="gdn_dw_out")
    do, dpz, dout_gain = _row_bwd(_f_gdn_out, [o, pz], [out_gain], [dy], [(0, F32), (1, F32)], [0], name="gdn_gate_bwd")
    dq, dk, dv, dba, dal, ddb = _gdn_bwd(act, ba, alog, dtb, states, do, name="gdn_rule_bwd")
    dact = jnp.concatenate([dq, dk, dv], axis=1)
    dc = _conv_fwd(pq, conv_w, dact, name="gdn_conv_d_pre")
    dpq, dconv = _conv_bwd(pq, conv_w, dc, name="gdn_conv_bwd")
    dw_in = jnp.concatenate([_mm(hn, dpq, ta=True, name="gdn_dw_qkv"), _mm(hn, dpz, ta=True, name="gdn_dw_z"),
                             _mm(hn, dba, ta=True, name="gdn_dw_ba")], axis=1)
    dhn = _mm(dpq, w_qkv, tb=True, name="gdn_d_norm_qkv")
    dhn = _mm(dpz, w_z, tb=True, res=dhn, name="gdn_d_norm_z")
    dhn = _mm(dba, w_ba, tb=True, res=dhn, name="gdn_d_norm_ba")
    dh, dgain = _row_bwd(_f_rms_res, [h], [gain], [dhn, dout], [(0, F32)], [0], name="gdn_d_in")
    return dh, dgain, dw_in, dconv, dal, ddb, dout_gain, dw_out


def _ple_fwd(h, gain, w_gate, w_proj, pe, tag):
    hn, = _row_fwd(_f_rms, [h], [gain], [(D_MODEL, BF16)], name=f"{tag}_norm")
    gl = _mm(hn, w_gate, name=f"{tag}_gate")
    pp = _mm(pe, w_proj, name=f"{tag}_proj")
    out, = _row_fwd(_f_ple, [h, gl, pp], [], [(D_MODEL, F32)], name=f"{tag}_mix")
    return out, (h, hn, gl, pp)


def _ple_bwd(dout, saved, gain, w_gate, pe, tag):
    h, hn, gl, pp = saved
    dha, dgl, dpp = _row_bwd(_f_ple, [h, gl, pp], [], [dout], [(0, F32), (1, BF16), (2, BF16)], [], name=f"{tag}_mix_bwd")
    dw_gate = _mm(hn, dgl, ta=True, name=f"{tag}_dw_gate")
    dw_proj = _mm(pe, dpp, ta=True, name=f"{tag}_dw_proj")
    dhn = _mm(dgl, w_gate, tb=True, name=f"{tag}_d_norm")
    dh, dgain = _row_bwd(_f_rms_res, [h], [gain], [dhn, dha], [(0, F32)], [0], name=f"{tag}_d_in")
    return dh, dgain, dw_gate, dw_proj


def kernel(x, p, ffn_norm, ffn_w_gate, ffn_w_up, ffn_w_down, mix_norm, att_w_in, att_q_norm, att_k_norm, att_sinks, att_w_out, gdn_w_in, gdn_conv_w, gdn_a_log, gdn_dt_bias, gdn_out_norm, gdn_w_out, ple_norm, ple_w_gate, ple_w_proj, loss_target, m_ffn_norm, m_ffn_w_gate, m_ffn_w_up, m_ffn_w_down, m_mix_norm, m_att_w_in, m_att_q_norm, m_att_k_norm, m_att_sinks, m_att_w_out, m_gdn_w_in, m_gdn_conv_w, m_gdn_a_log, m_gdn_dt_bias, m_gdn_out_norm, m_gdn_w_out, m_ple_norm, m_ple_w_gate, m_ple_w_proj, v_ffn_norm, v_ffn_w_gate, v_ffn_w_up, v_ffn_w_down, v_mix_norm, v_att_w_in, v_att_q_norm, v_att_k_norm, v_att_sinks, v_att_w_out, v_gdn_w_in, v_gdn_conv_w, v_gdn_a_log, v_gdn_dt_bias, v_gdn_out_norm, v_gdn_w_out, v_ple_norm, v_ple_w_gate, v_ple_w_proj):
    arg = dict(locals())
    cx, cy, cc = _coords()
    chip = (2 * cx + cy).astype(jnp.int32).reshape(1)
    core = cc.astype(jnp.int32).reshape(1)
    n_layers = ffn_norm.shape[0]

    gathered = _gather_quarters([_halves(arg[n].astype(BF16)) for n in _BIG], name="gather_weights")
    wq = {n: g.reshape((N_CHIPS,) + arg[n].shape) for n, g in zip(_BIG, gathered)}
    wt = {}
    wt['att_w_in'] = _join_quarters(wq['att_w_in'][:, 0], name="att_w_in_join")
    wt['gdn_w_in'] = _join_quarters(wq['gdn_w_in'][:, 0], name="gdn_w_in_join")
    wt['att_w_out'] = wq['att_w_out'].reshape(2, SB_W, D_MODEL)
    wt['gdn_w_out'] = wq['gdn_w_out'].reshape(GDN_VW, D_MODEL)
    wt['ple_w_gate'] = _from_quarters(wq['ple_w_gate'], 1)
    wt['ple_w_proj'] = _from_quarters(wq['ple_w_proj'], 2)

    small_names = list(_SMALL_CUT)
    small_shapes = [arg[n].shape for n in small_names]
    svec = _pack([arg[n].reshape(-1) for n in small_names], SMALL_ROW_MULT)
    srows = svec.shape[0]
    sall = _gather_all(svec, name="gather_gains").reshape(N_CHIPS, 2, srows, PACK_W)[:, 0]
    for n, q in zip(small_names, _unpack(sall, small_shapes)):
        wt[n] = _from_quarters(q, _SMALL_CUT[n])
    row = lambda v: v.reshape(1, -1)

    h = x[0]
    tape = []
    ffn_w = (wq['ffn_w_gate'], wq['ffn_w_up'], wq['ffn_w_down'])
    for i in range(n_layers):
        j = i // 2
        h, s0 = _ffn_fwd(h, row(wt['ffn_norm'][i, 0]), *ffn_w, (i, 0), f"ffn{i}a")
        if i % 2 == 0:
            h, sm = _att_fwd(h, row(mix_norm[i]), wt['att_w_in'], att_q_norm[j:j + 1], att_k_norm[j:j + 1],
                             att_sinks[j:j + 1], wt['att_w_out'])
        else:
            h, sm = _gdn_layer_fwd(h, row(mix_norm[i]), wt['gdn_w_in'], wt['gdn_conv_w'][j], gdn_a_log[j:j + 1],
                                   gdn_dt_bias[j:j + 1], gdn_out_norm[j:j + 1], wt['gdn_w_out'])
        h, s1 = _ffn_fwd(h, row(wt['ffn_norm'][i, 1]), *ffn_w, (i, 1), f"ffn{i}b")
        h, sp = _ple_fwd(h, row(ple_norm[i]), wt['ple_w_gate'][i], wt['ple_w_proj'][i], p[i, 0], f"ple{i}")
        tape.append((s0, sm, s1, sp))

    dh, loss_local = _loss_head(h, loss_target[0], name="loss_head")
    loss = lax.psum(loss_local, ("x", "y", "c"))

    gr = {}
    ffn_g = tuple(jnp.zeros(w.shape, F32) for w in ffn_w)
    d_ffn_norm = [[None, None] for _ in range(n_layers)]
    d_mix, d_ple_norm, d_ple_gate, d_ple_proj = [None] * n_layers, [None] * n_layers, [None] * n_layers, [None] * n_layers
    for i in reversed(range(n_layers)):
        j = i // 2
        s0, sm, s1, sp = tape[i]
        dh, d_ple_norm[i], d_ple_gate[i], d_ple_proj[i] = _ple_bwd(dh, sp, row(ple_norm[i]), wt['ple_w_gate'][i], p[i, 0],
                                                                   f"ple{i}")
        dh, d_ffn_norm[i][1], ffn_g = _ffn_bwd(dh, s1, row(wt['ffn_norm'][i, 1]), *ffn_w, (i, 1), ffn_g, f"ffn{i}b")
        if i % 2 == 0:
            (dh, d_mix[i], dw_in, gr['att_q_norm'], gr['att_k_norm'], gr['att_sinks'],
             dw_out) = _att_bwd(dh, sm, row(mix_norm[i]), wt['att_w_in'], att_q_norm[j:j + 1],
                                att_k_norm[j:j + 1], att_sinks[j:j + 1], wt['att_w_out'])
            gr['att_w_in'] = _split_quarters(dw_in, name="att_dw_in_split")
            gr['att_w_out'] = dw_out
        else:
            (dh, d_mix[i], dw_in, dconv, gr['gdn_a_log'], gr['gdn_dt_bias'], gr['gdn_out_norm'],
             dw_out) = _gdn_layer_bwd(dh, sm, row(mix_norm[i]), wt['gdn_conv_w'][j], gdn_a_log[j:j + 1],
                                      gdn_dt_bias[j:j + 1], gdn_out_norm[j:j + 1], wt['gdn_w_out'])
            gr['gdn_w_in'] = _split_quarters(dw_in, name="gdn_dw_in_split")
            gr['gdn_w_out'] = dw_out
            gr['gdn_conv_w'] = dconv[None]
        dh, d_ffn_norm[i][0], ffn_g = _ffn_bwd(dh, s0, row(wt['ffn_norm'][i, 0]), *ffn_w, (i, 0), ffn_g, f"ffn{i}a")
    grad_x = dh[None]

    gr['ffn_w_gate'], gr['ffn_w_up'], gr['ffn_w_down'] = ffn_g
    gr['ple_w_gate'] = _to_quarters(jnp.stack(d_ple_gate), 1)
    gr['ple_w_proj'] = _to_quarters(jnp.stack(d_ple_proj), 2)
    gr['ffn_norm'] = jnp.stack([jnp.stack([d_ffn_norm[i][k][0] for k in range(2)]) for i in range(n_layers)])
    gr['mix_norm'] = jnp.concatenate(d_mix, axis=0)
    gr['ple_norm'] = jnp.concatenate(d_ple_norm, axis=0)

    gq = [gr[n].reshape((N_CHIPS, 2, -1, arg[n].shape[-1])) for n in _BIG]
    got = _swap_halves(gq, name="grad_swap_halves")
    pairs = [_add_pair(g, o, core, name=f"grad_add_pair_{n}") for n, g, o in zip(_BIG, gq, got)]
    recv = _scatter_quarters(pairs, name="grad_scatter")
    tots = [_add_chips(pr, rc, chip, name=f"grad_add_chips_{n}") for n, pr, rc in zip(_BIG, pairs, recv)]
    theirs = _share_halves(tots, name="grad_share")

    whole_shapes = [arg[n].shape for n in _WHOLE]
    cut_full_shapes = [gr[n].shape for n in small_names]
    gvec = _pack([gr[n].reshape(-1) for n in _WHOLE + small_names], SMALL_ROW_MULT)
    gall = _sum_blocks(_gather_all(gvec, name="gather_small_grads"), N_DEV, name="sum_small_grads")
    parts = _unpack(gall, whole_shapes + cut_full_shapes)
    gsum = dict(zip(_WHOLE, parts))
    for n, g in zip(small_names, parts[len(_WHOLE):]):
        gsum[n] = lax.dynamic_index_in_dim(_to_quarters(g, _SMALL_CUT[n]), chip[0], axis=0, keepdims=False)

    delta, new_m, new_v = {}, {}, {}
    for n, mine, other in zip(_BIG, tots, theirs):
        res = _adamw_halves(_halves(arg[n]), _halves(arg["m_" + n]), _halves(arg["v_" + n]), mine, other, core,
                            name=f"adamw_{n}")
        gsum[n], delta[n], new_m[n], new_v[n] = (r.reshape(arg[n].shape) for r in res)
    for n in _WHOLE + small_names:
        delta[n], new_m[n], new_v[n] = _adamw(arg[n], gsum[n], arg["m_" + n], arg["v_" + n], name=f"adamw_{n}")
    return (loss, grad_x, *[gsum[n] for n in _WEIGHTS], *[delta[n] for n in _WEIGHTS],
            *[new_m[n] for n in _WEIGHTS], *[new_v[n] for n in _WEIGHTS])
```

```python
import functools
import math

import jax
import jax.numpy as jnp
from jax import lax
from jax.experimental import pallas as pl
from jax.experimental.pallas import tpu as pltpu

F32 = jnp.float32
BF16 = jnp.bfloat16
MESH = pl.DeviceIdType.MESH

LANES = 128
VMEM_LIMIT_BYTES = 56 * 1024 * 1024

EPS = 1e-6
D_MODEL = 1024
HEAD_DIM = 64
SB_HEADS = 8
SWA_HEADS = 8
SWA_KV_HEADS = 2
WINDOW = 128
GDN_K_HEADS = 8
GDN_V_HEADS = 16
GDN_HEAD_DIM = 128
GDN_CONV = 4
GDN_CHUNK = 64
SB_W = SB_HEADS * HEAD_DIM
SWA_QW = SWA_HEADS * HEAD_DIM
SWA_KVW = SWA_KV_HEADS * HEAD_DIM
GDN_KW = GDN_K_HEADS * GDN_HEAD_DIM
GDN_VW = GDN_V_HEADS * GDN_HEAD_DIM
GDN_CONV_W = 2 * GDN_KW + GDN_VW

ADAM_LR = 0.001
ADAM_B1 = 0.9
ADAM_B2 = 0.999
ADAM_EPS = 1e-08
ADAM_WD = 0.01
ADAM_STEP = 10

NEG = -1e30


def _params(*sem):
    return pltpu.CompilerParams(dimension_semantics=sem or None, vmem_limit_bytes=VMEM_LIMIT_BYTES)


def _tile(n, cap, align=LANES):
    if n <= cap:
        return n
    for t in range(cap - cap % align, 0, -align):
        if n % t == 0:
            return t
    return n


N_CHIPS = 4
MM_VMEM_BUDGET_BYTES = 40 * 1024 * 1024
Q = "q"


def _opnd(x):
    return x if isinstance(x, tuple) else (x, ())


def _mm(a, b, *, name, ta=False, tb=False, out_dtype=F32, res=None, scale=1.0, out_q=False, into=None,
        tm=None, tn=1024, tk=1024):
    (a_arr, a_lead), (b_arr, b_lead) = _opnd(a), _opnd(b)
    (k_a, m) = a_arr.shape[-2:] if ta else a_arr.shape[-2:][::-1]
    (n, k_b) = b_arr.shape[-2:] if tb else b_arr.shape[-2:][::-1]
    if into is not None:
        out_arr, out_lead = into
        out_q, out_dtype = Q in out_lead, out_arr.dtype
    else:
        out_lead = (Q,) if out_q else ()
    red_q = (Q in a_lead or Q in b_lead) and not out_q
    kq = min(k_a, k_b)
    assert (k_a == k_b) or (red_q and max(k_a, k_b) == N_CHIPS * kq), (a_arr.shape, b_arr.shape)
    tn, tk = _tile(n, tn), _tile(kq, tk)
    if tm is None:
        r_item = _opnd(res)[0].dtype.itemsize if res is not None else 0
        per_row = 2 * (tk * a_arr.dtype.itemsize + tn * (jnp.dtype(out_dtype).itemsize + r_item)) + 4 * tn
        room = MM_VMEM_BUDGET_BYTES - 2 * tk * tn * b_arr.dtype.itemsize
        tm = next(c for c in (4096, 2048, 1024, 512, 256, 128) if c * per_row <= room or c == 128)
    tm = _tile(m, tm)
    nk = kq // tk
    ksteps = nk * (N_CHIPS if red_q else 1)
    dims = (((0 if ta else 1,), (1 if tb else 0,)), ((), ()))
    has_res = res is not None

    def body(*refs):
        a_ref, b_ref = refs[0], refs[1]
        o_ref, acc_ref = refs[-2], refs[-1]
        k = pl.program_id(3)

        @pl.when(k == 0)
        def _():
            acc_ref[...] = jnp.zeros_like(acc_ref)

        acc_ref[...] += lax.dot_general(a_ref[...].astype(BF16), b_ref[...].astype(BF16), dims,
                                        preferred_element_type=F32)

        @pl.when(k == ksteps - 1)
        def _():
            r = acc_ref[...]
            if scale != 1.0:
                r = r * scale
            if has_res:
                r = r + refs[2][...].astype(F32)
            o_ref[...] = r.astype(o_ref.dtype)

    def spec(lead, blk, pos):
        def index(s, i, j, k):
            kk = k % nk if (red_q and Q in lead) else k
            quarter = s if out_q else k // nk
            return tuple(quarter if l == Q else l for l in lead) + pos(i, j, kk)
        return pl.BlockSpec((None,) * len(lead) + blk, index)

    a_spec = spec(a_lead, (tk, tm), lambda i, j, k: (k, i)) if ta else spec(a_lead, (tm, tk), lambda i, j, k: (i, k))
    b_spec = spec(b_lead, (tn, tk), lambda i, j, k: (j, k)) if tb else spec(b_lead, (tk, tn), lambda i, j, k: (k, j))
    o_spec = spec(out_lead, (tm, tn), lambda i, j, k: (i, j))
    in_specs, args = [a_spec, b_spec], [a_arr, b_arr]
    if has_res:
        r_arr, r_lead = _opnd(res)
        in_specs.append(spec(r_lead, (tm, tn), lambda i, j, k: (i, j)))
        args.append(r_arr)
    aliases = {}
    if into is not None:
        in_specs.append(pl.BlockSpec(memory_space=pl.ANY))
        args.append(out_arr)
        aliases = {len(args) - 1: 0}
        out_shape = jax.ShapeDtypeStruct(out_arr.shape, out_arr.dtype)
    else:
        out_shape = jax.ShapeDtypeStruct(((N_CHIPS,) if out_q else ()) + (m, n), out_dtype)
    return pl.pallas_call(
        body, name=name, grid=(N_CHIPS if out_q else 1, m // tm, n // tn, ksteps), in_specs=in_specs, out_specs=o_spec,
        out_shape=out_shape, scratch_shapes=[pltpu.VMEM((tm, tn), F32)], input_output_aliases=aliases,
        compiler_params=_params("parallel", "parallel", "parallel", "arbitrary"),
    )(*args)


def _row_spec(r, tm):
    if isinstance(r, tuple):
        arr, width, cb = r
        return arr, pl.BlockSpec((tm, width), lambda i, cb=cb: (i, cb))
    return r, pl.BlockSpec((tm, r.shape[1]), lambda i: (i, 0))


def _const_spec(c):
    return pl.BlockSpec(c.shape, lambda i: (0,) * c.ndim)


def _row_fwd(fn, rows, consts, outs, *, name, tm=256):
    tm = _tile(_row_spec(rows[0], tm)[0].shape[0], tm, 8)
    arrs, specs = zip(*[_row_spec(r, tm) for r in rows])
    t = arrs[0].shape[0]
    nr, nc = len(rows), len(consts)

    def body(*refs):
        vals = [r[...].astype(F32) for r in refs[:nr + nc]]
        res = fn(*vals)
        for o_ref, v in zip(refs[nr + nc:], res):
            o_ref[...] = v.astype(o_ref.dtype)

    out = pl.pallas_call(
        body, name=name, grid=(t // tm,),
        in_specs=list(specs) + [_const_spec(c) for c in consts],
        out_specs=[pl.BlockSpec((tm, w), lambda i: (i, 0)) for w, _ in outs],
        out_shape=[jax.ShapeDtypeStruct((t, w), dt) for w, dt in outs],
        compiler_params=_params("parallel"),
    )(*arrs, *consts)
    return list(out)


def _row_bwd(fn, rows, consts, cts, row_grads, const_grads, *, name, tm=256):
    tm = _tile(_row_spec(rows[0], tm)[0].shape[0], tm, 8)
    arrs, specs = zip(*[_row_spec(r, tm) for r in rows])
    ct_arrs, ct_specs = zip(*[_row_spec(r, tm) for r in cts])
    t = arrs[0].shape[0]
    nr, nc, nt = len(rows), len(consts), len(cts)
    n_in = nr + nc + nt

    def body(*refs):
        vals = [r[...].astype(F32) for r in refs[:nr + nc]]
        ctv = tuple(r[...].astype(F32) for r in refs[nr + nc:n_in])
        _, vjp = jax.vjp(fn, *vals)
        g = vjp(ctv)
        outs = refs[n_in:]
        for (idx, _), o_ref in zip(row_grads, outs[:len(row_grads)]):
            o_ref[...] = g[idx].astype(o_ref.dtype)
        first = pl.program_id(0) == 0
        for ci, o_ref in zip(const_grads, outs[len(row_grads):]):
            @pl.when(first)
            def _(o_ref=o_ref):
                o_ref[...] = jnp.zeros_like(o_ref)

            o_ref[...] += g[nr + ci]

    widths = [(_row_spec(rows[idx], tm)[1].block_shape[1], dt) for idx, dt in row_grads]
    out = pl.pallas_call(
        body, name=name, grid=(t // tm,),
        in_specs=list(specs) + [_const_spec(c) for c in consts] + list(ct_specs),
        out_specs=[pl.BlockSpec((tm, w), lambda i: (i, 0)) for w, _ in widths]
        + [_const_spec(consts[ci]) for ci in const_grads],
        out_shape=[jax.ShapeDtypeStruct((t, w), dt) for w, dt in widths]
        + [jax.ShapeDtypeStruct(consts[ci].shape, F32) for ci in const_grads],
        compiler_params=_params("arbitrary"),
    )(*arrs, *consts, *ct_arrs)
    return list(out)


def _rms(x, g):
    return x * lax.rsqrt(jnp.mean(x * x, axis=-1, keepdims=True) + EPS) * g


def _f_rms(h, g):
    return (_rms(h, g),)


def _f_rms_res(h, g):
    return (_rms(h, g), h)


def _f_swiglu(g, u):
    return (g * jax.nn.sigmoid(g) * u,)


def _f_ple(h, gl, pp):
    return (h + jax.nn.sigmoid(gl) * pp,)


def _f_gdn_out(o, z, gain):
    outs = []
    for hd in range(GDN_V_HEADS):
        sl = slice(hd * GDN_HEAD_DIM, (hd + 1) * GDN_HEAD_DIM)
        oh, zh = o[:, sl], z[:, sl]
        outs.append(_rms(oh, gain) * (zh * jax.nn.sigmoid(zh)))
    return (jnp.concatenate(outs, axis=1),)


def _loss_head(y, target, *, name, tm=512):
    t, d = y.shape
    tm = _tile(t, tm, 8)

    def body(y_ref, t_ref, dy_ref, l_ref):
        @pl.when(pl.program_id(0) == 0)
        def _():
            l_ref[...] = jnp.zeros_like(l_ref)

        e = y_ref[...] - t_ref[...]
        dy_ref[...] = e * (1.0 / d)
        l_ref[...] += jnp.sum(e * e) * (0.5 / d)

    dy, l = pl.pallas_call(
        body, name=name, grid=(t // tm,),
        in_specs=[pl.BlockSpec((tm, d), lambda i: (i, 0))] * 2,
        out_specs=[pl.BlockSpec((tm, d), lambda i: (i, 0)), pl.BlockSpec((8, LANES), lambda i: (0, 0))],
        out_shape=[jax.ShapeDtypeStruct((t, d), F32), jax.ShapeDtypeStruct((8, LANES), F32)],
        compiler_params=_params("arbitrary"),
    )(y, target)
    return dy, l[0, 0]


def _dg(a, b, ca, cb):
    nb = a.ndim - 2
    batch = tuple(range(nb))
    return lax.dot_general(a, b, (((ca + nb,), (cb + nb,)), (batch, batch)), preferred_element_type=F32)


def _b(x):
    return x.astype(BF16)


@jax.custom_vjp
def _bdot(a, b):
    return _dg(_b(a), _b(b), 1, 0)


def _bdot_fwd(a, b):
    return _bdot(a, b), (a, b)


def _bdot_bwd(r, ct):
    a, b = r
    return _dg(_b(ct), _b(b), 1, 1), _dg(_b(a), _b(ct), 0, 0)


_bdot.defvjp(_bdot_fwd, _bdot_bwd)


@jax.custom_vjp
def _bdot_nt(a, b):
    return _dg(_b(a), _b(b), 1, 1)


def _bdot_nt_fwd(a, b):
    return _bdot_nt(a, b), (a, b)


def _bdot_nt_bwd(r, ct):
    a, b = r
    return _dg(_b(ct), _b(b), 1, 0), _dg(_b(ct), _b(a), 0, 0)


_bdot_nt.defvjp(_bdot_nt_fwd, _bdot_nt_bwd)


@jax.custom_vjp
def _bdot_tn(a, b):
    return _dg(_b(a), _b(b), 0, 0)


def _bdot_tn_fwd(a, b):
    return _bdot_tn(a, b), (a, b)


def _bdot_tn_bwd(r, ct):
    a, b = r
    return _dg(_b(b), _b(ct), 1, 1), _dg(_b(a), _b(ct), 1, 0)


_bdot_tn.defvjp(_bdot_tn_fwd, _bdot_tn_bwd)


def _two(x):
    hi = x.astype(BF16)
    return hi, (x - hi.astype(F32)).astype(BF16)


def _dg3(a, b, ca, cb):
    (ah, al), (bh, bl) = _two(a), _two(b)
    return _dg(ah, bh, ca, cb) + (_dg(ah, bl, ca, cb) + _dg(al, bh, ca, cb))


@jax.custom_vjp
def _hdot(a, b):
    return _dg3(a, b, 1, 0)


def _hdot_fwd(a, b):
    return _hdot(a, b), (a, b)


def _hdot_bwd(r, ct):
    a, b = r
    return _dg3(ct, b, 1, 1), _dg3(a, ct, 0, 0)


_hdot.defvjp(_hdot_fwd, _hdot_bwd)


@jax.custom_vjp
def _unit_lower_inverse(x):
    c = x.shape[-1]
    eye = (lax.broadcasted_iota(jnp.int32, x.shape, 1) == lax.broadcasted_iota(jnp.int32, x.shape, 2)).astype(F32)
    inv, pw = eye + x, x
    for _ in range(int(math.log2(c)) - 1):
        pw = _dg3(pw, pw, 1, 0)
        inv = inv + _dg3(inv, pw, 1, 0)
    return inv


def _unit_lower_inverse_fwd(x):
    inv = _unit_lower_inverse(x)
    return inv, inv


def _unit_lower_inverse_bwd(inv, ct):
    return (_dg3(_dg3(inv, ct, 0, 0), inv, 1, 1),)


_unit_lower_inverse.defvjp(_unit_lower_inverse_fwd, _unit_lower_inverse_bwd)


def _split_dot(x, u):
    hi, lo = _two(x)
    return _dg(hi, u, 1, 0) + _dg(lo, u, 1, 0)


@jax.custom_vjp
def _ldot(l01, x):
    hi, lo = _two(x)
    l01 = l01.astype(BF16)
    return _dg(l01, hi, 1, 0) + _dg(l01, lo, 1, 0)


def _ldot_fwd(l01, x):
    return _ldot(l01, x), l01


def _ldot_bwd(l01, ct):
    hi, lo = _two(ct)
    l01b = l01.astype(BF16)
    return jnp.zeros_like(l01), _dg(l01b, hi, 0, 0) + _dg(l01b, lo, 0, 0)


_ldot.defvjp(_ldot_fwd, _ldot_bwd)


SB_BLK = 128
SB_KEYS = 512
SB_PAIRS = 2
SB_SCALE = HEAD_DIM ** -0.5


def _log_sigmoid(z):
    return jnp.minimum(z, 0.0) - jnp.log(1.0 + jnp.exp(-jnp.abs(z)))


def _sb_consts(t):
    kb = min(SB_KEYS, t)
    nh = 2 * SB_PAIRS
    lane = lax.broadcasted_iota(jnp.int32, (nh, SB_BLK, kb), 2)
    row = lax.broadcasted_iota(jnp.int32, (nh, SB_BLK, kb), 1)
    ur = lax.broadcasted_iota(jnp.int32, (kb, kb), 0)
    uc = lax.broadcasted_iota(jnp.int32, (kb, kb), 1)
    return kb, nh, lane, row, ur, uc


def _sb_heads(x):
    head0 = lax.broadcasted_iota(jnp.int32, (x.shape[0], LANES), 1) < HEAD_DIM
    out = []
    for p in range(SB_PAIRS):
        blk = x[:, p * LANES:(p + 1) * LANES]
        out += [jnp.where(head0, blk, 0.0), jnp.where(head0, 0.0, blk)]
    return jnp.stack(out)


def _sb_pairs(x):
    return jnp.stack([x[:, (h // 2) * LANES:(h // 2 + 1) * LANES] for h in range(2 * SB_PAIRS)])


def _sb_merge(x):
    head0 = lax.broadcasted_iota(jnp.int32, (x.shape[1], LANES), 1) < HEAD_DIM
    return jnp.concatenate([jnp.where(head0, x[2 * p], x[2 * p + 1]) for p in range(SB_PAIRS)], axis=1)


def _sb_rows_dot(x, u):
    nh, rows, k = x.shape
    return _split_dot(x.reshape(nh * rows, k), u).reshape(nh, rows, k)


def _sb_fwd(proj, *, name):
    t = proj.shape[0]
    nb = t // SB_BLK
    width = SB_PAIRS * LANES
    ng = SB_W // width

    def body(q_ref, k_ref, v_ref, o_ref, r_ref):
        i = pl.program_id(1)
        kb, nh, lane, row, ur, uc = _sb_consts(t)
        u_suffix = (ur >= uc).astype(BF16)
        qh = _b(_sb_heads(q_ref[...]))
        diag = (i * SB_BLK) // kb

        def block(j, carry, masked):
            acc, car = carry
            keys = pl.ds(pl.multiple_of(j * kb, kb), kb)
            kj, vj = _b(_sb_pairs(k_ref[keys, :])), _b(_sb_pairs(v_ref[keys, :]))
            z = _dg(qh, kj, 1, 1) * SB_SCALE
            ls = _log_sigmoid(z)
            lk = ls - z
            if masked:
                causal = (j * kb + lane) < (i * SB_BLK + row)
                lk = jnp.where(causal, lk, 0.0)
            suf = _sb_rows_dot(lk, u_suffix) + car
            w = jnp.exp(ls + (suf - lk))
            if masked:
                w = jnp.where(causal, w, 0.0)
            return acc + _dg(_b(w), vj, 1, 0), suf[:, :, 0:1]

        zero = (jnp.zeros((nh, SB_BLK, LANES), F32), jnp.zeros((nh, SB_BLK, 1), F32))
        carry = block(diag, zero, True)
        acc, car = lax.fori_loop(0, diag, lambda s, c: block(diag - 1 - s, c, False), carry)
        o_ref[...] = _sb_merge(acc)
        r_ref[...] = _sb_merge(jnp.broadcast_to(car, (nh, SB_BLK, LANES)))

    return pl.pallas_call(
        body, name=name, grid=(ng, nb),
        in_specs=[pl.BlockSpec((SB_BLK, width), lambda p, i: (i, p)),
                  pl.BlockSpec((t, width), lambda p, i: (0, ng + p)),
                  pl.BlockSpec((t, width), lambda p, i: (0, 2 * ng + p))],
        out_specs=[pl.BlockSpec((SB_BLK, width), lambda p, i: (i, p))] * 2,
        out_shape=[jax.ShapeDtypeStruct((t, SB_W), F32)] * 2,
        compiler_params=_params("parallel", "arbitrary"),
    )(proj, proj, proj)


def _sb_bwd(proj, rtot, dout, *, name):
    t = proj.shape[0]
    nb = t // SB_BLK
    width = SB_PAIRS * LANES
    ng = SB_W // width

    def body(q_ref, k_ref, v_ref, r_ref, do_ref, dq_ref, dk_ref, dv_ref):
        i = pl.program_id(1)
        kb, nh, lane, row, ur, uc = _sb_consts(t)
        u_incl = (ur <= uc).astype(BF16)
        u_excl = (ur < uc).astype(BF16)
        q, do = q_ref[...], do_ref[...]
        qh, doh = _b(_sb_heads(q)), _b(_sb_heads(do))
        qb, dob = _b(_sb_pairs(q)), _b(_sb_pairs(do))
        rh = jnp.min(_sb_heads(r_ref[...]), axis=2, keepdims=True)
        diag = (i * SB_BLK) // kb

        @pl.when(i == 0)
        def _():
            dk_ref[...] = jnp.zeros_like(dk_ref)
            dv_ref[...] = jnp.zeros_like(dv_ref)

        def block(j, carry, masked):
            dq_acc, clk, ce = carry
            keys = pl.ds(pl.multiple_of(j * kb, kb), kb)
            kj, vj = _b(_sb_pairs(k_ref[keys, :])), _b(_sb_pairs(v_ref[keys, :]))
            z = _dg(qh, kj, 1, 1) * SB_SCALE
            ls = _log_sigmoid(z)
            lk = ls - z
            if masked:
                causal = (j * kb + lane) < (i * SB_BLK + row)
                lk = jnp.where(causal, lk, 0.0)
            pre = _sb_rows_dot(lk, u_incl) + clk
            w = jnp.exp(ls + (rh - pre))
            if masked:
                w = jnp.where(causal, w, 0.0)
            e = _dg(doh, vj, 1, 1) * w
            pre_e = _sb_rows_dot(e, u_excl) + ce
            sig = jnp.exp(ls)
            dz = (e * (1.0 - sig) - sig * pre_e) * SB_SCALE
            if masked:
                dz = jnp.where(causal, dz, 0.0)
            dzb = _b(dz)
            dk_ref[keys, :] += _sb_merge(_dg(dzb, qb, 0, 0))
            dv_ref[keys, :] += _sb_merge(_dg(_b(w), dob, 0, 0))
            return dq_acc + _dg(dzb, kj, 1, 0), pre[:, :, kb - 1:], pre_e[:, :, kb - 1:] + e[:, :, kb - 1:]

        zero = (jnp.zeros((nh, SB_BLK, LANES), F32), jnp.zeros((nh, SB_BLK, 1), F32), jnp.zeros((nh, SB_BLK, 1), F32))
        carry = lax.fori_loop(0, diag, lambda j, c: block(j, c, False), zero)
        dq_acc, _, _ = block(diag, carry, True)
        dq_ref[...] = _sb_merge(dq_acc)

    blk = pl.BlockSpec((SB_BLK, width), lambda p, i: (i, p))
    whole = pl.BlockSpec((t, width), lambda p, i: (0, p))
    return pl.pallas_call(
        body, name=name, grid=(ng, nb),
        in_specs=[blk,
                  pl.BlockSpec((t, width), lambda p, i: (0, ng + p)),
                  pl.BlockSpec((t, width), lambda p, i: (0, 2 * ng + p)),
                  blk, blk],
        out_specs=[blk, whole, whole],
        out_shape=[jax.ShapeDtypeStruct((t, SB_W), F32)] * 3,
        compiler_params=_params("arbitrary", "arbitrary"),
    )(proj, proj, proj, rtot, dout)


SWA_G = SWA_HEADS // SWA_KV_HEADS


def _swa_heads(first, qs, ks, vs, qg, kg, sinks):
    shape = (SWA_HEADS, WINDOW, 2 * WINDOW)
    qi = lax.broadcasted_iota(jnp.int32, shape, 1)
    kj = lax.broadcasted_iota(jnp.int32, shape, 2)
    dist = qi + WINDOW - kj
    valid = (dist >= 0) & (dist < WINDOW) & (jnp.logical_not(first) | (kj >= WINDOW))
    head = lax.broadcasted_iota(jnp.int32, (SWA_HEADS, 1, 1), 0)
    slope = sum(jnp.where(head == h, 2.0 ** (-8.0 * (h + 1) / SWA_HEADS), 0.0) for h in range(SWA_HEADS))
    kn = _rms(ks, kg)
    per_q_head = lambda x: jnp.concatenate([x[h // SWA_G:h // SWA_G + 1] for h in range(SWA_HEADS)], axis=0)
    k8, v8 = per_q_head(kn), per_q_head(vs)
    s = _bdot_nt(_rms(qs, qg), k8) * (HEAD_DIM ** -0.5)
    s = jnp.where(valid, s - slope * dist.astype(F32), NEG)
    m = lax.stop_gradient(jnp.maximum(jnp.max(s, axis=2, keepdims=True), sinks))
    p = jnp.exp(s - m)
    den = jnp.sum(p, axis=2, keepdims=True) + jnp.exp(sinks - m)
    return _bdot(p / den, v8)


def _swa_split(q, kp, kc, vp, vc, sk):
    lanes = lambda x, n: jnp.stack([x[:, h * HEAD_DIM:(h + 1) * HEAD_DIM] for h in range(n)])
    k2, v2 = jnp.concatenate([kp, kc], axis=0), jnp.concatenate([vp, vc], axis=0)
    sinks = jnp.stack([sk[:, h:h + 1] for h in range(SWA_HEADS)])
    return lanes(q, SWA_HEADS), lanes(k2, SWA_KV_HEADS), lanes(v2, SWA_KV_HEADS), sinks


def _swa_join(x):
    return jnp.concatenate([x[h] for h in range(x.shape[0])], axis=1)


def _swa_specs(t):
    qcb = (3 * SB_W) // SWA_QW
    kcb = (3 * SB_W + SWA_QW) // SWA_KVW
    prev = lambda i: jnp.maximum(i - 1, 0)
    return [pl.BlockSpec((WINDOW, SWA_QW), lambda i: (i, qcb)),
            pl.BlockSpec((WINDOW, SWA_KVW), lambda i: (prev(i), kcb)),
            pl.BlockSpec((WINDOW, SWA_KVW), lambda i: (i, kcb)),
            pl.BlockSpec((WINDOW, SWA_KVW), lambda i: (prev(i), kcb + 1)),
            pl.BlockSpec((WINDOW, SWA_KVW), lambda i: (i, kcb + 1)),
            pl.BlockSpec((1, HEAD_DIM), lambda i: (0, 0)),
            pl.BlockSpec((1, HEAD_DIM), lambda i: (0, 0)),
            pl.BlockSpec((1, SWA_HEADS), lambda i: (0, 0))]


def _swa_fwd(proj, qg, kg, sinks, *, name):
    t = proj.shape[0]

    def body(q_ref, kp_ref, kc_ref, vp_ref, vc_ref, qg_ref, kg_ref, sk_ref, o_ref):
        first = pl.program_id(0) == 0
        qs, ks, vs, sk = _swa_split(q_ref[...], kp_ref[...], kc_ref[...], vp_ref[...], vc_ref[...], sk_ref[...])
        o_ref[...] = _swa_join(_swa_heads(first, qs, ks, vs, qg_ref[...], kg_ref[...], sk))

    return pl.pallas_call(
        body, name=name, grid=(t // WINDOW,), in_specs=_swa_specs(t),
        out_specs=pl.BlockSpec((WINDOW, SWA_QW), lambda i: (i, 0)),
        out_shape=jax.ShapeDtypeStruct((t, SWA_QW), F32),
        compiler_params=_params("parallel"),
    )(proj, proj, proj, proj, proj, qg, kg, sinks)


def _swa_bwd(proj, qg, kg, sinks, dout, *, name):
    t = proj.shape[0]

    def body(q_ref, kp_ref, kc_ref, vp_ref, vc_ref, qg_ref, kg_ref, sk_ref, do_ref,
             dq_ref, dk_ref, dv_ref, dqg_ref, dkg_ref, dsk_ref):
        i = pl.program_id(0)
        first = i == 0

        @pl.when(first)
        def _():
            for r in (dk_ref, dv_ref, dqg_ref, dkg_ref, dsk_ref):
                r[...] = jnp.zeros_like(r)

        qs, ks, vs, sk = _swa_split(q_ref[...], kp_ref[...], kc_ref[...], vp_ref[...], vc_ref[...], sk_ref[...])
        do = do_ref[...]
        cts = jnp.stack([do[:, h * HEAD_DIM:(h + 1) * HEAD_DIM] for h in range(SWA_HEADS)])
        _, vjp = jax.vjp(functools.partial(_swa_heads, first), qs, ks, vs, qg_ref[...], kg_ref[...], sk)
        dqs, dks, dvs, dqg, dkg, dsk = vjp(cts)
        dq_ref[...] = _swa_join(dqs)
        dk2, dv2 = _swa_join(dks), _swa_join(dvs)
        cur = pl.ds(pl.multiple_of(i * WINDOW, WINDOW), WINDOW)
        prv = pl.ds(pl.multiple_of(jnp.maximum(i - 1, 0) * WINDOW, WINDOW), WINDOW)
        dk_ref[prv, :] += dk2[:WINDOW]
        dv_ref[prv, :] += dv2[:WINDOW]
        dk_ref[cur, :] += dk2[WINDOW:]
        dv_ref[cur, :] += dv2[WINDOW:]
        dqg_ref[...] += dqg
        dkg_ref[...] += dkg
        dsk_ref[...] += _swa_join(dsk)

    whole = lambda shape: pl.BlockSpec(shape, lambda i: (0, 0))
    return pl.pallas_call(
        body, name=name, grid=(t // WINDOW,),
        in_specs=_swa_specs(t) + [pl.BlockSpec((WINDOW, SWA_QW), lambda i: (i, 0))],
        out_specs=[pl.BlockSpec((WINDOW, SWA_QW), lambda i: (i, 0)), whole((t, SWA_KVW)), whole((t, SWA_KVW)),
                   whole((1, HEAD_DIM)), whole((1, HEAD_DIM)), whole((1, SWA_HEADS))],
        out_shape=[jax.ShapeDtypeStruct((t, SWA_QW), F32), jax.ShapeDtypeStruct((t, SWA_KVW), F32),
                   jax.ShapeDtypeStruct((t, SWA_KVW), F32), jax.ShapeDtypeStruct((1, HEAD_DIM), F32),
                   jax.ShapeDtypeStruct((1, HEAD_DIM), F32), jax.ShapeDtypeStruct((1, SWA_HEADS), F32)],
        compiler_params=_params("arbitrary"),
    )(proj, proj, proj, proj, proj, qg, kg, sinks, dout)


CONV_CB = 512
CONV_TM = 512
HALO = 8


def _conv_pre(x_ref, h_ref, w_ref, i):
    halo = jnp.where(i > 0, h_ref[...], 0.0)
    xe = jnp.concatenate([halo, x_ref[...]], axis=0)
    tm = x_ref.shape[0]
    w = w_ref[...]
    c = sum(w[k:k + 1, :] * xe[HALO - (GDN_CONV - 1) + k:HALO - (GDN_CONV - 1) + k + tm] for k in range(GDN_CONV))
    return c, xe


def _conv_specs(tm, cb):
    return [pl.BlockSpec((tm, cb), lambda c, i: (i, c)),
            pl.BlockSpec((HALO, cb), lambda c, i: (jnp.maximum(i * (tm // HALO) - 1, 0), c)),
            pl.BlockSpec((GDN_CONV, cb), lambda c, i: (0, c))]


def _conv_fwd(x, w, dact=None, *, name):
    t, ch = x.shape
    tm, cb = _tile(t, CONV_TM), _tile(ch, CONV_CB)

    def body(*refs):
        x_ref, h_ref, w_ref = refs[:3]
        c, _ = _conv_pre(x_ref, h_ref, w_ref, pl.program_id(1))
        sig = jax.nn.sigmoid(c)
        if dact is None:
            refs[3][...] = c * sig
        else:
            refs[4][...] = refs[3][...] * (sig * (1.0 + c * (1.0 - sig)))

    tile = pl.BlockSpec((tm, cb), lambda c, i: (i, c))
    extra = () if dact is None else (dact,)
    return pl.pallas_call(
        body, name=name, grid=(ch // cb, t // tm),
        in_specs=_conv_specs(tm, cb) + [tile] * len(extra), out_specs=tile,
        out_shape=jax.ShapeDtypeStruct((t, ch), F32),
        compiler_params=_params("parallel", "parallel"),
    )(x, x, w, *extra)


def _conv_bwd(x, w, dc, *, name):
    t, ch = x.shape
    tm, cb = _tile(t, CONV_TM), _tile(ch, CONV_CB)
    nt = t // tm

    def body(x_ref, h_ref, w_ref, dc_ref, nh_ref, dx_ref, dw_ref):
        i = pl.program_id(1)

        @pl.when(i == 0)
        def _():
            dw_ref[...] = jnp.zeros_like(dw_ref)

        halo = jnp.where(i > 0, h_ref[...], 0.0)
        xe = jnp.concatenate([halo, x_ref[...]], axis=0)
        dc = dc_ref[...]
        dce = jnp.concatenate([dc, jnp.where(i < nt - 1, nh_ref[...], 0.0)], axis=0)
        w = w_ref[...]
        last = GDN_CONV - 1
        dx_ref[...] = sum(w[k:k + 1, :] * dce[last - k:last - k + tm] for k in range(GDN_CONV))
        dw_ref[...] += jnp.concatenate(
            [jnp.sum(dc * xe[HALO - last + k:HALO - last + k + tm], axis=0, keepdims=True) for k in range(GDN_CONV)],
            axis=0)

    tile = pl.BlockSpec((tm, cb), lambda c, i: (i, c))
    nxt = pl.BlockSpec((HALO, cb), lambda c, i: (jnp.minimum((i + 1) * (tm // HALO), t // HALO - 1), c))
    return pl.pallas_call(
        body, name=name, grid=(ch // cb, nt),
        in_specs=_conv_specs(tm, cb) + [tile, nxt],
        out_specs=[tile, pl.BlockSpec((GDN_CONV, cb), lambda c, i: (0, c))],
        out_shape=[jax.ShapeDtypeStruct((t, ch), F32), jax.ShapeDtypeStruct((GDN_CONV, ch), F32)],
        compiler_params=_params("parallel", "arbitrary"),
    )(x, x, w, dc, dc)


def _gdn_chunk(qraw, kraw, v, bl, a, alog, dtb, state):
    c, d = GDN_CHUNK, GDN_HEAD_DIM
    nh = qraw.shape[0]
    ri = lax.broadcasted_iota(jnp.int32, (nh, c, c), 1)
    ci = lax.broadcasted_iota(jnp.int32, (nh, c, c), 2)
    incl, strict = ri >= ci, ri > ci
    q = qraw * lax.rsqrt(jnp.sum(qraw * qraw, axis=-1, keepdims=True) + EPS) * (d ** -0.5)
    k = kraw * lax.rsqrt(jnp.sum(kraw * kraw, axis=-1, keepdims=True) + EPS)
    beta = jax.nn.sigmoid(bl)
    g = -jnp.exp(alog) * jax.nn.softplus(a + dtb)
    gc = _ldot(incl.astype(F32), jnp.broadcast_to(g, (nh, c, d)))
    gcm = gc[:, :, :c]
    decay = jnp.exp(jnp.where(incl, gcm - jnp.swapaxes(gcm, 1, 2), NEG))
    eg = jnp.exp(gc)
    kbeta = k * beta
    x = -jnp.where(strict, _bdot_nt(kbeta, k) * decay, 0.0)
    tinv = _unit_lower_inverse(x)
    u = _hdot(tinv, v * beta)
    w = _hdot(tinv, kbeta * eg)
    attn = jnp.where(incl, _bdot_nt(q, k) * decay, 0.0)
    glast = gc[:, c - 1:c, :]
    v_new = u - _bdot(w, state)
    o = _bdot(q * eg, state) + _bdot(attn, v_new)
    state = state * jnp.exp(glast) + _bdot_tn(k * jnp.exp(glast - gc), v_new)
    return o, state


GDN_REP = GDN_V_HEADS // GDN_K_HEADS
GDN_HB = 8


def _gdn_pick(vals, kh, r):
    ba, alog, dtb = vals
    lane = lax.broadcasted_iota(jnp.int32, ba.shape, 1)
    hv = kh * GDN_REP + r
    bl = jnp.sum(jnp.where(lane == hv, ba, 0.0), axis=1, keepdims=True)
    a = jnp.sum(jnp.where(lane == GDN_V_HEADS + hv, ba, 0.0), axis=1, keepdims=True)
    lane1 = lax.broadcasted_iota(jnp.int32, alog.shape, 1)
    al = jnp.sum(jnp.where(lane1 == hv, alog, 0.0), axis=1, keepdims=True)
    db = jnp.sum(jnp.where(lane1 == hv, dtb, 0.0), axis=1, keepdims=True)
    return bl, a, al, db


def _gdn_stack(qs, ks, vs, small, j):
    d = GDN_HEAD_DIM
    per = [[], [], [], [], [], [], []]
    for hh in range(GDN_HB):
        q, k = qs[:, hh * d:(hh + 1) * d], ks[:, hh * d:(hh + 1) * d]
        for r in range(GDN_REP):
            col = (hh * GDN_REP + r) * d
            for lst, val in zip(per, (q, k, vs[:, col:col + d]) + _gdn_pick(small, j * GDN_HB + hh, r)):
                lst.append(val)
    return tuple(jnp.stack(lst) for lst in per)


def _gdn_specs(nchunk, rev):
    c, d = GDN_CHUNK, GDN_HEAD_DIM
    at = (lambda n: nchunk - 1 - n) if rev else (lambda n: n)
    ng = GDN_K_HEADS // GDN_HB
    return at, [pl.BlockSpec((c, GDN_HB * d), lambda n, j: (at(n), j)),
                pl.BlockSpec((c, GDN_HB * d), lambda n, j: (at(n), ng + j)),
                pl.BlockSpec((c, GDN_HB * GDN_REP * d), lambda n, j: (at(n), ng + j)),
                pl.BlockSpec((c, 2 * GDN_V_HEADS), lambda n, j: (at(n), 0)),
                pl.BlockSpec((1, GDN_V_HEADS), lambda n, j: (0, 0)),
                pl.BlockSpec((1, GDN_V_HEADS), lambda n, j: (0, 0))]


def _gdn_fwd(act, ba, alog, dtb, *, name):
    t = act.shape[0]
    c, d = GDN_CHUNK, GDN_HEAD_DIM
    nchunk = t // c
    at, specs = _gdn_specs(nchunk, False)

    def body(q_ref, k_ref, v_ref, ba_ref, al_ref, db_ref, o_ref, s_ref, state):
        n, j = pl.program_id(0), pl.program_id(1)
        heads = pl.ds(j * GDN_HB, GDN_HB)

        @pl.when(n == 0)
        def _():
            state[heads] = jnp.zeros((GDN_HB, GDN_REP, d, d), F32)

        s_in = state[heads]
        s_ref[...] = s_in
        args = _gdn_stack(q_ref[...], k_ref[...], v_ref[...], (ba_ref[...], al_ref[...], db_ref[...]), j)
        o, s_new = _gdn_chunk(*args, s_in.reshape(GDN_HB * GDN_REP, d, d))
        o_ref[...] = jnp.concatenate([o[b] for b in range(GDN_HB * GDN_REP)], axis=1)
        state[heads] = s_new.reshape(GDN_HB, GDN_REP, d, d)

    return pl.pallas_call(
        body, name=name, grid=(nchunk, GDN_K_HEADS // GDN_HB), in_specs=specs,
        out_specs=[pl.BlockSpec((c, GDN_HB * GDN_REP * d), lambda n, j: (n, j)),
                   pl.BlockSpec((None, GDN_HB, GDN_REP, d, d), lambda n, j: (n, j, 0, 0, 0))],
        out_shape=[jax.ShapeDtypeStruct((t, GDN_VW), F32),
                   jax.ShapeDtypeStruct((nchunk, GDN_K_HEADS, GDN_REP, d, d), F32)],
        scratch_shapes=[pltpu.VMEM((GDN_K_HEADS, GDN_REP, d, d), F32)],
        compiler_params=_params("arbitrary", "arbitrary"),
    )(act, act, act, ba, alog, dtb)


def _gdn_bwd(act, ba, alog, dtb, states, dout, *, name):
    t = act.shape[0]
    c, d = GDN_CHUNK, GDN_HEAD_DIM
    nchunk = t // c
    at, specs = _gdn_specs(nchunk, True)

    def body(q_ref, k_ref, v_ref, ba_ref, al_ref, db_ref, s_ref, do_ref,
             dq_ref, dk_ref, dv_ref, dba_ref, dal_ref, ddb_ref, dstate):
        n, j = pl.program_id(0), pl.program_id(1)

        @pl.when(n == 0)
        def _():
            dstate[pl.ds(j * GDN_HB, GDN_HB)] = jnp.zeros((GDN_HB, GDN_REP, d, d), F32)

        @pl.when((n == 0) & (j == 0))
        def _():
            dal_ref[...] = jnp.zeros_like(dal_ref)
            ddb_ref[...] = jnp.zeros_like(ddb_ref)

        @pl.when(j == 0)
        def _():
            dba_ref[...] = jnp.zeros_like(dba_ref)

        heads = pl.ds(j * GDN_HB, GDN_HB)
        nh = GDN_HB * GDN_REP
        args = _gdn_stack(q_ref[...], k_ref[...], v_ref[...], (ba_ref[...], al_ref[...], db_ref[...]), j)
        _, vjp = jax.vjp(_gdn_chunk, *args, s_ref[...].reshape(nh, d, d))
        do = do_ref[...]
        do = jnp.stack([do[:, b * d:(b + 1) * d] for b in range(nh)])
        gq, gk, gv, gbl, ga, gal, gdb, gs = vjp((do, dstate[heads].reshape(nh, d, d)))
        dstate[heads] = gs.reshape(GDN_HB, GDN_REP, d, d)
        dq_ref[...] = jnp.concatenate([gq[GDN_REP * hh] + gq[GDN_REP * hh + 1] for hh in range(GDN_HB)], axis=1)
        dk_ref[...] = jnp.concatenate([gk[GDN_REP * hh] + gk[GDN_REP * hh + 1] for hh in range(GDN_HB)], axis=1)
        dv_ref[...] = jnp.concatenate([gv[b] for b in range(nh)], axis=1)
        lane = lax.broadcasted_iota(jnp.int32, (c, 2 * GDN_V_HEADS), 1)
        lane1 = lax.broadcasted_iota(jnp.int32, (1, GDN_V_HEADS), 1)
        dba = jnp.zeros((c, 2 * GDN_V_HEADS), F32)
        dal = jnp.zeros((1, GDN_V_HEADS), F32)
        ddb = jnp.zeros((1, GDN_V_HEADS), F32)
        for b in range(nh):
            hv = j * nh + b
            dba = dba + jnp.where(lane == hv, gbl[b], 0.0) + jnp.where(lane == GDN_V_HEADS + hv, ga[b], 0.0)
            dal = dal + jnp.where(lane1 == hv, gal[b], 0.0)
            ddb = ddb + jnp.where(lane1 == hv, gdb[b], 0.0)
        dba_ref[...] += dba
        dal_ref[...] += dal
        ddb_ref[...] += ddb

    small = pl.BlockSpec((1, GDN_V_HEADS), lambda n, j: (0, 0))
    return pl.pallas_call(
        body, name=name, grid=(nchunk, GDN_K_HEADS // GDN_HB),
        in_specs=specs + [pl.BlockSpec((None, GDN_HB, GDN_REP, d, d), lambda n, j: (at(n), j, 0, 0, 0)),
                          pl.BlockSpec((c, GDN_HB * GDN_REP * d), lambda n, j: (at(n), j))],
        out_specs=[pl.BlockSpec((c, GDN_HB * d), lambda n, j: (at(n), j)),
                   pl.BlockSpec((c, GDN_HB * d), lambda n, j: (at(n), j)),
                   pl.BlockSpec((c, GDN_HB * GDN_REP * d), lambda n, j: (at(n), j)),
                   pl.BlockSpec((c, 2 * GDN_V_HEADS), lambda n, j: (at(n), 0)),
                   small, small],
        out_shape=[jax.ShapeDtypeStruct((t, GDN_KW), F32), jax.ShapeDtypeStruct((t, GDN_KW), F32),
                   jax.ShapeDtypeStruct((t, GDN_VW), F32), jax.ShapeDtypeStruct((t, 2 * GDN_V_HEADS), F32),
                   jax.ShapeDtypeStruct((1, GDN_V_HEADS), F32), jax.ShapeDtypeStruct((1, GDN_V_HEADS), F32)],
        scratch_shapes=[pltpu.VMEM((GDN_K_HEADS, GDN_REP, d, d), F32)],
        compiler_params=_params("arbitrary", "arbitrary"),
    )(act, act, act, ba, alog, dtb, states, dout)


N_DEV = 8
ANY = pl.BlockSpec(memory_space=pl.ANY)


def _coords():
    return lax.axis_index("x"), lax.axis_index("y"), lax.axis_index("c")


def _other_chips(x, y):
    return [(1 - x, y), (x, 1 - y), (1 - x, 1 - y)]


def _remote(src, dst, send_sems, recv_sems, k, to):
    return pltpu.make_async_remote_copy(src_ref=src, dst_ref=dst, send_sem=send_sems.at[k], recv_sem=recv_sems.at[k],
                                        device_id=to, device_id_type=MESH)


def _dma_sems(n):
    return [pltpu.SemaphoreType.DMA((n,)), pltpu.SemaphoreType.DMA((n,))]


def _gather_quarters(parts, *, name):
    na = len(parts)

    def body(*refs):
        ins, outs = refs[:na], refs[na:2 * na]
        send_sems, recv_sems, local_sems = refs[2 * na:]
        x, y, c = _coords()
        sibling = (x, y, 1 - c)
        chips = _other_chips(x, y)
        mine, first, passed = [], [], []
        for a, (x_ref, out_ref) in enumerate(zip(ins, outs)):
            mine.append(pltpu.make_async_copy(x_ref, out_ref.at[2 * x + y], local_sems.at[a]))
            for j, (cx, cy) in enumerate(chips):
                first.append(_remote(x_ref.at[c], out_ref.at[2 * x + y, c], send_sems, recv_sems, 6 * a + j, (cx, cy, c)))
                passed.append(_remote(out_ref.at[2 * cx + cy, c], out_ref.at[2 * cx + cy, c], send_sems, recv_sems,
                                      6 * a + 3 + j, sibling))
        for cp in mine + first:
            cp.start()
        for a, (x_ref, out_ref) in enumerate(zip(ins, outs)):
            for j, (cx, cy) in enumerate(chips):
                _remote(x_ref.at[c], out_ref.at[2 * cx + cy, c], send_sems, recv_sems, 6 * a + j, (cx, cy, c)).wait_recv()
                passed[3 * a + j].start()
        for a, (x_ref, out_ref) in enumerate(zip(ins, outs)):
            for j, (cx, cy) in enumerate(chips):
                _remote(x_ref.at[c], out_ref.at[2 * cx + cy, 1 - c], send_sems, recv_sems, 6 * a + 3 + j,
                        sibling).wait_recv()
        for cp in first + passed:
            cp.wait_send()
        for cp in mine:
            cp.wait()

    return pl.pallas_call(
        body, name=name, in_specs=[ANY] * na, out_specs=[ANY] * na,
        out_shape=[jax.ShapeDtypeStruct((N_CHIPS,) + p.shape, p.dtype) for p in parts],
        scratch_shapes=_dma_sems(6 * na) + [pltpu.SemaphoreType.DMA((na,))],
    )(*parts)


def _swap_halves(grads, *, name):
    na = len(grads)

    def body(*refs):
        ins, outs = refs[:na], refs[na:2 * na]
        send_sems, recv_sems = refs[2 * na:]
        x, y, c = _coords()
        sends = [_remote(g_ref.at[j, 1 - c], o_ref.at[j], send_sems, recv_sems, N_CHIPS * a + j, (x, y, 1 - c))
                 for a, (g_ref, o_ref) in enumerate(zip(ins, outs)) for j in range(N_CHIPS)]
        for cp in sends:
            cp.start()
        for cp in sends:
            cp.wait()

    return pl.pallas_call(
        body, name=name, in_specs=[ANY] * na, out_specs=[ANY] * na,
        out_shape=[jax.ShapeDtypeStruct((N_CHIPS,) + g.shape[2:], g.dtype) for g in grads],
        scratch_shapes=_dma_sems(N_CHIPS * na),
    )(*grads)


def _scatter_quarters(pairs, *, name):
    na = len(pairs)

    def body(*refs):
        ins, outs = refs[:na], refs[na:4 * na]
        send_sems, recv_sems = refs[4 * na:]
        x, y, c = _coords()
        sends = [_remote(p_ref.at[2 * cx + cy], outs[3 * a + j], send_sems, recv_sems, 3 * a + j, (cx, cy, c))
                 for a, p_ref in enumerate(ins) for j, (cx, cy) in enumerate(_other_chips(x, y))]
        for cp in sends:
            cp.start()
        for cp in sends:
            cp.wait()

    out = pl.pallas_call(
        body, name=name, in_specs=[ANY] * na, out_specs=[ANY] * (3 * na),
        out_shape=[jax.ShapeDtypeStruct(p.shape[1:], p.dtype) for p in pairs for _ in range(3)],
        scratch_shapes=_dma_sems(3 * na),
    )(*pairs)
    return [out[3 * a:3 * a + 3] for a in range(na)]


def _share_halves(tots, *, name):
    na = len(tots)

    def body(*refs):
        ins, outs = refs[:na], refs[na:2 * na]
        send_sems, recv_sems = refs[2 * na:]
        x, y, c = _coords()
        sends = [_remote(t_ref, o_ref, send_sems, recv_sems, a, (x, y, 1 - c))
                 for a, (t_ref, o_ref) in enumerate(zip(ins, outs))]
        for cp in sends:
            cp.start()
        for cp in sends:
            cp.wait()

    return pl.pallas_call(
        body, name=name, in_specs=[ANY] * na, out_specs=[ANY] * na,
        out_shape=[jax.ShapeDtypeStruct(t.shape, t.dtype) for t in tots],
        scratch_shapes=_dma_sems(na),
    )(*tots)


def _gather_all(vec, *, name):
    m, w = vec.shape

    def body(x_ref, out_ref, send_sems, recv_sems, local_sem):
        x, y, c = _coords()
        me, sibling = (x, y, c), (x, y, 1 - c)
        chips = _other_chips(x, y)

        def rows(px, py, pc):
            return out_ref.at[pl.ds((4 * px + 2 * py + pc) * m, m), :]

        def copy(k, block, to, src=None):
            return _remote(rows(*block) if src is None else src, rows(*block), send_sems, recv_sems, k, to)

        mine = pltpu.make_async_copy(x_ref, rows(*me), local_sem)
        mine.start()
        first = [copy(0, me, sibling, src=x_ref)]
        first += [copy(1 + j, me, (*chip, c), src=x_ref) for j, chip in enumerate(chips)]
        for cp in first:
            cp.start()
        passed = [copy(4 + j, (*chip, c), sibling) for j, chip in enumerate(chips)]
        for j, chip in enumerate(chips):
            copy(1 + j, (*chip, c), me).wait_recv()
            passed[j].start()
        copy(0, sibling, me).wait_recv()
        for j, chip in enumerate(chips):
            copy(4 + j, (*chip, 1 - c), me).wait_recv()
        for cp in first + passed:
            cp.wait_send()
        mine.wait()

    vm = pl.BlockSpec(memory_space=pltpu.VMEM)
    return pl.pallas_call(
        body, name=name, in_specs=[vm], out_specs=vm, out_shape=jax.ShapeDtypeStruct((N_DEV * m, w), vec.dtype),
        scratch_shapes=_dma_sems(7) + [pltpu.SemaphoreType.DMA(())],
    )(vec)


def _sum_blocks(allv, n, *, name):
    m = allv.shape[0] // n

    def body(a_ref, o_ref):
        acc = a_ref[0:m, :]
        for d in range(1, n):
            acc = acc + a_ref[d * m:(d + 1) * m, :]
        o_ref[...] = acc

    return pl.pallas_call(body, name=name, out_shape=jax.ShapeDtypeStruct((m, allv.shape[1]), allv.dtype))(allv)


EW_BLOCK_BYTES = 1 << 20


def _ew_rows(rows, w):
    return _tile(rows, max(8, (EW_BLOCK_BYTES // (4 * w)) // 8 * 8), 8)


def _add_pair(g, got, c, *, name):
    _, _, rows, w = g.shape
    tr = _ew_rows(rows, w)

    def body(c_ref, g_ref, got_ref, o_ref):
        o_ref[...] = (g_ref[...] + got_ref[...]).astype(o_ref.dtype)

    blk = pl.BlockSpec((None, tr, w), lambda q, i, c_ref: (q, i, 0))
    return pl.pallas_call(
        body, name=name,
        grid_spec=pltpu.PrefetchScalarGridSpec(
            num_scalar_prefetch=1, grid=(N_CHIPS, rows // tr),
            in_specs=[pl.BlockSpec((None, None, tr, w), lambda q, i, c_ref: (q, c_ref[0], i, 0)), blk], out_specs=blk),
        out_shape=jax.ShapeDtypeStruct(got.shape, BF16),
        compiler_params=_params("parallel", "parallel"),
    )(c, g, got)


def _add_chips(pair, recv, chip, *, name):
    _, rows, w = pair.shape
    tr = _ew_rows(rows, w)

    def body(chip_ref, p_ref, r0_ref, r1_ref, r2_ref, o_ref):
        f = lambda r: r[...].astype(F32)
        o_ref[...] = ((f(p_ref) + f(r0_ref)) + f(r1_ref)) + f(r2_ref)

    blk = pl.BlockSpec((tr, w), lambda i, chip_ref: (i, 0))
    return pl.pallas_call(
        body, name=name,
        grid_spec=pltpu.PrefetchScalarGridSpec(
            num_scalar_prefetch=1, grid=(rows // tr,),
            in_specs=[pl.BlockSpec((None, tr, w), lambda i, chip_ref: (chip_ref[0], i, 0)), blk, blk, blk], out_specs=blk),
        out_shape=jax.ShapeDtypeStruct((rows, w), F32),
        compiler_params=_params("parallel"),
    )(chip, pair, *recv)


def _adamw_math(w, g, m, v):
    nm = ADAM_B1 * m + (1.0 - ADAM_B1) * g
    nv = ADAM_B2 * v + (1.0 - ADAM_B2) * (g * g)
    m_hat = nm / (1.0 - ADAM_B1 ** ADAM_STEP)
    v_hat = nv / (1.0 - ADAM_B2 ** ADAM_STEP)
    return -ADAM_LR * (m_hat / (jnp.sqrt(v_hat) + ADAM_EPS) + ADAM_WD * w), nm, nv


def _adamw(w, g, m, v, *, name):
    shape = w.shape
    last = shape[-1]
    w2, g2, m2, v2 = (a.reshape(-1, last) for a in (w, g, m, v))
    rows = w2.shape[0]
    tm = _ew_rows(rows, last)

    def body(w_ref, g_ref, m_ref, v_ref, d_ref, nm_ref, nv_ref):
        d_ref[...], nm_ref[...], nv_ref[...] = _adamw_math(w_ref[...], g_ref[...], m_ref[...], v_ref[...])

    spec = pl.BlockSpec((tm, last), lambda i: (i, 0))
    out = jax.ShapeDtypeStruct((rows, last), F32)
    d, nm, nv = pl.pallas_call(
        body, name=name, grid=(rows // tm,), in_specs=[spec] * 4, out_specs=[spec] * 3, out_shape=[out] * 3,
        compiler_params=_params("parallel"),
    )(w2, g2, m2, v2)
    return d.reshape(shape), nm.reshape(shape), nv.reshape(shape)


def _adamw_halves(w, m, v, mine, theirs, c, *, name):
    _, rows, wd = w.shape
    tr = _ew_rows(rows, wd)

    def body(c_ref, w_ref, m_ref, v_ref, a_ref, b_ref, g_ref, d_ref, nm_ref, nv_ref):
        g = jnp.where(pl.program_id(0) == c_ref[0], a_ref[...], b_ref[...])
        g_ref[...] = g
        d_ref[...], nm_ref[...], nv_ref[...] = _adamw_math(w_ref[...], g, m_ref[...], v_ref[...])

    full = pl.BlockSpec((None, tr, wd), lambda hf, i, c_ref: (hf, i, 0))
    half = pl.BlockSpec((tr, wd), lambda hf, i, c_ref: (i, 0))
    out = jax.ShapeDtypeStruct(w.shape, F32)
    return pl.pallas_call(
        body, name=name,
        grid_spec=pltpu.PrefetchScalarGridSpec(num_scalar_prefetch=1, grid=(2, rows // tr),
                                               in_specs=[full] * 3 + [half] * 2, out_specs=[full] * 4),
        out_shape=[out] * 4,
        compiler_params=_params("parallel", "parallel"),
    )(c, w, m, v, mine, theirs)


def _join_quarters(q, *, name):
    _, rows, n = q.shape
    tr = _tile(rows, 256, 16)

    def body(q_ref, o_ref):
        o_ref[...] = jnp.concatenate([q_ref[s] for s in range(N_CHIPS)], axis=1)

    return pl.pallas_call(
        body, name=name, grid=(rows // tr,),
        in_specs=[pl.BlockSpec((N_CHIPS, tr, n), lambda i: (0, i, 0))],
        out_specs=pl.BlockSpec((tr, N_CHIPS * n), lambda i: (i, 0)),
        out_shape=jax.ShapeDtypeStruct((rows, N_CHIPS * n), q.dtype),
        compiler_params=_params("parallel"),
    )(q)


def _split_quarters(full, *, name):
    rows, n4 = full.shape
    n = n4 // N_CHIPS
    tr = _tile(rows, 256, 16)

    def body(x_ref, o_ref):
        x = x_ref[...]
        for s in range(N_CHIPS):
            o_ref[s] = x[:, s * n:(s + 1) * n]

    return pl.pallas_call(
        body, name=name, grid=(rows // tr,),
        in_specs=[pl.BlockSpec((tr, n4), lambda i: (i, 0))],
        out_specs=pl.BlockSpec((N_CHIPS, tr, n), lambda i: (0, i, 0)),
        out_shape=jax.ShapeDtypeStruct((N_CHIPS, rows, n), full.dtype),
        compiler_params=_params("parallel"),
    )(full)


_WEIGHTS = ['ffn_norm', 'ffn_w_gate', 'ffn_w_up', 'ffn_w_down', 'mix_norm', 'att_w_in', 'att_q_norm', 'att_k_norm',
            'att_sinks', 'att_w_out', 'gdn_w_in', 'gdn_conv_w', 'gdn_a_log', 'gdn_dt_bias', 'gdn_out_norm', 'gdn_w_out',
            'ple_norm', 'ple_w_gate', 'ple_w_proj']
_BIG = ['ffn_w_gate', 'ffn_w_up', 'ffn_w_down', 'att_w_in', 'att_w_out', 'gdn_w_in', 'gdn_w_out', 'ple_w_gate',
        'ple_w_proj']
_SMALL_CUT = {'ffn_norm': 2, 'gdn_conv_w': 2}
_WHOLE = ['mix_norm', 'att_q_norm', 'att_k_norm', 'att_sinks', 'gdn_a_log', 'gdn_dt_bias', 'gdn_out_norm', 'ple_norm']
PACK_W = 1024
SMALL_ROW_MULT = 8


def _halves(a):
    return a.reshape(2, -1, a.shape[-1])


def _from_quarters(blk, axis):
    full = jnp.moveaxis(blk, 0, axis)
    shp = list(full.shape)
    shp[axis:axis + 2] = [shp[axis] * shp[axis + 1]]
    return full.reshape(shp)


def _to_quarters(full, axis):
    shp = list(full.shape)
    shp[axis:axis + 1] = [N_CHIPS, shp[axis] // N_CHIPS]
    return jnp.moveaxis(full.reshape(shp), axis, 0)


def _pack(parts, row_mult):
    flat = jnp.concatenate(parts, axis=-1)
    n = flat.shape[-1]
    rows = -(-n // (PACK_W * row_mult)) * row_mult
    return jnp.pad(flat, [(0, rows * PACK_W - n)]).reshape(rows, PACK_W)


def _unpack(flat, shapes):
    lead = flat.shape[:-2]
    flat = flat.reshape(lead + (-1,))
    out, off = [], 0
    for shp in shapes:
        n = math.prod(shp)
        out.append(flat[..., off:off + n].reshape(lead + tuple(shp)))
        off += n
    return out


FFN_TM = 1024


def _ffn_up(hn, wg, wu, at, *, name):
    t, d = hn.shape
    fq = wg.shape[-1]
    tm = _tile(t, FFN_TM)

    def body(h_ref, wg_ref, wu_ref, g_ref, u_ref, a_ref):
        h = h_ref[...]
        g, u = _dg(h, _b(wg_ref[...]), 1, 0), _dg(h, _b(wu_ref[...]), 1, 0)
        g_ref[...] = g.astype(BF16)
        u_ref[...] = u.astype(BF16)
        a_ref[...] = _f_swiglu(g, u)[0].astype(BF16)

    w_spec = pl.BlockSpec((None,) * (1 + len(at)) + (d, fq), lambda s, i: (s,) + at + (0, 0))
    o_spec = pl.BlockSpec((None, tm, fq), lambda s, i: (s, i, 0))
    out = jax.ShapeDtypeStruct((N_CHIPS, t, fq), BF16)
    return pl.pallas_call(
        body, name=name, grid=(N_CHIPS, t // tm),
        in_specs=[pl.BlockSpec((tm, d), lambda s, i: (i, 0)), w_spec, w_spec], out_specs=[o_spec] * 3,
        out_shape=[out] * 3, compiler_params=_params("parallel", "parallel"),
    )(hn, wg, wu)


def _ffn_d_up(dout, wd, g, u, at, *, name):
    t, d = dout.shape
    fq = wd.shape[-2]
    tm = _tile(t, FFN_TM)

    def body(do_ref, wd_ref, g_ref, u_ref, dg_ref, du_ref):
        da = _dg(_b(do_ref[...]), _b(wd_ref[...]), 1, 1) * 0.5
        _, vjp = jax.vjp(_f_swiglu, g_ref[...].astype(F32), u_ref[...].astype(F32))
        dg, du = vjp((da,))
        dg_ref[...] = dg.astype(BF16)
        du_ref[...] = du.astype(BF16)

    w_spec = pl.BlockSpec((None,) * (1 + len(at)) + (fq, d), lambda s, i: (s,) + at + (0, 0))
    o_spec = pl.BlockSpec((None, tm, fq), lambda s, i: (s, i, 0))
    out = jax.ShapeDtypeStruct((N_CHIPS, t, fq), BF16)
    return pl.pallas_call(
        body, name=name, grid=(N_CHIPS, t // tm),
        in_specs=[pl.BlockSpec((tm, d), lambda s, i: (i, 0)), w_spec, o_spec, o_spec], out_specs=[o_spec] * 2,
        out_shape=[out] * 2, compiler_params=_params("parallel", "parallel"),
    )(dout, wd, g, u)


def _ffn_fwd(h, gain, wg, wu, wd, at, tag):
    lead = (Q,) + at
    hn, = _row_fwd(_f_rms, [h], [gain], [(D_MODEL, BF16)], name=f"{tag}_norm")
    g, u, a = _ffn_up(hn, wg, wu, at, name=f"{tag}_up")
    out = _mm((a, (Q,)), (wd, lead), res=h, scale=0.5, name=f"{tag}_down")
    return out, (h, hn, g, u, a)


def _ffn_bwd(dout, saved, gain, wg, wu, wd, at, grads, tag):
    h, hn, g, u, a = saved
    lead = (Q,) + at
    dg, du = _ffn_d_up(dout, wd, g, u, at, name=f"{tag}_d_up")
    g_gate, g_up, g_down = grads
    g_down = _mm((a, (Q,)), dout, ta=True, scale=0.5, into=(g_down, lead), name=f"{tag}_dw_down")
    g_gate = _mm(hn, (dg, (Q,)), ta=True, into=(g_gate, lead), name=f"{tag}_dw_gate")
    g_up = _mm(hn, (du, (Q,)), ta=True, into=(g_up, lead), name=f"{tag}_dw_up")
    dhn = _mm((dg, (Q,)), (wg, lead), tb=True, name=f"{tag}_d_norm_gate")
    dhn = _mm((du, (Q,)), (wu, lead), tb=True, res=dhn, name=f"{tag}_d_norm_up")
    dh, dgain = _row_bwd(_f_rms_res, [h], [gain], [dhn, dout], [(0, F32)], [0], name=f"{tag}_d_in")
    return dh, dgain, (g_gate, g_up, g_down)


def _att_fwd(h, gain, w_in, qg, kg, sinks, w_out):
    hn, = _row_fwd(_f_rms, [h], [gain], [(D_MODEL, BF16)], name="att_norm")
    proj = _mm(hn, w_in, name="att_in")
    a, rtot = _sb_fwd(proj, name="att_sb")
    b = _swa_fwd(proj, qg, kg, sinks, name="att_swa")
    out = _mm(a, (w_out, (0,)), res=h, name="att_out_sb")
    out = _mm(b, (w_out, (1,)), res=out, name="att_out_swa")
    return out, (h, hn, proj, a, rtot, b)


def _att_bwd(dout, saved, gain, w_in, qg, kg, sinks, w_out):
    h, hn, proj, a, rtot, b = saved
    da = _mm(dout, (w_out, (0,)), tb=True, name="att_d_sb")
    db = _mm(dout, (w_out, (1,)), tb=True, name="att_d_swa")
    dw_out = jnp.zeros(w_out.shape, F32)
    dw_out = _mm(a, dout, ta=True, into=(dw_out, (0,)), name="att_dw_out_sb")
    dw_out = _mm(b, dout, ta=True, into=(dw_out, (1,)), name="att_dw_out_swa")
    dq, dk, dv = _sb_bwd(proj, rtot, da, name="att_sb_bwd")
    dqb, dkb, dvb, dqg, dkg, dsk = _swa_bwd(proj, qg, kg, sinks, db, name="att_swa_bwd")
    dproj = jnp.concatenate([dq, dk, dv, dqb, dkb, dvb], axis=1)
    dw_in = _mm(hn, dproj, ta=True, name="att_dw_in")
    dhn = _mm(dproj, w_in, tb=True, name="att_d_norm")
    dh, dgain = _row_bwd(_f_rms_res, [h], [gain], [dhn, dout], [(0, F32)], [0], name="att_d_in")
    return dh, dgain, dw_in, dqg, dkg, dsk, dw_out


def _gdn_layer_fwd(h, gain, w_in, conv_w, alog, dtb, out_gain, w_out):
    w_qkv, w_z, w_ba = w_in[:, :GDN_CONV_W], w_in[:, GDN_CONV_W:GDN_CONV_W + GDN_VW], w_in[:, GDN_CONV_W + GDN_VW:]
    hn, = _row_fwd(_f_rms, [h], [gain], [(D_MODEL, BF16)], name="gdn_norm")
    pq = _mm(hn, w_qkv, name="gdn_in_qkv")
    pz = _mm(hn, w_z, name="gdn_in_z")
    ba = _mm(hn, w_ba, name="gdn_in_ba")
    act = _conv_fwd(pq, conv_w, name="gdn_conv")
    o, states = _gdn_fwd(act, ba, alog, dtb, name="gdn_rule")
    y, = _row_fwd(_f_gdn_out, [o, pz], [out_gain], [(GDN_VW, BF16)], name="gdn_gate")
    out = _mm(y, w_out, res=h, name="gdn_out")
    return out, (h, hn, pq, pz, ba, act, o, states, y, (w_qkv, w_z, w_ba))


def _gdn_layer_bwd(dout, saved, gain, conv_w, alog, dtb, out_gain, w_out):
    h, hn, pq, pz, ba, act, o, states, y, (w_qkv, w_z, w_ba) = saved
    dy = _mm(dout, w_out, tb=True, name="gdn_d_gate")
    dw_out = _mm(y, dout, ta=True, name="gdn_dw_out")
    do, dpz, dout_gain = _row_bwd(_f_gdn_out, [o, pz], [out_gain], [dy], [(0, F32), (1, F32)], [0], name="gdn_gate_bwd")
    dq, dk, dv, dba, dal, ddb = _gdn_bwd(act, ba, alog, dtb, states, do, name="gdn_rule_bwd")
    dact = jnp.concatenate([dq, dk, dv], axis=1)
    dc = _conv_fwd(pq, conv_w, dact, name="gdn_conv_d_pre")
    dpq, dconv = _conv_bwd(pq, conv_w, dc, name="gdn_conv_bwd")
    dw_in = jnp.concatenate([_mm(hn, dpq, ta=True, name="gdn_dw_qkv"), _mm(hn, dpz, ta=True, name="gdn_dw_z"),
                             _mm(hn, dba, ta=True, name="gdn_dw_ba")], axis=1)
    dhn = _mm(dpq, w_qkv, tb=True, name="gdn_d_norm_qkv")
    dhn = _mm(dpz, w_z, tb=True, res=dhn, name="gdn_d_norm_z")
    dhn = _mm(dba, w_ba, tb=True, res=dhn, name="gdn_d_norm_ba")
    dh, dgain = _row_bwd(_f_rms_res, [h], [gain], [dhn, dout], [(0, F32)], [0], name="gdn_d_in")
    return dh, dgain, dw_in, dconv, dal, ddb, dout_gain, dw_out


def _ple_fwd(h, gain, w_gate, w_proj, pe, tag):
    hn, = _row_fwd(_f_rms, [h], [gain], [(D_MODEL, BF16)], name=f"{tag}_norm")
    gl = _mm(hn, w_gate, name=f"{tag}_gate")
    pp = _mm(pe, w_proj, name=f"{tag}_proj")
    out, = _row_fwd(_f_ple, [h, gl, pp], [], [(D_MODEL, F32)], name=f"{tag}_mix")
    return out, (h, hn, gl, pp)


def _ple_bwd(dout, saved, gain, w_gate, pe, tag):
    h, hn, gl, pp = saved
    dha, dgl, dpp = _row_bwd(_f_ple, [h, gl, pp], [], [dout], [(0, F32), (1, BF16), (2, BF16)], [], name=f"{tag}_mix_bwd")
    dw_gate = _mm(hn, dgl, ta=True, name=f"{tag}_dw_gate")
    dw_proj = _mm(pe, dpp, ta=True, name=f"{tag}_dw_proj")
    dhn = _mm(dgl, w_gate, tb=True, name=f"{tag}_d_norm")
    dh, dgain = _row_bwd(_f_rms_res, [h], [gain], [dhn, dha], [(0, F32)], [0], name=f"{tag}_d_in")
    return dh, dgain, dw_gate, dw_proj


def kernel(x, p, ffn_norm, ffn_w_gate, ffn_w_up, ffn_w_down, mix_norm, att_w_in, att_q_norm, att_k_norm, att_sinks, att_w_out, gdn_w_in, gdn_conv_w, gdn_a_log, gdn_dt_bias, gdn_out_norm, gdn_w_out, ple_norm, ple_w_gate, ple_w_proj, loss_target, m_ffn_norm, m_ffn_w_gate, m_ffn_w_up, m_ffn_w_down, m_mix_norm, m_att_w_in, m_att_q_norm, m_att_k_norm, m_att_sinks, m_att_w_out, m_gdn_w_in, m_gdn_conv_w, m_gdn_a_log, m_gdn_dt_bias, m_gdn_out_norm, m_gdn_w_out, m_ple_norm, m_ple_w_gate, m_ple_w_proj, v_ffn_norm, v_ffn_w_gate, v_ffn_w_up, v_ffn_w_down, v_mix_norm, v_att_w_in, v_att_q_norm, v_att_k_norm, v_att_sinks, v_att_w_out, v_gdn_w_in, v_gdn_conv_w, v_gdn_a_log, v_gdn_dt_bias, v_gdn_out_norm, v_gdn_w_out, v_ple_norm, v_ple_w_gate, v_ple_w_proj):
    arg = dict(locals())
    cx, cy, cc = _coords()
    chip = (2 * cx + cy).astype(jnp.int32).reshape(1)
    core = cc.astype(jnp.int32).reshape(1)
    n_layers = ffn_norm.shape[0]

    gathered = _gather_quarters([_halves(arg[n].astype(BF16)) for n in _BIG], name="gather_weights")
    wq = {n: g.reshape((N_CHIPS,) + arg[n].shape) for n, g in zip(_BIG, gathered)}
    wt = {}
    wt['att_w_in'] = _join_quarters(wq['att_w_in'][:, 0], name="att_w_in_join")
    wt['gdn_w_in'] = _join_quarters(wq['gdn_w_in'][:, 0], name="gdn_w_in_join")
    wt['att_w_out'] = wq['att_w_out'].reshape(2, SB_W, D_MODEL)
    wt['gdn_w_out'] = wq['gdn_w_out'].reshape(GDN_VW, D_MODEL)
    wt['ple_w_gate'] = _from_quarters(wq['ple_w_gate'], 1)
    wt['ple_w_proj'] = _from_quarters(wq['ple_w_proj'], 2)

    small_names = list(_SMALL_CUT)
    small_shapes = [arg[n].shape for n in small_names]
    svec = _pack([arg[n].reshape(-1) for n in small_names], SMALL_ROW_MULT)
    srows = svec.shape[0]
    sall = _gather_all(svec, name="gather_gains").reshape(N_CHIPS, 2, srows, PACK_W)[:, 0]
    for n, q in zip(small_names, _unpack(sall, small_shapes)):
        wt[n] = _from_quarters(q, _SMALL_CUT[n])
    row = lambda v: v.reshape(1, -1)

    h = x[0]
    tape = []
    ffn_w = (wq['ffn_w_gate'], wq['ffn_w_up'], wq['ffn_w_down'])
    for i in range(n_layers):
        j = i // 2
        h, s0 = _ffn_fwd(h, row(wt['ffn_norm'][i, 0]), *ffn_w, (i, 0), f"ffn{i}a")
        if i % 2 == 0:
            h, sm = _att_fwd(h, row(mix_norm[i]), wt['att_w_in'], att_q_norm[j:j + 1], att_k_norm[j:j + 1],
                             att_sinks[j:j + 1], wt['att_w_out'])
        else:
            h, sm = _gdn_layer_fwd(h, row(mix_norm[i]), wt['gdn_w_in'], wt['gdn_conv_w'][j], gdn_a_log[j:j + 1],
                                   gdn_dt_bias[j:j + 1], gdn_out_norm[j:j + 1], wt['gdn_w_out'])
        h, s1 = _ffn_fwd(h, row(wt['ffn_norm'][i, 1]), *ffn_w, (i, 1), f"ffn{i}b")
        h, sp = _ple_fwd(h, row(ple_norm[i]), wt['ple_w_gate'][i], wt['ple_w_proj'][i], p[i, 0], f"ple{i}")
        tape.append((s0, sm, s1, sp))

    dh, loss_local = _loss_head(h, loss_target[0], name="loss_head")
    loss = lax.psum(loss_local, ("x", "y", "c"))

    gr = {}
    ffn_g = tuple(jnp.zeros(w.shape, F32) for w in ffn_w)
    d_ffn_norm = [[None, None] for _ in range(n_layers)]
    d_mix, d_ple_norm, d_ple_gate, d_ple_proj = [None] * n_layers, [None] * n_layers, [None] * n_layers, [None] * n_layers
    for i in reversed(range(n_layers)):
        j = i // 2
        s0, sm, s1, sp = tape[i]
        dh, d_ple_norm[i], d_ple_gate[i], d_ple_proj[i] = _ple_bwd(dh, sp, row(ple_norm[i]), wt['ple_w_gate'][i], p[i, 0],
                                                                   f"ple{i}")
        dh, d_ffn_norm[i][1], ffn_g = _ffn_bwd(dh, s1, row(wt['ffn_norm'][i, 1]), *ffn_w, (i, 1), ffn_g, f"ffn{i}b")
        if i % 2 == 0:
            (dh, d_mix[i], dw_in, gr['att_q_norm'], gr['att_k_norm'], gr['att_sinks'],
             dw_out) = _att_bwd(dh, sm, row(mix_norm[i]), wt['att_w_in'], att_q_norm[j:j + 1],
                                att_k_norm[j:j + 1], att_sinks[j:j + 1], wt['att_w_out'])
            gr['att_w_in'] = _split_quarters(dw_in, name="att_dw_in_split")
            gr['att_w_out'] = dw_out
        else:
            (dh, d_mix[i], dw_in, dconv, gr['gdn_a_log'], gr['gdn_dt_bias'], gr['gdn_out_norm'],
             dw_out) = _gdn_layer_bwd(dh, sm, row(mix_norm[i]), wt['gdn_conv_w'][j], gdn_a_log[j:j + 1],
                                      gdn_dt_bias[j:j + 1], gdn_out_norm[j:j + 1], wt['gdn_w_out'])
            gr['gdn_w_in'] = _split_quarters(dw_in, name="gdn_dw_in_split")
            gr['gdn_w_out'] = dw_out
            gr['gdn_conv_w'] = dconv[None]
        dh, d_ffn_norm[i][0], ffn_g = _ffn_bwd(dh, s0, row(wt['ffn_norm'][i, 0]), *ffn_w, (i, 0), ffn_g, f"ffn{i}a")
    grad_x = dh[None]

    gr['ffn_w_gate'], gr['ffn_w_up'], gr['ffn_w_down'] = ffn_g
    gr['ple_w_gate'] = _to_quarters(jnp.stack(d_ple_gate), 1)
    gr['ple_w_proj'] = _to_quarters(jnp.stack(d_ple_proj), 2)
    gr['ffn_norm'] = jnp.stack([jnp.stack([d_ffn_norm[i][k][0] for k in range(2)]) for i in range(n_layers)])
    gr['mix_norm'] = jnp.concatenate(d_mix, axis=0)
    gr['ple_norm'] = jnp.concatenate(d_ple_norm, axis=0)

    gq = [gr[n].reshape((N_CHIPS, 2, -1, arg[n].shape[-1])) for n in _BIG]
    got = _swap_halves(gq, name="grad_swap_halves")
    pairs = [_add_pair(g, o, core, name=f"grad_add_pair_{n}") for n, g, o in zip(_BIG, gq, got)]
    recv = _scatter_quarters(pairs, name="grad_scatter")
    tots = [_add_chips(pr, rc, chip, name=f"grad_add_chips_{n}") for n, pr, rc in zip(_BIG, pairs, recv)]
    theirs = _share_halves(tots, name="grad_share")

    whole_shapes = [arg[n].shape for n in _WHOLE]
    cut_full_shapes = [gr[n].shape for n in small_names]
    gvec = _pack([gr[n].reshape(-1) for n in _WHOLE + small_names], SMALL_ROW_MULT)
    gall = _sum_blocks(_gather_all(gvec, name="gather_small_grads"), N_DEV, name="sum_small_grads")
    parts = _unpack(gall, whole_shapes + cut_full_shapes)
    gsum = dict(zip(_WHOLE, parts))
    for n, g in zip(small_names, parts[len(_WHOLE):]):
        gsum[n] = lax.dynamic_index_in_dim(_to_quarters(g, _SMALL_CUT[n]), chip[0], axis=0, keepdims=False)

    delta, new_m, new_v = {}, {}, {}
    for n, mine, other in zip(_BIG, tots, theirs):
        res = _adamw_halves(_halves(arg[n]), _halves(arg["m_" + n]), _halves(arg["v_" + n]), mine, other, core,
                            name=f"adamw_{n}")
        gsum[n], delta[n], new_m[n], new_v[n] = (r.reshape(arg[n].shape) for r in res)
    for n in _WHOLE + small_names:
        delta[n], new_m[n], new_v[n] = _adamw(arg[n], gsum[n], arg["m_" + n], arg["v_" + n], name=f"adamw_{n}")
    return (loss, grad_x, *[gsum[n] for n in _WEIGHTS], *[delta[n] for n in _WEIGHTS],
            *[new_m[n] for n in _WEIGHTS], *[new_v[n] for n in _WEIGHTS])
```

```python
import functools
import math

import jax
import jax.numpy as jnp
from jax import lax
from jax.experimental import pallas as pl
from jax.experimental.pallas import tpu as pltpu

F32 = jnp.float32
BF16 = jnp.bfloat16
MESH = pl.DeviceIdType.MESH

LANES = 128
VMEM_LIMIT_BYTES = 56 * 1024 * 1024

EPS = 1e-6
D_MODEL = 1024
HEAD_DIM = 64
SB_HEADS = 8
SWA_HEADS = 8
SWA_KV_HEADS = 2
WINDOW = 128
GDN_K_HEADS = 8
GDN_V_HEADS = 16
GDN_HEAD_DIM = 128
GDN_CONV = 4
GDN_CHUNK = 64
SB_W = SB_HEADS * HEAD_DIM
SWA_QW = SWA_HEADS * HEAD_DIM
SWA_KVW = SWA_KV_HEADS * HEAD_DIM
GDN_KW = GDN_K_HEADS * GDN_HEAD_DIM
GDN_VW = GDN_V_HEADS * GDN_HEAD_DIM
GDN_CONV_W = 2 * GDN_KW + GDN_VW

ADAM_LR = 0.001
ADAM_B1 = 0.9
ADAM_B2 = 0.999
ADAM_EPS = 1e-08
ADAM_WD = 0.01
ADAM_STEP = 10

NEG = -1e30


def _params(*sem):
    return pltpu.CompilerParams(dimension_semantics=sem or None, vmem_limit_bytes=VMEM_LIMIT_BYTES)


def _tile(n, cap, align=LANES):
    if n <= cap:
        return n
    for t in range(cap - cap % align, 0, -align):
        if n % t == 0:
            return t
    return n


N_CHIPS = 4
MM_VMEM_BUDGET_BYTES = 40 * 1024 * 1024
Q = "q"


def _opnd(x):
    return x if isinstance(x, tuple) else (x, ())


def _mm(a, b, *, name, ta=False, tb=False, out_dtype=F32, res=None, scale=1.0, out_q=False, into=None,
        tm=None, tn=1024, tk=1024):
    (a_arr, a_lead), (b_arr, b_lead) = _opnd(a), _opnd(b)
    (k_a, m) = a_arr.shape[-2:] if ta else a_arr.shape[-2:][::-1]
    (n, k_b) = b_arr.shape[-2:] if tb else b_arr.shape[-2:][::-1]
    if into is not None:
        out_arr, out_lead = into
        out_q, out_dtype = Q in out_lead, out_arr.dtype
    else:
        out_lead = (Q,) if out_q else ()
    red_q = (Q in a_lead or Q in b_lead) and not out_q
    kq = min(k_a, k_b)
    assert (k_a == k_b) or (red_q and max(k_a, k_b) == N_CHIPS * kq), (a_arr.shape, b_arr.shape)
    tn, tk = _tile(n, tn), _tile(kq, tk)
    if tm is None:
        r_item = _opnd(res)[0].dtype.itemsize if res is not None else 0
        per_row = 2 * (tk * a_arr.dtype.itemsize + tn * (jnp.dtype(out_dtype).itemsize + r_item)) + 4 * tn
        room = MM_VMEM_BUDGET_BYTES - 2 * tk * tn * b_arr.dtype.itemsize
        tm = next(c for c in (4096, 2048, 1024, 512, 256, 128) if c * per_row <= room or c == 128)
    tm = _tile(m, tm)
    nk = kq // tk
    ksteps = nk * (N_CHIPS if red_q else 1)
    dims = (((0 if ta else 1,), (1 if tb else 0,)), ((), ()))
    has_res = res is not None

    def body(*refs):
        a_ref, b_ref = refs[0], refs[1]
        o_ref, acc_ref = refs[-2], refs[-1]
        k = pl.program_id(3)

        @pl.when(k == 0)
        def _():
            acc_ref[...] = jnp.zeros_like(acc_ref)

        acc_ref[...] += lax.dot_general(a_ref[...].astype(BF16), b_ref[...].astype(BF16), dims,
                                        preferred_element_type=F32)

        @pl.when(k == ksteps - 1)
        def _():
            r = acc_ref[...]
            if scale != 1.0:
                r = r * scale
            if has_res:
                r = r + refs[2][...].astype(F32)
            o_ref[...] = r.astype(o_ref.dtype)

    def spec(lead, blk, pos):
        def index(s, i, j, k):
            kk = k % nk if (red_q and Q in lead) else k
            quarter = s if out_q else k // nk
            return tuple(quarter if l == Q else l for l in lead) + pos(i, j, kk)
        return pl.BlockSpec((None,) * len(lead) + blk, index)

    a_spec = spec(a_lead, (tk, tm), lambda i, j, k: (k, i)) if ta else spec(a_lead, (tm, tk), lambda i, j, k: (i, k))
    b_spec = spec(b_lead, (tn, tk), lambda i, j, k: (j, k)) if tb else spec(b_lead, (tk, tn), lambda i, j, k: (k, j))
    o_spec = spec(out_lead, (tm, tn), lambda i, j, k: (i, j))
    in_specs, args = [a_spec, b_spec], [a_arr, b_arr]
    if has_res:
        r_arr, r_lead = _opnd(res)
        in_specs.append(spec(r_lead, (tm, tn), lambda i, j, k: (i, j)))
        args.append(r_arr)
    aliases = {}
    if into is not None:
        in_specs.append(pl.BlockSpec(memory_space=pl.ANY))
        args.append(out_arr)
        aliases = {len(args) - 1: 0}
        out_shape = jax.ShapeDtypeStruct(out_arr.shape, out_arr.dtype)
    else:
        out_shape = jax.ShapeDtypeStruct(((N_CHIPS,) if out_q else ()) + (m, n), out_dtype)
    return pl.pallas_call(
        body, name=name, grid=(N_CHIPS if out_q else 1, m // tm, n // tn, ksteps), in_specs=in_specs, out_specs=o_spec,
        out_shape=out_shape, scratch_shapes=[pltpu.VMEM((tm, tn), F32)], input_output_aliases=aliases,
        compiler_params=_params("parallel", "parallel", "parallel", "arbitrary"),
    )(*args)


def _row_spec(r, tm):
    if isinstance(r, tuple):
        arr, width, cb = r
        return arr, pl.BlockSpec((tm, width), lambda i, cb=cb: (i, cb))
    return r, pl.BlockSpec((tm, r.shape[1]), lambda i: (i, 0))


def _const_spec(c):
    return pl.BlockSpec(c.shape, lambda i: (0,) * c.ndim)


def _row_fwd(fn, rows, consts, outs, *, name, tm=256):
    tm = _tile(_row_spec(rows[0], tm)[0].shape[0], tm, 8)
    arrs, specs = zip(*[_row_spec(r, tm) for r in rows])
    t = arrs[0].shape[0]
    nr, nc = len(rows), len(consts)

    def body(*refs):
        vals = [r[...].astype(F32) for r in refs[:nr + nc]]
        res = fn(*vals)
        for o_ref, v in zip(refs[nr + nc:], res):
            o_ref[...] = v.astype(o_ref.dtype)

    out = pl.pallas_call(
        body, name=name, grid=(t // tm,),
        in_specs=list(specs) + [_const_spec(c) for c in consts],
        out_specs=[pl.BlockSpec((tm, w), lambda i: (i, 0)) for w, _ in outs],
        out_shape=[jax.ShapeDtypeStruct((t, w), dt) for w, dt in outs],
        compiler_params=_params("parallel"),
    )(*arrs, *consts)
    return list(out)


def _row_bwd(fn, rows, consts, cts, row_grads, const_grads, *, name, tm=256):
    tm = _tile(_row_spec(rows[0], tm)[0].shape[0], tm, 8)
    arrs, specs = zip(*[_row_spec(r, tm) for r in rows])
    ct_arrs, ct_specs = zip(*[_row_spec(r, tm) for r in cts])
    t = arrs[0].shape[0]
    nr, nc, nt = len(rows), len(consts), len(cts)
    n_in = nr + nc + nt

    def body(*refs):
        vals = [r[...].astype(F32) for r in refs[:nr + nc]]
        ctv = tuple(r[...].astype(F32) for r in refs[nr + nc:n_in])
        _, vjp = jax.vjp(fn, *vals)
        g = vjp(ctv)
        outs = refs[n_in:]
        for (idx, _), o_ref in zip(row_grads, outs[:len(row_grads)]):
            o_ref[...] = g[idx].astype(o_ref.dtype)
        first = pl.program_id(0) == 0
        for ci, o_ref in zip(const_grads, outs[len(row_grads):]):
            @pl.when(first)
            def _(o_ref=o_ref):
                o_ref[...] = jnp.zeros_like(o_ref)

            o_ref[...] += g[nr + ci]

    widths = [(_row_spec(rows[idx], tm)[1].block_shape[1], dt) for idx, dt in row_grads]
    out = pl.pallas_call(
        body, name=name, grid=(t // tm,),
        in_specs=list(specs) + [_const_spec(c) for c in consts] + list(ct_specs),
        out_specs=[pl.BlockSpec((tm, w), lambda i: (i, 0)) for w, _ in widths]
        + [_const_spec(consts[ci]) for ci in const_grads],
        out_shape=[jax.ShapeDtypeStruct((t, w), dt) for w, dt in widths]
        + [jax.ShapeDtypeStruct(consts[ci].shape, F32) for ci in const_grads],
        compiler_params=_params("arbitrary"),
    )(*arrs, *consts, *ct_arrs)
    return list(out)


def _rms(x, g):
    return x * lax.rsqrt(jnp.mean(x * x, axis=-1, keepdims=True) + EPS) * g


def _f_rms(h, g):
    return (_rms(h, g),)


def _f_rms_res(h, g):
    return (_rms(h, g), h)


def _f_swiglu(g, u):
    return (g * jax.nn.sigmoid(g) * u,)


def _f_ple(h, gl, pp):
    return (h + jax.nn.sigmoid(gl) * pp,)


def _f_gdn_out(o, z, gain):
    outs = []
    for hd in range(GDN_V_HEADS):
        sl = slice(hd * GDN_HEAD_DIM, (hd + 1) * GDN_HEAD_DIM)
        oh, zh = o[:, sl], z[:, sl]
        outs.append(_rms(oh, gain) * (zh * jax.nn.sigmoid(zh)))
    return (jnp.concatenate(outs, axis=1),)


def _loss_head(y, target, *, name, tm=512):
    t, d = y.shape
    tm = _tile(t, tm, 8)

    def body(y_ref, t_ref, dy_ref, l_ref):
        @pl.when(pl.program_id(0) == 0)
        def _():
            l_ref[...] = jnp.zeros_like(l_ref)

        e = y_ref[...] - t_ref[...]
        dy_ref[...] = e * (1.0 / d)
        l_ref[...] += jnp.sum(e * e) * (0.5 / d)

    dy, l = pl.pallas_call(
        body, name=name, grid=(t // tm,),
        in_specs=[pl.BlockSpec((tm, d), lambda i: (i, 0))] * 2,
        out_specs=[pl.BlockSpec((tm, d), lambda i: (i, 0)), pl.BlockSpec((8, LANES), lambda i: (0, 0))],
        out_shape=[jax.ShapeDtypeStruct((t, d), F32), jax.ShapeDtypeStruct((8, LANES), F32)],
        compiler_params=_params("arbitrary"),
    )(y, target)
    return dy, l[0, 0]


def _dg(a, b, ca, cb):
    nb = a.ndim - 2
    batch = tuple(range(nb))
    return lax.dot_general(a, b, (((ca + nb,), (cb + nb,)), (batch, batch)), preferred_element_type=F32)


def _b(x):
    return x.astype(BF16)


@jax.custom_vjp
def _bdot(a, b):
    return _dg(_b(a), _b(b), 1, 0)


def _bdot_fwd(a, b):
    return _bdot(a, b), (a, b)


def _bdot_bwd(r, ct):
    a, b = r
    return _dg(_b(ct), _b(b), 1, 1), _dg(_b(a), _b(ct), 0, 0)


_bdot.defvjp(_bdot_fwd, _bdot_bwd)


@jax.custom_vjp
def _bdot_nt(a, b):
    return _dg(_b(a), _b(b), 1, 1)


def _bdot_nt_fwd(a, b):
    return _bdot_nt(a, b), (a, b)


def _bdot_nt_bwd(r, ct):
    a, b = r
    return _dg(_b(ct), _b(b), 1, 0), _dg(_b(ct), _b(a), 0, 0)


_bdot_nt.defvjp(_bdot_nt_fwd, _bdot_nt_bwd)


@jax.custom_vjp
def _bdot_tn(a, b):
    return _dg(_b(a), _b(b), 0, 0)


def _bdot_tn_fwd(a, b):
    return _bdot_tn(a, b), (a, b)


def _bdot_tn_bwd(r, ct):
    a, b = r
    return _dg(_b(b), _b(ct), 1, 1), _dg(_b(a), _b(ct), 1, 0)


_bdot_tn.defvjp(_bdot_tn_fwd, _bdot_tn_bwd)


def _two(x):
    hi = x.astype(BF16)
    return hi, (x - hi.astype(F32)).astype(BF16)


def _dg3(a, b, ca, cb):
    (ah, al), (bh, bl) = _two(a), _two(b)
    return _dg(ah, bh, ca, cb) + (_dg(ah, bl, ca, cb) + _dg(al, bh, ca, cb))


@jax.custom_vjp
def _hdot(a, b):
    return _dg3(a, b, 1, 0)


def _hdot_fwd(a, b):
    return _hdot(a, b), (a, b)


def _hdot_bwd(r, ct):
    a, b = r
    return _dg3(ct, b, 1, 1), _dg3(a, ct, 0, 0)


_hdot.defvjp(_hdot_fwd, _hdot_bwd)


@jax.custom_vjp
def _unit_lower_inverse(x):
    c = x.shape[-1]
    eye = (lax.broadcasted_iota(jnp.int32, x.shape, 1) == lax.broadcasted_iota(jnp.int32, x.shape, 2)).astype(F32)
    inv, pw = eye + x, x
    for _ in range(int(math.log2(c)) - 1):
        pw = _dg3(pw, pw, 1, 0)
        inv = inv + _dg3(inv, pw, 1, 0)
    return inv


def _unit_lower_inverse_fwd(x):
    inv = _unit_lower_inverse(x)
    return inv, inv


def _unit_lower_inverse_bwd(inv, ct):
    return (_dg3(_dg3(inv, ct, 0, 0), inv, 1, 1),)


_unit_lower_inverse.defvjp(_unit_lower_inverse_fwd, _unit_lower_inverse_bwd)


def _split_dot(x, u):
    hi, lo = _two(x)
    return _dg(hi, u, 1, 0) + _dg(lo, u, 1, 0)


@jax.custom_vjp
def _ldot(l01, x):
    hi, lo = _two(x)
    l01 = l01.astype(BF16)
    return _dg(l01, hi, 1, 0) + _dg(l01, lo, 1, 0)


def _ldot_fwd(l01, x):
    return _ldot(l01, x), l01


def _ldot_bwd(l01, ct):
    hi, lo = _two(ct)
    l01b = l01.astype(BF16)
    return jnp.zeros_like(l01), _dg(l01b, hi, 0, 0) + _dg(l01b, lo, 0, 0)


_ldot.defvjp(_ldot_fwd, _ldot_bwd)


SB_BLK = 128
SB_KEYS = 512
SB_PAIRS = 2
SB_SCALE = HEAD_DIM ** -0.5


def _log_sigmoid(z):
    return jnp.minimum(z, 0.0) - jnp.log(1.0 + jnp.exp(-jnp.abs(z)))


def _sb_consts(t):
    kb = min(SB_KEYS, t)
    nh = 2 * SB_PAIRS
    lane = lax.broadcasted_iota(jnp.int32, (nh, SB_BLK, kb), 2)
    row = lax.broadcasted_iota(jnp.int32, (nh, SB_BLK, kb), 1)
    ur = lax.broadcasted_iota(jnp.int32, (kb, kb), 0)
    uc = lax.broadcasted_iota(jnp.int32, (kb, kb), 1)
    return kb, nh, lane, row, ur, uc


def _sb_heads(x):
    head0 = lax.broadcasted_iota(jnp.int32, (x.shape[0], LANES), 1) < HEAD_DIM
    out = []
    for p in range(SB_PAIRS):
        blk = x[:, p * LANES:(p + 1) * LANES]
        out += [jnp.where(head0, blk, 0.0), jnp.where(head0, 0.0, blk)]
    return jnp.stack(out)


def _sb_pairs(x):
    return jnp.stack([x[:, (h // 2) * LANES:(h // 2 + 1) * LANES] for h in range(2 * SB_PAIRS)])


def _sb_merge(x):
    head0 = lax.broadcasted_iota(jnp.int32, (x.shape[1], LANES), 1) < HEAD_DIM
    return jnp.concatenate([jnp.where(head0, x[2 * p], x[2 * p + 1]) for p in range(SB_PAIRS)], axis=1)


def _sb_rows_dot(x, u):
    nh, rows, k = x.shape
    return _split_dot(x.reshape(nh * rows, k), u).reshape(nh, rows, k)


def _sb_fwd(proj, *, name):
    t = proj.shape[0]
    nb = t // SB_BLK
    width = SB_PAIRS * LANES
    ng = SB_W // width

    def body(q_ref, k_ref, v_ref, o_ref, r_ref):
        i = pl.program_id(1)
        kb, nh, lane, row, ur, uc = _sb_consts(t)
        u_suffix = (ur >= uc).astype(BF16)
        qh = _b(_sb_heads(q_ref[...]) * SB_SCALE)
        diag = (i * SB_BLK) // kb

        def block(j, carry, masked):
            acc, car = carry
            keys = pl.ds(pl.multiple_of(j * kb, kb), kb)
            kj, vj = _b(_sb_pairs(k_ref[keys, :])), _b(_sb_pairs(v_ref[keys, :]))
            z = _dg(qh, kj, 1, 1)
            lk = _log_sigmoid(-z)
            if masked:
                causal = (j * kb + lane) < (i * SB_BLK + row)
                lk = jnp.where(causal, lk, 0.0)
            suf = _sb_rows_dot(lk, u_suffix) + car
            w = jnp.exp(z + suf)
            if masked:
                w = jnp.where(causal, w, 0.0)
            return acc + _dg(_b(w), vj, 1, 0), suf[:, :, 0:1]

        zero = (jnp.zeros((nh, SB_BLK, LANES), F32), jnp.zeros((nh, SB_BLK, 1), F32))
        carry = block(diag, zero, True)
        acc, car = lax.fori_loop(0, diag, lambda s, c: block(diag - 1 - s, c, False), carry)
        o_ref[...] = _sb_merge(acc)
        r_ref[...] = _sb_merge(jnp.broadcast_to(car, (nh, SB_BLK, LANES)))

    return pl.pallas_call(
        body, name=name, grid=(ng, nb),
        in_specs=[pl.BlockSpec((SB_BLK, width), lambda p, i: (i, p)),
                  pl.BlockSpec((t, width), lambda p, i: (0, ng + p)),
                  pl.BlockSpec((t, width), lambda p, i: (0, 2 * ng + p))],
        out_specs=[pl.BlockSpec((SB_BLK, width), lambda p, i: (i, p))] * 2,
        out_shape=[jax.ShapeDtypeStruct((t, SB_W), F32)] * 2,
        compiler_params=_params("parallel", "arbitrary"),
    )(proj, proj, proj)


def _sb_bwd(proj, rtot, dout, *, name):
    t = proj.shape[0]
    nb = t // SB_BLK
    width = SB_PAIRS * LANES
    ng = SB_W // width

    def body(q_ref, k_ref, v_ref, r_ref, do_ref, dq_ref, dk_ref, dv_ref):
        i = pl.program_id(1)
        kb, nh, lane, row, ur, uc = _sb_consts(t)
        u_incl = (ur <= uc).astype(BF16)
        u_excl = (ur < uc).astype(BF16)
        q, do = q_ref[...], do_ref[...]
        qh, doh = _b(_sb_heads(q) * SB_SCALE), _b(_sb_heads(do))
        qb, dob = _b(_sb_pairs(q) * SB_SCALE), _b(_sb_pairs(do))
        rh = jnp.min(_sb_heads(r_ref[...]), axis=2, keepdims=True)
        diag = (i * SB_BLK) // kb

        @pl.when(i == 0)
        def _():
            dk_ref[...] = jnp.zeros_like(dk_ref)
            dv_ref[...] = jnp.zeros_like(dv_ref)

        def block(j, carry, masked):
            dq_acc, clk, ce = carry
            keys = pl.ds(pl.multiple_of(j * kb, kb), kb)
            kj, vj = _b(_sb_pairs(k_ref[keys, :])), _b(_sb_pairs(v_ref[keys, :]))
            z = _dg(qh, kj, 1, 1)
            lk = _log_sigmoid(-z)
            ls = z + lk
            if masked:
                causal = (j * kb + lane) < (i * SB_BLK + row)
                lk = jnp.where(causal, lk, 0.0)
            pre = _sb_rows_dot(lk, u_incl) + clk
            w = jnp.exp(ls + (rh - pre))
            if masked:
                w = jnp.where(causal, w, 0.0)
            e = _dg(doh, vj, 1, 1) * w
            pre_e = _sb_rows_dot(e, u_excl) + ce
            sig = jnp.exp(ls)
            dz = e - sig * (e + pre_e)
            if masked:
                dz = jnp.where(causal, dz, 0.0)
            dzb = _b(dz)
            dk_ref[keys, :] += _sb_merge(_dg(dzb, qb, 0, 0))
            dv_ref[keys, :] += _sb_merge(_dg(_b(w), dob, 0, 0))
            return dq_acc + _dg(dzb, kj, 1, 0), pre[:, :, kb - 1:], pre_e[:, :, kb - 1:] + e[:, :, kb - 1:]

        zero = (jnp.zeros((nh, SB_BLK, LANES), F32), jnp.zeros((nh, SB_BLK, 1), F32), jnp.zeros((nh, SB_BLK, 1), F32))
        carry = lax.fori_loop(0, diag, lambda j, c: block(j, c, False), zero)
        dq_acc, _, _ = block(diag, carry, True)
        dq_ref[...] = _sb_merge(dq_acc) * SB_SCALE

    blk = pl.BlockSpec((SB_BLK, width), lambda p, i: (i, p))
    whole = pl.BlockSpec((t, width), lambda p, i: (0, p))
    return pl.pallas_call(
        body, name=name, grid=(ng, nb),
        in_specs=[blk,
                  pl.BlockSpec((t, width), lambda p, i: (0, ng + p)),
                  pl.BlockSpec((t, width), lambda p, i: (0, 2 * ng + p)),
                  blk, blk],
        out_specs=[blk, whole, whole],
        out_shape=[jax.ShapeDtypeStruct((t, SB_W), F32)] * 3,
        compiler_params=_params("arbitrary", "arbitrary"),
    )(proj, proj, proj, rtot, dout)


SWA_G = SWA_HEADS // SWA_KV_HEADS


def _swa_heads(first, qs, ks, vs, qg, kg, sinks):
    shape = (SWA_HEADS, WINDOW, 2 * WINDOW)
    qi = lax.broadcasted_iota(jnp.int32, shape, 1)
    kj = lax.broadcasted_iota(jnp.int32, shape, 2)
    dist = qi + WINDOW - kj
    valid = (dist >= 0) & (dist < WINDOW) & (jnp.logical_not(first) | (kj >= WINDOW))
    head = lax.broadcasted_iota(jnp.int32, (SWA_HEADS, 1, 1), 0)
    slope = sum(jnp.where(head == h, 2.0 ** (-8.0 * (h + 1) / SWA_HEADS), 0.0) for h in range(SWA_HEADS))
    kn = _rms(ks, kg)
    per_q_head = lambda x: jnp.concatenate([x[h // SWA_G:h // SWA_G + 1] for h in range(SWA_HEADS)], axis=0)
    k8, v8 = per_q_head(kn), per_q_head(vs)
    s = _bdot_nt(_rms(qs, qg), k8) * (HEAD_DIM ** -0.5)
    s = jnp.where(valid, s - slope * dist.astype(F32), NEG)
    m = lax.stop_gradient(jnp.maximum(jnp.max(s, axis=2, keepdims=True), sinks))
    p = jnp.exp(s - m)
    den = jnp.sum(p, axis=2, keepdims=True) + jnp.exp(sinks - m)
    return _bdot(p / den, v8)


def _swa_split(q, kp, kc, vp, vc, sk):
    lanes = lambda x, n: jnp.stack([x[:, h * HEAD_DIM:(h + 1) * HEAD_DIM] for h in range(n)])
    k2, v2 = jnp.concatenate([kp, kc], axis=0), jnp.concatenate([vp, vc], axis=0)
    sinks = jnp.stack([sk[:, h:h + 1] for h in range(SWA_HEADS)])
    return lanes(q, SWA_HEADS), lanes(k2, SWA_KV_HEADS), lanes(v2, SWA_KV_HEADS), sinks


def _swa_join(x):
    return jnp.concatenate([x[h] for h in range(x.shape[0])], axis=1)


def _swa_specs(t):
    qcb = (3 * SB_W) // SWA_QW
    kcb = (3 * SB_W + SWA_QW) // SWA_KVW
    prev = lambda i: jnp.maximum(i - 1, 0)
    return [pl.BlockSpec((WINDOW, SWA_QW), lambda i: (i, qcb)),
            pl.BlockSpec((WINDOW, SWA_KVW), lambda i: (prev(i), kcb)),
            pl.BlockSpec((WINDOW, SWA_KVW), lambda i: (i, kcb)),
            pl.BlockSpec((WINDOW, SWA_KVW), lambda i: (prev(i), kcb + 1)),
            pl.BlockSpec((WINDOW, SWA_KVW), lambda i: (i, kcb + 1)),
            pl.BlockSpec((1, HEAD_DIM), lambda i: (0, 0)),
            pl.BlockSpec((1, HEAD_DIM), lambda i: (0, 0)),
            pl.BlockSpec((1, SWA_HEADS), lambda i: (0, 0))]


def _swa_fwd(proj, qg, kg, sinks, *, name):
    t = proj.shape[0]

    def body(q_ref, kp_ref, kc_ref, vp_ref, vc_ref, qg_ref, kg_ref, sk_ref, o_ref):
        first = pl.program_id(0) == 0
        qs, ks, vs, sk = _swa_split(q_ref[...], kp_ref[...], kc_ref[...], vp_ref[...], vc_ref[...], sk_ref[...])
        o_ref[...] = _swa_join(_swa_heads(first, qs, ks, vs, qg_ref[...], kg_ref[...], sk))

    return pl.pallas_call(
        body, name=name, grid=(t // WINDOW,), in_specs=_swa_specs(t),
        out_specs=pl.BlockSpec((WINDOW, SWA_QW), lambda i: (i, 0)),
        out_shape=jax.ShapeDtypeStruct((t, SWA_QW), F32),
        compiler_params=_params("parallel"),
    )(proj, proj, proj, proj, proj, qg, kg, sinks)


def _swa_bwd(proj, qg, kg, sinks, dout, *, name):
    t = proj.shape[0]

    def body(q_ref, kp_ref, kc_ref, vp_ref, vc_ref, qg_ref, kg_ref, sk_ref, do_ref,
             dq_ref, dk_ref, dv_ref, dqg_ref, dkg_ref, dsk_ref):
        i = pl.program_id(0)
        first = i == 0

        @pl.when(first)
        def _():
            for r in (dk_ref, dv_ref, dqg_ref, dkg_ref, dsk_ref):
                r[...] = jnp.zeros_like(r)

        qs, ks, vs, sk = _swa_split(q_ref[...], kp_ref[...], kc_ref[...], vp_ref[...], vc_ref[...], sk_ref[...])
        do = do_ref[...]
        cts = jnp.stack([do[:, h * HEAD_DIM:(h + 1) * HEAD_DIM] for h in range(SWA_HEADS)])
        _, vjp = jax.vjp(functools.partial(_swa_heads, first), qs, ks, vs, qg_ref[...], kg_ref[...], sk)
        dqs, dks, dvs, dqg, dkg, dsk = vjp(cts)
        dq_ref[...] = _swa_join(dqs)
        dk2, dv2 = _swa_join(dks), _swa_join(dvs)
        cur = pl.ds(pl.multiple_of(i * WINDOW, WINDOW), WINDOW)
        prv = pl.ds(pl.multiple_of(jnp.maximum(i - 1, 0) * WINDOW, WINDOW), WINDOW)
        dk_ref[prv, :] += dk2[:WINDOW]
        dv_ref[prv, :] += dv2[:WINDOW]
        dk_ref[cur, :] += dk2[WINDOW:]
        dv_ref[cur, :] += dv2[WINDOW:]
        dqg_ref[...] += dqg
        dkg_ref[...] += dkg
        dsk_ref[...] += _swa_join(dsk)

    whole = lambda shape: pl.BlockSpec(shape, lambda i: (0, 0))
    return pl.pallas_call(
        body, name=name, grid=(t // WINDOW,),
        in_specs=_swa_specs(t) + [pl.BlockSpec((WINDOW, SWA_QW), lambda i: (i, 0))],
        out_specs=[pl.BlockSpec((WINDOW, SWA_QW), lambda i: (i, 0)), whole((t, SWA_KVW)), whole((t, SWA_KVW)),
                   whole((1, HEAD_DIM)), whole((1, HEAD_DIM)), whole((1, SWA_HEADS))],
        out_shape=[jax.ShapeDtypeStruct((t, SWA_QW), F32), jax.ShapeDtypeStruct((t, SWA_KVW), F32),
                   jax.ShapeDtypeStruct((t, SWA_KVW), F32), jax.ShapeDtypeStruct((1, HEAD_DIM), F32),
                   jax.ShapeDtypeStruct((1, HEAD_DIM), F32), jax.ShapeDtypeStruct((1, SWA_HEADS), F32)],
        compiler_params=_params("arbitrary"),
    )(proj, proj, proj, proj, proj, qg, kg, sinks, dout)


CONV_CB = 512
CONV_TM = 512
HALO = 8


def _conv_pre(x_ref, h_ref, w_ref, i):
    halo = jnp.where(i > 0, h_ref[...], 0.0)
    xe = jnp.concatenate([halo, x_ref[...]], axis=0)
    tm = x_ref.shape[0]
    w = w_ref[...]
    c = sum(w[k:k + 1, :] * xe[HALO - (GDN_CONV - 1) + k:HALO - (GDN_CONV - 1) + k + tm] for k in range(GDN_CONV))
    return c, xe


def _conv_specs(tm, cb):
    return [pl.BlockSpec((tm, cb), lambda c, i: (i, c)),
            pl.BlockSpec((HALO, cb), lambda c, i: (jnp.maximum(i * (tm // HALO) - 1, 0), c)),
            pl.BlockSpec((GDN_CONV, cb), lambda c, i: (0, c))]


def _conv_fwd(x, w, dact=None, *, name):
    t, ch = x.shape
    tm, cb = _tile(t, CONV_TM), _tile(ch, CONV_CB)

    def body(*refs):
        x_ref, h_ref, w_ref = refs[:3]
        c, _ = _conv_pre(x_ref, h_ref, w_ref, pl.program_id(1))
        sig = jax.nn.sigmoid(c)
        if dact is None:
            refs[3][...] = c * sig
        else:
            refs[4][...] = refs[3][...] * (sig * (1.0 + c * (1.0 - sig)))

    tile = pl.BlockSpec((tm, cb), lambda c, i: (i, c))
    extra = () if dact is None else (dact,)
    return pl.pallas_call(
        body, name=name, grid=(ch // cb, t // tm),
        in_specs=_conv_specs(tm, cb) + [tile] * len(extra), out_specs=tile,
        out_shape=jax.ShapeDtypeStruct((t, ch), F32),
        compiler_params=_params("parallel", "parallel"),
    )(x, x, w, *extra)


def _conv_bwd(x, w, dc, *, name):
    t, ch = x.shape
    tm, cb = _tile(t, CONV_TM), _tile(ch, CONV_CB)
    nt = t // tm

    def body(x_ref, h_ref, w_ref, dc_ref, nh_ref, dx_ref, dw_ref):
        i = pl.program_id(1)

        @pl.when(i == 0)
        def _():
            dw_ref[...] = jnp.zeros_like(dw_ref)

        halo = jnp.where(i > 0, h_ref[...], 0.0)
        xe = jnp.concatenate([halo, x_ref[...]], axis=0)
        dc = dc_ref[...]
        dce = jnp.concatenate([dc, jnp.where(i < nt - 1, nh_ref[...], 0.0)], axis=0)
        w = w_ref[...]
        last = GDN_CONV - 1
        dx_ref[...] = sum(w[k:k + 1, :] * dce[last - k:last - k + tm] for k in range(GDN_CONV))
        dw_ref[...] += jnp.concatenate(
            [jnp.sum(dc * xe[HALO - last + k:HALO - last + k + tm], axis=0, keepdims=True) for k in range(GDN_CONV)],
            axis=0)

    tile = pl.BlockSpec((tm, cb), lambda c, i: (i, c))
    nxt = pl.BlockSpec((HALO, cb), lambda c, i: (jnp.minimum((i + 1) * (tm // HALO), t // HALO - 1), c))
    return pl.pallas_call(
        body, name=name, grid=(ch // cb, nt),
        in_specs=_conv_specs(tm, cb) + [tile, nxt],
        out_specs=[tile, pl.BlockSpec((GDN_CONV, cb), lambda c, i: (0, c))],
        out_shape=[jax.ShapeDtypeStruct((t, ch), F32), jax.ShapeDtypeStruct((GDN_CONV, ch), F32)],
        compiler_params=_params("parallel", "arbitrary"),
    )(x, x, w, dc, dc)


def _gdn_chunk(qraw, kraw, v, bl, a, alog, dtb, state):
    c, d = GDN_CHUNK, GDN_HEAD_DIM
    nh = qraw.shape[0]
    ri = lax.broadcasted_iota(jnp.int32, (nh, c, c), 1)
    ci = lax.broadcasted_iota(jnp.int32, (nh, c, c), 2)
    incl, strict = ri >= ci, ri > ci
    q = qraw * lax.rsqrt(jnp.sum(qraw * qraw, axis=-1, keepdims=True) + EPS) * (d ** -0.5)
    k = kraw * lax.rsqrt(jnp.sum(kraw * kraw, axis=-1, keepdims=True) + EPS)
    beta = jax.nn.sigmoid(bl)
    g = -jnp.exp(alog) * jax.nn.softplus(a + dtb)
    gc = _ldot(incl.astype(F32), jnp.broadcast_to(g, (nh, c, d)))
    gcm = gc[:, :, :c]
    decay = jnp.exp(jnp.where(incl, gcm - jnp.swapaxes(gcm, 1, 2), NEG))
    eg = jnp.exp(gc)
    kbeta = k * beta
    x = -jnp.where(strict, _bdot_nt(kbeta, k) * decay, 0.0)
    tinv = _unit_lower_inverse(x)
    u = _hdot(tinv, v * beta)
    w = _hdot(tinv, kbeta * eg)
    attn = jnp.where(incl, _bdot_nt(q, k) * decay, 0.0)
    glast = gc[:, c - 1:c, :]
    v_new = u - _bdot(w, state)
    o = _bdot(q * eg, state) + _bdot(attn, v_new)
    state = state * jnp.exp(glast) + _bdot_tn(k * jnp.exp(glast - gc), v_new)
    return o, state


GDN_REP = GDN_V_HEADS // GDN_K_HEADS
GDN_HB = 8


def _gdn_pick(vals, kh, r):
    ba, alog, dtb = vals
    lane = lax.broadcasted_iota(jnp.int32, ba.shape, 1)
    hv = kh * GDN_REP + r
    bl = jnp.sum(jnp.where(lane == hv, ba, 0.0), axis=1, keepdims=True)
    a = jnp.sum(jnp.where(lane == GDN_V_HEADS + hv, ba, 0.0), axis=1, keepdims=True)
    lane1 = lax.broadcasted_iota(jnp.int32, alog.shape, 1)
    al = jnp.sum(jnp.where(lane1 == hv, alog, 0.0), axis=1, keepdims=True)
    db = jnp.sum(jnp.where(lane1 == hv, dtb, 0.0), axis=1, keepdims=True)
    return bl, a, al, db


def _gdn_stack(qs, ks, vs, small, j):
    d = GDN_HEAD_DIM
    per = [[], [], [], [], [], [], []]
    for hh in range(GDN_HB):
        q, k = qs[:, hh * d:(hh + 1) * d], ks[:, hh * d:(hh + 1) * d]
        for r in range(GDN_REP):
            col = (hh * GDN_REP + r) * d
            for lst, val in zip(per, (q, k, vs[:, col:col + d]) + _gdn_pick(small, j * GDN_HB + hh, r)):
                lst.append(val)
    return tuple(jnp.stack(lst) for lst in per)


def _gdn_specs(nchunk, rev):
    c, d = GDN_CHUNK, GDN_HEAD_DIM
    at = (lambda n: nchunk - 1 - n) if rev else (lambda n: n)
    ng = GDN_K_HEADS // GDN_HB
    return at, [pl.BlockSpec((c, GDN_HB * d), lambda n, j: (at(n), j)),
                pl.BlockSpec((c, GDN_HB * d), lambda n, j: (at(n), ng + j)),
                pl.BlockSpec((c, GDN_HB * GDN_REP * d), lambda n, j: (at(n), ng + j)),
                pl.BlockSpec((c, 2 * GDN_V_HEADS), lambda n, j: (at(n), 0)),
                pl.BlockSpec((1, GDN_V_HEADS), lambda n, j: (0, 0)),
                pl.BlockSpec((1, GDN_V_HEADS), lambda n, j: (0, 0))]


def _gdn_fwd(act, ba, alog, dtb, *, name):
    t = act.shape[0]
    c, d = GDN_CHUNK, GDN_HEAD_DIM
    nchunk = t // c
    at, specs = _gdn_specs(nchunk, False)

    def body(q_ref, k_ref, v_ref, ba_ref, al_ref, db_ref, o_ref, s_ref, state):
        n, j = pl.program_id(0), pl.program_id(1)
        heads = pl.ds(j * GDN_HB, GDN_HB)

        @pl.when(n == 0)
        def _():
            state[heads] = jnp.zeros((GDN_HB, GDN_REP, d, d), F32)

        s_in = state[heads]
        s_ref[...] = s_in
        args = _gdn_stack(q_ref[...], k_ref[...], v_ref[...], (ba_ref[...], al_ref[...], db_ref[...]), j)
        o, s_new = _gdn_chunk(*args, s_in.reshape(GDN_HB * GDN_REP, d, d))
        o_ref[...] = jnp.concatenate([o[b] for b in range(GDN_HB * GDN_REP)], axis=1)
        state[heads] = s_new.reshape(GDN_HB, GDN_REP, d, d)

    return pl.pallas_call(
        body, name=name, grid=(nchunk, GDN_K_HEADS // GDN_HB), in_specs=specs,
        out_specs=[pl.BlockSpec((c, GDN_HB * GDN_REP * d), lambda n, j: (n, j)),
                   pl.BlockSpec((None, GDN_HB, GDN_REP, d, d), lambda n, j: (n, j, 0, 0, 0))],
        out_shape=[jax.ShapeDtypeStruct((t, GDN_VW), F32),
                   jax.ShapeDtypeStruct((nchunk, GDN_K_HEADS, GDN_REP, d, d), F32)],
        scratch_shapes=[pltpu.VMEM((GDN_K_HEADS, GDN_REP, d, d), F32)],
        compiler_params=_params("arbitrary", "arbitrary"),
    )(act, act, act, ba, alog, dtb)


def _gdn_bwd(act, ba, alog, dtb, states, dout, *, name):
    t = act.shape[0]
    c, d = GDN_CHUNK, GDN_HEAD_DIM
    nchunk = t // c
    at, specs = _gdn_specs(nchunk, True)

    def body(q_ref, k_ref, v_ref, ba_ref, al_ref, db_ref, s_ref, do_ref,
             dq_ref, dk_ref, dv_ref, dba_ref, dal_ref, ddb_ref, dstate):
        n, j = pl.program_id(0), pl.program_id(1)

        @pl.when(n == 0)
        def _():
            dstate[pl.ds(j * GDN_HB, GDN_HB)] = jnp.zeros((GDN_HB, GDN_REP, d, d), F32)

        @pl.when((n == 0) & (j == 0))
        def _():
            dal_ref[...] = jnp.zeros_like(dal_ref)
            ddb_ref[...] = jnp.zeros_like(ddb_ref)

        @pl.when(j == 0)
        def _():
            dba_ref[...] = jnp.zeros_like(dba_ref)

        heads = pl.ds(j * GDN_HB, GDN_HB)
        nh = GDN_HB * GDN_REP
        args = _gdn_stack(q_ref[...], k_ref[...], v_ref[...], (ba_ref[...], al_ref[...], db_ref[...]), j)
        _, vjp = jax.vjp(_gdn_chunk, *args, s_ref[...].reshape(nh, d, d))
        do = do_ref[...]
        do = jnp.stack([do[:, b * d:(b + 1) * d] for b in range(nh)])
        gq, gk, gv, gbl, ga, gal, gdb, gs = vjp((do, dstate[heads].reshape(nh, d, d)))
        dstate[heads] = gs.reshape(GDN_HB, GDN_REP, d, d)
        dq_ref[...] = jnp.concatenate([gq[GDN_REP * hh] + gq[GDN_REP * hh + 1] for hh in range(GDN_HB)], axis=1)
        dk_ref[...] = jnp.concatenate([gk[GDN_REP * hh] + gk[GDN_REP * hh + 1] for hh in range(GDN_HB)], axis=1)
        dv_ref[...] = jnp.concatenate([gv[b] for b in range(nh)], axis=1)
        lane = lax.broadcasted_iota(jnp.int32, (c, 2 * GDN_V_HEADS), 1)
        lane1 = lax.broadcasted_iota(jnp.int32, (1, GDN_V_HEADS), 1)
        dba = jnp.zeros((c, 2 * GDN_V_HEADS), F32)
        dal = jnp.zeros((1, GDN_V_HEADS), F32)
        ddb = jnp.zeros((1, GDN_V_HEADS), F32)
        for b in range(nh):
            hv = j * nh + b
            dba = dba + jnp.where(lane == hv, gbl[b], 0.0) + jnp.where(lane == GDN_V_HEADS + hv, ga[b], 0.0)
            dal = dal + jnp.where(lane1 == hv, gal[b], 0.0)
            ddb = ddb + jnp.where(lane1 == hv, gdb[b], 0.0)
        dba_ref[...] += dba
        dal_ref[...] += dal
        ddb_ref[...] += ddb

    small = pl.BlockSpec((1, GDN_V_HEADS), lambda n, j: (0, 0))
    return pl.pallas_call(
        body, name=name, grid=(nchunk, GDN_K_HEADS // GDN_HB),
        in_specs=specs + [pl.BlockSpec((None, GDN_HB, GDN_REP, d, d), lambda n, j: (at(n), j, 0, 0, 0)),
                          pl.BlockSpec((c, GDN_HB * GDN_REP * d), lambda n, j: (at(n), j))],
        out_specs=[pl.BlockSpec((c, GDN_HB * d), lambda n, j: (at(n), j)),
                   pl.BlockSpec((c, GDN_HB * d), lambda n, j: (at(n), j)),
                   pl.BlockSpec((c, GDN_HB * GDN_REP * d), lambda n, j: (at(n), j)),
                   pl.BlockSpec((c, 2 * GDN_V_HEADS), lambda n, j: (at(n), 0)),
                   small, small],
        out_shape=[jax.ShapeDtypeStruct((t, GDN_KW), F32), jax.ShapeDtypeStruct((t, GDN_KW), F32),
                   jax.ShapeDtypeStruct((t, GDN_VW), F32), jax.ShapeDtypeStruct((t, 2 * GDN_V_HEADS), F32),
                   jax.ShapeDtypeStruct((1, GDN_V_HEADS), F32), jax.ShapeDtypeStruct((1, GDN_V_HEADS), F32)],
        scratch_shapes=[pltpu.VMEM((GDN_K_HEADS, GDN_REP, d, d), F32)],
        compiler_params=_params("arbitrary", "arbitrary"),
    )(act, act, act, ba, alog, dtb, states, dout)


N_DEV = 8
ANY = pl.BlockSpec(memory_space=pl.ANY)


def _coords():
    return lax.axis_index("x"), lax.axis_index("y"), lax.axis_index("c")


def _other_chips(x, y):
    return [(1 - x, y), (x, 1 - y), (1 - x, 1 - y)]


def _remote(src, dst, send_sems, recv_sems, k, to):
    return pltpu.make_async_remote_copy(src_ref=src, dst_ref=dst, send_sem=send_sems.at[k], recv_sem=recv_sems.at[k],
                                        device_id=to, device_id_type=MESH)


def _dma_sems(n):
    return [pltpu.SemaphoreType.DMA((n,)), pltpu.SemaphoreType.DMA((n,))]


def _gather_quarters(parts, *, name):
    na = len(parts)

    def body(*refs):
        ins, outs = refs[:na], refs[na:2 * na]
        send_sems, recv_sems, local_sems = refs[2 * na:]
        x, y, c = _coords()
        sibling = (x, y, 1 - c)
        chips = _other_chips(x, y)
        mine, first, passed = [], [], []
        for a, (x_ref, out_ref) in enumerate(zip(ins, outs)):
            mine.append(pltpu.make_async_copy(x_ref, out_ref.at[2 * x + y], local_sems.at[a]))
            for j, (cx, cy) in enumerate(chips):
                first.append(_remote(x_ref.at[c], out_ref.at[2 * x + y, c], send_sems, recv_sems, 6 * a + j, (cx, cy, c)))
                passed.append(_remote(out_ref.at[2 * cx + cy, c], out_ref.at[2 * cx + cy, c], send_sems, recv_sems,
                                      6 * a + 3 + j, sibling))
        for cp in mine + first:
            cp.start()
        for a, (x_ref, out_ref) in enumerate(zip(ins, outs)):
            for j, (cx, cy) in enumerate(chips):
                _remote(x_ref.at[c], out_ref.at[2 * cx + cy, c], send_sems, recv_sems, 6 * a + j, (cx, cy, c)).wait_recv()
                passed[3 * a + j].start()
        for a, (x_ref, out_ref) in enumerate(zip(ins, outs)):
            for j, (cx, cy) in enumerate(chips):
                _remote(x_ref.at[c], out_ref.at[2 * cx + cy, 1 - c], send_sems, recv_sems, 6 * a + 3 + j,
                        sibling).wait_recv()
        for cp in first + passed:
            cp.wait_send()
        for cp in mine:
            cp.wait()

    return pl.pallas_call(
        body, name=name, in_specs=[ANY] * na, out_specs=[ANY] * na,
        out_shape=[jax.ShapeDtypeStruct((N_CHIPS,) + p.shape, p.dtype) for p in parts],
        scratch_shapes=_dma_sems(6 * na) + [pltpu.SemaphoreType.DMA((na,))],
    )(*parts)


def _swap_halves(grads, *, name):
    na = len(grads)

    def body(*refs):
        ins, outs = refs[:na], refs[na:2 * na]
        send_sems, recv_sems = refs[2 * na:]
        x, y, c = _coords()
        sends = [_remote(g_ref.at[j, 1 - c], o_ref.at[j], send_sems, recv_sems, N_CHIPS * a + j, (x, y, 1 - c))
                 for a, (g_ref, o_ref) in enumerate(zip(ins, outs)) for j in range(N_CHIPS)]
        for cp in sends:
            cp.start()
        for cp in sends:
            cp.wait()

    return pl.pallas_call(
        body, name=name, in_specs=[ANY] * na, out_specs=[ANY] * na,
        out_shape=[jax.ShapeDtypeStruct((N_CHIPS,) + g.shape[2:], g.dtype) for g in grads],
        scratch_shapes=_dma_sems(N_CHIPS * na),
    )(*grads)


def _scatter_quarters(pairs, *, name):
    na = len(pairs)

    def body(*refs):
        ins, outs = refs[:na], refs[na:4 * na]
        send_sems, recv_sems = refs[4 * na:]
        x, y, c = _coords()
        sends = [_remote(p_ref.at[2 * cx + cy], outs[3 * a + j], send_sems, recv_sems, 3 * a + j, (cx, cy, c))
                 for a, p_ref in enumerate(ins) for j, (cx, cy) in enumerate(_other_chips(x, y))]
        for cp in sends:
            cp.start()
        for cp in sends:
            cp.wait()

    out = pl.pallas_call(
        body, name=name, in_specs=[ANY] * na, out_specs=[ANY] * (3 * na),
        out_shape=[jax.ShapeDtypeStruct(p.shape[1:], p.dtype) for p in pairs for _ in range(3)],
        scratch_shapes=_dma_sems(3 * na),
    )(*pairs)
    return [out[3 * a:3 * a + 3] for a in range(na)]


def _share_halves(tots, *, name):
    na = len(tots)

    def body(*refs):
        ins, outs = refs[:na], refs[na:2 * na]
        send_sems, recv_sems = refs[2 * na:]
        x, y, c = _coords()
        sends = [_remote(t_ref, o_ref, send_sems, recv_sems, a, (x, y, 1 - c))
                 for a, (t_ref, o_ref) in enumerate(zip(ins, outs))]
        for cp in sends:
            cp.start()
        for cp in sends:
            cp.wait()

    return pl.pallas_call(
        body, name=name, in_specs=[ANY] * na, out_specs=[ANY] * na,
        out_shape=[jax.ShapeDtypeStruct(t.shape, t.dtype) for t in tots],
        scratch_shapes=_dma_sems(na),
    )(*tots)


def _gather_all(vec, *, name):
    m, w = vec.shape

    def body(x_ref, out_ref, send_sems, recv_sems, local_sem):
        x, y, c = _coords()
        me, sibling = (x, y, c), (x, y, 1 - c)
        chips = _other_chips(x, y)

        def rows(px, py, pc):
            return out_ref.at[pl.ds((4 * px + 2 * py + pc) * m, m), :]

        def copy(k, block, to, src=None):
            return _remote(rows(*block) if src is None else src, rows(*block), send_sems, recv_sems, k, to)

        mine = pltpu.make_async_copy(x_ref, rows(*me), local_sem)
        mine.start()
        first = [copy(0, me, sibling, src=x_ref)]
        first += [copy(1 + j, me, (*chip, c), src=x_ref) for j, chip in enumerate(chips)]
        for cp in first:
            cp.start()
        passed = [copy(4 + j, (*chip, c), sibling) for j, chip in enumerate(chips)]
        for j, chip in enumerate(chips):
            copy(1 + j, (*chip, c), me).wait_recv()
            passed[j].start()
        copy(0, sibling, me).wait_recv()
        for j, chip in enumerate(chips):
            copy(4 + j, (*chip, 1 - c), me).wait_recv()
        for cp in first + passed:
            cp.wait_send()
        mine.wait()

    vm = pl.BlockSpec(memory_space=pltpu.VMEM)
    return pl.pallas_call(
        body, name=name, in_specs=[vm], out_specs=vm, out_shape=jax.ShapeDtypeStruct((N_DEV * m, w), vec.dtype),
        scratch_shapes=_dma_sems(7) + [pltpu.SemaphoreType.DMA(())],
    )(vec)


def _sum_blocks(allv, n, *, name):
    m = allv.shape[0] // n

    def body(a_ref, o_ref):
        acc = a_ref[0:m, :]
        for d in range(1, n):
            acc = acc + a_ref[d * m:(d + 1) * m, :]
        o_ref[...] = acc

    return pl.pallas_call(body, name=name, out_shape=jax.ShapeDtypeStruct((m, allv.shape[1]), allv.dtype))(allv)


EW_BLOCK_BYTES = 1 << 20


def _ew_rows(rows, w):
    return _tile(rows, max(8, (EW_BLOCK_BYTES // (4 * w)) // 8 * 8), 8)


def _add_pair(g, got, c, *, name):
    _, _, rows, w = g.shape
    tr = _ew_rows(rows, w)

    def body(c_ref, g_ref, got_ref, o_ref):
        o_ref[...] = (g_ref[...] + got_ref[...]).astype(o_ref.dtype)

    blk = pl.BlockSpec((None, tr, w), lambda q, i, c_ref: (q, i, 0))
    return pl.pallas_call(
        body, name=name,
        grid_spec=pltpu.PrefetchScalarGridSpec(
            num_scalar_prefetch=1, grid=(N_CHIPS, rows // tr),
            in_specs=[pl.BlockSpec((None, None, tr, w), lambda q, i, c_ref: (q, c_ref[0], i, 0)), blk], out_specs=blk),
        out_shape=jax.ShapeDtypeStruct(got.shape, BF16),
        compiler_params=_params("parallel", "parallel"),
    )(c, g, got)


def _add_chips(pair, recv, chip, *, name):
    _, rows, w = pair.shape
    tr = _ew_rows(rows, w)

    def body(chip_ref, p_ref, r0_ref, r1_ref, r2_ref, o_ref):
        f = lambda r: r[...].astype(F32)
        o_ref[...] = ((f(p_ref) + f(r0_ref)) + f(r1_ref)) + f(r2_ref)

    blk = pl.BlockSpec((tr, w), lambda i, chip_ref: (i, 0))
    return pl.pallas_call(
        body, name=name,
        grid_spec=pltpu.PrefetchScalarGridSpec(
            num_scalar_prefetch=1, grid=(rows // tr,),
            in_specs=[pl.BlockSpec((None, tr, w), lambda i, chip_ref: (chip_ref[0], i, 0)), blk, blk, blk], out_specs=blk),
        out_shape=jax.ShapeDtypeStruct((rows, w), F32),
        compiler_params=_params("parallel"),
    )(chip, pair, *recv)


def _adamw_math(w, g, m, v):
    nm = ADAM_B1 * m + (1.0 - ADAM_B1) * g
    nv = ADAM_B2 * v + (1.0 - ADAM_B2) * (g * g)
    m_hat = nm / (1.0 - ADAM_B1 ** ADAM_STEP)
    v_hat = nv / (1.0 - ADAM_B2 ** ADAM_STEP)
    return -ADAM_LR * (m_hat / (jnp.sqrt(v_hat) + ADAM_EPS) + ADAM_WD * w), nm, nv


def _adamw(w, g, m, v, *, name):
    shape = w.shape
    last = shape[-1]
    w2, g2, m2, v2 = (a.reshape(-1, last) for a in (w, g, m, v))
    rows = w2.shape[0]
    tm = _ew_rows(rows, last)

    def body(w_ref, g_ref, m_ref, v_ref, d_ref, nm_ref, nv_ref):
        d_ref[...], nm_ref[...], nv_ref[...] = _adamw_math(w_ref[...], g_ref[...], m_ref[...], v_ref[...])

    spec = pl.BlockSpec((tm, last), lambda i: (i, 0))
    out = jax.ShapeDtypeStruct((rows, last), F32)
    d, nm, nv = pl.pallas_call(
        body, name=name, grid=(rows // tm,), in_specs=[spec] * 4, out_specs=[spec] * 3, out_shape=[out] * 3,
        compiler_params=_params("parallel"),
    )(w2, g2, m2, v2)
    return d.reshape(shape), nm.reshape(shape), nv.reshape(shape)


def _adamw_halves(w, m, v, mine, theirs, c, *, name):
    _, rows, wd = w.shape
    tr = _ew_rows(rows, wd)

    def body(c_ref, w_ref, m_ref, v_ref, a_ref, b_ref, g_ref, d_ref, nm_ref, nv_ref):
        g = jnp.where(pl.program_id(0) == c_ref[0], a_ref[...], b_ref[...])
        g_ref[...] = g
        d_ref[...], nm_ref[...], nv_ref[...] = _adamw_math(w_ref[...], g, m_ref[...], v_ref[...])

    full = pl.BlockSpec((None, tr, wd), lambda hf, i, c_ref: (hf, i, 0))
    half = pl.BlockSpec((tr, wd), lambda hf, i, c_ref: (i, 0))
    out = jax.ShapeDtypeStruct(w.shape, F32)
    return pl.pallas_call(
        body, name=name,
        grid_spec=pltpu.PrefetchScalarGridSpec(num_scalar_prefetch=1, grid=(2, rows // tr),
                                               in_specs=[full] * 3 + [half] * 2, out_specs=[full] * 4),
        out_shape=[out] * 4,
        compiler_params=_params("parallel", "parallel"),
    )(c, w, m, v, mine, theirs)


def _join_quarters(q, *, name):
    _, rows, n = q.shape
    tr = _tile(rows, 256, 16)

    def body(q_ref, o_ref):
        o_ref[...] = jnp.concatenate([q_ref[s] for s in range(N_CHIPS)], axis=1)

    return pl.pallas_call(
        body, name=name, grid=(rows // tr,),
        in_specs=[pl.BlockSpec((N_CHIPS, tr, n), lambda i: (0, i, 0))],
        out_specs=pl.BlockSpec((tr, N_CHIPS * n), lambda i: (i, 0)),
        out_shape=jax.ShapeDtypeStruct((rows, N_CHIPS * n), q.dtype),
        compiler_params=_params("parallel"),
    )(q)


def _split_quarters(full, *, name):
    rows, n4 = full.shape
    n = n4 // N_CHIPS
    tr = _tile(rows, 256, 16)

    def body(x_ref, o_ref):
        x = x_ref[...]
        for s in range(N_CHIPS):
            o_ref[s] = x[:, s * n:(s + 1) * n]

    return pl.pallas_call(
        body, name=name, grid=(rows // tr,),
        in_specs=[pl.BlockSpec((tr, n4), lambda i: (i, 0))],
        out_specs=pl.BlockSpec((N_CHIPS, tr, n), lambda i: (0, i, 0)),
        out_shape=jax.ShapeDtypeStruct((N_CHIPS, rows, n), full.dtype),
        compiler_params=_params("parallel"),
    )(full)


_WEIGHTS = ['ffn_norm', 'ffn_w_gate', 'ffn_w_up', 'ffn_w_down', 'mix_norm', 'att_w_in', 'att_q_norm', 'att_k_norm',
            'att_sinks', 'att_w_out', 'gdn_w_in', 'gdn_conv_w', 'gdn_a_log', 'gdn_dt_bias', 'gdn_out_norm', 'gdn_w_out',
            'ple_norm', 'ple_w_gate', 'ple_w_proj']
_BIG = ['ffn_w_gate', 'ffn_w_up', 'ffn_w_down', 'att_w_in', 'att_w_out', 'gdn_w_in', 'gdn_w_out', 'ple_w_gate',
        'ple_w_proj']
_SMALL_CUT = {'ffn_norm': 2, 'gdn_conv_w': 2}
_WHOLE = ['mix_norm', 'att_q_norm', 'att_k_norm', 'att_sinks', 'gdn_a_log', 'gdn_dt_bias', 'gdn_out_norm', 'ple_norm']
PACK_W = 1024
SMALL_ROW_MULT = 8


def _halves(a):
    return a.reshape(2, -1, a.shape[-1])


def _from_quarters(blk, axis):
    full = jnp.moveaxis(blk, 0, axis)
    shp = list(full.shape)
    shp[axis:axis + 2] = [shp[axis] * shp[axis + 1]]
    return full.reshape(shp)


def _to_quarters(full, axis):
    shp = list(full.shape)
    shp[axis:axis + 1] = [N_CHIPS, shp[axis] // N_CHIPS]
    return jnp.moveaxis(full.reshape(shp), axis, 0)


def _pack(parts, row_mult):
    flat = jnp.concatenate(parts, axis=-1)
    n = flat.shape[-1]
    rows = -(-n // (PACK_W * row_mult)) * row_mult
    return jnp.pad(flat, [(0, rows * PACK_W - n)]).reshape(rows, PACK_W)


def _unpack(flat, shapes):
    lead = flat.shape[:-2]
    flat = flat.reshape(lead + (-1,))
    out, off = [], 0
    for shp in shapes:
        n = math.prod(shp)
        out.append(flat[..., off:off + n].reshape(lead + tuple(shp)))
        off += n
    return out


FFN_TM = 1024


def _ffn_up(hn, wg, wu, at, *, name):
    t, d = hn.shape
    fq = wg.shape[-1]
    tm = _tile(t, FFN_TM)

    def body(h_ref, wg_ref, wu_ref, g_ref, u_ref, a_ref):
        h = h_ref[...]
        g, u = _dg(h, _b(wg_ref[...]), 1, 0), _dg(h, _b(wu_ref[...]), 1, 0)
        g_ref[...] = g.astype(BF16)
        u_ref[...] = u.astype(BF16)
        a_ref[...] = _f_swiglu(g, u)[0].astype(BF16)

    w_spec = pl.BlockSpec((None,) * (1 + len(at)) + (d, fq), lambda s, i: (s,) + at + (0, 0))
    o_spec = pl.BlockSpec((None, tm, fq), lambda s, i: (s, i, 0))
    out = jax.ShapeDtypeStruct((N_CHIPS, t, fq), BF16)
    return pl.pallas_call(
        body, name=name, grid=(N_CHIPS, t // tm),
        in_specs=[pl.BlockSpec((tm, d), lambda s, i: (i, 0)), w_spec, w_spec], out_specs=[o_spec] * 3,
        out_shape=[out] * 3, compiler_params=_params("parallel", "parallel"),
    )(hn, wg, wu)


def _ffn_d_up(dout, wd, g, u, at, *, name):
    t, d = dout.shape
    fq = wd.shape[-2]
    tm = _tile(t, FFN_TM)

    def body(do_ref, wd_ref, g_ref, u_ref, dg_ref, du_ref):
        da = _dg(_b(do_ref[...]), _b(wd_ref[...]), 1, 1) * 0.5
        _, vjp = jax.vjp(_f_swiglu, g_ref[...].astype(F32), u_ref[...].astype(F32))
        dg, du = vjp((da,))
        dg_ref[...] = dg.astype(BF16)
        du_ref[...] = du.astype(BF16)

    w_spec = pl.BlockSpec((None,) * (1 + len(at)) + (fq, d), lambda s, i: (s,) + at + (0, 0))
    o_spec = pl.BlockSpec((None, tm, fq), lambda s, i: (s, i, 0))
    out = jax.ShapeDtypeStruct((N_CHIPS, t, fq), BF16)
    return pl.pallas_call(
        body, name=name, grid=(N_CHIPS, t // tm),
        in_specs=[pl.BlockSpec((tm, d), lambda s, i: (i, 0)), w_spec, o_spec, o_spec], out_specs=[o_spec] * 2,
        out_shape=[out] * 2, compiler_params=_params("parallel", "parallel"),
    )(dout, wd, g, u)


def _ffn_fwd(h, gain, wg, wu, wd, at, tag):
    lead = (Q,) + at
    hn, = _row_fwd(_f_rms, [h], [gain], [(D_MODEL, BF16)], name=f"{tag}_norm")
    g, u, a = _ffn_up(hn, wg, wu, at, name=f"{tag}_up")
    out = _mm((a, (Q,)), (wd, lead), res=h, scale=0.5, name=f"{tag}_down")
    return out, (h, hn, g, u, a)


def _ffn_bwd(dout, saved, gain, wg, wu, wd, at, grads, tag):
    h, hn, g, u, a = saved
    lead = (Q,) + at
    dg, du = _ffn_d_up(dout, wd, g, u, at, name=f"{tag}_d_up")
    g_gate, g_up, g_down = grads
    g_down = _mm((a, (Q,)), dout, ta=True, scale=0.5, into=(g_down, lead), name=f"{tag}_dw_down")
    g_gate = _mm(hn, (dg, (Q,)), ta=True, into=(g_gate, lead), name=f"{tag}_dw_gate")
    g_up = _mm(hn, (du, (Q,)), ta=True, into=(g_up, lead), name=f"{tag}_dw_up")
    dhn = _mm((dg, (Q,)), (wg, lead), tb=True, name=f"{tag}_d_norm_gate")
    dhn = _mm((du, (Q,)), (wu, lead), tb=True, res=dhn, name=f"{tag}_d_norm_up")
    dh, dgain = _row_bwd(_f_rms_res, [h], [gain], [dhn, dout], [(0, F32)], [0], name=f"{tag}_d_in")
    return dh, dgain, (g_gate, g_up, g_down)


def _att_fwd(h, gain, w_in, qg, kg, sinks, w_out):
    hn, = _row_fwd(_f_rms, [h], [gain], [(D_MODEL, BF16)], name="att_norm")
    proj = _mm(hn, w_in, name="att_in")
    a, rtot = _sb_fwd(proj, name="att_sb")
    b = _swa_fwd(proj, qg, kg, sinks, name="att_swa")
    out = _mm(a, (w_out, (0,)), res=h, name="att_out_sb")
    out = _mm(b, (w_out, (1,)), res=out, name="att_out_swa")
    return out, (h, hn, proj, a, rtot, b)


def _att_bwd(dout, saved, gain, w_in, qg, kg, sinks, w_out):
    h, hn, proj, a, rtot, b = saved
    da = _mm(dout, (w_out, (0,)), tb=True, name="att_d_sb")
    db = _mm(dout, (w_out, (1,)), tb=True, name="att_d_swa")
    dw_out = lax.empty(w_out.shape, F32)
    dw_out = _mm(a, dout, ta=True, into=(dw_out, (0,)), name="att_dw_out_sb")
    dw_out = _mm(b, dout, ta=True, into=(dw_out, (1,)), name="att_dw_out_swa")
    dq, dk, dv = _sb_bwd(proj, rtot, da, name="att_sb_bwd")
    dqb, dkb, dvb, dqg, dkg, dsk = _swa_bwd(proj, qg, kg, sinks, db, name="att_swa_bwd")
    dproj = jnp.concatenate([dq, dk, dv, dqb, dkb, dvb], axis=1)
    dw_in = _mm(hn, dproj, ta=True, name="att_dw_in")
    dhn = _mm(dproj, w_in, tb=True, name="att_d_norm")
    dh, dgain = _row_bwd(_f_rms_res, [h], [gain], [dhn, dout], [(0, F32)], [0], name="att_d_in")
    return dh, dgain, dw_in, dqg, dkg, dsk, dw_out


def _gdn_layer_fwd(h, gain, w_in, conv_w, alog, dtb, out_gain, w_out):
    w_qkv, w_z, w_ba = w_in[:, :GDN_CONV_W], w_in[:, GDN_CONV_W:GDN_CONV_W + GDN_VW], w_in[:, GDN_CONV_W + GDN_VW:]
    hn, = _row_fwd(_f_rms, [h], [gain], [(D_MODEL, BF16)], name="gdn_norm")
    pq = _mm(hn, w_qkv, name="gdn_in_qkv")
    pz = _mm(hn, w_z, name="gdn_in_z")
    ba = _mm(hn, w_ba, name="gdn_in_ba")
    act = _conv_fwd(pq, conv_w, name="gdn_conv")
    o, states = _gdn_fwd(act, ba, alog, dtb, name="gdn_rule")
    y, = _row_fwd(_f_gdn_out, [o, pz], [out_gain], [(GDN_VW, BF16)], name="gdn_gate")
    out = _mm(y, w_out, res=h, name="gdn_out")
    return out, (h, hn, pq, pz, ba, act, o, states, y, (w_qkv, w_z, w_ba))


def _gdn_layer_bwd(dout, saved, gain, conv_w, alog, dtb, out_gain, w_out):
    h, hn, pq, pz, ba, act, o, states, y, (w_qkv, w_z, w_ba) = saved
    dy = _mm(dout, w_out, tb=True, name="gdn_d_gate")
    dw_out = _mm(y, dout, ta=True, name="gdn_dw_out")
    do, dpz, dout_gain = _row_bwd(_f_gdn_out, [o, pz], [out_gain], [dy], [(0, F32), (1, F32)], [0], name="gdn_gate_bwd")
    dq, dk, dv, dba, dal, ddb = _gdn_bwd(act, ba, alog, dtb, states, do, name="gdn_rule_bwd")
    dact = jnp.concatenate([dq, dk, dv], axis=1)
    dc = _conv_fwd(pq, conv_w, dact, name="gdn_conv_d_pre")
    dpq, dconv = _conv_bwd(pq, conv_w, dc, name="gdn_conv_bwd")
    dw_in = jnp.concatenate([_mm(hn, dpq, ta=True, name="gdn_dw_qkv"), _mm(hn, dpz, ta=True, name="gdn_dw_z"),
                             _mm(hn, dba, ta=True, name="gdn_dw_ba")], axis=1)
    dhn = _mm(dpq, w_qkv, tb=True, name="gdn_d_norm_qkv")
    dhn = _mm(dpz, w_z, tb=True, res=dhn, name="gdn_d_norm_z")
    dhn = _mm(dba, w_ba, tb=True, res=dhn, name="gdn_d_norm_ba")
    dh, dgain = _row_bwd(_f_rms_res, [h], [gain], [dhn, dout], [(0, F32)], [0], name="gdn_d_in")
    return dh, dgain, dw_in, dconv, dal, ddb, dout_gain, dw_out


def _ple_fwd(h, gain, w_gate, w_proj, pe, tag):
    hn, = _row_fwd(_f_rms, [h], [gain], [(D_MODEL, BF16)], name=f"{tag}_norm")
    gl = _mm(hn, w_gate, name=f"{tag}_gate")
    pp = _mm(pe, w_proj, name=f"{tag}_proj")
    out, = _row_fwd(_f_ple, [h, gl, pp], [], [(D_MODEL, F32)], name=f"{tag}_mix")
    return out, (h, hn, gl, pp)


def _ple_bwd(dout, saved, gain, w_gate, pe, tag):
    h, hn, gl, pp = saved
    dha, dgl, dpp = _row_bwd(_f_ple, [h, gl, pp], [], [dout], [(0, F32), (1, BF16), (2, BF16)], [], name=f"{tag}_mix_bwd")
    dw_gate = _mm(hn, dgl, ta=True, name=f"{tag}_dw_gate")
    dw_proj = _mm(pe, dpp, ta=True, name=f"{tag}_dw_proj")
    dhn = _mm(dgl, w_gate, tb=True, name=f"{tag}_d_norm")
    dh, dgain = _row_bwd(_f_rms_res, [h], [gain], [dhn, dha], [(0, F32)], [0], name=f"{tag}_d_in")
    return dh, dgain, dw_gate, dw_proj


def kernel(x, p, ffn_norm, ffn_w_gate, ffn_w_up, ffn_w_down, mix_norm, att_w_in, att_q_norm, att_k_norm, att_sinks, att_w_out, gdn_w_in, gdn_conv_w, gdn_a_log, gdn_dt_bias, gdn_out_norm, gdn_w_out, ple_norm, ple_w_gate, ple_w_proj, loss_target, m_ffn_norm, m_ffn_w_gate, m_ffn_w_up, m_ffn_w_down, m_mix_norm, m_att_w_in, m_att_q_norm, m_att_k_norm, m_att_sinks, m_att_w_out, m_gdn_w_in, m_gdn_conv_w, m_gdn_a_log, m_gdn_dt_bias, m_gdn_out_norm, m_gdn_w_out, m_ple_norm, m_ple_w_gate, m_ple_w_proj, v_ffn_norm, v_ffn_w_gate, v_ffn_w_up, v_ffn_w_down, v_mix_norm, v_att_w_in, v_att_q_norm, v_att_k_norm, v_att_sinks, v_att_w_out, v_gdn_w_in, v_gdn_conv_w, v_gdn_a_log, v_gdn_dt_bias, v_gdn_out_norm, v_gdn_w_out, v_ple_norm, v_ple_w_gate, v_ple_w_proj):
    arg = dict(locals())
    cx, cy, cc = _coords()
    chip = (2 * cx + cy).astype(jnp.int32).reshape(1)
    core = cc.astype(jnp.int32).reshape(1)
    n_layers = ffn_norm.shape[0]

    gathered = _gather_quarters([_halves(arg[n].astype(BF16)) for n in _BIG], name="gather_weights")
    wq = {n: g.reshape((N_CHIPS,) + arg[n].shape) for n, g in zip(_BIG, gathered)}
    wt = {}
    wt['att_w_in'] = _join_quarters(wq['att_w_in'][:, 0], name="att_w_in_join")
    wt['gdn_w_in'] = _join_quarters(wq['gdn_w_in'][:, 0], name="gdn_w_in_join")
    wt['att_w_out'] = wq['att_w_out'].reshape(2, SB_W, D_MODEL)
    wt['gdn_w_out'] = wq['gdn_w_out'].reshape(GDN_VW, D_MODEL)
    wt['ple_w_gate'] = _from_quarters(wq['ple_w_gate'], 1)
    wt['ple_w_proj'] = _from_quarters(wq['ple_w_proj'], 2)

    small_names = list(_SMALL_CUT)
    small_shapes = [arg[n].shape for n in small_names]
    svec = _pack([arg[n].reshape(-1) for n in small_names], SMALL_ROW_MULT)
    srows = svec.shape[0]
    sall = _gather_all(svec, name="gather_gains").reshape(N_CHIPS, 2, srows, PACK_W)[:, 0]
    for n, q in zip(small_names, _unpack(sall, small_shapes)):
        wt[n] = _from_quarters(q, _SMALL_CUT[n])
    row = lambda v: v.reshape(1, -1)

    h = x[0]
    tape = []
    ffn_w = (wq['ffn_w_gate'], wq['ffn_w_up'], wq['ffn_w_down'])
    for i in range(n_layers):
        j = i // 2
        h, s0 = _ffn_fwd(h, row(wt['ffn_norm'][i, 0]), *ffn_w, (i, 0), f"ffn{i}a")
        if i % 2 == 0:
            h, sm = _att_fwd(h, row(mix_norm[i]), wt['att_w_in'], att_q_norm[j:j + 1], att_k_norm[j:j + 1],
                             att_sinks[j:j + 1], wt['att_w_out'])
        else:
            h, sm = _gdn_layer_fwd(h, row(mix_norm[i]), wt['gdn_w_in'], wt['gdn_conv_w'][j], gdn_a_log[j:j + 1],
                                   gdn_dt_bias[j:j + 1], gdn_out_norm[j:j + 1], wt['gdn_w_out'])
        h, s1 = _ffn_fwd(h, row(wt['ffn_norm'][i, 1]), *ffn_w, (i, 1), f"ffn{i}b")
        h, sp = _ple_fwd(h, row(ple_norm[i]), wt['ple_w_gate'][i], wt['ple_w_proj'][i], p[i, 0], f"ple{i}")
        tape.append((s0, sm, s1, sp))

    dh, loss_local = _loss_head(h, loss_target[0], name="loss_head")
    loss = lax.psum(loss_local, ("x", "y", "c"))

    gr = {}
    ffn_g = tuple(lax.empty(w.shape, F32) for w in ffn_w)
    d_ffn_norm = [[None, None] for _ in range(n_layers)]
    d_mix, d_ple_norm, d_ple_gate, d_ple_proj = [None] * n_layers, [None] * n_layers, [None] * n_layers, [None] * n_layers
    for i in reversed(range(n_layers)):
        j = i // 2
        s0, sm, s1, sp = tape[i]
        dh, d_ple_norm[i], d_ple_gate[i], d_ple_proj[i] = _ple_bwd(dh, sp, row(ple_norm[i]), wt['ple_w_gate'][i], p[i, 0],
                                                                   f"ple{i}")
        dh, d_ffn_norm[i][1], ffn_g = _ffn_bwd(dh, s1, row(wt['ffn_norm'][i, 1]), *ffn_w, (i, 1), ffn_g, f"ffn{i}b")
        if i % 2 == 0:
            (dh, d_mix[i], dw_in, gr['att_q_norm'], gr['att_k_norm'], gr['att_sinks'],
             dw_out) = _att_bwd(dh, sm, row(mix_norm[i]), wt['att_w_in'], att_q_norm[j:j + 1],
                                att_k_norm[j:j + 1], att_sinks[j:j + 1], wt['att_w_out'])
            gr['att_w_in'] = _split_quarters(dw_in, name="att_dw_in_split")
            gr['att_w_out'] = dw_out
        else:
            (dh, d_mix[i], dw_in, dconv, gr['gdn_a_log'], gr['gdn_dt_bias'], gr['gdn_out_norm'],
             dw_out) = _gdn_layer_bwd(dh, sm, row(mix_norm[i]), wt['gdn_conv_w'][j], gdn_a_log[j:j + 1],
                                      gdn_dt_bias[j:j + 1], gdn_out_norm[j:j + 1], wt['gdn_w_out'])
            gr['gdn_w_in'] = _split_quarters(dw_in, name="gdn_dw_in_split")
            gr['gdn_w_out'] = dw_out
            gr['gdn_conv_w'] = dconv[None]
        dh, d_ffn_norm[i][0], ffn_g = _ffn_bwd(dh, s0, row(wt['ffn_norm'][i, 0]), *ffn_w, (i, 0), ffn_g, f"ffn{i}a")
    grad_x = dh[None]

    gr['ffn_w_gate'], gr['ffn_w_up'], gr['ffn_w_down'] = ffn_g
    gr['ple_w_gate'] = _to_quarters(jnp.stack(d_ple_gate), 1)
    gr['ple_w_proj'] = _to_quarters(jnp.stack(d_ple_proj), 2)
    gr['ffn_norm'] = jnp.stack([jnp.stack([d_ffn_norm[i][k][0] for k in range(2)]) for i in range(n_layers)])
    gr['mix_norm'] = jnp.concatenate(d_mix, axis=0)
    gr['ple_norm'] = jnp.concatenate(d_ple_norm, axis=0)

    gq = [gr[n].reshape((N_CHIPS, 2, -1, arg[n].shape[-1])) for n in _BIG]
    got = _swap_halves(gq, name="grad_swap_halves")
    pairs = [_add_pair(g, o, core, name=f"grad_add_pair_{n}") for n, g, o in zip(_BIG, gq, got)]
    recv = _scatter_quarters(pairs, name="grad_scatter")
    tots = [_add_chips(pr, rc, chip, name=f"grad_add_chips_{n}") for n, pr, rc in zip(_BIG, pairs, recv)]
    theirs = _share_halves(tots, name="grad_share")

    whole_shapes = [arg[n].shape for n in _WHOLE]
    cut_full_shapes = [gr[n].shape for n in small_names]
    gvec = _pack([gr[n].reshape(-1) for n in _WHOLE + small_names], SMALL_ROW_MULT)
    gall = _sum_blocks(_gather_all(gvec, name="gather_small_grads"), N_DEV, name="sum_small_grads")
    parts = _unpack(gall, whole_shapes + cut_full_shapes)
    gsum = dict(zip(_WHOLE, parts))
    for n, g in zip(small_names, parts[len(_WHOLE):]):
        gsum[n] = lax.dynamic_index_in_dim(_to_quarters(g, _SMALL_CUT[n]), chip[0], axis=0, keepdims=False)

    delta, new_m, new_v = {}, {}, {}
    for n, mine, other in zip(_BIG, tots, theirs):
        res = _adamw_halves(_halves(arg[n]), _halves(arg["m_" + n]), _halves(arg["v_" + n]), mine, other, core,
                            name=f"adamw_{n}")
        gsum[n], delta[n], new_m[n], new_v[n] = (r.reshape(arg[n].shape) for r in res)
    for n in _WHOLE + small_names:
        delta[n], new_m[n], new_v[n] = _adamw(arg[n], gsum[n], arg["m_" + n], arg["v_" + n], name=f"adamw_{n}")
    return (loss, grad_x, *[gsum[n] for n in _WEIGHTS], *[delta[n] for n in _WEIGHTS],
            *[new_m[n] for n in _WEIGHTS], *[new_v[n] for n in _WEIGHTS])
```

```python
import functools
import math

import jax
import jax.numpy as jnp
from jax import lax
from jax.experimental import pallas as pl
from jax.experimental.pallas import tpu as pltpu

F32 = jnp.float32
BF16 = jnp.bfloat16
MESH = pl.DeviceIdType.MESH

LANES = 128
VMEM_LIMIT_BYTES = 56 * 1024 * 1024

EPS = 1e-6
D_MODEL = 1024
HEAD_DIM = 64
SB_HEADS = 8
SWA_HEADS = 8
SWA_KV_HEADS = 2
WINDOW = 128
GDN_K_HEADS = 8
GDN_V_HEADS = 16
GDN_HEAD_DIM = 128
GDN_CONV = 4
GDN_CHUNK = 64
SB_W = SB_HEADS * HEAD_DIM
SWA_QW = SWA_HEADS * HEAD_DIM
SWA_KVW = SWA_KV_HEADS * HEAD_DIM
GDN_KW = GDN_K_HEADS * GDN_HEAD_DIM
GDN_VW = GDN_V_HEADS * GDN_HEAD_DIM
GDN_CONV_W = 2 * GDN_KW + GDN_VW

ADAM_LR = 0.001
ADAM_B1 = 0.9
ADAM_B2 = 0.999
ADAM_EPS = 1e-08
ADAM_WD = 0.01
ADAM_STEP = 10

NEG = -1e30


def _params(*sem):
    return pltpu.CompilerParams(dimension_semantics=sem or None, vmem_limit_bytes=VMEM_LIMIT_BYTES)


def _tile(n, cap, align=LANES):
    if n <= cap:
        return n
    for t in range(cap - cap % align, 0, -align):
        if n % t == 0:
            return t
    return n


N_CHIPS = 4
MM_VMEM_BUDGET_BYTES = 40 * 1024 * 1024
Q = "q"


def _opnd(x):
    return x if isinstance(x, tuple) else (x, ())


def _mm(a, b, *, name, ta=False, tb=False, out_dtype=F32, res=None, scale=1.0, out_q=False, into=None,
        tm=None, tn=1024, tk=1024):
    (a_arr, a_lead), (b_arr, b_lead) = _opnd(a), _opnd(b)
    (k_a, m) = a_arr.shape[-2:] if ta else a_arr.shape[-2:][::-1]
    (n, k_b) = b_arr.shape[-2:] if tb else b_arr.shape[-2:][::-1]
    if into is not None:
        out_arr, out_lead = into
        out_q, out_dtype = Q in out_lead, out_arr.dtype
    else:
        out_lead = (Q,) if out_q else ()
    red_q = (Q in a_lead or Q in b_lead) and not out_q
    kq = min(k_a, k_b)
    assert (k_a == k_b) or (red_q and max(k_a, k_b) == N_CHIPS * kq), (a_arr.shape, b_arr.shape)
    tn, tk = _tile(n, tn), _tile(kq, tk)
    if tm is None:
        r_item = _opnd(res)[0].dtype.itemsize if res is not None else 0
        per_row = 2 * (tk * a_arr.dtype.itemsize + tn * (jnp.dtype(out_dtype).itemsize + r_item)) + 4 * tn
        room = MM_VMEM_BUDGET_BYTES - 2 * tk * tn * b_arr.dtype.itemsize
        tm = next(c for c in (4096, 2048, 1024, 512, 256, 128) if c * per_row <= room or c == 128)
    tm = _tile(m, tm)
    nk = kq // tk
    ksteps = nk * (N_CHIPS if red_q else 1)
    dims = (((0 if ta else 1,), (1 if tb else 0,)), ((), ()))
    has_res = res is not None

    def body(*refs):
        a_ref, b_ref = refs[0], refs[1]
        o_ref, acc_ref = refs[-2], refs[-1]
        k = pl.program_id(3)

        @pl.when(k == 0)
        def _():
            acc_ref[...] = jnp.zeros_like(acc_ref)

        acc_ref[...] += lax.dot_general(a_ref[...].astype(BF16), b_ref[...].astype(BF16), dims,
                                        preferred_element_type=F32)

        @pl.when(k == ksteps - 1)
        def _():
            r = acc_ref[...]
            if scale != 1.0:
                r = r * scale
            if has_res:
                r = r + refs[2][...].astype(F32)
            o_ref[...] = r.astype(o_ref.dtype)

    def spec(lead, blk, pos):
        def index(s, i, j, k):
            kk = k % nk if (red_q and Q in lead) else k
            quarter = s if out_q else k // nk
            return tuple(quarter if l == Q else l for l in lead) + pos(i, j, kk)
        return pl.BlockSpec((None,) * len(lead) + blk, index)

    a_spec = spec(a_lead, (tk, tm), lambda i, j, k: (k, i)) if ta else spec(a_lead, (tm, tk), lambda i, j, k: (i, k))
    b_spec = spec(b_lead, (tn, tk), lambda i, j, k: (j, k)) if tb else spec(b_lead, (tk, tn), lambda i, j, k: (k, j))
    o_spec = spec(out_lead, (tm, tn), lambda i, j, k: (i, j))
    in_specs, args = [a_spec, b_spec], [a_arr, b_arr]
    if has_res:
        r_arr, r_lead = _opnd(res)
        in_specs.append(spec(r_lead, (tm, tn), lambda i, j, k: (i, j)))
        args.append(r_arr)
    aliases = {}
    if into is not None:
        in_specs.append(pl.BlockSpec(memory_space=pl.ANY))
        args.append(out_arr)
        aliases = {len(args) - 1: 0}
        out_shape = jax.ShapeDtypeStruct(out_arr.shape, out_arr.dtype)
    else:
        out_shape = jax.ShapeDtypeStruct(((N_CHIPS,) if out_q else ()) + (m, n), out_dtype)
    return pl.pallas_call(
        body, name=name, grid=(N_CHIPS if out_q else 1, m // tm, n // tn, ksteps), in_specs=in_specs, out_specs=o_spec,
        out_shape=out_shape, scratch_shapes=[pltpu.VMEM((tm, tn), F32)], input_output_aliases=aliases,
        compiler_params=_params("parallel", "parallel", "parallel", "arbitrary"),
    )(*args)


def _row_spec(r, tm):
    if isinstance(r, tuple):
        arr, width, cb = r
        return arr, pl.BlockSpec((tm, width), lambda i, cb=cb: (i, cb))
    return r, pl.BlockSpec((tm, r.shape[1]), lambda i: (i, 0))


def _const_spec(c):
    return pl.BlockSpec(c.shape, lambda i: (0,) * c.ndim)


def _row_fwd(fn, rows, consts, outs, *, name, tm=256):
    tm = _tile(_row_spec(rows[0], tm)[0].shape[0], tm, 8)
    arrs, specs = zip(*[_row_spec(r, tm) for r in rows])
    t = arrs[0].shape[0]
    nr, nc = len(rows), len(consts)

    def body(*refs):
        vals = [r[...].astype(F32) for r in refs[:nr + nc]]
        res = fn(*vals)
        for o_ref, v in zip(refs[nr + nc:], res):
            o_ref[...] = v.astype(o_ref.dtype)

    out = pl.pallas_call(
        body, name=name, grid=(t // tm,),
        in_specs=list(specs) + [_const_spec(c) for c in consts],
        out_specs=[pl.BlockSpec((tm, w), lambda i: (i, 0)) for w, _ in outs],
        out_shape=[jax.ShapeDtypeStruct((t, w), dt) for w, dt in outs],
        compiler_params=_params("parallel"),
    )(*arrs, *consts)
    return list(out)


def _row_bwd(fn, rows, consts, cts, row_grads, const_grads, *, name, tm=256):
    tm = _tile(_row_spec(rows[0], tm)[0].shape[0], tm, 8)
    arrs, specs = zip(*[_row_spec(r, tm) for r in rows])
    ct_arrs, ct_specs = zip(*[_row_spec(r, tm) for r in cts])
    t = arrs[0].shape[0]
    nr, nc, nt = len(rows), len(consts), len(cts)
    n_in = nr + nc + nt

    def body(*refs):
        vals = [r[...].astype(F32) for r in refs[:nr + nc]]
        ctv = tuple(r[...].astype(F32) for r in refs[nr + nc:n_in])
        _, vjp = jax.vjp(fn, *vals)
        g = vjp(ctv)
        outs = refs[n_in:]
        for (idx, _), o_ref in zip(row_grads, outs[:len(row_grads)]):
            o_ref[...] = g[idx].astype(o_ref.dtype)
        first = pl.program_id(0) == 0
        for ci, o_ref in zip(const_grads, outs[len(row_grads):]):
            @pl.when(first)
            def _(o_ref=o_ref):
                o_ref[...] = jnp.zeros_like(o_ref)

            o_ref[...] += g[nr + ci]

    widths = [(_row_spec(rows[idx], tm)[1].block_shape[1], dt) for idx, dt in row_grads]
    out = pl.pallas_call(
        body, name=name, grid=(t // tm,),
        in_specs=list(specs) + [_const_spec(c) for c in consts] + list(ct_specs),
        out_specs=[pl.BlockSpec((tm, w), lambda i: (i, 0)) for w, _ in widths]
        + [_const_spec(consts[ci]) for ci in const_grads],
        out_shape=[jax.ShapeDtypeStruct((t, w), dt) for w, dt in widths]
        + [jax.ShapeDtypeStruct(consts[ci].shape, F32) for ci in const_grads],
        compiler_params=_params("arbitrary"),
    )(*arrs, *consts, *ct_arrs)
    return list(out)


def _rms(x, g):
    return x * lax.rsqrt(jnp.mean(x * x, axis=-1, keepdims=True) + EPS) * g


def _f_rms(h, g):
    return (_rms(h, g),)


def _f_rms_res(h, g):
    return (_rms(h, g), h)


def _f_swiglu(g, u):
    return (g * jax.nn.sigmoid(g) * u,)


def _f_ple(h, gl, pp):
    return (h + jax.nn.sigmoid(gl) * pp,)


def _f_gdn_out(o, z, gain):
    outs = []
    for hd in range(GDN_V_HEADS):
        sl = slice(hd * GDN_HEAD_DIM, (hd + 1) * GDN_HEAD_DIM)
        oh, zh = o[:, sl], z[:, sl]
        outs.append(_rms(oh, gain) * (zh * jax.nn.sigmoid(zh)))
    return (jnp.concatenate(outs, axis=1),)


def _loss_head(y, target, *, name, tm=512):
    t, d = y.shape
    tm = _tile(t, tm, 8)

    def body(y_ref, t_ref, dy_ref, l_ref):
        @pl.when(pl.program_id(0) == 0)
        def _():
            l_ref[...] = jnp.zeros_like(l_ref)

        e = y_ref[...] - t_ref[...]
        dy_ref[...] = e * (1.0 / d)
        l_ref[...] += jnp.sum(e * e) * (0.5 / d)

    dy, l = pl.pallas_call(
        body, name=name, grid=(t // tm,),
        in_specs=[pl.BlockSpec((tm, d), lambda i: (i, 0))] * 2,
        out_specs=[pl.BlockSpec((tm, d), lambda i: (i, 0)), pl.BlockSpec((8, LANES), lambda i: (0, 0))],
        out_shape=[jax.ShapeDtypeStruct((t, d), F32), jax.ShapeDtypeStruct((8, LANES), F32)],
        compiler_params=_params("arbitrary"),
    )(y, target)
    return dy, l[0, 0]


def _dg(a, b, ca, cb):
    nb = a.ndim - 2
    batch = tuple(range(nb))
    return lax.dot_general(a, b, (((ca + nb,), (cb + nb,)), (batch, batch)), preferred_element_type=F32)


def _b(x):
    return x.astype(BF16)


@jax.custom_vjp
def _bdot(a, b):
    return _dg(_b(a), _b(b), 1, 0)


def _bdot_fwd(a, b):
    return _bdot(a, b), (a, b)


def _bdot_bwd(r, ct):
    a, b = r
    return _dg(_b(ct), _b(b), 1, 1), _dg(_b(a), _b(ct), 0, 0)


_bdot.defvjp(_bdot_fwd, _bdot_bwd)


@jax.custom_vjp
def _bdot_nt(a, b):
    return _dg(_b(a), _b(b), 1, 1)


def _bdot_nt_fwd(a, b):
    return _bdot_nt(a, b), (a, b)


def _bdot_nt_bwd(r, ct):
    a, b = r
    return _dg(_b(ct), _b(b), 1, 0), _dg(_b(ct), _b(a), 0, 0)


_bdot_nt.defvjp(_bdot_nt_fwd, _bdot_nt_bwd)


@jax.custom_vjp
def _bdot_tn(a, b):
    return _dg(_b(a), _b(b), 0, 0)


def _bdot_tn_fwd(a, b):
    return _bdot_tn(a, b), (a, b)


def _bdot_tn_bwd(r, ct):
    a, b = r
    return _dg(_b(b), _b(ct), 1, 1), _dg(_b(a), _b(ct), 1, 0)


_bdot_tn.defvjp(_bdot_tn_fwd, _bdot_tn_bwd)


def _two(x):
    hi = x.astype(BF16)
    return hi, (x - hi.astype(F32)).astype(BF16)


def _dg3(a, b, ca, cb):
    (ah, al), (bh, bl) = _two(a), _two(b)
    return _dg(ah, bh, ca, cb) + (_dg(ah, bl, ca, cb) + _dg(al, bh, ca, cb))


@jax.custom_vjp
def _hdot(a, b):
    return _dg3(a, b, 1, 0)


def _hdot_fwd(a, b):
    return _hdot(a, b), (a, b)


def _hdot_bwd(r, ct):
    a, b = r
    return _dg3(ct, b, 1, 1), _dg3(a, ct, 0, 0)


_hdot.defvjp(_hdot_fwd, _hdot_bwd)


@jax.custom_vjp
def _unit_lower_inverse(x):
    c = x.shape[-1]
    eye = (lax.broadcasted_iota(jnp.int32, x.shape, 1) == lax.broadcasted_iota(jnp.int32, x.shape, 2)).astype(F32)
    inv, pw = eye + x, x
    for _ in range(int(math.log2(c)) - 1):
        pw = _dg3(pw, pw, 1, 0)
        inv = inv + _dg3(inv, pw, 1, 0)
    return inv


def _unit_lower_inverse_fwd(x):
    inv = _unit_lower_inverse(x)
    return inv, inv


def _unit_lower_inverse_bwd(inv, ct):
    return (_dg3(_dg3(inv, ct, 0, 0), inv, 1, 1),)


_unit_lower_inverse.defvjp(_unit_lower_inverse_fwd, _unit_lower_inverse_bwd)


def _split_dot(x, u):
    hi, lo = _two(x)
    return _dg(hi, u, 1, 0) + _dg(lo, u, 1, 0)


@jax.custom_vjp
def _ldot(l01, x):
    hi, lo = _two(x)
    l01 = l01.astype(BF16)
    return _dg(l01, hi, 1, 0) + _dg(l01, lo, 1, 0)


def _ldot_fwd(l01, x):
    return _ldot(l01, x), l01


def _ldot_bwd(l01, ct):
    hi, lo = _two(ct)
    l01b = l01.astype(BF16)
    return jnp.zeros_like(l01), _dg(l01b, hi, 0, 0) + _dg(l01b, lo, 0, 0)


_ldot.defvjp(_ldot_fwd, _ldot_bwd)


SB_BLK = 128
SB_KEYS = 512
SB_PAIRS = 2
SB_SCALE = HEAD_DIM ** -0.5


def _log_sigmoid(z):
    return jnp.minimum(z, 0.0) - jnp.log(1.0 + jnp.exp(-jnp.abs(z)))


def _sb_consts(t):
    kb = min(SB_KEYS, t)
    nh = 2 * SB_PAIRS
    lane = lax.broadcasted_iota(jnp.int32, (nh, SB_BLK, kb), 2)
    row = lax.broadcasted_iota(jnp.int32, (nh, SB_BLK, kb), 1)
    ur = lax.broadcasted_iota(jnp.int32, (kb, kb), 0)
    uc = lax.broadcasted_iota(jnp.int32, (kb, kb), 1)
    return kb, nh, lane, row, ur, uc


def _sb_heads(x):
    head0 = lax.broadcasted_iota(jnp.int32, (x.shape[0], LANES), 1) < HEAD_DIM
    out = []
    for p in range(SB_PAIRS):
        blk = x[:, p * LANES:(p + 1) * LANES]
        out += [jnp.where(head0, blk, 0.0), jnp.where(head0, 0.0, blk)]
    return jnp.stack(out)


def _sb_pairs(x):
    return jnp.stack([x[:, (h // 2) * LANES:(h // 2 + 1) * LANES] for h in range(2 * SB_PAIRS)])


def _sb_merge(x):
    head0 = lax.broadcasted_iota(jnp.int32, (x.shape[1], LANES), 1) < HEAD_DIM
    return jnp.concatenate([jnp.where(head0, x[2 * p], x[2 * p + 1]) for p in range(SB_PAIRS)], axis=1)


def _sb_rows_dot(x, u):
    nh, rows, k = x.shape
    return _split_dot(x.reshape(nh * rows, k), u).reshape(nh, rows, k)


def _sb_fwd(proj, *, name, gather=()):
    t = proj.shape[0]
    nb = t // SB_BLK
    width = SB_PAIRS * LANES
    ng = SB_W // width
    na = len(gather)

    def body(q_ref, k_ref, v_ref, *rest):
        o_ref, r_ref = rest[na:na + 2]
        i = pl.program_id(1)
        if na:
            step = pl.program_id(0) * nb + i
            copies = lambda **kw: _gather_copies(rest[:na], rest[na + 2:2 * na + 2], *rest[2 * na + 2:], **kw)
            pl.when(step == 0)(lambda: _gather_start(copies(only_first=True)))
        kb, nh, lane, row, ur, uc = _sb_consts(t)
        u_suffix = (ur >= uc).astype(BF16)
        qh = _b(_sb_heads(q_ref[...]) * SB_SCALE)
        diag = (i * SB_BLK) // kb

        def block(j, carry, masked):
            acc, car = carry
            keys = pl.ds(pl.multiple_of(j * kb, kb), kb)
            kj, vj = _b(_sb_pairs(k_ref[keys, :])), _b(_sb_pairs(v_ref[keys, :]))
            z = _dg(qh, kj, 1, 1)
            lk = _log_sigmoid(-z)
            if masked:
                causal = (j * kb + lane) < (i * SB_BLK + row)
                lk = jnp.where(causal, lk, 0.0)
            suf = _sb_rows_dot(lk, u_suffix) + car
            w = jnp.exp(z + suf)
            if masked:
                w = jnp.where(causal, w, 0.0)
            return acc + _dg(_b(w), vj, 1, 0), suf[:, :, 0:1]

        zero = (jnp.zeros((nh, SB_BLK, LANES), F32), jnp.zeros((nh, SB_BLK, 1), F32))
        carry = block(diag, zero, True)
        acc, car = lax.fori_loop(0, diag, lambda s, c: block(diag - 1 - s, c, False), carry)
        o_ref[...] = _sb_merge(acc)
        r_ref[...] = _sb_merge(jnp.broadcast_to(car, (nh, SB_BLK, LANES)))
        if na:
            pl.when(step == ng * nb - 1)(lambda: _gather_finish(copies()))

    return pl.pallas_call(
        body, name=name, grid=(ng, nb),
        in_specs=[pl.BlockSpec((SB_BLK, width), lambda p, i: (i, p)),
                  pl.BlockSpec((t, width), lambda p, i: (0, ng + p)),
                  pl.BlockSpec((t, width), lambda p, i: (0, 2 * ng + p))] + [ANY] * na,
        out_specs=[pl.BlockSpec((SB_BLK, width), lambda p, i: (i, p))] * 2 + [ANY] * na,
        out_shape=[jax.ShapeDtypeStruct((t, SB_W), F32)] * 2
        + [jax.ShapeDtypeStruct((N_CHIPS,) + g.shape, g.dtype) for g in gather],
        scratch_shapes=_gather_scratch(na) if na else [],
        compiler_params=_params("arbitrary", "arbitrary"),
    )(proj, proj, proj, *gather)


def _sb_bwd(proj, rtot, dout, *, name):
    t = proj.shape[0]
    nb = t // SB_BLK
    width = SB_PAIRS * LANES
    ng = SB_W // width

    def body(q_ref, k_ref, v_ref, r_ref, do_ref, dq_ref, dk_ref, dv_ref):
        i = pl.program_id(1)
        kb, nh, lane, row, ur, uc = _sb_consts(t)
        u_incl = (ur <= uc).astype(BF16)
        u_excl = (ur < uc).astype(BF16)
        q, do = q_ref[...], do_ref[...]
        qh, doh = _b(_sb_heads(q) * SB_SCALE), _b(_sb_heads(do))
        qb, dob = _b(_sb_pairs(q) * SB_SCALE), _b(_sb_pairs(do))
        rh = jnp.min(_sb_heads(r_ref[...]), axis=2, keepdims=True)
        diag = (i * SB_BLK) // kb

        @pl.when(i == 0)
        def _():
            dk_ref[...] = jnp.zeros_like(dk_ref)
            dv_ref[...] = jnp.zeros_like(dv_ref)

        def block(j, carry, masked):
            dq_acc, clk, ce = carry
            keys = pl.ds(pl.multiple_of(j * kb, kb), kb)
            kj, vj = _b(_sb_pairs(k_ref[keys, :])), _b(_sb_pairs(v_ref[keys, :]))
            z = _dg(qh, kj, 1, 1)
            lk = _log_sigmoid(-z)
            ls = z + lk
            if masked:
                causal = (j * kb + lane) < (i * SB_BLK + row)
                lk = jnp.where(causal, lk, 0.0)
            pre = _sb_rows_dot(lk, u_incl) + clk
            w = jnp.exp(ls + (rh - pre))
            if masked:
                w = jnp.where(causal, w, 0.0)
            e = _dg(doh, vj, 1, 1) * w
            pre_e = _sb_rows_dot(e, u_excl) + ce
            sig = jnp.exp(ls)
            dz = e - sig * (e + pre_e)
            if masked:
                dz = jnp.where(causal, dz, 0.0)
            dzb = _b(dz)
            dk_ref[keys, :] += _sb_merge(_dg(dzb, qb, 0, 0))
            dv_ref[keys, :] += _sb_merge(_dg(_b(w), dob, 0, 0))
            return dq_acc + _dg(dzb, kj, 1, 0), pre[:, :, kb - 1:], pre_e[:, :, kb - 1:] + e[:, :, kb - 1:]

        zero = (jnp.zeros((nh, SB_BLK, LANES), F32), jnp.zeros((nh, SB_BLK, 1), F32), jnp.zeros((nh, SB_BLK, 1), F32))
        carry = lax.fori_loop(0, diag, lambda j, c: block(j, c, False), zero)
        dq_acc, _, _ = block(diag, carry, True)
        dq_ref[...] = _sb_merge(dq_acc) * SB_SCALE

    blk = pl.BlockSpec((SB_BLK, width), lambda p, i: (i, p))
    whole = pl.BlockSpec((t, width), lambda p, i: (0, p))
    return pl.pallas_call(
        body, name=name, grid=(ng, nb),
        in_specs=[blk,
                  pl.BlockSpec((t, width), lambda p, i: (0, ng + p)),
                  pl.BlockSpec((t, width), lambda p, i: (0, 2 * ng + p)),
                  blk, blk],
        out_specs=[blk, whole, whole],
        out_shape=[jax.ShapeDtypeStruct((t, SB_W), F32)] * 3,
        compiler_params=_params("arbitrary", "arbitrary"),
    )(proj, proj, proj, rtot, dout)


SWA_G = SWA_HEADS // SWA_KV_HEADS


def _swa_heads(first, qs, ks, vs, qg, kg, sinks):
    shape = (SWA_HEADS, WINDOW, 2 * WINDOW)
    qi = lax.broadcasted_iota(jnp.int32, shape, 1)
    kj = lax.broadcasted_iota(jnp.int32, shape, 2)
    dist = qi + WINDOW - kj
    valid = (dist >= 0) & (dist < WINDOW) & (jnp.logical_not(first) | (kj >= WINDOW))
    head = lax.broadcasted_iota(jnp.int32, (SWA_HEADS, 1, 1), 0)
    slope = sum(jnp.where(head == h, 2.0 ** (-8.0 * (h + 1) / SWA_HEADS), 0.0) for h in range(SWA_HEADS))
    kn = _rms(ks, kg)
    per_q_head = lambda x: jnp.concatenate([x[h // SWA_G:h // SWA_G + 1] for h in range(SWA_HEADS)], axis=0)
    k8, v8 = per_q_head(kn), per_q_head(vs)
    s = _bdot_nt(_rms(qs, qg), k8) * (HEAD_DIM ** -0.5)
    s = jnp.where(valid, s - slope * dist.astype(F32), NEG)
    m = lax.stop_gradient(jnp.maximum(jnp.max(s, axis=2, keepdims=True), sinks))
    p = jnp.exp(s - m)
    den = jnp.sum(p, axis=2, keepdims=True) + jnp.exp(sinks - m)
    return _bdot(p / den, v8)


def _swa_split(q, kp, kc, vp, vc, sk):
    lanes = lambda x, n: jnp.stack([x[:, h * HEAD_DIM:(h + 1) * HEAD_DIM] for h in range(n)])
    k2, v2 = jnp.concatenate([kp, kc], axis=0), jnp.concatenate([vp, vc], axis=0)
    sinks = jnp.stack([sk[:, h:h + 1] for h in range(SWA_HEADS)])
    return lanes(q, SWA_HEADS), lanes(k2, SWA_KV_HEADS), lanes(v2, SWA_KV_HEADS), sinks


def _swa_join(x):
    return jnp.concatenate([x[h] for h in range(x.shape[0])], axis=1)


def _swa_specs(t):
    qcb = (3 * SB_W) // SWA_QW
    kcb = (3 * SB_W + SWA_QW) // SWA_KVW
    prev = lambda i: jnp.maximum(i - 1, 0)
    return [pl.BlockSpec((WINDOW, SWA_QW), lambda i: (i, qcb)),
            pl.BlockSpec((WINDOW, SWA_KVW), lambda i: (prev(i), kcb)),
            pl.BlockSpec((WINDOW, SWA_KVW), lambda i: (i, kcb)),
            pl.BlockSpec((WINDOW, SWA_KVW), lambda i: (prev(i), kcb + 1)),
            pl.BlockSpec((WINDOW, SWA_KVW), lambda i: (i, kcb + 1)),
            pl.BlockSpec((1, HEAD_DIM), lambda i: (0, 0)),
            pl.BlockSpec((1, HEAD_DIM), lambda i: (0, 0)),
            pl.BlockSpec((1, SWA_HEADS), lambda i: (0, 0))]


def _swa_fwd(proj, qg, kg, sinks, *, name):
    t = proj.shape[0]

    def body(q_ref, kp_ref, kc_ref, vp_ref, vc_ref, qg_ref, kg_ref, sk_ref, o_ref):
        first = pl.program_id(0) == 0
        qs, ks, vs, sk = _swa_split(q_ref[...], kp_ref[...], kc_ref[...], vp_ref[...], vc_ref[...], sk_ref[...])
        o_ref[...] = _swa_join(_swa_heads(first, qs, ks, vs, qg_ref[...], kg_ref[...], sk))

    return pl.pallas_call(
        body, name=name, grid=(t // WINDOW,), in_specs=_swa_specs(t),
        out_specs=pl.BlockSpec((WINDOW, SWA_QW), lambda i: (i, 0)),
        out_shape=jax.ShapeDtypeStruct((t, SWA_QW), F32),
        compiler_params=_params("parallel"),
    )(proj, proj, proj, proj, proj, qg, kg, sinks)


def _swa_bwd(proj, qg, kg, sinks, dout, *, name):
    t = proj.shape[0]

    def body(q_ref, kp_ref, kc_ref, vp_ref, vc_ref, qg_ref, kg_ref, sk_ref, do_ref,
             dq_ref, dk_ref, dv_ref, dqg_ref, dkg_ref, dsk_ref):
        i = pl.program_id(0)
        first = i == 0

        @pl.when(first)
        def _():
            for r in (dk_ref, dv_ref, dqg_ref, dkg_ref, dsk_ref):
                r[...] = jnp.zeros_like(r)

        qs, ks, vs, sk = _swa_split(q_ref[...], kp_ref[...], kc_ref[...], vp_ref[...], vc_ref[...], sk_ref[...])
        do = do_ref[...]
        cts = jnp.stack([do[:, h * HEAD_DIM:(h + 1) * HEAD_DIM] for h in range(SWA_HEADS)])
        _, vjp = jax.vjp(functools.partial(_swa_heads, first), qs, ks, vs, qg_ref[...], kg_ref[...], sk)
        dqs, dks, dvs, dqg, dkg, dsk = vjp(cts)
        dq_ref[...] = _swa_join(dqs)
        dk2, dv2 = _swa_join(dks), _swa_join(dvs)
        cur = pl.ds(pl.multiple_of(i * WINDOW, WINDOW), WINDOW)
        prv = pl.ds(pl.multiple_of(jnp.maximum(i - 1, 0) * WINDOW, WINDOW), WINDOW)
        dk_ref[prv, :] += dk2[:WINDOW]
        dv_ref[prv, :] += dv2[:WINDOW]
        dk_ref[cur, :] += dk2[WINDOW:]
        dv_ref[cur, :] += dv2[WINDOW:]
        dqg_ref[...] += dqg
        dkg_ref[...] += dkg
        dsk_ref[...] += _swa_join(dsk)

    whole = lambda shape: pl.BlockSpec(shape, lambda i: (0, 0))
    return pl.pallas_call(
        body, name=name, grid=(t // WINDOW,),
        in_specs=_swa_specs(t) + [pl.BlockSpec((WINDOW, SWA_QW), lambda i: (i, 0))],
        out_specs=[pl.BlockSpec((WINDOW, SWA_QW), lambda i: (i, 0)), whole((t, SWA_KVW)), whole((t, SWA_KVW)),
                   whole((1, HEAD_DIM)), whole((1, HEAD_DIM)), whole((1, SWA_HEADS))],
        out_shape=[jax.ShapeDtypeStruct((t, SWA_QW), F32), jax.ShapeDtypeStruct((t, SWA_KVW), F32),
                   jax.ShapeDtypeStruct((t, SWA_KVW), F32), jax.ShapeDtypeStruct((1, HEAD_DIM), F32),
                   jax.ShapeDtypeStruct((1, HEAD_DIM), F32), jax.ShapeDtypeStruct((1, SWA_HEADS), F32)],
        compiler_params=_params("arbitrary"),
    )(proj, proj, proj, proj, proj, qg, kg, sinks, dout)


CONV_CB = 512
CONV_TM = 512
HALO = 8


def _conv_pre(x_ref, h_ref, w_ref, i):
    halo = jnp.where(i > 0, h_ref[...], 0.0)
    xe = jnp.concatenate([halo, x_ref[...]], axis=0)
    tm = x_ref.shape[0]
    w = w_ref[...]
    c = sum(w[k:k + 1, :] * xe[HALO - (GDN_CONV - 1) + k:HALO - (GDN_CONV - 1) + k + tm] for k in range(GDN_CONV))
    return c, xe


def _conv_specs(tm, cb):
    return [pl.BlockSpec((tm, cb), lambda c, i: (i, c)),
            pl.BlockSpec((HALO, cb), lambda c, i: (jnp.maximum(i * (tm // HALO) - 1, 0), c)),
            pl.BlockSpec((GDN_CONV, cb), lambda c, i: (0, c))]


def _conv_fwd(x, w, dact=None, *, name):
    t, ch = x.shape
    tm, cb = _tile(t, CONV_TM), _tile(ch, CONV_CB)

    def body(*refs):
        x_ref, h_ref, w_ref = refs[:3]
        c, _ = _conv_pre(x_ref, h_ref, w_ref, pl.program_id(1))
        sig = jax.nn.sigmoid(c)
        if dact is None:
            refs[3][...] = c * sig
        else:
            refs[4][...] = refs[3][...] * (sig * (1.0 + c * (1.0 - sig)))

    tile = pl.BlockSpec((tm, cb), lambda c, i: (i, c))
    extra = () if dact is None else (dact,)
    return pl.pallas_call(
        body, name=name, grid=(ch // cb, t // tm),
        in_specs=_conv_specs(tm, cb) + [tile] * len(extra), out_specs=tile,
        out_shape=jax.ShapeDtypeStruct((t, ch), F32),
        compiler_params=_params("parallel", "parallel"),
    )(x, x, w, *extra)


def _conv_bwd(x, w, dc, *, name):
    t, ch = x.shape
    tm, cb = _tile(t, CONV_TM), _tile(ch, CONV_CB)
    nt = t // tm

    def body(x_ref, h_ref, w_ref, dc_ref, nh_ref, dx_ref, dw_ref):
        i = pl.program_id(1)

        @pl.when(i == 0)
        def _():
            dw_ref[...] = jnp.zeros_like(dw_ref)

        halo = jnp.where(i > 0, h_ref[...], 0.0)
        xe = jnp.concatenate([halo, x_ref[...]], axis=0)
        dc = dc_ref[...]
        dce = jnp.concatenate([dc, jnp.where(i < nt - 1, nh_ref[...], 0.0)], axis=0)
        w = w_ref[...]
        last = GDN_CONV - 1
        dx_ref[...] = sum(w[k:k + 1, :] * dce[last - k:last - k + tm] for k in range(GDN_CONV))
        dw_ref[...] += jnp.concatenate(
            [jnp.sum(dc * xe[HALO - last + k:HALO - last + k + tm], axis=0, keepdims=True) for k in range(GDN_CONV)],
            axis=0)

    tile = pl.BlockSpec((tm, cb), lambda c, i: (i, c))
    nxt = pl.BlockSpec((HALO, cb), lambda c, i: (jnp.minimum((i + 1) * (tm // HALO), t // HALO - 1), c))
    return pl.pallas_call(
        body, name=name, grid=(ch // cb, nt),
        in_specs=_conv_specs(tm, cb) + [tile, nxt],
        out_specs=[tile, pl.BlockSpec((GDN_CONV, cb), lambda c, i: (0, c))],
        out_shape=[jax.ShapeDtypeStruct((t, ch), F32), jax.ShapeDtypeStruct((GDN_CONV, ch), F32)],
        compiler_params=_params("parallel", "arbitrary"),
    )(x, x, w, dc, dc)


def _gdn_chunk(qraw, kraw, v, bl, a, alog, dtb, state):
    c, d = GDN_CHUNK, GDN_HEAD_DIM
    nh = qraw.shape[0]
    ri = lax.broadcasted_iota(jnp.int32, (nh, c, c), 1)
    ci = lax.broadcasted_iota(jnp.int32, (nh, c, c), 2)
    incl, strict = ri >= ci, ri > ci
    q = qraw * lax.rsqrt(jnp.sum(qraw * qraw, axis=-1, keepdims=True) + EPS) * (d ** -0.5)
    k = kraw * lax.rsqrt(jnp.sum(kraw * kraw, axis=-1, keepdims=True) + EPS)
    beta = jax.nn.sigmoid(bl)
    g = -jnp.exp(alog) * jax.nn.softplus(a + dtb)
    gc = _ldot(incl.astype(F32), jnp.broadcast_to(g, (nh, c, d)))
    gcm = gc[:, :, :c]
    decay = jnp.exp(jnp.where(incl, gcm - jnp.swapaxes(gcm, 1, 2), NEG))
    eg = jnp.exp(gc)
    kbeta = k * beta
    x = -jnp.where(strict, _bdot_nt(kbeta, k) * decay, 0.0)
    tinv = _unit_lower_inverse(x)
    u = _hdot(tinv, v * beta)
    w = _hdot(tinv, kbeta * eg)
    attn = jnp.where(incl, _bdot_nt(q, k) * decay, 0.0)
    glast = gc[:, c - 1:c, :]
    v_new = u - _bdot(w, state)
    o = _bdot(q * eg, state) + _bdot(attn, v_new)
    state = state * jnp.exp(glast) + _bdot_tn(k * jnp.exp(glast - gc), v_new)
    return o, state


GDN_REP = GDN_V_HEADS // GDN_K_HEADS
GDN_HB = 8


def _gdn_pick(vals, kh, r):
    ba, alog, dtb = vals
    lane = lax.broadcasted_iota(jnp.int32, ba.shape, 1)
    hv = kh * GDN_REP + r
    bl = jnp.sum(jnp.where(lane == hv, ba, 0.0), axis=1, keepdims=True)
    a = jnp.sum(jnp.where(lane == GDN_V_HEADS + hv, ba, 0.0), axis=1, keepdims=True)
    lane1 = lax.broadcasted_iota(jnp.int32, alog.shape, 1)
    al = jnp.sum(jnp.where(lane1 == hv, alog, 0.0), axis=1, keepdims=True)
    db = jnp.sum(jnp.where(lane1 == hv, dtb, 0.0), axis=1, keepdims=True)
    return bl, a, al, db


def _gdn_stack(qs, ks, vs, small, j):
    d = GDN_HEAD_DIM
    per = [[], [], [], [], [], [], []]
    for hh in range(GDN_HB):
        q, k = qs[:, hh * d:(hh + 1) * d], ks[:, hh * d:(hh + 1) * d]
        for r in range(GDN_REP):
            col = (hh * GDN_REP + r) * d
            for lst, val in zip(per, (q, k, vs[:, col:col + d]) + _gdn_pick(small, j * GDN_HB + hh, r)):
                lst.append(val)
    return tuple(jnp.stack(lst) for lst in per)


def _gdn_specs(nchunk, rev):
    c, d = GDN_CHUNK, GDN_HEAD_DIM
    at = (lambda n: nchunk - 1 - n) if rev else (lambda n: n)
    ng = GDN_K_HEADS // GDN_HB
    return at, [pl.BlockSpec((c, GDN_HB * d), lambda n, j: (at(n), j)),
                pl.BlockSpec((c, GDN_HB * d), lambda n, j: (at(n), ng + j)),
                pl.BlockSpec((c, GDN_HB * GDN_REP * d), lambda n, j: (at(n), ng + j)),
                pl.BlockSpec((c, 2 * GDN_V_HEADS), lambda n, j: (at(n), 0)),
                pl.BlockSpec((1, GDN_V_HEADS), lambda n, j: (0, 0)),
                pl.BlockSpec((1, GDN_V_HEADS), lambda n, j: (0, 0))]


def _gdn_fwd(act, ba, alog, dtb, *, name):
    t = act.shape[0]
    c, d = GDN_CHUNK, GDN_HEAD_DIM
    nchunk = t // c
    at, specs = _gdn_specs(nchunk, False)

    def body(q_ref, k_ref, v_ref, ba_ref, al_ref, db_ref, o_ref, s_ref, state):
        n, j = pl.program_id(0), pl.program_id(1)
        heads = pl.ds(j * GDN_HB, GDN_HB)

        @pl.when(n == 0)
        def _():
            state[heads] = jnp.zeros((GDN_HB, GDN_REP, d, d), F32)

        s_in = state[heads]
        s_ref[...] = s_in
        args = _gdn_stack(q_ref[...], k_ref[...], v_ref[...], (ba_ref[...], al_ref[...], db_ref[...]), j)
        o, s_new = _gdn_chunk(*args, s_in.reshape(GDN_HB * GDN_REP, d, d))
        o_ref[...] = jnp.concatenate([o[b] for b in range(GDN_HB * GDN_REP)], axis=1)
        state[heads] = s_new.reshape(GDN_HB, GDN_REP, d, d)

    return pl.pallas_call(
        body, name=name, grid=(nchunk, GDN_K_HEADS // GDN_HB), in_specs=specs,
        out_specs=[pl.BlockSpec((c, GDN_HB * GDN_REP * d), lambda n, j: (n, j)),
                   pl.BlockSpec((None, GDN_HB, GDN_REP, d, d), lambda n, j: (n, j, 0, 0, 0))],
        out_shape=[jax.ShapeDtypeStruct((t, GDN_VW), F32),
                   jax.ShapeDtypeStruct((nchunk, GDN_K_HEADS, GDN_REP, d, d), F32)],
        scratch_shapes=[pltpu.VMEM((GDN_K_HEADS, GDN_REP, d, d), F32)],
        compiler_params=_params("arbitrary", "arbitrary"),
    )(act, act, act, ba, alog, dtb)


def _gdn_bwd(act, ba, alog, dtb, states, dout, *, name):
    t = act.shape[0]
    c, d = GDN_CHUNK, GDN_HEAD_DIM
    nchunk = t // c
    at, specs = _gdn_specs(nchunk, True)

    def body(q_ref, k_ref, v_ref, ba_ref, al_ref, db_ref, s_ref, do_ref,
             dq_ref, dk_ref, dv_ref, dba_ref, dal_ref, ddb_ref, dstate):
        n, j = pl.program_id(0), pl.program_id(1)

        @pl.when(n == 0)
        def _():
            dstate[pl.ds(j * GDN_HB, GDN_HB)] = jnp.zeros((GDN_HB, GDN_REP, d, d), F32)

        @pl.when((n == 0) & (j == 0))
        def _():
            dal_ref[...] = jnp.zeros_like(dal_ref)
            ddb_ref[...] = jnp.zeros_like(ddb_ref)

        @pl.when(j == 0)
        def _():
            dba_ref[...] = jnp.zeros_like(dba_ref)

        heads = pl.ds(j * GDN_HB, GDN_HB)
        nh = GDN_HB * GDN_REP
        args = _gdn_stack(q_ref[...], k_ref[...], v_ref[...], (ba_ref[...], al_ref[...], db_ref[...]), j)
        _, vjp = jax.vjp(_gdn_chunk, *args, s_ref[...].reshape(nh, d, d))
        do = do_ref[...]
        do = jnp.stack([do[:, b * d:(b + 1) * d] for b in range(nh)])
        gq, gk, gv, gbl, ga, gal, gdb, gs = vjp((do, dstate[heads].reshape(nh, d, d)))
        dstate[heads] = gs.reshape(GDN_HB, GDN_REP, d, d)
        dq_ref[...] = jnp.concatenate([gq[GDN_REP * hh] + gq[GDN_REP * hh + 1] for hh in range(GDN_HB)], axis=1)
        dk_ref[...] = jnp.concatenate([gk[GDN_REP * hh] + gk[GDN_REP * hh + 1] for hh in range(GDN_HB)], axis=1)
        dv_ref[...] = jnp.concatenate([gv[b] for b in range(nh)], axis=1)
        lane = lax.broadcasted_iota(jnp.int32, (c, 2 * GDN_V_HEADS), 1)
        lane1 = lax.broadcasted_iota(jnp.int32, (1, GDN_V_HEADS), 1)
        dba = jnp.zeros((c, 2 * GDN_V_HEADS), F32)
        dal = jnp.zeros((1, GDN_V_HEADS), F32)
        ddb = jnp.zeros((1, GDN_V_HEADS), F32)
        for b in range(nh):
            hv = j * nh + b
            dba = dba + jnp.where(lane == hv, gbl[b], 0.0) + jnp.where(lane == GDN_V_HEADS + hv, ga[b], 0.0)
            dal = dal + jnp.where(lane1 == hv, gal[b], 0.0)
            ddb = ddb + jnp.where(lane1 == hv, gdb[b], 0.0)
        dba_ref[...] += dba
        dal_ref[...] += dal
        ddb_ref[...] += ddb

    small = pl.BlockSpec((1, GDN_V_HEADS), lambda n, j: (0, 0))
    return pl.pallas_call(
        body, name=name, grid=(nchunk, GDN_K_HEADS // GDN_HB),
        in_specs=specs + [pl.BlockSpec((None, GDN_HB, GDN_REP, d, d), lambda n, j: (at(n), j, 0, 0, 0)),
                          pl.BlockSpec((c, GDN_HB * GDN_REP * d), lambda n, j: (at(n), j))],
        out_specs=[pl.BlockSpec((c, GDN_HB * d), lambda n, j: (at(n), j)),
                   pl.BlockSpec((c, GDN_HB * d), lambda n, j: (at(n), j)),
                   pl.BlockSpec((c, GDN_HB * GDN_REP * d), lambda n, j: (at(n), j)),
                   pl.BlockSpec((c, 2 * GDN_V_HEADS), lambda n, j: (at(n), 0)),
                   small, small],
        out_shape=[jax.ShapeDtypeStruct((t, GDN_KW), F32), jax.ShapeDtypeStruct((t, GDN_KW), F32),
                   jax.ShapeDtypeStruct((t, GDN_VW), F32), jax.ShapeDtypeStruct((t, 2 * GDN_V_HEADS), F32),
                   jax.ShapeDtypeStruct((1, GDN_V_HEADS), F32), jax.ShapeDtypeStruct((1, GDN_V_HEADS), F32)],
        scratch_shapes=[pltpu.VMEM((GDN_K_HEADS, GDN_REP, d, d), F32)],
        compiler_params=_params("arbitrary", "arbitrary"),
    )(act, act, act, ba, alog, dtb, states, dout)


N_DEV = 8
ANY = pl.BlockSpec(memory_space=pl.ANY)


def _coords():
    return lax.axis_index("x"), lax.axis_index("y"), lax.axis_index("c")


def _other_chips(x, y):
    return [(1 - x, y), (x, 1 - y), (1 - x, 1 - y)]


def _remote(src, dst, send_sems, recv_sems, k, to):
    return pltpu.make_async_remote_copy(src_ref=src, dst_ref=dst, send_sem=send_sems.at[k], recv_sem=recv_sems.at[k],
                                        device_id=to, device_id_type=MESH)


def _dma_sems(n):
    return [pltpu.SemaphoreType.DMA((n,)), pltpu.SemaphoreType.DMA((n,))]


def _gather_copies(ins, outs, send_sems, recv_sems, local_sems, only_first=False):
    x, y, c = _coords()
    sibling = (x, y, 1 - c)
    local, sends, arrivals, relays, relayed = [], [], [], [], []
    for a, (x_ref, out_ref) in enumerate(zip(ins, outs)):
        local.append(pltpu.make_async_copy(x_ref, out_ref.at[2 * x + y], local_sems.at[a]))
        for j, (cx, cy) in enumerate(_other_chips(x, y)):
            k, theirs = 6 * a + j, 2 * cx + cy
            sends.append(_remote(x_ref.at[c], out_ref.at[2 * x + y, c], send_sems, recv_sems, k, (cx, cy, c)))
            if only_first:
                continue
            arrivals.append(_remote(x_ref.at[c], out_ref.at[theirs, c], send_sems, recv_sems, k, (cx, cy, c)))
            relays.append(_remote(out_ref.at[theirs, c], out_ref.at[theirs, c], send_sems, recv_sems, k + 3, sibling))
            relayed.append(_remote(x_ref.at[c], out_ref.at[theirs, 1 - c], send_sems, recv_sems, k + 3, sibling))
    return local, sends, arrivals, relays, relayed


def _gather_start(copies):
    local, sends, _, _, _ = copies
    for cp in local + sends:
        cp.start()


def _gather_finish(copies):
    local, sends, arrivals, relays, relayed = copies
    for landed, relay in zip(arrivals, relays):
        landed.wait_recv()
        relay.start()
    for cp in relayed:
        cp.wait_recv()
    for cp in sends + relays:
        cp.wait_send()
    for cp in local:
        cp.wait()


def _gather_scratch(na):
    return _dma_sems(6 * na) + [pltpu.SemaphoreType.DMA((na,))]


def _gather_quarters(parts, *, name):
    na = len(parts)

    def body(*refs):
        copies = _gather_copies(refs[:na], refs[na:2 * na], *refs[2 * na:])
        _gather_start(copies)
        _gather_finish(copies)

    return pl.pallas_call(
        body, name=name, in_specs=[ANY] * na, out_specs=[ANY] * na,
        out_shape=[jax.ShapeDtypeStruct((N_CHIPS,) + p.shape, p.dtype) for p in parts],
        scratch_shapes=_gather_scratch(na),
    )(*parts)


def _swap_halves(grads, *, name):
    na = len(grads)

    def body(*refs):
        ins, outs = refs[:na], refs[na:2 * na]
        send_sems, recv_sems = refs[2 * na:]
        x, y, c = _coords()
        sends = [_remote(g_ref.at[j, 1 - c], o_ref.at[j], send_sems, recv_sems, N_CHIPS * a + j, (x, y, 1 - c))
                 for a, (g_ref, o_ref) in enumerate(zip(ins, outs)) for j in range(N_CHIPS)]
        for cp in sends:
            cp.start()
        for cp in sends:
            cp.wait()

    return pl.pallas_call(
        body, name=name, in_specs=[ANY] * na, out_specs=[ANY] * na,
        out_shape=[jax.ShapeDtypeStruct((N_CHIPS,) + g.shape[2:], g.dtype) for g in grads],
        scratch_shapes=_dma_sems(N_CHIPS * na),
    )(*grads)


def _scatter_quarters(pairs, *, name):
    na = len(pairs)

    def body(*refs):
        ins, outs = refs[:na], refs[na:4 * na]
        send_sems, recv_sems = refs[4 * na:]
        x, y, c = _coords()
        sends = [_remote(p_ref.at[2 * cx + cy], outs[3 * a + j], send_sems, recv_sems, 3 * a + j, (cx, cy, c))
                 for a, p_ref in enumerate(ins) for j, (cx, cy) in enumerate(_other_chips(x, y))]
        for cp in sends:
            cp.start()
        for cp in sends:
            cp.wait()

    out = pl.pallas_call(
        body, name=name, in_specs=[ANY] * na, out_specs=[ANY] * (3 * na),
        out_shape=[jax.ShapeDtypeStruct(p.shape[1:], p.dtype) for p in pairs for _ in range(3)],
        scratch_shapes=_dma_sems(3 * na),
    )(*pairs)
    return [out[3 * a:3 * a + 3] for a in range(na)]


def _share_halves(tots, *, name):
    na = len(tots)

    def body(*refs):
        ins, outs = refs[:na], refs[na:2 * na]
        send_sems, recv_sems = refs[2 * na:]
        x, y, c = _coords()
        sends = [_remote(t_ref, o_ref, send_sems, recv_sems, a, (x, y, 1 - c))
                 for a, (t_ref, o_ref) in enumerate(zip(ins, outs))]
        for cp in sends:
            cp.start()
        for cp in sends:
            cp.wait()

    return pl.pallas_call(
        body, name=name, in_specs=[ANY] * na, out_specs=[ANY] * na,
        out_shape=[jax.ShapeDtypeStruct(t.shape, t.dtype) for t in tots],
        scratch_shapes=_dma_sems(na),
    )(*tots)


def _gather_all(vec, *, name):
    m, w = vec.shape

    def body(x_ref, out_ref, send_sems, recv_sems, local_sem):
        x, y, c = _coords()
        me, sibling = (x, y, c), (x, y, 1 - c)
        chips = _other_chips(x, y)

        def rows(px, py, pc):
            return out_ref.at[pl.ds((4 * px + 2 * py + pc) * m, m), :]

        def copy(k, block, to, src=None):
            return _remote(rows(*block) if src is None else src, rows(*block), send_sems, recv_sems, k, to)

        mine = pltpu.make_async_copy(x_ref, rows(*me), local_sem)
        mine.start()
        first = [copy(0, me, sibling, src=x_ref)]
        first += [copy(1 + j, me, (*chip, c), src=x_ref) for j, chip in enumerate(chips)]
        for cp in first:
            cp.start()
        passed = [copy(4 + j, (*chip, c), sibling) for j, chip in enumerate(chips)]
        for j, chip in enumerate(chips):
            copy(1 + j, (*chip, c), me).wait_recv()
            passed[j].start()
        copy(0, sibling, me).wait_recv()
        for j, chip in enumerate(chips):
            copy(4 + j, (*chip, 1 - c), me).wait_recv()
        for cp in first + passed:
            cp.wait_send()
        mine.wait()

    vm = pl.BlockSpec(memory_space=pltpu.VMEM)
    return pl.pallas_call(
        body, name=name, in_specs=[vm], out_specs=vm, out_shape=jax.ShapeDtypeStruct((N_DEV * m, w), vec.dtype),
        scratch_shapes=_dma_sems(7) + [pltpu.SemaphoreType.DMA(())],
    )(vec)


def _sum_blocks(allv, n, *, name):
    m = allv.shape[0] // n

    def body(a_ref, o_ref):
        acc = a_ref[0:m, :]
        for d in range(1, n):
            acc = acc + a_ref[d * m:(d + 1) * m, :]
        o_ref[...] = acc

    return pl.pallas_call(body, name=name, out_shape=jax.ShapeDtypeStruct((m, allv.shape[1]), allv.dtype))(allv)


EW_BLOCK_BYTES = 1 << 20


def _ew_rows(rows, w):
    return _tile(rows, max(8, (EW_BLOCK_BYTES // (4 * w)) // 8 * 8), 8)


def _add_pair(g, got, c, *, name):
    _, _, rows, w = g.shape
    tr = _ew_rows(rows, w)

    def body(c_ref, g_ref, got_ref, o_ref):
        o_ref[...] = (g_ref[...] + got_ref[...]).astype(o_ref.dtype)

    blk = pl.BlockSpec((None, tr, w), lambda q, i, c_ref: (q, i, 0))
    return pl.pallas_call(
        body, name=name,
        grid_spec=pltpu.PrefetchScalarGridSpec(
            num_scalar_prefetch=1, grid=(N_CHIPS, rows // tr),
            in_specs=[pl.BlockSpec((None, None, tr, w), lambda q, i, c_ref: (q, c_ref[0], i, 0)), blk], out_specs=blk),
        out_shape=jax.ShapeDtypeStruct(got.shape, BF16),
        compiler_params=_params("parallel", "parallel"),
    )(c, g, got)


def _add_chips(pair, recv, chip, *, name):
    _, rows, w = pair.shape
    tr = _ew_rows(rows, w)

    def body(chip_ref, p_ref, r0_ref, r1_ref, r2_ref, o_ref):
        f = lambda r: r[...].astype(F32)
        o_ref[...] = ((f(p_ref) + f(r0_ref)) + f(r1_ref)) + f(r2_ref)

    blk = pl.BlockSpec((tr, w), lambda i, chip_ref: (i, 0))
    return pl.pallas_call(
        body, name=name,
        grid_spec=pltpu.PrefetchScalarGridSpec(
            num_scalar_prefetch=1, grid=(rows // tr,),
            in_specs=[pl.BlockSpec((None, tr, w), lambda i, chip_ref: (chip_ref[0], i, 0)), blk, blk, blk], out_specs=blk),
        out_shape=jax.ShapeDtypeStruct((rows, w), F32),
        compiler_params=_params("parallel"),
    )(chip, pair, *recv)


def _adamw_math(w, g, m, v):
    nm = ADAM_B1 * m + (1.0 - ADAM_B1) * g
    nv = ADAM_B2 * v + (1.0 - ADAM_B2) * (g * g)
    m_hat = nm / (1.0 - ADAM_B1 ** ADAM_STEP)
    v_hat = nv / (1.0 - ADAM_B2 ** ADAM_STEP)
    return -ADAM_LR * (m_hat / (jnp.sqrt(v_hat) + ADAM_EPS) + ADAM_WD * w), nm, nv


def _adamw(w, g, m, v, *, name):
    shape = w.shape
    last = shape[-1]
    w2, g2, m2, v2 = (a.reshape(-1, last) for a in (w, g, m, v))
    rows = w2.shape[0]
    tm = _ew_rows(rows, last)

    def body(w_ref, g_ref, m_ref, v_ref, d_ref, nm_ref, nv_ref):
        d_ref[...], nm_ref[...], nv_ref[...] = _adamw_math(w_ref[...], g_ref[...], m_ref[...], v_ref[...])

    spec = pl.BlockSpec((tm, last), lambda i: (i, 0))
    out = jax.ShapeDtypeStruct((rows, last), F32)
    d, nm, nv = pl.pallas_call(
        body, name=name, grid=(rows // tm,), in_specs=[spec] * 4, out_specs=[spec] * 3, out_shape=[out] * 3,
        compiler_params=_params("parallel"),
    )(w2, g2, m2, v2)
    return d.reshape(shape), nm.reshape(shape), nv.reshape(shape)


def _adamw_halves(w, m, v, mine, theirs, c, *, name):
    _, rows, wd = w.shape
    tr = _ew_rows(rows, wd)

    def body(c_ref, w_ref, m_ref, v_ref, a_ref, b_ref, g_ref, d_ref, nm_ref, nv_ref):
        g = jnp.where(pl.program_id(0) == c_ref[0], a_ref[...], b_ref[...])
        g_ref[...] = g
        d_ref[...], nm_ref[...], nv_ref[...] = _adamw_math(w_ref[...], g, m_ref[...], v_ref[...])

    full = pl.BlockSpec((None, tr, wd), lambda hf, i, c_ref: (hf, i, 0))
    half = pl.BlockSpec((tr, wd), lambda hf, i, c_ref: (i, 0))
    out = jax.ShapeDtypeStruct(w.shape, F32)
    return pl.pallas_call(
        body, name=name,
        grid_spec=pltpu.PrefetchScalarGridSpec(num_scalar_prefetch=1, grid=(2, rows // tr),
                                               in_specs=[full] * 3 + [half] * 2, out_specs=[full] * 4),
        out_shape=[out] * 4,
        compiler_params=_params("parallel", "parallel"),
    )(c, w, m, v, mine, theirs)


def _join_quarters(q, *, name):
    _, rows, n = q.shape
    tr = _tile(rows, 256, 16)

    def body(q_ref, o_ref):
        o_ref[...] = jnp.concatenate([q_ref[s] for s in range(N_CHIPS)], axis=1)

    return pl.pallas_call(
        body, name=name, grid=(rows // tr,),
        in_specs=[pl.BlockSpec((N_CHIPS, tr, n), lambda i: (0, i, 0))],
        out_specs=pl.BlockSpec((tr, N_CHIPS * n), lambda i: (i, 0)),
        out_shape=jax.ShapeDtypeStruct((rows, N_CHIPS * n), q.dtype),
        compiler_params=_params("parallel"),
    )(q)


def _split_quarters(full, *, name):
    rows, n4 = full.shape
    n = n4 // N_CHIPS
    tr = _tile(rows, 256, 16)

    def body(x_ref, o_ref):
        x = x_ref[...]
        for s in range(N_CHIPS):
            o_ref[s] = x[:, s * n:(s + 1) * n]

    return pl.pallas_call(
        body, name=name, grid=(rows // tr,),
        in_specs=[pl.BlockSpec((tr, n4), lambda i: (i, 0))],
        out_specs=pl.BlockSpec((N_CHIPS, tr, n), lambda i: (0, i, 0)),
        out_shape=jax.ShapeDtypeStruct((N_CHIPS, rows, n), full.dtype),
        compiler_params=_params("parallel"),
    )(full)


_WEIGHTS = ['ffn_norm', 'ffn_w_gate', 'ffn_w_up', 'ffn_w_down', 'mix_norm', 'att_w_in', 'att_q_norm', 'att_k_norm',
            'att_sinks', 'att_w_out', 'gdn_w_in', 'gdn_conv_w', 'gdn_a_log', 'gdn_dt_bias', 'gdn_out_norm', 'gdn_w_out',
            'ple_norm', 'ple_w_gate', 'ple_w_proj']
_BIG = ['ffn_w_gate', 'ffn_w_up', 'ffn_w_down', 'att_w_in', 'att_w_out', 'gdn_w_in', 'gdn_w_out', 'ple_w_gate',
        'ple_w_proj']
_SMALL_CUT = {'ffn_norm': 2, 'gdn_conv_w': 2}
_WHOLE = ['mix_norm', 'att_q_norm', 'att_k_norm', 'att_sinks', 'gdn_a_log', 'gdn_dt_bias', 'gdn_out_norm', 'ple_norm']
PACK_W = 1024
SMALL_ROW_MULT = 8


def _halves(a):
    return a.reshape(2, -1, a.shape[-1])


def _from_quarters(blk, axis):
    full = jnp.moveaxis(blk, 0, axis)
    shp = list(full.shape)
    shp[axis:axis + 2] = [shp[axis] * shp[axis + 1]]
    return full.reshape(shp)


def _to_quarters(full, axis):
    shp = list(full.shape)
    shp[axis:axis + 1] = [N_CHIPS, shp[axis] // N_CHIPS]
    return jnp.moveaxis(full.reshape(shp), axis, 0)


def _pack(parts, row_mult):
    flat = jnp.concatenate(parts, axis=-1)
    n = flat.shape[-1]
    rows = -(-n // (PACK_W * row_mult)) * row_mult
    return jnp.pad(flat, [(0, rows * PACK_W - n)]).reshape(rows, PACK_W)


def _unpack(flat, shapes):
    lead = flat.shape[:-2]
    flat = flat.reshape(lead + (-1,))
    out, off = [], 0
    for shp in shapes:
        n = math.prod(shp)
        out.append(flat[..., off:off + n].reshape(lead + tuple(shp)))
        off += n
    return out


FFN_TM = 1024


def _ffn_up(hn, wg, wu, at, *, name):
    t, d = hn.shape
    fq = wg.shape[-1]
    tm = _tile(t, FFN_TM)

    def body(h_ref, wg_ref, wu_ref, g_ref, u_ref, a_ref):
        h = h_ref[...]
        g, u = _dg(h, _b(wg_ref[...]), 1, 0), _dg(h, _b(wu_ref[...]), 1, 0)
        g_ref[...] = g.astype(BF16)
        u_ref[...] = u.astype(BF16)
        a_ref[...] = _f_swiglu(g, u)[0].astype(BF16)

    w_spec = pl.BlockSpec((None,) * (1 + len(at)) + (d, fq), lambda s, i: (s,) + at + (0, 0))
    o_spec = pl.BlockSpec((None, tm, fq), lambda s, i: (s, i, 0))
    out = jax.ShapeDtypeStruct((N_CHIPS, t, fq), BF16)
    return pl.pallas_call(
        body, name=name, grid=(N_CHIPS, t // tm),
        in_specs=[pl.BlockSpec((tm, d), lambda s, i: (i, 0)), w_spec, w_spec], out_specs=[o_spec] * 3,
        out_shape=[out] * 3, compiler_params=_params("parallel", "parallel"),
    )(hn, wg, wu)


def _ffn_d_up(dout, wd, g, u, at, *, name):
    t, d = dout.shape
    fq = wd.shape[-2]
    tm = _tile(t, FFN_TM)

    def body(do_ref, wd_ref, g_ref, u_ref, dg_ref, du_ref):
        da = _dg(_b(do_ref[...]), _b(wd_ref[...]), 1, 1) * 0.5
        _, vjp = jax.vjp(_f_swiglu, g_ref[...].astype(F32), u_ref[...].astype(F32))
        dg, du = vjp((da,))
        dg_ref[...] = dg.astype(BF16)
        du_ref[...] = du.astype(BF16)

    w_spec = pl.BlockSpec((None,) * (1 + len(at)) + (fq, d), lambda s, i: (s,) + at + (0, 0))
    o_spec = pl.BlockSpec((None, tm, fq), lambda s, i: (s, i, 0))
    out = jax.ShapeDtypeStruct((N_CHIPS, t, fq), BF16)
    return pl.pallas_call(
        body, name=name, grid=(N_CHIPS, t // tm),
        in_specs=[pl.BlockSpec((tm, d), lambda s, i: (i, 0)), w_spec, o_spec, o_spec], out_specs=[o_spec] * 2,
        out_shape=[out] * 2, compiler_params=_params("parallel", "parallel"),
    )(dout, wd, g, u)


def _ffn_fwd(h, gain, wg, wu, wd, at, tag):
    lead = (Q,) + at
    hn, = _row_fwd(_f_rms, [h], [gain], [(D_MODEL, BF16)], name=f"{tag}_norm")
    g, u, a = _ffn_up(hn, wg, wu, at, name=f"{tag}_up")
    out = _mm((a, (Q,)), (wd, lead), res=h, scale=0.5, name=f"{tag}_down")
    return out, (h, hn, g, u, a)


def _ffn_bwd(dout, saved, gain, wg, wu, wd, at, grads, g_at, tag):
    h, hn, g, u, a = saved
    lead = (Q,) + at
    g_lead = (Q,) + g_at
    dg, du = _ffn_d_up(dout, wd, g, u, at, name=f"{tag}_d_up")
    g_gate, g_up, g_down = grads
    g_down = _mm((a, (Q,)), dout, ta=True, scale=0.5, into=(g_down, g_lead), name=f"{tag}_dw_down")
    g_gate = _mm(hn, (dg, (Q,)), ta=True, into=(g_gate, g_lead), name=f"{tag}_dw_gate")
    g_up = _mm(hn, (du, (Q,)), ta=True, into=(g_up, g_lead), name=f"{tag}_dw_up")
    dhn = _mm((dg, (Q,)), (wg, lead), tb=True, name=f"{tag}_d_norm_gate")
    dhn = _mm((du, (Q,)), (wu, lead), tb=True, res=dhn, name=f"{tag}_d_norm_up")
    dh, dgain = _row_bwd(_f_rms_res, [h], [gain], [dhn, dout], [(0, F32)], [0], name=f"{tag}_d_in")
    return dh, dgain, (g_gate, g_up, g_down)


def _att_fwd(h, gain, w_in, qg, kg, sinks, w_out, gather):
    hn, = _row_fwd(_f_rms, [h], [gain], [(D_MODEL, BF16)], name="att_norm")
    proj = _mm(hn, w_in, name="att_in")
    a, rtot, *gathered = _sb_fwd(proj, name="att_sb", gather=gather)
    b = _swa_fwd(proj, qg, kg, sinks, name="att_swa")
    out = _mm(a, (w_out, (0,)), res=h, name="att_out_sb")
    out = _mm(b, (w_out, (1,)), res=out, name="att_out_swa")
    return out, (h, hn, proj, a, rtot, b), gathered


def _att_bwd(dout, saved, gain, w_in, qg, kg, sinks, w_out):
    h, hn, proj, a, rtot, b = saved
    da = _mm(dout, (w_out, (0,)), tb=True, name="att_d_sb")
    db = _mm(dout, (w_out, (1,)), tb=True, name="att_d_swa")
    dw_out = lax.empty(w_out.shape, F32)
    dw_out = _mm(a, dout, ta=True, into=(dw_out, (0,)), name="att_dw_out_sb")
    dw_out = _mm(b, dout, ta=True, into=(dw_out, (1,)), name="att_dw_out_swa")
    dq, dk, dv = _sb_bwd(proj, rtot, da, name="att_sb_bwd")
    dqb, dkb, dvb, dqg, dkg, dsk = _swa_bwd(proj, qg, kg, sinks, db, name="att_swa_bwd")
    dproj = jnp.concatenate([dq, dk, dv, dqb, dkb, dvb], axis=1)
    dw_in = _mm(hn, dproj, ta=True, name="att_dw_in")
    dhn = _mm(dproj, w_in, tb=True, name="att_d_norm")
    dh, dgain = _row_bwd(_f_rms_res, [h], [gain], [dhn, dout], [(0, F32)], [0], name="att_d_in")
    return dh, dgain, dw_in, dqg, dkg, dsk, dw_out


def _gdn_layer_fwd(h, gain, w_in, conv_w, alog, dtb, out_gain, w_out):
    w_qkv, w_z, w_ba = w_in[:, :GDN_CONV_W], w_in[:, GDN_CONV_W:GDN_CONV_W + GDN_VW], w_in[:, GDN_CONV_W + GDN_VW:]
    hn, = _row_fwd(_f_rms, [h], [gain], [(D_MODEL, BF16)], name="gdn_norm")
    pq = _mm(hn, w_qkv, name="gdn_in_qkv")
    pz = _mm(hn, w_z, name="gdn_in_z")
    ba = _mm(hn, w_ba, name="gdn_in_ba")
    act = _conv_fwd(pq, conv_w, name="gdn_conv")
    o, states = _gdn_fwd(act, ba, alog, dtb, name="gdn_rule")
    y, = _row_fwd(_f_gdn_out, [o, pz], [out_gain], [(GDN_VW, BF16)], name="gdn_gate")
    out = _mm(y, w_out, res=h, name="gdn_out")
    return out, (h, hn, pq, pz, ba, act, o, states, y, (w_qkv, w_z, w_ba))


def _gdn_layer_bwd(dout, saved, gain, conv_w, alog, dtb, out_gain, w_out):
    h, hn, pq, pz, ba, act, o, states, y, (w_qkv, w_z, w_ba) = saved
    dy = _mm(dout, w_out, tb=True, name="gdn_d_gate")
    dw_out = _mm(y, dout, ta=True, name="gdn_dw_out")
    do, dpz, dout_gain = _row_bwd(_f_gdn_out, [o, pz], [out_gain], [dy], [(0, F32), (1, F32)], [0], name="gdn_gate_bwd")
    dq, dk, dv, dba, dal, ddb = _gdn_bwd(act, ba, alog, dtb, states, do, name="gdn_rule_bwd")
    dact = jnp.concatenate([dq, dk, dv], axis=1)
    dc = _conv_fwd(pq, conv_w, dact, name="gdn_conv_d_pre")
    dpq, dconv = _conv_bwd(pq, conv_w, dc, name="gdn_conv_bwd")
    dw_in = jnp.concatenate([_mm(hn, dpq, ta=True, name="gdn_dw_qkv"), _mm(hn, dpz, ta=True, name="gdn_dw_z"),
                             _mm(hn, dba, ta=True, name="gdn_dw_ba")], axis=1)
    dhn = _mm(dpq, w_qkv, tb=True, name="gdn_d_norm_qkv")
    dhn = _mm(dpz, w_z, tb=True, res=dhn, name="gdn_d_norm_z")
    dhn = _mm(dba, w_ba, tb=True, res=dhn, name="gdn_d_norm_ba")
    dh, dgain = _row_bwd(_f_rms_res, [h], [gain], [dhn, dout], [(0, F32)], [0], name="gdn_d_in")
    return dh, dgain, dw_in, dconv, dal, ddb, dout_gain, dw_out


def _ple_fwd(h, gain, w_gate, w_proj, pe, tag):
    hn, = _row_fwd(_f_rms, [h], [gain], [(D_MODEL, BF16)], name=f"{tag}_norm")
    gl = _mm(hn, w_gate, name=f"{tag}_gate")
    pp = _mm(pe, w_proj, name=f"{tag}_proj")
    out, = _row_fwd(_f_ple, [h, gl, pp], [], [(D_MODEL, F32)], name=f"{tag}_mix")
    return out, (h, hn, gl, pp)


def _ple_bwd(dout, saved, gain, w_gate, pe, tag):
    h, hn, gl, pp = saved
    dha, dgl, dpp = _row_bwd(_f_ple, [h, gl, pp], [], [dout], [(0, F32), (1, BF16), (2, BF16)], [], name=f"{tag}_mix_bwd")
    dw_gate = _mm(hn, dgl, ta=True, name=f"{tag}_dw_gate")
    dw_proj = _mm(pe, dpp, ta=True, name=f"{tag}_dw_proj")
    dhn = _mm(dgl, w_gate, tb=True, name=f"{tag}_d_norm")
    dh, dgain = _row_bwd(_f_rms_res, [h], [gain], [dhn, dha], [(0, F32)], [0], name=f"{tag}_d_in")
    return dh, dgain, dw_gate, dw_proj


def kernel(x, p, ffn_norm, ffn_w_gate, ffn_w_up, ffn_w_down, mix_norm, att_w_in, att_q_norm, att_k_norm, att_sinks, att_w_out, gdn_w_in, gdn_conv_w, gdn_a_log, gdn_dt_bias, gdn_out_norm, gdn_w_out, ple_norm, ple_w_gate, ple_w_proj, loss_target, m_ffn_norm, m_ffn_w_gate, m_ffn_w_up, m_ffn_w_down, m_mix_norm, m_att_w_in, m_att_q_norm, m_att_k_norm, m_att_sinks, m_att_w_out, m_gdn_w_in, m_gdn_conv_w, m_gdn_a_log, m_gdn_dt_bias, m_gdn_out_norm, m_gdn_w_out, m_ple_norm, m_ple_w_gate, m_ple_w_proj, v_ffn_norm, v_ffn_w_gate, v_ffn_w_up, v_ffn_w_down, v_mix_norm, v_att_w_in, v_att_q_norm, v_att_k_norm, v_att_sinks, v_att_w_out, v_gdn_w_in, v_gdn_conv_w, v_gdn_a_log, v_gdn_dt_bias, v_gdn_out_norm, v_gdn_w_out, v_ple_norm, v_ple_w_gate, v_ple_w_proj):
    arg = dict(locals())
    cx, cy, cc = _coords()
    chip = (2 * cx + cy).astype(jnp.int32).reshape(1)
    core = cc.astype(jnp.int32).reshape(1)
    n_layers = ffn_norm.shape[0]

    quarter = lambda n, i=None: _halves((arg[n] if i is None else arg[n][i]).astype(BF16))
    ffn_names = ('ffn_w_gate', 'ffn_w_up', 'ffn_w_down')
    early = [quarter(n, 0) for n in ffn_names] + [quarter('att_w_in'), quarter('att_w_out')]
    late_names = ('gdn_w_in', 'gdn_w_out', 'ple_w_gate', 'ple_w_proj')
    late = [quarter(n, 1) for n in ffn_names] + [quarter(n) for n in late_names]
    *ffn_w0, att_in_q, att_out_q = _gather_quarters(early, name="gather_weights")
    wt = {'att_w_in': _join_quarters(att_in_q.reshape((N_CHIPS,) + att_w_in.shape[1:]), name="att_w_in_join"),
          'att_w_out': att_out_q.reshape(2, SB_W, D_MODEL)}

    small_names = list(_SMALL_CUT)
    small_shapes = [arg[n].shape for n in small_names]
    svec = _pack([arg[n].reshape(-1) for n in small_names], SMALL_ROW_MULT)
    srows = svec.shape[0]
    sall = _gather_all(svec, name="gather_gains").reshape(N_CHIPS, 2, srows, PACK_W)[:, 0]
    for n, q in zip(small_names, _unpack(sall, small_shapes)):
        wt[n] = _from_quarters(q, _SMALL_CUT[n])
    row = lambda v: v.reshape(1, -1)

    as_ffn = lambda g, n: g.reshape((N_CHIPS,) + arg[n].shape[1:])
    ffn_w = [tuple(as_ffn(g, n) for g, n in zip(ffn_w0, ffn_names)), None]
    h = x[0]
    tape = []
    for i in range(n_layers):
        j = i // 2
        h, s0 = _ffn_fwd(h, row(wt['ffn_norm'][i, 0]), *ffn_w[i], (0,), f"ffn{i}a")
        if i % 2 == 0:
            h, sm, gathered = _att_fwd(h, row(mix_norm[i]), wt['att_w_in'], att_q_norm[j:j + 1], att_k_norm[j:j + 1],
                                       att_sinks[j:j + 1], wt['att_w_out'], late)
            ffn_w[1] = tuple(as_ffn(g, n) for g, n in zip(gathered[:3], ffn_names))
            wq = {n: g.reshape((N_CHIPS,) + arg[n].shape) for n, g in zip(late_names, gathered[3:])}
            wt['gdn_w_in'] = _join_quarters(wq['gdn_w_in'][:, 0], name="gdn_w_in_join")
            wt['gdn_w_out'] = wq['gdn_w_out'].reshape(GDN_VW, D_MODEL)
            wt['ple_w_gate'] = _from_quarters(wq['ple_w_gate'], 1)
            wt['ple_w_proj'] = _from_quarters(wq['ple_w_proj'], 2)
        else:
            h, sm = _gdn_layer_fwd(h, row(mix_norm[i]), wt['gdn_w_in'], wt['gdn_conv_w'][j], gdn_a_log[j:j + 1],
                                   gdn_dt_bias[j:j + 1], gdn_out_norm[j:j + 1], wt['gdn_w_out'])
        h, s1 = _ffn_fwd(h, row(wt['ffn_norm'][i, 1]), *ffn_w[i], (1,), f"ffn{i}b")
        h, sp = _ple_fwd(h, row(ple_norm[i]), wt['ple_w_gate'][i], wt['ple_w_proj'][i], p[i, 0], f"ple{i}")
        tape.append((s0, sm, s1, sp))

    dh, loss_local = _loss_head(h, loss_target[0], name="loss_head")
    loss = lax.psum(loss_local, ("x", "y", "c"))

    gr = {}
    ffn_g = tuple(lax.empty((N_CHIPS,) + arg[n].shape, F32) for n in ffn_names)
    d_ffn_norm = [[None, None] for _ in range(n_layers)]
    d_mix, d_ple_norm, d_ple_gate, d_ple_proj = [None] * n_layers, [None] * n_layers, [None] * n_layers, [None] * n_layers
    for i in reversed(range(n_layers)):
        j = i // 2
        s0, sm, s1, sp = tape[i]
        dh, d_ple_norm[i], d_ple_gate[i], d_ple_proj[i] = _ple_bwd(dh, sp, row(ple_norm[i]), wt['ple_w_gate'][i], p[i, 0],
                                                                   f"ple{i}")
        dh, d_ffn_norm[i][1], ffn_g = _ffn_bwd(dh, s1, row(wt['ffn_norm'][i, 1]), *ffn_w[i], (1,), ffn_g, (i, 1),
                                               f"ffn{i}b")
        if i % 2 == 0:
            (dh, d_mix[i], dw_in, gr['att_q_norm'], gr['att_k_norm'], gr['att_sinks'],
             dw_out) = _att_bwd(dh, sm, row(mix_norm[i]), wt['att_w_in'], att_q_norm[j:j + 1],
                                att_k_norm[j:j + 1], att_sinks[j:j + 1], wt['att_w_out'])
            gr['att_w_in'] = _split_quarters(dw_in, name="att_dw_in_split")
            gr['att_w_out'] = dw_out
        else:
            (dh, d_mix[i], dw_in, dconv, gr['gdn_a_log'], gr['gdn_dt_bias'], gr['gdn_out_norm'],
             dw_out) = _gdn_layer_bwd(dh, sm, row(mix_norm[i]), wt['gdn_conv_w'][j], gdn_a_log[j:j + 1],
                                      gdn_dt_bias[j:j + 1], gdn_out_norm[j:j + 1], wt['gdn_w_out'])
            gr['gdn_w_in'] = _split_quarters(dw_in, name="gdn_dw_in_split")
            gr['gdn_w_out'] = dw_out
            gr['gdn_conv_w'] = dconv[None]
        dh, d_ffn_norm[i][0], ffn_g = _ffn_bwd(dh, s0, row(wt['ffn_norm'][i, 0]), *ffn_w[i], (0,), ffn_g, (i, 0),
                                               f"ffn{i}a")
    grad_x = dh[None]

    gr['ffn_w_gate'], gr['ffn_w_up'], gr['ffn_w_down'] = ffn_g
    gr['ple_w_gate'] = _to_quarters(jnp.stack(d_ple_gate), 1)
    gr['ple_w_proj'] = _to_quarters(jnp.stack(d_ple_proj), 2)
    gr['ffn_norm'] = jnp.stack([jnp.stack([d_ffn_norm[i][k][0] for k in range(2)]) for i in range(n_layers)])
    gr['mix_norm'] = jnp.concatenate(d_mix, axis=0)
    gr['ple_norm'] = jnp.concatenate(d_ple_norm, axis=0)

    gq = [gr[n].reshape((N_CHIPS, 2, -1, arg[n].shape[-1])) for n in _BIG]
    got = _swap_halves(gq, name="grad_swap_halves")
    pairs = [_add_pair(g, o, core, name=f"grad_add_pair_{n}") for n, g, o in zip(_BIG, gq, got)]
    recv = _scatter_quarters(pairs, name="grad_scatter")
    tots = [_add_chips(pr, rc, chip, name=f"grad_add_chips_{n}") for n, pr, rc in zip(_BIG, pairs, recv)]
    theirs = _share_halves(tots, name="grad_share")

    whole_shapes = [arg[n].shape for n in _WHOLE]
    cut_full_shapes = [gr[n].shape for n in small_names]
    gvec = _pack([gr[n].reshape(-1) for n in _WHOLE + small_names], SMALL_ROW_MULT)
    gall = _sum_blocks(_gather_all(gvec, name="gather_small_grads"), N_DEV, name="sum_small_grads")
    parts = _unpack(gall, whole_shapes + cut_full_shapes)
    gsum = dict(zip(_WHOLE, parts))
    for n, g in zip(small_names, parts[len(_WHOLE):]):
        gsum[n] = lax.dynamic_index_in_dim(_to_quarters(g, _SMALL_CUT[n]), chip[0], axis=0, keepdims=False)

    delta, new_m, new_v = {}, {}, {}
    for n, mine, other in zip(_BIG, tots, theirs):
        res = _adamw_halves(_halves(arg[n]), _halves(arg["m_" + n]), _halves(arg["v_" + n]), mine, other, core,
                            name=f"adamw_{n}")
        gsum[n], delta[n], new_m[n], new_v[n] = (r.reshape(arg[n].shape) for r in res)
    for n in _WHOLE + small_names:
        delta[n], new_m[n], new_v[n] = _adamw(arg[n], gsum[n], arg["m_" + n], arg["v_" + n], name=f"adamw_{n}")
    return (loss, grad_x, *[gsum[n] for n in _WEIGHTS], *[delta[n] for n in _WEIGHTS],
            *[new_m[n] for n in _WEIGHTS], *[new_v[n] for n in _WEIGHTS])
```

```python
import functools
import math

import jax
import jax.numpy as jnp
from jax import lax
from jax.experimental import pallas as pl
from jax.experimental.pallas import tpu as pltpu

F32 = jnp.float32
BF16 = jnp.bfloat16
MESH = pl.DeviceIdType.MESH

LANES = 128
VMEM_LIMIT_BYTES = 56 * 1024 * 1024

EPS = 1e-6
D_MODEL = 1024
HEAD_DIM = 64
SB_HEADS = 8
SWA_HEADS = 8
SWA_KV_HEADS = 2
WINDOW = 128
GDN_K_HEADS = 8
GDN_V_HEADS = 16
GDN_HEAD_DIM = 128
GDN_CONV = 4
GDN_CHUNK = 64
SB_W = SB_HEADS * HEAD_DIM
SWA_QW = SWA_HEADS * HEAD_DIM
SWA_KVW = SWA_KV_HEADS * HEAD_DIM
GDN_KW = GDN_K_HEADS * GDN_HEAD_DIM
GDN_VW = GDN_V_HEADS * GDN_HEAD_DIM
GDN_CONV_W = 2 * GDN_KW + GDN_VW

ADAM_LR = 0.001
ADAM_B1 = 0.9
ADAM_B2 = 0.999
ADAM_EPS = 1e-08
ADAM_WD = 0.01
ADAM_STEP = 10

NEG = -1e30


def _params(*sem):
    return pltpu.CompilerParams(dimension_semantics=sem or None, vmem_limit_bytes=VMEM_LIMIT_BYTES)


def _tile(n, cap, align=LANES):
    if n <= cap:
        return n
    for t in range(cap - cap % align, 0, -align):
        if n % t == 0:
            return t
    return n


N_CHIPS = 4
MM_VMEM_BUDGET_BYTES = 40 * 1024 * 1024
Q = "q"


def _opnd(x):
    return x if isinstance(x, tuple) else (x, ())


def _mm(a, b, *, name, ta=False, tb=False, out_dtype=F32, res=None, scale=1.0, out_q=False, into=None,
        tm=None, tn=1024, tk=1024):
    (a_arr, a_lead), (b_arr, b_lead) = _opnd(a), _opnd(b)
    (k_a, m) = a_arr.shape[-2:] if ta else a_arr.shape[-2:][::-1]
    (n, k_b) = b_arr.shape[-2:] if tb else b_arr.shape[-2:][::-1]
    if into is not None:
        out_arr, out_lead = into
        out_q, out_dtype = Q in out_lead, out_arr.dtype
    else:
        out_lead = (Q,) if out_q else ()
    red_q = (Q in a_lead or Q in b_lead) and not out_q
    kq = min(k_a, k_b)
    assert (k_a == k_b) or (red_q and max(k_a, k_b) == N_CHIPS * kq), (a_arr.shape, b_arr.shape)
    tn, tk = _tile(n, tn), _tile(kq, tk)
    if tm is None:
        r_item = _opnd(res)[0].dtype.itemsize if res is not None else 0
        per_row = 2 * (tk * a_arr.dtype.itemsize + tn * (jnp.dtype(out_dtype).itemsize + r_item)) + 4 * tn
        room = MM_VMEM_BUDGET_BYTES - 2 * tk * tn * b_arr.dtype.itemsize
        tm = next(c for c in (4096, 2048, 1024, 512, 256, 128) if c * per_row <= room or c == 128)
    tm = _tile(m, tm)
    nk = kq // tk
    ksteps = nk * (N_CHIPS if red_q else 1)
    dims = (((0 if ta else 1,), (1 if tb else 0,)), ((), ()))
    has_res = res is not None

    def body(*refs):
        a_ref, b_ref = refs[0], refs[1]
        o_ref, acc_ref = refs[-2], refs[-1]
        k = pl.program_id(3)

        @pl.when(k == 0)
        def _():
            acc_ref[...] = jnp.zeros_like(acc_ref)

        acc_ref[...] += lax.dot_general(a_ref[...].astype(BF16), b_ref[...].astype(BF16), dims,
                                        preferred_element_type=F32)

        @pl.when(k == ksteps - 1)
        def _():
            r = acc_ref[...]
            if scale != 1.0:
                r = r * scale
            if has_res:
                r = r + refs[2][...].astype(F32)
            o_ref[...] = r.astype(o_ref.dtype)

    def spec(lead, blk, pos):
        def index(s, i, j, k):
            kk = k % nk if (red_q and Q in lead) else k
            quarter = s if out_q else k // nk
            return tuple(quarter if l == Q else l for l in lead) + pos(i, j, kk)
        return pl.BlockSpec((None,) * len(lead) + blk, index)

    a_spec = spec(a_lead, (tk, tm), lambda i, j, k: (k, i)) if ta else spec(a_lead, (tm, tk), lambda i, j, k: (i, k))
    b_spec = spec(b_lead, (tn, tk), lambda i, j, k: (j, k)) if tb else spec(b_lead, (tk, tn), lambda i, j, k: (k, j))
    o_spec = spec(out_lead, (tm, tn), lambda i, j, k: (i, j))
    in_specs, args = [a_spec, b_spec], [a_arr, b_arr]
    if has_res:
        r_arr, r_lead = _opnd(res)
        in_specs.append(spec(r_lead, (tm, tn), lambda i, j, k: (i, j)))
        args.append(r_arr)
    aliases = {}
    if into is not None:
        in_specs.append(pl.BlockSpec(memory_space=pl.ANY))
        args.append(out_arr)
        aliases = {len(args) - 1: 0}
        out_shape = jax.ShapeDtypeStruct(out_arr.shape, out_arr.dtype)
    else:
        out_shape = jax.ShapeDtypeStruct(((N_CHIPS,) if out_q else ()) + (m, n), out_dtype)
    return pl.pallas_call(
        body, name=name, grid=(N_CHIPS if out_q else 1, m // tm, n // tn, ksteps), in_specs=in_specs, out_specs=o_spec,
        out_shape=out_shape, scratch_shapes=[pltpu.VMEM((tm, tn), F32)], input_output_aliases=aliases,
        compiler_params=_params("parallel", "parallel", "parallel", "arbitrary"),
    )(*args)


def _row_spec(r, tm):
    if isinstance(r, tuple):
        arr, width, cb = r
        return arr, pl.BlockSpec((tm, width), lambda i, cb=cb: (i, cb))
    return r, pl.BlockSpec((tm, r.shape[1]), lambda i: (i, 0))


def _const_spec(c):
    return pl.BlockSpec(c.shape, lambda i: (0,) * c.ndim)


def _row_fwd(fn, rows, consts, outs, *, name, tm=256):
    tm = _tile(_row_spec(rows[0], tm)[0].shape[0], tm, 8)
    arrs, specs = zip(*[_row_spec(r, tm) for r in rows])
    t = arrs[0].shape[0]
    nr, nc = len(rows), len(consts)

    def body(*refs):
        vals = [r[...].astype(F32) for r in refs[:nr + nc]]
        res = fn(*vals)
        for o_ref, v in zip(refs[nr + nc:], res):
            o_ref[...] = v.astype(o_ref.dtype)

    out = pl.pallas_call(
        body, name=name, grid=(t // tm,),
        in_specs=list(specs) + [_const_spec(c) for c in consts],
        out_specs=[pl.BlockSpec((tm, w), lambda i: (i, 0)) for w, _ in outs],
        out_shape=[jax.ShapeDtypeStruct((t, w), dt) for w, dt in outs],
        compiler_params=_params("parallel"),
    )(*arrs, *consts)
    return list(out)


def _row_bwd(fn, rows, consts, cts, row_grads, const_grads, *, name, tm=256):
    tm = _tile(_row_spec(rows[0], tm)[0].shape[0], tm, 8)
    arrs, specs = zip(*[_row_spec(r, tm) for r in rows])
    ct_arrs, ct_specs = zip(*[_row_spec(r, tm) for r in cts])
    t = arrs[0].shape[0]
    nr, nc, nt = len(rows), len(consts), len(cts)
    n_in = nr + nc + nt

    def body(*refs):
        vals = [r[...].astype(F32) for r in refs[:nr + nc]]
        ctv = tuple(r[...].astype(F32) for r in refs[nr + nc:n_in])
        _, vjp = jax.vjp(fn, *vals)
        g = vjp(ctv)
        outs = refs[n_in:]
        for (idx, _), o_ref in zip(row_grads, outs[:len(row_grads)]):
            o_ref[...] = g[idx].astype(o_ref.dtype)
        first = pl.program_id(0) == 0
        for ci, o_ref in zip(const_grads, outs[len(row_grads):]):
            @pl.when(first)
            def _(o_ref=o_ref):
                o_ref[...] = jnp.zeros_like(o_ref)

            o_ref[...] += g[nr + ci]

    widths = [(_row_spec(rows[idx], tm)[1].block_shape[1], dt) for idx, dt in row_grads]
    out = pl.pallas_call(
        body, name=name, grid=(t // tm,),
        in_specs=list(specs) + [_const_spec(c) for c in consts] + list(ct_specs),
        out_specs=[pl.BlockSpec((tm, w), lambda i: (i, 0)) for w, _ in widths]
        + [_const_spec(consts[ci]) for ci in const_grads],
        out_shape=[jax.ShapeDtypeStruct((t, w), dt) for w, dt in widths]
        + [jax.ShapeDtypeStruct(consts[ci].shape, F32) for ci in const_grads],
        compiler_params=_params("arbitrary"),
    )(*arrs, *consts, *ct_arrs)
    return list(out)


def _rms(x, g):
    return x * lax.rsqrt(jnp.mean(x * x, axis=-1, keepdims=True) + EPS) * g


def _f_rms(h, g):
    return (_rms(h, g),)


def _f_rms_res(h, g):
    return (_rms(h, g), h)


def _f_swiglu(g, u):
    return (g * jax.nn.sigmoid(g) * u,)


def _f_ple(h, gl, pp):
    return (h + jax.nn.sigmoid(gl) * pp,)


def _f_gdn_out(o, z, gain):
    outs = []
    for hd in range(GDN_V_HEADS):
        sl = slice(hd * GDN_HEAD_DIM, (hd + 1) * GDN_HEAD_DIM)
        oh, zh = o[:, sl], z[:, sl]
        outs.append(_rms(oh, gain) * (zh * jax.nn.sigmoid(zh)))
    return (jnp.concatenate(outs, axis=1),)


def _loss_head(y, target, *, name, tm=512):
    t, d = y.shape
    tm = _tile(t, tm, 8)

    def body(y_ref, t_ref, dy_ref, l_ref):
        @pl.when(pl.program_id(0) == 0)
        def _():
            l_ref[...] = jnp.zeros_like(l_ref)

        e = y_ref[...] - t_ref[...]
        dy_ref[...] = e * (1.0 / d)
        l_ref[...] += jnp.sum(e * e) * (0.5 / d)

    dy, l = pl.pallas_call(
        body, name=name, grid=(t // tm,),
        in_specs=[pl.BlockSpec((tm, d), lambda i: (i, 0))] * 2,
        out_specs=[pl.BlockSpec((tm, d), lambda i: (i, 0)), pl.BlockSpec((8, LANES), lambda i: (0, 0))],
        out_shape=[jax.ShapeDtypeStruct((t, d), F32), jax.ShapeDtypeStruct((8, LANES), F32)],
        compiler_params=_params("arbitrary"),
    )(y, target)
    return dy, l[0, 0]


def _dg(a, b, ca, cb):
    nb = a.ndim - 2
    batch = tuple(range(nb))
    return lax.dot_general(a, b, (((ca + nb,), (cb + nb,)), (batch, batch)), preferred_element_type=F32)


def _b(x):
    return x.astype(BF16)


@jax.custom_vjp
def _bdot(a, b):
    return _dg(_b(a), _b(b), 1, 0)


def _bdot_fwd(a, b):
    return _bdot(a, b), (a, b)


def _bdot_bwd(r, ct):
    a, b = r
    return _dg(_b(ct), _b(b), 1, 1), _dg(_b(a), _b(ct), 0, 0)


_bdot.defvjp(_bdot_fwd, _bdot_bwd)


@jax.custom_vjp
def _bdot_nt(a, b):
    return _dg(_b(a), _b(b), 1, 1)


def _bdot_nt_fwd(a, b):
    return _bdot_nt(a, b), (a, b)


def _bdot_nt_bwd(r, ct):
    a, b = r
    return _dg(_b(ct), _b(b), 1, 0), _dg(_b(ct), _b(a), 0, 0)


_bdot_nt.defvjp(_bdot_nt_fwd, _bdot_nt_bwd)


@jax.custom_vjp
def _bdot_tn(a, b):
    return _dg(_b(a), _b(b), 0, 0)


def _bdot_tn_fwd(a, b):
    return _bdot_tn(a, b), (a, b)


def _bdot_tn_bwd(r, ct):
    a, b = r
    return _dg(_b(b), _b(ct), 1, 1), _dg(_b(a), _b(ct), 1, 0)


_bdot_tn.defvjp(_bdot_tn_fwd, _bdot_tn_bwd)


def _two(x):
    hi = x.astype(BF16)
    return hi, (x - hi.astype(F32)).astype(BF16)


def _dg3(a, b, ca, cb):
    (ah, al), (bh, bl) = _two(a), _two(b)
    return _dg(ah, bh, ca, cb) + (_dg(ah, bl, ca, cb) + _dg(al, bh, ca, cb))


@jax.custom_vjp
def _hdot(a, b):
    return _dg3(a, b, 1, 0)


def _hdot_fwd(a, b):
    return _hdot(a, b), (a, b)


def _hdot_bwd(r, ct):
    a, b = r
    return _dg3(ct, b, 1, 1), _dg3(a, ct, 0, 0)


_hdot.defvjp(_hdot_fwd, _hdot_bwd)


@jax.custom_vjp
def _unit_lower_inverse(x):
    c = x.shape[-1]
    eye = (lax.broadcasted_iota(jnp.int32, x.shape, 1) == lax.broadcasted_iota(jnp.int32, x.shape, 2)).astype(F32)
    inv, pw = eye + x, x
    for _ in range(int(math.log2(c)) - 1):
        pw = _dg3(pw, pw, 1, 0)
        inv = inv + _dg3(inv, pw, 1, 0)
    return inv


def _unit_lower_inverse_fwd(x):
    inv = _unit_lower_inverse(x)
    return inv, inv


def _unit_lower_inverse_bwd(inv, ct):
    return (_dg3(_dg3(inv, ct, 0, 0), inv, 1, 1),)


_unit_lower_inverse.defvjp(_unit_lower_inverse_fwd, _unit_lower_inverse_bwd)


def _split_dot(x, u):
    hi, lo = _two(x)
    return _dg(hi, u, 1, 0) + _dg(lo, u, 1, 0)


@jax.custom_vjp
def _ldot(l01, x):
    hi, lo = _two(x)
    l01 = l01.astype(BF16)
    return _dg(l01, hi, 1, 0) + _dg(l01, lo, 1, 0)


def _ldot_fwd(l01, x):
    return _ldot(l01, x), l01


def _ldot_bwd(l01, ct):
    hi, lo = _two(ct)
    l01b = l01.astype(BF16)
    return jnp.zeros_like(l01), _dg(l01b, hi, 0, 0) + _dg(l01b, lo, 0, 0)


_ldot.defvjp(_ldot_fwd, _ldot_bwd)


SB_BLK = 128
SB_KEYS = 512
SB_PAIRS = 2
SB_SCALE = HEAD_DIM ** -0.5


def _log_sigmoid(z):
    return jnp.minimum(z, 0.0) - jnp.log(1.0 + jnp.exp(-jnp.abs(z)))


def _sb_consts(t):
    kb = min(SB_KEYS, t)
    nh = 2 * SB_PAIRS
    lane = lax.broadcasted_iota(jnp.int32, (nh, SB_BLK, kb), 2)
    row = lax.broadcasted_iota(jnp.int32, (nh, SB_BLK, kb), 1)
    ur = lax.broadcasted_iota(jnp.int32, (kb, kb), 0)
    uc = lax.broadcasted_iota(jnp.int32, (kb, kb), 1)
    return kb, nh, lane, row, ur, uc


def _sb_heads(x):
    head0 = lax.broadcasted_iota(jnp.int32, (x.shape[0], LANES), 1) < HEAD_DIM
    out = []
    for p in range(SB_PAIRS):
        blk = x[:, p * LANES:(p + 1) * LANES]
        out += [jnp.where(head0, blk, 0.0), jnp.where(head0, 0.0, blk)]
    return jnp.stack(out)


def _sb_pairs(x):
    return jnp.stack([x[:, (h // 2) * LANES:(h // 2 + 1) * LANES] for h in range(2 * SB_PAIRS)])


def _sb_merge(x):
    head0 = lax.broadcasted_iota(jnp.int32, (x.shape[1], LANES), 1) < HEAD_DIM
    return jnp.concatenate([jnp.where(head0, x[2 * p], x[2 * p + 1]) for p in range(SB_PAIRS)], axis=1)


def _sb_rows_dot(x, u):
    nh, rows, k = x.shape
    return _split_dot(x.reshape(nh * rows, k), u).reshape(nh, rows, k)


def _sb_fwd(proj, *, name, gather=()):
    t = proj.shape[0]
    nb = t // SB_BLK
    width = SB_PAIRS * LANES
    ng = SB_W // width
    na = len(gather)

    def body(q_ref, k_ref, v_ref, *rest):
        o_ref, r_ref = rest[na:na + 2]
        i = pl.program_id(1)
        if na:
            step = pl.program_id(0) * nb + i
            copies = lambda **kw: _gather_copies(rest[:na], rest[na + 2:2 * na + 2], *rest[2 * na + 2:], **kw)
            pl.when(step == 0)(lambda: _gather_start(copies(only_first=True)))
        kb, nh, lane, row, ur, uc = _sb_consts(t)
        u_suffix = (ur >= uc).astype(BF16)
        qh = _b(_sb_heads(q_ref[...]) * SB_SCALE)
        diag = (i * SB_BLK) // kb

        def block(j, carry, masked):
            acc, car = carry
            keys = pl.ds(pl.multiple_of(j * kb, kb), kb)
            kj, vj = _b(_sb_pairs(k_ref[keys, :])), _b(_sb_pairs(v_ref[keys, :]))
            z = _dg(qh, kj, 1, 1)
            lk = _log_sigmoid(-z)
            if masked:
                causal = (j * kb + lane) < (i * SB_BLK + row)
                lk = jnp.where(causal, lk, 0.0)
            suf = _sb_rows_dot(lk, u_suffix) + car
            w = jnp.exp(z + suf)
            if masked:
                w = jnp.where(causal, w, 0.0)
            return acc + _dg(_b(w), vj, 1, 0), suf[:, :, 0:1]

        zero = (jnp.zeros((nh, SB_BLK, LANES), F32), jnp.zeros((nh, SB_BLK, 1), F32))
        carry = block(diag, zero, True)
        acc, car = lax.fori_loop(0, diag, lambda s, c: block(diag - 1 - s, c, False), carry)
        o_ref[...] = _sb_merge(acc)
        r_ref[...] = _sb_merge(jnp.broadcast_to(car, (nh, SB_BLK, LANES)))
        if na:
            pl.when(step == ng * nb - 1)(lambda: _gather_finish(copies()))

    return pl.pallas_call(
        body, name=name, grid=(ng, nb),
        in_specs=[pl.BlockSpec((SB_BLK, width), lambda p, i: (i, p)),
                  pl.BlockSpec((t, width), lambda p, i: (0, ng + p)),
                  pl.BlockSpec((t, width), lambda p, i: (0, 2 * ng + p))] + [ANY] * na,
        out_specs=[pl.BlockSpec((SB_BLK, width), lambda p, i: (i, p))] * 2 + [ANY] * na,
        out_shape=[jax.ShapeDtypeStruct((t, SB_W), F32)] * 2
        + [jax.ShapeDtypeStruct((N_CHIPS,) + g.shape, g.dtype) for g in gather],
        scratch_shapes=_gather_scratch(na) if na else [],
        compiler_params=_params("arbitrary", "arbitrary"),
    )(proj, proj, proj, *gather)


def _sb_bwd(proj, rtot, dout, *, name, scatter=()):
    t = proj.shape[0]
    nb = t // SB_BLK
    width = SB_PAIRS * LANES
    ng = SB_W // width
    na = len(scatter)

    def body(q_ref, k_ref, v_ref, r_ref, do_ref, *rest):
        dq_ref, dk_ref, dv_ref = rest[na:na + 3]
        i = pl.program_id(1)
        if na:
            step = pl.program_id(0) * nb + i
            copies = lambda: _scatter_copies(rest[:na], rest[na + 3:4 * na + 3], *rest[4 * na + 3:])
            pl.when(step == 0)(lambda: [cp.start() for cp in copies()] and None)
        kb, nh, lane, row, ur, uc = _sb_consts(t)
        u_incl = (ur <= uc).astype(BF16)
        u_excl = (ur < uc).astype(BF16)
        q, do = q_ref[...], do_ref[...]
        qh, doh = _b(_sb_heads(q) * SB_SCALE), _b(_sb_heads(do))
        qb, dob = _b(_sb_pairs(q) * SB_SCALE), _b(_sb_pairs(do))
        rh = jnp.min(_sb_heads(r_ref[...]), axis=2, keepdims=True)
        diag = (i * SB_BLK) // kb

        @pl.when(i == 0)
        def _():
            dk_ref[...] = jnp.zeros_like(dk_ref)
            dv_ref[...] = jnp.zeros_like(dv_ref)

        def block(j, carry, masked):
            dq_acc, clk, ce = carry
            keys = pl.ds(pl.multiple_of(j * kb, kb), kb)
            kj, vj = _b(_sb_pairs(k_ref[keys, :])), _b(_sb_pairs(v_ref[keys, :]))
            z = _dg(qh, kj, 1, 1)
            lk = _log_sigmoid(-z)
            ls = z + lk
            if masked:
                causal = (j * kb + lane) < (i * SB_BLK + row)
                lk = jnp.where(causal, lk, 0.0)
            pre = _sb_rows_dot(lk, u_incl) + clk
            w = jnp.exp(ls + (rh - pre))
            if masked:
                w = jnp.where(causal, w, 0.0)
            e = _dg(doh, vj, 1, 1) * w
            pre_e = _sb_rows_dot(e, u_excl) + ce
            sig = jnp.exp(ls)
            dz = e - sig * (e + pre_e)
            if masked:
                dz = jnp.where(causal, dz, 0.0)
            dzb = _b(dz)
            dk_ref[keys, :] += _sb_merge(_dg(dzb, qb, 0, 0))
            dv_ref[keys, :] += _sb_merge(_dg(_b(w), dob, 0, 0))
            return dq_acc + _dg(dzb, kj, 1, 0), pre[:, :, kb - 1:], pre_e[:, :, kb - 1:] + e[:, :, kb - 1:]

        zero = (jnp.zeros((nh, SB_BLK, LANES), F32), jnp.zeros((nh, SB_BLK, 1), F32), jnp.zeros((nh, SB_BLK, 1), F32))
        carry = lax.fori_loop(0, diag, lambda j, c: block(j, c, False), zero)
        dq_acc, _, _ = block(diag, carry, True)
        dq_ref[...] = _sb_merge(dq_acc) * SB_SCALE
        if na:
            pl.when(step == ng * nb - 1)(lambda: [cp.wait() for cp in copies()] and None)

    blk = pl.BlockSpec((SB_BLK, width), lambda p, i: (i, p))
    whole = pl.BlockSpec((t, width), lambda p, i: (0, p))
    return pl.pallas_call(
        body, name=name, grid=(ng, nb),
        in_specs=[blk,
                  pl.BlockSpec((t, width), lambda p, i: (0, ng + p)),
                  pl.BlockSpec((t, width), lambda p, i: (0, 2 * ng + p)),
                  blk, blk] + [ANY] * na,
        out_specs=[blk, whole, whole] + [ANY] * (3 * na),
        out_shape=[jax.ShapeDtypeStruct((t, SB_W), F32)] * 3 + _scatter_shapes(scatter),
        scratch_shapes=_dma_sems(3 * na) if na else [],
        compiler_params=_params("arbitrary", "arbitrary"),
    )(proj, proj, proj, rtot, dout, *scatter)


SWA_G = SWA_HEADS // SWA_KV_HEADS


def _swa_heads(first, qs, ks, vs, qg, kg, sinks):
    shape = (SWA_HEADS, WINDOW, 2 * WINDOW)
    qi = lax.broadcasted_iota(jnp.int32, shape, 1)
    kj = lax.broadcasted_iota(jnp.int32, shape, 2)
    dist = qi + WINDOW - kj
    valid = (dist >= 0) & (dist < WINDOW) & (jnp.logical_not(first) | (kj >= WINDOW))
    head = lax.broadcasted_iota(jnp.int32, (SWA_HEADS, 1, 1), 0)
    slope = sum(jnp.where(head == h, 2.0 ** (-8.0 * (h + 1) / SWA_HEADS), 0.0) for h in range(SWA_HEADS))
    kn = _rms(ks, kg)
    per_q_head = lambda x: jnp.concatenate([x[h // SWA_G:h // SWA_G + 1] for h in range(SWA_HEADS)], axis=0)
    k8, v8 = per_q_head(kn), per_q_head(vs)
    s = _bdot_nt(_rms(qs, qg), k8) * (HEAD_DIM ** -0.5)
    s = jnp.where(valid, s - slope * dist.astype(F32), NEG)
    m = lax.stop_gradient(jnp.maximum(jnp.max(s, axis=2, keepdims=True), sinks))
    p = jnp.exp(s - m)
    den = jnp.sum(p, axis=2, keepdims=True) + jnp.exp(sinks - m)
    return _bdot(p / den, v8)


def _swa_split(q, kp, kc, vp, vc, sk):
    lanes = lambda x, n: jnp.stack([x[:, h * HEAD_DIM:(h + 1) * HEAD_DIM] for h in range(n)])
    k2, v2 = jnp.concatenate([kp, kc], axis=0), jnp.concatenate([vp, vc], axis=0)
    sinks = jnp.stack([sk[:, h:h + 1] for h in range(SWA_HEADS)])
    return lanes(q, SWA_HEADS), lanes(k2, SWA_KV_HEADS), lanes(v2, SWA_KV_HEADS), sinks


def _swa_join(x):
    return jnp.concatenate([x[h] for h in range(x.shape[0])], axis=1)


def _swa_specs(t):
    qcb = (3 * SB_W) // SWA_QW
    kcb = (3 * SB_W + SWA_QW) // SWA_KVW
    prev = lambda i: jnp.maximum(i - 1, 0)
    return [pl.BlockSpec((WINDOW, SWA_QW), lambda i: (i, qcb)),
            pl.BlockSpec((WINDOW, SWA_KVW), lambda i: (prev(i), kcb)),
            pl.BlockSpec((WINDOW, SWA_KVW), lambda i: (i, kcb)),
            pl.BlockSpec((WINDOW, SWA_KVW), lambda i: (prev(i), kcb + 1)),
            pl.BlockSpec((WINDOW, SWA_KVW), lambda i: (i, kcb + 1)),
            pl.BlockSpec((1, HEAD_DIM), lambda i: (0, 0)),
            pl.BlockSpec((1, HEAD_DIM), lambda i: (0, 0)),
            pl.BlockSpec((1, SWA_HEADS), lambda i: (0, 0))]


def _swa_fwd(proj, qg, kg, sinks, *, name):
    t = proj.shape[0]

    def body(q_ref, kp_ref, kc_ref, vp_ref, vc_ref, qg_ref, kg_ref, sk_ref, o_ref):
        first = pl.program_id(0) == 0
        qs, ks, vs, sk = _swa_split(q_ref[...], kp_ref[...], kc_ref[...], vp_ref[...], vc_ref[...], sk_ref[...])
        o_ref[...] = _swa_join(_swa_heads(first, qs, ks, vs, qg_ref[...], kg_ref[...], sk))

    return pl.pallas_call(
        body, name=name, grid=(t // WINDOW,), in_specs=_swa_specs(t),
        out_specs=pl.BlockSpec((WINDOW, SWA_QW), lambda i: (i, 0)),
        out_shape=jax.ShapeDtypeStruct((t, SWA_QW), F32),
        compiler_params=_params("parallel"),
    )(proj, proj, proj, proj, proj, qg, kg, sinks)


def _swa_bwd(proj, qg, kg, sinks, dout, *, name):
    t = proj.shape[0]

    def body(q_ref, kp_ref, kc_ref, vp_ref, vc_ref, qg_ref, kg_ref, sk_ref, do_ref,
             dq_ref, dk_ref, dv_ref, dqg_ref, dkg_ref, dsk_ref):
        i = pl.program_id(0)
        first = i == 0

        @pl.when(first)
        def _():
            for r in (dk_ref, dv_ref, dqg_ref, dkg_ref, dsk_ref):
                r[...] = jnp.zeros_like(r)

        qs, ks, vs, sk = _swa_split(q_ref[...], kp_ref[...], kc_ref[...], vp_ref[...], vc_ref[...], sk_ref[...])
        do = do_ref[...]
        cts = jnp.stack([do[:, h * HEAD_DIM:(h + 1) * HEAD_DIM] for h in range(SWA_HEADS)])
        _, vjp = jax.vjp(functools.partial(_swa_heads, first), qs, ks, vs, qg_ref[...], kg_ref[...], sk)
        dqs, dks, dvs, dqg, dkg, dsk = vjp(cts)
        dq_ref[...] = _swa_join(dqs)
        dk2, dv2 = _swa_join(dks), _swa_join(dvs)
        cur = pl.ds(pl.multiple_of(i * WINDOW, WINDOW), WINDOW)
        prv = pl.ds(pl.multiple_of(jnp.maximum(i - 1, 0) * WINDOW, WINDOW), WINDOW)
        dk_ref[prv, :] += dk2[:WINDOW]
        dv_ref[prv, :] += dv2[:WINDOW]
        dk_ref[cur, :] += dk2[WINDOW:]
        dv_ref[cur, :] += dv2[WINDOW:]
        dqg_ref[...] += dqg
        dkg_ref[...] += dkg
        dsk_ref[...] += _swa_join(dsk)

    whole = lambda shape: pl.BlockSpec(shape, lambda i: (0, 0))
    return pl.pallas_call(
        body, name=name, grid=(t // WINDOW,),
        in_specs=_swa_specs(t) + [pl.BlockSpec((WINDOW, SWA_QW), lambda i: (i, 0))],
        out_specs=[pl.BlockSpec((WINDOW, SWA_QW), lambda i: (i, 0)), whole((t, SWA_KVW)), whole((t, SWA_KVW)),
                   whole((1, HEAD_DIM)), whole((1, HEAD_DIM)), whole((1, SWA_HEADS))],
        out_shape=[jax.ShapeDtypeStruct((t, SWA_QW), F32), jax.ShapeDtypeStruct((t, SWA_KVW), F32),
                   jax.ShapeDtypeStruct((t, SWA_KVW), F32), jax.ShapeDtypeStruct((1, HEAD_DIM), F32),
                   jax.ShapeDtypeStruct((1, HEAD_DIM), F32), jax.ShapeDtypeStruct((1, SWA_HEADS), F32)],
        compiler_params=_params("arbitrary"),
    )(proj, proj, proj, proj, proj, qg, kg, sinks, dout)


CONV_CB = 512
CONV_TM = 512
HALO = 8


def _conv_pre(x_ref, h_ref, w_ref, i):
    halo = jnp.where(i > 0, h_ref[...], 0.0)
    xe = jnp.concatenate([halo, x_ref[...]], axis=0)
    tm = x_ref.shape[0]
    w = w_ref[...]
    c = sum(w[k:k + 1, :] * xe[HALO - (GDN_CONV - 1) + k:HALO - (GDN_CONV - 1) + k + tm] for k in range(GDN_CONV))
    return c, xe


def _conv_specs(tm, cb):
    return [pl.BlockSpec((tm, cb), lambda c, i: (i, c)),
            pl.BlockSpec((HALO, cb), lambda c, i: (jnp.maximum(i * (tm // HALO) - 1, 0), c)),
            pl.BlockSpec((GDN_CONV, cb), lambda c, i: (0, c))]


def _conv_fwd(x, w, dact=None, *, name):
    t, ch = x.shape
    tm, cb = _tile(t, CONV_TM), _tile(ch, CONV_CB)

    def body(*refs):
        x_ref, h_ref, w_ref = refs[:3]
        c, _ = _conv_pre(x_ref, h_ref, w_ref, pl.program_id(1))
        sig = jax.nn.sigmoid(c)
        if dact is None:
            refs[3][...] = c * sig
        else:
            refs[4][...] = refs[3][...] * (sig * (1.0 + c * (1.0 - sig)))

    tile = pl.BlockSpec((tm, cb), lambda c, i: (i, c))
    extra = () if dact is None else (dact,)
    return pl.pallas_call(
        body, name=name, grid=(ch // cb, t // tm),
        in_specs=_conv_specs(tm, cb) + [tile] * len(extra), out_specs=tile,
        out_shape=jax.ShapeDtypeStruct((t, ch), F32),
        compiler_params=_params("parallel", "parallel"),
    )(x, x, w, *extra)


def _conv_bwd(x, w, dc, *, name):
    t, ch = x.shape
    tm, cb = _tile(t, CONV_TM), _tile(ch, CONV_CB)
    nt = t // tm

    def body(x_ref, h_ref, w_ref, dc_ref, nh_ref, dx_ref, dw_ref):
        i = pl.program_id(1)

        @pl.when(i == 0)
        def _():
            dw_ref[...] = jnp.zeros_like(dw_ref)

        halo = jnp.where(i > 0, h_ref[...], 0.0)
        xe = jnp.concatenate([halo, x_ref[...]], axis=0)
        dc = dc_ref[...]
        dce = jnp.concatenate([dc, jnp.where(i < nt - 1, nh_ref[...], 0.0)], axis=0)
        w = w_ref[...]
        last = GDN_CONV - 1
        dx_ref[...] = sum(w[k:k + 1, :] * dce[last - k:last - k + tm] for k in range(GDN_CONV))
        dw_ref[...] += jnp.concatenate(
            [jnp.sum(dc * xe[HALO - last + k:HALO - last + k + tm], axis=0, keepdims=True) for k in range(GDN_CONV)],
            axis=0)

    tile = pl.BlockSpec((tm, cb), lambda c, i: (i, c))
    nxt = pl.BlockSpec((HALO, cb), lambda c, i: (jnp.minimum((i + 1) * (tm // HALO), t // HALO - 1), c))
    return pl.pallas_call(
        body, name=name, grid=(ch // cb, nt),
        in_specs=_conv_specs(tm, cb) + [tile, nxt],
        out_specs=[tile, pl.BlockSpec((GDN_CONV, cb), lambda c, i: (0, c))],
        out_shape=[jax.ShapeDtypeStruct((t, ch), F32), jax.ShapeDtypeStruct((GDN_CONV, ch), F32)],
        compiler_params=_params("parallel", "arbitrary"),
    )(x, x, w, dc, dc)


def _gdn_chunk(qraw, kraw, v, bl, a, alog, dtb, state):
    c, d = GDN_CHUNK, GDN_HEAD_DIM
    nh = qraw.shape[0]
    ri = lax.broadcasted_iota(jnp.int32, (nh, c, c), 1)
    ci = lax.broadcasted_iota(jnp.int32, (nh, c, c), 2)
    incl, strict = ri >= ci, ri > ci
    q = qraw * lax.rsqrt(jnp.sum(qraw * qraw, axis=-1, keepdims=True) + EPS) * (d ** -0.5)
    k = kraw * lax.rsqrt(jnp.sum(kraw * kraw, axis=-1, keepdims=True) + EPS)
    beta = jax.nn.sigmoid(bl)
    g = -jnp.exp(alog) * jax.nn.softplus(a + dtb)
    gc = _ldot(incl.astype(F32), jnp.broadcast_to(g, (nh, c, d)))
    gcm = gc[:, :, :c]
    decay = jnp.exp(jnp.where(incl, gcm - jnp.swapaxes(gcm, 1, 2), NEG))
    eg = jnp.exp(gc)
    kbeta = k * beta
    x = -jnp.where(strict, _bdot_nt(kbeta, k) * decay, 0.0)
    tinv = _unit_lower_inverse(x)
    u = _hdot(tinv, v * beta)
    w = _hdot(tinv, kbeta * eg)
    attn = jnp.where(incl, _bdot_nt(q, k) * decay, 0.0)
    glast = gc[:, c - 1:c, :]
    v_new = u - _bdot(w, state)
    o = _bdot(q * eg, state) + _bdot(attn, v_new)
    state = state * jnp.exp(glast) + _bdot_tn(k * jnp.exp(glast - gc), v_new)
    return o, state


GDN_REP = GDN_V_HEADS // GDN_K_HEADS
GDN_HB = 8


def _gdn_pick(vals, kh, r):
    ba, alog, dtb = vals
    lane = lax.broadcasted_iota(jnp.int32, ba.shape, 1)
    hv = kh * GDN_REP + r
    bl = jnp.sum(jnp.where(lane == hv, ba, 0.0), axis=1, keepdims=True)
    a = jnp.sum(jnp.where(lane == GDN_V_HEADS + hv, ba, 0.0), axis=1, keepdims=True)
    lane1 = lax.broadcasted_iota(jnp.int32, alog.shape, 1)
    al = jnp.sum(jnp.where(lane1 == hv, alog, 0.0), axis=1, keepdims=True)
    db = jnp.sum(jnp.where(lane1 == hv, dtb, 0.0), axis=1, keepdims=True)
    return bl, a, al, db


def _gdn_stack(qs, ks, vs, small, j):
    d = GDN_HEAD_DIM
    per = [[], [], [], [], [], [], []]
    for hh in range(GDN_HB):
        q, k = qs[:, hh * d:(hh + 1) * d], ks[:, hh * d:(hh + 1) * d]
        for r in range(GDN_REP):
            col = (hh * GDN_REP + r) * d
            for lst, val in zip(per, (q, k, vs[:, col:col + d]) + _gdn_pick(small, j * GDN_HB + hh, r)):
                lst.append(val)
    return tuple(jnp.stack(lst) for lst in per)


def _gdn_specs(nchunk, rev):
    c, d = GDN_CHUNK, GDN_HEAD_DIM
    at = (lambda n: nchunk - 1 - n) if rev else (lambda n: n)
    ng = GDN_K_HEADS // GDN_HB
    return at, [pl.BlockSpec((c, GDN_HB * d), lambda n, j: (at(n), j)),
                pl.BlockSpec((c, GDN_HB * d), lambda n, j: (at(n), ng + j)),
                pl.BlockSpec((c, GDN_HB * GDN_REP * d), lambda n, j: (at(n), ng + j)),
                pl.BlockSpec((c, 2 * GDN_V_HEADS), lambda n, j: (at(n), 0)),
                pl.BlockSpec((1, GDN_V_HEADS), lambda n, j: (0, 0)),
                pl.BlockSpec((1, GDN_V_HEADS), lambda n, j: (0, 0))]


def _gdn_fwd(act, ba, alog, dtb, *, name):
    t = act.shape[0]
    c, d = GDN_CHUNK, GDN_HEAD_DIM
    nchunk = t // c
    at, specs = _gdn_specs(nchunk, False)

    def body(q_ref, k_ref, v_ref, ba_ref, al_ref, db_ref, o_ref, s_ref, state):
        n, j = pl.program_id(0), pl.program_id(1)
        heads = pl.ds(j * GDN_HB, GDN_HB)

        @pl.when(n == 0)
        def _():
            state[heads] = jnp.zeros((GDN_HB, GDN_REP, d, d), F32)

        s_in = state[heads]
        s_ref[...] = s_in
        args = _gdn_stack(q_ref[...], k_ref[...], v_ref[...], (ba_ref[...], al_ref[...], db_ref[...]), j)
        o, s_new = _gdn_chunk(*args, s_in.reshape(GDN_HB * GDN_REP, d, d))
        o_ref[...] = jnp.concatenate([o[b] for b in range(GDN_HB * GDN_REP)], axis=1)
        state[heads] = s_new.reshape(GDN_HB, GDN_REP, d, d)

    return pl.pallas_call(
        body, name=name, grid=(nchunk, GDN_K_HEADS // GDN_HB), in_specs=specs,
        out_specs=[pl.BlockSpec((c, GDN_HB * GDN_REP * d), lambda n, j: (n, j)),
                   pl.BlockSpec((None, GDN_HB, GDN_REP, d, d), lambda n, j: (n, j, 0, 0, 0))],
        out_shape=[jax.ShapeDtypeStruct((t, GDN_VW), F32),
                   jax.ShapeDtypeStruct((nchunk, GDN_K_HEADS, GDN_REP, d, d), F32)],
        scratch_shapes=[pltpu.VMEM((GDN_K_HEADS, GDN_REP, d, d), F32)],
        compiler_params=_params("arbitrary", "arbitrary"),
    )(act, act, act, ba, alog, dtb)


def _gdn_bwd(act, ba, alog, dtb, states, dout, *, name):
    t = act.shape[0]
    c, d = GDN_CHUNK, GDN_HEAD_DIM
    nchunk = t // c
    at, specs = _gdn_specs(nchunk, True)

    def body(q_ref, k_ref, v_ref, ba_ref, al_ref, db_ref, s_ref, do_ref,
             dq_ref, dk_ref, dv_ref, dba_ref, dal_ref, ddb_ref, dstate):
        n, j = pl.program_id(0), pl.program_id(1)

        @pl.when(n == 0)
        def _():
            dstate[pl.ds(j * GDN_HB, GDN_HB)] = jnp.zeros((GDN_HB, GDN_REP, d, d), F32)

        @pl.when((n == 0) & (j == 0))
        def _():
            dal_ref[...] = jnp.zeros_like(dal_ref)
            ddb_ref[...] = jnp.zeros_like(ddb_ref)

        @pl.when(j == 0)
        def _():
            dba_ref[...] = jnp.zeros_like(dba_ref)

        heads = pl.ds(j * GDN_HB, GDN_HB)
        nh = GDN_HB * GDN_REP
        args = _gdn_stack(q_ref[...], k_ref[...], v_ref[...], (ba_ref[...], al_ref[...], db_ref[...]), j)
        _, vjp = jax.vjp(_gdn_chunk, *args, s_ref[...].reshape(nh, d, d))
        do = do_ref[...]
        do = jnp.stack([do[:, b * d:(b + 1) * d] for b in range(nh)])
        gq, gk, gv, gbl, ga, gal, gdb, gs = vjp((do, dstate[heads].reshape(nh, d, d)))
        dstate[heads] = gs.reshape(GDN_HB, GDN_REP, d, d)
        dq_ref[...] = jnp.concatenate([gq[GDN_REP * hh] + gq[GDN_REP * hh + 1] for hh in range(GDN_HB)], axis=1)
        dk_ref[...] = jnp.concatenate([gk[GDN_REP * hh] + gk[GDN_REP * hh + 1] for hh in range(GDN_HB)], axis=1)
        dv_ref[...] = jnp.concatenate([gv[b] for b in range(nh)], axis=1)
        lane = lax.broadcasted_iota(jnp.int32, (c, 2 * GDN_V_HEADS), 1)
        lane1 = lax.broadcasted_iota(jnp.int32, (1, GDN_V_HEADS), 1)
        dba = jnp.zeros((c, 2 * GDN_V_HEADS), F32)
        dal = jnp.zeros((1, GDN_V_HEADS), F32)
        ddb = jnp.zeros((1, GDN_V_HEADS), F32)
        for b in range(nh):
            hv = j * nh + b
            dba = dba + jnp.where(lane == hv, gbl[b], 0.0) + jnp.where(lane == GDN_V_HEADS + hv, ga[b], 0.0)
            dal = dal + jnp.where(lane1 == hv, gal[b], 0.0)
            ddb = ddb + jnp.where(lane1 == hv, gdb[b], 0.0)
        dba_ref[...] += dba
        dal_ref[...] += dal
        ddb_ref[...] += ddb

    small = pl.BlockSpec((1, GDN_V_HEADS), lambda n, j: (0, 0))
    return pl.pallas_call(
        body, name=name, grid=(nchunk, GDN_K_HEADS // GDN_HB),
        in_specs=specs + [pl.BlockSpec((None, GDN_HB, GDN_REP, d, d), lambda n, j: (at(n), j, 0, 0, 0)),
                          pl.BlockSpec((c, GDN_HB * GDN_REP * d), lambda n, j: (at(n), j))],
        out_specs=[pl.BlockSpec((c, GDN_HB * d), lambda n, j: (at(n), j)),
                   pl.BlockSpec((c, GDN_HB * d), lambda n, j: (at(n), j)),
                   pl.BlockSpec((c, GDN_HB * GDN_REP * d), lambda n, j: (at(n), j)),
                   pl.BlockSpec((c, 2 * GDN_V_HEADS), lambda n, j: (at(n), 0)),
                   small, small],
        out_shape=[jax.ShapeDtypeStruct((t, GDN_KW), F32), jax.ShapeDtypeStruct((t, GDN_KW), F32),
                   jax.ShapeDtypeStruct((t, GDN_VW), F32), jax.ShapeDtypeStruct((t, 2 * GDN_V_HEADS), F32),
                   jax.ShapeDtypeStruct((1, GDN_V_HEADS), F32), jax.ShapeDtypeStruct((1, GDN_V_HEADS), F32)],
        scratch_shapes=[pltpu.VMEM((GDN_K_HEADS, GDN_REP, d, d), F32)],
        compiler_params=_params("arbitrary", "arbitrary"),
    )(act, act, act, ba, alog, dtb, states, dout)


N_DEV = 8
ANY = pl.BlockSpec(memory_space=pl.ANY)


def _coords():
    return lax.axis_index("x"), lax.axis_index("y"), lax.axis_index("c")


def _other_chips(x, y):
    return [(1 - x, y), (x, 1 - y), (1 - x, 1 - y)]


def _remote(src, dst, send_sems, recv_sems, k, to):
    return pltpu.make_async_remote_copy(src_ref=src, dst_ref=dst, send_sem=send_sems.at[k], recv_sem=recv_sems.at[k],
                                        device_id=to, device_id_type=MESH)


def _dma_sems(n):
    return [pltpu.SemaphoreType.DMA((n,)), pltpu.SemaphoreType.DMA((n,))]


def _gather_copies(ins, outs, send_sems, recv_sems, local_sems, only_first=False):
    x, y, c = _coords()
    sibling = (x, y, 1 - c)
    local, sends, arrivals, relays, relayed = [], [], [], [], []
    for a, (x_ref, out_ref) in enumerate(zip(ins, outs)):
        local.append(pltpu.make_async_copy(x_ref, out_ref.at[2 * x + y], local_sems.at[a]))
        for j, (cx, cy) in enumerate(_other_chips(x, y)):
            k, theirs = 6 * a + j, 2 * cx + cy
            sends.append(_remote(x_ref.at[c], out_ref.at[2 * x + y, c], send_sems, recv_sems, k, (cx, cy, c)))
            if only_first:
                continue
            arrivals.append(_remote(x_ref.at[c], out_ref.at[theirs, c], send_sems, recv_sems, k, (cx, cy, c)))
            relays.append(_remote(out_ref.at[theirs, c], out_ref.at[theirs, c], send_sems, recv_sems, k + 3, sibling))
            relayed.append(_remote(x_ref.at[c], out_ref.at[theirs, 1 - c], send_sems, recv_sems, k + 3, sibling))
    return local, sends, arrivals, relays, relayed


def _gather_start(copies):
    local, sends, _, _, _ = copies
    for cp in local + sends:
        cp.start()


def _gather_finish(copies):
    local, sends, arrivals, relays, relayed = copies
    for landed, relay in zip(arrivals, relays):
        landed.wait_recv()
        relay.start()
    for cp in relayed:
        cp.wait_recv()
    for cp in sends + relays:
        cp.wait_send()
    for cp in local:
        cp.wait()


def _gather_scratch(na):
    return _dma_sems(6 * na) + [pltpu.SemaphoreType.DMA((na,))]


def _gather_quarters(parts, *, name):
    na = len(parts)

    def body(*refs):
        copies = _gather_copies(refs[:na], refs[na:2 * na], *refs[2 * na:])
        _gather_start(copies)
        _gather_finish(copies)

    return pl.pallas_call(
        body, name=name, in_specs=[ANY] * na, out_specs=[ANY] * na,
        out_shape=[jax.ShapeDtypeStruct((N_CHIPS,) + p.shape, p.dtype) for p in parts],
        scratch_shapes=_gather_scratch(na),
    )(*parts)


def _swap_halves(grads, *, name):
    na = len(grads)

    def body(*refs):
        ins, outs = refs[:na], refs[na:2 * na]
        send_sems, recv_sems = refs[2 * na:]
        x, y, c = _coords()
        sends = [_remote(g_ref.at[j, 1 - c], o_ref.at[j], send_sems, recv_sems, N_CHIPS * a + j, (x, y, 1 - c))
                 for a, (g_ref, o_ref) in enumerate(zip(ins, outs)) for j in range(N_CHIPS)]
        for cp in sends:
            cp.start()
        for cp in sends:
            cp.wait()

    return pl.pallas_call(
        body, name=name, in_specs=[ANY] * na, out_specs=[ANY] * na,
        out_shape=[jax.ShapeDtypeStruct((N_CHIPS,) + g.shape[2:], g.dtype) for g in grads],
        scratch_shapes=_dma_sems(N_CHIPS * na),
    )(*grads)


def _scatter_copies(ins, outs, send_sems, recv_sems):
    x, y, c = _coords()
    return [_remote(p_ref.at[2 * cx + cy], outs[3 * a + j], send_sems, recv_sems, 3 * a + j, (cx, cy, c))
            for a, p_ref in enumerate(ins) for j, (cx, cy) in enumerate(_other_chips(x, y))]


def _scatter_shapes(pairs):
    return [jax.ShapeDtypeStruct(p.shape[1:], p.dtype) for p in pairs for _ in range(3)]


def _scatter_quarters(pairs, *, name):
    na = len(pairs)

    def body(*refs):
        sends = _scatter_copies(refs[:na], refs[na:4 * na], *refs[4 * na:])
        for cp in sends:
            cp.start()
        for cp in sends:
            cp.wait()

    out = pl.pallas_call(
        body, name=name, in_specs=[ANY] * na, out_specs=[ANY] * (3 * na), out_shape=_scatter_shapes(pairs),
        scratch_shapes=_dma_sems(3 * na),
    )(*pairs)
    return [out[3 * a:3 * a + 3] for a in range(na)]


def _share_halves(tots, *, name):
    na = len(tots)

    def body(*refs):
        ins, outs = refs[:na], refs[na:2 * na]
        send_sems, recv_sems = refs[2 * na:]
        x, y, c = _coords()
        sends = [_remote(t_ref, o_ref, send_sems, recv_sems, a, (x, y, 1 - c))
                 for a, (t_ref, o_ref) in enumerate(zip(ins, outs))]
        for cp in sends:
            cp.start()
        for cp in sends:
            cp.wait()

    return pl.pallas_call(
        body, name=name, in_specs=[ANY] * na, out_specs=[ANY] * na,
        out_shape=[jax.ShapeDtypeStruct(t.shape, t.dtype) for t in tots],
        scratch_shapes=_dma_sems(na),
    )(*tots)


def _gather_all(vec, *, name):
    m, w = vec.shape

    def body(x_ref, out_ref, send_sems, recv_sems, local_sem):
        x, y, c = _coords()
        me, sibling = (x, y, c), (x, y, 1 - c)
        chips = _other_chips(x, y)

        def rows(px, py, pc):
            return out_ref.at[pl.ds((4 * px + 2 * py + pc) * m, m), :]

        def copy(k, block, to, src=None):
            return _remote(rows(*block) if src is None else src, rows(*block), send_sems, recv_sems, k, to)

        mine = pltpu.make_async_copy(x_ref, rows(*me), local_sem)
        mine.start()
        first = [copy(0, me, sibling, src=x_ref)]
        first += [copy(1 + j, me, (*chip, c), src=x_ref) for j, chip in enumerate(chips)]
        for cp in first:
            cp.start()
        passed = [copy(4 + j, (*chip, c), sibling) for j, chip in enumerate(chips)]
        for j, chip in enumerate(chips):
            copy(1 + j, (*chip, c), me).wait_recv()
            passed[j].start()
        copy(0, sibling, me).wait_recv()
        for j, chip in enumerate(chips):
            copy(4 + j, (*chip, 1 - c), me).wait_recv()
        for cp in first + passed:
            cp.wait_send()
        mine.wait()

    vm = pl.BlockSpec(memory_space=pltpu.VMEM)
    return pl.pallas_call(
        body, name=name, in_specs=[vm], out_specs=vm, out_shape=jax.ShapeDtypeStruct((N_DEV * m, w), vec.dtype),
        scratch_shapes=_dma_sems(7) + [pltpu.SemaphoreType.DMA(())],
    )(vec)


def _sum_blocks(allv, n, *, name):
    m = allv.shape[0] // n

    def body(a_ref, o_ref):
        acc = a_ref[0:m, :]
        for d in range(1, n):
            acc = acc + a_ref[d * m:(d + 1) * m, :]
        o_ref[...] = acc

    return pl.pallas_call(body, name=name, out_shape=jax.ShapeDtypeStruct((m, allv.shape[1]), allv.dtype))(allv)


EW_BLOCK_BYTES = 1 << 20


def _ew_rows(rows, w):
    return _tile(rows, max(8, (EW_BLOCK_BYTES // (4 * w)) // 8 * 8), 8)


def _add_pair(g, got, c, *, name):
    _, _, rows, w = g.shape
    tr = _ew_rows(rows, w)

    def body(c_ref, g_ref, got_ref, o_ref):
        o_ref[...] = (g_ref[...] + got_ref[...]).astype(o_ref.dtype)

    blk = pl.BlockSpec((None, tr, w), lambda q, i, c_ref: (q, i, 0))
    return pl.pallas_call(
        body, name=name,
        grid_spec=pltpu.PrefetchScalarGridSpec(
            num_scalar_prefetch=1, grid=(N_CHIPS, rows // tr),
            in_specs=[pl.BlockSpec((None, None, tr, w), lambda q, i, c_ref: (q, c_ref[0], i, 0)), blk], out_specs=blk),
        out_shape=jax.ShapeDtypeStruct(got.shape, BF16),
        compiler_params=_params("parallel", "parallel"),
    )(c, g, got)


def _add_chips(pair, recv, chip, *, name):
    _, rows, w = pair.shape
    tr = _ew_rows(rows, w)

    def body(chip_ref, p_ref, r0_ref, r1_ref, r2_ref, o_ref):
        f = lambda r: r[...].astype(F32)
        o_ref[...] = ((f(p_ref) + f(r0_ref)) + f(r1_ref)) + f(r2_ref)

    blk = pl.BlockSpec((tr, w), lambda i, chip_ref: (i, 0))
    return pl.pallas_call(
        body, name=name,
        grid_spec=pltpu.PrefetchScalarGridSpec(
            num_scalar_prefetch=1, grid=(rows // tr,),
            in_specs=[pl.BlockSpec((None, tr, w), lambda i, chip_ref: (chip_ref[0], i, 0)), blk, blk, blk], out_specs=blk),
        out_shape=jax.ShapeDtypeStruct((rows, w), F32),
        compiler_params=_params("parallel"),
    )(chip, pair, *recv)


def _adamw_math(w, g, m, v):
    nm = ADAM_B1 * m + (1.0 - ADAM_B1) * g
    nv = ADAM_B2 * v + (1.0 - ADAM_B2) * (g * g)
    m_hat = nm / (1.0 - ADAM_B1 ** ADAM_STEP)
    v_hat = nv / (1.0 - ADAM_B2 ** ADAM_STEP)
    return -ADAM_LR * (m_hat / (jnp.sqrt(v_hat) + ADAM_EPS) + ADAM_WD * w), nm, nv


def _adamw(w, g, m, v, *, name):
    shape = w.shape
    last = shape[-1]
    w2, g2, m2, v2 = (a.reshape(-1, last) for a in (w, g, m, v))
    rows = w2.shape[0]
    tm = _ew_rows(rows, last)

    def body(w_ref, g_ref, m_ref, v_ref, d_ref, nm_ref, nv_ref):
        d_ref[...], nm_ref[...], nv_ref[...] = _adamw_math(w_ref[...], g_ref[...], m_ref[...], v_ref[...])

    spec = pl.BlockSpec((tm, last), lambda i: (i, 0))
    out = jax.ShapeDtypeStruct((rows, last), F32)
    d, nm, nv = pl.pallas_call(
        body, name=name, grid=(rows // tm,), in_specs=[spec] * 4, out_specs=[spec] * 3, out_shape=[out] * 3,
        compiler_params=_params("parallel"),
    )(w2, g2, m2, v2)
    return d.reshape(shape), nm.reshape(shape), nv.reshape(shape)


def _adamw_halves(w, m, v, mine, theirs, c, *, name, into=None):
    rows, wd = w.shape[-2:]
    tr = _ew_rows(rows, wd)
    bufs, at = into if into is not None else ((), ())

    def body(c_ref, w_ref, m_ref, v_ref, a_ref, b_ref, *rest):
        g_ref, d_ref, nm_ref, nv_ref = rest[len(bufs):]
        g = jnp.where(pl.program_id(0) == c_ref[0], a_ref[...], b_ref[...])
        g_ref[...] = g
        d_ref[...], nm_ref[...], nv_ref[...] = _adamw_math(w_ref[...], g, m_ref[...], v_ref[...])

    full = pl.BlockSpec((None,) * (1 + len(at)) + (tr, wd), lambda hf, i, c_ref: at + (hf, i, 0))
    half = pl.BlockSpec((tr, wd), lambda hf, i, c_ref: (i, 0))
    out = jax.ShapeDtypeStruct(w.shape, F32)
    return pl.pallas_call(
        body, name=name,
        grid_spec=pltpu.PrefetchScalarGridSpec(num_scalar_prefetch=1, grid=(2, rows // tr),
                                               in_specs=[full] * 3 + [half] * 2 + [ANY] * len(bufs),
                                               out_specs=[full] * 4),
        out_shape=[out] * 4, input_output_aliases={6 + b: b for b in range(len(bufs))},
        compiler_params=_params("parallel", "parallel"),
    )(c, w, m, v, mine, theirs, *bufs)


def _join_quarters(q, *, name):
    _, rows, n = q.shape
    tr = _tile(rows, 256, 16)

    def body(q_ref, o_ref):
        o_ref[...] = jnp.concatenate([q_ref[s] for s in range(N_CHIPS)], axis=1)

    return pl.pallas_call(
        body, name=name, grid=(rows // tr,),
        in_specs=[pl.BlockSpec((N_CHIPS, tr, n), lambda i: (0, i, 0))],
        out_specs=pl.BlockSpec((tr, N_CHIPS * n), lambda i: (i, 0)),
        out_shape=jax.ShapeDtypeStruct((rows, N_CHIPS * n), q.dtype),
        compiler_params=_params("parallel"),
    )(q)


def _split_quarters(full, *, name):
    rows, n4 = full.shape
    n = n4 // N_CHIPS
    tr = _tile(rows, 256, 16)

    def body(x_ref, o_ref):
        x = x_ref[...]
        for s in range(N_CHIPS):
            o_ref[s] = x[:, s * n:(s + 1) * n]

    return pl.pallas_call(
        body, name=name, grid=(rows // tr,),
        in_specs=[pl.BlockSpec((tr, n4), lambda i: (i, 0))],
        out_specs=pl.BlockSpec((N_CHIPS, tr, n), lambda i: (0, i, 0)),
        out_shape=jax.ShapeDtypeStruct((N_CHIPS, rows, n), full.dtype),
        compiler_params=_params("parallel"),
    )(full)


_WEIGHTS = ['ffn_norm', 'ffn_w_gate', 'ffn_w_up', 'ffn_w_down', 'mix_norm', 'att_w_in', 'att_q_norm', 'att_k_norm',
            'att_sinks', 'att_w_out', 'gdn_w_in', 'gdn_conv_w', 'gdn_a_log', 'gdn_dt_bias', 'gdn_out_norm', 'gdn_w_out',
            'ple_norm', 'ple_w_gate', 'ple_w_proj']
_BIG = ['ffn_w_gate', 'ffn_w_up', 'ffn_w_down', 'att_w_in', 'att_w_out', 'gdn_w_in', 'gdn_w_out', 'ple_w_gate',
        'ple_w_proj']
_SMALL_CUT = {'ffn_norm': 2, 'gdn_conv_w': 2}
_WHOLE = ['mix_norm', 'att_q_norm', 'att_k_norm', 'att_sinks', 'gdn_a_log', 'gdn_dt_bias', 'gdn_out_norm', 'ple_norm']
PACK_W = 1024
SMALL_ROW_MULT = 8


def _halves(a):
    return a.reshape(2, -1, a.shape[-1])


def _from_quarters(blk, axis):
    full = jnp.moveaxis(blk, 0, axis)
    shp = list(full.shape)
    shp[axis:axis + 2] = [shp[axis] * shp[axis + 1]]
    return full.reshape(shp)


def _to_quarters(full, axis):
    shp = list(full.shape)
    shp[axis:axis + 1] = [N_CHIPS, shp[axis] // N_CHIPS]
    return jnp.moveaxis(full.reshape(shp), axis, 0)


def _pack(parts, row_mult):
    flat = jnp.concatenate(parts, axis=-1)
    n = flat.shape[-1]
    rows = -(-n // (PACK_W * row_mult)) * row_mult
    return jnp.pad(flat, [(0, rows * PACK_W - n)]).reshape(rows, PACK_W)


def _unpack(flat, shapes):
    lead = flat.shape[:-2]
    flat = flat.reshape(lead + (-1,))
    out, off = [], 0
    for shp in shapes:
        n = math.prod(shp)
        out.append(flat[..., off:off + n].reshape(lead + tuple(shp)))
        off += n
    return out


FFN_TM = 1024


def _ffn_up(hn, wg, wu, at, *, name):
    t, d = hn.shape
    fq = wg.shape[-1]
    tm = _tile(t, FFN_TM)

    def body(h_ref, wg_ref, wu_ref, g_ref, u_ref, a_ref):
        h = h_ref[...]
        g, u = _dg(h, _b(wg_ref[...]), 1, 0), _dg(h, _b(wu_ref[...]), 1, 0)
        g_ref[...] = g.astype(BF16)
        u_ref[...] = u.astype(BF16)
        a_ref[...] = _f_swiglu(g, u)[0].astype(BF16)

    w_spec = pl.BlockSpec((None,) * (1 + len(at)) + (d, fq), lambda s, i: (s,) + at + (0, 0))
    o_spec = pl.BlockSpec((None, tm, fq), lambda s, i: (s, i, 0))
    out = jax.ShapeDtypeStruct((N_CHIPS, t, fq), BF16)
    return pl.pallas_call(
        body, name=name, grid=(N_CHIPS, t // tm),
        in_specs=[pl.BlockSpec((tm, d), lambda s, i: (i, 0)), w_spec, w_spec], out_specs=[o_spec] * 3,
        out_shape=[out] * 3, compiler_params=_params("parallel", "parallel"),
    )(hn, wg, wu)


def _ffn_d_up(dout, wd, g, u, at, *, name):
    t, d = dout.shape
    fq = wd.shape[-2]
    tm = _tile(t, FFN_TM)

    def body(do_ref, wd_ref, g_ref, u_ref, dg_ref, du_ref):
        da = _dg(_b(do_ref[...]), _b(wd_ref[...]), 1, 1) * 0.5
        _, vjp = jax.vjp(_f_swiglu, g_ref[...].astype(F32), u_ref[...].astype(F32))
        dg, du = vjp((da,))
        dg_ref[...] = dg.astype(BF16)
        du_ref[...] = du.astype(BF16)

    w_spec = pl.BlockSpec((None,) * (1 + len(at)) + (fq, d), lambda s, i: (s,) + at + (0, 0))
    o_spec = pl.BlockSpec((None, tm, fq), lambda s, i: (s, i, 0))
    out = jax.ShapeDtypeStruct((N_CHIPS, t, fq), BF16)
    return pl.pallas_call(
        body, name=name, grid=(N_CHIPS, t // tm),
        in_specs=[pl.BlockSpec((tm, d), lambda s, i: (i, 0)), w_spec, o_spec, o_spec], out_specs=[o_spec] * 2,
        out_shape=[out] * 2, compiler_params=_params("parallel", "parallel"),
    )(dout, wd, g, u)


def _ffn_fwd(h, gain, wg, wu, wd, at, tag):
    lead = (Q,) + at
    hn, = _row_fwd(_f_rms, [h], [gain], [(D_MODEL, BF16)], name=f"{tag}_norm")
    g, u, a = _ffn_up(hn, wg, wu, at, name=f"{tag}_up")
    out = _mm((a, (Q,)), (wd, lead), res=h, scale=0.5, name=f"{tag}_down")
    return out, (h, hn, g, u, a)


def _ffn_bwd(dout, saved, gain, wg, wu, wd, at, grads, g_at, tag):
    h, hn, g, u, a = saved
    lead = (Q,) + at
    g_lead = (Q,) + g_at
    dg, du = _ffn_d_up(dout, wd, g, u, at, name=f"{tag}_d_up")
    g_gate, g_up, g_down = grads
    g_down = _mm((a, (Q,)), dout, ta=True, scale=0.5, into=(g_down, g_lead), name=f"{tag}_dw_down")
    g_gate = _mm(hn, (dg, (Q,)), ta=True, into=(g_gate, g_lead), name=f"{tag}_dw_gate")
    g_up = _mm(hn, (du, (Q,)), ta=True, into=(g_up, g_lead), name=f"{tag}_dw_up")
    dhn = _mm((dg, (Q,)), (wg, lead), tb=True, name=f"{tag}_d_norm_gate")
    dhn = _mm((du, (Q,)), (wu, lead), tb=True, res=dhn, name=f"{tag}_d_norm_up")
    dh, dgain = _row_bwd(_f_rms_res, [h], [gain], [dhn, dout], [(0, F32)], [0], name=f"{tag}_d_in")
    return dh, dgain, (g_gate, g_up, g_down)


def _att_fwd(h, gain, w_in, qg, kg, sinks, w_out, gather):
    hn, = _row_fwd(_f_rms, [h], [gain], [(D_MODEL, BF16)], name="att_norm")
    proj = _mm(hn, w_in, name="att_in")
    a, rtot, *gathered = _sb_fwd(proj, name="att_sb", gather=gather)
    b = _swa_fwd(proj, qg, kg, sinks, name="att_swa")
    out = _mm(a, (w_out, (0,)), res=h, name="att_out_sb")
    out = _mm(b, (w_out, (1,)), res=out, name="att_out_swa")
    return out, (h, hn, proj, a, rtot, b), gathered


def _att_bwd(dout, saved, gain, w_in, qg, kg, sinks, w_out, scatter):
    h, hn, proj, a, rtot, b = saved
    da = _mm(dout, (w_out, (0,)), tb=True, name="att_d_sb")
    db = _mm(dout, (w_out, (1,)), tb=True, name="att_d_swa")
    dw_out = lax.empty(w_out.shape, F32)
    dw_out = _mm(a, dout, ta=True, into=(dw_out, (0,)), name="att_dw_out_sb")
    dw_out = _mm(b, dout, ta=True, into=(dw_out, (1,)), name="att_dw_out_swa")
    dq, dk, dv, *landed = _sb_bwd(proj, rtot, da, name="att_sb_bwd", scatter=scatter)
    dqb, dkb, dvb, dqg, dkg, dsk = _swa_bwd(proj, qg, kg, sinks, db, name="att_swa_bwd")
    dproj = jnp.concatenate([dq, dk, dv, dqb, dkb, dvb], axis=1)
    dw_in = _mm(hn, dproj, ta=True, name="att_dw_in")
    dhn = _mm(dproj, w_in, tb=True, name="att_d_norm")
    dh, dgain = _row_bwd(_f_rms_res, [h], [gain], [dhn, dout], [(0, F32)], [0], name="att_d_in")
    return dh, dgain, dw_in, dqg, dkg, dsk, dw_out, [landed[3 * a:3 * a + 3] for a in range(len(scatter))]


def _gdn_layer_fwd(h, gain, w_in, conv_w, alog, dtb, out_gain, w_out):
    w_qkv, w_z, w_ba = w_in[:, :GDN_CONV_W], w_in[:, GDN_CONV_W:GDN_CONV_W + GDN_VW], w_in[:, GDN_CONV_W + GDN_VW:]
    hn, = _row_fwd(_f_rms, [h], [gain], [(D_MODEL, BF16)], name="gdn_norm")
    pq = _mm(hn, w_qkv, name="gdn_in_qkv")
    pz = _mm(hn, w_z, name="gdn_in_z")
    ba = _mm(hn, w_ba, name="gdn_in_ba")
    act = _conv_fwd(pq, conv_w, name="gdn_conv")
    o, states = _gdn_fwd(act, ba, alog, dtb, name="gdn_rule")
    y, = _row_fwd(_f_gdn_out, [o, pz], [out_gain], [(GDN_VW, BF16)], name="gdn_gate")
    out = _mm(y, w_out, res=h, name="gdn_out")
    return out, (h, hn, pq, pz, ba, act, o, states, y, (w_qkv, w_z, w_ba))


def _gdn_layer_bwd(dout, saved, gain, conv_w, alog, dtb, out_gain, w_out):
    h, hn, pq, pz, ba, act, o, states, y, (w_qkv, w_z, w_ba) = saved
    dy = _mm(dout, w_out, tb=True, name="gdn_d_gate")
    dw_out = _mm(y, dout, ta=True, name="gdn_dw_out")
    do, dpz, dout_gain = _row_bwd(_f_gdn_out, [o, pz], [out_gain], [dy], [(0, F32), (1, F32)], [0], name="gdn_gate_bwd")
    dq, dk, dv, dba, dal, ddb = _gdn_bwd(act, ba, alog, dtb, states, do, name="gdn_rule_bwd")
    dact = jnp.concatenate([dq, dk, dv], axis=1)
    dc = _conv_fwd(pq, conv_w, dact, name="gdn_conv_d_pre")
    dpq, dconv = _conv_bwd(pq, conv_w, dc, name="gdn_conv_bwd")
    dw_in = jnp.concatenate([_mm(hn, dpq, ta=True, name="gdn_dw_qkv"), _mm(hn, dpz, ta=True, name="gdn_dw_z"),
                             _mm(hn, dba, ta=True, name="gdn_dw_ba")], axis=1)
    dhn = _mm(dpq, w_qkv, tb=True, name="gdn_d_norm_qkv")
    dhn = _mm(dpz, w_z, tb=True, res=dhn, name="gdn_d_norm_z")
    dhn = _mm(dba, w_ba, tb=True, res=dhn, name="gdn_d_norm_ba")
    dh, dgain = _row_bwd(_f_rms_res, [h], [gain], [dhn, dout], [(0, F32)], [0], name="gdn_d_in")
    return dh, dgain, dw_in, dconv, dal, ddb, dout_gain, dw_out


def _ple_fwd(h, gain, w_gate, w_proj, pe, tag):
    hn, = _row_fwd(_f_rms, [h], [gain], [(D_MODEL, BF16)], name=f"{tag}_norm")
    gl = _mm(hn, w_gate, name=f"{tag}_gate")
    pp = _mm(pe, w_proj, name=f"{tag}_proj")
    out, = _row_fwd(_f_ple, [h, gl, pp], [], [(D_MODEL, F32)], name=f"{tag}_mix")
    return out, (h, hn, gl, pp)


def _ple_bwd(dout, saved, gain, w_gate, pe, tag):
    h, hn, gl, pp = saved
    dha, dgl, dpp = _row_bwd(_f_ple, [h, gl, pp], [], [dout], [(0, F32), (1, BF16), (2, BF16)], [], name=f"{tag}_mix_bwd")
    dw_gate = _mm(hn, dgl, ta=True, name=f"{tag}_dw_gate")
    dw_proj = _mm(pe, dpp, ta=True, name=f"{tag}_dw_proj")
    dhn = _mm(dgl, w_gate, tb=True, name=f"{tag}_d_norm")
    dh, dgain = _row_bwd(_f_rms_res, [h], [gain], [dhn, dha], [(0, F32)], [0], name=f"{tag}_d_in")
    return dh, dgain, dw_gate, dw_proj


def kernel(x, p, ffn_norm, ffn_w_gate, ffn_w_up, ffn_w_down, mix_norm, att_w_in, att_q_norm, att_k_norm, att_sinks, att_w_out, gdn_w_in, gdn_conv_w, gdn_a_log, gdn_dt_bias, gdn_out_norm, gdn_w_out, ple_norm, ple_w_gate, ple_w_proj, loss_target, m_ffn_norm, m_ffn_w_gate, m_ffn_w_up, m_ffn_w_down, m_mix_norm, m_att_w_in, m_att_q_norm, m_att_k_norm, m_att_sinks, m_att_w_out, m_gdn_w_in, m_gdn_conv_w, m_gdn_a_log, m_gdn_dt_bias, m_gdn_out_norm, m_gdn_w_out, m_ple_norm, m_ple_w_gate, m_ple_w_proj, v_ffn_norm, v_ffn_w_gate, v_ffn_w_up, v_ffn_w_down, v_mix_norm, v_att_w_in, v_att_q_norm, v_att_k_norm, v_att_sinks, v_att_w_out, v_gdn_w_in, v_gdn_conv_w, v_gdn_a_log, v_gdn_dt_bias, v_gdn_out_norm, v_gdn_w_out, v_ple_norm, v_ple_w_gate, v_ple_w_proj):
    arg = dict(locals())
    cx, cy, cc = _coords()
    chip = (2 * cx + cy).astype(jnp.int32).reshape(1)
    core = cc.astype(jnp.int32).reshape(1)
    n_layers = ffn_norm.shape[0]

    quarter = lambda n, i=None: _halves((arg[n] if i is None else arg[n][i]).astype(BF16))
    ffn_names = ('ffn_w_gate', 'ffn_w_up', 'ffn_w_down')
    early = [quarter(n, 0) for n in ffn_names] + [quarter('att_w_in'), quarter('att_w_out')]
    late_names = ('gdn_w_in', 'gdn_w_out', 'ple_w_gate', 'ple_w_proj')
    late = [quarter(n, 1) for n in ffn_names] + [quarter(n) for n in late_names]
    *ffn_w0, att_in_q, att_out_q = _gather_quarters(early, name="gather_weights")
    wt = {'att_w_in': _join_quarters(att_in_q.reshape((N_CHIPS,) + att_w_in.shape[1:]), name="att_w_in_join"),
          'att_w_out': att_out_q.reshape(2, SB_W, D_MODEL)}

    small_names = list(_SMALL_CUT)
    small_shapes = [arg[n].shape for n in small_names]
    svec = _pack([arg[n].reshape(-1) for n in small_names], SMALL_ROW_MULT)
    srows = svec.shape[0]
    sall = _gather_all(svec, name="gather_gains").reshape(N_CHIPS, 2, srows, PACK_W)[:, 0]
    for n, q in zip(small_names, _unpack(sall, small_shapes)):
        wt[n] = _from_quarters(q, _SMALL_CUT[n])
    row = lambda v: v.reshape(1, -1)

    as_ffn = lambda g, n: g.reshape((N_CHIPS,) + arg[n].shape[1:])
    ffn_w = [tuple(as_ffn(g, n) for g, n in zip(ffn_w0, ffn_names)), None]
    h = x[0]
    tape = []
    for i in range(n_layers):
        j = i // 2
        h, s0 = _ffn_fwd(h, row(wt['ffn_norm'][i, 0]), *ffn_w[i], (0,), f"ffn{i}a")
        if i % 2 == 0:
            h, sm, gathered = _att_fwd(h, row(mix_norm[i]), wt['att_w_in'], att_q_norm[j:j + 1], att_k_norm[j:j + 1],
                                       att_sinks[j:j + 1], wt['att_w_out'], late)
            ffn_w[1] = tuple(as_ffn(g, n) for g, n in zip(gathered[:3], ffn_names))
            wq = {n: g.reshape((N_CHIPS,) + arg[n].shape) for n, g in zip(late_names, gathered[3:])}
            wt['gdn_w_in'] = _join_quarters(wq['gdn_w_in'][:, 0], name="gdn_w_in_join")
            wt['gdn_w_out'] = wq['gdn_w_out'].reshape(GDN_VW, D_MODEL)
            wt['ple_w_gate'] = _from_quarters(wq['ple_w_gate'], 1)
            wt['ple_w_proj'] = _from_quarters(wq['ple_w_proj'], 2)
        else:
            h, sm = _gdn_layer_fwd(h, row(mix_norm[i]), wt['gdn_w_in'], wt['gdn_conv_w'][j], gdn_a_log[j:j + 1],
                                   gdn_dt_bias[j:j + 1], gdn_out_norm[j:j + 1], wt['gdn_w_out'])
        h, s1 = _ffn_fwd(h, row(wt['ffn_norm'][i, 1]), *ffn_w[i], (1,), f"ffn{i}b")
        h, sp = _ple_fwd(h, row(ple_norm[i]), wt['ple_w_gate'][i], wt['ple_w_proj'][i], p[i, 0], f"ple{i}")
        tape.append((s0, sm, s1, sp))

    dh, loss_local = _loss_head(h, loss_target[0], name="loss_head")
    loss = lax.psum(loss_local, ("x", "y", "c"))

    gr = {}
    ffn_g = [tuple(lax.empty((N_CHIPS,) + arg[n].shape[1:], F32) for n in ffn_names) for _ in range(n_layers)]
    d_ffn_norm = [[None, None] for _ in range(n_layers)]
    d_mix, d_ple_norm, d_ple_gate, d_ple_proj = [None] * n_layers, [None] * n_layers, [None] * n_layers, [None] * n_layers

    def as_halves(g):
        return g.reshape((N_CHIPS, 2, -1, g.shape[-1]))

    def pair_up(keys, grads, tag):
        got = _swap_halves(grads, name=f"grad_swap_halves_{tag}")
        return [_add_pair(g, o, core, name=f"grad_add_pair_{k}") for k, g, o in zip(keys, grads, got)]

    for i in reversed(range(n_layers)):
        j = i // 2
        s0, sm, s1, sp = tape[i]
        dh, d_ple_norm[i], d_ple_gate[i], d_ple_proj[i] = _ple_bwd(dh, sp, row(ple_norm[i]), wt['ple_w_gate'][i], p[i, 0],
                                                                   f"ple{i}")
        dh, d_ffn_norm[i][1], ffn_g[i] = _ffn_bwd(dh, s1, row(wt['ffn_norm'][i, 1]), *ffn_w[i], (1,), ffn_g[i], (1,),
                                                  f"ffn{i}b")
        if i % 2 == 0:
            gr['ple_w_gate'] = _to_quarters(jnp.stack(d_ple_gate), 1)
            gr['ple_w_proj'] = _to_quarters(jnp.stack(d_ple_proj), 2)
            first_keys = [f"{n}_1" for n in ffn_names] + list(late_names)
            first_pairs = pair_up(first_keys, [as_halves(g) for g in ffn_g[1]] + [as_halves(gr[n]) for n in late_names], "a")
            (dh, d_mix[i], dw_in, gr['att_q_norm'], gr['att_k_norm'], gr['att_sinks'], dw_out,
             first_recv) = _att_bwd(dh, sm, row(mix_norm[i]), wt['att_w_in'], att_q_norm[j:j + 1],
                                    att_k_norm[j:j + 1], att_sinks[j:j + 1], wt['att_w_out'], first_pairs)
            gr['att_w_in'] = _split_quarters(dw_in, name="att_dw_in_split")
            gr['att_w_out'] = dw_out
        else:
            (dh, d_mix[i], dw_in, dconv, gr['gdn_a_log'], gr['gdn_dt_bias'], gr['gdn_out_norm'],
             dw_out) = _gdn_layer_bwd(dh, sm, row(mix_norm[i]), wt['gdn_conv_w'][j], gdn_a_log[j:j + 1],
                                      gdn_dt_bias[j:j + 1], gdn_out_norm[j:j + 1], wt['gdn_w_out'])
            gr['gdn_w_in'] = _split_quarters(dw_in, name="gdn_dw_in_split")
            gr['gdn_w_out'] = dw_out
            gr['gdn_conv_w'] = dconv[None]
        dh, d_ffn_norm[i][0], ffn_g[i] = _ffn_bwd(dh, s0, row(wt['ffn_norm'][i, 0]), *ffn_w[i], (0,), ffn_g[i], (0,),
                                                  f"ffn{i}a")
    grad_x = dh[None]

    gr['ffn_norm'] = jnp.stack([jnp.stack([d_ffn_norm[i][k][0] for k in range(2)]) for i in range(n_layers)])
    gr['mix_norm'] = jnp.concatenate(d_mix, axis=0)
    gr['ple_norm'] = jnp.concatenate(d_ple_norm, axis=0)

    last_keys = [f"{n}_0" for n in ffn_names] + ['att_w_in', 'att_w_out']
    last_pairs = pair_up(last_keys, [as_halves(g) for g in ffn_g[0]] + [as_halves(gr['att_w_in']), as_halves(gr['att_w_out'])],
                         "b")
    last_recv = _scatter_quarters(last_pairs, name="grad_scatter")
    keys = first_keys + last_keys
    tots = [_add_chips(pr, rc, chip, name=f"grad_add_chips_{k}")
            for k, pr, rc in zip(keys, first_pairs + last_pairs, first_recv + last_recv)]
    theirs = _share_halves(tots, name="grad_share")
    summed = dict(zip(keys, zip(tots, theirs)))

    whole_shapes = [arg[n].shape for n in _WHOLE]
    cut_full_shapes = [gr[n].shape for n in small_names]
    gvec = _pack([gr[n].reshape(-1) for n in _WHOLE + small_names], SMALL_ROW_MULT)
    gall = _sum_blocks(_gather_all(gvec, name="gather_small_grads"), N_DEV, name="sum_small_grads")
    parts = _unpack(gall, whole_shapes + cut_full_shapes)
    gsum = dict(zip(_WHOLE, parts))
    for n, g in zip(small_names, parts[len(_WHOLE):]):
        gsum[n] = lax.dynamic_index_in_dim(_to_quarters(g, _SMALL_CUT[n]), chip[0], axis=0, keepdims=False)

    delta, new_m, new_v = {}, {}, {}
    for n in ('att_w_in', 'att_w_out') + late_names:
        res = _adamw_halves(_halves(arg[n]), _halves(arg["m_" + n]), _halves(arg["v_" + n]), *summed[n], core,
                            name=f"adamw_{n}")
        gsum[n], delta[n], new_m[n], new_v[n] = (r.reshape(arg[n].shape) for r in res)
    for n in ffn_names:
        res = tuple(lax.empty(arg[n].shape, F32) for _ in range(4))
        for i in range(n_layers):
            res = _adamw_halves(arg[n], arg["m_" + n], arg["v_" + n], *summed[f"{n}_{i}"], core,
                                name=f"adamw_{n}_{i}", into=(res, (i,)))
        gsum[n], delta[n], new_m[n], new_v[n] = res
    for n in _WHOLE + small_names:
        delta[n], new_m[n], new_v[n] = _adamw(arg[n], gsum[n], arg["m_" + n], arg["v_" + n], name=f"adamw_{n}")
    return (loss, grad_x, *[gsum[n] for n in _WEIGHTS], *[delta[n] for n in _WEIGHTS],
            *[new_m[n] for n in _WEIGHTS], *[new_v[n] for n in _WEIGHTS])
```

```python
import functools
import math

import jax
import jax.numpy as jnp
from jax import lax
from jax.experimental import pallas as pl
from jax.experimental.pallas import tpu as pltpu

F32 = jnp.float32
BF16 = jnp.bfloat16
MESH = pl.DeviceIdType.MESH

LANES = 128
VMEM_LIMIT_BYTES = 56 * 1024 * 1024

EPS = 1e-6
D_MODEL = 1024
HEAD_DIM = 64
SB_HEADS = 8
SWA_HEADS = 8
SWA_KV_HEADS = 2
WINDOW = 128
GDN_K_HEADS = 8
GDN_V_HEADS = 16
GDN_HEAD_DIM = 128
GDN_CONV = 4
GDN_CHUNK = 64
SB_W = SB_HEADS * HEAD_DIM
SWA_QW = SWA_HEADS * HEAD_DIM
SWA_KVW = SWA_KV_HEADS * HEAD_DIM
GDN_KW = GDN_K_HEADS * GDN_HEAD_DIM
GDN_VW = GDN_V_HEADS * GDN_HEAD_DIM
GDN_CONV_W = 2 * GDN_KW + GDN_VW

ADAM_LR = 0.001
ADAM_B1 = 0.9
ADAM_B2 = 0.999
ADAM_EPS = 1e-08
ADAM_WD = 0.01
ADAM_STEP = 10

NEG = -1e30


def _params(*sem):
    return pltpu.CompilerParams(dimension_semantics=sem or None, vmem_limit_bytes=VMEM_LIMIT_BYTES)


def _tile(n, cap, align=LANES):
    if n <= cap:
        return n
    for t in range(cap - cap % align, 0, -align):
        if n % t == 0:
            return t
    return n


N_CHIPS = 4
MM_VMEM_BUDGET_BYTES = 40 * 1024 * 1024
Q = "q"


def _opnd(x):
    return x if isinstance(x, tuple) else (x, ())


def _mm(a, b, *, name, ta=False, tb=False, out_dtype=F32, res=None, scale=1.0, out_q=False, into=None,
        tm=None, tn=1024, tk=1024):
    (a_arr, a_lead), (b_arr, b_lead) = _opnd(a), _opnd(b)
    (k_a, m) = a_arr.shape[-2:] if ta else a_arr.shape[-2:][::-1]
    (n, k_b) = b_arr.shape[-2:] if tb else b_arr.shape[-2:][::-1]
    if into is not None:
        out_arr, out_lead = into
        out_q, out_dtype = Q in out_lead, out_arr.dtype
    else:
        out_lead = (Q,) if out_q else ()
    red_q = (Q in a_lead or Q in b_lead) and not out_q
    kq = min(k_a, k_b)
    assert (k_a == k_b) or (red_q and max(k_a, k_b) == N_CHIPS * kq), (a_arr.shape, b_arr.shape)
    tn, tk = _tile(n, tn), _tile(kq, tk)
    if tm is None:
        r_item = _opnd(res)[0].dtype.itemsize if res is not None else 0
        per_row = 2 * (tk * a_arr.dtype.itemsize + tn * (jnp.dtype(out_dtype).itemsize + r_item)) + 4 * tn
        room = MM_VMEM_BUDGET_BYTES - 2 * tk * tn * b_arr.dtype.itemsize
        tm = next(c for c in (4096, 2048, 1024, 512, 256, 128) if c * per_row <= room or c == 128)
    tm = _tile(m, tm)
    nk = kq // tk
    ksteps = nk * (N_CHIPS if red_q else 1)
    dims = (((0 if ta else 1,), (1 if tb else 0,)), ((), ()))
    has_res = res is not None

    def body(*refs):
        a_ref, b_ref = refs[0], refs[1]
        o_ref, acc_ref = refs[-2], refs[-1]
        k = pl.program_id(3)

        @pl.when(k == 0)
        def _():
            acc_ref[...] = jnp.zeros_like(acc_ref)

        acc_ref[...] += lax.dot_general(a_ref[...].astype(BF16), b_ref[...].astype(BF16), dims,
                                        preferred_element_type=F32)

        @pl.when(k == ksteps - 1)
        def _():
            r = acc_ref[...]
            if scale != 1.0:
                r = r * scale
            if has_res:
                r = r + refs[2][...].astype(F32)
            o_ref[...] = r.astype(o_ref.dtype)

    def spec(lead, blk, pos):
        def index(s, i, j, k):
            kk = k % nk if (red_q and Q in lead) else k
            quarter = s if out_q else k // nk
            return tuple(quarter if l == Q else l for l in lead) + pos(i, j, kk)
        return pl.BlockSpec((None,) * len(lead) + blk, index)

    a_spec = spec(a_lead, (tk, tm), lambda i, j, k: (k, i)) if ta else spec(a_lead, (tm, tk), lambda i, j, k: (i, k))
    b_spec = spec(b_lead, (tn, tk), lambda i, j, k: (j, k)) if tb else spec(b_lead, (tk, tn), lambda i, j, k: (k, j))
    o_spec = spec(out_lead, (tm, tn), lambda i, j, k: (i, j))
    in_specs, args = [a_spec, b_spec], [a_arr, b_arr]
    if has_res:
        r_arr, r_lead = _opnd(res)
        in_specs.append(spec(r_lead, (tm, tn), lambda i, j, k: (i, j)))
        args.append(r_arr)
    aliases = {}
    if into is not None:
        in_specs.append(pl.BlockSpec(memory_space=pl.ANY))
        args.append(out_arr)
        aliases = {len(args) - 1: 0}
        out_shape = jax.ShapeDtypeStruct(out_arr.shape, out_arr.dtype)
    else:
        out_shape = jax.ShapeDtypeStruct(((N_CHIPS,) if out_q else ()) + (m, n), out_dtype)
    return pl.pallas_call(
        body, name=name, grid=(N_CHIPS if out_q else 1, m // tm, n // tn, ksteps), in_specs=in_specs, out_specs=o_spec,
        out_shape=out_shape, scratch_shapes=[pltpu.VMEM((tm, tn), F32)], input_output_aliases=aliases,
        compiler_params=_params("parallel", "parallel", "parallel", "arbitrary"),
    )(*args)


def _row_spec(r, tm):
    if isinstance(r, tuple):
        arr, width, cb = r
        return arr, pl.BlockSpec((tm, width), lambda i, cb=cb: (i, cb))
    return r, pl.BlockSpec((tm, r.shape[1]), lambda i: (i, 0))


def _const_spec(c):
    return pl.BlockSpec(c.shape, lambda i: (0,) * c.ndim)


def _row_fwd(fn, rows, consts, outs, *, name, tm=256):
    tm = _tile(_row_spec(rows[0], tm)[0].shape[0], tm, 8)
    arrs, specs = zip(*[_row_spec(r, tm) for r in rows])
    t = arrs[0].shape[0]
    nr, nc = len(rows), len(consts)

    def body(*refs):
        vals = [r[...].astype(F32) for r in refs[:nr + nc]]
        res = fn(*vals)
        for o_ref, v in zip(refs[nr + nc:], res):
            o_ref[...] = v.astype(o_ref.dtype)

    out = pl.pallas_call(
        body, name=name, grid=(t // tm,),
        in_specs=list(specs) + [_const_spec(c) for c in consts],
        out_specs=[pl.BlockSpec((tm, w), lambda i: (i, 0)) for w, _ in outs],
        out_shape=[jax.ShapeDtypeStruct((t, w), dt) for w, dt in outs],
        compiler_params=_params("parallel"),
    )(*arrs, *consts)
    return list(out)


def _row_bwd(fn, rows, consts, cts, row_grads, const_grads, *, name, tm=256):
    tm = _tile(_row_spec(rows[0], tm)[0].shape[0], tm, 8)
    arrs, specs = zip(*[_row_spec(r, tm) for r in rows])
    ct_arrs, ct_specs = zip(*[_row_spec(r, tm) for r in cts])
    t = arrs[0].shape[0]
    nr, nc, nt = len(rows), len(consts), len(cts)
    n_in = nr + nc + nt

    def body(*refs):
        vals = [r[...].astype(F32) for r in refs[:nr + nc]]
        ctv = tuple(r[...].astype(F32) for r in refs[nr + nc:n_in])
        _, vjp = jax.vjp(fn, *vals)
        g = vjp(ctv)
        outs = refs[n_in:]
        for (idx, _), o_ref in zip(row_grads, outs[:len(row_grads)]):
            o_ref[...] = g[idx].astype(o_ref.dtype)
        first = pl.program_id(0) == 0
        for ci, o_ref in zip(const_grads, outs[len(row_grads):]):
            @pl.when(first)
            def _(o_ref=o_ref):
                o_ref[...] = jnp.zeros_like(o_ref)

            o_ref[...] += g[nr + ci]

    widths = [(_row_spec(rows[idx], tm)[1].block_shape[1], dt) for idx, dt in row_grads]
    out = pl.pallas_call(
        body, name=name, grid=(t // tm,),
        in_specs=list(specs) + [_const_spec(c) for c in consts] + list(ct_specs),
        out_specs=[pl.BlockSpec((tm, w), lambda i: (i, 0)) for w, _ in widths]
        + [_const_spec(consts[ci]) for ci in const_grads],
        out_shape=[jax.ShapeDtypeStruct((t, w), dt) for w, dt in widths]
        + [jax.ShapeDtypeStruct(consts[ci].shape, F32) for ci in const_grads],
        compiler_params=_params("arbitrary"),
    )(*arrs, *consts, *ct_arrs)
    return list(out)


def _rms(x, g):
    return x * lax.rsqrt(jnp.mean(x * x, axis=-1, keepdims=True) + EPS) * g


def _f_rms(h, g):
    return (_rms(h, g),)


def _f_rms_res(h, g):
    return (_rms(h, g), h)


def _f_swiglu(g, u):
    return (g * jax.nn.sigmoid(g) * u,)


def _f_ple(h, gl, pp):
    return (h + jax.nn.sigmoid(gl) * pp,)


def _f_gdn_out(o, z, gain):
    outs = []
    for hd in range(GDN_V_HEADS):
        sl = slice(hd * GDN_HEAD_DIM, (hd + 1) * GDN_HEAD_DIM)
        oh, zh = o[:, sl], z[:, sl]
        outs.append(_rms(oh, gain) * (zh * jax.nn.sigmoid(zh)))
    return (jnp.concatenate(outs, axis=1),)


def _loss_head(y, target, *, name, tm=512):
    t, d = y.shape
    tm = _tile(t, tm, 8)

    def body(y_ref, t_ref, dy_ref, l_ref):
        @pl.when(pl.program_id(0) == 0)
        def _():
            l_ref[...] = jnp.zeros_like(l_ref)

        e = y_ref[...] - t_ref[...]
        dy_ref[...] = e * (1.0 / d)
        l_ref[...] += jnp.sum(e * e) * (0.5 / d)

    dy, l = pl.pallas_call(
        body, name=name, grid=(t // tm,),
        in_specs=[pl.BlockSpec((tm, d), lambda i: (i, 0))] * 2,
        out_specs=[pl.BlockSpec((tm, d), lambda i: (i, 0)), pl.BlockSpec((8, LANES), lambda i: (0, 0))],
        out_shape=[jax.ShapeDtypeStruct((t, d), F32), jax.ShapeDtypeStruct((8, LANES), F32)],
        compiler_params=_params("arbitrary"),
    )(y, target)
    return dy, l[0, 0]


def _dg(a, b, ca, cb):
    nb = a.ndim - 2
    batch = tuple(range(nb))
    return lax.dot_general(a, b, (((ca + nb,), (cb + nb,)), (batch, batch)), preferred_element_type=F32)


def _b(x):
    return x.astype(BF16)


@jax.custom_vjp
def _bdot(a, b):
    return _dg(_b(a), _b(b), 1, 0)


def _bdot_fwd(a, b):
    return _bdot(a, b), (a, b)


def _bdot_bwd(r, ct):
    a, b = r
    return _dg(_b(ct), _b(b), 1, 1), _dg(_b(a), _b(ct), 0, 0)


_bdot.defvjp(_bdot_fwd, _bdot_bwd)


@jax.custom_vjp
def _bdot_nt(a, b):
    return _dg(_b(a), _b(b), 1, 1)


def _bdot_nt_fwd(a, b):
    return _bdot_nt(a, b), (a, b)


def _bdot_nt_bwd(r, ct):
    a, b = r
    return _dg(_b(ct), _b(b), 1, 0), _dg(_b(ct), _b(a), 0, 0)


_bdot_nt.defvjp(_bdot_nt_fwd, _bdot_nt_bwd)


@jax.custom_vjp
def _bdot_tn(a, b):
    return _dg(_b(a), _b(b), 0, 0)


def _bdot_tn_fwd(a, b):
    return _bdot_tn(a, b), (a, b)


def _bdot_tn_bwd(r, ct):
    a, b = r
    return _dg(_b(b), _b(ct), 1, 1), _dg(_b(a), _b(ct), 1, 0)


_bdot_tn.defvjp(_bdot_tn_fwd, _bdot_tn_bwd)


def _two(x):
    hi = x.astype(BF16)
    return hi, (x - hi.astype(F32)).astype(BF16)


def _dg3(a, b, ca, cb):
    (ah, al), (bh, bl) = _two(a), _two(b)
    return _dg(ah, bh, ca, cb) + (_dg(ah, bl, ca, cb) + _dg(al, bh, ca, cb))


@jax.custom_vjp
def _hdot(a, b):
    return _dg3(a, b, 1, 0)


def _hdot_fwd(a, b):
    return _hdot(a, b), (a, b)


def _hdot_bwd(r, ct):
    a, b = r
    return _dg3(ct, b, 1, 1), _dg3(a, ct, 0, 0)


_hdot.defvjp(_hdot_fwd, _hdot_bwd)


@jax.custom_vjp
def _unit_lower_inverse(x):
    c = x.shape[-1]
    eye = (lax.broadcasted_iota(jnp.int32, x.shape, 1) == lax.broadcasted_iota(jnp.int32, x.shape, 2)).astype(F32)
    inv, pw = eye + x, x
    for _ in range(int(math.log2(c)) - 1):
        pw = _dg3(pw, pw, 1, 0)
        inv = inv + _dg3(inv, pw, 1, 0)
    return inv


def _unit_lower_inverse_fwd(x):
    inv = _unit_lower_inverse(x)
    return inv, inv


def _unit_lower_inverse_bwd(inv, ct):
    return (_dg3(_dg3(inv, ct, 0, 0), inv, 1, 1),)


_unit_lower_inverse.defvjp(_unit_lower_inverse_fwd, _unit_lower_inverse_bwd)


def _split_dot(x, u):
    hi, lo = _two(x)
    return _dg(hi, u, 1, 0) + _dg(lo, u, 1, 0)


@jax.custom_vjp
def _ldot(l01, x):
    hi, lo = _two(x)
    l01 = l01.astype(BF16)
    return _dg(l01, hi, 1, 0) + _dg(l01, lo, 1, 0)


def _ldot_fwd(l01, x):
    return _ldot(l01, x), l01


def _ldot_bwd(l01, ct):
    hi, lo = _two(ct)
    l01b = l01.astype(BF16)
    return jnp.zeros_like(l01), _dg(l01b, hi, 0, 0) + _dg(l01b, lo, 0, 0)


_ldot.defvjp(_ldot_fwd, _ldot_bwd)


SB_BLK = 128
SB_KEYS = 512
SB_PAIRS = 2
SB_SCALE = HEAD_DIM ** -0.5


def _log_sigmoid(z):
    return jnp.minimum(z, 0.0) - jnp.log(1.0 + jnp.exp(-jnp.abs(z)))


def _sb_consts(t):
    kb = min(SB_KEYS, t)
    nh = 2 * SB_PAIRS
    lane = lax.broadcasted_iota(jnp.int32, (nh, SB_BLK, kb), 2)
    row = lax.broadcasted_iota(jnp.int32, (nh, SB_BLK, kb), 1)
    ur = lax.broadcasted_iota(jnp.int32, (kb, kb), 0)
    uc = lax.broadcasted_iota(jnp.int32, (kb, kb), 1)
    return kb, nh, lane, row, ur, uc


def _sb_heads(x):
    head0 = lax.broadcasted_iota(jnp.int32, (x.shape[0], LANES), 1) < HEAD_DIM
    out = []
    for p in range(SB_PAIRS):
        blk = x[:, p * LANES:(p + 1) * LANES]
        out += [jnp.where(head0, blk, 0.0), jnp.where(head0, 0.0, blk)]
    return jnp.stack(out)


def _sb_pairs(x):
    return jnp.stack([x[:, (h // 2) * LANES:(h // 2 + 1) * LANES] for h in range(2 * SB_PAIRS)])


def _sb_merge(x):
    head0 = lax.broadcasted_iota(jnp.int32, (x.shape[1], LANES), 1) < HEAD_DIM
    return jnp.concatenate([jnp.where(head0, x[2 * p], x[2 * p + 1]) for p in range(SB_PAIRS)], axis=1)


def _sb_rows_dot(x, u):
    nh, rows, k = x.shape
    return _split_dot(x.reshape(nh * rows, k), u).reshape(nh, rows, k)


def _sb_fwd(proj, *, name, gather=()):
    t = proj.shape[0]
    nb = t // SB_BLK
    width = SB_PAIRS * LANES
    ng = SB_W // width
    na = len(gather)

    def body(q_ref, k_ref, v_ref, *rest):
        o_ref, r_ref = rest[na:na + 2]
        i = pl.program_id(1)
        if na:
            step = pl.program_id(0) * nb + i
            copies = lambda **kw: _gather_copies(rest[:na], rest[na + 2:2 * na + 2], *rest[2 * na + 2:], **kw)
            pl.when(step == 0)(lambda: _gather_start(copies(only_first=True)))
        kb, nh, lane, row, ur, uc = _sb_consts(t)
        u_suffix = (ur >= uc).astype(BF16)
        qh = _b(_sb_heads(q_ref[...]) * SB_SCALE)
        diag = (i * SB_BLK) // kb

        def block(j, carry, masked):
            acc, car = carry
            keys = pl.ds(pl.multiple_of(j * kb, kb), kb)
            kj, vj = _b(_sb_pairs(k_ref[keys, :])), _b(_sb_pairs(v_ref[keys, :]))
            z = _dg(qh, kj, 1, 1)
            lk = _log_sigmoid(-z)
            if masked:
                causal = (j * kb + lane) < (i * SB_BLK + row)
                lk = jnp.where(causal, lk, 0.0)
            suf = _sb_rows_dot(lk, u_suffix) + car
            w = jnp.exp(z + suf)
            if masked:
                w = jnp.where(causal, w, 0.0)
            return acc + _dg(_b(w), vj, 1, 0), suf[:, :, 0:1]

        zero = (jnp.zeros((nh, SB_BLK, LANES), F32), jnp.zeros((nh, SB_BLK, 1), F32))
        carry = block(diag, zero, True)
        acc, car = lax.fori_loop(0, diag, lambda s, c: block(diag - 1 - s, c, False), carry)
        o_ref[...] = _sb_merge(acc)
        r_ref[...] = _sb_merge(jnp.broadcast_to(car, (nh, SB_BLK, LANES)))
        if na:
            pl.when(step == ng * nb - 1)(lambda: _gather_finish(copies()))

    return pl.pallas_call(
        body, name=name, grid=(ng, nb),
        in_specs=[pl.BlockSpec((SB_BLK, width), lambda p, i: (i, p)),
                  pl.BlockSpec((t, width), lambda p, i: (0, ng + p)),
                  pl.BlockSpec((t, width), lambda p, i: (0, 2 * ng + p))] + [ANY] * na,
        out_specs=[pl.BlockSpec((SB_BLK, width), lambda p, i: (i, p))] * 2 + [ANY] * na,
        out_shape=[jax.ShapeDtypeStruct((t, SB_W), F32)] * 2
        + [jax.ShapeDtypeStruct((N_CHIPS,) + g.shape, g.dtype) for g in gather],
        scratch_shapes=_gather_scratch(na) if na else [],
        compiler_params=_params("arbitrary", "arbitrary"),
    )(proj, proj, proj, *gather)


def _sb_bwd(proj, rtot, dout, *, name, scatter=()):
    t = proj.shape[0]
    nb = t // SB_BLK
    width = SB_PAIRS * LANES
    ng = SB_W // width
    na = len(scatter)

    def body(q_ref, k_ref, v_ref, r_ref, do_ref, *rest):
        dq_ref, dk_ref, dv_ref = rest[na:na + 3]
        i = pl.program_id(1)
        if na:
            step = pl.program_id(0) * nb + i
            copies = lambda: _scatter_copies(rest[:na], rest[na + 3:4 * na + 3], *rest[4 * na + 3:])
            pl.when(step == 0)(lambda: [cp.start() for cp in copies()] and None)
        kb, nh, lane, row, ur, uc = _sb_consts(t)
        u_incl = (ur <= uc).astype(BF16)
        u_excl = (ur < uc).astype(BF16)
        q, do = q_ref[...], do_ref[...]
        qh, doh = _b(_sb_heads(q) * SB_SCALE), _b(_sb_heads(do))
        qb, dob = _b(_sb_pairs(q) * SB_SCALE), _b(_sb_pairs(do))
        rh = jnp.min(_sb_heads(r_ref[...]), axis=2, keepdims=True)
        diag = (i * SB_BLK) // kb

        @pl.when(i == 0)
        def _():
            dk_ref[...] = jnp.zeros_like(dk_ref)
            dv_ref[...] = jnp.zeros_like(dv_ref)

        def block(j, carry, masked):
            dq_acc, clk, ce = carry
            keys = pl.ds(pl.multiple_of(j * kb, kb), kb)
            kj, vj = _b(_sb_pairs(k_ref[keys, :])), _b(_sb_pairs(v_ref[keys, :]))
            z = _dg(qh, kj, 1, 1)
            lk = _log_sigmoid(-z)
            ls = z + lk
            if masked:
                causal = (j * kb + lane) < (i * SB_BLK + row)
                lk = jnp.where(causal, lk, 0.0)
            pre = _sb_rows_dot(lk, u_incl) + clk
            w = jnp.exp(ls + (rh - pre))
            if masked:
                w = jnp.where(causal, w, 0.0)
            e = _dg(doh, vj, 1, 1) * w
            pre_e = _sb_rows_dot(e, u_excl) + ce
            sig = jnp.exp(ls)
            dz = e - sig * (e + pre_e)
            if masked:
                dz = jnp.where(causal, dz, 0.0)
            dzb = _b(dz)
            dk_ref[keys, :] += _sb_merge(_dg(dzb, qb, 0, 0))
            dv_ref[keys, :] += _sb_merge(_dg(_b(w), dob, 0, 0))
            return dq_acc + _dg(dzb, kj, 1, 0), pre[:, :, kb - 1:], pre_e[:, :, kb - 1:] + e[:, :, kb - 1:]

        zero = (jnp.zeros((nh, SB_BLK, LANES), F32), jnp.zeros((nh, SB_BLK, 1), F32), jnp.zeros((nh, SB_BLK, 1), F32))
        carry = lax.fori_loop(0, diag, lambda j, c: block(j, c, False), zero)
        dq_acc, _, _ = block(diag, carry, True)
        dq_ref[...] = _sb_merge(dq_acc) * SB_SCALE
        if na:
            pl.when(step == ng * nb - 1)(lambda: [cp.wait() for cp in copies()] and None)

    blk = pl.BlockSpec((SB_BLK, width), lambda p, i: (i, p))
    whole = pl.BlockSpec((t, width), lambda p, i: (0, p))
    return pl.pallas_call(
        body, name=name, grid=(ng, nb),
        in_specs=[blk,
                  pl.BlockSpec((t, width), lambda p, i: (0, ng + p)),
                  pl.BlockSpec((t, width), lambda p, i: (0, 2 * ng + p)),
                  blk, blk] + [ANY] * na,
        out_specs=[blk, whole, whole] + [ANY] * (3 * na),
        out_shape=[jax.ShapeDtypeStruct((t, SB_W), F32)] * 3 + _scatter_shapes(scatter),
        scratch_shapes=_dma_sems(3 * na) if na else [],
        compiler_params=_params("arbitrary", "arbitrary"),
    )(proj, proj, proj, rtot, dout, *scatter)


SWA_G = SWA_HEADS // SWA_KV_HEADS


def _swa_heads(first, qs, ks, vs, qg, kg, sinks):
    shape = (SWA_HEADS, WINDOW, 2 * WINDOW)
    qi = lax.broadcasted_iota(jnp.int32, shape, 1)
    kj = lax.broadcasted_iota(jnp.int32, shape, 2)
    dist = qi + WINDOW - kj
    valid = (dist >= 0) & (dist < WINDOW) & (jnp.logical_not(first) | (kj >= WINDOW))
    head = lax.broadcasted_iota(jnp.int32, (SWA_HEADS, 1, 1), 0)
    slope = sum(jnp.where(head == h, 2.0 ** (-8.0 * (h + 1) / SWA_HEADS), 0.0) for h in range(SWA_HEADS))
    kn = _rms(ks, kg)
    per_q_head = lambda x: jnp.concatenate([x[h // SWA_G:h // SWA_G + 1] for h in range(SWA_HEADS)], axis=0)
    k8, v8 = per_q_head(kn), per_q_head(vs)
    s = _bdot_nt(_rms(qs, qg), k8) * (HEAD_DIM ** -0.5)
    s = jnp.where(valid, s - slope * dist.astype(F32), NEG)
    m = lax.stop_gradient(jnp.maximum(jnp.max(s, axis=2, keepdims=True), sinks))
    p = jnp.exp(s - m)
    den = jnp.sum(p, axis=2, keepdims=True) + jnp.exp(sinks - m)
    return _bdot(p / den, v8)


def _swa_split(q, kp, kc, vp, vc, sk):
    lanes = lambda x, n: jnp.stack([x[:, h * HEAD_DIM:(h + 1) * HEAD_DIM] for h in range(n)])
    k2, v2 = jnp.concatenate([kp, kc], axis=0), jnp.concatenate([vp, vc], axis=0)
    sinks = jnp.stack([sk[:, h:h + 1] for h in range(SWA_HEADS)])
    return lanes(q, SWA_HEADS), lanes(k2, SWA_KV_HEADS), lanes(v2, SWA_KV_HEADS), sinks


def _swa_join(x):
    return jnp.concatenate([x[h] for h in range(x.shape[0])], axis=1)


def _swa_specs(t):
    qcb = (3 * SB_W) // SWA_QW
    kcb = (3 * SB_W + SWA_QW) // SWA_KVW
    prev = lambda i: jnp.maximum(i - 1, 0)
    return [pl.BlockSpec((WINDOW, SWA_QW), lambda i: (i, qcb)),
            pl.BlockSpec((WINDOW, SWA_KVW), lambda i: (prev(i), kcb)),
            pl.BlockSpec((WINDOW, SWA_KVW), lambda i: (i, kcb)),
            pl.BlockSpec((WINDOW, SWA_KVW), lambda i: (prev(i), kcb + 1)),
            pl.BlockSpec((WINDOW, SWA_KVW), lambda i: (i, kcb + 1)),
            pl.BlockSpec((1, HEAD_DIM), lambda i: (0, 0)),
            pl.BlockSpec((1, HEAD_DIM), lambda i: (0, 0)),
            pl.BlockSpec((1, SWA_HEADS), lambda i: (0, 0))]


def _swa_fwd(proj, qg, kg, sinks, *, name):
    t = proj.shape[0]

    def body(q_ref, kp_ref, kc_ref, vp_ref, vc_ref, qg_ref, kg_ref, sk_ref, o_ref):
        first = pl.program_id(0) == 0
        qs, ks, vs, sk = _swa_split(q_ref[...], kp_ref[...], kc_ref[...], vp_ref[...], vc_ref[...], sk_ref[...])
        o_ref[...] = _swa_join(_swa_heads(first, qs, ks, vs, qg_ref[...], kg_ref[...], sk))

    return pl.pallas_call(
        body, name=name, grid=(t // WINDOW,), in_specs=_swa_specs(t),
        out_specs=pl.BlockSpec((WINDOW, SWA_QW), lambda i: (i, 0)),
        out_shape=jax.ShapeDtypeStruct((t, SWA_QW), F32),
        compiler_params=_params("parallel"),
    )(proj, proj, proj, proj, proj, qg, kg, sinks)


def _swa_bwd(proj, qg, kg, sinks, dout, *, name):
    t = proj.shape[0]

    def body(q_ref, kp_ref, kc_ref, vp_ref, vc_ref, qg_ref, kg_ref, sk_ref, do_ref,
             dq_ref, dk_ref, dv_ref, dqg_ref, dkg_ref, dsk_ref):
        i = pl.program_id(0)
        first = i == 0

        @pl.when(first)
        def _():
            for r in (dk_ref, dv_ref, dqg_ref, dkg_ref, dsk_ref):
                r[...] = jnp.zeros_like(r)

        qs, ks, vs, sk = _swa_split(q_ref[...], kp_ref[...], kc_ref[...], vp_ref[...], vc_ref[...], sk_ref[...])
        do = do_ref[...]
        cts = jnp.stack([do[:, h * HEAD_DIM:(h + 1) * HEAD_DIM] for h in range(SWA_HEADS)])
        _, vjp = jax.vjp(functools.partial(_swa_heads, first), qs, ks, vs, qg_ref[...], kg_ref[...], sk)
        dqs, dks, dvs, dqg, dkg, dsk = vjp(cts)
        dq_ref[...] = _swa_join(dqs)
        dk2, dv2 = _swa_join(dks), _swa_join(dvs)
        cur = pl.ds(pl.multiple_of(i * WINDOW, WINDOW), WINDOW)
        prv = pl.ds(pl.multiple_of(jnp.maximum(i - 1, 0) * WINDOW, WINDOW), WINDOW)
        dk_ref[prv, :] += dk2[:WINDOW]
        dv_ref[prv, :] += dv2[:WINDOW]
        dk_ref[cur, :] += dk2[WINDOW:]
        dv_ref[cur, :] += dv2[WINDOW:]
        dqg_ref[...] += dqg
        dkg_ref[...] += dkg
        dsk_ref[...] += _swa_join(dsk)

    whole = lambda shape: pl.BlockSpec(shape, lambda i: (0, 0))
    return pl.pallas_call(
        body, name=name, grid=(t // WINDOW,),
        in_specs=_swa_specs(t) + [pl.BlockSpec((WINDOW, SWA_QW), lambda i: (i, 0))],
        out_specs=[pl.BlockSpec((WINDOW, SWA_QW), lambda i: (i, 0)), whole((t, SWA_KVW)), whole((t, SWA_KVW)),
                   whole((1, HEAD_DIM)), whole((1, HEAD_DIM)), whole((1, SWA_HEADS))],
        out_shape=[jax.ShapeDtypeStruct((t, SWA_QW), F32), jax.ShapeDtypeStruct((t, SWA_KVW), F32),
                   jax.ShapeDtypeStruct((t, SWA_KVW), F32), jax.ShapeDtypeStruct((1, HEAD_DIM), F32),
                   jax.ShapeDtypeStruct((1, HEAD_DIM), F32), jax.ShapeDtypeStruct((1, SWA_HEADS), F32)],
        compiler_params=_params("arbitrary"),
    )(proj, proj, proj, proj, proj, qg, kg, sinks, dout)


CONV_CB = 512
CONV_TM = 512
HALO = 8


def _conv_pre(x_ref, h_ref, w_ref, i):
    halo = jnp.where(i > 0, h_ref[...], 0.0)
    xe = jnp.concatenate([halo, x_ref[...]], axis=0)
    tm = x_ref.shape[0]
    w = w_ref[...]
    c = sum(w[k:k + 1, :] * xe[HALO - (GDN_CONV - 1) + k:HALO - (GDN_CONV - 1) + k + tm] for k in range(GDN_CONV))
    return c, xe


def _conv_specs(tm, cb):
    return [pl.BlockSpec((tm, cb), lambda c, i: (i, c)),
            pl.BlockSpec((HALO, cb), lambda c, i: (jnp.maximum(i * (tm // HALO) - 1, 0), c)),
            pl.BlockSpec((GDN_CONV, cb), lambda c, i: (0, c))]


def _conv_fwd(x, w, dact=None, *, name):
    t, ch = x.shape
    tm, cb = _tile(t, CONV_TM), _tile(ch, CONV_CB)

    def body(*refs):
        x_ref, h_ref, w_ref = refs[:3]
        c, _ = _conv_pre(x_ref, h_ref, w_ref, pl.program_id(1))
        sig = jax.nn.sigmoid(c)
        if dact is None:
            refs[3][...] = c * sig
        else:
            refs[4][...] = refs[3][...] * (sig * (1.0 + c * (1.0 - sig)))

    tile = pl.BlockSpec((tm, cb), lambda c, i: (i, c))
    extra = () if dact is None else (dact,)
    return pl.pallas_call(
        body, name=name, grid=(ch // cb, t // tm),
        in_specs=_conv_specs(tm, cb) + [tile] * len(extra), out_specs=tile,
        out_shape=jax.ShapeDtypeStruct((t, ch), F32),
        compiler_params=_params("parallel", "parallel"),
    )(x, x, w, *extra)


def _conv_bwd(x, w, dc, *, name):
    t, ch = x.shape
    tm, cb = _tile(t, CONV_TM), _tile(ch, CONV_CB)
    nt = t // tm

    def body(x_ref, h_ref, w_ref, dc_ref, nh_ref, dx_ref, dw_ref):
        i = pl.program_id(1)

        @pl.when(i == 0)
        def _():
            dw_ref[...] = jnp.zeros_like(dw_ref)

        halo = jnp.where(i > 0, h_ref[...], 0.0)
        xe = jnp.concatenate([halo, x_ref[...]], axis=0)
        dc = dc_ref[...]
        dce = jnp.concatenate([dc, jnp.where(i < nt - 1, nh_ref[...], 0.0)], axis=0)
        w = w_ref[...]
        last = GDN_CONV - 1
        dx_ref[...] = sum(w[k:k + 1, :] * dce[last - k:last - k + tm] for k in range(GDN_CONV))
        dw_ref[...] += jnp.concatenate(
            [jnp.sum(dc * xe[HALO - last + k:HALO - last + k + tm], axis=0, keepdims=True) for k in range(GDN_CONV)],
            axis=0)

    tile = pl.BlockSpec((tm, cb), lambda c, i: (i, c))
    nxt = pl.BlockSpec((HALO, cb), lambda c, i: (jnp.minimum((i + 1) * (tm // HALO), t // HALO - 1), c))
    return pl.pallas_call(
        body, name=name, grid=(ch // cb, nt),
        in_specs=_conv_specs(tm, cb) + [tile, nxt],
        out_specs=[tile, pl.BlockSpec((GDN_CONV, cb), lambda c, i: (0, c))],
        out_shape=[jax.ShapeDtypeStruct((t, ch), F32), jax.ShapeDtypeStruct((GDN_CONV, ch), F32)],
        compiler_params=_params("parallel", "arbitrary"),
    )(x, x, w, dc, dc)


def _gdn_chunk(qraw, kraw, v, bl, a, alog, dtb, state):
    c, d = GDN_CHUNK, GDN_HEAD_DIM
    nh = qraw.shape[0]
    ri = lax.broadcasted_iota(jnp.int32, (nh, c, c), 1)
    ci = lax.broadcasted_iota(jnp.int32, (nh, c, c), 2)
    incl, strict = ri >= ci, ri > ci
    q = qraw * lax.rsqrt(jnp.sum(qraw * qraw, axis=-1, keepdims=True) + EPS) * (d ** -0.5)
    k = kraw * lax.rsqrt(jnp.sum(kraw * kraw, axis=-1, keepdims=True) + EPS)
    beta = jax.nn.sigmoid(bl)
    g = -jnp.exp(alog) * jax.nn.softplus(a + dtb)
    gc = _ldot(incl.astype(F32), jnp.broadcast_to(g, (nh, c, d)))
    gcm = gc[:, :, :c]
    decay = jnp.exp(jnp.where(incl, gcm - jnp.swapaxes(gcm, 1, 2), NEG))
    eg = jnp.exp(gc)
    kbeta = k * beta
    x = -jnp.where(strict, _bdot_nt(kbeta, k) * decay, 0.0)
    tinv = _unit_lower_inverse(x)
    u = _hdot(tinv, v * beta)
    w = _hdot(tinv, kbeta * eg)
    attn = jnp.where(incl, _bdot_nt(q, k) * decay, 0.0)
    glast = gc[:, c - 1:c, :]
    v_new = u - _bdot(w, state)
    o = _bdot(q * eg, state) + _bdot(attn, v_new)
    state = state * jnp.exp(glast) + _bdot_tn(k * jnp.exp(glast - gc), v_new)
    return o, state


GDN_REP = GDN_V_HEADS // GDN_K_HEADS
GDN_HB = 8


def _gdn_pick(vals, kh, r):
    ba, alog, dtb = vals
    lane = lax.broadcasted_iota(jnp.int32, ba.shape, 1)
    hv = kh * GDN_REP + r
    bl = jnp.sum(jnp.where(lane == hv, ba, 0.0), axis=1, keepdims=True)
    a = jnp.sum(jnp.where(lane == GDN_V_HEADS + hv, ba, 0.0), axis=1, keepdims=True)
    lane1 = lax.broadcasted_iota(jnp.int32, alog.shape, 1)
    al = jnp.sum(jnp.where(lane1 == hv, alog, 0.0), axis=1, keepdims=True)
    db = jnp.sum(jnp.where(lane1 == hv, dtb, 0.0), axis=1, keepdims=True)
    return bl, a, al, db


def _gdn_stack(qs, ks, vs, small, j):
    d = GDN_HEAD_DIM
    per = [[], [], [], [], [], [], []]
    for hh in range(GDN_HB):
        q, k = qs[:, hh * d:(hh + 1) * d], ks[:, hh * d:(hh + 1) * d]
        for r in range(GDN_REP):
            col = (hh * GDN_REP + r) * d
            for lst, val in zip(per, (q, k, vs[:, col:col + d]) + _gdn_pick(small, j * GDN_HB + hh, r)):
                lst.append(val)
    return tuple(jnp.stack(lst) for lst in per)


def _gdn_specs(nchunk, rev):
    c, d = GDN_CHUNK, GDN_HEAD_DIM
    at = (lambda n: nchunk - 1 - n) if rev else (lambda n: n)
    ng = GDN_K_HEADS // GDN_HB
    return at, [pl.BlockSpec((c, GDN_HB * d), lambda n, j: (at(n), j)),
                pl.BlockSpec((c, GDN_HB * d), lambda n, j: (at(n), ng + j)),
                pl.BlockSpec((c, GDN_HB * GDN_REP * d), lambda n, j: (at(n), ng + j)),
                pl.BlockSpec((c, 2 * GDN_V_HEADS), lambda n, j: (at(n), 0)),
                pl.BlockSpec((1, GDN_V_HEADS), lambda n, j: (0, 0)),
                pl.BlockSpec((1, GDN_V_HEADS), lambda n, j: (0, 0))]


def _gdn_fwd(act, ba, alog, dtb, *, name):
    t = act.shape[0]
    c, d = GDN_CHUNK, GDN_HEAD_DIM
    nchunk = t // c
    at, specs = _gdn_specs(nchunk, False)

    def body(q_ref, k_ref, v_ref, ba_ref, al_ref, db_ref, o_ref, s_ref, state):
        n, j = pl.program_id(0), pl.program_id(1)
        heads = pl.ds(j * GDN_HB, GDN_HB)

        @pl.when(n == 0)
        def _():
            state[heads] = jnp.zeros((GDN_HB, GDN_REP, d, d), F32)

        s_in = state[heads]
        s_ref[...] = s_in
        args = _gdn_stack(q_ref[...], k_ref[...], v_ref[...], (ba_ref[...], al_ref[...], db_ref[...]), j)
        o, s_new = _gdn_chunk(*args, s_in.reshape(GDN_HB * GDN_REP, d, d))
        o_ref[...] = jnp.concatenate([o[b] for b in range(GDN_HB * GDN_REP)], axis=1)
        state[heads] = s_new.reshape(GDN_HB, GDN_REP, d, d)

    return pl.pallas_call(
        body, name=name, grid=(nchunk, GDN_K_HEADS // GDN_HB), in_specs=specs,
        out_specs=[pl.BlockSpec((c, GDN_HB * GDN_REP * d), lambda n, j: (n, j)),
                   pl.BlockSpec((None, GDN_HB, GDN_REP, d, d), lambda n, j: (n, j, 0, 0, 0))],
        out_shape=[jax.ShapeDtypeStruct((t, GDN_VW), F32),
                   jax.ShapeDtypeStruct((nchunk, GDN_K_HEADS, GDN_REP, d, d), F32)],
        scratch_shapes=[pltpu.VMEM((GDN_K_HEADS, GDN_REP, d, d), F32)],
        compiler_params=_params("arbitrary", "arbitrary"),
    )(act, act, act, ba, alog, dtb)


def _gdn_bwd(act, ba, alog, dtb, states, dout, *, name):
    t = act.shape[0]
    c, d = GDN_CHUNK, GDN_HEAD_DIM
    nchunk = t // c
    at, specs = _gdn_specs(nchunk, True)

    def body(q_ref, k_ref, v_ref, ba_ref, al_ref, db_ref, s_ref, do_ref,
             dq_ref, dk_ref, dv_ref, dba_ref, dal_ref, ddb_ref, dstate):
        n, j = pl.program_id(0), pl.program_id(1)

        @pl.when(n == 0)
        def _():
            dstate[pl.ds(j * GDN_HB, GDN_HB)] = jnp.zeros((GDN_HB, GDN_REP, d, d), F32)

        @pl.when((n == 0) & (j == 0))
        def _():
            dal_ref[...] = jnp.zeros_like(dal_ref)
            ddb_ref[...] = jnp.zeros_like(ddb_ref)

        @pl.when(j == 0)
        def _():
            dba_ref[...] = jnp.zeros_like(dba_ref)

        heads = pl.ds(j * GDN_HB, GDN_HB)
        nh = GDN_HB * GDN_REP
        args = _gdn_stack(q_ref[...], k_ref[...], v_ref[...], (ba_ref[...], al_ref[...], db_ref[...]), j)
        _, vjp = jax.vjp(_gdn_chunk, *args, s_ref[...].reshape(nh, d, d))
        do = do_ref[...]
        do = jnp.stack([do[:, b * d:(b + 1) * d] for b in range(nh)])
        gq, gk, gv, gbl, ga, gal, gdb, gs = vjp((do, dstate[heads].reshape(nh, d, d)))
        dstate[heads] = gs.reshape(GDN_HB, GDN_REP, d, d)
        dq_ref[...] = jnp.concatenate([gq[GDN_REP * hh] + gq[GDN_REP * hh + 1] for hh in range(GDN_HB)], axis=1)
        dk_ref[...] = jnp.concatenate([gk[GDN_REP * hh] + gk[GDN_REP * hh + 1] for hh in range(GDN_HB)], axis=1)
        dv_ref[...] = jnp.concatenate([gv[b] for b in range(nh)], axis=1)
        lane = lax.broadcasted_iota(jnp.int32, (c, 2 * GDN_V_HEADS), 1)
        lane1 = lax.broadcasted_iota(jnp.int32, (1, GDN_V_HEADS), 1)
        dba = jnp.zeros((c, 2 * GDN_V_HEADS), F32)
        dal = jnp.zeros((1, GDN_V_HEADS), F32)
        ddb = jnp.zeros((1, GDN_V_HEADS), F32)
        for b in range(nh):
            hv = j * nh + b
            dba = dba + jnp.where(lane == hv, gbl[b], 0.0) + jnp.where(lane == GDN_V_HEADS + hv, ga[b], 0.0)
            dal = dal + jnp.where(lane1 == hv, gal[b], 0.0)
            ddb = ddb + jnp.where(lane1 == hv, gdb[b], 0.0)
        dba_ref[...] += dba
        dal_ref[...] += dal
        ddb_ref[...] += ddb

    small = pl.BlockSpec((1, GDN_V_HEADS), lambda n, j: (0, 0))
    return pl.pallas_call(
        body, name=name, grid=(nchunk, GDN_K_HEADS // GDN_HB),
        in_specs=specs + [pl.BlockSpec((None, GDN_HB, GDN_REP, d, d), lambda n, j: (at(n), j, 0, 0, 0)),
                          pl.BlockSpec((c, GDN_HB * GDN_REP * d), lambda n, j: (at(n), j))],
        out_specs=[pl.BlockSpec((c, GDN_HB * d), lambda n, j: (at(n), j)),
                   pl.BlockSpec((c, GDN_HB * d), lambda n, j: (at(n), j)),
                   pl.BlockSpec((c, GDN_HB * GDN_REP * d), lambda n, j: (at(n), j)),
                   pl.BlockSpec((c, 2 * GDN_V_HEADS), lambda n, j: (at(n), 0)),
                   small, small],
        out_shape=[jax.ShapeDtypeStruct((t, GDN_KW), F32), jax.ShapeDtypeStruct((t, GDN_KW), F32),
                   jax.ShapeDtypeStruct((t, GDN_VW), F32), jax.ShapeDtypeStruct((t, 2 * GDN_V_HEADS), F32),
                   jax.ShapeDtypeStruct((1, GDN_V_HEADS), F32), jax.ShapeDtypeStruct((1, GDN_V_HEADS), F32)],
        scratch_shapes=[pltpu.VMEM((GDN_K_HEADS, GDN_REP, d, d), F32)],
        compiler_params=_params("arbitrary", "arbitrary"),
    )(act, act, act, ba, alog, dtb, states, dout)


N_DEV = 8
ANY = pl.BlockSpec(memory_space=pl.ANY)


def _coords():
    return lax.axis_index("x"), lax.axis_index("y"), lax.axis_index("c")


def _other_chips(x, y):
    return [(1 - x, y), (x, 1 - y), (1 - x, 1 - y)]


def _remote(src, dst, send_sems, recv_sems, k, to):
    return pltpu.make_async_remote_copy(src_ref=src, dst_ref=dst, send_sem=send_sems.at[k], recv_sem=recv_sems.at[k],
                                        device_id=to, device_id_type=MESH)


def _dma_sems(n):
    return [pltpu.SemaphoreType.DMA((n,)), pltpu.SemaphoreType.DMA((n,))]


def _gather_copies(ins, outs, send_sems, recv_sems, local_sems, only_first=False):
    x, y, c = _coords()
    sibling = (x, y, 1 - c)
    local, sends, arrivals, relays, relayed = [], [], [], [], []
    for a, (x_ref, out_ref) in enumerate(zip(ins, outs)):
        local.append(pltpu.make_async_copy(x_ref, out_ref.at[2 * x + y], local_sems.at[a]))
        for j, (cx, cy) in enumerate(_other_chips(x, y)):
            k, theirs = 6 * a + j, 2 * cx + cy
            sends.append(_remote(x_ref.at[c], out_ref.at[2 * x + y, c], send_sems, recv_sems, k, (cx, cy, c)))
            if only_first:
                continue
            arrivals.append(_remote(x_ref.at[c], out_ref.at[theirs, c], send_sems, recv_sems, k, (cx, cy, c)))
            relays.append(_remote(out_ref.at[theirs, c], out_ref.at[theirs, c], send_sems, recv_sems, k + 3, sibling))
            relayed.append(_remote(x_ref.at[c], out_ref.at[theirs, 1 - c], send_sems, recv_sems, k + 3, sibling))
    return local, sends, arrivals, relays, relayed


def _gather_start(copies):
    local, sends, _, _, _ = copies
    for cp in local + sends:
        cp.start()


def _gather_finish(copies):
    local, sends, arrivals, relays, relayed = copies
    for landed, relay in zip(arrivals, relays):
        landed.wait_recv()
        relay.start()
    for cp in relayed:
        cp.wait_recv()
    for cp in sends + relays:
        cp.wait_send()
    for cp in local:
        cp.wait()


def _gather_scratch(na):
    return _dma_sems(6 * na) + [pltpu.SemaphoreType.DMA((na,))]


def _gather_quarters(parts, *, name):
    na = len(parts)

    def body(*refs):
        copies = _gather_copies(refs[:na], refs[na:2 * na], *refs[2 * na:])
        _gather_start(copies)
        _gather_finish(copies)

    return pl.pallas_call(
        body, name=name, in_specs=[ANY] * na, out_specs=[ANY] * na,
        out_shape=[jax.ShapeDtypeStruct((N_CHIPS,) + p.shape, p.dtype) for p in parts],
        scratch_shapes=_gather_scratch(na),
    )(*parts)


def _swap_halves(grads, *, name):
    na = len(grads)

    def body(*refs):
        ins, outs = refs[:na], refs[na:2 * na]
        send_sems, recv_sems = refs[2 * na:]
        x, y, c = _coords()
        sends = [_remote(g_ref.at[j, 1 - c], o_ref.at[j], send_sems, recv_sems, N_CHIPS * a + j, (x, y, 1 - c))
                 for a, (g_ref, o_ref) in enumerate(zip(ins, outs)) for j in range(N_CHIPS)]
        for cp in sends:
            cp.start()
        for cp in sends:
            cp.wait()

    return pl.pallas_call(
        body, name=name, in_specs=[ANY] * na, out_specs=[ANY] * na,
        out_shape=[jax.ShapeDtypeStruct((N_CHIPS,) + g.shape[2:], g.dtype) for g in grads],
        scratch_shapes=_dma_sems(N_CHIPS * na),
    )(*grads)


def _scatter_copies(ins, outs, send_sems, recv_sems):
    x, y, c = _coords()
    return [_remote(p_ref.at[2 * cx + cy], outs[3 * a + j], send_sems, recv_sems, 3 * a + j, (cx, cy, c))
            for a, p_ref in enumerate(ins) for j, (cx, cy) in enumerate(_other_chips(x, y))]


def _scatter_shapes(pairs):
    return [jax.ShapeDtypeStruct(p.shape[1:], p.dtype) for p in pairs for _ in range(3)]


def _scatter_quarters(pairs, *, name):
    na = len(pairs)

    def body(*refs):
        sends = _scatter_copies(refs[:na], refs[na:4 * na], *refs[4 * na:])
        for cp in sends:
            cp.start()
        for cp in sends:
            cp.wait()

    out = pl.pallas_call(
        body, name=name, in_specs=[ANY] * na, out_specs=[ANY] * (3 * na), out_shape=_scatter_shapes(pairs),
        scratch_shapes=_dma_sems(3 * na),
    )(*pairs)
    return [out[3 * a:3 * a + 3] for a in range(na)]


def _share_halves(tots, *, name):
    na = len(tots)

    def body(*refs):
        ins, outs = refs[:na], refs[na:2 * na]
        send_sems, recv_sems = refs[2 * na:]
        x, y, c = _coords()
        sends = [_remote(t_ref, o_ref, send_sems, recv_sems, a, (x, y, 1 - c))
                 for a, (t_ref, o_ref) in enumerate(zip(ins, outs))]
        for cp in sends:
            cp.start()
        for cp in sends:
            cp.wait()

    return pl.pallas_call(
        body, name=name, in_specs=[ANY] * na, out_specs=[ANY] * na,
        out_shape=[jax.ShapeDtypeStruct(t.shape, t.dtype) for t in tots],
        scratch_shapes=_dma_sems(na),
    )(*tots)


def _gather_all(vec, *, name):
    m, w = vec.shape

    def body(x_ref, out_ref, send_sems, recv_sems, local_sem):
        x, y, c = _coords()
        me, sibling = (x, y, c), (x, y, 1 - c)
        chips = _other_chips(x, y)

        def rows(px, py, pc):
            return out_ref.at[pl.ds((4 * px + 2 * py + pc) * m, m), :]

        def copy(k, block, to, src=None):
            return _remote(rows(*block) if src is None else src, rows(*block), send_sems, recv_sems, k, to)

        mine = pltpu.make_async_copy(x_ref, rows(*me), local_sem)
        mine.start()
        first = [copy(0, me, sibling, src=x_ref)]
        first += [copy(1 + j, me, (*chip, c), src=x_ref) for j, chip in enumerate(chips)]
        for cp in first:
            cp.start()
        passed = [copy(4 + j, (*chip, c), sibling) for j, chip in enumerate(chips)]
        for j, chip in enumerate(chips):
            copy(1 + j, (*chip, c), me).wait_recv()
            passed[j].start()
        copy(0, sibling, me).wait_recv()
        for j, chip in enumerate(chips):
            copy(4 + j, (*chip, 1 - c), me).wait_recv()
        for cp in first + passed:
            cp.wait_send()
        mine.wait()

    vm = pl.BlockSpec(memory_space=pltpu.VMEM)
    return pl.pallas_call(
        body, name=name, in_specs=[vm], out_specs=vm, out_shape=jax.ShapeDtypeStruct((N_DEV * m, w), vec.dtype),
        scratch_shapes=_dma_sems(7) + [pltpu.SemaphoreType.DMA(())],
    )(vec)


def _sum_blocks(allv, n, *, name):
    m = allv.shape[0] // n

    def body(a_ref, o_ref):
        acc = a_ref[0:m, :]
        for d in range(1, n):
            acc = acc + a_ref[d * m:(d + 1) * m, :]
        o_ref[...] = acc

    return pl.pallas_call(body, name=name, out_shape=jax.ShapeDtypeStruct((m, allv.shape[1]), allv.dtype))(allv)


EW_BLOCK_BYTES = 1 << 20


def _ew_rows(rows, w):
    return _tile(rows, max(8, (EW_BLOCK_BYTES // (4 * w)) // 8 * 8), 8)


def _add_pair(g, got, c, *, name):
    _, _, rows, w = g.shape
    tr = _ew_rows(rows, w)

    def body(c_ref, g_ref, got_ref, o_ref):
        o_ref[...] = (g_ref[...] + got_ref[...]).astype(o_ref.dtype)

    blk = pl.BlockSpec((None, tr, w), lambda q, i, c_ref: (q, i, 0))
    return pl.pallas_call(
        body, name=name,
        grid_spec=pltpu.PrefetchScalarGridSpec(
            num_scalar_prefetch=1, grid=(N_CHIPS, rows // tr),
            in_specs=[pl.BlockSpec((None, None, tr, w), lambda q, i, c_ref: (q, c_ref[0], i, 0)), blk], out_specs=blk),
        out_shape=jax.ShapeDtypeStruct(got.shape, BF16),
        compiler_params=_params("parallel", "parallel"),
    )(c, g, got)


def _add_chips(pair, recv, chip, *, name):
    _, rows, w = pair.shape
    tr = _ew_rows(rows, w)

    def body(chip_ref, p_ref, r0_ref, r1_ref, r2_ref, o_ref):
        f = lambda r: r[...].astype(F32)
        o_ref[...] = ((f(p_ref) + f(r0_ref)) + f(r1_ref)) + f(r2_ref)

    blk = pl.BlockSpec((tr, w), lambda i, chip_ref: (i, 0))
    return pl.pallas_call(
        body, name=name,
        grid_spec=pltpu.PrefetchScalarGridSpec(
            num_scalar_prefetch=1, grid=(rows // tr,),
            in_specs=[pl.BlockSpec((None, tr, w), lambda i, chip_ref: (chip_ref[0], i, 0)), blk, blk, blk], out_specs=blk),
        out_shape=jax.ShapeDtypeStruct((rows, w), F32),
        compiler_params=_params("parallel"),
    )(chip, pair, *recv)


def _adamw_math(w, g, m, v):
    nm = ADAM_B1 * m + (1.0 - ADAM_B1) * g
    nv = ADAM_B2 * v + (1.0 - ADAM_B2) * (g * g)
    m_hat = nm / (1.0 - ADAM_B1 ** ADAM_STEP)
    v_hat = nv / (1.0 - ADAM_B2 ** ADAM_STEP)
    return -ADAM_LR * (m_hat / (jnp.sqrt(v_hat) + ADAM_EPS) + ADAM_WD * w), nm, nv


def _adamw(w, g, m, v, *, name):
    shape = w.shape
    last = shape[-1]
    w2, g2, m2, v2 = (a.reshape(-1, last) for a in (w, g, m, v))
    rows = w2.shape[0]
    tm = _ew_rows(rows, last)

    def body(w_ref, g_ref, m_ref, v_ref, d_ref, nm_ref, nv_ref):
        d_ref[...], nm_ref[...], nv_ref[...] = _adamw_math(w_ref[...], g_ref[...], m_ref[...], v_ref[...])

    spec = pl.BlockSpec((tm, last), lambda i: (i, 0))
    out = jax.ShapeDtypeStruct((rows, last), F32)
    d, nm, nv = pl.pallas_call(
        body, name=name, grid=(rows // tm,), in_specs=[spec] * 4, out_specs=[spec] * 3, out_shape=[out] * 3,
        compiler_params=_params("parallel"),
    )(w2, g2, m2, v2)
    return d.reshape(shape), nm.reshape(shape), nv.reshape(shape)


def _adamw_halves(w, m, v, mine, theirs, c, *, name, into=None):
    rows, wd = w.shape[-2:]
    tr = _ew_rows(rows, wd)
    bufs, at = into if into is not None else ((), ())

    def body(c_ref, w_ref, m_ref, v_ref, a_ref, b_ref, *rest):
        g_ref, d_ref, nm_ref, nv_ref = rest[len(bufs):]
        g = jnp.where(pl.program_id(0) == c_ref[0], a_ref[...], b_ref[...])
        g_ref[...] = g
        d_ref[...], nm_ref[...], nv_ref[...] = _adamw_math(w_ref[...], g, m_ref[...], v_ref[...])

    full = pl.BlockSpec((None,) * (1 + len(at)) + (tr, wd), lambda hf, i, c_ref: at + (hf, i, 0))
    half = pl.BlockSpec((tr, wd), lambda hf, i, c_ref: (i, 0))
    out = jax.ShapeDtypeStruct(w.shape, F32)
    return pl.pallas_call(
        body, name=name,
        grid_spec=pltpu.PrefetchScalarGridSpec(num_scalar_prefetch=1, grid=(2, rows // tr),
                                               in_specs=[full] * 3 + [half] * 2 + [ANY] * len(bufs),
                                               out_specs=[full] * 4),
        out_shape=[out] * 4, input_output_aliases={6 + b: b for b in range(len(bufs))},
        compiler_params=_params("parallel", "parallel"),
    )(c, w, m, v, mine, theirs, *bufs)


def _join_quarters(q, *, name):
    _, rows, n = q.shape
    tr = _tile(rows, 256, 16)

    def body(q_ref, o_ref):
        o_ref[...] = jnp.concatenate([q_ref[s] for s in range(N_CHIPS)], axis=1)

    return pl.pallas_call(
        body, name=name, grid=(rows // tr,),
        in_specs=[pl.BlockSpec((N_CHIPS, tr, n), lambda i: (0, i, 0))],
        out_specs=pl.BlockSpec((tr, N_CHIPS * n), lambda i: (i, 0)),
        out_shape=jax.ShapeDtypeStruct((rows, N_CHIPS * n), q.dtype),
        compiler_params=_params("parallel"),
    )(q)


def _split_quarters(full, *, name):
    rows, n4 = full.shape
    n = n4 // N_CHIPS
    tr = _tile(rows, 256, 16)

    def body(x_ref, o_ref):
        x = x_ref[...]
        for s in range(N_CHIPS):
            o_ref[s] = x[:, s * n:(s + 1) * n]

    return pl.pallas_call(
        body, name=name, grid=(rows // tr,),
        in_specs=[pl.BlockSpec((tr, n4), lambda i: (i, 0))],
        out_specs=pl.BlockSpec((N_CHIPS, tr, n), lambda i: (0, i, 0)),
        out_shape=jax.ShapeDtypeStruct((N_CHIPS, rows, n), full.dtype),
        compiler_params=_params("parallel"),
    )(full)


_WEIGHTS = ['ffn_norm', 'ffn_w_gate', 'ffn_w_up', 'ffn_w_down', 'mix_norm', 'att_w_in', 'att_q_norm', 'att_k_norm',
            'att_sinks', 'att_w_out', 'gdn_w_in', 'gdn_conv_w', 'gdn_a_log', 'gdn_dt_bias', 'gdn_out_norm', 'gdn_w_out',
            'ple_norm', 'ple_w_gate', 'ple_w_proj']
_BIG = ['ffn_w_gate', 'ffn_w_up', 'ffn_w_down', 'att_w_in', 'att_w_out', 'gdn_w_in', 'gdn_w_out', 'ple_w_gate',
        'ple_w_proj']
_SMALL_CUT = {'ffn_norm': 2, 'gdn_conv_w': 2}
_WHOLE = ['mix_norm', 'att_q_norm', 'att_k_norm', 'att_sinks', 'gdn_a_log', 'gdn_dt_bias', 'gdn_out_norm', 'ple_norm']
PACK_W = 1024
SMALL_ROW_MULT = 8


def _halves(a):
    return a.reshape(2, -1, a.shape[-1])


def _from_quarters(blk, axis):
    full = jnp.moveaxis(blk, 0, axis)
    shp = list(full.shape)
    shp[axis:axis + 2] = [shp[axis] * shp[axis + 1]]
    return full.reshape(shp)


def _to_quarters(full, axis):
    shp = list(full.shape)
    shp[axis:axis + 1] = [N_CHIPS, shp[axis] // N_CHIPS]
    return jnp.moveaxis(full.reshape(shp), axis, 0)


def _pack(parts, row_mult):
    flat = jnp.concatenate(parts, axis=-1)
    n = flat.shape[-1]
    rows = -(-n // (PACK_W * row_mult)) * row_mult
    return jnp.pad(flat, [(0, rows * PACK_W - n)]).reshape(rows, PACK_W)


def _unpack(flat, shapes):
    lead = flat.shape[:-2]
    flat = flat.reshape(lead + (-1,))
    out, off = [], 0
    for shp in shapes:
        n = math.prod(shp)
        out.append(flat[..., off:off + n].reshape(lead + tuple(shp)))
        off += n
    return out


FFN_TM = 1024


def _ffn_up(hn, wg, wu, at, *, name):
    t, d = hn.shape
    fq = wg.shape[-1]
    tm = _tile(t, FFN_TM)

    def body(h_ref, wg_ref, wu_ref, g_ref, u_ref, a_ref):
        h = h_ref[...]
        g, u = _dg(h, _b(wg_ref[...]), 1, 0), _dg(h, _b(wu_ref[...]), 1, 0)
        g_ref[...] = g.astype(BF16)
        u_ref[...] = u.astype(BF16)
        a_ref[...] = _f_swiglu(g, u)[0].astype(BF16)

    w_spec = pl.BlockSpec((None,) * (1 + len(at)) + (d, fq), lambda s, i: (s,) + at + (0, 0))
    o_spec = pl.BlockSpec((None, tm, fq), lambda s, i: (s, i, 0))
    out = jax.ShapeDtypeStruct((N_CHIPS, t, fq), BF16)
    return pl.pallas_call(
        body, name=name, grid=(N_CHIPS, t // tm),
        in_specs=[pl.BlockSpec((tm, d), lambda s, i: (i, 0)), w_spec, w_spec], out_specs=[o_spec] * 3,
        out_shape=[out] * 3, compiler_params=_params("parallel", "parallel"),
    )(hn, wg, wu)


def _ffn_d_up(dout, wd, g, u, at, *, name):
    t, d = dout.shape
    fq = wd.shape[-2]
    tm = _tile(t, FFN_TM)

    def body(do_ref, wd_ref, g_ref, u_ref, dg_ref, du_ref):
        da = _dg(_b(do_ref[...]), _b(wd_ref[...]), 1, 1) * 0.5
        _, vjp = jax.vjp(_f_swiglu, g_ref[...].astype(F32), u_ref[...].astype(F32))
        dg, du = vjp((da,))
        dg_ref[...] = dg.astype(BF16)
        du_ref[...] = du.astype(BF16)

    w_spec = pl.BlockSpec((None,) * (1 + len(at)) + (fq, d), lambda s, i: (s,) + at + (0, 0))
    o_spec = pl.BlockSpec((None, tm, fq), lambda s, i: (s, i, 0))
    out = jax.ShapeDtypeStruct((N_CHIPS, t, fq), BF16)
    return pl.pallas_call(
        body, name=name, grid=(N_CHIPS, t // tm),
        in_specs=[pl.BlockSpec((tm, d), lambda s, i: (i, 0)), w_spec, o_spec, o_spec], out_specs=[o_spec] * 2,
        out_shape=[out] * 2, compiler_params=_params("parallel", "parallel"),
    )(dout, wd, g, u)


def _ffn_fwd(h, gain, wg, wu, wd, at, tag):
    lead = (Q,) + at
    hn, = _row_fwd(_f_rms, [h], [gain], [(D_MODEL, BF16)], name=f"{tag}_norm")
    g, u, a = _ffn_up(hn, wg, wu, at, name=f"{tag}_up")
    out = _mm((a, (Q,)), (wd, lead), res=h, scale=0.5, name=f"{tag}_down")
    return out, (h, hn, g, u, a)


def _ffn_bwd(dout, saved, gain, wg, wu, wd, at, grads, g_at, tag):
    h, hn, g, u, a = saved
    lead = (Q,) + at
    g_lead = (Q,) + g_at
    dg, du = _ffn_d_up(dout, wd, g, u, at, name=f"{tag}_d_up")
    g_gate, g_up, g_down = grads
    g_down = _mm((a, (Q,)), dout, ta=True, scale=0.5, into=(g_down, g_lead), name=f"{tag}_dw_down")
    g_gate = _mm((dg, (Q,)), hn, ta=True, into=(g_gate, g_lead), name=f"{tag}_dw_gate")
    g_up = _mm((du, (Q,)), hn, ta=True, into=(g_up, g_lead), name=f"{tag}_dw_up")
    dhn = _mm((dg, (Q,)), (wg, lead), tb=True, name=f"{tag}_d_norm_gate")
    dhn = _mm((du, (Q,)), (wu, lead), tb=True, res=dhn, name=f"{tag}_d_norm_up")
    dh, dgain = _row_bwd(_f_rms_res, [h], [gain], [dhn, dout], [(0, F32)], [0], name=f"{tag}_d_in")
    return dh, dgain, (g_gate, g_up, g_down)


def _att_fwd(h, gain, w_in, qg, kg, sinks, w_out, gather):
    hn, = _row_fwd(_f_rms, [h], [gain], [(D_MODEL, BF16)], name="att_norm")
    proj = _mm(hn, w_in, name="att_in")
    a, rtot, *gathered = _sb_fwd(proj, name="att_sb", gather=gather)
    b = _swa_fwd(proj, qg, kg, sinks, name="att_swa")
    out = _mm(a, (w_out, (0,)), res=h, name="att_out_sb")
    out = _mm(b, (w_out, (1,)), res=out, name="att_out_swa")
    return out, (h, hn, proj, a, rtot, b), gathered


def _att_bwd(dout, saved, gain, w_in, qg, kg, sinks, w_out, scatter):
    h, hn, proj, a, rtot, b = saved
    da = _mm(dout, (w_out, (0,)), tb=True, name="att_d_sb")
    db = _mm(dout, (w_out, (1,)), tb=True, name="att_d_swa")
    dw_out = lax.empty(w_out.shape, F32)
    dw_out = _mm(a, dout, ta=True, into=(dw_out, (0,)), name="att_dw_out_sb")
    dw_out = _mm(b, dout, ta=True, into=(dw_out, (1,)), name="att_dw_out_swa")
    dq, dk, dv, *landed = _sb_bwd(proj, rtot, da, name="att_sb_bwd", scatter=scatter)
    dqb, dkb, dvb, dqg, dkg, dsk = _swa_bwd(proj, qg, kg, sinks, db, name="att_swa_bwd")
    dproj = jnp.concatenate([dq, dk, dv, dqb, dkb, dvb], axis=1)
    dw_in = _mm(hn, dproj, ta=True, name="att_dw_in")
    dhn = _mm(dproj, w_in, tb=True, name="att_d_norm")
    dh, dgain = _row_bwd(_f_rms_res, [h], [gain], [dhn, dout], [(0, F32)], [0], name="att_d_in")
    return dh, dgain, dw_in, dqg, dkg, dsk, dw_out, [landed[3 * a:3 * a + 3] for a in range(len(scatter))]


def _gdn_layer_fwd(h, gain, w_in, conv_w, alog, dtb, out_gain, w_out):
    w_qkv, w_z, w_ba = w_in[:, :GDN_CONV_W], w_in[:, GDN_CONV_W:GDN_CONV_W + GDN_VW], w_in[:, GDN_CONV_W + GDN_VW:]
    hn, = _row_fwd(_f_rms, [h], [gain], [(D_MODEL, BF16)], name="gdn_norm")
    pq = _mm(hn, w_qkv, name="gdn_in_qkv")
    pz = _mm(hn, w_z, name="gdn_in_z")
    ba = _mm(hn, w_ba, name="gdn_in_ba")
    act = _conv_fwd(pq, conv_w, name="gdn_conv")
    o, states = _gdn_fwd(act, ba, alog, dtb, name="gdn_rule")
    y, = _row_fwd(_f_gdn_out, [o, pz], [out_gain], [(GDN_VW, BF16)], name="gdn_gate")
    out = _mm(y, w_out, res=h, name="gdn_out")
    return out, (h, hn, pq, pz, ba, act, o, states, y, (w_qkv, w_z, w_ba))


def _gdn_layer_bwd(dout, saved, gain, conv_w, alog, dtb, out_gain, w_out):
    h, hn, pq, pz, ba, act, o, states, y, (w_qkv, w_z, w_ba) = saved
    dy = _mm(dout, w_out, tb=True, name="gdn_d_gate")
    dw_out = _mm(y, dout, ta=True, name="gdn_dw_out")
    do, dpz, dout_gain = _row_bwd(_f_gdn_out, [o, pz], [out_gain], [dy], [(0, F32), (1, F32)], [0], name="gdn_gate_bwd")
    dq, dk, dv, dba, dal, ddb = _gdn_bwd(act, ba, alog, dtb, states, do, name="gdn_rule_bwd")
    dact = jnp.concatenate([dq, dk, dv], axis=1)
    dc = _conv_fwd(pq, conv_w, dact, name="gdn_conv_d_pre")
    dpq, dconv = _conv_bwd(pq, conv_w, dc, name="gdn_conv_bwd")
    dw_in = jnp.concatenate([_mm(hn, dpq, ta=True, name="gdn_dw_qkv"), _mm(hn, dpz, ta=True, name="gdn_dw_z"),
                             _mm(hn, dba, ta=True, name="gdn_dw_ba")], axis=1)
    dhn = _mm(dpq, w_qkv, tb=True, name="gdn_d_norm_qkv")
    dhn = _mm(dpz, w_z, tb=True, res=dhn, name="gdn_d_norm_z")
    dhn = _mm(dba, w_ba, tb=True, res=dhn, name="gdn_d_norm_ba")
    dh, dgain = _row_bwd(_f_rms_res, [h], [gain], [dhn, dout], [(0, F32)], [0], name="gdn_d_in")
    return dh, dgain, dw_in, dconv, dal, ddb, dout_gain, dw_out


def _ple_fwd(h, gain, w_gate, w_proj, pe, tag):
    hn, = _row_fwd(_f_rms, [h], [gain], [(D_MODEL, BF16)], name=f"{tag}_norm")
    gl = _mm(hn, w_gate, name=f"{tag}_gate")
    pp = _mm(pe, w_proj, name=f"{tag}_proj")
    out, = _row_fwd(_f_ple, [h, gl, pp], [], [(D_MODEL, F32)], name=f"{tag}_mix")
    return out, (h, hn, gl, pp)


def _ple_bwd(dout, saved, gain, w_gate, pe, tag):
    h, hn, gl, pp = saved
    dha, dgl, dpp = _row_bwd(_f_ple, [h, gl, pp], [], [dout], [(0, F32), (1, BF16), (2, BF16)], [], name=f"{tag}_mix_bwd")
    dw_gate = _mm(hn, dgl, ta=True, name=f"{tag}_dw_gate")
    dw_proj = _mm(pe, dpp, ta=True, name=f"{tag}_dw_proj")
    dhn = _mm(dgl, w_gate, tb=True, name=f"{tag}_d_norm")
    dh, dgain = _row_bwd(_f_rms_res, [h], [gain], [dhn, dha], [(0, F32)], [0], name=f"{tag}_d_in")
    return dh, dgain, dw_gate, dw_proj


def kernel(x, p, ffn_norm, ffn_w_gate, ffn_w_up, ffn_w_down, mix_norm, att_w_in, att_q_norm, att_k_norm, att_sinks, att_w_out, gdn_w_in, gdn_conv_w, gdn_a_log, gdn_dt_bias, gdn_out_norm, gdn_w_out, ple_norm, ple_w_gate, ple_w_proj, loss_target, m_ffn_norm, m_ffn_w_gate, m_ffn_w_up, m_ffn_w_down, m_mix_norm, m_att_w_in, m_att_q_norm, m_att_k_norm, m_att_sinks, m_att_w_out, m_gdn_w_in, m_gdn_conv_w, m_gdn_a_log, m_gdn_dt_bias, m_gdn_out_norm, m_gdn_w_out, m_ple_norm, m_ple_w_gate, m_ple_w_proj, v_ffn_norm, v_ffn_w_gate, v_ffn_w_up, v_ffn_w_down, v_mix_norm, v_att_w_in, v_att_q_norm, v_att_k_norm, v_att_sinks, v_att_w_out, v_gdn_w_in, v_gdn_conv_w, v_gdn_a_log, v_gdn_dt_bias, v_gdn_out_norm, v_gdn_w_out, v_ple_norm, v_ple_w_gate, v_ple_w_proj):
    arg = dict(locals())
    cx, cy, cc = _coords()
    chip = (2 * cx + cy).astype(jnp.int32).reshape(1)
    core = cc.astype(jnp.int32).reshape(1)
    n_layers = ffn_norm.shape[0]

    quarter = lambda n, i=None: _halves((arg[n] if i is None else arg[n][i]).astype(BF16))
    ffn_names = ('ffn_w_gate', 'ffn_w_up', 'ffn_w_down')
    early = [quarter(n, 0) for n in ffn_names] + [quarter('att_w_in'), quarter('att_w_out')]
    late_names = ('gdn_w_in', 'gdn_w_out', 'ple_w_gate', 'ple_w_proj')
    late = [quarter(n, 1) for n in ffn_names] + [quarter(n) for n in late_names]
    *ffn_w0, att_in_q, att_out_q = _gather_quarters(early, name="gather_weights")
    wt = {'att_w_in': _join_quarters(att_in_q.reshape((N_CHIPS,) + att_w_in.shape[1:]), name="att_w_in_join"),
          'att_w_out': att_out_q.reshape(2, SB_W, D_MODEL)}

    small_names = list(_SMALL_CUT)
    small_shapes = [arg[n].shape for n in small_names]
    svec = _pack([arg[n].reshape(-1) for n in small_names], SMALL_ROW_MULT)
    srows = svec.shape[0]
    sall = _gather_all(svec, name="gather_gains").reshape(N_CHIPS, 2, srows, PACK_W)[:, 0]
    for n, q in zip(small_names, _unpack(sall, small_shapes)):
        wt[n] = _from_quarters(q, _SMALL_CUT[n])
    row = lambda v: v.reshape(1, -1)

    as_ffn = lambda g, n: g.reshape((N_CHIPS,) + arg[n].shape[1:])
    ffn_w = [tuple(as_ffn(g, n) for g, n in zip(ffn_w0, ffn_names)), None]
    h = x[0]
    tape = []
    for i in range(n_layers):
        j = i // 2
        h, s0 = _ffn_fwd(h, row(wt['ffn_norm'][i, 0]), *ffn_w[i], (0,), f"ffn{i}a")
        if i % 2 == 0:
            h, sm, gathered = _att_fwd(h, row(mix_norm[i]), wt['att_w_in'], att_q_norm[j:j + 1], att_k_norm[j:j + 1],
                                       att_sinks[j:j + 1], wt['att_w_out'], late)
            ffn_w[1] = tuple(as_ffn(g, n) for g, n in zip(gathered[:3], ffn_names))
            wq = {n: g.reshape((N_CHIPS,) + arg[n].shape) for n, g in zip(late_names, gathered[3:])}
            wt['gdn_w_in'] = _join_quarters(wq['gdn_w_in'][:, 0], name="gdn_w_in_join")
            wt['gdn_w_out'] = wq['gdn_w_out'].reshape(GDN_VW, D_MODEL)
            wt['ple_w_gate'] = _from_quarters(wq['ple_w_gate'], 1)
            wt['ple_w_proj'] = _from_quarters(wq['ple_w_proj'], 2)
        else:
            h, sm = _gdn_layer_fwd(h, row(mix_norm[i]), wt['gdn_w_in'], wt['gdn_conv_w'][j], gdn_a_log[j:j + 1],
                                   gdn_dt_bias[j:j + 1], gdn_out_norm[j:j + 1], wt['gdn_w_out'])
        h, s1 = _ffn_fwd(h, row(wt['ffn_norm'][i, 1]), *ffn_w[i], (1,), f"ffn{i}b")
        h, sp = _ple_fwd(h, row(ple_norm[i]), wt['ple_w_gate'][i], wt['ple_w_proj'][i], p[i, 0], f"ple{i}")
        tape.append((s0, sm, s1, sp))

    dh, loss_local = _loss_head(h, loss_target[0], name="loss_head")
    loss = lax.psum(loss_local, ("x", "y", "c"))

    gr = {}
    stored_t = ('ffn_w_gate', 'ffn_w_up')
    as_stored = lambda a, n: jnp.swapaxes(a, -1, -2) if n in stored_t else a
    ffn_g = [tuple(lax.empty((N_CHIPS,) + as_stored(arg[n], n).shape[1:], F32) for n in ffn_names)
             for _ in range(n_layers)]
    d_ffn_norm = [[None, None] for _ in range(n_layers)]
    d_mix, d_ple_norm, d_ple_gate, d_ple_proj = [None] * n_layers, [None] * n_layers, [None] * n_layers, [None] * n_layers

    def as_halves(g):
        return g.reshape((N_CHIPS, 2, -1, g.shape[-1]))

    def pair_up(keys, grads, tag):
        got = _swap_halves(grads, name=f"grad_swap_halves_{tag}")
        return [_add_pair(g, o, core, name=f"grad_add_pair_{k}") for k, g, o in zip(keys, grads, got)]

    for i in reversed(range(n_layers)):
        j = i // 2
        s0, sm, s1, sp = tape[i]
        dh, d_ple_norm[i], d_ple_gate[i], d_ple_proj[i] = _ple_bwd(dh, sp, row(ple_norm[i]), wt['ple_w_gate'][i], p[i, 0],
                                                                   f"ple{i}")
        dh, d_ffn_norm[i][1], ffn_g[i] = _ffn_bwd(dh, s1, row(wt['ffn_norm'][i, 1]), *ffn_w[i], (1,), ffn_g[i], (1,),
                                                  f"ffn{i}b")
        if i % 2 == 0:
            gr['ple_w_gate'] = _to_quarters(jnp.stack(d_ple_gate), 1)
            gr['ple_w_proj'] = _to_quarters(jnp.stack(d_ple_proj), 2)
            first_keys = [f"{n}_1" for n in ffn_names] + list(late_names)
            first_pairs = pair_up(first_keys, [as_halves(g) for g in ffn_g[1]] + [as_halves(gr[n]) for n in late_names], "a")
            (dh, d_mix[i], dw_in, gr['att_q_norm'], gr['att_k_norm'], gr['att_sinks'], dw_out,
             first_recv) = _att_bwd(dh, sm, row(mix_norm[i]), wt['att_w_in'], att_q_norm[j:j + 1],
                                    att_k_norm[j:j + 1], att_sinks[j:j + 1], wt['att_w_out'], first_pairs)
            gr['att_w_in'] = _split_quarters(dw_in, name="att_dw_in_split")
            gr['att_w_out'] = dw_out
        else:
            (dh, d_mix[i], dw_in, dconv, gr['gdn_a_log'], gr['gdn_dt_bias'], gr['gdn_out_norm'],
             dw_out) = _gdn_layer_bwd(dh, sm, row(mix_norm[i]), wt['gdn_conv_w'][j], gdn_a_log[j:j + 1],
                                      gdn_dt_bias[j:j + 1], gdn_out_norm[j:j + 1], wt['gdn_w_out'])
            gr['gdn_w_in'] = _split_quarters(dw_in, name="gdn_dw_in_split")
            gr['gdn_w_out'] = dw_out
            gr['gdn_conv_w'] = dconv[None]
        dh, d_ffn_norm[i][0], ffn_g[i] = _ffn_bwd(dh, s0, row(wt['ffn_norm'][i, 0]), *ffn_w[i], (0,), ffn_g[i], (0,),
                                                  f"ffn{i}a")
    grad_x = dh[None]

    gr['ffn_norm'] = jnp.stack([jnp.stack([d_ffn_norm[i][k][0] for k in range(2)]) for i in range(n_layers)])
    gr['mix_norm'] = jnp.concatenate(d_mix, axis=0)
    gr['ple_norm'] = jnp.concatenate(d_ple_norm, axis=0)

    last_keys = [f"{n}_0" for n in ffn_names] + ['att_w_in', 'att_w_out']
    last_pairs = pair_up(last_keys, [as_halves(g) for g in ffn_g[0]] + [as_halves(gr['att_w_in']), as_halves(gr['att_w_out'])],
                         "b")
    last_recv = _scatter_quarters(last_pairs, name="grad_scatter")
    keys = first_keys + last_keys
    tots = [_add_chips(pr, rc, chip, name=f"grad_add_chips_{k}")
            for k, pr, rc in zip(keys, first_pairs + last_pairs, first_recv + last_recv)]
    theirs = _share_halves(tots, name="grad_share")
    summed = dict(zip(keys, zip(tots, theirs)))

    whole_shapes = [arg[n].shape for n in _WHOLE]
    cut_full_shapes = [gr[n].shape for n in small_names]
    gvec = _pack([gr[n].reshape(-1) for n in _WHOLE + small_names], SMALL_ROW_MULT)
    gall = _sum_blocks(_gather_all(gvec, name="gather_small_grads"), N_DEV, name="sum_small_grads")
    parts = _unpack(gall, whole_shapes + cut_full_shapes)
    gsum = dict(zip(_WHOLE, parts))
    for n, g in zip(small_names, parts[len(_WHOLE):]):
        gsum[n] = lax.dynamic_index_in_dim(_to_quarters(g, _SMALL_CUT[n]), chip[0], axis=0, keepdims=False)

    delta, new_m, new_v = {}, {}, {}
    for n in ('att_w_in', 'att_w_out') + late_names:
        res = _adamw_halves(_halves(arg[n]), _halves(arg["m_" + n]), _halves(arg["v_" + n]), *summed[n], core,
                            name=f"adamw_{n}")
        gsum[n], delta[n], new_m[n], new_v[n] = (r.reshape(arg[n].shape) for r in res)
    for n in ffn_names:
        wmv = [as_stored(arg[k + n], n) for k in ("", "m_", "v_")]
        res = tuple(lax.empty(wmv[0].shape, F32) for _ in range(4))
        for i in range(n_layers):
            res = _adamw_halves(*wmv, *summed[f"{n}_{i}"], core, name=f"adamw_{n}_{i}", into=(res, (i,)))
        gsum[n], delta[n], new_m[n], new_v[n] = (as_stored(r, n) for r in res)
    for n in _WHOLE + small_names:
        delta[n], new_m[n], new_v[n] = _adamw(arg[n], gsum[n], arg["m_" + n], arg["v_" + n], name=f"adamw_{n}")
    return (loss, grad_x, *[gsum[n] for n in _WEIGHTS], *[delta[n] for n in _WEIGHTS],
            *[new_m[n] for n in _WEIGHTS], *[new_v[n] for n in _WEIGHTS])
```

```python
import functools
import math

import jax
import jax.numpy as jnp
from jax import lax
from jax.experimental import pallas as pl
from jax.experimental.pallas import tpu as pltpu

F32 = jnp.float32
BF16 = jnp.bfloat16
MESH = pl.DeviceIdType.MESH

LANES = 128
VMEM_LIMIT_BYTES = 56 * 1024 * 1024

EPS = 1e-6
D_MODEL = 1024
HEAD_DIM = 64
SB_HEADS = 8
SWA_HEADS = 8
SWA_KV_HEADS = 2
WINDOW = 128
GDN_K_HEADS = 8
GDN_V_HEADS = 16
GDN_HEAD_DIM = 128
GDN_CONV = 4
GDN_CHUNK = 64
SB_W = SB_HEADS * HEAD_DIM
SWA_QW = SWA_HEADS * HEAD_DIM
SWA_KVW = SWA_KV_HEADS * HEAD_DIM
GDN_KW = GDN_K_HEADS * GDN_HEAD_DIM
GDN_VW = GDN_V_HEADS * GDN_HEAD_DIM
GDN_CONV_W = 2 * GDN_KW + GDN_VW

ADAM_LR = 0.001
ADAM_B1 = 0.9
ADAM_B2 = 0.999
ADAM_EPS = 1e-08
ADAM_WD = 0.01
ADAM_STEP = 10

NEG = -1e30


def _params(*sem):
    return pltpu.CompilerParams(dimension_semantics=sem or None, vmem_limit_bytes=VMEM_LIMIT_BYTES)


def _tile(n, cap, align=LANES):
    if n <= cap:
        return n
    for t in range(cap - cap % align, 0, -align):
        if n % t == 0:
            return t
    return n


N_CHIPS = 4
MM_VMEM_BUDGET_BYTES = 40 * 1024 * 1024
Q = "q"


def _opnd(x):
    return x if isinstance(x, tuple) else (x, ())


def _mm(a, b, *, name, ta=False, tb=False, out_dtype=F32, res=None, scale=1.0, out_q=False, into=None,
        tm=None, tn=1024, tk=1024):
    (a_arr, a_lead), (b_arr, b_lead) = _opnd(a), _opnd(b)
    (k_a, m) = a_arr.shape[-2:] if ta else a_arr.shape[-2:][::-1]
    (n, k_b) = b_arr.shape[-2:] if tb else b_arr.shape[-2:][::-1]
    if into is not None:
        out_arr, out_lead = into
        out_q, out_dtype = Q in out_lead, out_arr.dtype
    else:
        out_lead = (Q,) if out_q else ()
    red_q = (Q in a_lead or Q in b_lead) and not out_q
    kq = min(k_a, k_b)
    assert (k_a == k_b) or (red_q and max(k_a, k_b) == N_CHIPS * kq), (a_arr.shape, b_arr.shape)
    tn, tk = _tile(n, tn), _tile(kq, tk)
    if tm is None:
        r_item = _opnd(res)[0].dtype.itemsize if res is not None else 0
        per_row = 2 * (tk * a_arr.dtype.itemsize + tn * (jnp.dtype(out_dtype).itemsize + r_item)) + 4 * tn
        room = MM_VMEM_BUDGET_BYTES - 2 * tk * tn * b_arr.dtype.itemsize
        tm = next(c for c in (4096, 2048, 1024, 512, 256, 128) if c * per_row <= room or c == 128)
    tm = _tile(m, tm)
    nk = kq // tk
    ksteps = nk * (N_CHIPS if red_q else 1)
    dims = (((0 if ta else 1,), (1 if tb else 0,)), ((), ()))
    has_res = res is not None

    def body(*refs):
        a_ref, b_ref = refs[0], refs[1]
        o_ref, acc_ref = refs[-2], refs[-1]
        k = pl.program_id(3)

        @pl.when(k == 0)
        def _():
            acc_ref[...] = jnp.zeros_like(acc_ref)

        acc_ref[...] += lax.dot_general(a_ref[...].astype(BF16), b_ref[...].astype(BF16), dims,
                                        preferred_element_type=F32)

        @pl.when(k == ksteps - 1)
        def _():
            r = acc_ref[...]
            if scale != 1.0:
                r = r * scale
            if has_res:
                r = r + refs[2][...].astype(F32)
            o_ref[...] = r.astype(o_ref.dtype)

    def spec(lead, blk, pos):
        def index(s, i, j, k):
            kk = k % nk if (red_q and Q in lead) else k
            quarter = s if out_q else k // nk
            return tuple(quarter if l == Q else l for l in lead) + pos(i, j, kk)
        return pl.BlockSpec((None,) * len(lead) + blk, index)

    a_spec = spec(a_lead, (tk, tm), lambda i, j, k: (k, i)) if ta else spec(a_lead, (tm, tk), lambda i, j, k: (i, k))
    b_spec = spec(b_lead, (tn, tk), lambda i, j, k: (j, k)) if tb else spec(b_lead, (tk, tn), lambda i, j, k: (k, j))
    o_spec = spec(out_lead, (tm, tn), lambda i, j, k: (i, j))
    in_specs, args = [a_spec, b_spec], [a_arr, b_arr]
    if has_res:
        r_arr, r_lead = _opnd(res)
        in_specs.append(spec(r_lead, (tm, tn), lambda i, j, k: (i, j)))
        args.append(r_arr)
    aliases = {}
    if into is not None:
        in_specs.append(pl.BlockSpec(memory_space=pl.ANY))
        args.append(out_arr)
        aliases = {len(args) - 1: 0}
        out_shape = jax.ShapeDtypeStruct(out_arr.shape, out_arr.dtype)
    else:
        out_shape = jax.ShapeDtypeStruct(((N_CHIPS,) if out_q else ()) + (m, n), out_dtype)
    return pl.pallas_call(
        body, name=name, grid=(N_CHIPS if out_q else 1, m // tm, n // tn, ksteps), in_specs=in_specs, out_specs=o_spec,
        out_shape=out_shape, scratch_shapes=[pltpu.VMEM((tm, tn), F32)], input_output_aliases=aliases,
        compiler_params=_params("parallel", "parallel", "parallel", "arbitrary"),
    )(*args)


def _row_spec(r, tm):
    if isinstance(r, tuple):
        arr, width, cb = r
        return arr, pl.BlockSpec((tm, width), lambda i, cb=cb: (i, cb))
    return r, pl.BlockSpec((tm, r.shape[1]), lambda i: (i, 0))


def _const_spec(c):
    return pl.BlockSpec(c.shape, lambda i: (0,) * c.ndim)


def _row_fwd(fn, rows, consts, outs, *, name, tm=256):
    tm = _tile(_row_spec(rows[0], tm)[0].shape[0], tm, 8)
    arrs, specs = zip(*[_row_spec(r, tm) for r in rows])
    t = arrs[0].shape[0]
    nr, nc = len(rows), len(consts)

    def body(*refs):
        vals = [r[...].astype(F32) for r in refs[:nr + nc]]
        res = fn(*vals)
        for o_ref, v in zip(refs[nr + nc:], res):
            o_ref[...] = v.astype(o_ref.dtype)

    out = pl.pallas_call(
        body, name=name, grid=(t // tm,),
        in_specs=list(specs) + [_const_spec(c) for c in consts],
        out_specs=[pl.BlockSpec((tm, w), lambda i: (i, 0)) for w, _ in outs],
        out_shape=[jax.ShapeDtypeStruct((t, w), dt) for w, dt in outs],
        compiler_params=_params("parallel"),
    )(*arrs, *consts)
    return list(out)


def _row_bwd(fn, rows, consts, cts, row_grads, const_grads, *, name, tm=256):
    tm = _tile(_row_spec(rows[0], tm)[0].shape[0], tm, 8)
    arrs, specs = zip(*[_row_spec(r, tm) for r in rows])
    ct_arrs, ct_specs = zip(*[_row_spec(r, tm) for r in cts])
    t = arrs[0].shape[0]
    nr, nc, nt = len(rows), len(consts), len(cts)
    n_in = nr + nc + nt

    def body(*refs):
        vals = [r[...].astype(F32) for r in refs[:nr + nc]]
        ctv = tuple(r[...].astype(F32) for r in refs[nr + nc:n_in])
        _, vjp = jax.vjp(fn, *vals)
        g = vjp(ctv)
        outs = refs[n_in:]
        for (idx, _), o_ref in zip(row_grads, outs[:len(row_grads)]):
            o_ref[...] = g[idx].astype(o_ref.dtype)
        first = pl.program_id(0) == 0
        for ci, o_ref in zip(const_grads, outs[len(row_grads):]):
            @pl.when(first)
            def _(o_ref=o_ref):
                o_ref[...] = jnp.zeros_like(o_ref)

            o_ref[...] += g[nr + ci]

    widths = [(_row_spec(rows[idx], tm)[1].block_shape[1], dt) for idx, dt in row_grads]
    out = pl.pallas_call(
        body, name=name, grid=(t // tm,),
        in_specs=list(specs) + [_const_spec(c) for c in consts] + list(ct_specs),
        out_specs=[pl.BlockSpec((tm, w), lambda i: (i, 0)) for w, _ in widths]
        + [_const_spec(consts[ci]) for ci in const_grads],
        out_shape=[jax.ShapeDtypeStruct((t, w), dt) for w, dt in widths]
        + [jax.ShapeDtypeStruct(consts[ci].shape, F32) for ci in const_grads],
        compiler_params=_params("arbitrary"),
    )(*arrs, *consts, *ct_arrs)
    return list(out)


def _rms(x, g):
    return x * lax.rsqrt(jnp.mean(x * x, axis=-1, keepdims=True) + EPS) * g


def _f_rms(h, g):
    return (_rms(h, g),)


def _f_rms_res(h, g):
    return (_rms(h, g), h)


def _f_swiglu(g, u):
    return (g * jax.nn.sigmoid(g) * u,)


def _f_ple(h, gl, pp):
    return (h + jax.nn.sigmoid(gl) * pp,)


def _f_gdn_out(o, z, gain):
    outs = []
    for hd in range(GDN_V_HEADS):
        sl = slice(hd * GDN_HEAD_DIM, (hd + 1) * GDN_HEAD_DIM)
        oh, zh = o[:, sl], z[:, sl]
        outs.append(_rms(oh, gain) * (zh * jax.nn.sigmoid(zh)))
    return (jnp.concatenate(outs, axis=1),)


def _loss_head(y, target, *, name, tm=512):
    t, d = y.shape
    tm = _tile(t, tm, 8)

    def body(y_ref, t_ref, dy_ref, l_ref):
        @pl.when(pl.program_id(0) == 0)
        def _():
            l_ref[...] = jnp.zeros_like(l_ref)

        e = y_ref[...] - t_ref[...]
        dy_ref[...] = e * (1.0 / d)
        l_ref[...] += jnp.sum(e * e) * (0.5 / d)

    dy, l = pl.pallas_call(
        body, name=name, grid=(t // tm,),
        in_specs=[pl.BlockSpec((tm, d), lambda i: (i, 0))] * 2,
        out_specs=[pl.BlockSpec((tm, d), lambda i: (i, 0)), pl.BlockSpec((8, LANES), lambda i: (0, 0))],
        out_shape=[jax.ShapeDtypeStruct((t, d), F32), jax.ShapeDtypeStruct((8, LANES), F32)],
        compiler_params=_params("arbitrary"),
    )(y, target)
    return dy, l[0, 0]


def _dg(a, b, ca, cb):
    nb = a.ndim - 2
    batch = tuple(range(nb))
    return lax.dot_general(a, b, (((ca + nb,), (cb + nb,)), (batch, batch)), preferred_element_type=F32)


def _b(x):
    return x.astype(BF16)


@jax.custom_vjp
def _bdot(a, b):
    return _dg(_b(a), _b(b), 1, 0)


def _bdot_fwd(a, b):
    return _bdot(a, b), (a, b)


def _bdot_bwd(r, ct):
    a, b = r
    return _dg(_b(ct), _b(b), 1, 1), _dg(_b(a), _b(ct), 0, 0)


_bdot.defvjp(_bdot_fwd, _bdot_bwd)


@jax.custom_vjp
def _bdot_nt(a, b):
    return _dg(_b(a), _b(b), 1, 1)


def _bdot_nt_fwd(a, b):
    return _bdot_nt(a, b), (a, b)


def _bdot_nt_bwd(r, ct):
    a, b = r
    return _dg(_b(ct), _b(b), 1, 0), _dg(_b(ct), _b(a), 0, 0)


_bdot_nt.defvjp(_bdot_nt_fwd, _bdot_nt_bwd)


@jax.custom_vjp
def _bdot_tn(a, b):
    return _dg(_b(a), _b(b), 0, 0)


def _bdot_tn_fwd(a, b):
    return _bdot_tn(a, b), (a, b)


def _bdot_tn_bwd(r, ct):
    a, b = r
    return _dg(_b(b), _b(ct), 1, 1), _dg(_b(a), _b(ct), 1, 0)


_bdot_tn.defvjp(_bdot_tn_fwd, _bdot_tn_bwd)


def _two(x):
    hi = x.astype(BF16)
    return hi, (x - hi.astype(F32)).astype(BF16)


def _dg3(a, b, ca, cb):
    (ah, al), (bh, bl) = _two(a), _two(b)
    return _dg(ah, bh, ca, cb) + (_dg(ah, bl, ca, cb) + _dg(al, bh, ca, cb))


@jax.custom_vjp
def _hdot(a, b):
    return _dg3(a, b, 1, 0)


def _hdot_fwd(a, b):
    return _hdot(a, b), (a, b)


def _hdot_bwd(r, ct):
    a, b = r
    return _dg3(ct, b, 1, 1), _dg3(a, ct, 0, 0)


_hdot.defvjp(_hdot_fwd, _hdot_bwd)


@jax.custom_vjp
def _unit_lower_inverse(x):
    c = x.shape[-1]
    eye = (lax.broadcasted_iota(jnp.int32, x.shape, 1) == lax.broadcasted_iota(jnp.int32, x.shape, 2)).astype(F32)
    inv, pw = eye + x, x
    for _ in range(int(math.log2(c)) - 1):
        pw = _dg3(pw, pw, 1, 0)
        inv = inv + _dg3(inv, pw, 1, 0)
    return inv


def _unit_lower_inverse_fwd(x):
    inv = _unit_lower_inverse(x)
    return inv, inv


def _unit_lower_inverse_bwd(inv, ct):
    return (_dg3(_dg3(inv, ct, 0, 0), inv, 1, 1),)


_unit_lower_inverse.defvjp(_unit_lower_inverse_fwd, _unit_lower_inverse_bwd)


@jax.custom_vjp
def _known_inverse(x, inv):
    return inv


def _known_inverse_fwd(x, inv):
    return inv, inv


def _known_inverse_bwd(inv, ct):
    return _dg3(_dg3(inv, ct, 0, 0), inv, 1, 1), jnp.zeros_like(inv)


_known_inverse.defvjp(_known_inverse_fwd, _known_inverse_bwd)


def _split_dot(x, u):
    hi, lo = _two(x)
    return _dg(hi, u, 1, 0) + _dg(lo, u, 1, 0)


@jax.custom_vjp
def _ldot(l01, x):
    hi, lo = _two(x)
    l01 = l01.astype(BF16)
    return _dg(l01, hi, 1, 0) + _dg(l01, lo, 1, 0)


def _ldot_fwd(l01, x):
    return _ldot(l01, x), l01


def _ldot_bwd(l01, ct):
    hi, lo = _two(ct)
    l01b = l01.astype(BF16)
    return jnp.zeros_like(l01), _dg(l01b, hi, 0, 0) + _dg(l01b, lo, 0, 0)


_ldot.defvjp(_ldot_fwd, _ldot_bwd)


SB_BLK = 128
SB_KEYS = 512
SB_PAIRS = 2
SB_SCALE = HEAD_DIM ** -0.5


def _log_sigmoid(z):
    return jnp.minimum(z, 0.0) - jnp.log(1.0 + jnp.exp(-jnp.abs(z)))


def _sb_consts(t):
    kb = min(SB_KEYS, t)
    nh = 2 * SB_PAIRS
    lane = lax.broadcasted_iota(jnp.int32, (nh, SB_BLK, kb), 2)
    row = lax.broadcasted_iota(jnp.int32, (nh, SB_BLK, kb), 1)
    ur = lax.broadcasted_iota(jnp.int32, (kb, kb), 0)
    uc = lax.broadcasted_iota(jnp.int32, (kb, kb), 1)
    return kb, nh, lane, row, ur, uc


def _sb_heads(x):
    head0 = lax.broadcasted_iota(jnp.int32, (x.shape[0], LANES), 1) < HEAD_DIM
    out = []
    for p in range(SB_PAIRS):
        blk = x[:, p * LANES:(p + 1) * LANES]
        out += [jnp.where(head0, blk, 0.0), jnp.where(head0, 0.0, blk)]
    return jnp.stack(out)


def _sb_pairs(x):
    return jnp.stack([x[:, (h // 2) * LANES:(h // 2 + 1) * LANES] for h in range(2 * SB_PAIRS)])


def _sb_merge(x):
    head0 = lax.broadcasted_iota(jnp.int32, (x.shape[1], LANES), 1) < HEAD_DIM
    return jnp.concatenate([jnp.where(head0, x[2 * p], x[2 * p + 1]) for p in range(SB_PAIRS)], axis=1)


def _sb_rows_dot(x, u):
    nh, rows, k = x.shape
    return _split_dot(x.reshape(nh * rows, k), u).reshape(nh, rows, k)


def _sb_fwd(proj, *, name, gather=()):
    t = proj.shape[0]
    nb = t // SB_BLK
    width = SB_PAIRS * LANES
    ng = SB_W // width
    na = len(gather)

    def body(q_ref, k_ref, v_ref, *rest):
        o_ref, r_ref = rest[na:na + 2]
        i = pl.program_id(1)
        if na:
            step = pl.program_id(0) * nb + i
            copies = lambda **kw: _gather_copies(rest[:na], rest[na + 2:2 * na + 2], *rest[2 * na + 2:], **kw)
            pl.when(step == 0)(lambda: _gather_start(copies(only_first=True)))
        kb, nh, lane, row, ur, uc = _sb_consts(t)
        u_suffix = (ur >= uc).astype(BF16)
        qh = _b(_sb_heads(q_ref[...]) * SB_SCALE)
        diag = (i * SB_BLK) // kb

        def block(j, carry, masked):
            acc, car = carry
            keys = pl.ds(pl.multiple_of(j * kb, kb), kb)
            kj, vj = _b(_sb_pairs(k_ref[keys, :])), _b(_sb_pairs(v_ref[keys, :]))
            z = _dg(qh, kj, 1, 1)
            lk = _log_sigmoid(-z)
            if masked:
                causal = (j * kb + lane) < (i * SB_BLK + row)
                lk = jnp.where(causal, lk, 0.0)
            suf = _sb_rows_dot(lk, u_suffix) + car
            w = jnp.exp(z + suf)
            if masked:
                w = jnp.where(causal, w, 0.0)
            return acc + _dg(_b(w), vj, 1, 0), suf[:, :, 0:1]

        zero = (jnp.zeros((nh, SB_BLK, LANES), F32), jnp.zeros((nh, SB_BLK, 1), F32))
        carry = block(diag, zero, True)
        acc, car = lax.fori_loop(0, diag, lambda s, c: block(diag - 1 - s, c, False), carry)
        o_ref[...] = _sb_merge(acc)
        r_ref[...] = _sb_merge(jnp.broadcast_to(car, (nh, SB_BLK, LANES)))
        if na:
            pl.when(step == ng * nb - 1)(lambda: _gather_finish(copies()))

    return pl.pallas_call(
        body, name=name, grid=(ng, nb),
        in_specs=[pl.BlockSpec((SB_BLK, width), lambda p, i: (i, p)),
                  pl.BlockSpec((t, width), lambda p, i: (0, ng + p)),
                  pl.BlockSpec((t, width), lambda p, i: (0, 2 * ng + p))] + [ANY] * na,
        out_specs=[pl.BlockSpec((SB_BLK, width), lambda p, i: (i, p))] * 2 + [ANY] * na,
        out_shape=[jax.ShapeDtypeStruct((t, SB_W), F32)] * 2
        + [jax.ShapeDtypeStruct((N_CHIPS,) + g.shape, g.dtype) for g in gather],
        scratch_shapes=_gather_scratch(na) if na else [],
        compiler_params=_params("arbitrary", "arbitrary"),
    )(proj, proj, proj, *gather)


def _sb_bwd(proj, rtot, dout, *, name, scatter=()):
    t = proj.shape[0]
    nb = t // SB_BLK
    width = SB_PAIRS * LANES
    ng = SB_W // width
    na = len(scatter)

    def body(q_ref, k_ref, v_ref, r_ref, do_ref, *rest):
        dq_ref, dk_ref, dv_ref = rest[na:na + 3]
        i = pl.program_id(1)
        if na:
            step = pl.program_id(0) * nb + i
            copies = lambda: _scatter_copies(rest[:na], rest[na + 3:4 * na + 3], *rest[4 * na + 3:])
            pl.when(step == 0)(lambda: [cp.start() for cp in copies()] and None)
        kb, nh, lane, row, ur, uc = _sb_consts(t)
        u_incl = (ur <= uc).astype(BF16)
        u_excl = (ur < uc).astype(BF16)
        q, do = q_ref[...], do_ref[...]
        qh, doh = _b(_sb_heads(q) * SB_SCALE), _b(_sb_heads(do))
        qb, dob = _b(_sb_pairs(q) * SB_SCALE), _b(_sb_pairs(do))
        rh = jnp.min(_sb_heads(r_ref[...]), axis=2, keepdims=True)
        diag = (i * SB_BLK) // kb

        @pl.when(i == 0)
        def _():
            dk_ref[...] = jnp.zeros_like(dk_ref)
            dv_ref[...] = jnp.zeros_like(dv_ref)

        def block(j, carry, masked):
            dq_acc, clk, ce = carry
            keys = pl.ds(pl.multiple_of(j * kb, kb), kb)
            kj, vj = _b(_sb_pairs(k_ref[keys, :])), _b(_sb_pairs(v_ref[keys, :]))
            z = _dg(qh, kj, 1, 1)
            lk = _log_sigmoid(-z)
            ls = z + lk
            if masked:
                causal = (j * kb + lane) < (i * SB_BLK + row)
                lk = jnp.where(causal, lk, 0.0)
            pre = _sb_rows_dot(lk, u_incl) + clk
            w = jnp.exp(ls + (rh - pre))
            if masked:
                w = jnp.where(causal, w, 0.0)
            e = _dg(doh, vj, 1, 1) * w
            pre_e = _sb_rows_dot(e, u_excl) + ce
            sig = jnp.exp(ls)
            dz = e - sig * (e + pre_e)
            if masked:
                dz = jnp.where(causal, dz, 0.0)
            dzb = _b(dz)
            dk_ref[keys, :] += _sb_merge(_dg(dzb, qb, 0, 0))
            dv_ref[keys, :] += _sb_merge(_dg(_b(w), dob, 0, 0))
            return dq_acc + _dg(dzb, kj, 1, 0), pre[:, :, kb - 1:], pre_e[:, :, kb - 1:] + e[:, :, kb - 1:]

        zero = (jnp.zeros((nh, SB_BLK, LANES), F32), jnp.zeros((nh, SB_BLK, 1), F32), jnp.zeros((nh, SB_BLK, 1), F32))
        carry = lax.fori_loop(0, diag, lambda j, c: block(j, c, False), zero)
        dq_acc, _, _ = block(diag, carry, True)
        dq_ref[...] = _sb_merge(dq_acc) * SB_SCALE
        if na:
            pl.when(step == ng * nb - 1)(lambda: [cp.wait() for cp in copies()] and None)

    blk = pl.BlockSpec((SB_BLK, width), lambda p, i: (i, p))
    whole = pl.BlockSpec((t, width), lambda p, i: (0, p))
    return pl.pallas_call(
        body, name=name, grid=(ng, nb),
        in_specs=[blk,
                  pl.BlockSpec((t, width), lambda p, i: (0, ng + p)),
                  pl.BlockSpec((t, width), lambda p, i: (0, 2 * ng + p)),
                  blk, blk] + [ANY] * na,
        out_specs=[blk, whole, whole] + [ANY] * (3 * na),
        out_shape=[jax.ShapeDtypeStruct((t, SB_W), F32)] * 3 + _scatter_shapes(scatter),
        scratch_shapes=_dma_sems(3 * na) if na else [],
        compiler_params=_params("arbitrary", "arbitrary"),
    )(proj, proj, proj, rtot, dout, *scatter)


SWA_G = SWA_HEADS // SWA_KV_HEADS


def _swa_heads(first, qs, ks, vs, qg, kg, sinks):
    shape = (SWA_HEADS, WINDOW, 2 * WINDOW)
    qi = lax.broadcasted_iota(jnp.int32, shape, 1)
    kj = lax.broadcasted_iota(jnp.int32, shape, 2)
    dist = qi + WINDOW - kj
    valid = (dist >= 0) & (dist < WINDOW) & (jnp.logical_not(first) | (kj >= WINDOW))
    head = lax.broadcasted_iota(jnp.int32, (SWA_HEADS, 1, 1), 0)
    slope = sum(jnp.where(head == h, 2.0 ** (-8.0 * (h + 1) / SWA_HEADS), 0.0) for h in range(SWA_HEADS))
    kn = _rms(ks, kg)
    per_q_head = lambda x: jnp.concatenate([x[h // SWA_G:h // SWA_G + 1] for h in range(SWA_HEADS)], axis=0)
    k8, v8 = per_q_head(kn), per_q_head(vs)
    s = _bdot_nt(_rms(qs, qg), k8) * (HEAD_DIM ** -0.5)
    s = jnp.where(valid, s - slope * dist.astype(F32), NEG)
    m = lax.stop_gradient(jnp.maximum(jnp.max(s, axis=2, keepdims=True), sinks))
    p = jnp.exp(s - m)
    den = jnp.sum(p, axis=2, keepdims=True) + jnp.exp(sinks - m)
    return _bdot(p / den, v8)


def _swa_split(q, kp, kc, vp, vc, sk):
    lanes = lambda x, n: jnp.stack([x[:, h * HEAD_DIM:(h + 1) * HEAD_DIM].astype(F32) for h in range(n)])
    k2, v2 = jnp.concatenate([kp, kc], axis=0), jnp.concatenate([vp, vc], axis=0)
    sinks = jnp.stack([sk[:, h:h + 1] for h in range(SWA_HEADS)])
    return lanes(q, SWA_HEADS), lanes(k2, SWA_KV_HEADS), lanes(v2, SWA_KV_HEADS), sinks


def _swa_join(x):
    return jnp.concatenate([x[h] for h in range(x.shape[0])], axis=1)


def _swa_specs(t):
    qcb = (3 * SB_W) // SWA_QW
    kcb = (3 * SB_W + SWA_QW) // SWA_KVW
    prev = lambda i: jnp.maximum(i - 1, 0)
    return [pl.BlockSpec((WINDOW, SWA_QW), lambda i: (i, qcb)),
            pl.BlockSpec((WINDOW, SWA_KVW), lambda i: (prev(i), kcb)),
            pl.BlockSpec((WINDOW, SWA_KVW), lambda i: (i, kcb)),
            pl.BlockSpec((WINDOW, SWA_KVW), lambda i: (prev(i), kcb + 1)),
            pl.BlockSpec((WINDOW, SWA_KVW), lambda i: (i, kcb + 1)),
            pl.BlockSpec((1, HEAD_DIM), lambda i: (0, 0)),
            pl.BlockSpec((1, HEAD_DIM), lambda i: (0, 0)),
            pl.BlockSpec((1, SWA_HEADS), lambda i: (0, 0))]


def _swa_fwd(proj, qg, kg, sinks, *, name):
    t = proj.shape[0]

    def body(q_ref, kp_ref, kc_ref, vp_ref, vc_ref, qg_ref, kg_ref, sk_ref, o_ref):
        first = pl.program_id(0) == 0
        qs, ks, vs, sk = _swa_split(q_ref[...], kp_ref[...], kc_ref[...], vp_ref[...], vc_ref[...], sk_ref[...])
        o_ref[...] = _swa_join(_swa_heads(first, qs, ks, vs, qg_ref[...], kg_ref[...], sk))

    return pl.pallas_call(
        body, name=name, grid=(t // WINDOW,), in_specs=_swa_specs(t),
        out_specs=pl.BlockSpec((WINDOW, SWA_QW), lambda i: (i, 0)),
        out_shape=jax.ShapeDtypeStruct((t, SWA_QW), F32),
        compiler_params=_params("parallel"),
    )(proj, proj, proj, proj, proj, qg, kg, sinks)


def _swa_bwd(proj, qg, kg, sinks, dout, *, name):
    t = proj.shape[0]

    def body(q_ref, kp_ref, kc_ref, vp_ref, vc_ref, qg_ref, kg_ref, sk_ref, do_ref,
             dq_ref, dk_ref, dv_ref, dqg_ref, dkg_ref, dsk_ref):
        i = pl.program_id(0)
        first = i == 0

        @pl.when(first)
        def _():
            for r in (dk_ref, dv_ref, dqg_ref, dkg_ref, dsk_ref):
                r[...] = jnp.zeros_like(r)

        qs, ks, vs, sk = _swa_split(q_ref[...], kp_ref[...], kc_ref[...], vp_ref[...], vc_ref[...], sk_ref[...])
        do = do_ref[...]
        cts = jnp.stack([do[:, h * HEAD_DIM:(h + 1) * HEAD_DIM] for h in range(SWA_HEADS)])
        _, vjp = jax.vjp(functools.partial(_swa_heads, first), qs, ks, vs, qg_ref[...], kg_ref[...], sk)
        dqs, dks, dvs, dqg, dkg, dsk = vjp(cts)
        dq_ref[...] = _swa_join(dqs)
        dk2, dv2 = _swa_join(dks), _swa_join(dvs)
        cur = pl.ds(pl.multiple_of(i * WINDOW, WINDOW), WINDOW)
        prv = pl.ds(pl.multiple_of(jnp.maximum(i - 1, 0) * WINDOW, WINDOW), WINDOW)
        dk_ref[prv, :] += dk2[:WINDOW]
        dv_ref[prv, :] += dv2[:WINDOW]
        dk_ref[cur, :] += dk2[WINDOW:]
        dv_ref[cur, :] += dv2[WINDOW:]
        dqg_ref[...] += dqg
        dkg_ref[...] += dkg
        dsk_ref[...] += _swa_join(dsk)

    whole = lambda shape: pl.BlockSpec(shape, lambda i: (0, 0))
    return pl.pallas_call(
        body, name=name, grid=(t // WINDOW,),
        in_specs=_swa_specs(t) + [pl.BlockSpec((WINDOW, SWA_QW), lambda i: (i, 0))],
        out_specs=[pl.BlockSpec((WINDOW, SWA_QW), lambda i: (i, 0)), whole((t, SWA_KVW)), whole((t, SWA_KVW)),
                   whole((1, HEAD_DIM)), whole((1, HEAD_DIM)), whole((1, SWA_HEADS))],
        out_shape=[jax.ShapeDtypeStruct((t, SWA_QW), F32), jax.ShapeDtypeStruct((t, SWA_KVW), F32),
                   jax.ShapeDtypeStruct((t, SWA_KVW), F32), jax.ShapeDtypeStruct((1, HEAD_DIM), F32),
                   jax.ShapeDtypeStruct((1, HEAD_DIM), F32), jax.ShapeDtypeStruct((1, SWA_HEADS), F32)],
        compiler_params=_params("arbitrary"),
    )(proj, proj, proj, proj, proj, qg, kg, sinks, dout)


CONV_CB = 512
CONV_TM = 512
HALO = 8


def _conv_pre(x_ref, h_ref, w_ref, i):
    halo = jnp.where(i > 0, h_ref[...], 0.0)
    xe = jnp.concatenate([halo, x_ref[...]], axis=0)
    tm = x_ref.shape[0]
    w = w_ref[...]
    c = sum(w[k:k + 1, :] * xe[HALO - (GDN_CONV - 1) + k:HALO - (GDN_CONV - 1) + k + tm] for k in range(GDN_CONV))
    return c, xe


def _conv_specs(tm, cb):
    return [pl.BlockSpec((tm, cb), lambda c, i: (i, c)),
            pl.BlockSpec((HALO, cb), lambda c, i: (jnp.maximum(i * (tm // HALO) - 1, 0), c)),
            pl.BlockSpec((GDN_CONV, cb), lambda c, i: (0, c))]


def _conv_fwd(x, w, dact=None, *, name):
    t, ch = x.shape
    tm, cb = _tile(t, CONV_TM), _tile(ch, CONV_CB)

    def body(*refs):
        x_ref, h_ref, w_ref = refs[:3]
        c, _ = _conv_pre(x_ref, h_ref, w_ref, pl.program_id(1))
        sig = jax.nn.sigmoid(c)
        if dact is None:
            refs[3][...] = c * sig
        else:
            refs[4][...] = refs[3][...] * (sig * (1.0 + c * (1.0 - sig)))

    tile = pl.BlockSpec((tm, cb), lambda c, i: (i, c))
    extra = () if dact is None else (dact,)
    return pl.pallas_call(
        body, name=name, grid=(ch // cb, t // tm),
        in_specs=_conv_specs(tm, cb) + [tile] * len(extra), out_specs=tile,
        out_shape=jax.ShapeDtypeStruct((t, ch), F32),
        compiler_params=_params("parallel", "parallel"),
    )(x, x, w, *extra)


def _conv_bwd(x, w, dc, *, name):
    t, ch = x.shape
    tm, cb = _tile(t, CONV_TM), _tile(ch, CONV_CB)
    nt = t // tm

    def body(x_ref, h_ref, w_ref, dc_ref, nh_ref, dx_ref, dw_ref):
        i = pl.program_id(1)

        @pl.when(i == 0)
        def _():
            dw_ref[...] = jnp.zeros_like(dw_ref)

        halo = jnp.where(i > 0, h_ref[...], 0.0)
        xe = jnp.concatenate([halo, x_ref[...]], axis=0)
        dc = dc_ref[...]
        dce = jnp.concatenate([dc, jnp.where(i < nt - 1, nh_ref[...], 0.0)], axis=0)
        w = w_ref[...]
        last = GDN_CONV - 1
        dx_ref[...] = sum(w[k:k + 1, :] * dce[last - k:last - k + tm] for k in range(GDN_CONV))
        dw_ref[...] += jnp.concatenate(
            [jnp.sum(dc * xe[HALO - last + k:HALO - last + k + tm], axis=0, keepdims=True) for k in range(GDN_CONV)],
            axis=0)

    tile = pl.BlockSpec((tm, cb), lambda c, i: (i, c))
    nxt = pl.BlockSpec((HALO, cb), lambda c, i: (jnp.minimum((i + 1) * (tm // HALO), t // HALO - 1), c))
    return pl.pallas_call(
        body, name=name, grid=(ch // cb, nt),
        in_specs=_conv_specs(tm, cb) + [tile, nxt],
        out_specs=[tile, pl.BlockSpec((GDN_CONV, cb), lambda c, i: (0, c))],
        out_shape=[jax.ShapeDtypeStruct((t, ch), F32), jax.ShapeDtypeStruct((GDN_CONV, ch), F32)],
        compiler_params=_params("parallel", "arbitrary"),
    )(x, x, w, dc, dc)


def _gdn_chunk(qraw, kraw, v, bl, a, alog, dtb, state, inverse=None, keep_inverse=False):
    c, d = GDN_CHUNK, GDN_HEAD_DIM
    nh = qraw.shape[0]
    ri = lax.broadcasted_iota(jnp.int32, (nh, c, c), 1)
    ci = lax.broadcasted_iota(jnp.int32, (nh, c, c), 2)
    incl, strict = ri >= ci, ri > ci
    q = qraw * lax.rsqrt(jnp.sum(qraw * qraw, axis=-1, keepdims=True) + EPS) * (d ** -0.5)
    k = kraw * lax.rsqrt(jnp.sum(kraw * kraw, axis=-1, keepdims=True) + EPS)
    beta = jax.nn.sigmoid(bl)
    g = -jnp.exp(alog) * jax.nn.softplus(a + dtb)
    gc = _ldot(incl.astype(F32), jnp.broadcast_to(g, (nh, c, d)))
    gcm = gc[:, :, :c]
    decay = jnp.exp(jnp.where(incl, gcm - jnp.swapaxes(gcm, 1, 2), NEG))
    eg = jnp.exp(gc)
    kbeta = k * beta
    x = -jnp.where(strict, _bdot_nt(kbeta, k) * decay, 0.0)
    tinv = _unit_lower_inverse(x) if inverse is None else _known_inverse(x, inverse)
    u = _hdot(tinv, v * beta)
    w = _hdot(tinv, kbeta * eg)
    attn = jnp.where(incl, _bdot_nt(q, k) * decay, 0.0)
    glast = gc[:, c - 1:c, :]
    v_new = u - _bdot(w, state)
    o = _bdot(q * eg, state) + _bdot(attn, v_new)
    state = state * jnp.exp(glast) + _bdot_tn(k * jnp.exp(glast - gc), v_new)
    return (o, state, tinv) if keep_inverse else (o, state)


GDN_REP = GDN_V_HEADS // GDN_K_HEADS
GDN_HB = 8


def _gdn_pick(vals, kh, r):
    ba, alog, dtb = vals
    lane = lax.broadcasted_iota(jnp.int32, ba.shape, 1)
    hv = kh * GDN_REP + r
    bl = jnp.sum(jnp.where(lane == hv, ba, 0.0), axis=1, keepdims=True)
    a = jnp.sum(jnp.where(lane == GDN_V_HEADS + hv, ba, 0.0), axis=1, keepdims=True)
    lane1 = lax.broadcasted_iota(jnp.int32, alog.shape, 1)
    al = jnp.sum(jnp.where(lane1 == hv, alog, 0.0), axis=1, keepdims=True)
    db = jnp.sum(jnp.where(lane1 == hv, dtb, 0.0), axis=1, keepdims=True)
    return bl, a, al, db


def _gdn_stack(qs, ks, vs, small, j):
    d = GDN_HEAD_DIM
    per = [[], [], [], [], [], [], []]
    for hh in range(GDN_HB):
        q, k = qs[:, hh * d:(hh + 1) * d], ks[:, hh * d:(hh + 1) * d]
        for r in range(GDN_REP):
            col = (hh * GDN_REP + r) * d
            for lst, val in zip(per, (q, k, vs[:, col:col + d]) + _gdn_pick(small, j * GDN_HB + hh, r)):
                lst.append(val)
    return tuple(jnp.stack(lst) for lst in per)


def _gdn_specs(nchunk, rev):
    c, d = GDN_CHUNK, GDN_HEAD_DIM
    at = (lambda n: nchunk - 1 - n) if rev else (lambda n: n)
    ng = GDN_K_HEADS // GDN_HB
    return at, [pl.BlockSpec((c, GDN_HB * d), lambda n, j: (at(n), j)),
                pl.BlockSpec((c, GDN_HB * d), lambda n, j: (at(n), ng + j)),
                pl.BlockSpec((c, GDN_HB * GDN_REP * d), lambda n, j: (at(n), ng + j)),
                pl.BlockSpec((c, 2 * GDN_V_HEADS), lambda n, j: (at(n), 0)),
                pl.BlockSpec((1, GDN_V_HEADS), lambda n, j: (0, 0)),
                pl.BlockSpec((1, GDN_V_HEADS), lambda n, j: (0, 0))]


def _gdn_fwd(act, ba, alog, dtb, *, name):
    t = act.shape[0]
    c, d = GDN_CHUNK, GDN_HEAD_DIM
    nchunk = t // c
    at, specs = _gdn_specs(nchunk, False)

    def body(q_ref, k_ref, v_ref, ba_ref, al_ref, db_ref, o_ref, s_ref, inv_ref, state):
        n, j = pl.program_id(0), pl.program_id(1)
        heads = pl.ds(j * GDN_HB, GDN_HB)

        @pl.when(n == 0)
        def _():
            state[heads] = jnp.zeros((GDN_HB, GDN_REP, d, d), F32)

        s_in = state[heads]
        s_ref[...] = s_in
        args = _gdn_stack(q_ref[...], k_ref[...], v_ref[...], (ba_ref[...], al_ref[...], db_ref[...]), j)
        o, s_new, inv_ref[...] = _gdn_chunk(*args, s_in.reshape(GDN_HB * GDN_REP, d, d), keep_inverse=True)
        o_ref[...] = jnp.concatenate([o[b] for b in range(GDN_HB * GDN_REP)], axis=1)
        state[heads] = s_new.reshape(GDN_HB, GDN_REP, d, d)

    return pl.pallas_call(
        body, name=name, grid=(nchunk, GDN_K_HEADS // GDN_HB), in_specs=specs,
        out_specs=[pl.BlockSpec((c, GDN_HB * GDN_REP * d), lambda n, j: (n, j)),
                   pl.BlockSpec((None, GDN_HB, GDN_REP, d, d), lambda n, j: (n, j, 0, 0, 0)),
                   pl.BlockSpec((None, GDN_HB * GDN_REP, c, c), lambda n, j: (n, j, 0, 0))],
        out_shape=[jax.ShapeDtypeStruct((t, GDN_VW), F32),
                   jax.ShapeDtypeStruct((nchunk, GDN_K_HEADS, GDN_REP, d, d), F32),
                   jax.ShapeDtypeStruct((nchunk, GDN_V_HEADS, c, c), F32)],
        scratch_shapes=[pltpu.VMEM((GDN_K_HEADS, GDN_REP, d, d), F32)],
        compiler_params=_params("arbitrary", "arbitrary"),
    )(act, act, act, ba, alog, dtb)


def _gdn_bwd(act, ba, alog, dtb, states, inverses, dout, *, name):
    t = act.shape[0]
    c, d = GDN_CHUNK, GDN_HEAD_DIM
    nchunk = t // c
    at, specs = _gdn_specs(nchunk, True)

    def body(q_ref, k_ref, v_ref, ba_ref, al_ref, db_ref, s_ref, inv_ref, do_ref,
             dq_ref, dk_ref, dv_ref, dba_ref, dal_ref, ddb_ref, dstate):
        n, j = pl.program_id(0), pl.program_id(1)

        @pl.when(n == 0)
        def _():
            dstate[pl.ds(j * GDN_HB, GDN_HB)] = jnp.zeros((GDN_HB, GDN_REP, d, d), F32)

        @pl.when((n == 0) & (j == 0))
        def _():
            dal_ref[...] = jnp.zeros_like(dal_ref)
            ddb_ref[...] = jnp.zeros_like(ddb_ref)

        @pl.when(j == 0)
        def _():
            dba_ref[...] = jnp.zeros_like(dba_ref)

        heads = pl.ds(j * GDN_HB, GDN_HB)
        nh = GDN_HB * GDN_REP
        args = _gdn_stack(q_ref[...], k_ref[...], v_ref[...], (ba_ref[...], al_ref[...], db_ref[...]), j)
        _, vjp = jax.vjp(functools.partial(_gdn_chunk, inverse=inv_ref[...]), *args, s_ref[...].reshape(nh, d, d))
        do = do_ref[...]
        do = jnp.stack([do[:, b * d:(b + 1) * d] for b in range(nh)])
        gq, gk, gv, gbl, ga, gal, gdb, gs = vjp((do, dstate[heads].reshape(nh, d, d)))
        dstate[heads] = gs.reshape(GDN_HB, GDN_REP, d, d)
        dq_ref[...] = jnp.concatenate([gq[GDN_REP * hh] + gq[GDN_REP * hh + 1] for hh in range(GDN_HB)], axis=1)
        dk_ref[...] = jnp.concatenate([gk[GDN_REP * hh] + gk[GDN_REP * hh + 1] for hh in range(GDN_HB)], axis=1)
        dv_ref[...] = jnp.concatenate([gv[b] for b in range(nh)], axis=1)
        lane = lax.broadcasted_iota(jnp.int32, (c, 2 * GDN_V_HEADS), 1)
        lane1 = lax.broadcasted_iota(jnp.int32, (1, GDN_V_HEADS), 1)
        dba = jnp.zeros((c, 2 * GDN_V_HEADS), F32)
        dal = jnp.zeros((1, GDN_V_HEADS), F32)
        ddb = jnp.zeros((1, GDN_V_HEADS), F32)
        for b in range(nh):
            hv = j * nh + b
            dba = dba + jnp.where(lane == hv, gbl[b], 0.0) + jnp.where(lane == GDN_V_HEADS + hv, ga[b], 0.0)
            dal = dal + jnp.where(lane1 == hv, gal[b], 0.0)
            ddb = ddb + jnp.where(lane1 == hv, gdb[b], 0.0)
        dba_ref[...] += dba
        dal_ref[...] += dal
        ddb_ref[...] += ddb

    small = pl.BlockSpec((1, GDN_V_HEADS), lambda n, j: (0, 0))
    return pl.pallas_call(
        body, name=name, grid=(nchunk, GDN_K_HEADS // GDN_HB),
        in_specs=specs + [pl.BlockSpec((None, GDN_HB, GDN_REP, d, d), lambda n, j: (at(n), j, 0, 0, 0)),
                          pl.BlockSpec((None, GDN_HB * GDN_REP, c, c), lambda n, j: (at(n), j, 0, 0)),
                          pl.BlockSpec((c, GDN_HB * GDN_REP * d), lambda n, j: (at(n), j))],
        out_specs=[pl.BlockSpec((c, GDN_HB * d), lambda n, j: (at(n), j)),
                   pl.BlockSpec((c, GDN_HB * d), lambda n, j: (at(n), j)),
                   pl.BlockSpec((c, GDN_HB * GDN_REP * d), lambda n, j: (at(n), j)),
                   pl.BlockSpec((c, 2 * GDN_V_HEADS), lambda n, j: (at(n), 0)),
                   small, small],
        out_shape=[jax.ShapeDtypeStruct((t, GDN_KW), F32), jax.ShapeDtypeStruct((t, GDN_KW), F32),
                   jax.ShapeDtypeStruct((t, GDN_VW), F32), jax.ShapeDtypeStruct((t, 2 * GDN_V_HEADS), F32),
                   jax.ShapeDtypeStruct((1, GDN_V_HEADS), F32), jax.ShapeDtypeStruct((1, GDN_V_HEADS), F32)],
        scratch_shapes=[pltpu.VMEM((GDN_K_HEADS, GDN_REP, d, d), F32)],
        compiler_params=_params("arbitrary", "arbitrary"),
    )(act, act, act, ba, alog, dtb, states, inverses, dout)


N_DEV = 8
ANY = pl.BlockSpec(memory_space=pl.ANY)


def _coords():
    return lax.axis_index("x"), lax.axis_index("y"), lax.axis_index("c")


def _other_chips(x, y):
    return [(1 - x, y), (x, 1 - y), (1 - x, 1 - y)]


def _remote(src, dst, send_sems, recv_sems, k, to):
    return pltpu.make_async_remote_copy(src_ref=src, dst_ref=dst, send_sem=send_sems.at[k], recv_sem=recv_sems.at[k],
                                        device_id=to, device_id_type=MESH)


def _dma_sems(n):
    return [pltpu.SemaphoreType.DMA((n,)), pltpu.SemaphoreType.DMA((n,))]


def _gather_copies(ins, outs, send_sems, recv_sems, local_sems, only_first=False):
    x, y, c = _coords()
    sibling = (x, y, 1 - c)
    local, sends, arrivals, relays, relayed = [], [], [], [], []
    for a, (x_ref, out_ref) in enumerate(zip(ins, outs)):
        local.append(pltpu.make_async_copy(x_ref, out_ref.at[2 * x + y], local_sems.at[a]))
        for j, (cx, cy) in enumerate(_other_chips(x, y)):
            k, theirs = 6 * a + j, 2 * cx + cy
            sends.append(_remote(x_ref.at[c], out_ref.at[2 * x + y, c], send_sems, recv_sems, k, (cx, cy, c)))
            if only_first:
                continue
            arrivals.append(_remote(x_ref.at[c], out_ref.at[theirs, c], send_sems, recv_sems, k, (cx, cy, c)))
            relays.append(_remote(out_ref.at[theirs, c], out_ref.at[theirs, c], send_sems, recv_sems, k + 3, sibling))
            relayed.append(_remote(x_ref.at[c], out_ref.at[theirs, 1 - c], send_sems, recv_sems, k + 3, sibling))
    return local, sends, arrivals, relays, relayed


def _gather_start(copies):
    local, sends, _, _, _ = copies
    for cp in local + sends:
        cp.start()


def _gather_finish(copies):
    local, sends, arrivals, relays, relayed = copies
    for landed, relay in zip(arrivals, relays):
        landed.wait_recv()
        relay.start()
    for cp in relayed:
        cp.wait_recv()
    for cp in sends + relays:
        cp.wait_send()
    for cp in local:
        cp.wait()


def _gather_scratch(na):
    return _dma_sems(6 * na) + [pltpu.SemaphoreType.DMA((na,))]


def _gather_quarters(parts, *, name):
    na = len(parts)

    def body(*refs):
        copies = _gather_copies(refs[:na], refs[na:2 * na], *refs[2 * na:])
        _gather_start(copies)
        _gather_finish(copies)

    return pl.pallas_call(
        body, name=name, in_specs=[ANY] * na, out_specs=[ANY] * na,
        out_shape=[jax.ShapeDtypeStruct((N_CHIPS,) + p.shape, p.dtype) for p in parts],
        scratch_shapes=_gather_scratch(na),
    )(*parts)


def _swap_halves(grads, *, name):
    na = len(grads)

    def body(*refs):
        ins, outs = refs[:na], refs[na:2 * na]
        send_sems, recv_sems = refs[2 * na:]
        x, y, c = _coords()
        sends = [_remote(g_ref.at[j, 1 - c], o_ref.at[j], send_sems, recv_sems, N_CHIPS * a + j, (x, y, 1 - c))
                 for a, (g_ref, o_ref) in enumerate(zip(ins, outs)) for j in range(N_CHIPS)]
        for cp in sends:
            cp.start()
        for cp in sends:
            cp.wait()

    return pl.pallas_call(
        body, name=name, in_specs=[ANY] * na, out_specs=[ANY] * na,
        out_shape=[jax.ShapeDtypeStruct((N_CHIPS,) + g.shape[2:], g.dtype) for g in grads],
        scratch_shapes=_dma_sems(N_CHIPS * na),
    )(*grads)


def _scatter_copies(ins, outs, send_sems, recv_sems):
    x, y, c = _coords()
    return [_remote(p_ref.at[2 * cx + cy], outs[3 * a + j], send_sems, recv_sems, 3 * a + j, (cx, cy, c))
            for a, p_ref in enumerate(ins) for j, (cx, cy) in enumerate(_other_chips(x, y))]


def _scatter_shapes(pairs):
    return [jax.ShapeDtypeStruct(p.shape[1:], p.dtype) for p in pairs for _ in range(3)]


def _scatter_quarters(pairs, *, name):
    na = len(pairs)

    def body(*refs):
        sends = _scatter_copies(refs[:na], refs[na:4 * na], *refs[4 * na:])
        for cp in sends:
            cp.start()
        for cp in sends:
            cp.wait()

    out = pl.pallas_call(
        body, name=name, in_specs=[ANY] * na, out_specs=[ANY] * (3 * na), out_shape=_scatter_shapes(pairs),
        scratch_shapes=_dma_sems(3 * na),
    )(*pairs)
    return [out[3 * a:3 * a + 3] for a in range(na)]


def _share_halves(tots, *, name):
    na = len(tots)

    def body(*refs):
        ins, outs = refs[:na], refs[na:2 * na]
        send_sems, recv_sems = refs[2 * na:]
        x, y, c = _coords()
        sends = [_remote(t_ref, o_ref, send_sems, recv_sems, a, (x, y, 1 - c))
                 for a, (t_ref, o_ref) in enumerate(zip(ins, outs))]
        for cp in sends:
            cp.start()
        for cp in sends:
            cp.wait()

    return pl.pallas_call(
        body, name=name, in_specs=[ANY] * na, out_specs=[ANY] * na,
        out_shape=[jax.ShapeDtypeStruct(t.shape, t.dtype) for t in tots],
        scratch_shapes=_dma_sems(na),
    )(*tots)


def _gather_all(vec, *, name):
    m, w = vec.shape

    def body(x_ref, out_ref, send_sems, recv_sems, local_sem):
        x, y, c = _coords()
        me, sibling = (x, y, c), (x, y, 1 - c)
        chips = _other_chips(x, y)

        def rows(px, py, pc):
            return out_ref.at[pl.ds((4 * px + 2 * py + pc) * m, m), :]

        def copy(k, block, to, src=None):
            return _remote(rows(*block) if src is None else src, rows(*block), send_sems, recv_sems, k, to)

        mine = pltpu.make_async_copy(x_ref, rows(*me), local_sem)
        mine.start()
        first = [copy(0, me, sibling, src=x_ref)]
        first += [copy(1 + j, me, (*chip, c), src=x_ref) for j, chip in enumerate(chips)]
        for cp in first:
            cp.start()
        passed = [copy(4 + j, (*chip, c), sibling) for j, chip in enumerate(chips)]
        for j, chip in enumerate(chips):
            copy(1 + j, (*chip, c), me).wait_recv()
            passed[j].start()
        copy(0, sibling, me).wait_recv()
        for j, chip in enumerate(chips):
            copy(4 + j, (*chip, 1 - c), me).wait_recv()
        for cp in first + passed:
            cp.wait_send()
        mine.wait()

    vm = pl.BlockSpec(memory_space=pltpu.VMEM)
    return pl.pallas_call(
        body, name=name, in_specs=[vm], out_specs=vm, out_shape=jax.ShapeDtypeStruct((N_DEV * m, w), vec.dtype),
        scratch_shapes=_dma_sems(7) + [pltpu.SemaphoreType.DMA(())],
    )(vec)


def _sum_blocks(allv, n, *, name):
    m = allv.shape[0] // n

    def body(a_ref, o_ref):
        acc = a_ref[0:m, :]
        for d in range(1, n):
            acc = acc + a_ref[d * m:(d + 1) * m, :]
        o_ref[...] = acc

    return pl.pallas_call(body, name=name, out_shape=jax.ShapeDtypeStruct((m, allv.shape[1]), allv.dtype))(allv)


EW_BLOCK_BYTES = 1 << 20


def _ew_rows(rows, w):
    return _tile(rows, max(8, (EW_BLOCK_BYTES // (4 * w)) // 8 * 8), 8)


def _add_pair(g, got, c, *, name):
    _, _, rows, w = g.shape
    tr = _ew_rows(rows, w)

    def body(c_ref, g_ref, got_ref, o_ref):
        o_ref[...] = (g_ref[...] + got_ref[...]).astype(o_ref.dtype)

    blk = pl.BlockSpec((None, tr, w), lambda q, i, c_ref: (q, i, 0))
    return pl.pallas_call(
        body, name=name,
        grid_spec=pltpu.PrefetchScalarGridSpec(
            num_scalar_prefetch=1, grid=(N_CHIPS, rows // tr),
            in_specs=[pl.BlockSpec((None, None, tr, w), lambda q, i, c_ref: (q, c_ref[0], i, 0)), blk], out_specs=blk),
        out_shape=jax.ShapeDtypeStruct(got.shape, BF16),
        compiler_params=_params("parallel", "parallel"),
    )(c, g, got)


def _add_chips(pair, recv, chip, *, name):
    _, rows, w = pair.shape
    tr = _ew_rows(rows, w)

    def body(chip_ref, p_ref, r0_ref, r1_ref, r2_ref, o_ref):
        f = lambda r: r[...].astype(F32)
        o_ref[...] = ((f(p_ref) + f(r0_ref)) + f(r1_ref)) + f(r2_ref)

    blk = pl.BlockSpec((tr, w), lambda i, chip_ref: (i, 0))
    return pl.pallas_call(
        body, name=name,
        grid_spec=pltpu.PrefetchScalarGridSpec(
            num_scalar_prefetch=1, grid=(rows // tr,),
            in_specs=[pl.BlockSpec((None, tr, w), lambda i, chip_ref: (chip_ref[0], i, 0)), blk, blk, blk], out_specs=blk),
        out_shape=jax.ShapeDtypeStruct((rows, w), F32),
        compiler_params=_params("parallel"),
    )(chip, pair, *recv)


def _adamw_math(w, g, m, v):
    nm = ADAM_B1 * m + (1.0 - ADAM_B1) * g
    nv = ADAM_B2 * v + (1.0 - ADAM_B2) * (g * g)
    m_hat = nm / (1.0 - ADAM_B1 ** ADAM_STEP)
    v_hat = nv / (1.0 - ADAM_B2 ** ADAM_STEP)
    return -ADAM_LR * (m_hat / (jnp.sqrt(v_hat) + ADAM_EPS) + ADAM_WD * w), nm, nv


def _adamw(w, g, m, v, *, name):
    shape = w.shape
    last = shape[-1]
    w2, g2, m2, v2 = (a.reshape(-1, last) for a in (w, g, m, v))
    rows = w2.shape[0]
    tm = _ew_rows(rows, last)

    def body(w_ref, g_ref, m_ref, v_ref, d_ref, nm_ref, nv_ref):
        d_ref[...], nm_ref[...], nv_ref[...] = _adamw_math(w_ref[...], g_ref[...], m_ref[...], v_ref[...])

    spec = pl.BlockSpec((tm, last), lambda i: (i, 0))
    out = jax.ShapeDtypeStruct((rows, last), F32)
    d, nm, nv = pl.pallas_call(
        body, name=name, grid=(rows // tm,), in_specs=[spec] * 4, out_specs=[spec] * 3, out_shape=[out] * 3,
        compiler_params=_params("parallel"),
    )(w2, g2, m2, v2)
    return d.reshape(shape), nm.reshape(shape), nv.reshape(shape)


def _adamw_halves(w, m, v, mine, theirs, c, *, name, into=None):
    rows, wd = w.shape[-2:]
    tr = _ew_rows(rows, wd)
    bufs, at = into if into is not None else ((), ())

    def body(c_ref, w_ref, m_ref, v_ref, a_ref, b_ref, *rest):
        g_ref, d_ref, nm_ref, nv_ref = rest[len(bufs):]
        g = jnp.where(pl.program_id(0) == c_ref[0], a_ref[...], b_ref[...])
        g_ref[...] = g
        d_ref[...], nm_ref[...], nv_ref[...] = _adamw_math(w_ref[...], g, m_ref[...], v_ref[...])

    full = pl.BlockSpec((None,) * (1 + len(at)) + (tr, wd), lambda hf, i, c_ref: at + (hf, i, 0))
    half = pl.BlockSpec((tr, wd), lambda hf, i, c_ref: (i, 0))
    out = jax.ShapeDtypeStruct(w.shape, F32)
    return pl.pallas_call(
        body, name=name,
        grid_spec=pltpu.PrefetchScalarGridSpec(num_scalar_prefetch=1, grid=(2, rows // tr),
                                               in_specs=[full] * 3 + [half] * 2 + [ANY] * len(bufs),
                                               out_specs=[full] * 4),
        out_shape=[out] * 4, input_output_aliases={6 + b: b for b in range(len(bufs))},
        compiler_params=_params("parallel", "parallel"),
    )(c, w, m, v, mine, theirs, *bufs)


def _join_quarters(q, *, name):
    _, rows, n = q.shape
    tr = _tile(rows, 256, 16)

    def body(q_ref, o_ref):
        o_ref[...] = jnp.concatenate([q_ref[s] for s in range(N_CHIPS)], axis=1)

    return pl.pallas_call(
        body, name=name, grid=(rows // tr,),
        in_specs=[pl.BlockSpec((N_CHIPS, tr, n), lambda i: (0, i, 0))],
        out_specs=pl.BlockSpec((tr, N_CHIPS * n), lambda i: (i, 0)),
        out_shape=jax.ShapeDtypeStruct((rows, N_CHIPS * n), q.dtype),
        compiler_params=_params("parallel"),
    )(q)


def _split_quarters(full, *, name):
    rows, n4 = full.shape
    n = n4 // N_CHIPS
    tr = _tile(rows, 256, 16)

    def body(x_ref, o_ref):
        x = x_ref[...]
        for s in range(N_CHIPS):
            o_ref[s] = x[:, s * n:(s + 1) * n]

    return pl.pallas_call(
        body, name=name, grid=(rows // tr,),
        in_specs=[pl.BlockSpec((tr, n4), lambda i: (i, 0))],
        out_specs=pl.BlockSpec((N_CHIPS, tr, n), lambda i: (0, i, 0)),
        out_shape=jax.ShapeDtypeStruct((N_CHIPS, rows, n), full.dtype),
        compiler_params=_params("parallel"),
    )(full)


_WEIGHTS = ['ffn_norm', 'ffn_w_gate', 'ffn_w_up', 'ffn_w_down', 'mix_norm', 'att_w_in', 'att_q_norm', 'att_k_norm',
            'att_sinks', 'att_w_out', 'gdn_w_in', 'gdn_conv_w', 'gdn_a_log', 'gdn_dt_bias', 'gdn_out_norm', 'gdn_w_out',
            'ple_norm', 'ple_w_gate', 'ple_w_proj']
_BIG = ['ffn_w_gate', 'ffn_w_up', 'ffn_w_down', 'att_w_in', 'att_w_out', 'gdn_w_in', 'gdn_w_out', 'ple_w_gate',
        'ple_w_proj']
_SMALL_CUT = {'ffn_norm': 2, 'gdn_conv_w': 2}
_WHOLE = ['mix_norm', 'att_q_norm', 'att_k_norm', 'att_sinks', 'gdn_a_log', 'gdn_dt_bias', 'gdn_out_norm', 'ple_norm']
PACK_W = 1024
SMALL_ROW_MULT = 8


def _halves(a):
    return a.reshape(2, -1, a.shape[-1])


def _from_quarters(blk, axis):
    full = jnp.moveaxis(blk, 0, axis)
    shp = list(full.shape)
    shp[axis:axis + 2] = [shp[axis] * shp[axis + 1]]
    return full.reshape(shp)


def _to_quarters(full, axis):
    shp = list(full.shape)
    shp[axis:axis + 1] = [N_CHIPS, shp[axis] // N_CHIPS]
    return jnp.moveaxis(full.reshape(shp), axis, 0)


def _pack(parts, row_mult):
    flat = jnp.concatenate(parts, axis=-1)
    n = flat.shape[-1]
    rows = -(-n // (PACK_W * row_mult)) * row_mult
    return jnp.pad(flat, [(0, rows * PACK_W - n)]).reshape(rows, PACK_W)


def _unpack(flat, shapes):
    lead = flat.shape[:-2]
    flat = flat.reshape(lead + (-1,))
    out, off = [], 0
    for shp in shapes:
        n = math.prod(shp)
        out.append(flat[..., off:off + n].reshape(lead + tuple(shp)))
        off += n
    return out


FFN_TM = 1024


def _ffn_up(hn, wg, wu, at, *, name):
    t, d = hn.shape
    fq = wg.shape[-1]
    tm = _tile(t, FFN_TM)

    def body(h_ref, wg_ref, wu_ref, g_ref, u_ref, a_ref):
        h = h_ref[...]
        g, u = _dg(h, _b(wg_ref[...]), 1, 0), _dg(h, _b(wu_ref[...]), 1, 0)
        g_ref[...] = g.astype(BF16)
        u_ref[...] = u.astype(BF16)
        a_ref[...] = _f_swiglu(g, u)[0].astype(BF16)

    w_spec = pl.BlockSpec((None,) * (1 + len(at)) + (d, fq), lambda s, i: (s,) + at + (0, 0))
    o_spec = pl.BlockSpec((None, tm, fq), lambda s, i: (s, i, 0))
    out = jax.ShapeDtypeStruct((N_CHIPS, t, fq), BF16)
    return pl.pallas_call(
        body, name=name, grid=(N_CHIPS, t // tm),
        in_specs=[pl.BlockSpec((tm, d), lambda s, i: (i, 0)), w_spec, w_spec], out_specs=[o_spec] * 3,
        out_shape=[out] * 3, compiler_params=_params("parallel", "parallel"),
    )(hn, wg, wu)


def _ffn_d_up(dout, wd, g, u, at, *, name):
    t, d = dout.shape
    fq = wd.shape[-2]
    tm = _tile(t, FFN_TM)

    def body(do_ref, wd_ref, g_ref, u_ref, dg_ref, du_ref):
        da = _dg(_b(do_ref[...]), _b(wd_ref[...]), 1, 1) * 0.5
        _, vjp = jax.vjp(_f_swiglu, g_ref[...].astype(F32), u_ref[...].astype(F32))
        dg, du = vjp((da,))
        dg_ref[...] = dg.astype(BF16)
        du_ref[...] = du.astype(BF16)

    w_spec = pl.BlockSpec((None,) * (1 + len(at)) + (fq, d), lambda s, i: (s,) + at + (0, 0))
    o_spec = pl.BlockSpec((None, tm, fq), lambda s, i: (s, i, 0))
    out = jax.ShapeDtypeStruct((N_CHIPS, t, fq), BF16)
    return pl.pallas_call(
        body, name=name, grid=(N_CHIPS, t // tm),
        in_specs=[pl.BlockSpec((tm, d), lambda s, i: (i, 0)), w_spec, o_spec, o_spec], out_specs=[o_spec] * 2,
        out_shape=[out] * 2, compiler_params=_params("parallel", "parallel"),
    )(dout, wd, g, u)


def _ffn_fwd(h, gain, wg, wu, wd, at, tag):
    lead = (Q,) + at
    hn, = _row_fwd(_f_rms, [h], [gain], [(D_MODEL, BF16)], name=f"{tag}_norm")
    g, u, a = _ffn_up(hn, wg, wu, at, name=f"{tag}_up")
    out = _mm((a, (Q,)), (wd, lead), res=h, scale=0.5, name=f"{tag}_down")
    return out, (h, hn, g, u, a)


def _ffn_bwd(dout, saved, gain, wg, wu, wd, at, grads, g_at, tag):
    h, hn, g, u, a = saved
    lead = (Q,) + at
    g_lead = (Q,) + g_at
    dg, du = _ffn_d_up(dout, wd, g, u, at, name=f"{tag}_d_up")
    g_gate, g_up, g_down = grads
    g_down = _mm((a, (Q,)), dout, ta=True, scale=0.5, into=(g_down, g_lead), name=f"{tag}_dw_down")
    g_gate = _mm((dg, (Q,)), hn, ta=True, into=(g_gate, g_lead), name=f"{tag}_dw_gate")
    g_up = _mm((du, (Q,)), hn, ta=True, into=(g_up, g_lead), name=f"{tag}_dw_up")
    dhn = _mm((dg, (Q,)), (wg, lead), tb=True, name=f"{tag}_d_norm_gate")
    dhn = _mm((du, (Q,)), (wu, lead), tb=True, res=dhn, name=f"{tag}_d_norm_up")
    dh, dgain = _row_bwd(_f_rms_res, [h], [gain], [dhn, dout], [(0, F32)], [0], name=f"{tag}_d_in")
    return dh, dgain, (g_gate, g_up, g_down)


def _att_fwd(h, gain, w_in, qg, kg, sinks, w_out, gather):
    hn, = _row_fwd(_f_rms, [h], [gain], [(D_MODEL, BF16)], name="att_norm")
    proj = _mm(hn, w_in, out_dtype=BF16, name="att_in")
    a, rtot, *gathered = _sb_fwd(proj, name="att_sb", gather=gather)
    b = _swa_fwd(proj, qg, kg, sinks, name="att_swa")
    out = _mm(a, (w_out, (0,)), res=h, name="att_out_sb")
    out = _mm(b, (w_out, (1,)), res=out, name="att_out_swa")
    return out, (h, hn, proj, a, rtot, b), gathered


def _att_bwd(dout, saved, gain, w_in, qg, kg, sinks, w_out, scatter):
    h, hn, proj, a, rtot, b = saved
    da = _mm(dout, (w_out, (0,)), tb=True, name="att_d_sb")
    db = _mm(dout, (w_out, (1,)), tb=True, name="att_d_swa")
    dw_out = lax.empty(w_out.shape, F32)
    dw_out = _mm(a, dout, ta=True, into=(dw_out, (0,)), name="att_dw_out_sb")
    dw_out = _mm(b, dout, ta=True, into=(dw_out, (1,)), name="att_dw_out_swa")
    dq, dk, dv, *landed = _sb_bwd(proj, rtot, da, name="att_sb_bwd", scatter=scatter)
    dqb, dkb, dvb, dqg, dkg, dsk = _swa_bwd(proj, qg, kg, sinks, db, name="att_swa_bwd")
    dproj = jnp.concatenate([dq, dk, dv, dqb, dkb, dvb], axis=1)
    dw_in = _mm(hn, dproj, ta=True, name="att_dw_in")
    dhn = _mm(dproj, w_in, tb=True, name="att_d_norm")
    dh, dgain = _row_bwd(_f_rms_res, [h], [gain], [dhn, dout], [(0, F32)], [0], name="att_d_in")
    return dh, dgain, dw_in, dqg, dkg, dsk, dw_out, [landed[3 * a:3 * a + 3] for a in range(len(scatter))]


def _gdn_layer_fwd(h, gain, w_in, conv_w, alog, dtb, out_gain, w_out):
    w_qkv, w_z, w_ba = w_in[:, :GDN_CONV_W], w_in[:, GDN_CONV_W:GDN_CONV_W + GDN_VW], w_in[:, GDN_CONV_W + GDN_VW:]
    hn, = _row_fwd(_f_rms, [h], [gain], [(D_MODEL, BF16)], name="gdn_norm")
    pq = _mm(hn, w_qkv, name="gdn_in_qkv")
    pz = _mm(hn, w_z, name="gdn_in_z")
    ba = _mm(hn, w_ba, name="gdn_in_ba")
    act = _conv_fwd(pq, conv_w, name="gdn_conv")
    o, states, inverses = _gdn_fwd(act, ba, alog, dtb, name="gdn_rule")
    y, = _row_fwd(_f_gdn_out, [o, pz], [out_gain], [(GDN_VW, BF16)], name="gdn_gate")
    out = _mm(y, w_out, res=h, name="gdn_out")
    return out, (h, hn, pq, pz, ba, act, o, states, inverses, y, (w_qkv, w_z, w_ba))


def _gdn_layer_bwd(dout, saved, gain, conv_w, alog, dtb, out_gain, w_out):
    h, hn, pq, pz, ba, act, o, states, inverses, y, (w_qkv, w_z, w_ba) = saved
    dy = _mm(dout, w_out, tb=True, name="gdn_d_gate")
    dw_out = _mm(y, dout, ta=True, name="gdn_dw_out")
    do, dpz, dout_gain = _row_bwd(_f_gdn_out, [o, pz], [out_gain], [dy], [(0, F32), (1, F32)], [0], name="gdn_gate_bwd")
    dq, dk, dv, dba, dal, ddb = _gdn_bwd(act, ba, alog, dtb, states, inverses, do, name="gdn_rule_bwd")
    dact = jnp.concatenate([dq, dk, dv], axis=1)
    dc = _conv_fwd(pq, conv_w, dact, name="gdn_conv_d_pre")
    dpq, dconv = _conv_bwd(pq, conv_w, dc, name="gdn_conv_bwd")
    dw_in = jnp.concatenate([_mm(hn, dpq, ta=True, name="gdn_dw_qkv"), _mm(hn, dpz, ta=True, name="gdn_dw_z"),
                             _mm(hn, dba, ta=True, name="gdn_dw_ba")], axis=1)
    dhn = _mm(dpq, w_qkv, tb=True, name="gdn_d_norm_qkv")
    dhn = _mm(dpz, w_z, tb=True, res=dhn, name="gdn_d_norm_z")
    dhn = _mm(dba, w_ba, tb=True, res=dhn, name="gdn_d_norm_ba")
    dh, dgain = _row_bwd(_f_rms_res, [h], [gain], [dhn, dout], [(0, F32)], [0], name="gdn_d_in")
    return dh, dgain, dw_in, dconv, dal, ddb, dout_gain, dw_out


def _ple_fwd(h, gain, w_gate, w_proj, pe, tag):
    hn, = _row_fwd(_f_rms, [h], [gain], [(D_MODEL, BF16)], name=f"{tag}_norm")
    gl = _mm(hn, w_gate, name=f"{tag}_gate")
    pp = _mm(pe, w_proj, name=f"{tag}_proj")
    out, = _row_fwd(_f_ple, [h, gl, pp], [], [(D_MODEL, F32)], name=f"{tag}_mix")
    return out, (h, hn, gl, pp)


def _ple_bwd(dout, saved, gain, w_gate, pe, tag):
    h, hn, gl, pp = saved
    dha, dgl, dpp = _row_bwd(_f_ple, [h, gl, pp], [], [dout], [(0, F32), (1, BF16), (2, BF16)], [], name=f"{tag}_mix_bwd")
    dw_gate = _mm(hn, dgl, ta=True, name=f"{tag}_dw_gate")
    dw_proj = _mm(pe, dpp, ta=True, name=f"{tag}_dw_proj")
    dhn = _mm(dgl, w_gate, tb=True, name=f"{tag}_d_norm")
    dh, dgain = _row_bwd(_f_rms_res, [h], [gain], [dhn, dha], [(0, F32)], [0], name=f"{tag}_d_in")
    return dh, dgain, dw_gate, dw_proj


def kernel(x, p, ffn_norm, ffn_w_gate, ffn_w_up, ffn_w_down, mix_norm, att_w_in, att_q_norm, att_k_norm, att_sinks, att_w_out, gdn_w_in, gdn_conv_w, gdn_a_log, gdn_dt_bias, gdn_out_norm, gdn_w_out, ple_norm, ple_w_gate, ple_w_proj, loss_target, m_ffn_norm, m_ffn_w_gate, m_ffn_w_up, m_ffn_w_down, m_mix_norm, m_att_w_in, m_att_q_norm, m_att_k_norm, m_att_sinks, m_att_w_out, m_gdn_w_in, m_gdn_conv_w, m_gdn_a_log, m_gdn_dt_bias, m_gdn_out_norm, m_gdn_w_out, m_ple_norm, m_ple_w_gate, m_ple_w_proj, v_ffn_norm, v_ffn_w_gate, v_ffn_w_up, v_ffn_w_down, v_mix_norm, v_att_w_in, v_att_q_norm, v_att_k_norm, v_att_sinks, v_att_w_out, v_gdn_w_in, v_gdn_conv_w, v_gdn_a_log, v_gdn_dt_bias, v_gdn_out_norm, v_gdn_w_out, v_ple_norm, v_ple_w_gate, v_ple_w_proj):
    arg = dict(locals())
    cx, cy, cc = _coords()
    chip = (2 * cx + cy).astype(jnp.int32).reshape(1)
    core = cc.astype(jnp.int32).reshape(1)
    n_layers = ffn_norm.shape[0]

    quarter = lambda n, i=None: _halves((arg[n] if i is None else arg[n][i]).astype(BF16))
    ffn_names = ('ffn_w_gate', 'ffn_w_up', 'ffn_w_down')
    early = [quarter(n, 0) for n in ffn_names] + [quarter('att_w_in'), quarter('att_w_out')]
    late_names = ('gdn_w_in', 'gdn_w_out', 'ple_w_gate', 'ple_w_proj')
    late = [quarter(n, 1) for n in ffn_names] + [quarter(n) for n in late_names]
    *ffn_w0, att_in_q, att_out_q = _gather_quarters(early, name="gather_weights")
    wt = {'att_w_in': _join_quarters(att_in_q.reshape((N_CHIPS,) + att_w_in.shape[1:]), name="att_w_in_join"),
          'att_w_out': att_out_q.reshape(2, SB_W, D_MODEL)}

    small_names = list(_SMALL_CUT)
    small_shapes = [arg[n].shape for n in small_names]
    svec = _pack([arg[n].reshape(-1) for n in small_names], SMALL_ROW_MULT)
    srows = svec.shape[0]
    sall = _gather_all(svec, name="gather_gains").reshape(N_CHIPS, 2, srows, PACK_W)[:, 0]
    for n, q in zip(small_names, _unpack(sall, small_shapes)):
        wt[n] = _from_quarters(q, _SMALL_CUT[n])
    row = lambda v: v.reshape(1, -1)

    as_ffn = lambda g, n: g.reshape((N_CHIPS,) + arg[n].shape[1:])
    ffn_w = [tuple(as_ffn(g, n) for g, n in zip(ffn_w0, ffn_names)), None]
    h = x[0]
    tape = []
    for i in range(n_layers):
        j = i // 2
        h, s0 = _ffn_fwd(h, row(wt['ffn_norm'][i, 0]), *ffn_w[i], (0,), f"ffn{i}a")
        if i % 2 == 0:
            h, sm, gathered = _att_fwd(h, row(mix_norm[i]), wt['att_w_in'], att_q_norm[j:j + 1], att_k_norm[j:j + 1],
                                       att_sinks[j:j + 1], wt['att_w_out'], late)
            ffn_w[1] = tuple(as_ffn(g, n) for g, n in zip(gathered[:3], ffn_names))
            wq = {n: g.reshape((N_CHIPS,) + arg[n].shape) for n, g in zip(late_names, gathered[3:])}
            wt['gdn_w_in'] = _join_quarters(wq['gdn_w_in'][:, 0], name="gdn_w_in_join")
            wt['gdn_w_out'] = wq['gdn_w_out'].reshape(GDN_VW, D_MODEL)
            wt['ple_w_gate'] = _from_quarters(wq['ple_w_gate'], 1)
            wt['ple_w_proj'] = _from_quarters(wq['ple_w_proj'], 2)
        else:
            h, sm = _gdn_layer_fwd(h, row(mix_norm[i]), wt['gdn_w_in'], wt['gdn_conv_w'][j], gdn_a_log[j:j + 1],
                                   gdn_dt_bias[j:j + 1], gdn_out_norm[j:j + 1], wt['gdn_w_out'])
        h, s1 = _ffn_fwd(h, row(wt['ffn_norm'][i, 1]), *ffn_w[i], (1,), f"ffn{i}b")
        h, sp = _ple_fwd(h, row(ple_norm[i]), wt['ple_w_gate'][i], wt['ple_w_proj'][i], p[i, 0], f"ple{i}")
        tape.append((s0, sm, s1, sp))

    dh, loss_local = _loss_head(h, loss_target[0], name="loss_head")
    loss = lax.psum(loss_local, ("x", "y", "c"))

    gr = {}
    stored_t = ('ffn_w_gate', 'ffn_w_up')
    as_stored = lambda a, n: jnp.swapaxes(a, -1, -2) if n in stored_t else a
    ffn_g = [tuple(lax.empty((N_CHIPS,) + as_stored(arg[n], n).shape[1:], F32) for n in ffn_names)
             for _ in range(n_layers)]
    d_ffn_norm = [[None, None] for _ in range(n_layers)]
    d_mix, d_ple_norm, d_ple_gate, d_ple_proj = [None] * n_layers, [None] * n_layers, [None] * n_layers, [None] * n_layers

    def as_halves(g):
        return g.reshape((N_CHIPS, 2, -1, g.shape[-1]))

    def pair_up(keys, grads, tag):
        got = _swap_halves(grads, name=f"grad_swap_halves_{tag}")
        return [_add_pair(g, o, core, name=f"grad_add_pair_{k}") for k, g, o in zip(keys, grads, got)]

    for i in reversed(range(n_layers)):
        j = i // 2
        s0, sm, s1, sp = tape[i]
        dh, d_ple_norm[i], d_ple_gate[i], d_ple_proj[i] = _ple_bwd(dh, sp, row(ple_norm[i]), wt['ple_w_gate'][i], p[i, 0],
                                                                   f"ple{i}")
        dh, d_ffn_norm[i][1], ffn_g[i] = _ffn_bwd(dh, s1, row(wt['ffn_norm'][i, 1]), *ffn_w[i], (1,), ffn_g[i], (1,),
                                                  f"ffn{i}b")
        if i % 2 == 0:
            gr['ple_w_gate'] = _to_quarters(jnp.stack(d_ple_gate), 1)
            gr['ple_w_proj'] = _to_quarters(jnp.stack(d_ple_proj), 2)
            first_keys = [f"{n}_1" for n in ffn_names] + list(late_names)
            first_pairs = pair_up(first_keys, [as_halves(g) for g in ffn_g[1]] + [as_halves(gr[n]) for n in late_names], "a")
            (dh, d_mix[i], dw_in, gr['att_q_norm'], gr['att_k_norm'], gr['att_sinks'], dw_out,
             first_recv) = _att_bwd(dh, sm, row(mix_norm[i]), wt['att_w_in'], att_q_norm[j:j + 1],
                                    att_k_norm[j:j + 1], att_sinks[j:j + 1], wt['att_w_out'], first_pairs)
            gr['att_w_in'] = _split_quarters(dw_in, name="att_dw_in_split")
            gr['att_w_out'] = dw_out
        else:
            (dh, d_mix[i], dw_in, dconv, gr['gdn_a_log'], gr['gdn_dt_bias'], gr['gdn_out_norm'],
             dw_out) = _gdn_layer_bwd(dh, sm, row(mix_norm[i]), wt['gdn_conv_w'][j], gdn_a_log[j:j + 1],
                                      gdn_dt_bias[j:j + 1], gdn_out_norm[j:j + 1], wt['gdn_w_out'])
            gr['gdn_w_in'] = _split_quarters(dw_in, name="gdn_dw_in_split")
            gr['gdn_w_out'] = dw_out
            gr['gdn_conv_w'] = dconv[None]
        dh, d_ffn_norm[i][0], ffn_g[i] = _ffn_bwd(dh, s0, row(wt['ffn_norm'][i, 0]), *ffn_w[i], (0,), ffn_g[i], (0,),
                                                  f"ffn{i}a")
    grad_x = dh[None]

    gr['ffn_norm'] = jnp.stack([jnp.stack([d_ffn_norm[i][k][0] for k in range(2)]) for i in range(n_layers)])
    gr['mix_norm'] = jnp.concatenate(d_mix, axis=0)
    gr['ple_norm'] = jnp.concatenate(d_ple_norm, axis=0)

    last_keys = [f"{n}_0" for n in ffn_names] + ['att_w_in', 'att_w_out']
    last_pairs = pair_up(last_keys, [as_halves(g) for g in ffn_g[0]] + [as_halves(gr['att_w_in']), as_halves(gr['att_w_out'])],
                         "b")
    last_recv = _scatter_quarters(last_pairs, name="grad_scatter")
    keys = first_keys + last_keys
    tots = [_add_chips(pr, rc, chip, name=f"grad_add_chips_{k}")
            for k, pr, rc in zip(keys, first_pairs + last_pairs, first_recv + last_recv)]
    theirs = _share_halves(tots, name="grad_share")
    summed = dict(zip(keys, zip(tots, theirs)))

    whole_shapes = [arg[n].shape for n in _WHOLE]
    cut_full_shapes = [gr[n].shape for n in small_names]
    gvec = _pack([gr[n].reshape(-1) for n in _WHOLE + small_names], SMALL_ROW_MULT)
    gall = _sum_blocks(_gather_all(gvec, name="gather_small_grads"), N_DEV, name="sum_small_grads")
    parts = _unpack(gall, whole_shapes + cut_full_shapes)
    gsum = dict(zip(_WHOLE, parts))
    for n, g in zip(small_names, parts[len(_WHOLE):]):
        gsum[n] = lax.dynamic_index_in_dim(_to_quarters(g, _SMALL_CUT[n]), chip[0], axis=0, keepdims=False)

    delta, new_m, new_v = {}, {}, {}
    for n in ('att_w_in', 'att_w_out') + late_names:
        res = _adamw_halves(_halves(arg[n]), _halves(arg["m_" + n]), _halves(arg["v_" + n]), *summed[n], core,
                            name=f"adamw_{n}")
        gsum[n], delta[n], new_m[n], new_v[n] = (r.reshape(arg[n].shape) for r in res)
    for n in ffn_names:
        wmv = [as_stored(arg[k + n], n) for k in ("", "m_", "v_")]
        res = tuple(lax.empty(wmv[0].shape, F32) for _ in range(4))
        for i in range(n_layers):
            res = _adamw_halves(*wmv, *summed[f"{n}_{i}"], core, name=f"adamw_{n}_{i}", into=(res, (i,)))
        gsum[n], delta[n], new_m[n], new_v[n] = (as_stored(r, n) for r in res)
    for n in _WHOLE + small_names:
        delta[n], new_m[n], new_v[n] = _adamw(arg[n], gsum[n], arg["m_" + n], arg["v_" + n], name=f"adamw_{n}")
    return (loss, grad_x, *[gsum[n] for n in _WEIGHTS], *[delta[n] for n in _WEIGHTS],
            *[new_m[n] for n in _WEIGHTS], *[new_v[n] for n in _WEIGHTS])
```

```python
import functools
import math

import jax
import jax.numpy as jnp
from jax import lax
from jax.experimental import pallas as pl
from jax.experimental.pallas import tpu as pltpu

F32 = jnp.float32
BF16 = jnp.bfloat16
MESH = pl.DeviceIdType.MESH

LANES = 128
VMEM_LIMIT_BYTES = 56 * 1024 * 1024

EPS = 1e-6
D_MODEL = 1024
HEAD_DIM = 64
SB_HEADS = 8
SWA_HEADS = 8
SWA_KV_HEADS = 2
WINDOW = 128
GDN_K_HEADS = 8
GDN_V_HEADS = 16
GDN_HEAD_DIM = 128
GDN_CONV = 4
GDN_CHUNK = 64
SB_W = SB_HEADS * HEAD_DIM
SWA_QW = SWA_HEADS * HEAD_DIM
SWA_KVW = SWA_KV_HEADS * HEAD_DIM
GDN_KW = GDN_K_HEADS * GDN_HEAD_DIM
GDN_VW = GDN_V_HEADS * GDN_HEAD_DIM
GDN_CONV_W = 2 * GDN_KW + GDN_VW

ADAM_LR = 0.001
ADAM_B1 = 0.9
ADAM_B2 = 0.999
ADAM_EPS = 1e-08
ADAM_WD = 0.01
ADAM_STEP = 10

NEG = -1e30


def _params(*sem):
    return pltpu.CompilerParams(dimension_semantics=sem or None, vmem_limit_bytes=VMEM_LIMIT_BYTES)


def _tile(n, cap, align=LANES):
    if n <= cap:
        return n
    for t in range(cap - cap % align, 0, -align):
        if n % t == 0:
            return t
    return n


N_CHIPS = 4
MM_VMEM_BUDGET_BYTES = 40 * 1024 * 1024
Q = "q"


def _opnd(x):
    return x if isinstance(x, tuple) else (x, ())


def _mm(a, b, *, name, ta=False, tb=False, out_dtype=F32, res=None, scale=1.0, out_q=False, into=None, norm_bwd=None,
        tm=None, tn=1024, tk=1024):
    (a_arr, a_lead), (b_arr, b_lead) = _opnd(a), _opnd(b)
    (k_a, m) = a_arr.shape[-2:] if ta else a_arr.shape[-2:][::-1]
    (n, k_b) = b_arr.shape[-2:] if tb else b_arr.shape[-2:][::-1]
    if into is not None:
        out_arr, out_lead = into
        out_q, out_dtype = Q in out_lead, out_arr.dtype
    else:
        out_lead = (Q,) if out_q else ()
    red_q = (Q in a_lead or Q in b_lead) and not out_q
    kq = min(k_a, k_b)
    assert (k_a == k_b) or (red_q and max(k_a, k_b) == N_CHIPS * kq), (a_arr.shape, b_arr.shape)
    tn, tk = _tile(n, tn), _tile(kq, tk)
    if tm is None:
        r_item = _opnd(res)[0].dtype.itemsize if res is not None else 0
        per_row = 2 * (tk * a_arr.dtype.itemsize + tn * (jnp.dtype(out_dtype).itemsize + r_item)) + 4 * tn
        if norm_bwd is not None:
            per_row += (2 * 2 + 4) * 4 * tn
        room = MM_VMEM_BUDGET_BYTES - 2 * tk * tn * b_arr.dtype.itemsize
        tm = next(c for c in (4096, 2048, 1024, 512, 256, 128) if c * per_row <= room or c == 128)
    tm = _tile(m, tm)
    nk = kq // tk
    ksteps = nk * (N_CHIPS if red_q else 1)
    dims = (((0 if ta else 1,), (1 if tb else 0,)), ((), ()))
    has_res = res is not None
    n_out = 2 if norm_bwd is not None else 1

    def body(*refs):
        a_ref, b_ref = refs[0], refs[1]
        o_ref, acc_ref = refs[-1 - n_out], refs[-1]
        k = pl.program_id(3)
        first_rows = pl.program_id(1) == 0

        @pl.when(k == 0)
        def _():
            acc_ref[...] = jnp.zeros_like(acc_ref)

        acc_ref[...] += lax.dot_general(a_ref[...].astype(BF16), b_ref[...].astype(BF16), dims,
                                        preferred_element_type=F32)

        @pl.when(k == ksteps - 1)
        def _():
            r = acc_ref[...]
            if scale != 1.0:
                r = r * scale
            if has_res:
                r = r + refs[2][...].astype(F32)
            if norm_bwd is not None:
                h_ref, gain_ref, dres_ref = refs[2 + has_res:5 + has_res]
                dgain_ref = refs[-2]
                _, vjp = jax.vjp(_f_rms_res, h_ref[...], gain_ref[...])
                r, dgain = vjp((r, dres_ref[...]))

                @pl.when(first_rows)
                def _():
                    dgain_ref[...] = jnp.zeros_like(dgain_ref)

                dgain_ref[...] += dgain
            o_ref[...] = r.astype(o_ref.dtype)

    def spec(lead, blk, pos):
        def index(s, i, j, k):
            kk = k % nk if (red_q and Q in lead) else k
            quarter = s if out_q else k // nk
            return tuple(quarter if l == Q else l for l in lead) + pos(i, j, kk)
        return pl.BlockSpec((None,) * len(lead) + blk, index)

    a_spec = spec(a_lead, (tk, tm), lambda i, j, k: (k, i)) if ta else spec(a_lead, (tm, tk), lambda i, j, k: (i, k))
    b_spec = spec(b_lead, (tn, tk), lambda i, j, k: (j, k)) if tb else spec(b_lead, (tk, tn), lambda i, j, k: (k, j))
    o_spec = spec(out_lead, (tm, tn), lambda i, j, k: (i, j))
    in_specs, args = [a_spec, b_spec], [a_arr, b_arr]
    if has_res:
        r_arr, r_lead = _opnd(res)
        in_specs.append(spec(r_lead, (tm, tn), lambda i, j, k: (i, j)))
        args.append(r_arr)
    out_specs, out_shapes = [o_spec], []
    if norm_bwd is not None:
        assert tn == n and not out_q and into is None, "the norm's backward needs whole rows"
        h_arr, gain_arr, dres_arr = norm_bwd
        row_spec = spec((), (tm, tn), lambda i, j, k: (i, j))
        gain_spec = pl.BlockSpec((1, tn), lambda s, i, j, k: (0, 0))
        in_specs += [row_spec, gain_spec, row_spec]
        args += [h_arr, gain_arr, dres_arr]
        out_specs.append(gain_spec)
    aliases = {}
    if into is not None:
        in_specs.append(pl.BlockSpec(memory_space=pl.ANY))
        args.append(out_arr)
        aliases = {len(args) - 1: 0}
        out_shapes.append(jax.ShapeDtypeStruct(out_arr.shape, out_arr.dtype))
    else:
        out_shapes.append(jax.ShapeDtypeStruct(((N_CHIPS,) if out_q else ()) + (m, n), out_dtype))
    if norm_bwd is not None:
        out_shapes.append(jax.ShapeDtypeStruct((1, n), F32))
    out = pl.pallas_call(
        body, name=name, grid=(N_CHIPS if out_q else 1, m // tm, n // tn, ksteps), in_specs=in_specs,
        out_specs=out_specs, out_shape=out_shapes, scratch_shapes=[pltpu.VMEM((tm, tn), F32)],
        input_output_aliases=aliases,
        compiler_params=_params("parallel", *(("arbitrary",) * 3 if norm_bwd is not None else ("parallel", "parallel", "arbitrary"))),
    )(*args)
    return out if norm_bwd is not None else out[0]


def _row_spec(r, tm):
    if isinstance(r, tuple):
        arr, width, cb = r
        return arr, pl.BlockSpec((tm, width), lambda i, cb=cb: (i, cb))
    return r, pl.BlockSpec((tm, r.shape[1]), lambda i: (i, 0))


def _const_spec(c):
    return pl.BlockSpec(c.shape, lambda i: (0,) * c.ndim)


def _row_fwd(fn, rows, consts, outs, *, name, tm=256):
    tm = _tile(_row_spec(rows[0], tm)[0].shape[0], tm, 8)
    arrs, specs = zip(*[_row_spec(r, tm) for r in rows])
    t = arrs[0].shape[0]
    nr, nc = len(rows), len(consts)

    def body(*refs):
        vals = [r[...].astype(F32) for r in refs[:nr + nc]]
        res = fn(*vals)
        for o_ref, v in zip(refs[nr + nc:], res):
            o_ref[...] = v.astype(o_ref.dtype)

    out = pl.pallas_call(
        body, name=name, grid=(t // tm,),
        in_specs=list(specs) + [_const_spec(c) for c in consts],
        out_specs=[pl.BlockSpec((tm, w), lambda i: (i, 0)) for w, _ in outs],
        out_shape=[jax.ShapeDtypeStruct((t, w), dt) for w, dt in outs],
        compiler_params=_params("parallel"),
    )(*arrs, *consts)
    return list(out)


def _row_bwd(fn, rows, consts, cts, row_grads, const_grads, *, name, tm=256):
    tm = _tile(_row_spec(rows[0], tm)[0].shape[0], tm, 8)
    arrs, specs = zip(*[_row_spec(r, tm) for r in rows])
    ct_arrs, ct_specs = zip(*[_row_spec(r, tm) for r in cts])
    t = arrs[0].shape[0]
    nr, nc, nt = len(rows), len(consts), len(cts)
    n_in = nr + nc + nt

    def body(*refs):
        vals = [r[...].astype(F32) for r in refs[:nr + nc]]
        ctv = tuple(r[...].astype(F32) for r in refs[nr + nc:n_in])
        _, vjp = jax.vjp(fn, *vals)
        g = vjp(ctv)
        outs = refs[n_in:]
        for (idx, _), o_ref in zip(row_grads, outs[:len(row_grads)]):
            o_ref[...] = g[idx].astype(o_ref.dtype)
        first = pl.program_id(0) == 0
        for ci, o_ref in zip(const_grads, outs[len(row_grads):]):
            @pl.when(first)
            def _(o_ref=o_ref):
                o_ref[...] = jnp.zeros_like(o_ref)

            o_ref[...] += g[nr + ci]

    widths = [(_row_spec(rows[idx], tm)[1].block_shape[1], dt) for idx, dt in row_grads]
    out = pl.pallas_call(
        body, name=name, grid=(t // tm,),
        in_specs=list(specs) + [_const_spec(c) for c in consts] + list(ct_specs),
        out_specs=[pl.BlockSpec((tm, w), lambda i: (i, 0)) for w, _ in widths]
        + [_const_spec(consts[ci]) for ci in const_grads],
        out_shape=[jax.ShapeDtypeStruct((t, w), dt) for w, dt in widths]
        + [jax.ShapeDtypeStruct(consts[ci].shape, F32) for ci in const_grads],
        compiler_params=_params("arbitrary"),
    )(*arrs, *consts, *ct_arrs)
    return list(out)


def _rms(x, g):
    return x * lax.rsqrt(jnp.mean(x * x, axis=-1, keepdims=True) + EPS) * g


def _f_rms(h, g):
    return (_rms(h, g),)


def _f_rms_res(h, g):
    return (_rms(h, g), h)


def _f_swiglu(g, u):
    return (g * jax.nn.sigmoid(g) * u,)


def _f_ple(h, gl, pp):
    return (h + jax.nn.sigmoid(gl) * pp,)


def _f_gdn_out(o, z, gain):
    outs = []
    for hd in range(GDN_V_HEADS):
        sl = slice(hd * GDN_HEAD_DIM, (hd + 1) * GDN_HEAD_DIM)
        oh, zh = o[:, sl], z[:, sl]
        outs.append(_rms(oh, gain) * (zh * jax.nn.sigmoid(zh)))
    return (jnp.concatenate(outs, axis=1),)


def _loss_head(y, target, *, name, tm=512):
    t, d = y.shape
    tm = _tile(t, tm, 8)

    def body(y_ref, t_ref, dy_ref, l_ref):
        @pl.when(pl.program_id(0) == 0)
        def _():
            l_ref[...] = jnp.zeros_like(l_ref)

        e = y_ref[...] - t_ref[...]
        dy_ref[...] = e * (1.0 / d)
        l_ref[...] += jnp.sum(e * e) * (0.5 / d)

    dy, l = pl.pallas_call(
        body, name=name, grid=(t // tm,),
        in_specs=[pl.BlockSpec((tm, d), lambda i: (i, 0))] * 2,
        out_specs=[pl.BlockSpec((tm, d), lambda i: (i, 0)), pl.BlockSpec((8, LANES), lambda i: (0, 0))],
        out_shape=[jax.ShapeDtypeStruct((t, d), F32), jax.ShapeDtypeStruct((8, LANES), F32)],
        compiler_params=_params("arbitrary"),
    )(y, target)
    return dy, l[0, 0]


def _dg(a, b, ca, cb):
    nb = a.ndim - 2
    batch = tuple(range(nb))
    return lax.dot_general(a, b, (((ca + nb,), (cb + nb,)), (batch, batch)), preferred_element_type=F32)


def _b(x):
    return x.astype(BF16)


@jax.custom_vjp
def _bdot(a, b):
    return _dg(_b(a), _b(b), 1, 0)


def _bdot_fwd(a, b):
    return _bdot(a, b), (a, b)


def _bdot_bwd(r, ct):
    a, b = r
    return _dg(_b(ct), _b(b), 1, 1), _dg(_b(a), _b(ct), 0, 0)


_bdot.defvjp(_bdot_fwd, _bdot_bwd)


@jax.custom_vjp
def _bdot_nt(a, b):
    return _dg(_b(a), _b(b), 1, 1)


def _bdot_nt_fwd(a, b):
    return _bdot_nt(a, b), (a, b)


def _bdot_nt_bwd(r, ct):
    a, b = r
    return _dg(_b(ct), _b(b), 1, 0), _dg(_b(ct), _b(a), 0, 0)


_bdot_nt.defvjp(_bdot_nt_fwd, _bdot_nt_bwd)


@jax.custom_vjp
def _bdot_tn(a, b):
    return _dg(_b(a), _b(b), 0, 0)


def _bdot_tn_fwd(a, b):
    return _bdot_tn(a, b), (a, b)


def _bdot_tn_bwd(r, ct):
    a, b = r
    return _dg(_b(b), _b(ct), 1, 1), _dg(_b(a), _b(ct), 1, 0)


_bdot_tn.defvjp(_bdot_tn_fwd, _bdot_tn_bwd)


def _two(x):
    hi = x.astype(BF16)
    return hi, (x - hi.astype(F32)).astype(BF16)


def _dg3(a, b, ca, cb):
    (ah, al), (bh, bl) = _two(a), _two(b)
    return _dg(ah, bh, ca, cb) + (_dg(ah, bl, ca, cb) + _dg(al, bh, ca, cb))


@jax.custom_vjp
def _hdot(a, b):
    return _dg3(a, b, 1, 0)


def _hdot_fwd(a, b):
    return _hdot(a, b), (a, b)


def _hdot_bwd(r, ct):
    a, b = r
    return _dg3(ct, b, 1, 1), _dg3(a, ct, 0, 0)


_hdot.defvjp(_hdot_fwd, _hdot_bwd)


@jax.custom_vjp
def _unit_lower_inverse(x):
    c = x.shape[-1]
    eye = (lax.broadcasted_iota(jnp.int32, x.shape, 1) == lax.broadcasted_iota(jnp.int32, x.shape, 2)).astype(F32)
    inv, pw = eye + x, x
    for _ in range(int(math.log2(c)) - 1):
        pw = _dg3(pw, pw, 1, 0)
        inv = inv + _dg3(inv, pw, 1, 0)
    return inv


def _unit_lower_inverse_fwd(x):
    inv = _unit_lower_inverse(x)
    return inv, inv


def _unit_lower_inverse_bwd(inv, ct):
    return (_dg3(_dg3(inv, ct, 0, 0), inv, 1, 1),)


_unit_lower_inverse.defvjp(_unit_lower_inverse_fwd, _unit_lower_inverse_bwd)


@jax.custom_vjp
def _known_inverse(x, inv):
    return inv


def _known_inverse_fwd(x, inv):
    return inv, inv


def _known_inverse_bwd(inv, ct):
    return _dg3(_dg3(inv, ct, 0, 0), inv, 1, 1), jnp.zeros_like(inv)


_known_inverse.defvjp(_known_inverse_fwd, _known_inverse_bwd)


def _split_dot(x, u):
    hi, lo = _two(x)
    return _dg(hi, u, 1, 0) + _dg(lo, u, 1, 0)


@jax.custom_vjp
def _ldot(l01, x):
    hi, lo = _two(x)
    l01 = l01.astype(BF16)
    return _dg(l01, hi, 1, 0) + _dg(l01, lo, 1, 0)


def _ldot_fwd(l01, x):
    return _ldot(l01, x), l01


def _ldot_bwd(l01, ct):
    hi, lo = _two(ct)
    l01b = l01.astype(BF16)
    return jnp.zeros_like(l01), _dg(l01b, hi, 0, 0) + _dg(l01b, lo, 0, 0)


_ldot.defvjp(_ldot_fwd, _ldot_bwd)


SB_BLK = 128
SB_KEYS = 512
SB_PAIRS = 2
SB_SCALE = HEAD_DIM ** -0.5


def _log_sigmoid(z):
    return jnp.minimum(z, 0.0) - jnp.log(1.0 + jnp.exp(-jnp.abs(z)))


def _sb_consts(t):
    kb = min(SB_KEYS, t)
    nh = 2 * SB_PAIRS
    lane = lax.broadcasted_iota(jnp.int32, (nh, SB_BLK, kb), 2)
    row = lax.broadcasted_iota(jnp.int32, (nh, SB_BLK, kb), 1)
    ur = lax.broadcasted_iota(jnp.int32, (kb, kb), 0)
    uc = lax.broadcasted_iota(jnp.int32, (kb, kb), 1)
    return kb, nh, lane, row, ur, uc


def _sb_heads(x):
    head0 = lax.broadcasted_iota(jnp.int32, (x.shape[0], LANES), 1) < HEAD_DIM
    out = []
    for p in range(SB_PAIRS):
        blk = x[:, p * LANES:(p + 1) * LANES]
        out += [jnp.where(head0, blk, 0.0), jnp.where(head0, 0.0, blk)]
    return jnp.stack(out)


def _sb_pairs(x):
    return jnp.stack([x[:, (h // 2) * LANES:(h // 2 + 1) * LANES] for h in range(2 * SB_PAIRS)])


def _sb_merge(x):
    head0 = lax.broadcasted_iota(jnp.int32, (x.shape[1], LANES), 1) < HEAD_DIM
    return jnp.concatenate([jnp.where(head0, x[2 * p], x[2 * p + 1]) for p in range(SB_PAIRS)], axis=1)


def _sb_rows_dot(x, u):
    nh, rows, k = x.shape
    return _split_dot(x.reshape(nh * rows, k), u).reshape(nh, rows, k)


def _sb_fwd(proj, *, name, gather=()):
    t = proj.shape[0]
    nb = t // SB_BLK
    width = SB_PAIRS * LANES
    ng = SB_W // width
    na = len(gather)

    def body(q_ref, k_ref, v_ref, *rest):
        o_ref, r_ref = rest[na:na + 2]
        i = pl.program_id(1)
        if na:
            step = pl.program_id(0) * nb + i
            copies = lambda **kw: _gather_copies(rest[:na], rest[na + 2:2 * na + 2], *rest[2 * na + 2:], **kw)
            pl.when(step == 0)(lambda: _gather_start(copies(only_first=True)))
        kb, nh, lane, row, ur, uc = _sb_consts(t)
        u_suffix = (ur >= uc).astype(BF16)
        qh = _b(_sb_heads(q_ref[...]) * SB_SCALE)
        diag = (i * SB_BLK) // kb

        def block(j, carry, masked):
            acc, car = carry
            keys = pl.ds(pl.multiple_of(j * kb, kb), kb)
            kj, vj = _b(_sb_pairs(k_ref[keys, :])), _b(_sb_pairs(v_ref[keys, :]))
            z = _dg(qh, kj, 1, 1)
            lk = _log_sigmoid(-z)
            if masked:
                causal = (j * kb + lane) < (i * SB_BLK + row)
                lk = jnp.where(causal, lk, 0.0)
            suf = _sb_rows_dot(lk, u_suffix) + car
            w = jnp.exp(z + suf)
            if masked:
                w = jnp.where(causal, w, 0.0)
            return acc + _dg(_b(w), vj, 1, 0), suf[:, :, 0:1]

        zero = (jnp.zeros((nh, SB_BLK, LANES), F32), jnp.zeros((nh, SB_BLK, 1), F32))
        carry = block(diag, zero, True)
        acc, car = lax.fori_loop(0, diag, lambda s, c: block(diag - 1 - s, c, False), carry)
        o_ref[...] = _sb_merge(acc)
        r_ref[...] = _sb_merge(jnp.broadcast_to(car, (nh, SB_BLK, LANES)))
        if na:
            pl.when(step == ng * nb - 1)(lambda: _gather_finish(copies()))

    return pl.pallas_call(
        body, name=name, grid=(ng, nb),
        in_specs=[pl.BlockSpec((SB_BLK, width), lambda p, i: (i, p)),
                  pl.BlockSpec((t, width), lambda p, i: (0, ng + p)),
                  pl.BlockSpec((t, width), lambda p, i: (0, 2 * ng + p))] + [ANY] * na,
        out_specs=[pl.BlockSpec((SB_BLK, width), lambda p, i: (i, p))] * 2 + [ANY] * na,
        out_shape=[jax.ShapeDtypeStruct((t, SB_W), F32)] * 2
        + [jax.ShapeDtypeStruct((N_CHIPS,) + g.shape, g.dtype) for g in gather],
        scratch_shapes=_gather_scratch(na) if na else [],
        compiler_params=_params("arbitrary", "arbitrary"),
    )(proj, proj, proj, *gather)


def _sb_bwd(proj, rtot, dout, *, name, scatter=()):
    t = proj.shape[0]
    nb = t // SB_BLK
    width = SB_PAIRS * LANES
    ng = SB_W // width
    na = len(scatter)

    def body(q_ref, k_ref, v_ref, r_ref, do_ref, *rest):
        dq_ref, dk_ref, dv_ref = rest[na:na + 3]
        i = pl.program_id(1)
        if na:
            step = pl.program_id(0) * nb + i
            copies = lambda: _scatter_copies(rest[:na], rest[na + 3:4 * na + 3], *rest[4 * na + 3:])
            pl.when(step == 0)(lambda: [cp.start() for cp in copies()] and None)
        kb, nh, lane, row, ur, uc = _sb_consts(t)
        u_incl = (ur <= uc).astype(BF16)
        u_excl = (ur < uc).astype(BF16)
        q, do = q_ref[...], do_ref[...]
        qh, doh = _b(_sb_heads(q) * SB_SCALE), _b(_sb_heads(do))
        qb, dob = _b(_sb_pairs(q) * SB_SCALE), _b(_sb_pairs(do))
        rh = jnp.min(_sb_heads(r_ref[...]), axis=2, keepdims=True)
        diag = (i * SB_BLK) // kb

        @pl.when(i == 0)
        def _():
            dk_ref[...] = jnp.zeros_like(dk_ref)
            dv_ref[...] = jnp.zeros_like(dv_ref)

        def block(j, carry, masked):
            dq_acc, clk, ce = carry
            keys = pl.ds(pl.multiple_of(j * kb, kb), kb)
            kj, vj = _b(_sb_pairs(k_ref[keys, :])), _b(_sb_pairs(v_ref[keys, :]))
            z = _dg(qh, kj, 1, 1)
            lk = _log_sigmoid(-z)
            ls = z + lk
            if masked:
                causal = (j * kb + lane) < (i * SB_BLK + row)
                lk = jnp.where(causal, lk, 0.0)
            pre = _sb_rows_dot(lk, u_incl) + clk
            w = jnp.exp(ls + (rh - pre))
            if masked:
                w = jnp.where(causal, w, 0.0)
            e = _dg(doh, vj, 1, 1) * w
            pre_e = _sb_rows_dot(e, u_excl) + ce
            sig = jnp.exp(ls)
            dz = e - sig * (e + pre_e)
            if masked:
                dz = jnp.where(causal, dz, 0.0)
            dzb = _b(dz)
            dk_ref[keys, :] += _sb_merge(_dg(dzb, qb, 0, 0))
            dv_ref[keys, :] += _sb_merge(_dg(_b(w), dob, 0, 0))
            return dq_acc + _dg(dzb, kj, 1, 0), pre[:, :, kb - 1:], pre_e[:, :, kb - 1:] + e[:, :, kb - 1:]

        zero = (jnp.zeros((nh, SB_BLK, LANES), F32), jnp.zeros((nh, SB_BLK, 1), F32), jnp.zeros((nh, SB_BLK, 1), F32))
        carry = lax.fori_loop(0, diag, lambda j, c: block(j, c, False), zero)
        dq_acc, _, _ = block(diag, carry, True)
        dq_ref[...] = _sb_merge(dq_acc) * SB_SCALE
        if na:
            pl.when(step == ng * nb - 1)(lambda: [cp.wait() for cp in copies()] and None)

    blk = pl.BlockSpec((SB_BLK, width), lambda p, i: (i, p))
    whole = pl.BlockSpec((t, width), lambda p, i: (0, p))
    return pl.pallas_call(
        body, name=name, grid=(ng, nb),
        in_specs=[blk,
                  pl.BlockSpec((t, width), lambda p, i: (0, ng + p)),
                  pl.BlockSpec((t, width), lambda p, i: (0, 2 * ng + p)),
                  blk, blk] + [ANY] * na,
        out_specs=[blk, whole, whole] + [ANY] * (3 * na),
        out_shape=[jax.ShapeDtypeStruct((t, SB_W), F32)] * 3 + _scatter_shapes(scatter),
        scratch_shapes=_dma_sems(3 * na) if na else [],
        compiler_params=_params("arbitrary", "arbitrary"),
    )(proj, proj, proj, rtot, dout, *scatter)


SWA_G = SWA_HEADS // SWA_KV_HEADS


def _swa_heads(first, qs, ks, vs, qg, kg, sinks):
    shape = (SWA_HEADS, WINDOW, 2 * WINDOW)
    qi = lax.broadcasted_iota(jnp.int32, shape, 1)
    kj = lax.broadcasted_iota(jnp.int32, shape, 2)
    dist = qi + WINDOW - kj
    valid = (dist >= 0) & (dist < WINDOW) & (jnp.logical_not(first) | (kj >= WINDOW))
    head = lax.broadcasted_iota(jnp.int32, (SWA_HEADS, 1, 1), 0)
    slope = sum(jnp.where(head == h, 2.0 ** (-8.0 * (h + 1) / SWA_HEADS), 0.0) for h in range(SWA_HEADS))
    kn = _rms(ks, kg)
    per_q_head = lambda x: jnp.concatenate([x[h // SWA_G:h // SWA_G + 1] for h in range(SWA_HEADS)], axis=0)
    k8, v8 = per_q_head(kn), per_q_head(vs)
    s = _bdot_nt(_rms(qs, qg), k8) * (HEAD_DIM ** -0.5)
    s = jnp.where(valid, s - slope * dist.astype(F32), NEG)
    m = lax.stop_gradient(jnp.maximum(jnp.max(s, axis=2, keepdims=True), sinks))
    p = jnp.exp(s - m)
    den = jnp.sum(p, axis=2, keepdims=True) + jnp.exp(sinks - m)
    return _bdot(p / den, v8)


def _swa_split(q, kp, kc, vp, vc, sk):
    lanes = lambda x, n: jnp.stack([x[:, h * HEAD_DIM:(h + 1) * HEAD_DIM].astype(F32) for h in range(n)])
    k2, v2 = jnp.concatenate([kp, kc], axis=0), jnp.concatenate([vp, vc], axis=0)
    sinks = jnp.stack([sk[:, h:h + 1] for h in range(SWA_HEADS)])
    return lanes(q, SWA_HEADS), lanes(k2, SWA_KV_HEADS), lanes(v2, SWA_KV_HEADS), sinks


def _swa_join(x):
    return jnp.concatenate([x[h] for h in range(x.shape[0])], axis=1)


def _swa_specs(t):
    qcb = (3 * SB_W) // SWA_QW
    kcb = (3 * SB_W + SWA_QW) // SWA_KVW
    prev = lambda i: jnp.maximum(i - 1, 0)
    return [pl.BlockSpec((WINDOW, SWA_QW), lambda i: (i, qcb)),
            pl.BlockSpec((WINDOW, SWA_KVW), lambda i: (prev(i), kcb)),
            pl.BlockSpec((WINDOW, SWA_KVW), lambda i: (i, kcb)),
            pl.BlockSpec((WINDOW, SWA_KVW), lambda i: (prev(i), kcb + 1)),
            pl.BlockSpec((WINDOW, SWA_KVW), lambda i: (i, kcb + 1)),
            pl.BlockSpec((1, HEAD_DIM), lambda i: (0, 0)),
            pl.BlockSpec((1, HEAD_DIM), lambda i: (0, 0)),
            pl.BlockSpec((1, SWA_HEADS), lambda i: (0, 0))]


def _swa_fwd(proj, qg, kg, sinks, *, name):
    t = proj.shape[0]

    def body(q_ref, kp_ref, kc_ref, vp_ref, vc_ref, qg_ref, kg_ref, sk_ref, o_ref):
        first = pl.program_id(0) == 0
        qs, ks, vs, sk = _swa_split(q_ref[...], kp_ref[...], kc_ref[...], vp_ref[...], vc_ref[...], sk_ref[...])
        o_ref[...] = _swa_join(_swa_heads(first, qs, ks, vs, qg_ref[...], kg_ref[...], sk))

    return pl.pallas_call(
        body, name=name, grid=(t // WINDOW,), in_specs=_swa_specs(t),
        out_specs=pl.BlockSpec((WINDOW, SWA_QW), lambda i: (i, 0)),
        out_shape=jax.ShapeDtypeStruct((t, SWA_QW), F32),
        compiler_params=_params("parallel"),
    )(proj, proj, proj, proj, proj, qg, kg, sinks)


def _swa_bwd(proj, qg, kg, sinks, dout, *, name):
    t = proj.shape[0]

    def body(q_ref, kp_ref, kc_ref, vp_ref, vc_ref, qg_ref, kg_ref, sk_ref, do_ref,
             dq_ref, dk_ref, dv_ref, dqg_ref, dkg_ref, dsk_ref):
        i = pl.program_id(0)
        first = i == 0

        @pl.when(first)
        def _():
            for r in (dk_ref, dv_ref, dqg_ref, dkg_ref, dsk_ref):
                r[...] = jnp.zeros_like(r)

        qs, ks, vs, sk = _swa_split(q_ref[...], kp_ref[...], kc_ref[...], vp_ref[...], vc_ref[...], sk_ref[...])
        do = do_ref[...]
        cts = jnp.stack([do[:, h * HEAD_DIM:(h + 1) * HEAD_DIM] for h in range(SWA_HEADS)])
        _, vjp = jax.vjp(functools.partial(_swa_heads, first), qs, ks, vs, qg_ref[...], kg_ref[...], sk)
        dqs, dks, dvs, dqg, dkg, dsk = vjp(cts)
        dq_ref[...] = _swa_join(dqs)
        dk2, dv2 = _swa_join(dks), _swa_join(dvs)
        cur = pl.ds(pl.multiple_of(i * WINDOW, WINDOW), WINDOW)
        prv = pl.ds(pl.multiple_of(jnp.maximum(i - 1, 0) * WINDOW, WINDOW), WINDOW)
        dk_ref[prv, :] += dk2[:WINDOW]
        dv_ref[prv, :] += dv2[:WINDOW]
        dk_ref[cur, :] += dk2[WINDOW:]
        dv_ref[cur, :] += dv2[WINDOW:]
        dqg_ref[...] += dqg
        dkg_ref[...] += dkg
        dsk_ref[...] += _swa_join(dsk)

    whole = lambda shape: pl.BlockSpec(shape, lambda i: (0, 0))
    return pl.pallas_call(
        body, name=name, grid=(t // WINDOW,),
        in_specs=_swa_specs(t) + [pl.BlockSpec((WINDOW, SWA_QW), lambda i: (i, 0))],
        out_specs=[pl.BlockSpec((WINDOW, SWA_QW), lambda i: (i, 0)), whole((t, SWA_KVW)), whole((t, SWA_KVW)),
                   whole((1, HEAD_DIM)), whole((1, HEAD_DIM)), whole((1, SWA_HEADS))],
        out_shape=[jax.ShapeDtypeStruct((t, SWA_QW), F32), jax.ShapeDtypeStruct((t, SWA_KVW), F32),
                   jax.ShapeDtypeStruct((t, SWA_KVW), F32), jax.ShapeDtypeStruct((1, HEAD_DIM), F32),
                   jax.ShapeDtypeStruct((1, HEAD_DIM), F32), jax.ShapeDtypeStruct((1, SWA_HEADS), F32)],
        compiler_params=_params("arbitrary"),
    )(proj, proj, proj, proj, proj, qg, kg, sinks, dout)


CONV_CB = 512
CONV_TM = 512
HALO = 8


def _conv_pre(x_ref, h_ref, w_ref, i):
    halo = jnp.where(i > 0, h_ref[...], 0.0)
    xe = jnp.concatenate([halo, x_ref[...]], axis=0)
    tm = x_ref.shape[0]
    w = w_ref[...]
    c = sum(w[k:k + 1, :] * xe[HALO - (GDN_CONV - 1) + k:HALO - (GDN_CONV - 1) + k + tm] for k in range(GDN_CONV))
    return c, xe


def _conv_specs(tm, cb):
    return [pl.BlockSpec((tm, cb), lambda c, i: (i, c)),
            pl.BlockSpec((HALO, cb), lambda c, i: (jnp.maximum(i * (tm // HALO) - 1, 0), c)),
            pl.BlockSpec((GDN_CONV, cb), lambda c, i: (0, c))]


def _conv_fwd(x, w, dact=None, *, name):
    t, ch = x.shape
    tm, cb = _tile(t, CONV_TM), _tile(ch, CONV_CB)

    def body(*refs):
        x_ref, h_ref, w_ref = refs[:3]
        c, _ = _conv_pre(x_ref, h_ref, w_ref, pl.program_id(1))
        sig = jax.nn.sigmoid(c)
        if dact is None:
            refs[3][...] = c * sig
        else:
            refs[4][...] = refs[3][...] * (sig * (1.0 + c * (1.0 - sig)))

    tile = pl.BlockSpec((tm, cb), lambda c, i: (i, c))
    extra = () if dact is None else (dact,)
    return pl.pallas_call(
        body, name=name, grid=(ch // cb, t // tm),
        in_specs=_conv_specs(tm, cb) + [tile] * len(extra), out_specs=tile,
        out_shape=jax.ShapeDtypeStruct((t, ch), F32),
        compiler_params=_params("parallel", "parallel"),
    )(x, x, w, *extra)


def _conv_bwd(x, w, dc, *, name):
    t, ch = x.shape
    tm, cb = _tile(t, CONV_TM), _tile(ch, CONV_CB)
    nt = t // tm

    def body(x_ref, h_ref, w_ref, dc_ref, nh_ref, dx_ref, dw_ref):
        i = pl.program_id(1)

        @pl.when(i == 0)
        def _():
            dw_ref[...] = jnp.zeros_like(dw_ref)

        halo = jnp.where(i > 0, h_ref[...], 0.0)
        xe = jnp.concatenate([halo, x_ref[...]], axis=0)
        dc = dc_ref[...]
        dce = jnp.concatenate([dc, jnp.where(i < nt - 1, nh_ref[...], 0.0)], axis=0)
        w = w_ref[...]
        last = GDN_CONV - 1
        dx_ref[...] = sum(w[k:k + 1, :] * dce[last - k:last - k + tm] for k in range(GDN_CONV))
        dw_ref[...] += jnp.concatenate(
            [jnp.sum(dc * xe[HALO - last + k:HALO - last + k + tm], axis=0, keepdims=True) for k in range(GDN_CONV)],
            axis=0)

    tile = pl.BlockSpec((tm, cb), lambda c, i: (i, c))
    nxt = pl.BlockSpec((HALO, cb), lambda c, i: (jnp.minimum((i + 1) * (tm // HALO), t // HALO - 1), c))
    return pl.pallas_call(
        body, name=name, grid=(ch // cb, nt),
        in_specs=_conv_specs(tm, cb) + [tile, nxt],
        out_specs=[tile, pl.BlockSpec((GDN_CONV, cb), lambda c, i: (0, c))],
        out_shape=[jax.ShapeDtypeStruct((t, ch), F32), jax.ShapeDtypeStruct((GDN_CONV, ch), F32)],
        compiler_params=_params("parallel", "arbitrary"),
    )(x, x, w, dc, dc)


def _gdn_chunk(qraw, kraw, v, bl, a, alog, dtb, state, inverse=None, keep_inverse=False):
    c, d = GDN_CHUNK, GDN_HEAD_DIM
    nh = qraw.shape[0]
    ri = lax.broadcasted_iota(jnp.int32, (nh, c, c), 1)
    ci = lax.broadcasted_iota(jnp.int32, (nh, c, c), 2)
    incl, strict = ri >= ci, ri > ci
    q = qraw * lax.rsqrt(jnp.sum(qraw * qraw, axis=-1, keepdims=True) + EPS) * (d ** -0.5)
    k = kraw * lax.rsqrt(jnp.sum(kraw * kraw, axis=-1, keepdims=True) + EPS)
    beta = jax.nn.sigmoid(bl)
    g = -jnp.exp(alog) * jax.nn.softplus(a + dtb)
    gc = _ldot(incl.astype(F32), jnp.broadcast_to(g, (nh, c, d)))
    gcm = gc[:, :, :c]
    decay = jnp.exp(jnp.where(incl, gcm - jnp.swapaxes(gcm, 1, 2), NEG))
    eg = jnp.exp(gc)
    kbeta = k * beta
    x = -jnp.where(strict, _bdot_nt(kbeta, k) * decay, 0.0)
    tinv = _unit_lower_inverse(x) if inverse is None else _known_inverse(x, inverse)
    u = _hdot(tinv, v * beta)
    w = _hdot(tinv, kbeta * eg)
    attn = jnp.where(incl, _bdot_nt(q, k) * decay, 0.0)
    glast = gc[:, c - 1:c, :]
    v_new = u - _bdot(w, state)
    o = _bdot(q * eg, state) + _bdot(attn, v_new)
    state = state * jnp.exp(glast) + _bdot_tn(k * jnp.exp(glast - gc), v_new)
    return (o, state, tinv) if keep_inverse else (o, state)


GDN_REP = GDN_V_HEADS // GDN_K_HEADS
GDN_HB = 8


def _gdn_pick(vals, kh, r):
    ba, alog, dtb = vals
    lane = lax.broadcasted_iota(jnp.int32, ba.shape, 1)
    hv = kh * GDN_REP + r
    bl = jnp.sum(jnp.where(lane == hv, ba, 0.0), axis=1, keepdims=True)
    a = jnp.sum(jnp.where(lane == GDN_V_HEADS + hv, ba, 0.0), axis=1, keepdims=True)
    lane1 = lax.broadcasted_iota(jnp.int32, alog.shape, 1)
    al = jnp.sum(jnp.where(lane1 == hv, alog, 0.0), axis=1, keepdims=True)
    db = jnp.sum(jnp.where(lane1 == hv, dtb, 0.0), axis=1, keepdims=True)
    return bl, a, al, db


def _gdn_stack(qs, ks, vs, small, j):
    d = GDN_HEAD_DIM
    per = [[], [], [], [], [], [], []]
    for hh in range(GDN_HB):
        q, k = qs[:, hh * d:(hh + 1) * d], ks[:, hh * d:(hh + 1) * d]
        for r in range(GDN_REP):
            col = (hh * GDN_REP + r) * d
            for lst, val in zip(per, (q, k, vs[:, col:col + d]) + _gdn_pick(small, j * GDN_HB + hh, r)):
                lst.append(val)
    return tuple(jnp.stack(lst) for lst in per)


def _gdn_specs(nchunk, rev):
    c, d = GDN_CHUNK, GDN_HEAD_DIM
    at = (lambda n: nchunk - 1 - n) if rev else (lambda n: n)
    ng = GDN_K_HEADS // GDN_HB
    return at, [pl.BlockSpec((c, GDN_HB * d), lambda n, j: (at(n), j)),
                pl.BlockSpec((c, GDN_HB * d), lambda n, j: (at(n), ng + j)),
                pl.BlockSpec((c, GDN_HB * GDN_REP * d), lambda n, j: (at(n), ng + j)),
                pl.BlockSpec((c, 2 * GDN_V_HEADS), lambda n, j: (at(n), 0)),
                pl.BlockSpec((1, GDN_V_HEADS), lambda n, j: (0, 0)),
                pl.BlockSpec((1, GDN_V_HEADS), lambda n, j: (0, 0))]


def _gdn_fwd(act, ba, alog, dtb, *, name):
    t = act.shape[0]
    c, d = GDN_CHUNK, GDN_HEAD_DIM
    nchunk = t // c
    at, specs = _gdn_specs(nchunk, False)

    def body(q_ref, k_ref, v_ref, ba_ref, al_ref, db_ref, o_ref, s_ref, inv_ref, state):
        n, j = pl.program_id(0), pl.program_id(1)
        heads = pl.ds(j * GDN_HB, GDN_HB)

        @pl.when(n == 0)
        def _():
            state[heads] = jnp.zeros((GDN_HB, GDN_REP, d, d), F32)

        s_in = state[heads]
        s_ref[...] = s_in
        args = _gdn_stack(q_ref[...], k_ref[...], v_ref[...], (ba_ref[...], al_ref[...], db_ref[...]), j)
        o, s_new, inv_ref[...] = _gdn_chunk(*args, s_in.reshape(GDN_HB * GDN_REP, d, d), keep_inverse=True)
        o_ref[...] = jnp.concatenate([o[b] for b in range(GDN_HB * GDN_REP)], axis=1)
        state[heads] = s_new.reshape(GDN_HB, GDN_REP, d, d)

    return pl.pallas_call(
        body, name=name, grid=(nchunk, GDN_K_HEADS // GDN_HB), in_specs=specs,
        out_specs=[pl.BlockSpec((c, GDN_HB * GDN_REP * d), lambda n, j: (n, j)),
                   pl.BlockSpec((None, GDN_HB, GDN_REP, d, d), lambda n, j: (n, j, 0, 0, 0)),
                   pl.BlockSpec((None, GDN_HB * GDN_REP, c, c), lambda n, j: (n, j, 0, 0))],
        out_shape=[jax.ShapeDtypeStruct((t, GDN_VW), F32),
                   jax.ShapeDtypeStruct((nchunk, GDN_K_HEADS, GDN_REP, d, d), F32),
                   jax.ShapeDtypeStruct((nchunk, GDN_V_HEADS, c, c), F32)],
        scratch_shapes=[pltpu.VMEM((GDN_K_HEADS, GDN_REP, d, d), F32)],
        compiler_params=_params("arbitrary", "arbitrary"),
    )(act, act, act, ba, alog, dtb)


def _gdn_bwd(act, ba, alog, dtb, states, inverses, dout, *, name):
    t = act.shape[0]
    c, d = GDN_CHUNK, GDN_HEAD_DIM
    nchunk = t // c
    at, specs = _gdn_specs(nchunk, True)

    def body(q_ref, k_ref, v_ref, ba_ref, al_ref, db_ref, s_ref, inv_ref, do_ref,
             dq_ref, dk_ref, dv_ref, dba_ref, dal_ref, ddb_ref, dstate):
        n, j = pl.program_id(0), pl.program_id(1)

        @pl.when(n == 0)
        def _():
            dstate[pl.ds(j * GDN_HB, GDN_HB)] = jnp.zeros((GDN_HB, GDN_REP, d, d), F32)

        @pl.when((n == 0) & (j == 0))
        def _():
            dal_ref[...] = jnp.zeros_like(dal_ref)
            ddb_ref[...] = jnp.zeros_like(ddb_ref)

        @pl.when(j == 0)
        def _():
            dba_ref[...] = jnp.zeros_like(dba_ref)

        heads = pl.ds(j * GDN_HB, GDN_HB)
        nh = GDN_HB * GDN_REP
        args = _gdn_stack(q_ref[...], k_ref[...], v_ref[...], (ba_ref[...], al_ref[...], db_ref[...]), j)
        _, vjp = jax.vjp(functools.partial(_gdn_chunk, inverse=inv_ref[...]), *args, s_ref[...].reshape(nh, d, d))
        do = do_ref[...]
        do = jnp.stack([do[:, b * d:(b + 1) * d] for b in range(nh)])
        gq, gk, gv, gbl, ga, gal, gdb, gs = vjp((do, dstate[heads].reshape(nh, d, d)))
        dstate[heads] = gs.reshape(GDN_HB, GDN_REP, d, d)
        dq_ref[...] = jnp.concatenate([gq[GDN_REP * hh] + gq[GDN_REP * hh + 1] for hh in range(GDN_HB)], axis=1)
        dk_ref[...] = jnp.concatenate([gk[GDN_REP * hh] + gk[GDN_REP * hh + 1] for hh in range(GDN_HB)], axis=1)
        dv_ref[...] = jnp.concatenate([gv[b] for b in range(nh)], axis=1)
        lane = lax.broadcasted_iota(jnp.int32, (c, 2 * GDN_V_HEADS), 1)
        lane1 = lax.broadcasted_iota(jnp.int32, (1, GDN_V_HEADS), 1)
        dba = jnp.zeros((c, 2 * GDN_V_HEADS), F32)
        dal = jnp.zeros((1, GDN_V_HEADS), F32)
        ddb = jnp.zeros((1, GDN_V_HEADS), F32)
        for b in range(nh):
            hv = j * nh + b
            dba = dba + jnp.where(lane == hv, gbl[b], 0.0) + jnp.where(lane == GDN_V_HEADS + hv, ga[b], 0.0)
            dal = dal + jnp.where(lane1 == hv, gal[b], 0.0)
            ddb = ddb + jnp.where(lane1 == hv, gdb[b], 0.0)
        dba_ref[...] += dba
        dal_ref[...] += dal
        ddb_ref[...] += ddb

    small = pl.BlockSpec((1, GDN_V_HEADS), lambda n, j: (0, 0))
    return pl.pallas_call(
        body, name=name, grid=(nchunk, GDN_K_HEADS // GDN_HB),
        in_specs=specs + [pl.BlockSpec((None, GDN_HB, GDN_REP, d, d), lambda n, j: (at(n), j, 0, 0, 0)),
                          pl.BlockSpec((None, GDN_HB * GDN_REP, c, c), lambda n, j: (at(n), j, 0, 0)),
                          pl.BlockSpec((c, GDN_HB * GDN_REP * d), lambda n, j: (at(n), j))],
        out_specs=[pl.BlockSpec((c, GDN_HB * d), lambda n, j: (at(n), j)),
                   pl.BlockSpec((c, GDN_HB * d), lambda n, j: (at(n), j)),
                   pl.BlockSpec((c, GDN_HB * GDN_REP * d), lambda n, j: (at(n), j)),
                   pl.BlockSpec((c, 2 * GDN_V_HEADS), lambda n, j: (at(n), 0)),
                   small, small],
        out_shape=[jax.ShapeDtypeStruct((t, GDN_KW), F32), jax.ShapeDtypeStruct((t, GDN_KW), F32),
                   jax.ShapeDtypeStruct((t, GDN_VW), F32), jax.ShapeDtypeStruct((t, 2 * GDN_V_HEADS), F32),
                   jax.ShapeDtypeStruct((1, GDN_V_HEADS), F32), jax.ShapeDtypeStruct((1, GDN_V_HEADS), F32)],
        scratch_shapes=[pltpu.VMEM((GDN_K_HEADS, GDN_REP, d, d), F32)],
        compiler_params=_params("arbitrary", "arbitrary"),
    )(act, act, act, ba, alog, dtb, states, inverses, dout)


N_DEV = 8
ANY = pl.BlockSpec(memory_space=pl.ANY)


def _coords():
    return lax.axis_index("x"), lax.axis_index("y"), lax.axis_index("c")


def _other_chips(x, y):
    return [(1 - x, y), (x, 1 - y), (1 - x, 1 - y)]


def _remote(src, dst, send_sems, recv_sems, k, to):
    return pltpu.make_async_remote_copy(src_ref=src, dst_ref=dst, send_sem=send_sems.at[k], recv_sem=recv_sems.at[k],
                                        device_id=to, device_id_type=MESH)


def _dma_sems(n):
    return [pltpu.SemaphoreType.DMA((n,)), pltpu.SemaphoreType.DMA((n,))]


def _gather_copies(ins, outs, send_sems, recv_sems, local_sems, only_first=False):
    x, y, c = _coords()
    sibling = (x, y, 1 - c)
    local, sends, arrivals, relays, relayed = [], [], [], [], []
    for a, (x_ref, out_ref) in enumerate(zip(ins, outs)):
        local.append(pltpu.make_async_copy(x_ref, out_ref.at[2 * x + y], local_sems.at[a]))
        for j, (cx, cy) in enumerate(_other_chips(x, y)):
            k, theirs = 6 * a + j, 2 * cx + cy
            sends.append(_remote(x_ref.at[c], out_ref.at[2 * x + y, c], send_sems, recv_sems, k, (cx, cy, c)))
            if only_first:
                continue
            arrivals.append(_remote(x_ref.at[c], out_ref.at[theirs, c], send_sems, recv_sems, k, (cx, cy, c)))
            relays.append(_remote(out_ref.at[theirs, c], out_ref.at[theirs, c], send_sems, recv_sems, k + 3, sibling))
            relayed.append(_remote(x_ref.at[c], out_ref.at[theirs, 1 - c], send_sems, recv_sems, k + 3, sibling))
    return local, sends, arrivals, relays, relayed


def _gather_start(copies):
    local, sends, _, _, _ = copies
    for cp in local + sends:
        cp.start()


def _gather_finish(copies):
    local, sends, arrivals, relays, relayed = copies
    for landed, relay in zip(arrivals, relays):
        landed.wait_recv()
        relay.start()
    for cp in relayed:
        cp.wait_recv()
    for cp in sends + relays:
        cp.wait_send()
    for cp in local:
        cp.wait()


def _gather_scratch(na):
    return _dma_sems(6 * na) + [pltpu.SemaphoreType.DMA((na,))]


def _gather_quarters(parts, *, name):
    na = len(parts)

    def body(*refs):
        copies = _gather_copies(refs[:na], refs[na:2 * na], *refs[2 * na:])
        _gather_start(copies)
        _gather_finish(copies)

    return pl.pallas_call(
        body, name=name, in_specs=[ANY] * na, out_specs=[ANY] * na,
        out_shape=[jax.ShapeDtypeStruct((N_CHIPS,) + p.shape, p.dtype) for p in parts],
        scratch_shapes=_gather_scratch(na),
    )(*parts)


def _swap_halves(grads, *, name):
    na = len(grads)

    def body(*refs):
        ins, outs = refs[:na], refs[na:2 * na]
        send_sems, recv_sems = refs[2 * na:]
        x, y, c = _coords()
        sends = [_remote(g_ref.at[j, 1 - c], o_ref.at[j], send_sems, recv_sems, N_CHIPS * a + j, (x, y, 1 - c))
                 for a, (g_ref, o_ref) in enumerate(zip(ins, outs)) for j in range(N_CHIPS)]
        for cp in sends:
            cp.start()
        for cp in sends:
            cp.wait()

    return pl.pallas_call(
        body, name=name, in_specs=[ANY] * na, out_specs=[ANY] * na,
        out_shape=[jax.ShapeDtypeStruct((N_CHIPS,) + g.shape[2:], g.dtype) for g in grads],
        scratch_shapes=_dma_sems(N_CHIPS * na),
    )(*grads)


def _scatter_copies(ins, outs, send_sems, recv_sems):
    x, y, c = _coords()
    return [_remote(p_ref.at[2 * cx + cy], outs[3 * a + j], send_sems, recv_sems, 3 * a + j, (cx, cy, c))
            for a, p_ref in enumerate(ins) for j, (cx, cy) in enumerate(_other_chips(x, y))]


def _scatter_shapes(pairs):
    return [jax.ShapeDtypeStruct(p.shape[1:], p.dtype) for p in pairs for _ in range(3)]


def _scatter_quarters(pairs, *, name):
    na = len(pairs)

    def body(*refs):
        sends = _scatter_copies(refs[:na], refs[na:4 * na], *refs[4 * na:])
        for cp in sends:
            cp.start()
        for cp in sends:
            cp.wait()

    out = pl.pallas_call(
        body, name=name, in_specs=[ANY] * na, out_specs=[ANY] * (3 * na), out_shape=_scatter_shapes(pairs),
        scratch_shapes=_dma_sems(3 * na),
    )(*pairs)
    return [out[3 * a:3 * a + 3] for a in range(na)]


def _share_halves(tots, *, name):
    na = len(tots)

    def body(*refs):
        ins, outs = refs[:na], refs[na:2 * na]
        send_sems, recv_sems = refs[2 * na:]
        x, y, c = _coords()
        sends = [_remote(t_ref, o_ref, send_sems, recv_sems, a, (x, y, 1 - c))
                 for a, (t_ref, o_ref) in enumerate(zip(ins, outs))]
        for cp in sends:
            cp.start()
        for cp in sends:
            cp.wait()

    return pl.pallas_call(
        body, name=name, in_specs=[ANY] * na, out_specs=[ANY] * na,
        out_shape=[jax.ShapeDtypeStruct(t.shape, t.dtype) for t in tots],
        scratch_shapes=_dma_sems(na),
    )(*tots)


def _gather_all(vec, *, name):
    m, w = vec.shape

    def body(x_ref, out_ref, send_sems, recv_sems, local_sem):
        x, y, c = _coords()
        me, sibling = (x, y, c), (x, y, 1 - c)
        chips = _other_chips(x, y)

        def rows(px, py, pc):
            return out_ref.at[pl.ds((4 * px + 2 * py + pc) * m, m), :]

        def copy(k, block, to, src=None):
            return _remote(rows(*block) if src is None else src, rows(*block), send_sems, recv_sems, k, to)

        mine = pltpu.make_async_copy(x_ref, rows(*me), local_sem)
        mine.start()
        first = [copy(0, me, sibling, src=x_ref)]
        first += [copy(1 + j, me, (*chip, c), src=x_ref) for j, chip in enumerate(chips)]
        for cp in first:
            cp.start()
        passed = [copy(4 + j, (*chip, c), sibling) for j, chip in enumerate(chips)]
        for j, chip in enumerate(chips):
            copy(1 + j, (*chip, c), me).wait_recv()
            passed[j].start()
        copy(0, sibling, me).wait_recv()
        for j, chip in enumerate(chips):
            copy(4 + j, (*chip, 1 - c), me).wait_recv()
        for cp in first + passed:
            cp.wait_send()
        mine.wait()

    vm = pl.BlockSpec(memory_space=pltpu.VMEM)
    return pl.pallas_call(
        body, name=name, in_specs=[vm], out_specs=vm, out_shape=jax.ShapeDtypeStruct((N_DEV * m, w), vec.dtype),
        scratch_shapes=_dma_sems(7) + [pltpu.SemaphoreType.DMA(())],
    )(vec)


def _sum_blocks(allv, n, *, name):
    m = allv.shape[0] // n

    def body(a_ref, o_ref):
        acc = a_ref[0:m, :]
        for d in range(1, n):
            acc = acc + a_ref[d * m:(d + 1) * m, :]
        o_ref[...] = acc

    return pl.pallas_call(body, name=name, out_shape=jax.ShapeDtypeStruct((m, allv.shape[1]), allv.dtype))(allv)


EW_BLOCK_BYTES = 1 << 20


def _ew_rows(rows, w):
    return _tile(rows, max(8, (EW_BLOCK_BYTES // (4 * w)) // 8 * 8), 8)


def _add_pair(g, got, c, *, name):
    _, _, rows, w = g.shape
    tr = _ew_rows(rows, w)

    def body(c_ref, g_ref, got_ref, o_ref):
        o_ref[...] = (g_ref[...] + got_ref[...]).astype(o_ref.dtype)

    blk = pl.BlockSpec((None, tr, w), lambda q, i, c_ref: (q, i, 0))
    return pl.pallas_call(
        body, name=name,
        grid_spec=pltpu.PrefetchScalarGridSpec(
            num_scalar_prefetch=1, grid=(N_CHIPS, rows // tr),
            in_specs=[pl.BlockSpec((None, None, tr, w), lambda q, i, c_ref: (q, c_ref[0], i, 0)), blk], out_specs=blk),
        out_shape=jax.ShapeDtypeStruct(got.shape, BF16),
        compiler_params=_params("parallel", "parallel"),
    )(c, g, got)


def _add_chips(pair, recv, chip, *, name):
    _, rows, w = pair.shape
    tr = _ew_rows(rows, w)

    def body(chip_ref, p_ref, r0_ref, r1_ref, r2_ref, o_ref):
        f = lambda r: r[...].astype(F32)
        o_ref[...] = ((f(p_ref) + f(r0_ref)) + f(r1_ref)) + f(r2_ref)

    blk = pl.BlockSpec((tr, w), lambda i, chip_ref: (i, 0))
    return pl.pallas_call(
        body, name=name,
        grid_spec=pltpu.PrefetchScalarGridSpec(
            num_scalar_prefetch=1, grid=(rows // tr,),
            in_specs=[pl.BlockSpec((None, tr, w), lambda i, chip_ref: (chip_ref[0], i, 0)), blk, blk, blk], out_specs=blk),
        out_shape=jax.ShapeDtypeStruct((rows, w), F32),
        compiler_params=_params("parallel"),
    )(chip, pair, *recv)


def _adamw_math(w, g, m, v):
    nm = ADAM_B1 * m + (1.0 - ADAM_B1) * g
    nv = ADAM_B2 * v + (1.0 - ADAM_B2) * (g * g)
    m_hat = nm / (1.0 - ADAM_B1 ** ADAM_STEP)
    v_hat = nv / (1.0 - ADAM_B2 ** ADAM_STEP)
    return -ADAM_LR * (m_hat / (jnp.sqrt(v_hat) + ADAM_EPS) + ADAM_WD * w), nm, nv


def _adamw(w, g, m, v, *, name):
    shape = w.shape
    last = shape[-1]
    w2, g2, m2, v2 = (a.reshape(-1, last) for a in (w, g, m, v))
    rows = w2.shape[0]
    tm = _ew_rows(rows, last)

    def body(w_ref, g_ref, m_ref, v_ref, d_ref, nm_ref, nv_ref):
        d_ref[...], nm_ref[...], nv_ref[...] = _adamw_math(w_ref[...], g_ref[...], m_ref[...], v_ref[...])

    spec = pl.BlockSpec((tm, last), lambda i: (i, 0))
    out = jax.ShapeDtypeStruct((rows, last), F32)
    d, nm, nv = pl.pallas_call(
        body, name=name, grid=(rows // tm,), in_specs=[spec] * 4, out_specs=[spec] * 3, out_shape=[out] * 3,
        compiler_params=_params("parallel"),
    )(w2, g2, m2, v2)
    return d.reshape(shape), nm.reshape(shape), nv.reshape(shape)


def _adamw_halves(w, m, v, mine, theirs, c, *, name, into=None):
    rows, wd = w.shape[-2:]
    tr = _ew_rows(rows, wd)
    bufs, at = into if into is not None else ((), ())

    def body(c_ref, w_ref, m_ref, v_ref, a_ref, b_ref, *rest):
        g_ref, d_ref, nm_ref, nv_ref = rest[len(bufs):]
        g = jnp.where(pl.program_id(0) == c_ref[0], a_ref[...], b_ref[...])
        g_ref[...] = g
        d_ref[...], nm_ref[...], nv_ref[...] = _adamw_math(w_ref[...], g, m_ref[...], v_ref[...])

    full = pl.BlockSpec((None,) * (1 + len(at)) + (tr, wd), lambda hf, i, c_ref: at + (hf, i, 0))
    half = pl.BlockSpec((tr, wd), lambda hf, i, c_ref: (i, 0))
    out = jax.ShapeDtypeStruct(w.shape, F32)
    return pl.pallas_call(
        body, name=name,
        grid_spec=pltpu.PrefetchScalarGridSpec(num_scalar_prefetch=1, grid=(2, rows // tr),
                                               in_specs=[full] * 3 + [half] * 2 + [ANY] * len(bufs),
                                               out_specs=[full] * 4),
        out_shape=[out] * 4, input_output_aliases={6 + b: b for b in range(len(bufs))},
        compiler_params=_params("parallel", "parallel"),
    )(c, w, m, v, mine, theirs, *bufs)


def _join_quarters(q, *, name):
    _, rows, n = q.shape
    tr = _tile(rows, 256, 16)

    def body(q_ref, o_ref):
        o_ref[...] = jnp.concatenate([q_ref[s] for s in range(N_CHIPS)], axis=1)

    return pl.pallas_call(
        body, name=name, grid=(rows // tr,),
        in_specs=[pl.BlockSpec((N_CHIPS, tr, n), lambda i: (0, i, 0))],
        out_specs=pl.BlockSpec((tr, N_CHIPS * n), lambda i: (i, 0)),
        out_shape=jax.ShapeDtypeStruct((rows, N_CHIPS * n), q.dtype),
        compiler_params=_params("parallel"),
    )(q)


def _split_quarters(full, *, name):
    rows, n4 = full.shape
    n = n4 // N_CHIPS
    tr = _tile(rows, 256, 16)

    def body(x_ref, o_ref):
        x = x_ref[...]
        for s in range(N_CHIPS):
            o_ref[s] = x[:, s * n:(s + 1) * n]

    return pl.pallas_call(
        body, name=name, grid=(rows // tr,),
        in_specs=[pl.BlockSpec((tr, n4), lambda i: (i, 0))],
        out_specs=pl.BlockSpec((N_CHIPS, tr, n), lambda i: (0, i, 0)),
        out_shape=jax.ShapeDtypeStruct((N_CHIPS, rows, n), full.dtype),
        compiler_params=_params("parallel"),
    )(full)


_WEIGHTS = ['ffn_norm', 'ffn_w_gate', 'ffn_w_up', 'ffn_w_down', 'mix_norm', 'att_w_in', 'att_q_norm', 'att_k_norm',
            'att_sinks', 'att_w_out', 'gdn_w_in', 'gdn_conv_w', 'gdn_a_log', 'gdn_dt_bias', 'gdn_out_norm', 'gdn_w_out',
            'ple_norm', 'ple_w_gate', 'ple_w_proj']
_BIG = ['ffn_w_gate', 'ffn_w_up', 'ffn_w_down', 'att_w_in', 'att_w_out', 'gdn_w_in', 'gdn_w_out', 'ple_w_gate',
        'ple_w_proj']
_SMALL_CUT = {'ffn_norm': 2, 'gdn_conv_w': 2}
_WHOLE = ['mix_norm', 'att_q_norm', 'att_k_norm', 'att_sinks', 'gdn_a_log', 'gdn_dt_bias', 'gdn_out_norm', 'ple_norm']
PACK_W = 1024
SMALL_ROW_MULT = 8


def _halves(a):
    return a.reshape(2, -1, a.shape[-1])


def _from_quarters(blk, axis):
    full = jnp.moveaxis(blk, 0, axis)
    shp = list(full.shape)
    shp[axis:axis + 2] = [shp[axis] * shp[axis + 1]]
    return full.reshape(shp)


def _to_quarters(full, axis):
    shp = list(full.shape)
    shp[axis:axis + 1] = [N_CHIPS, shp[axis] // N_CHIPS]
    return jnp.moveaxis(full.reshape(shp), axis, 0)


def _pack(parts, row_mult):
    flat = jnp.concatenate(parts, axis=-1)
    n = flat.shape[-1]
    rows = -(-n // (PACK_W * row_mult)) * row_mult
    return jnp.pad(flat, [(0, rows * PACK_W - n)]).reshape(rows, PACK_W)


def _unpack(flat, shapes):
    lead = flat.shape[:-2]
    flat = flat.reshape(lead + (-1,))
    out, off = [], 0
    for shp in shapes:
        n = math.prod(shp)
        out.append(flat[..., off:off + n].reshape(lead + tuple(shp)))
        off += n
    return out


FFN_TM = 1024


def _ffn_up(hn, wg, wu, at, *, name):
    t, d = hn.shape
    fq = wg.shape[-1]
    tm = _tile(t, FFN_TM)

    def body(h_ref, wg_ref, wu_ref, g_ref, u_ref, a_ref):
        h = h_ref[...]
        g, u = _dg(h, _b(wg_ref[...]), 1, 0), _dg(h, _b(wu_ref[...]), 1, 0)
        g_ref[...] = g.astype(BF16)
        u_ref[...] = u.astype(BF16)
        a_ref[...] = _f_swiglu(g, u)[0].astype(BF16)

    w_spec = pl.BlockSpec((None,) * (1 + len(at)) + (d, fq), lambda s, i: (s,) + at + (0, 0))
    o_spec = pl.BlockSpec((None, tm, fq), lambda s, i: (s, i, 0))
    out = jax.ShapeDtypeStruct((N_CHIPS, t, fq), BF16)
    return pl.pallas_call(
        body, name=name, grid=(N_CHIPS, t // tm),
        in_specs=[pl.BlockSpec((tm, d), lambda s, i: (i, 0)), w_spec, w_spec], out_specs=[o_spec] * 3,
        out_shape=[out] * 3, compiler_params=_params("parallel", "parallel"),
    )(hn, wg, wu)


def _ffn_d_up(dout, wd, g, u, at, *, name):
    t, d = dout.shape
    fq = wd.shape[-2]
    tm = _tile(t, FFN_TM)

    def body(do_ref, wd_ref, g_ref, u_ref, dg_ref, du_ref):
        da = _dg(_b(do_ref[...]), _b(wd_ref[...]), 1, 1) * 0.5
        _, vjp = jax.vjp(_f_swiglu, g_ref[...].astype(F32), u_ref[...].astype(F32))
        dg, du = vjp((da,))
        dg_ref[...] = dg.astype(BF16)
        du_ref[...] = du.astype(BF16)

    w_spec = pl.BlockSpec((None,) * (1 + len(at)) + (fq, d), lambda s, i: (s,) + at + (0, 0))
    o_spec = pl.BlockSpec((None, tm, fq), lambda s, i: (s, i, 0))
    out = jax.ShapeDtypeStruct((N_CHIPS, t, fq), BF16)
    return pl.pallas_call(
        body, name=name, grid=(N_CHIPS, t // tm),
        in_specs=[pl.BlockSpec((tm, d), lambda s, i: (i, 0)), w_spec, o_spec, o_spec], out_specs=[o_spec] * 2,
        out_shape=[out] * 2, compiler_params=_params("parallel", "parallel"),
    )(dout, wd, g, u)


def _ffn_fwd(h, gain, wg, wu, wd, at, tag):
    lead = (Q,) + at
    hn, = _row_fwd(_f_rms, [h], [gain], [(D_MODEL, BF16)], name=f"{tag}_norm")
    g, u, a = _ffn_up(hn, wg, wu, at, name=f"{tag}_up")
    out = _mm((a, (Q,)), (wd, lead), res=h, scale=0.5, name=f"{tag}_down")
    return out, (h, hn, g, u, a)


def _ffn_bwd(dout, saved, gain, wg, wu, wd, at, grads, g_at, tag):
    h, hn, g, u, a = saved
    lead = (Q,) + at
    g_lead = (Q,) + g_at
    dg, du = _ffn_d_up(dout, wd, g, u, at, name=f"{tag}_d_up")
    g_gate, g_up, g_down = grads
    g_down = _mm((a, (Q,)), dout, ta=True, scale=0.5, into=(g_down, g_lead), name=f"{tag}_dw_down")
    g_gate = _mm((dg, (Q,)), hn, ta=True, into=(g_gate, g_lead), name=f"{tag}_dw_gate")
    g_up = _mm((du, (Q,)), hn, ta=True, into=(g_up, g_lead), name=f"{tag}_dw_up")
    dhn = _mm((dg, (Q,)), (wg, lead), tb=True, name=f"{tag}_d_norm_gate")
    dh, dgain = _mm((du, (Q,)), (wu, lead), tb=True, res=dhn, norm_bwd=(h, gain, dout), name=f"{tag}_d_in")
    return dh, dgain, (g_gate, g_up, g_down)


def _att_fwd(h, gain, w_in, qg, kg, sinks, w_out, gather):
    hn, = _row_fwd(_f_rms, [h], [gain], [(D_MODEL, BF16)], name="att_norm")
    proj = _mm(hn, w_in, out_dtype=BF16, name="att_in")
    a, rtot, *gathered = _sb_fwd(proj, name="att_sb", gather=gather)
    b = _swa_fwd(proj, qg, kg, sinks, name="att_swa")
    out = _mm(a, (w_out, (0,)), res=h, name="att_out_sb")
    out = _mm(b, (w_out, (1,)), res=out, name="att_out_swa")
    return out, (h, hn, proj, a, rtot, b), gathered


def _att_bwd(dout, saved, gain, w_in, qg, kg, sinks, w_out, scatter):
    h, hn, proj, a, rtot, b = saved
    da = _mm(dout, (w_out, (0,)), tb=True, name="att_d_sb")
    db = _mm(dout, (w_out, (1,)), tb=True, name="att_d_swa")
    dw_out = lax.empty(w_out.shape, F32)
    dw_out = _mm(a, dout, ta=True, into=(dw_out, (0,)), name="att_dw_out_sb")
    dw_out = _mm(b, dout, ta=True, into=(dw_out, (1,)), name="att_dw_out_swa")
    dq, dk, dv, *landed = _sb_bwd(proj, rtot, da, name="att_sb_bwd", scatter=scatter)
    dqb, dkb, dvb, dqg, dkg, dsk = _swa_bwd(proj, qg, kg, sinks, db, name="att_swa_bwd")
    dproj = jnp.concatenate([dq, dk, dv, dqb, dkb, dvb], axis=1)
    dw_in = _mm(hn, dproj, ta=True, name="att_dw_in")
    dh, dgain = _mm(dproj, w_in, tb=True, norm_bwd=(h, gain, dout), name="att_d_in")
    return dh, dgain, dw_in, dqg, dkg, dsk, dw_out, [landed[3 * a:3 * a + 3] for a in range(len(scatter))]


def _gdn_layer_fwd(h, gain, w_in, conv_w, alog, dtb, out_gain, w_out):
    w_qkv, w_z, w_ba = w_in[:, :GDN_CONV_W], w_in[:, GDN_CONV_W:GDN_CONV_W + GDN_VW], w_in[:, GDN_CONV_W + GDN_VW:]
    hn, = _row_fwd(_f_rms, [h], [gain], [(D_MODEL, BF16)], name="gdn_norm")
    pq = _mm(hn, w_qkv, name="gdn_in_qkv")
    pz = _mm(hn, w_z, name="gdn_in_z")
    ba = _mm(hn, w_ba, name="gdn_in_ba")
    act = _conv_fwd(pq, conv_w, name="gdn_conv")
    o, states, inverses = _gdn_fwd(act, ba, alog, dtb, name="gdn_rule")
    y, = _row_fwd(_f_gdn_out, [o, pz], [out_gain], [(GDN_VW, BF16)], name="gdn_gate")
    out = _mm(y, w_out, res=h, name="gdn_out")
    return out, (h, hn, pq, pz, ba, act, o, states, inverses, y, (w_qkv, w_z, w_ba))


def _gdn_layer_bwd(dout, saved, gain, conv_w, alog, dtb, out_gain, w_out):
    h, hn, pq, pz, ba, act, o, states, inverses, y, (w_qkv, w_z, w_ba) = saved
    dy = _mm(dout, w_out, tb=True, name="gdn_d_gate")
    dw_out = _mm(y, dout, ta=True, name="gdn_dw_out")
    do, dpz, dout_gain = _row_bwd(_f_gdn_out, [o, pz], [out_gain], [dy], [(0, F32), (1, F32)], [0], name="gdn_gate_bwd")
    dq, dk, dv, dba, dal, ddb = _gdn_bwd(act, ba, alog, dtb, states, inverses, do, name="gdn_rule_bwd")
    dact = jnp.concatenate([dq, dk, dv], axis=1)
    dc = _conv_fwd(pq, conv_w, dact, name="gdn_conv_d_pre")
    dpq, dconv = _conv_bwd(pq, conv_w, dc, name="gdn_conv_bwd")
    dw_in = jnp.concatenate([_mm(hn, dpq, ta=True, name="gdn_dw_qkv"), _mm(hn, dpz, ta=True, name="gdn_dw_z"),
                             _mm(hn, dba, ta=True, name="gdn_dw_ba")], axis=1)
    dhn = _mm(dpq, w_qkv, tb=True, name="gdn_d_norm_qkv")
    dhn = _mm(dpz, w_z, tb=True, res=dhn, name="gdn_d_norm_z")
    dh, dgain = _mm(dba, w_ba, tb=True, res=dhn, norm_bwd=(h, gain, dout), name="gdn_d_in")
    return dh, dgain, dw_in, dconv, dal, ddb, dout_gain, dw_out


def _ple_fwd(h, gain, w_gate, w_proj, pe, tag):
    hn, = _row_fwd(_f_rms, [h], [gain], [(D_MODEL, BF16)], name=f"{tag}_norm")
    gl = _mm(hn, w_gate, name=f"{tag}_gate")
    pp = _mm(pe, w_proj, name=f"{tag}_proj")
    out, = _row_fwd(_f_ple, [h, gl, pp], [], [(D_MODEL, F32)], name=f"{tag}_mix")
    return out, (h, hn, gl, pp)


def _ple_bwd(dout, saved, gain, w_gate, pe, tag):
    h, hn, gl, pp = saved
    dha, dgl, dpp = _row_bwd(_f_ple, [h, gl, pp], [], [dout], [(0, F32), (1, BF16), (2, BF16)], [], name=f"{tag}_mix_bwd")
    dw_gate = _mm(hn, dgl, ta=True, name=f"{tag}_dw_gate")
    dw_proj = _mm(pe, dpp, ta=True, name=f"{tag}_dw_proj")
    dh, dgain = _mm(dgl, w_gate, tb=True, norm_bwd=(h, gain, dha), name=f"{tag}_d_in")
    return dh, dgain, dw_gate, dw_proj


def kernel(x, p, ffn_norm, ffn_w_gate, ffn_w_up, ffn_w_down, mix_norm, att_w_in, att_q_norm, att_k_norm, att_sinks, att_w_out, gdn_w_in, gdn_conv_w, gdn_a_log, gdn_dt_bias, gdn_out_norm, gdn_w_out, ple_norm, ple_w_gate, ple_w_proj, loss_target, m_ffn_norm, m_ffn_w_gate, m_ffn_w_up, m_ffn_w_down, m_mix_norm, m_att_w_in, m_att_q_norm, m_att_k_norm, m_att_sinks, m_att_w_out, m_gdn_w_in, m_gdn_conv_w, m_gdn_a_log, m_gdn_dt_bias, m_gdn_out_norm, m_gdn_w_out, m_ple_norm, m_ple_w_gate, m_ple_w_proj, v_ffn_norm, v_ffn_w_gate, v_ffn_w_up, v_ffn_w_down, v_mix_norm, v_att_w_in, v_att_q_norm, v_att_k_norm, v_att_sinks, v_att_w_out, v_gdn_w_in, v_gdn_conv_w, v_gdn_a_log, v_gdn_dt_bias, v_gdn_out_norm, v_gdn_w_out, v_ple_norm, v_ple_w_gate, v_ple_w_proj):
    arg = dict(locals())
    cx, cy, cc = _coords()
    chip = (2 * cx + cy).astype(jnp.int32).reshape(1)
    core = cc.astype(jnp.int32).reshape(1)
    n_layers = ffn_norm.shape[0]

    quarter = lambda n, i=None: _halves((arg[n] if i is None else arg[n][i]).astype(BF16))
    ffn_names = ('ffn_w_gate', 'ffn_w_up', 'ffn_w_down')
    early = [quarter(n, 0) for n in ffn_names] + [quarter('att_w_in'), quarter('att_w_out')]
    late_names = ('gdn_w_in', 'gdn_w_out', 'ple_w_gate', 'ple_w_proj')
    late = [quarter(n, 1) for n in ffn_names] + [quarter(n) for n in late_names]
    *ffn_w0, att_in_q, att_out_q = _gather_quarters(early, name="gather_weights")
    wt = {'att_w_in': _join_quarters(att_in_q.reshape((N_CHIPS,) + att_w_in.shape[1:]), name="att_w_in_join"),
          'att_w_out': att_out_q.reshape(2, SB_W, D_MODEL)}

    small_names = list(_SMALL_CUT)
    small_shapes = [arg[n].shape for n in small_names]
    svec = _pack([arg[n].reshape(-1) for n in small_names], SMALL_ROW_MULT)
    srows = svec.shape[0]
    sall = _gather_all(svec, name="gather_gains").reshape(N_CHIPS, 2, srows, PACK_W)[:, 0]
    for n, q in zip(small_names, _unpack(sall, small_shapes)):
        wt[n] = _from_quarters(q, _SMALL_CUT[n])
    row = lambda v: v.reshape(1, -1)

    as_ffn = lambda g, n: g.reshape((N_CHIPS,) + arg[n].shape[1:])
    ffn_w = [tuple(as_ffn(g, n) for g, n in zip(ffn_w0, ffn_names)), None]
    h = x[0]
    tape = []
    for i in range(n_layers):
        j = i // 2
        h, s0 = _ffn_fwd(h, row(wt['ffn_norm'][i, 0]), *ffn_w[i], (0,), f"ffn{i}a")
        if i % 2 == 0:
            h, sm, gathered = _att_fwd(h, row(mix_norm[i]), wt['att_w_in'], att_q_norm[j:j + 1], att_k_norm[j:j + 1],
                                       att_sinks[j:j + 1], wt['att_w_out'], late)
            ffn_w[1] = tuple(as_ffn(g, n) for g, n in zip(gathered[:3], ffn_names))
            wq = {n: g.reshape((N_CHIPS,) + arg[n].shape) for n, g in zip(late_names, gathered[3:])}
            wt['gdn_w_in'] = _join_quarters(wq['gdn_w_in'][:, 0], name="gdn_w_in_join")
            wt['gdn_w_out'] = wq['gdn_w_out'].reshape(GDN_VW, D_MODEL)
            wt['ple_w_gate'] = _from_quarters(wq['ple_w_gate'], 1)
            wt['ple_w_proj'] = _from_quarters(wq['ple_w_proj'], 2)
        else:
            h, sm = _gdn_layer_fwd(h, row(mix_norm[i]), wt['gdn_w_in'], wt['gdn_conv_w'][j], gdn_a_log[j:j + 1],
                                   gdn_dt_bias[j:j + 1], gdn_out_norm[j:j + 1], wt['gdn_w_out'])
        h, s1 = _ffn_fwd(h, row(wt['ffn_norm'][i, 1]), *ffn_w[i], (1,), f"ffn{i}b")
        h, sp = _ple_fwd(h, row(ple_norm[i]), wt['ple_w_gate'][i], wt['ple_w_proj'][i], p[i, 0], f"ple{i}")
        tape.append((s0, sm, s1, sp))

    dh, loss_local = _loss_head(h, loss_target[0], name="loss_head")
    loss = lax.psum(loss_local, ("x", "y", "c"))

    gr = {}
    stored_t = ('ffn_w_gate', 'ffn_w_up')
    as_stored = lambda a, n: jnp.swapaxes(a, -1, -2) if n in stored_t else a
    ffn_g = [tuple(lax.empty((N_CHIPS,) + as_stored(arg[n], n).shape[1:], F32) for n in ffn_names)
             for _ in range(n_layers)]
    d_ffn_norm = [[None, None] for _ in range(n_layers)]
    d_mix, d_ple_norm, d_ple_gate, d_ple_proj = [None] * n_layers, [None] * n_layers, [None] * n_layers, [None] * n_layers

    def as_halves(g):
        return g.reshape((N_CHIPS, 2, -1, g.shape[-1]))

    def pair_up(keys, grads, tag):
        got = _swap_halves(grads, name=f"grad_swap_halves_{tag}")
        return [_add_pair(g, o, core, name=f"grad_add_pair_{k}") for k, g, o in zip(keys, grads, got)]

    for i in reversed(range(n_layers)):
        j = i // 2
        s0, sm, s1, sp = tape[i]
        dh, d_ple_norm[i], d_ple_gate[i], d_ple_proj[i] = _ple_bwd(dh, sp, row(ple_norm[i]), wt['ple_w_gate'][i], p[i, 0],
                                                                   f"ple{i}")
        dh, d_ffn_norm[i][1], ffn_g[i] = _ffn_bwd(dh, s1, row(wt['ffn_norm'][i, 1]), *ffn_w[i], (1,), ffn_g[i], (1,),
                                                  f"ffn{i}b")
        if i % 2 == 0:
            gr['ple_w_gate'] = _to_quarters(jnp.stack(d_ple_gate), 1)
            gr['ple_w_proj'] = _to_quarters(jnp.stack(d_ple_proj), 2)
            first_keys = [f"{n}_1" for n in ffn_names] + list(late_names)
            first_pairs = pair_up(first_keys, [as_halves(g) for g in ffn_g[1]] + [as_halves(gr[n]) for n in late_names], "a")
            (dh, d_mix[i], dw_in, gr['att_q_norm'], gr['att_k_norm'], gr['att_sinks'], dw_out,
             first_recv) = _att_bwd(dh, sm, row(mix_norm[i]), wt['att_w_in'], att_q_norm[j:j + 1],
                                    att_k_norm[j:j + 1], att_sinks[j:j + 1], wt['att_w_out'], first_pairs)
            gr['att_w_in'] = _split_quarters(dw_in, name="att_dw_in_split")
            gr['att_w_out'] = dw_out
        else:
            (dh, d_mix[i], dw_in, dconv, gr['gdn_a_log'], gr['gdn_dt_bias'], gr['gdn_out_norm'],
             dw_out) = _gdn_layer_bwd(dh, sm, row(mix_norm[i]), wt['gdn_conv_w'][j], gdn_a_log[j:j + 1],
                                      gdn_dt_bias[j:j + 1], gdn_out_norm[j:j + 1], wt['gdn_w_out'])
            gr['gdn_w_in'] = _split_quarters(dw_in, name="gdn_dw_in_split")
            gr['gdn_w_out'] = dw_out
            gr['gdn_conv_w'] = dconv[None]
        dh, d_ffn_norm[i][0], ffn_g[i] = _ffn_bwd(dh, s0, row(wt['ffn_norm'][i, 0]), *ffn_w[i], (0,), ffn_g[i], (0,),
                                                  f"ffn{i}a")
    grad_x = dh[None]

    gr['ffn_norm'] = jnp.stack([jnp.stack([d_ffn_norm[i][k][0] for k in range(2)]) for i in range(n_layers)])
    gr['mix_norm'] = jnp.concatenate(d_mix, axis=0)
    gr['ple_norm'] = jnp.concatenate(d_ple_norm, axis=0)

    last_keys = [f"{n}_0" for n in ffn_names] + ['att_w_in', 'att_w_out']
    last_pairs = pair_up(last_keys, [as_halves(g) for g in ffn_g[0]] + [as_halves(gr['att_w_in']), as_halves(gr['att_w_out'])],
                         "b")
    last_recv = _scatter_quarters(last_pairs, name="grad_scatter")
    keys = first_keys + last_keys
    tots = [_add_chips(pr, rc, chip, name=f"grad_add_chips_{k}")
            for k, pr, rc in zip(keys, first_pairs + last_pairs, first_recv + last_recv)]
    theirs = _share_halves(tots, name="grad_share")
    summed = dict(zip(keys, zip(tots, theirs)))

    whole_shapes = [arg[n].shape for n in _WHOLE]
    cut_full_shapes = [gr[n].shape for n in small_names]
    gvec = _pack([gr[n].reshape(-1) for n in _WHOLE + small_names], SMALL_ROW_MULT)
    gall = _sum_blocks(_gather_all(gvec, name="gather_small_grads"), N_DEV, name="sum_small_grads")
    parts = _unpack(gall, whole_shapes + cut_full_shapes)
    gsum = dict(zip(_WHOLE, parts))
    for n, g in zip(small_names, parts[len(_WHOLE):]):
        gsum[n] = lax.dynamic_index_in_dim(_to_quarters(g, _SMALL_CUT[n]), chip[0], axis=0, keepdims=False)

    delta, new_m, new_v = {}, {}, {}
    for n in ('att_w_in', 'att_w_out') + late_names:
        res = _adamw_halves(_halves(arg[n]), _halves(arg["m_" + n]), _halves(arg["v_" + n]), *summed[n], core,
                            name=f"adamw_{n}")
        gsum[n], delta[n], new_m[n], new_v[n] = (r.reshape(arg[n].shape) for r in res)
    for n in ffn_names:
        wmv = [as_stored(arg[k + n], n) for k in ("", "m_", "v_")]
        res = tuple(lax.empty(wmv[0].shape, F32) for _ in range(4))
        for i in range(n_layers):
            res = _adamw_halves(*wmv, *summed[f"{n}_{i}"], core, name=f"adamw_{n}_{i}", into=(res, (i,)))
        gsum[n], delta[n], new_m[n], new_v[n] = (as_stored(r, n) for r in res)
    for n in _WHOLE + small_names:
        delta[n], new_m[n], new_v[n] = _adamw(arg[n], gsum[n], arg["m_" + n], arg["v_" + n], name=f"adamw_{n}")
    return (loss, grad_x, *[gsum[n] for n in _WEIGHTS], *[delta[n] for n in _WEIGHTS],
            *[new_m[n] for n in _WEIGHTS], *[new_v[n] for n in _WEIGHTS])
```

```python
import functools
import math

import jax
import jax.numpy as jnp
from jax import lax
from jax.experimental import pallas as pl
from jax.experimental.pallas import tpu as pltpu

F32 = jnp.float32
BF16 = jnp.bfloat16
MESH = pl.DeviceIdType.MESH

LANES = 128
VMEM_LIMIT_BYTES = 56 * 1024 * 1024

EPS = 1e-6
D_MODEL = 1024
HEAD_DIM = 64
SB_HEADS = 8
SWA_HEADS = 8
SWA_KV_HEADS = 2
WINDOW = 128
GDN_K_HEADS = 8
GDN_V_HEADS = 16
GDN_HEAD_DIM = 128
GDN_CONV = 4
GDN_CHUNK = 64
SB_W = SB_HEADS * HEAD_DIM
SWA_QW = SWA_HEADS * HEAD_DIM
SWA_KVW = SWA_KV_HEADS * HEAD_DIM
GDN_KW = GDN_K_HEADS * GDN_HEAD_DIM
GDN_VW = GDN_V_HEADS * GDN_HEAD_DIM
GDN_CONV_W = 2 * GDN_KW + GDN_VW

ADAM_LR = 0.001
ADAM_B1 = 0.9
ADAM_B2 = 0.999
ADAM_EPS = 1e-08
ADAM_WD = 0.01
ADAM_STEP = 10

NEG = -1e30


def _params(*sem):
    return pltpu.CompilerParams(dimension_semantics=sem or None, vmem_limit_bytes=VMEM_LIMIT_BYTES)


def _tile(n, cap, align=LANES):
    if n <= cap:
        return n
    for t in range(cap - cap % align, 0, -align):
        if n % t == 0:
            return t
    return n


N_CHIPS = 4
MM_VMEM_BUDGET_BYTES = 40 * 1024 * 1024
Q = "q"


def _opnd(x):
    return x if isinstance(x, tuple) else (x, ())


def _mm(a, b, *, name, ta=False, tb=False, out_dtype=F32, res=None, scale=1.0, out_q=False, into=None, norm_bwd=None,
        tm=None, tn=1024, tk=1024):
    (a_arr, a_lead), (b_arr, b_lead) = _opnd(a), _opnd(b)
    (k_a, m) = a_arr.shape[-2:] if ta else a_arr.shape[-2:][::-1]
    (n, k_b) = b_arr.shape[-2:] if tb else b_arr.shape[-2:][::-1]
    if into is not None:
        out_arr, out_lead = into
        out_q, out_dtype = Q in out_lead, out_arr.dtype
    else:
        out_lead = (Q,) if out_q else ()
    red_q = (Q in a_lead or Q in b_lead) and not out_q
    kq = min(k_a, k_b)
    assert (k_a == k_b) or (red_q and max(k_a, k_b) == N_CHIPS * kq), (a_arr.shape, b_arr.shape)
    tn, tk = _tile(n, tn), _tile(kq, tk)
    if tm is None:
        r_item = _opnd(res)[0].dtype.itemsize if res is not None else 0
        per_row = 2 * (tk * a_arr.dtype.itemsize + tn * (jnp.dtype(out_dtype).itemsize + r_item)) + 4 * tn
        if norm_bwd is not None:
            per_row += (2 * 2 + 4) * 4 * tn
        room = MM_VMEM_BUDGET_BYTES - 2 * tk * tn * b_arr.dtype.itemsize
        tm = next(c for c in (4096, 2048, 1024, 512, 256, 128) if c * per_row <= room or c == 128)
    tm = _tile(m, tm)
    nk = kq // tk
    ksteps = nk * (N_CHIPS if red_q else 1)
    dims = (((0 if ta else 1,), (1 if tb else 0,)), ((), ()))
    has_res = res is not None
    n_out = 2 if norm_bwd is not None else 1

    def body(*refs):
        a_ref, b_ref = refs[0], refs[1]
        o_ref, acc_ref = refs[-1 - n_out], refs[-1]
        k = pl.program_id(3)
        first_rows = pl.program_id(1) == 0

        @pl.when(k == 0)
        def _():
            acc_ref[...] = jnp.zeros_like(acc_ref)

        acc_ref[...] += lax.dot_general(a_ref[...].astype(BF16), b_ref[...].astype(BF16), dims,
                                        preferred_element_type=F32)

        @pl.when(k == ksteps - 1)
        def _():
            r = acc_ref[...]
            if scale != 1.0:
                r = r * scale
            if has_res:
                r = r + refs[2][...].astype(F32)
            if norm_bwd is not None:
                h_ref, gain_ref, dres_ref = refs[2 + has_res:5 + has_res]
                dgain_ref = refs[-2]
                _, vjp = jax.vjp(_f_rms_res, h_ref[...], gain_ref[...])
                r, dgain = vjp((r, dres_ref[...]))

                @pl.when(first_rows)
                def _():
                    dgain_ref[...] = jnp.zeros_like(dgain_ref)

                dgain_ref[...] += dgain
            o_ref[...] = r.astype(o_ref.dtype)

    def spec(lead, blk, pos):
        def index(s, i, j, k):
            kk = k % nk if (red_q and Q in lead) else k
            quarter = s if out_q else k // nk
            return tuple(quarter if l == Q else l for l in lead) + pos(i, j, kk)
        return pl.BlockSpec((None,) * len(lead) + blk, index)

    a_spec = spec(a_lead, (tk, tm), lambda i, j, k: (k, i)) if ta else spec(a_lead, (tm, tk), lambda i, j, k: (i, k))
    b_spec = spec(b_lead, (tn, tk), lambda i, j, k: (j, k)) if tb else spec(b_lead, (tk, tn), lambda i, j, k: (k, j))
    o_spec = spec(out_lead, (tm, tn), lambda i, j, k: (i, j))
    in_specs, args = [a_spec, b_spec], [a_arr, b_arr]
    if has_res:
        r_arr, r_lead = _opnd(res)
        in_specs.append(spec(r_lead, (tm, tn), lambda i, j, k: (i, j)))
        args.append(r_arr)
    out_specs, out_shapes = [o_spec], []
    if norm_bwd is not None:
        assert tn == n and not out_q and into is None, "the norm's backward needs whole rows"
        h_arr, gain_arr, dres_arr = norm_bwd
        row_spec = spec((), (tm, tn), lambda i, j, k: (i, j))
        gain_spec = pl.BlockSpec((1, tn), lambda s, i, j, k: (0, 0))
        in_specs += [row_spec, gain_spec, row_spec]
        args += [h_arr, gain_arr, dres_arr]
        out_specs.append(gain_spec)
    aliases = {}
    if into is not None:
        in_specs.append(pl.BlockSpec(memory_space=pl.ANY))
        args.append(out_arr)
        aliases = {len(args) - 1: 0}
        out_shapes.append(jax.ShapeDtypeStruct(out_arr.shape, out_arr.dtype))
    else:
        out_shapes.append(jax.ShapeDtypeStruct(((N_CHIPS,) if out_q else ()) + (m, n), out_dtype))
    if norm_bwd is not None:
        out_shapes.append(jax.ShapeDtypeStruct((1, n), F32))
    out = pl.pallas_call(
        body, name=name, grid=(N_CHIPS if out_q else 1, m // tm, n // tn, ksteps), in_specs=in_specs,
        out_specs=out_specs, out_shape=out_shapes, scratch_shapes=[pltpu.VMEM((tm, tn), F32)],
        input_output_aliases=aliases,
        compiler_params=_params("parallel", *(("arbitrary",) * 3 if norm_bwd is not None else ("parallel", "parallel", "arbitrary"))),
    )(*args)
    return out if norm_bwd is not None else out[0]


def _row_spec(r, tm):
    if isinstance(r, tuple):
        arr, width, cb = r
        return arr, pl.BlockSpec((tm, width), lambda i, cb=cb: (i, cb))
    return r, pl.BlockSpec((tm, r.shape[1]), lambda i: (i, 0))


def _const_spec(c):
    return pl.BlockSpec(c.shape, lambda i: (0,) * c.ndim)


def _row_fwd(fn, rows, consts, outs, *, name, tm=256):
    tm = _tile(_row_spec(rows[0], tm)[0].shape[0], tm, 8)
    arrs, specs = zip(*[_row_spec(r, tm) for r in rows])
    t = arrs[0].shape[0]
    nr, nc = len(rows), len(consts)

    def body(*refs):
        vals = [r[...].astype(F32) for r in refs[:nr + nc]]
        res = fn(*vals)
        for o_ref, v in zip(refs[nr + nc:], res):
            o_ref[...] = v.astype(o_ref.dtype)

    out = pl.pallas_call(
        body, name=name, grid=(t // tm,),
        in_specs=list(specs) + [_const_spec(c) for c in consts],
        out_specs=[pl.BlockSpec((tm, w), lambda i: (i, 0)) for w, _ in outs],
        out_shape=[jax.ShapeDtypeStruct((t, w), dt) for w, dt in outs],
        compiler_params=_params("parallel"),
    )(*arrs, *consts)
    return list(out)


def _row_bwd(fn, rows, consts, cts, row_grads, const_grads, *, name, tm=256):
    tm = _tile(_row_spec(rows[0], tm)[0].shape[0], tm, 8)
    arrs, specs = zip(*[_row_spec(r, tm) for r in rows])
    ct_arrs, ct_specs = zip(*[_row_spec(r, tm) for r in cts])
    t = arrs[0].shape[0]
    nr, nc, nt = len(rows), len(consts), len(cts)
    n_in = nr + nc + nt

    def body(*refs):
        vals = [r[...].astype(F32) for r in refs[:nr + nc]]
        ctv = tuple(r[...].astype(F32) for r in refs[nr + nc:n_in])
        _, vjp = jax.vjp(fn, *vals)
        g = vjp(ctv)
        outs = refs[n_in:]
        for (idx, _), o_ref in zip(row_grads, outs[:len(row_grads)]):
            o_ref[...] = g[idx].astype(o_ref.dtype)
        first = pl.program_id(0) == 0
        for ci, o_ref in zip(const_grads, outs[len(row_grads):]):
            @pl.when(first)
            def _(o_ref=o_ref):
                o_ref[...] = jnp.zeros_like(o_ref)

            o_ref[...] += g[nr + ci]

    widths = [(_row_spec(rows[idx], tm)[1].block_shape[1], dt) for idx, dt in row_grads]
    out = pl.pallas_call(
        body, name=name, grid=(t // tm,),
        in_specs=list(specs) + [_const_spec(c) for c in consts] + list(ct_specs),
        out_specs=[pl.BlockSpec((tm, w), lambda i: (i, 0)) for w, _ in widths]
        + [_const_spec(consts[ci]) for ci in const_grads],
        out_shape=[jax.ShapeDtypeStruct((t, w), dt) for w, dt in widths]
        + [jax.ShapeDtypeStruct(consts[ci].shape, F32) for ci in const_grads],
        compiler_params=_params("arbitrary"),
    )(*arrs, *consts, *ct_arrs)
    return list(out)


def _rms(x, g):
    return x * lax.rsqrt(jnp.mean(x * x, axis=-1, keepdims=True) + EPS) * g


def _f_rms(h, g):
    return (_rms(h, g),)


def _f_rms_res(h, g):
    return (_rms(h, g), h)


def _f_swiglu(g, u):
    return (g * jax.nn.sigmoid(g) * u,)


def _f_ple(h, gl, pp):
    return (h + jax.nn.sigmoid(gl) * pp,)


def _f_gdn_out(o, z, gain):
    outs = []
    for hd in range(GDN_V_HEADS):
        sl = slice(hd * GDN_HEAD_DIM, (hd + 1) * GDN_HEAD_DIM)
        oh, zh = o[:, sl], z[:, sl]
        outs.append(_rms(oh, gain) * (zh * jax.nn.sigmoid(zh)))
    return (jnp.concatenate(outs, axis=1),)


def _loss_head(y, target, *, name, tm=512):
    t, d = y.shape
    tm = _tile(t, tm, 8)

    def body(y_ref, t_ref, dy_ref, l_ref):
        @pl.when(pl.program_id(0) == 0)
        def _():
            l_ref[...] = jnp.zeros_like(l_ref)

        e = y_ref[...] - t_ref[...]
        dy_ref[...] = e * (1.0 / d)
        l_ref[...] += jnp.sum(e * e) * (0.5 / d)

    dy, l = pl.pallas_call(
        body, name=name, grid=(t // tm,),
        in_specs=[pl.BlockSpec((tm, d), lambda i: (i, 0))] * 2,
        out_specs=[pl.BlockSpec((tm, d), lambda i: (i, 0)), pl.BlockSpec((8, LANES), lambda i: (0, 0))],
        out_shape=[jax.ShapeDtypeStruct((t, d), F32), jax.ShapeDtypeStruct((8, LANES), F32)],
        compiler_params=_params("arbitrary"),
    )(y, target)
    return dy, l[0, 0]


def _dg(a, b, ca, cb):
    nb = a.ndim - 2
    batch = tuple(range(nb))
    return lax.dot_general(a, b, (((ca + nb,), (cb + nb,)), (batch, batch)), preferred_element_type=F32)


def _b(x):
    return x.astype(BF16)


@jax.custom_vjp
def _bdot(a, b):
    return _dg(_b(a), _b(b), 1, 0)


def _bdot_fwd(a, b):
    return _bdot(a, b), (a, b)


def _bdot_bwd(r, ct):
    a, b = r
    return _dg(_b(ct), _b(b), 1, 1), _dg(_b(a), _b(ct), 0, 0)


_bdot.defvjp(_bdot_fwd, _bdot_bwd)


@jax.custom_vjp
def _bdot_nt(a, b):
    return _dg(_b(a), _b(b), 1, 1)


def _bdot_nt_fwd(a, b):
    return _bdot_nt(a, b), (a, b)


def _bdot_nt_bwd(r, ct):
    a, b = r
    return _dg(_b(ct), _b(b), 1, 0), _dg(_b(ct), _b(a), 0, 0)


_bdot_nt.defvjp(_bdot_nt_fwd, _bdot_nt_bwd)


@jax.custom_vjp
def _bdot_tn(a, b):
    return _dg(_b(a), _b(b), 0, 0)


def _bdot_tn_fwd(a, b):
    return _bdot_tn(a, b), (a, b)


def _bdot_tn_bwd(r, ct):
    a, b = r
    return _dg(_b(b), _b(ct), 1, 1), _dg(_b(a), _b(ct), 1, 0)


_bdot_tn.defvjp(_bdot_tn_fwd, _bdot_tn_bwd)


def _two(x):
    hi = x.astype(BF16)
    return hi, (x - hi.astype(F32)).astype(BF16)


def _dg3(a, b, ca, cb):
    (ah, al), (bh, bl) = _two(a), _two(b)
    return _dg(ah, bh, ca, cb) + (_dg(ah, bl, ca, cb) + _dg(al, bh, ca, cb))


@jax.custom_vjp
def _hdot(a, b):
    return _dg3(a, b, 1, 0)


def _hdot_fwd(a, b):
    return _hdot(a, b), (a, b)


def _hdot_bwd(r, ct):
    a, b = r
    return _dg3(ct, b, 1, 1), _dg3(a, ct, 0, 0)


_hdot.defvjp(_hdot_fwd, _hdot_bwd)


@jax.custom_vjp
def _unit_lower_inverse(x):
    c = x.shape[-1]
    eye = (lax.broadcasted_iota(jnp.int32, x.shape, 1) == lax.broadcasted_iota(jnp.int32, x.shape, 2)).astype(F32)
    inv, pw = eye + x, x
    for _ in range(int(math.log2(c)) - 1):
        pw = _dg3(pw, pw, 1, 0)
        inv = inv + _dg3(inv, pw, 1, 0)
    return inv


def _unit_lower_inverse_fwd(x):
    inv = _unit_lower_inverse(x)
    return inv, inv


def _unit_lower_inverse_bwd(inv, ct):
    return (_dg3(_dg3(inv, ct, 0, 0), inv, 1, 1),)


_unit_lower_inverse.defvjp(_unit_lower_inverse_fwd, _unit_lower_inverse_bwd)


@jax.custom_vjp
def _known_inverse(x, inv):
    return inv


def _known_inverse_fwd(x, inv):
    return inv, inv


def _known_inverse_bwd(inv, ct):
    return _dg3(_dg3(inv, ct, 0, 0), inv, 1, 1), jnp.zeros_like(inv)


_known_inverse.defvjp(_known_inverse_fwd, _known_inverse_bwd)


def _split_dot(x, u):
    hi, lo = _two(x)
    return _dg(hi, u, 1, 0) + _dg(lo, u, 1, 0)


@jax.custom_vjp
def _ldot(l01, x):
    hi, lo = _two(x)
    l01 = l01.astype(BF16)
    return _dg(l01, hi, 1, 0) + _dg(l01, lo, 1, 0)


def _ldot_fwd(l01, x):
    return _ldot(l01, x), l01


def _ldot_bwd(l01, ct):
    hi, lo = _two(ct)
    l01b = l01.astype(BF16)
    return jnp.zeros_like(l01), _dg(l01b, hi, 0, 0) + _dg(l01b, lo, 0, 0)


_ldot.defvjp(_ldot_fwd, _ldot_bwd)


SB_BLK = 128
SB_KEYS = 512
SB_PAIRS = 2
SB_SCALE = HEAD_DIM ** -0.5


def _log_sigmoid(z):
    return jnp.minimum(z, 0.0) - jnp.log(1.0 + jnp.exp(-jnp.abs(z)))


def _sb_consts(t):
    kb = min(SB_KEYS, t)
    nh = 2 * SB_PAIRS
    lane = lax.broadcasted_iota(jnp.int32, (nh, SB_BLK, kb), 2)
    row = lax.broadcasted_iota(jnp.int32, (nh, SB_BLK, kb), 1)
    ur = lax.broadcasted_iota(jnp.int32, (kb, kb), 0)
    uc = lax.broadcasted_iota(jnp.int32, (kb, kb), 1)
    return kb, nh, lane, row, ur, uc


def _sb_heads(x):
    head0 = lax.broadcasted_iota(jnp.int32, (x.shape[0], LANES), 1) < HEAD_DIM
    out = []
    for p in range(SB_PAIRS):
        blk = x[:, p * LANES:(p + 1) * LANES]
        out += [jnp.where(head0, blk, 0.0), jnp.where(head0, 0.0, blk)]
    return jnp.stack(out)


def _sb_pairs(x):
    return jnp.stack([x[:, (h // 2) * LANES:(h // 2 + 1) * LANES] for h in range(2 * SB_PAIRS)])


def _sb_merge(x):
    head0 = lax.broadcasted_iota(jnp.int32, (x.shape[1], LANES), 1) < HEAD_DIM
    return jnp.concatenate([jnp.where(head0, x[2 * p], x[2 * p + 1]) for p in range(SB_PAIRS)], axis=1)


def _sb_rows_dot(x, u):
    nh, rows, k = x.shape
    return _split_dot(x.reshape(nh * rows, k), u).reshape(nh, rows, k)


def _sb_fwd(proj, *, name, gather=()):
    t = proj.shape[0]
    nb = t // SB_BLK
    width = SB_PAIRS * LANES
    ng = SB_W // width
    na = len(gather)

    def body(q_ref, k_ref, v_ref, *rest):
        o_ref, r_ref = rest[na:na + 2]
        i = pl.program_id(1)
        if na:
            step = pl.program_id(0) * nb + i
            copies = lambda **kw: _gather_copies(rest[:na], rest[na + 2:2 * na + 2], *rest[2 * na + 2:], **kw)
            pl.when(step == 0)(lambda: _gather_start(copies(only_first=True)))
        kb, nh, lane, row, ur, uc = _sb_consts(t)
        u_suffix = (ur >= uc).astype(BF16)
        qh = _b(_sb_heads(q_ref[...]) * SB_SCALE)
        diag = (i * SB_BLK) // kb

        def block(j, carry, masked):
            acc, car = carry
            keys = pl.ds(pl.multiple_of(j * kb, kb), kb)
            kj, vj = _b(_sb_pairs(k_ref[keys, :])), _b(_sb_pairs(v_ref[keys, :]))
            z = _dg(qh, kj, 1, 1)
            lk = _log_sigmoid(-z)
            if masked:
                causal = (j * kb + lane) < (i * SB_BLK + row)
                lk = jnp.where(causal, lk, 0.0)
            suf = _sb_rows_dot(lk, u_suffix) + car
            w = jnp.exp(z + suf)
            if masked:
                w = jnp.where(causal, w, 0.0)
            return acc + _dg(_b(w), vj, 1, 0), suf[:, :, 0:1]

        zero = (jnp.zeros((nh, SB_BLK, LANES), F32), jnp.zeros((nh, SB_BLK, 1), F32))
        carry = block(diag, zero, True)
        acc, car = lax.fori_loop(0, diag, lambda s, c: block(diag - 1 - s, c, False), carry)
        o_ref[...] = _sb_merge(acc)
        r_ref[...] = _sb_merge(jnp.broadcast_to(car, (nh, SB_BLK, LANES)))
        if na:
            pl.when(step == ng * nb - 1)(lambda: _gather_finish(copies()))

    return pl.pallas_call(
        body, name=name, grid=(ng, nb),
        in_specs=[pl.BlockSpec((SB_BLK, width), lambda p, i: (i, p)),
                  pl.BlockSpec((t, width), lambda p, i: (0, ng + p)),
                  pl.BlockSpec((t, width), lambda p, i: (0, 2 * ng + p))] + [ANY] * na,
        out_specs=[pl.BlockSpec((SB_BLK, width), lambda p, i: (i, p))] * 2 + [ANY] * na,
        out_shape=[jax.ShapeDtypeStruct((t, SB_W), F32)] * 2
        + [jax.ShapeDtypeStruct((N_CHIPS,) + g.shape, g.dtype) for g in gather],
        scratch_shapes=_gather_scratch(na) if na else [],
        compiler_params=_params("arbitrary", "arbitrary"),
    )(proj, proj, proj, *gather)


def _sb_bwd(proj, rtot, dout, *, name, scatter=()):
    t = proj.shape[0]
    nb = t // SB_BLK
    width = SB_PAIRS * LANES
    ng = SB_W // width
    na = len(scatter)

    def body(q_ref, k_ref, v_ref, r_ref, do_ref, *rest):
        dq_ref, dk_ref, dv_ref = rest[na:na + 3]
        i = pl.program_id(1)
        if na:
            step = pl.program_id(0) * nb + i
            copies = lambda: _scatter_copies(rest[:na], rest[na + 3:4 * na + 3], *rest[4 * na + 3:])
            pl.when(step == 0)(lambda: [cp.start() for cp in copies()] and None)
        kb, nh, lane, row, ur, uc = _sb_consts(t)
        u_incl = (ur <= uc).astype(BF16)
        u_excl = (ur < uc).astype(BF16)
        q, do = q_ref[...], do_ref[...]
        qh, doh = _b(_sb_heads(q) * SB_SCALE), _b(_sb_heads(do))
        qb, dob = _b(_sb_pairs(q) * SB_SCALE), _b(_sb_pairs(do))
        rh = jnp.min(_sb_heads(r_ref[...]), axis=2, keepdims=True)
        diag = (i * SB_BLK) // kb

        @pl.when(i == 0)
        def _():
            dk_ref[...] = jnp.zeros_like(dk_ref)
            dv_ref[...] = jnp.zeros_like(dv_ref)

        def block(j, carry, masked):
            dq_acc, clk, ce = carry
            keys = pl.ds(pl.multiple_of(j * kb, kb), kb)
            kj, vj = _b(_sb_pairs(k_ref[keys, :])), _b(_sb_pairs(v_ref[keys, :]))
            z = _dg(qh, kj, 1, 1)
            lk = _log_sigmoid(-z)
            ls = z + lk
            if masked:
                causal = (j * kb + lane) < (i * SB_BLK + row)
                lk = jnp.where(causal, lk, 0.0)
            pre = _sb_rows_dot(lk, u_incl) + clk
            w = jnp.exp(ls + (rh - pre))
            if masked:
                w = jnp.where(causal, w, 0.0)
            e = _dg(doh, vj, 1, 1) * w
            pre_e = _sb_rows_dot(e, u_excl) + ce
            sig = jnp.exp(ls)
            dz = e - sig * (e + pre_e)
            if masked:
                dz = jnp.where(causal, dz, 0.0)
            dzb = _b(dz)
            dk_ref[keys, :] += _sb_merge(_dg(dzb, qb, 0, 0))
            dv_ref[keys, :] += _sb_merge(_dg(_b(w), dob, 0, 0))
            return dq_acc + _dg(dzb, kj, 1, 0), pre[:, :, kb - 1:], pre_e[:, :, kb - 1:] + e[:, :, kb - 1:]

        zero = (jnp.zeros((nh, SB_BLK, LANES), F32), jnp.zeros((nh, SB_BLK, 1), F32), jnp.zeros((nh, SB_BLK, 1), F32))
        carry = lax.fori_loop(0, diag, lambda j, c: block(j, c, False), zero)
        dq_acc, _, _ = block(diag, carry, True)
        dq_ref[...] = _sb_merge(dq_acc) * SB_SCALE
        if na:
            pl.when(step == ng * nb - 1)(lambda: [cp.wait() for cp in copies()] and None)

    blk = pl.BlockSpec((SB_BLK, width), lambda p, i: (i, p))
    whole = pl.BlockSpec((t, width), lambda p, i: (0, p))
    return pl.pallas_call(
        body, name=name, grid=(ng, nb),
        in_specs=[blk,
                  pl.BlockSpec((t, width), lambda p, i: (0, ng + p)),
                  pl.BlockSpec((t, width), lambda p, i: (0, 2 * ng + p)),
                  blk, blk] + [ANY] * na,
        out_specs=[blk, whole, whole] + [ANY] * (3 * na),
        out_shape=[jax.ShapeDtypeStruct((t, SB_W), F32)] * 3 + _scatter_shapes(scatter),
        scratch_shapes=_dma_sems(3 * na) if na else [],
        compiler_params=_params("arbitrary", "arbitrary"),
    )(proj, proj, proj, rtot, dout, *scatter)


SWA_G = SWA_HEADS // SWA_KV_HEADS


def _swa_heads(first, qs, ks, vs, qg, kg, sinks):
    shape = (SWA_HEADS, WINDOW, 2 * WINDOW)
    qi = lax.broadcasted_iota(jnp.int32, shape, 1)
    kj = lax.broadcasted_iota(jnp.int32, shape, 2)
    dist = qi + WINDOW - kj
    valid = (dist >= 0) & (dist < WINDOW) & (jnp.logical_not(first) | (kj >= WINDOW))
    head = lax.broadcasted_iota(jnp.int32, (SWA_HEADS, 1, 1), 0)
    slope = sum(jnp.where(head == h, 2.0 ** (-8.0 * (h + 1) / SWA_HEADS), 0.0) for h in range(SWA_HEADS))
    kn = _rms(ks, kg)
    per_q_head = lambda x: jnp.concatenate([x[h // SWA_G:h // SWA_G + 1] for h in range(SWA_HEADS)], axis=0)
    k8, v8 = per_q_head(kn), per_q_head(vs)
    s = _bdot_nt(_rms(qs, qg), k8) * (HEAD_DIM ** -0.5)
    s = jnp.where(valid, s - slope * dist.astype(F32), NEG)
    m = lax.stop_gradient(jnp.maximum(jnp.max(s, axis=2, keepdims=True), sinks))
    p = jnp.exp(s - m)
    den = jnp.sum(p, axis=2, keepdims=True) + jnp.exp(sinks - m)
    return _bdot(p / den, v8)


def _swa_split(q, kp, kc, vp, vc, sk):
    lanes = lambda x, n: jnp.stack([x[:, h * HEAD_DIM:(h + 1) * HEAD_DIM].astype(F32) for h in range(n)])
    k2, v2 = jnp.concatenate([kp, kc], axis=0), jnp.concatenate([vp, vc], axis=0)
    sinks = jnp.stack([sk[:, h:h + 1] for h in range(SWA_HEADS)])
    return lanes(q, SWA_HEADS), lanes(k2, SWA_KV_HEADS), lanes(v2, SWA_KV_HEADS), sinks


def _swa_join(x):
    return jnp.concatenate([x[h] for h in range(x.shape[0])], axis=1)


def _swa_specs(t):
    qcb = (3 * SB_W) // SWA_QW
    kcb = (3 * SB_W + SWA_QW) // SWA_KVW
    prev = lambda i: jnp.maximum(i - 1, 0)
    return [pl.BlockSpec((WINDOW, SWA_QW), lambda i: (i, qcb)),
            pl.BlockSpec((WINDOW, SWA_KVW), lambda i: (prev(i), kcb)),
            pl.BlockSpec((WINDOW, SWA_KVW), lambda i: (i, kcb)),
            pl.BlockSpec((WINDOW, SWA_KVW), lambda i: (prev(i), kcb + 1)),
            pl.BlockSpec((WINDOW, SWA_KVW), lambda i: (i, kcb + 1)),
            pl.BlockSpec((1, HEAD_DIM), lambda i: (0, 0)),
            pl.BlockSpec((1, HEAD_DIM), lambda i: (0, 0)),
            pl.BlockSpec((1, SWA_HEADS), lambda i: (0, 0))]


def _swa_fwd(proj, qg, kg, sinks, *, name):
    t = proj.shape[0]

    def body(q_ref, kp_ref, kc_ref, vp_ref, vc_ref, qg_ref, kg_ref, sk_ref, o_ref):
        first = pl.program_id(0) == 0
        qs, ks, vs, sk = _swa_split(q_ref[...], kp_ref[...], kc_ref[...], vp_ref[...], vc_ref[...], sk_ref[...])
        o_ref[...] = _swa_join(_swa_heads(first, qs, ks, vs, qg_ref[...], kg_ref[...], sk))

    return pl.pallas_call(
        body, name=name, grid=(t // WINDOW,), in_specs=_swa_specs(t),
        out_specs=pl.BlockSpec((WINDOW, SWA_QW), lambda i: (i, 0)),
        out_shape=jax.ShapeDtypeStruct((t, SWA_QW), F32),
        compiler_params=_params("parallel"),
    )(proj, proj, proj, proj, proj, qg, kg, sinks)


def _swa_bwd(proj, qg, kg, sinks, dout, *, name):
    t = proj.shape[0]

    def body(q_ref, kp_ref, kc_ref, vp_ref, vc_ref, qg_ref, kg_ref, sk_ref, do_ref,
             dq_ref, dk_ref, dv_ref, dqg_ref, dkg_ref, dsk_ref):
        i = pl.program_id(0)
        first = i == 0

        @pl.when(first)
        def _():
            for r in (dk_ref, dv_ref, dqg_ref, dkg_ref, dsk_ref):
                r[...] = jnp.zeros_like(r)

        qs, ks, vs, sk = _swa_split(q_ref[...], kp_ref[...], kc_ref[...], vp_ref[...], vc_ref[...], sk_ref[...])
        do = do_ref[...]
        cts = jnp.stack([do[:, h * HEAD_DIM:(h + 1) * HEAD_DIM] for h in range(SWA_HEADS)])
        _, vjp = jax.vjp(functools.partial(_swa_heads, first), qs, ks, vs, qg_ref[...], kg_ref[...], sk)
        dqs, dks, dvs, dqg, dkg, dsk = vjp(cts)
        dq_ref[...] = _swa_join(dqs)
        dk2, dv2 = _swa_join(dks), _swa_join(dvs)
        cur = pl.ds(pl.multiple_of(i * WINDOW, WINDOW), WINDOW)
        prv = pl.ds(pl.multiple_of(jnp.maximum(i - 1, 0) * WINDOW, WINDOW), WINDOW)
        dk_ref[prv, :] += dk2[:WINDOW]
        dv_ref[prv, :] += dv2[:WINDOW]
        dk_ref[cur, :] += dk2[WINDOW:]
        dv_ref[cur, :] += dv2[WINDOW:]
        dqg_ref[...] += dqg
        dkg_ref[...] += dkg
        dsk_ref[...] += _swa_join(dsk)

    whole = lambda shape: pl.BlockSpec(shape, lambda i: (0, 0))
    return pl.pallas_call(
        body, name=name, grid=(t // WINDOW,),
        in_specs=_swa_specs(t) + [pl.BlockSpec((WINDOW, SWA_QW), lambda i: (i, 0))],
        out_specs=[pl.BlockSpec((WINDOW, SWA_QW), lambda i: (i, 0)), whole((t, SWA_KVW)), whole((t, SWA_KVW)),
                   whole((1, HEAD_DIM)), whole((1, HEAD_DIM)), whole((1, SWA_HEADS))],
        out_shape=[jax.ShapeDtypeStruct((t, SWA_QW), F32), jax.ShapeDtypeStruct((t, SWA_KVW), F32),
                   jax.ShapeDtypeStruct((t, SWA_KVW), F32), jax.ShapeDtypeStruct((1, HEAD_DIM), F32),
                   jax.ShapeDtypeStruct((1, HEAD_DIM), F32), jax.ShapeDtypeStruct((1, SWA_HEADS), F32)],
        compiler_params=_params("arbitrary"),
    )(proj, proj, proj, proj, proj, qg, kg, sinks, dout)


CONV_CB = 512
CONV_TM = 512
HALO = 8


def _conv_pre(x_ref, h_ref, w_ref, i):
    halo = jnp.where(i > 0, h_ref[...], 0.0)
    xe = jnp.concatenate([halo, x_ref[...]], axis=0)
    tm = x_ref.shape[0]
    w = w_ref[...]
    c = sum(w[k:k + 1, :] * xe[HALO - (GDN_CONV - 1) + k:HALO - (GDN_CONV - 1) + k + tm] for k in range(GDN_CONV))
    return c, xe


def _conv_specs(tm, cb):
    return [pl.BlockSpec((tm, cb), lambda c, i: (i, c)),
            pl.BlockSpec((HALO, cb), lambda c, i: (jnp.maximum(i * (tm // HALO) - 1, 0), c)),
            pl.BlockSpec((GDN_CONV, cb), lambda c, i: (0, c))]


def _conv_fwd(x, w, *, name):
    t, ch = x.shape
    tm, cb = _tile(t, CONV_TM), _tile(ch, CONV_CB)

    def body(x_ref, h_ref, w_ref, o_ref):
        c, _ = _conv_pre(x_ref, h_ref, w_ref, pl.program_id(1))
        o_ref[...] = c * jax.nn.sigmoid(c)

    tile = pl.BlockSpec((tm, cb), lambda c, i: (i, c))
    return pl.pallas_call(
        body, name=name, grid=(ch // cb, t // tm), in_specs=_conv_specs(tm, cb), out_specs=tile,
        out_shape=jax.ShapeDtypeStruct((t, ch), F32),
        compiler_params=_params("parallel", "parallel"),
    )(x, x, w)


def _conv_bwd(x, w, dact, *, name):
    t, ch = x.shape
    tm, cb = _tile(t, CONV_TM), _tile(ch, CONV_CB)
    nt = t // tm
    last = GDN_CONV - 1

    def body(x_ref, h_ref, w_ref, xn_ref, da_ref, dan_ref, dx_ref, dw_ref):
        i = pl.program_id(1)

        @pl.when(i == 0)
        def _():
            dw_ref[...] = jnp.zeros_like(dw_ref)

        w = w_ref[...]
        xe = jnp.concatenate([jnp.where(i > 0, h_ref[...], 0.0), x_ref[...], xn_ref[...]], axis=0)
        rows = tm + HALO
        c = sum(w[k:k + 1, :] * xe[HALO - last + k:HALO - last + k + rows] for k in range(GDN_CONV))
        sig = jax.nn.sigmoid(c)
        da = jnp.concatenate([da_ref[...], jnp.where(i < nt - 1, dan_ref[...], 0.0)], axis=0)
        dce = da * (sig * (1.0 + c * (1.0 - sig)))
        dc = dce[:tm]
        dx_ref[...] = sum(w[k:k + 1, :] * dce[last - k:last - k + tm] for k in range(GDN_CONV))
        dw_ref[...] += jnp.concatenate(
            [jnp.sum(dc * xe[HALO - last + k:HALO - last + k + tm], axis=0, keepdims=True) for k in range(GDN_CONV)],
            axis=0)

    tile = pl.BlockSpec((tm, cb), lambda c, i: (i, c))
    nxt = pl.BlockSpec((HALO, cb), lambda c, i: (jnp.minimum((i + 1) * (tm // HALO), t // HALO - 1), c))
    return pl.pallas_call(
        body, name=name, grid=(ch // cb, nt),
        in_specs=_conv_specs(tm, cb) + [nxt, tile, nxt],
        out_specs=[tile, pl.BlockSpec((GDN_CONV, cb), lambda c, i: (0, c))],
        out_shape=[jax.ShapeDtypeStruct((t, ch), F32), jax.ShapeDtypeStruct((GDN_CONV, ch), F32)],
        compiler_params=_params("parallel", "arbitrary"),
    )(x, x, w, x, dact, dact)


def _gdn_chunk(qraw, kraw, v, bl, a, alog, dtb, state, inverse=None, keep_inverse=False):
    c, d = GDN_CHUNK, GDN_HEAD_DIM
    nh = qraw.shape[0]
    ri = lax.broadcasted_iota(jnp.int32, (nh, c, c), 1)
    ci = lax.broadcasted_iota(jnp.int32, (nh, c, c), 2)
    incl, strict = ri >= ci, ri > ci
    q = qraw * lax.rsqrt(jnp.sum(qraw * qraw, axis=-1, keepdims=True) + EPS) * (d ** -0.5)
    k = kraw * lax.rsqrt(jnp.sum(kraw * kraw, axis=-1, keepdims=True) + EPS)
    beta = jax.nn.sigmoid(bl)
    g = -jnp.exp(alog) * jax.nn.softplus(a + dtb)
    gc = _ldot(incl.astype(F32), jnp.broadcast_to(g, (nh, c, d)))
    gcm = gc[:, :, :c]
    decay = jnp.exp(jnp.where(incl, gcm - jnp.swapaxes(gcm, 1, 2), NEG))
    eg = jnp.exp(gc)
    kbeta = k * beta
    x = -jnp.where(strict, _bdot_nt(kbeta, k) * decay, 0.0)
    tinv = _unit_lower_inverse(x) if inverse is None else _known_inverse(x, inverse)
    u = _hdot(tinv, v * beta)
    w = _hdot(tinv, kbeta * eg)
    attn = jnp.where(incl, _bdot_nt(q, k) * decay, 0.0)
    glast = gc[:, c - 1:c, :]
    v_new = u - _bdot(w, state)
    o = _bdot(q * eg, state) + _bdot(attn, v_new)
    state = state * jnp.exp(glast) + _bdot_tn(k * jnp.exp(glast - gc), v_new)
    return (o, state, tinv) if keep_inverse else (o, state)


GDN_REP = GDN_V_HEADS // GDN_K_HEADS
GDN_HB = 8


def _gdn_pick(vals, kh, r):
    ba, alog, dtb = vals
    lane = lax.broadcasted_iota(jnp.int32, ba.shape, 1)
    hv = kh * GDN_REP + r
    bl = jnp.sum(jnp.where(lane == hv, ba, 0.0), axis=1, keepdims=True)
    a = jnp.sum(jnp.where(lane == GDN_V_HEADS + hv, ba, 0.0), axis=1, keepdims=True)
    lane1 = lax.broadcasted_iota(jnp.int32, alog.shape, 1)
    al = jnp.sum(jnp.where(lane1 == hv, alog, 0.0), axis=1, keepdims=True)
    db = jnp.sum(jnp.where(lane1 == hv, dtb, 0.0), axis=1, keepdims=True)
    return bl, a, al, db


def _gdn_stack(qs, ks, vs, small, j):
    d = GDN_HEAD_DIM
    per = [[], [], [], [], [], [], []]
    for hh in range(GDN_HB):
        q, k = qs[:, hh * d:(hh + 1) * d], ks[:, hh * d:(hh + 1) * d]
        for r in range(GDN_REP):
            col = (hh * GDN_REP + r) * d
            for lst, val in zip(per, (q, k, vs[:, col:col + d]) + _gdn_pick(small, j * GDN_HB + hh, r)):
                lst.append(val)
    return tuple(jnp.stack(lst) for lst in per)


def _gdn_specs(nchunk, rev):
    c, d = GDN_CHUNK, GDN_HEAD_DIM
    at = (lambda n: nchunk - 1 - n) if rev else (lambda n: n)
    ng = GDN_K_HEADS // GDN_HB
    return at, [pl.BlockSpec((c, GDN_HB * d), lambda n, j: (at(n), j)),
                pl.BlockSpec((c, GDN_HB * d), lambda n, j: (at(n), ng + j)),
                pl.BlockSpec((c, GDN_HB * GDN_REP * d), lambda n, j: (at(n), ng + j)),
                pl.BlockSpec((c, 2 * GDN_V_HEADS), lambda n, j: (at(n), 0)),
                pl.BlockSpec((1, GDN_V_HEADS), lambda n, j: (0, 0)),
                pl.BlockSpec((1, GDN_V_HEADS), lambda n, j: (0, 0))]


def _gdn_fwd(act, ba, alog, dtb, *, name):
    t = act.shape[0]
    c, d = GDN_CHUNK, GDN_HEAD_DIM
    nchunk = t // c
    at, specs = _gdn_specs(nchunk, False)

    def body(q_ref, k_ref, v_ref, ba_ref, al_ref, db_ref, o_ref, s_ref, inv_ref, state):
        n, j = pl.program_id(0), pl.program_id(1)
        heads = pl.ds(j * GDN_HB, GDN_HB)

        @pl.when(n == 0)
        def _():
            state[heads] = jnp.zeros((GDN_HB, GDN_REP, d, d), F32)

        s_in = state[heads]
        s_ref[...] = s_in
        args = _gdn_stack(q_ref[...], k_ref[...], v_ref[...], (ba_ref[...], al_ref[...], db_ref[...]), j)
        o, s_new, inv_ref[...] = _gdn_chunk(*args, s_in.reshape(GDN_HB * GDN_REP, d, d), keep_inverse=True)
        o_ref[...] = jnp.concatenate([o[b] for b in range(GDN_HB * GDN_REP)], axis=1)
        state[heads] = s_new.reshape(GDN_HB, GDN_REP, d, d)

    return pl.pallas_call(
        body, name=name, grid=(nchunk, GDN_K_HEADS // GDN_HB), in_specs=specs,
        out_specs=[pl.BlockSpec((c, GDN_HB * GDN_REP * d), lambda n, j: (n, j)),
                   pl.BlockSpec((None, GDN_HB, GDN_REP, d, d), lambda n, j: (n, j, 0, 0, 0)),
                   pl.BlockSpec((None, GDN_HB * GDN_REP, c, c), lambda n, j: (n, j, 0, 0))],
        out_shape=[jax.ShapeDtypeStruct((t, GDN_VW), F32),
                   jax.ShapeDtypeStruct((nchunk, GDN_K_HEADS, GDN_REP, d, d), F32),
                   jax.ShapeDtypeStruct((nchunk, GDN_V_HEADS, c, c), F32)],
        scratch_shapes=[pltpu.VMEM((GDN_K_HEADS, GDN_REP, d, d), F32)],
        compiler_params=_params("arbitrary", "arbitrary"),
    )(act, act, act, ba, alog, dtb)


def _gdn_bwd(act, ba, alog, dtb, states, inverses, dout, *, name):
    assert GDN_HB == GDN_K_HEADS
    t = act.shape[0]
    c, d = GDN_CHUNK, GDN_HEAD_DIM
    nchunk = t // c
    at, specs = _gdn_specs(nchunk, True)

    def body(q_ref, k_ref, v_ref, ba_ref, al_ref, db_ref, s_ref, inv_ref, do_ref,
             dact_ref, dba_ref, dal_ref, ddb_ref, dstate):
        n, j = pl.program_id(0), pl.program_id(1)

        @pl.when(n == 0)
        def _():
            dstate[pl.ds(j * GDN_HB, GDN_HB)] = jnp.zeros((GDN_HB, GDN_REP, d, d), F32)

        @pl.when((n == 0) & (j == 0))
        def _():
            dal_ref[...] = jnp.zeros_like(dal_ref)
            ddb_ref[...] = jnp.zeros_like(ddb_ref)

        @pl.when(j == 0)
        def _():
            dba_ref[...] = jnp.zeros_like(dba_ref)

        heads = pl.ds(j * GDN_HB, GDN_HB)
        nh = GDN_HB * GDN_REP
        args = _gdn_stack(q_ref[...], k_ref[...], v_ref[...], (ba_ref[...], al_ref[...], db_ref[...]), j)
        _, vjp = jax.vjp(functools.partial(_gdn_chunk, inverse=inv_ref[...]), *args, s_ref[...].reshape(nh, d, d))
        do = do_ref[...]
        do = jnp.stack([do[:, b * d:(b + 1) * d] for b in range(nh)])
        gq, gk, gv, gbl, ga, gal, gdb, gs = vjp((do, dstate[heads].reshape(nh, d, d)))
        dstate[heads] = gs.reshape(GDN_HB, GDN_REP, d, d)
        dact_ref[...] = jnp.concatenate([gq[GDN_REP * hh] + gq[GDN_REP * hh + 1] for hh in range(GDN_HB)]
                                        + [gk[GDN_REP * hh] + gk[GDN_REP * hh + 1] for hh in range(GDN_HB)]
                                        + [gv[b] for b in range(nh)], axis=1)
        lane = lax.broadcasted_iota(jnp.int32, (c, 2 * GDN_V_HEADS), 1)
        lane1 = lax.broadcasted_iota(jnp.int32, (1, GDN_V_HEADS), 1)
        dba = jnp.zeros((c, 2 * GDN_V_HEADS), F32)
        dal = jnp.zeros((1, GDN_V_HEADS), F32)
        ddb = jnp.zeros((1, GDN_V_HEADS), F32)
        for b in range(nh):
            hv = j * nh + b
            dba = dba + jnp.where(lane == hv, gbl[b], 0.0) + jnp.where(lane == GDN_V_HEADS + hv, ga[b], 0.0)
            dal = dal + jnp.where(lane1 == hv, gal[b], 0.0)
            ddb = ddb + jnp.where(lane1 == hv, gdb[b], 0.0)
        dba_ref[...] += dba
        dal_ref[...] += dal
        ddb_ref[...] += ddb

    small = pl.BlockSpec((1, GDN_V_HEADS), lambda n, j: (0, 0))
    return pl.pallas_call(
        body, name=name, grid=(nchunk, GDN_K_HEADS // GDN_HB),
        in_specs=specs + [pl.BlockSpec((None, GDN_HB, GDN_REP, d, d), lambda n, j: (at(n), j, 0, 0, 0)),
                          pl.BlockSpec((None, GDN_HB * GDN_REP, c, c), lambda n, j: (at(n), j, 0, 0)),
                          pl.BlockSpec((c, GDN_HB * GDN_REP * d), lambda n, j: (at(n), j))],
        out_specs=[pl.BlockSpec((c, GDN_CONV_W), lambda n, j: (at(n), 0)),
                   pl.BlockSpec((c, 2 * GDN_V_HEADS), lambda n, j: (at(n), 0)),
                   small, small],
        out_shape=[jax.ShapeDtypeStruct((t, GDN_CONV_W), F32), jax.ShapeDtypeStruct((t, 2 * GDN_V_HEADS), F32),
                   jax.ShapeDtypeStruct((1, GDN_V_HEADS), F32), jax.ShapeDtypeStruct((1, GDN_V_HEADS), F32)],
        scratch_shapes=[pltpu.VMEM((GDN_K_HEADS, GDN_REP, d, d), F32)],
        compiler_params=_params("arbitrary", "arbitrary"),
    )(act, act, act, ba, alog, dtb, states, inverses, dout)


N_DEV = 8
ANY = pl.BlockSpec(memory_space=pl.ANY)


def _coords():
    return lax.axis_index("x"), lax.axis_index("y"), lax.axis_index("c")


def _other_chips(x, y):
    return [(1 - x, y), (x, 1 - y), (1 - x, 1 - y)]


def _remote(src, dst, send_sems, recv_sems, k, to):
    return pltpu.make_async_remote_copy(src_ref=src, dst_ref=dst, send_sem=send_sems.at[k], recv_sem=recv_sems.at[k],
                                        device_id=to, device_id_type=MESH)


def _dma_sems(n):
    return [pltpu.SemaphoreType.DMA((n,)), pltpu.SemaphoreType.DMA((n,))]


def _gather_copies(ins, outs, send_sems, recv_sems, local_sems, only_first=False):
    x, y, c = _coords()
    sibling = (x, y, 1 - c)
    local, sends, arrivals, relays, relayed = [], [], [], [], []
    for a, (x_ref, out_ref) in enumerate(zip(ins, outs)):
        local.append(pltpu.make_async_copy(x_ref, out_ref.at[2 * x + y], local_sems.at[a]))
        for j, (cx, cy) in enumerate(_other_chips(x, y)):
            k, theirs = 6 * a + j, 2 * cx + cy
            sends.append(_remote(x_ref.at[c], out_ref.at[2 * x + y, c], send_sems, recv_sems, k, (cx, cy, c)))
            if only_first:
                continue
            arrivals.append(_remote(x_ref.at[c], out_ref.at[theirs, c], send_sems, recv_sems, k, (cx, cy, c)))
            relays.append(_remote(out_ref.at[theirs, c], out_ref.at[theirs, c], send_sems, recv_sems, k + 3, sibling))
            relayed.append(_remote(x_ref.at[c], out_ref.at[theirs, 1 - c], send_sems, recv_sems, k + 3, sibling))
    return local, sends, arrivals, relays, relayed


def _gather_start(copies):
    local, sends, _, _, _ = copies
    for cp in local + sends:
        cp.start()


def _gather_finish(copies):
    local, sends, arrivals, relays, relayed = copies
    for landed, relay in zip(arrivals, relays):
        landed.wait_recv()
        relay.start()
    for cp in relayed:
        cp.wait_recv()
    for cp in sends + relays:
        cp.wait_send()
    for cp in local:
        cp.wait()


def _gather_scratch(na):
    return _dma_sems(6 * na) + [pltpu.SemaphoreType.DMA((na,))]


def _gather_quarters(parts, *, name):
    na = len(parts)

    def body(*refs):
        copies = _gather_copies(refs[:na], refs[na:2 * na], *refs[2 * na:])
        _gather_start(copies)
        _gather_finish(copies)

    return pl.pallas_call(
        body, name=name, in_specs=[ANY] * na, out_specs=[ANY] * na,
        out_shape=[jax.ShapeDtypeStruct((N_CHIPS,) + p.shape, p.dtype) for p in parts],
        scratch_shapes=_gather_scratch(na),
    )(*parts)


def _swap_halves(grads, *, name):
    na = len(grads)

    def body(*refs):
        ins, outs = refs[:na], refs[na:2 * na]
        send_sems, recv_sems = refs[2 * na:]
        x, y, c = _coords()
        sends = [_remote(g_ref.at[j, 1 - c], o_ref.at[j], send_sems, recv_sems, N_CHIPS * a + j, (x, y, 1 - c))
                 for a, (g_ref, o_ref) in enumerate(zip(ins, outs)) for j in range(N_CHIPS)]
        for cp in sends:
            cp.start()
        for cp in sends:
            cp.wait()

    return pl.pallas_call(
        body, name=name, in_specs=[ANY] * na, out_specs=[ANY] * na,
        out_shape=[jax.ShapeDtypeStruct((N_CHIPS,) + g.shape[2:], g.dtype) for g in grads],
        scratch_shapes=_dma_sems(N_CHIPS * na),
    )(*grads)


def _scatter_copies(ins, outs, send_sems, recv_sems):
    x, y, c = _coords()
    return [_remote(p_ref.at[2 * cx + cy], outs[3 * a + j], send_sems, recv_sems, 3 * a + j, (cx, cy, c))
            for a, p_ref in enumerate(ins) for j, (cx, cy) in enumerate(_other_chips(x, y))]


def _scatter_shapes(pairs):
    return [jax.ShapeDtypeStruct(p.shape[1:], p.dtype) for p in pairs for _ in range(3)]


def _scatter_quarters(pairs, *, name):
    na = len(pairs)

    def body(*refs):
        sends = _scatter_copies(refs[:na], refs[na:4 * na], *refs[4 * na:])
        for cp in sends:
            cp.start()
        for cp in sends:
            cp.wait()

    out = pl.pallas_call(
        body, name=name, in_specs=[ANY] * na, out_specs=[ANY] * (3 * na), out_shape=_scatter_shapes(pairs),
        scratch_shapes=_dma_sems(3 * na),
    )(*pairs)
    return [out[3 * a:3 * a + 3] for a in range(na)]


def _share_halves(tots, *, name):
    na = len(tots)

    def body(*refs):
        ins, outs = refs[:na], refs[na:2 * na]
        send_sems, recv_sems = refs[2 * na:]
        x, y, c = _coords()
        sends = [_remote(t_ref, o_ref, send_sems, recv_sems, a, (x, y, 1 - c))
                 for a, (t_ref, o_ref) in enumerate(zip(ins, outs))]
        for cp in sends:
            cp.start()
        for cp in sends:
            cp.wait()

    return pl.pallas_call(
        body, name=name, in_specs=[ANY] * na, out_specs=[ANY] * na,
        out_shape=[jax.ShapeDtypeStruct(t.shape, t.dtype) for t in tots],
        scratch_shapes=_dma_sems(na),
    )(*tots)


def _gather_all(vec, *, name):
    m, w = vec.shape

    def body(x_ref, out_ref, send_sems, recv_sems, local_sem):
        x, y, c = _coords()
        me, sibling = (x, y, c), (x, y, 1 - c)
        chips = _other_chips(x, y)

        def rows(px, py, pc):
            return out_ref.at[pl.ds((4 * px + 2 * py + pc) * m, m), :]

        def copy(k, block, to, src=None):
            return _remote(rows(*block) if src is None else src, rows(*block), send_sems, recv_sems, k, to)

        mine = pltpu.make_async_copy(x_ref, rows(*me), local_sem)
        mine.start()
        first = [copy(0, me, sibling, src=x_ref)]
        first += [copy(1 + j, me, (*chip, c), src=x_ref) for j, chip in enumerate(chips)]
        for cp in first:
            cp.start()
        passed = [copy(4 + j, (*chip, c), sibling) for j, chip in enumerate(chips)]
        for j, chip in enumerate(chips):
            copy(1 + j, (*chip, c), me).wait_recv()
            passed[j].start()
        copy(0, sibling, me).wait_recv()
        for j, chip in enumerate(chips):
            copy(4 + j, (*chip, 1 - c), me).wait_recv()
        for cp in first + passed:
            cp.wait_send()
        mine.wait()

    vm = pl.BlockSpec(memory_space=pltpu.VMEM)
    return pl.pallas_call(
        body, name=name, in_specs=[vm], out_specs=vm, out_shape=jax.ShapeDtypeStruct((N_DEV * m, w), vec.dtype),
        scratch_shapes=_dma_sems(7) + [pltpu.SemaphoreType.DMA(())],
    )(vec)


def _sum_blocks(allv, n, *, name):
    m = allv.shape[0] // n

    def body(a_ref, o_ref):
        acc = a_ref[0:m, :]
        for d in range(1, n):
            acc = acc + a_ref[d * m:(d + 1) * m, :]
        o_ref[...] = acc

    return pl.pallas_call(body, name=name, out_shape=jax.ShapeDtypeStruct((m, allv.shape[1]), allv.dtype))(allv)


EW_BLOCK_BYTES = 1 << 20


def _ew_rows(rows, w):
    return _tile(rows, max(8, (EW_BLOCK_BYTES // (4 * w)) // 8 * 8), 8)


def _add_pair(g, got, c, *, name):
    _, _, rows, w = g.shape
    tr = _ew_rows(rows, w)

    def body(c_ref, g_ref, got_ref, o_ref):
        o_ref[...] = (g_ref[...] + got_ref[...]).astype(o_ref.dtype)

    blk = pl.BlockSpec((None, tr, w), lambda q, i, c_ref: (q, i, 0))
    return pl.pallas_call(
        body, name=name,
        grid_spec=pltpu.PrefetchScalarGridSpec(
            num_scalar_prefetch=1, grid=(N_CHIPS, rows // tr),
            in_specs=[pl.BlockSpec((None, None, tr, w), lambda q, i, c_ref: (q, c_ref[0], i, 0)), blk], out_specs=blk),
        out_shape=jax.ShapeDtypeStruct(got.shape, BF16),
        compiler_params=_params("parallel", "parallel"),
    )(c, g, got)


def _add_chips(pair, recv, chip, *, name):
    _, rows, w = pair.shape
    tr = _ew_rows(rows, w)

    def body(chip_ref, p_ref, r0_ref, r1_ref, r2_ref, o_ref):
        f = lambda r: r[...].astype(F32)
        o_ref[...] = ((f(p_ref) + f(r0_ref)) + f(r1_ref)) + f(r2_ref)

    blk = pl.BlockSpec((tr, w), lambda i, chip_ref: (i, 0))
    return pl.pallas_call(
        body, name=name,
        grid_spec=pltpu.PrefetchScalarGridSpec(
            num_scalar_prefetch=1, grid=(rows // tr,),
            in_specs=[pl.BlockSpec((None, tr, w), lambda i, chip_ref: (chip_ref[0], i, 0)), blk, blk, blk], out_specs=blk),
        out_shape=jax.ShapeDtypeStruct((rows, w), F32),
        compiler_params=_params("parallel"),
    )(chip, pair, *recv)


def _adamw_math(w, g, m, v):
    nm = ADAM_B1 * m + (1.0 - ADAM_B1) * g
    nv = ADAM_B2 * v + (1.0 - ADAM_B2) * (g * g)
    m_hat = nm / (1.0 - ADAM_B1 ** ADAM_STEP)
    v_hat = nv / (1.0 - ADAM_B2 ** ADAM_STEP)
    return -ADAM_LR * (m_hat / (jnp.sqrt(v_hat) + ADAM_EPS) + ADAM_WD * w), nm, nv


def _adamw(w, g, m, v, *, name):
    shape = w.shape
    last = shape[-1]
    w2, g2, m2, v2 = (a.reshape(-1, last) for a in (w, g, m, v))
    rows = w2.shape[0]
    tm = _ew_rows(rows, last)

    def body(w_ref, g_ref, m_ref, v_ref, d_ref, nm_ref, nv_ref):
        d_ref[...], nm_ref[...], nv_ref[...] = _adamw_math(w_ref[...], g_ref[...], m_ref[...], v_ref[...])

    spec = pl.BlockSpec((tm, last), lambda i: (i, 0))
    out = jax.ShapeDtypeStruct((rows, last), F32)
    d, nm, nv = pl.pallas_call(
        body, name=name, grid=(rows // tm,), in_specs=[spec] * 4, out_specs=[spec] * 3, out_shape=[out] * 3,
        compiler_params=_params("parallel"),
    )(w2, g2, m2, v2)
    return d.reshape(shape), nm.reshape(shape), nv.reshape(shape)


def _adamw_halves(w, m, v, mine, theirs, c, *, name, into=None):
    rows, wd = w.shape[-2:]
    tr = _ew_rows(rows, wd)
    bufs, at = into if into is not None else ((), ())

    def body(c_ref, w_ref, m_ref, v_ref, a_ref, b_ref, *rest):
        g_ref, d_ref, nm_ref, nv_ref = rest[len(bufs):]
        g = jnp.where(pl.program_id(0) == c_ref[0], a_ref[...], b_ref[...])
        g_ref[...] = g
        d_ref[...], nm_ref[...], nv_ref[...] = _adamw_math(w_ref[...], g, m_ref[...], v_ref[...])

    full = pl.BlockSpec((None,) * (1 + len(at)) + (tr, wd), lambda hf, i, c_ref: at + (hf, i, 0))
    half = pl.BlockSpec((tr, wd), lambda hf, i, c_ref: (i, 0))
    out = jax.ShapeDtypeStruct(w.shape, F32)
    return pl.pallas_call(
        body, name=name,
        grid_spec=pltpu.PrefetchScalarGridSpec(num_scalar_prefetch=1, grid=(2, rows // tr),
                                               in_specs=[full] * 3 + [half] * 2 + [ANY] * len(bufs),
                                               out_specs=[full] * 4),
        out_shape=[out] * 4, input_output_aliases={6 + b: b for b in range(len(bufs))},
        compiler_params=_params("parallel", "parallel"),
    )(c, w, m, v, mine, theirs, *bufs)


def _join_quarters(q, *, name):
    _, rows, n = q.shape
    tr = _tile(rows, 256, 16)

    def body(q_ref, o_ref):
        o_ref[...] = jnp.concatenate([q_ref[s] for s in range(N_CHIPS)], axis=1)

    return pl.pallas_call(
        body, name=name, grid=(rows // tr,),
        in_specs=[pl.BlockSpec((N_CHIPS, tr, n), lambda i: (0, i, 0))],
        out_specs=pl.BlockSpec((tr, N_CHIPS * n), lambda i: (i, 0)),
        out_shape=jax.ShapeDtypeStruct((rows, N_CHIPS * n), q.dtype),
        compiler_params=_params("parallel"),
    )(q)


def _split_quarters(pieces, *, name):
    rows = pieces[0].shape[0]
    n = sum(p.shape[1] for p in pieces) // N_CHIPS
    tr = _tile(rows, 256, 16)

    def body(*refs):
        x = jnp.concatenate([r[...] for r in refs[:-1]], axis=1)
        for s in range(N_CHIPS):
            refs[-1][s] = x[:, s * n:(s + 1) * n]

    return pl.pallas_call(
        body, name=name, grid=(rows // tr,),
        in_specs=[pl.BlockSpec((tr, p.shape[1]), lambda i: (i, 0)) for p in pieces],
        out_specs=pl.BlockSpec((N_CHIPS, tr, n), lambda i: (0, i, 0)),
        out_shape=jax.ShapeDtypeStruct((N_CHIPS, rows, n), pieces[0].dtype),
        compiler_params=_params("parallel"),
    )(*pieces)


_WEIGHTS = ['ffn_norm', 'ffn_w_gate', 'ffn_w_up', 'ffn_w_down', 'mix_norm', 'att_w_in', 'att_q_norm', 'att_k_norm',
            'att_sinks', 'att_w_out', 'gdn_w_in', 'gdn_conv_w', 'gdn_a_log', 'gdn_dt_bias', 'gdn_out_norm', 'gdn_w_out',
            'ple_norm', 'ple_w_gate', 'ple_w_proj']
_BIG = ['ffn_w_gate', 'ffn_w_up', 'ffn_w_down', 'att_w_in', 'att_w_out', 'gdn_w_in', 'gdn_w_out', 'ple_w_gate',
        'ple_w_proj']
_SMALL_CUT = {'ffn_norm': 2, 'gdn_conv_w': 2}
_WHOLE = ['mix_norm', 'att_q_norm', 'att_k_norm', 'att_sinks', 'gdn_a_log', 'gdn_dt_bias', 'gdn_out_norm', 'ple_norm']
PACK_W = 1024
SMALL_ROW_MULT = 8


def _halves(a):
    return a.reshape(2, -1, a.shape[-1])


def _from_quarters(blk, axis):
    full = jnp.moveaxis(blk, 0, axis)
    shp = list(full.shape)
    shp[axis:axis + 2] = [shp[axis] * shp[axis + 1]]
    return full.reshape(shp)


def _to_quarters(full, axis):
    shp = list(full.shape)
    shp[axis:axis + 1] = [N_CHIPS, shp[axis] // N_CHIPS]
    return jnp.moveaxis(full.reshape(shp), axis, 0)


def _pack(parts, row_mult):
    flat = jnp.concatenate(parts, axis=-1)
    n = flat.shape[-1]
    rows = -(-n // (PACK_W * row_mult)) * row_mult
    return jnp.pad(flat, [(0, rows * PACK_W - n)]).reshape(rows, PACK_W)


def _unpack(flat, shapes):
    lead = flat.shape[:-2]
    flat = flat.reshape(lead + (-1,))
    out, off = [], 0
    for shp in shapes:
        n = math.prod(shp)
        out.append(flat[..., off:off + n].reshape(lead + tuple(shp)))
        off += n
    return out


FFN_TM = 1024


def _ffn_up(hn, wg, wu, at, *, name):
    t, d = hn.shape
    fq = wg.shape[-1]
    tm = _tile(t, FFN_TM)

    def body(h_ref, wg_ref, wu_ref, g_ref, u_ref, a_ref):
        h = h_ref[...]
        g, u = _dg(h, _b(wg_ref[...]), 1, 0), _dg(h, _b(wu_ref[...]), 1, 0)
        g_ref[...] = g.astype(BF16)
        u_ref[...] = u.astype(BF16)
        a_ref[...] = _f_swiglu(g, u)[0].astype(BF16)

    w_spec = pl.BlockSpec((None,) * (1 + len(at)) + (d, fq), lambda s, i: (s,) + at + (0, 0))
    o_spec = pl.BlockSpec((None, tm, fq), lambda s, i: (s, i, 0))
    out = jax.ShapeDtypeStruct((N_CHIPS, t, fq), BF16)
    return pl.pallas_call(
        body, name=name, grid=(N_CHIPS, t // tm),
        in_specs=[pl.BlockSpec((tm, d), lambda s, i: (i, 0)), w_spec, w_spec], out_specs=[o_spec] * 3,
        out_shape=[out] * 3, compiler_params=_params("parallel", "parallel"),
    )(hn, wg, wu)


def _ffn_d_up(dout, wd, g, u, at, *, name):
    t, d = dout.shape
    fq = wd.shape[-2]
    tm = _tile(t, FFN_TM)

    def body(do_ref, wd_ref, g_ref, u_ref, dg_ref, du_ref):
        da = _dg(_b(do_ref[...]), _b(wd_ref[...]), 1, 1) * 0.5
        _, vjp = jax.vjp(_f_swiglu, g_ref[...].astype(F32), u_ref[...].astype(F32))
        dg, du = vjp((da,))
        dg_ref[...] = dg.astype(BF16)
        du_ref[...] = du.astype(BF16)

    w_spec = pl.BlockSpec((None,) * (1 + len(at)) + (fq, d), lambda s, i: (s,) + at + (0, 0))
    o_spec = pl.BlockSpec((None, tm, fq), lambda s, i: (s, i, 0))
    out = jax.ShapeDtypeStruct((N_CHIPS, t, fq), BF16)
    return pl.pallas_call(
        body, name=name, grid=(N_CHIPS, t // tm),
        in_specs=[pl.BlockSpec((tm, d), lambda s, i: (i, 0)), w_spec, o_spec, o_spec], out_specs=[o_spec] * 2,
        out_shape=[out] * 2, compiler_params=_params("parallel", "parallel"),
    )(dout, wd, g, u)


def _ffn_fwd(h, gain, wg, wu, wd, at, tag):
    lead = (Q,) + at
    hn, = _row_fwd(_f_rms, [h], [gain], [(D_MODEL, BF16)], name=f"{tag}_norm")
    g, u, a = _ffn_up(hn, wg, wu, at, name=f"{tag}_up")
    out = _mm((a, (Q,)), (wd, lead), res=h, scale=0.5, name=f"{tag}_down")
    return out, (h, hn, g, u, a)


def _ffn_bwd(dout, saved, gain, wg, wu, wd, at, grads, g_at, tag):
    h, hn, g, u, a = saved
    lead = (Q,) + at
    g_lead = (Q,) + g_at
    dg, du = _ffn_d_up(dout, wd, g, u, at, name=f"{tag}_d_up")
    g_gate, g_up, g_down = grads
    g_down = _mm((a, (Q,)), dout, ta=True, scale=0.5, into=(g_down, g_lead), name=f"{tag}_dw_down")
    g_gate = _mm((dg, (Q,)), hn, ta=True, into=(g_gate, g_lead), name=f"{tag}_dw_gate")
    g_up = _mm((du, (Q,)), hn, ta=True, into=(g_up, g_lead), name=f"{tag}_dw_up")
    dhn = _mm((dg, (Q,)), (wg, lead), tb=True, name=f"{tag}_d_norm_gate")
    dh, dgain = _mm((du, (Q,)), (wu, lead), tb=True, res=dhn, norm_bwd=(h, gain, dout), name=f"{tag}_d_in")
    return dh, dgain, (g_gate, g_up, g_down)


def _att_fwd(h, gain, w_in, qg, kg, sinks, w_out, gather):
    hn, = _row_fwd(_f_rms, [h], [gain], [(D_MODEL, BF16)], name="att_norm")
    proj = _mm(hn, w_in, out_dtype=BF16, name="att_in")
    a, rtot, *gathered = _sb_fwd(proj, name="att_sb", gather=gather)
    b = _swa_fwd(proj, qg, kg, sinks, name="att_swa")
    out = _mm(a, (w_out, (0,)), res=h, name="att_out_sb")
    out = _mm(b, (w_out, (1,)), res=out, name="att_out_swa")
    return out, (h, hn, proj, a, rtot, b), gathered


def _att_bwd(dout, saved, gain, w_in, qg, kg, sinks, w_out, scatter):
    h, hn, proj, a, rtot, b = saved
    da = _mm(dout, (w_out, (0,)), tb=True, name="att_d_sb")
    db = _mm(dout, (w_out, (1,)), tb=True, name="att_d_swa")
    dw_out = lax.empty(w_out.shape, F32)
    dw_out = _mm(a, dout, ta=True, into=(dw_out, (0,)), name="att_dw_out_sb")
    dw_out = _mm(b, dout, ta=True, into=(dw_out, (1,)), name="att_dw_out_swa")
    dq, dk, dv, *landed = _sb_bwd(proj, rtot, da, name="att_sb_bwd", scatter=scatter)
    dqb, dkb, dvb, dqg, dkg, dsk = _swa_bwd(proj, qg, kg, sinks, db, name="att_swa_bwd")
    dproj = jnp.concatenate([dq, dk, dv, dqb, dkb, dvb], axis=1)
    dw_in = _mm(hn, dproj, ta=True, name="att_dw_in")
    dh, dgain = _mm(dproj, w_in, tb=True, norm_bwd=(h, gain, dout), name="att_d_in")
    return dh, dgain, dw_in, dqg, dkg, dsk, dw_out, [landed[3 * a:3 * a + 3] for a in range(len(scatter))]


def _gdn_layer_fwd(h, gain, w_in, conv_w, alog, dtb, out_gain, w_out):
    w_qkv, w_z, w_ba = w_in[:, :GDN_CONV_W], w_in[:, GDN_CONV_W:GDN_CONV_W + GDN_VW], w_in[:, GDN_CONV_W + GDN_VW:]
    hn, = _row_fwd(_f_rms, [h], [gain], [(D_MODEL, BF16)], name="gdn_norm")
    pq = _mm(hn, w_qkv, name="gdn_in_qkv")
    pz = _mm(hn, w_z, name="gdn_in_z")
    ba = _mm(hn, w_ba, name="gdn_in_ba")
    act = _conv_fwd(pq, conv_w, name="gdn_conv")
    o, states, inverses = _gdn_fwd(act, ba, alog, dtb, name="gdn_rule")
    y, = _row_fwd(_f_gdn_out, [o, pz], [out_gain], [(GDN_VW, BF16)], name="gdn_gate")
    out = _mm(y, w_out, res=h, name="gdn_out")
    return out, (h, hn, pq, pz, ba, act, o, states, inverses, y, (w_qkv, w_z, w_ba))


def _gdn_layer_bwd(dout, saved, gain, conv_w, alog, dtb, out_gain, w_out):
    h, hn, pq, pz, ba, act, o, states, inverses, y, (w_qkv, w_z, w_ba) = saved
    dy = _mm(dout, w_out, tb=True, name="gdn_d_gate")
    dw_out = _mm(y, dout, ta=True, name="gdn_dw_out")
    do, dpz, dout_gain = _row_bwd(_f_gdn_out, [o, pz], [out_gain], [dy], [(0, F32), (1, F32)], [0], name="gdn_gate_bwd")
    dact, dba, dal, ddb = _gdn_bwd(act, ba, alog, dtb, states, inverses, do, name="gdn_rule_bwd")
    dpq, dconv = _conv_bwd(pq, conv_w, dact, name="gdn_conv_bwd")
    dw_in = [_mm(hn, dpq, ta=True, name="gdn_dw_qkv"), _mm(hn, dpz, ta=True, name="gdn_dw_z"),
             _mm(hn, dba, ta=True, name="gdn_dw_ba")]
    dhn = _mm(dpq, w_qkv, tb=True, name="gdn_d_norm_qkv")
    dhn = _mm(dpz, w_z, tb=True, res=dhn, name="gdn_d_norm_z")
    dh, dgain = _mm(dba, w_ba, tb=True, res=dhn, norm_bwd=(h, gain, dout), name="gdn_d_in")
    return dh, dgain, dw_in, dconv, dal, ddb, dout_gain, dw_out


def _ple_fwd(h, gain, w_gate, w_proj, pe, tag):
    hn, = _row_fwd(_f_rms, [h], [gain], [(D_MODEL, BF16)], name=f"{tag}_norm")
    gl = _mm(hn, w_gate, name=f"{tag}_gate")
    pp = _mm(pe, w_proj, name=f"{tag}_proj")
    out, = _row_fwd(_f_ple, [h, gl, pp], [], [(D_MODEL, F32)], name=f"{tag}_mix")
    return out, (h, hn, gl, pp)


def _ple_bwd(dout, saved, gain, w_gate, pe, tag):
    h, hn, gl, pp = saved
    dha, dgl, dpp = _row_bwd(_f_ple, [h, gl, pp], [], [dout], [(0, F32), (1, BF16), (2, BF16)], [], name=f"{tag}_mix_bwd")
    dw_gate = _mm(hn, dgl, ta=True, name=f"{tag}_dw_gate")
    dw_proj = _mm(pe, dpp, ta=True, name=f"{tag}_dw_proj")
    dh, dgain = _mm(dgl, w_gate, tb=True, norm_bwd=(h, gain, dha), name=f"{tag}_d_in")
    return dh, dgain, dw_gate, dw_proj


def kernel(x, p, ffn_norm, ffn_w_gate, ffn_w_up, ffn_w_down, mix_norm, att_w_in, att_q_norm, att_k_norm, att_sinks, att_w_out, gdn_w_in, gdn_conv_w, gdn_a_log, gdn_dt_bias, gdn_out_norm, gdn_w_out, ple_norm, ple_w_gate, ple_w_proj, loss_target, m_ffn_norm, m_ffn_w_gate, m_ffn_w_up, m_ffn_w_down, m_mix_norm, m_att_w_in, m_att_q_norm, m_att_k_norm, m_att_sinks, m_att_w_out, m_gdn_w_in, m_gdn_conv_w, m_gdn_a_log, m_gdn_dt_bias, m_gdn_out_norm, m_gdn_w_out, m_ple_norm, m_ple_w_gate, m_ple_w_proj, v_ffn_norm, v_ffn_w_gate, v_ffn_w_up, v_ffn_w_down, v_mix_norm, v_att_w_in, v_att_q_norm, v_att_k_norm, v_att_sinks, v_att_w_out, v_gdn_w_in, v_gdn_conv_w, v_gdn_a_log, v_gdn_dt_bias, v_gdn_out_norm, v_gdn_w_out, v_ple_norm, v_ple_w_gate, v_ple_w_proj):
    arg = dict(locals())
    cx, cy, cc = _coords()
    chip = (2 * cx + cy).astype(jnp.int32).reshape(1)
    core = cc.astype(jnp.int32).reshape(1)
    n_layers = ffn_norm.shape[0]

    quarter = lambda n, i=None: _halves((arg[n] if i is None else arg[n][i]).astype(BF16))
    ffn_names = ('ffn_w_gate', 'ffn_w_up', 'ffn_w_down')
    early = [quarter(n, 0) for n in ffn_names] + [quarter('att_w_in'), quarter('att_w_out')]
    late_names = ('gdn_w_in', 'gdn_w_out', 'ple_w_gate', 'ple_w_proj')
    late = [quarter(n, 1) for n in ffn_names] + [quarter(n) for n in late_names]
    *ffn_w0, att_in_q, att_out_q = _gather_quarters(early, name="gather_weights")
    wt = {'att_w_in': _join_quarters(att_in_q.reshape((N_CHIPS,) + att_w_in.shape[1:]), name="att_w_in_join"),
          'att_w_out': att_out_q.reshape(2, SB_W, D_MODEL)}

    small_names = list(_SMALL_CUT)
    small_shapes = [arg[n].shape for n in small_names]
    svec = _pack([arg[n].reshape(-1) for n in small_names], SMALL_ROW_MULT)
    srows = svec.shape[0]
    sall = _gather_all(svec, name="gather_gains").reshape(N_CHIPS, 2, srows, PACK_W)[:, 0]
    for n, q in zip(small_names, _unpack(sall, small_shapes)):
        wt[n] = _from_quarters(q, _SMALL_CUT[n])
    row = lambda v: v.reshape(1, -1)

    as_ffn = lambda g, n: g.reshape((N_CHIPS,) + arg[n].shape[1:])
    ffn_w = [tuple(as_ffn(g, n) for g, n in zip(ffn_w0, ffn_names)), None]
    h = x[0]
    tape = []
    for i in range(n_layers):
        j = i // 2
        h, s0 = _ffn_fwd(h, row(wt['ffn_norm'][i, 0]), *ffn_w[i], (0,), f"ffn{i}a")
        if i % 2 == 0:
            h, sm, gathered = _att_fwd(h, row(mix_norm[i]), wt['att_w_in'], att_q_norm[j:j + 1], att_k_norm[j:j + 1],
                                       att_sinks[j:j + 1], wt['att_w_out'], late)
            ffn_w[1] = tuple(as_ffn(g, n) for g, n in zip(gathered[:3], ffn_names))
            wq = {n: g.reshape((N_CHIPS,) + arg[n].shape) for n, g in zip(late_names, gathered[3:])}
            wt['gdn_w_in'] = _join_quarters(wq['gdn_w_in'][:, 0], name="gdn_w_in_join")
            wt['gdn_w_out'] = wq['gdn_w_out'].reshape(GDN_VW, D_MODEL)
            wt['ple_w_gate'] = _from_quarters(wq['ple_w_gate'], 1)
            wt['ple_w_proj'] = _from_quarters(wq['ple_w_proj'], 2)
        else:
            h, sm = _gdn_layer_fwd(h, row(mix_norm[i]), wt['gdn_w_in'], wt['gdn_conv_w'][j], gdn_a_log[j:j + 1],
                                   gdn_dt_bias[j:j + 1], gdn_out_norm[j:j + 1], wt['gdn_w_out'])
        h, s1 = _ffn_fwd(h, row(wt['ffn_norm'][i, 1]), *ffn_w[i], (1,), f"ffn{i}b")
        h, sp = _ple_fwd(h, row(ple_norm[i]), wt['ple_w_gate'][i], wt['ple_w_proj'][i], p[i, 0], f"ple{i}")
        tape.append((s0, sm, s1, sp))

    dh, loss_local = _loss_head(h, loss_target[0], name="loss_head")
    loss = lax.psum(loss_local, ("x", "y", "c"))

    gr = {}
    stored_t = ('ffn_w_gate', 'ffn_w_up')
    as_stored = lambda a, n: jnp.swapaxes(a, -1, -2) if n in stored_t else a
    ffn_g = [tuple(lax.empty((N_CHIPS,) + as_stored(arg[n], n).shape[1:], F32) for n in ffn_names)
             for _ in range(n_layers)]
    d_ffn_norm = [[None, None] for _ in range(n_layers)]
    d_mix, d_ple_norm, d_ple_gate, d_ple_proj = [None] * n_layers, [None] * n_layers, [None] * n_layers, [None] * n_layers

    def as_halves(g):
        return g.reshape((N_CHIPS, 2, -1, g.shape[-1]))

    def pair_up(keys, grads, tag):
        got = _swap_halves(grads, name=f"grad_swap_halves_{tag}")
        return [_add_pair(g, o, core, name=f"grad_add_pair_{k}") for k, g, o in zip(keys, grads, got)]

    for i in reversed(range(n_layers)):
        j = i // 2
        s0, sm, s1, sp = tape[i]
        dh, d_ple_norm[i], d_ple_gate[i], d_ple_proj[i] = _ple_bwd(dh, sp, row(ple_norm[i]), wt['ple_w_gate'][i], p[i, 0],
                                                                   f"ple{i}")
        dh, d_ffn_norm[i][1], ffn_g[i] = _ffn_bwd(dh, s1, row(wt['ffn_norm'][i, 1]), *ffn_w[i], (1,), ffn_g[i], (1,),
                                                  f"ffn{i}b")
        if i % 2 == 0:
            gr['ple_w_gate'] = _to_quarters(jnp.stack(d_ple_gate), 1)
            gr['ple_w_proj'] = _to_quarters(jnp.stack(d_ple_proj), 2)
            first_keys = [f"{n}_1" for n in ffn_names] + list(late_names)
            first_pairs = pair_up(first_keys, [as_halves(g) for g in ffn_g[1]] + [as_halves(gr[n]) for n in late_names], "a")
            (dh, d_mix[i], dw_in, gr['att_q_norm'], gr['att_k_norm'], gr['att_sinks'], dw_out,
             first_recv) = _att_bwd(dh, sm, row(mix_norm[i]), wt['att_w_in'], att_q_norm[j:j + 1],
                                    att_k_norm[j:j + 1], att_sinks[j:j + 1], wt['att_w_out'], first_pairs)
            gr['att_w_in'] = _split_quarters([dw_in], name="att_dw_in_split")
            gr['att_w_out'] = dw_out
        else:
            (dh, d_mix[i], dw_in, dconv, gr['gdn_a_log'], gr['gdn_dt_bias'], gr['gdn_out_norm'],
             dw_out) = _gdn_layer_bwd(dh, sm, row(mix_norm[i]), wt['gdn_conv_w'][j], gdn_a_log[j:j + 1],
                                      gdn_dt_bias[j:j + 1], gdn_out_norm[j:j + 1], wt['gdn_w_out'])
            gr['gdn_w_in'] = _split_quarters(dw_in, name="gdn_dw_in_split")
            gr['gdn_w_out'] = dw_out
            gr['gdn_conv_w'] = dconv[None]
        dh, d_ffn_norm[i][0], ffn_g[i] = _ffn_bwd(dh, s0, row(wt['ffn_norm'][i, 0]), *ffn_w[i], (0,), ffn_g[i], (0,),
                                                  f"ffn{i}a")
    grad_x = dh[None]

    gr['ffn_norm'] = jnp.stack([jnp.stack([d_ffn_norm[i][k][0] for k in range(2)]) for i in range(n_layers)])
    gr['mix_norm'] = jnp.concatenate(d_mix, axis=0)
    gr['ple_norm'] = jnp.concatenate(d_ple_norm, axis=0)

    last_keys = [f"{n}_0" for n in ffn_names] + ['att_w_in', 'att_w_out']
    last_pairs = pair_up(last_keys, [as_halves(g) for g in ffn_g[0]] + [as_halves(gr['att_w_in']), as_halves(gr['att_w_out'])],
                         "b")
    last_recv = _scatter_quarters(last_pairs, name="grad_scatter")
    keys = first_keys + last_keys
    tots = [_add_chips(pr, rc, chip, name=f"grad_add_chips_{k}")
            for k, pr, rc in zip(keys, first_pairs + last_pairs, first_recv + last_recv)]
    theirs = _share_halves(tots, name="grad_share")
    summed = dict(zip(keys, zip(tots, theirs)))

    whole_shapes = [arg[n].shape for n in _WHOLE]
    cut_full_shapes = [gr[n].shape for n in small_names]
    gvec = _pack([gr[n].reshape(-1) for n in _WHOLE + small_names], SMALL_ROW_MULT)
    gall = _sum_blocks(_gather_all(gvec, name="gather_small_grads"), N_DEV, name="sum_small_grads")
    parts = _unpack(gall, whole_shapes + cut_full_shapes)
    gsum = dict(zip(_WHOLE, parts))
    for n, g in zip(small_names, parts[len(_WHOLE):]):
        gsum[n] = lax.dynamic_index_in_dim(_to_quarters(g, _SMALL_CUT[n]), chip[0], axis=0, keepdims=False)

    delta, new_m, new_v = {}, {}, {}
    for n in ('att_w_in', 'att_w_out') + late_names:
        res = _adamw_halves(_halves(arg[n]), _halves(arg["m_" + n]), _halves(arg["v_" + n]), *summed[n], core,
                            name=f"adamw_{n}")
        gsum[n], delta[n], new_m[n], new_v[n] = (r.reshape(arg[n].shape) for r in res)
    for n in ffn_names:
        wmv = [as_stored(arg[k + n], n) for k in ("", "m_", "v_")]
        res = tuple(lax.empty(wmv[0].shape, F32) for _ in range(4))
        for i in range(n_layers):
            res = _adamw_halves(*wmv, *summed[f"{n}_{i}"], core, name=f"adamw_{n}_{i}", into=(res, (i,)))
        gsum[n], delta[n], new_m[n], new_v[n] = (as_stored(r, n) for r in res)
    for n in _WHOLE + small_names:
        delta[n], new_m[n], new_v[n] = _adamw(arg[n], gsum[n], arg["m_" + n], arg["v_" + n], name=f"adamw_{n}")
    return (loss, grad_x, *[gsum[n] for n in _WEIGHTS], *[delta[n] for n in _WEIGHTS],
            *[new_m[n] for n in _WEIGHTS], *[new_v[n] for n in _WEIGHTS])
```

```python
import functools
import math

import jax
import jax.numpy as jnp
from jax import lax
from jax.experimental import pallas as pl
from jax.experimental.pallas import tpu as pltpu

F32 = jnp.float32
BF16 = jnp.bfloat16
MESH = pl.DeviceIdType.MESH

LANES = 128
VMEM_LIMIT_BYTES = 56 * 1024 * 1024

EPS = 1e-6
D_MODEL = 1024
HEAD_DIM = 64
SB_HEADS = 8
SWA_HEADS = 8
SWA_KV_HEADS = 2
WINDOW = 128
GDN_K_HEADS = 8
GDN_V_HEADS = 16
GDN_HEAD_DIM = 128
GDN_CONV = 4
GDN_CHUNK = 64
SB_W = SB_HEADS * HEAD_DIM
SWA_QW = SWA_HEADS * HEAD_DIM
SWA_KVW = SWA_KV_HEADS * HEAD_DIM
GDN_KW = GDN_K_HEADS * GDN_HEAD_DIM
GDN_VW = GDN_V_HEADS * GDN_HEAD_DIM
GDN_CONV_W = 2 * GDN_KW + GDN_VW

ADAM_LR = 0.001
ADAM_B1 = 0.9
ADAM_B2 = 0.999
ADAM_EPS = 1e-08
ADAM_WD = 0.01
ADAM_STEP = 10

NEG = -1e30


def _params(*sem):
    return pltpu.CompilerParams(dimension_semantics=sem or None, vmem_limit_bytes=VMEM_LIMIT_BYTES)


def _tile(n, cap, align=LANES):
    if n <= cap:
        return n
    for t in range(cap - cap % align, 0, -align):
        if n % t == 0:
            return t
    return n


N_CHIPS = 4
MM_VMEM_BUDGET_BYTES = 40 * 1024 * 1024
Q = "q"


def _opnd(x):
    return x if isinstance(x, tuple) else (x, ())


def _mm(a, b, *, name, ta=False, tb=False, out_dtype=F32, res=None, scale=1.0, out_q=False, into=None, norm_bwd=None,
        tm=None, tn=1024, tk=1024):
    (a_arr, a_lead), (b_arr, b_lead) = _opnd(a), _opnd(b)
    (k_a, m) = a_arr.shape[-2:] if ta else a_arr.shape[-2:][::-1]
    (n, k_b) = b_arr.shape[-2:] if tb else b_arr.shape[-2:][::-1]
    if into is not None:
        out_arr, out_lead = into
        out_q, out_dtype = Q in out_lead, out_arr.dtype
    else:
        out_lead = (Q,) if out_q else ()
    red_q = (Q in a_lead or Q in b_lead) and not out_q
    kq = min(k_a, k_b)
    assert (k_a == k_b) or (red_q and max(k_a, k_b) == N_CHIPS * kq), (a_arr.shape, b_arr.shape)
    tn, tk = _tile(n, tn), _tile(kq, tk)
    if tm is None:
        r_item = _opnd(res)[0].dtype.itemsize if res is not None else 0
        per_row = 2 * (tk * a_arr.dtype.itemsize + tn * (jnp.dtype(out_dtype).itemsize + r_item)) + 4 * tn
        if norm_bwd is not None:
            per_row += (2 * 2 + 4) * 4 * tn
        room = MM_VMEM_BUDGET_BYTES - 2 * tk * tn * b_arr.dtype.itemsize
        tm = next(c for c in (4096, 2048, 1024, 512, 256, 128) if c * per_row <= room or c == 128)
    tm = _tile(m, tm)
    nk = kq // tk
    ksteps = nk * (N_CHIPS if red_q else 1)
    dims = (((0 if ta else 1,), (1 if tb else 0,)), ((), ()))
    has_res = res is not None
    n_out = 2 if norm_bwd is not None else 1

    def body(*refs):
        a_ref, b_ref = refs[0], refs[1]
        o_ref, acc_ref = refs[-1 - n_out], refs[-1]
        k = pl.program_id(3)
        first_rows = pl.program_id(1) == 0

        @pl.when(k == 0)
        def _():
            acc_ref[...] = jnp.zeros_like(acc_ref)

        acc_ref[...] += lax.dot_general(a_ref[...].astype(BF16), b_ref[...].astype(BF16), dims,
                                        preferred_element_type=F32)

        @pl.when(k == ksteps - 1)
        def _():
            r = acc_ref[...]
            if scale != 1.0:
                r = r * scale
            if has_res:
                r = r + refs[2][...].astype(F32)
            if norm_bwd is not None:
                h_ref, gain_ref, dres_ref = refs[2 + has_res:5 + has_res]
                dgain_ref = refs[-2]
                _, vjp = jax.vjp(_f_rms_res, h_ref[...], gain_ref[...])
                r, dgain = vjp((r, dres_ref[...]))

                @pl.when(first_rows)
                def _():
                    dgain_ref[...] = jnp.zeros_like(dgain_ref)

                dgain_ref[...] += dgain
            o_ref[...] = r.astype(o_ref.dtype)

    def spec(lead, blk, pos):
        def index(s, i, j, k):
            kk = k % nk if (red_q and Q in lead) else k
            quarter = s if out_q else k // nk
            return tuple(quarter if l == Q else l for l in lead) + pos(i, j, kk)
        return pl.BlockSpec((None,) * len(lead) + blk, index)

    a_spec = spec(a_lead, (tk, tm), lambda i, j, k: (k, i)) if ta else spec(a_lead, (tm, tk), lambda i, j, k: (i, k))
    b_spec = spec(b_lead, (tn, tk), lambda i, j, k: (j, k)) if tb else spec(b_lead, (tk, tn), lambda i, j, k: (k, j))
    o_spec = spec(out_lead, (tm, tn), lambda i, j, k: (i, j))
    in_specs, args = [a_spec, b_spec], [a_arr, b_arr]
    if has_res:
        r_arr, r_lead = _opnd(res)
        in_specs.append(spec(r_lead, (tm, tn), lambda i, j, k: (i, j)))
        args.append(r_arr)
    out_specs, out_shapes = [o_spec], []
    if norm_bwd is not None:
        assert tn == n and not out_q and into is None, "the norm's backward needs whole rows"
        h_arr, gain_arr, dres_arr = norm_bwd
        row_spec = spec((), (tm, tn), lambda i, j, k: (i, j))
        gain_spec = pl.BlockSpec((1, tn), lambda s, i, j, k: (0, 0))
        in_specs += [row_spec, gain_spec, row_spec]
        args += [h_arr, gain_arr, dres_arr]
        out_specs.append(gain_spec)
    aliases = {}
    if into is not None:
        in_specs.append(pl.BlockSpec(memory_space=pl.ANY))
        args.append(out_arr)
        aliases = {len(args) - 1: 0}
        out_shapes.append(jax.ShapeDtypeStruct(out_arr.shape, out_arr.dtype))
    else:
        out_shapes.append(jax.ShapeDtypeStruct(((N_CHIPS,) if out_q else ()) + (m, n), out_dtype))
    if norm_bwd is not None:
        out_shapes.append(jax.ShapeDtypeStruct((1, n), F32))
    out = pl.pallas_call(
        body, name=name, grid=(N_CHIPS if out_q else 1, m // tm, n // tn, ksteps), in_specs=in_specs,
        out_specs=out_specs, out_shape=out_shapes, scratch_shapes=[pltpu.VMEM((tm, tn), F32)],
        input_output_aliases=aliases,
        compiler_params=_params("parallel", *(("arbitrary",) * 3 if norm_bwd is not None else ("parallel", "parallel", "arbitrary"))),
    )(*args)
    return out if norm_bwd is not None else out[0]


def _row_spec(r, tm):
    if isinstance(r, tuple):
        arr, width, cb = r
        return arr, pl.BlockSpec((tm, width), lambda i, cb=cb: (i, cb))
    return r, pl.BlockSpec((tm, r.shape[1]), lambda i: (i, 0))


def _const_spec(c):
    return pl.BlockSpec(c.shape, lambda i: (0,) * c.ndim)


def _row_fwd(fn, rows, consts, outs, *, name, tm=256):
    tm = _tile(_row_spec(rows[0], tm)[0].shape[0], tm, 8)
    arrs, specs = zip(*[_row_spec(r, tm) for r in rows])
    t = arrs[0].shape[0]
    nr, nc = len(rows), len(consts)

    def body(*refs):
        vals = [r[...].astype(F32) for r in refs[:nr + nc]]
        res = fn(*vals)
        for o_ref, v in zip(refs[nr + nc:], res):
            o_ref[...] = v.astype(o_ref.dtype)

    out = pl.pallas_call(
        body, name=name, grid=(t // tm,),
        in_specs=list(specs) + [_const_spec(c) for c in consts],
        out_specs=[pl.BlockSpec((tm, w), lambda i: (i, 0)) for w, _ in outs],
        out_shape=[jax.ShapeDtypeStruct((t, w), dt) for w, dt in outs],
        compiler_params=_params("parallel"),
    )(*arrs, *consts)
    return list(out)


def _row_bwd(fn, rows, consts, cts, row_grads, const_grads, *, name, tm=256):
    tm = _tile(_row_spec(rows[0], tm)[0].shape[0], tm, 8)
    arrs, specs = zip(*[_row_spec(r, tm) for r in rows])
    ct_arrs, ct_specs = zip(*[_row_spec(r, tm) for r in cts])
    t = arrs[0].shape[0]
    nr, nc, nt = len(rows), len(consts), len(cts)
    n_in = nr + nc + nt

    def body(*refs):
        vals = [r[...].astype(F32) for r in refs[:nr + nc]]
        ctv = tuple(r[...].astype(F32) for r in refs[nr + nc:n_in])
        _, vjp = jax.vjp(fn, *vals)
        g = vjp(ctv)
        outs = refs[n_in:]
        for (idx, _), o_ref in zip(row_grads, outs[:len(row_grads)]):
            o_ref[...] = g[idx].astype(o_ref.dtype)
        first = pl.program_id(0) == 0
        for ci, o_ref in zip(const_grads, outs[len(row_grads):]):
            @pl.when(first)
            def _(o_ref=o_ref):
                o_ref[...] = jnp.zeros_like(o_ref)

            o_ref[...] += g[nr + ci]

    widths = [(_row_spec(rows[idx], tm)[1].block_shape[1], dt) for idx, dt in row_grads]
    out = pl.pallas_call(
        body, name=name, grid=(t // tm,),
        in_specs=list(specs) + [_const_spec(c) for c in consts] + list(ct_specs),
        out_specs=[pl.BlockSpec((tm, w), lambda i: (i, 0)) for w, _ in widths]
        + [_const_spec(consts[ci]) for ci in const_grads],
        out_shape=[jax.ShapeDtypeStruct((t, w), dt) for w, dt in widths]
        + [jax.ShapeDtypeStruct(consts[ci].shape, F32) for ci in const_grads],
        compiler_params=_params("arbitrary"),
    )(*arrs, *consts, *ct_arrs)
    return list(out)


def _rms(x, g):
    return x * lax.rsqrt(jnp.mean(x * x, axis=-1, keepdims=True) + EPS) * g


def _f_rms(h, g):
    return (_rms(h, g),)


def _f_rms_res(h, g):
    return (_rms(h, g), h)


def _f_swiglu(g, u):
    return (g * jax.nn.sigmoid(g) * u,)


def _f_ple(h, gl, pp):
    return (h + jax.nn.sigmoid(gl) * pp,)


def _f_gdn_out(o, z, gain):
    outs = []
    for hd in range(GDN_V_HEADS):
        sl = slice(hd * GDN_HEAD_DIM, (hd + 1) * GDN_HEAD_DIM)
        oh, zh = o[:, sl], z[:, sl]
        outs.append(_rms(oh, gain) * (zh * jax.nn.sigmoid(zh)))
    return (jnp.concatenate(outs, axis=1),)


def _loss_head(y, target, *, name, tm=512):
    t, d = y.shape
    tm = _tile(t, tm, 8)

    def body(y_ref, t_ref, dy_ref, l_ref):
        @pl.when(pl.program_id(0) == 0)
        def _():
            l_ref[...] = jnp.zeros_like(l_ref)

        e = y_ref[...] - t_ref[...]
        dy_ref[...] = e * (1.0 / d)
        l_ref[...] += jnp.sum(e * e) * (0.5 / d)

    dy, l = pl.pallas_call(
        body, name=name, grid=(t // tm,),
        in_specs=[pl.BlockSpec((tm, d), lambda i: (i, 0))] * 2,
        out_specs=[pl.BlockSpec((tm, d), lambda i: (i, 0)), pl.BlockSpec((8, LANES), lambda i: (0, 0))],
        out_shape=[jax.ShapeDtypeStruct((t, d), F32), jax.ShapeDtypeStruct((8, LANES), F32)],
        compiler_params=_params("arbitrary"),
    )(y, target)
    return dy, l[0, 0]


def _dg(a, b, ca, cb):
    nb = a.ndim - 2
    batch = tuple(range(nb))
    return lax.dot_general(a, b, (((ca + nb,), (cb + nb,)), (batch, batch)), preferred_element_type=F32)


def _b(x):
    return x.astype(BF16)


@jax.custom_vjp
def _bdot(a, b):
    return _dg(_b(a), _b(b), 1, 0)


def _bdot_fwd(a, b):
    return _bdot(a, b), (a, b)


def _bdot_bwd(r, ct):
    a, b = r
    return _dg(_b(ct), _b(b), 1, 1), _dg(_b(a), _b(ct), 0, 0)


_bdot.defvjp(_bdot_fwd, _bdot_bwd)


@jax.custom_vjp
def _bdot_nt(a, b):
    return _dg(_b(a), _b(b), 1, 1)


def _bdot_nt_fwd(a, b):
    return _bdot_nt(a, b), (a, b)


def _bdot_nt_bwd(r, ct):
    a, b = r
    return _dg(_b(ct), _b(b), 1, 0), _dg(_b(ct), _b(a), 0, 0)


_bdot_nt.defvjp(_bdot_nt_fwd, _bdot_nt_bwd)


@jax.custom_vjp
def _bdot_tn(a, b):
    return _dg(_b(a), _b(b), 0, 0)


def _bdot_tn_fwd(a, b):
    return _bdot_tn(a, b), (a, b)


def _bdot_tn_bwd(r, ct):
    a, b = r
    return _dg(_b(b), _b(ct), 1, 1), _dg(_b(a), _b(ct), 1, 0)


_bdot_tn.defvjp(_bdot_tn_fwd, _bdot_tn_bwd)


def _two(x):
    hi = x.astype(BF16)
    return hi, (x - hi.astype(F32)).astype(BF16)


def _dg3(a, b, ca, cb):
    (ah, al), (bh, bl) = _two(a), _two(b)
    return _dg(ah, bh, ca, cb) + (_dg(ah, bl, ca, cb) + _dg(al, bh, ca, cb))


@jax.custom_vjp
def _hdot(a, b):
    return _dg3(a, b, 1, 0)


def _hdot_fwd(a, b):
    return _hdot(a, b), (a, b)


def _hdot_bwd(r, ct):
    a, b = r
    return _dg3(ct, b, 1, 1), _dg3(a, ct, 0, 0)


_hdot.defvjp(_hdot_fwd, _hdot_bwd)


@jax.custom_vjp
def _unit_lower_inverse(x):
    c = x.shape[-1]
    eye = (lax.broadcasted_iota(jnp.int32, x.shape, 1) == lax.broadcasted_iota(jnp.int32, x.shape, 2)).astype(F32)
    inv, pw = eye + x, x
    for _ in range(int(math.log2(c)) - 1):
        pw = _dg3(pw, pw, 1, 0)
        inv = inv + _dg3(inv, pw, 1, 0)
    return inv


def _unit_lower_inverse_fwd(x):
    inv = _unit_lower_inverse(x)
    return inv, inv


def _unit_lower_inverse_bwd(inv, ct):
    return (_dg3(_dg3(inv, ct, 0, 0), inv, 1, 1),)


_unit_lower_inverse.defvjp(_unit_lower_inverse_fwd, _unit_lower_inverse_bwd)


@jax.custom_vjp
def _known_inverse(x, inv):
    return inv


def _known_inverse_fwd(x, inv):
    return inv, inv


def _known_inverse_bwd(inv, ct):
    return _dg3(_dg3(inv, ct, 0, 0), inv, 1, 1), jnp.zeros_like(inv)


_known_inverse.defvjp(_known_inverse_fwd, _known_inverse_bwd)


def _split_dot(x, u):
    hi, lo = _two(x)
    return _dg(hi, u, 1, 0) + _dg(lo, u, 1, 0)


@jax.custom_vjp
def _ldot(l01, x):
    hi, lo = _two(x)
    l01 = l01.astype(BF16)
    return _dg(l01, hi, 1, 0) + _dg(l01, lo, 1, 0)


def _ldot_fwd(l01, x):
    return _ldot(l01, x), l01


def _ldot_bwd(l01, ct):
    hi, lo = _two(ct)
    l01b = l01.astype(BF16)
    return jnp.zeros_like(l01), _dg(l01b, hi, 0, 0) + _dg(l01b, lo, 0, 0)


_ldot.defvjp(_ldot_fwd, _ldot_bwd)


SB_BLK = 128
SB_BWD_BLK = 256
SB_KEYS = 512
SB_PAIRS = 2
SB_SCALE = HEAD_DIM ** -0.5


def _log_sigmoid(z):
    return jnp.minimum(z, 0.0) - jnp.log(1.0 + jnp.exp(-jnp.abs(z)))


def _sb_consts(t, blk):
    kb = min(SB_KEYS, t)
    nh = 2 * SB_PAIRS
    lane = lax.broadcasted_iota(jnp.int32, (nh, blk, kb), 2)
    row = lax.broadcasted_iota(jnp.int32, (nh, blk, kb), 1)
    ur = lax.broadcasted_iota(jnp.int32, (kb, kb), 0)
    uc = lax.broadcasted_iota(jnp.int32, (kb, kb), 1)
    return kb, nh, lane, row, ur, uc


def _sb_heads(x):
    head0 = lax.broadcasted_iota(jnp.int32, (x.shape[0], LANES), 1) < HEAD_DIM
    out = []
    for p in range(SB_PAIRS):
        blk = x[:, p * LANES:(p + 1) * LANES]
        out += [jnp.where(head0, blk, 0.0), jnp.where(head0, 0.0, blk)]
    return jnp.stack(out)


def _sb_pairs(x):
    return jnp.stack([x[:, (h // 2) * LANES:(h // 2 + 1) * LANES] for h in range(2 * SB_PAIRS)])


def _sb_merge(x):
    head0 = lax.broadcasted_iota(jnp.int32, (x.shape[1], LANES), 1) < HEAD_DIM
    return jnp.concatenate([jnp.where(head0, x[2 * p], x[2 * p + 1]) for p in range(SB_PAIRS)], axis=1)


def _sb_rows_dot(x, u):
    nh, rows, k = x.shape
    return _split_dot(x.reshape(nh * rows, k), u).reshape(nh, rows, k)


def _sb_fwd(proj, *, name, gather=()):
    t = proj.shape[0]
    nb = t // SB_BLK
    width = SB_PAIRS * LANES
    ng = SB_W // width
    na = len(gather)

    def body(q_ref, k_ref, v_ref, *rest):
        o_ref, r_ref = rest[na:na + 2]
        i = pl.program_id(1)
        if na:
            step = pl.program_id(0) * nb + i
            copies = lambda **kw: _gather_copies(rest[:na], rest[na + 2:2 * na + 2], *rest[2 * na + 2:], **kw)
            pl.when(step == 0)(lambda: _gather_start(copies(only_first=True)))
        kb, nh, lane, row, ur, uc = _sb_consts(t, SB_BLK)
        u_suffix = (ur >= uc).astype(BF16)
        qh = _b(_sb_heads(q_ref[...]) * SB_SCALE)
        diag = (i * SB_BLK) // kb

        def block(j, carry, masked):
            acc, car = carry
            keys = pl.ds(pl.multiple_of(j * kb, kb), kb)
            kj, vj = _b(_sb_pairs(k_ref[keys, :])), _b(_sb_pairs(v_ref[keys, :]))
            z = _dg(qh, kj, 1, 1)
            lk = _log_sigmoid(-z)
            if masked:
                causal = (j * kb + lane) < (i * SB_BLK + row)
                lk = jnp.where(causal, lk, 0.0)
            suf = _sb_rows_dot(lk, u_suffix) + car
            w = jnp.exp(z + suf)
            if masked:
                w = jnp.where(causal, w, 0.0)
            return acc + _dg(_b(w), vj, 1, 0), suf[:, :, 0:1]

        zero = (jnp.zeros((nh, SB_BLK, LANES), F32), jnp.zeros((nh, SB_BLK, 1), F32))
        carry = block(diag, zero, True)
        acc, car = lax.fori_loop(0, diag, lambda s, c: block(diag - 1 - s, c, False), carry)
        o_ref[...] = _sb_merge(acc)
        r_ref[...] = _sb_merge(jnp.broadcast_to(car, (nh, SB_BLK, LANES)))
        if na:
            pl.when(step == ng * nb - 1)(lambda: _gather_finish(copies()))

    return pl.pallas_call(
        body, name=name, grid=(ng, nb),
        in_specs=[pl.BlockSpec((SB_BLK, width), lambda p, i: (i, p)),
                  pl.BlockSpec((t, width), lambda p, i: (0, ng + p)),
                  pl.BlockSpec((t, width), lambda p, i: (0, 2 * ng + p))] + [ANY] * na,
        out_specs=[pl.BlockSpec((SB_BLK, width), lambda p, i: (i, p))] * 2 + [ANY] * na,
        out_shape=[jax.ShapeDtypeStruct((t, SB_W), F32)] * 2
        + [jax.ShapeDtypeStruct((N_CHIPS,) + g.shape, g.dtype) for g in gather],
        scratch_shapes=_gather_scratch(na) if na else [],
        compiler_params=_params("arbitrary", "arbitrary"),
    )(proj, proj, proj, *gather)


def _sb_bwd(proj, rtot, dout, *, name, scatter=()):
    t = proj.shape[0]
    nb = t // SB_BWD_BLK
    width = SB_PAIRS * LANES
    ng = SB_W // width
    na = len(scatter)

    def body(q_ref, k_ref, v_ref, r_ref, do_ref, *rest):
        dq_ref, dk_ref, dv_ref = rest[na:na + 3]
        i = pl.program_id(1)
        if na:
            step = pl.program_id(0) * nb + i
            copies = lambda: _scatter_copies(rest[:na], rest[na + 3:4 * na + 3], *rest[4 * na + 3:])
            pl.when(step == 0)(lambda: [cp.start() for cp in copies()] and None)
        kb, nh, lane, row, ur, uc = _sb_consts(t, SB_BWD_BLK)
        u_incl = (ur <= uc).astype(BF16)
        u_excl = (ur < uc).astype(BF16)
        q, do = q_ref[...], do_ref[...]
        qh, doh = _b(_sb_heads(q) * SB_SCALE), _b(_sb_heads(do))
        qb, dob = _b(_sb_pairs(q) * SB_SCALE), _b(_sb_pairs(do))
        rh = jnp.min(_sb_heads(r_ref[...]), axis=2, keepdims=True)
        diag = (i * SB_BWD_BLK) // kb

        @pl.when(i == 0)
        def _():
            dk_ref[...] = jnp.zeros_like(dk_ref)
            dv_ref[...] = jnp.zeros_like(dv_ref)

        def block(j, carry, masked):
            dq_acc, clk, ce = carry
            keys = pl.ds(pl.multiple_of(j * kb, kb), kb)
            kj, vj = _b(_sb_pairs(k_ref[keys, :])), _b(_sb_pairs(v_ref[keys, :]))
            z = _dg(qh, kj, 1, 1)
            lk = _log_sigmoid(-z)
            ls = z + lk
            if masked:
                causal = (j * kb + lane) < (i * SB_BWD_BLK + row)
                lk = jnp.where(causal, lk, 0.0)
            pre = _sb_rows_dot(lk, u_incl) + clk
            w = jnp.exp(ls + (rh - pre))
            if masked:
                w = jnp.where(causal, w, 0.0)
            e = _dg(doh, vj, 1, 1) * w
            pre_e = _sb_rows_dot(e, u_excl) + ce
            sig = jnp.exp(ls)
            dz = e - sig * (e + pre_e)
            if masked:
                dz = jnp.where(causal, dz, 0.0)
            dzb = _b(dz)
            dk_ref[keys, :] += _sb_merge(_dg(dzb, qb, 0, 0))
            dv_ref[keys, :] += _sb_merge(_dg(_b(w), dob, 0, 0))
            return dq_acc + _dg(dzb, kj, 1, 0), pre[:, :, kb - 1:], pre_e[:, :, kb - 1:] + e[:, :, kb - 1:]

        zero = (jnp.zeros((nh, SB_BWD_BLK, LANES), F32), jnp.zeros((nh, SB_BWD_BLK, 1), F32), jnp.zeros((nh, SB_BWD_BLK, 1), F32))
        carry = lax.fori_loop(0, diag, lambda j, c: block(j, c, False), zero)
        dq_acc, _, _ = block(diag, carry, True)
        dq_ref[...] = _sb_merge(dq_acc) * SB_SCALE
        if na:
            pl.when(step == ng * nb - 1)(lambda: [cp.wait() for cp in copies()] and None)

    blk = pl.BlockSpec((SB_BWD_BLK, width), lambda p, i: (i, p))
    whole = pl.BlockSpec((t, width), lambda p, i: (0, p))
    return pl.pallas_call(
        body, name=name, grid=(ng, nb),
        in_specs=[blk,
                  pl.BlockSpec((t, width), lambda p, i: (0, ng + p)),
                  pl.BlockSpec((t, width), lambda p, i: (0, 2 * ng + p)),
                  blk, blk] + [ANY] * na,
        out_specs=[blk, whole, whole] + [ANY] * (3 * na),
        out_shape=[jax.ShapeDtypeStruct((t, SB_W), F32)] * 3 + _scatter_shapes(scatter),
        scratch_shapes=_dma_sems(3 * na) if na else [],
        compiler_params=_params("arbitrary", "arbitrary"),
    )(proj, proj, proj, rtot, dout, *scatter)


SWA_G = SWA_HEADS // SWA_KV_HEADS


def _swa_heads(first, qs, ks, vs, qg, kg, sinks):
    shape = (SWA_HEADS, WINDOW, 2 * WINDOW)
    qi = lax.broadcasted_iota(jnp.int32, shape, 1)
    kj = lax.broadcasted_iota(jnp.int32, shape, 2)
    dist = qi + WINDOW - kj
    valid = (dist >= 0) & (dist < WINDOW) & (jnp.logical_not(first) | (kj >= WINDOW))
    head = lax.broadcasted_iota(jnp.int32, (SWA_HEADS, 1, 1), 0)
    slope = sum(jnp.where(head == h, 2.0 ** (-8.0 * (h + 1) / SWA_HEADS), 0.0) for h in range(SWA_HEADS))
    kn = _rms(ks, kg)
    per_q_head = lambda x: jnp.concatenate([x[h // SWA_G:h // SWA_G + 1] for h in range(SWA_HEADS)], axis=0)
    k8, v8 = per_q_head(kn), per_q_head(vs)
    s = _bdot_nt(_rms(qs, qg), k8) * (HEAD_DIM ** -0.5)
    s = jnp.where(valid, s - slope * dist.astype(F32), NEG)
    m = lax.stop_gradient(jnp.maximum(jnp.max(s, axis=2, keepdims=True), sinks))
    p = jnp.exp(s - m)
    den = jnp.sum(p, axis=2, keepdims=True) + jnp.exp(sinks - m)
    return _bdot(p / den, v8)


def _swa_split(q, kp, kc, vp, vc, sk):
    lanes = lambda x, n: jnp.stack([x[:, h * HEAD_DIM:(h + 1) * HEAD_DIM].astype(F32) for h in range(n)])
    k2, v2 = jnp.concatenate([kp, kc], axis=0), jnp.concatenate([vp, vc], axis=0)
    sinks = jnp.stack([sk[:, h:h + 1] for h in range(SWA_HEADS)])
    return lanes(q, SWA_HEADS), lanes(k2, SWA_KV_HEADS), lanes(v2, SWA_KV_HEADS), sinks


def _swa_join(x):
    return jnp.concatenate([x[h] for h in range(x.shape[0])], axis=1)


def _swa_specs(t):
    qcb = (3 * SB_W) // SWA_QW
    kcb = (3 * SB_W + SWA_QW) // SWA_KVW
    prev = lambda i: jnp.maximum(i - 1, 0)
    return [pl.BlockSpec((WINDOW, SWA_QW), lambda i: (i, qcb)),
            pl.BlockSpec((WINDOW, SWA_KVW), lambda i: (prev(i), kcb)),
            pl.BlockSpec((WINDOW, SWA_KVW), lambda i: (i, kcb)),
            pl.BlockSpec((WINDOW, SWA_KVW), lambda i: (prev(i), kcb + 1)),
            pl.BlockSpec((WINDOW, SWA_KVW), lambda i: (i, kcb + 1)),
            pl.BlockSpec((1, HEAD_DIM), lambda i: (0, 0)),
            pl.BlockSpec((1, HEAD_DIM), lambda i: (0, 0)),
            pl.BlockSpec((1, SWA_HEADS), lambda i: (0, 0))]


def _swa_fwd(proj, qg, kg, sinks, *, name):
    t = proj.shape[0]

    def body(q_ref, kp_ref, kc_ref, vp_ref, vc_ref, qg_ref, kg_ref, sk_ref, o_ref):
        first = pl.program_id(0) == 0
        qs, ks, vs, sk = _swa_split(q_ref[...], kp_ref[...], kc_ref[...], vp_ref[...], vc_ref[...], sk_ref[...])
        o_ref[...] = _swa_join(_swa_heads(first, qs, ks, vs, qg_ref[...], kg_ref[...], sk))

    return pl.pallas_call(
        body, name=name, grid=(t // WINDOW,), in_specs=_swa_specs(t),
        out_specs=pl.BlockSpec((WINDOW, SWA_QW), lambda i: (i, 0)),
        out_shape=jax.ShapeDtypeStruct((t, SWA_QW), F32),
        compiler_params=_params("parallel"),
    )(proj, proj, proj, proj, proj, qg, kg, sinks)


def _swa_bwd(proj, qg, kg, sinks, dout, *, name):
    t = proj.shape[0]

    def body(q_ref, kp_ref, kc_ref, vp_ref, vc_ref, qg_ref, kg_ref, sk_ref, do_ref,
             dq_ref, dk_ref, dv_ref, dqg_ref, dkg_ref, dsk_ref):
        i = pl.program_id(0)
        first = i == 0

        @pl.when(first)
        def _():
            for r in (dk_ref, dv_ref, dqg_ref, dkg_ref, dsk_ref):
                r[...] = jnp.zeros_like(r)

        qs, ks, vs, sk = _swa_split(q_ref[...], kp_ref[...], kc_ref[...], vp_ref[...], vc_ref[...], sk_ref[...])
        do = do_ref[...]
        cts = jnp.stack([do[:, h * HEAD_DIM:(h + 1) * HEAD_DIM] for h in range(SWA_HEADS)])
        _, vjp = jax.vjp(functools.partial(_swa_heads, first), qs, ks, vs, qg_ref[...], kg_ref[...], sk)
        dqs, dks, dvs, dqg, dkg, dsk = vjp(cts)
        dq_ref[...] = _swa_join(dqs)
        dk2, dv2 = _swa_join(dks), _swa_join(dvs)
        cur = pl.ds(pl.multiple_of(i * WINDOW, WINDOW), WINDOW)
        prv = pl.ds(pl.multiple_of(jnp.maximum(i - 1, 0) * WINDOW, WINDOW), WINDOW)
        dk_ref[prv, :] += dk2[:WINDOW]
        dv_ref[prv, :] += dv2[:WINDOW]
        dk_ref[cur, :] += dk2[WINDOW:]
        dv_ref[cur, :] += dv2[WINDOW:]
        dqg_ref[...] += dqg
        dkg_ref[...] += dkg
        dsk_ref[...] += _swa_join(dsk)

    whole = lambda shape: pl.BlockSpec(shape, lambda i: (0, 0))
    return pl.pallas_call(
        body, name=name, grid=(t // WINDOW,),
        in_specs=_swa_specs(t) + [pl.BlockSpec((WINDOW, SWA_QW), lambda i: (i, 0))],
        out_specs=[pl.BlockSpec((WINDOW, SWA_QW), lambda i: (i, 0)), whole((t, SWA_KVW)), whole((t, SWA_KVW)),
                   whole((1, HEAD_DIM)), whole((1, HEAD_DIM)), whole((1, SWA_HEADS))],
        out_shape=[jax.ShapeDtypeStruct((t, SWA_QW), F32), jax.ShapeDtypeStruct((t, SWA_KVW), F32),
                   jax.ShapeDtypeStruct((t, SWA_KVW), F32), jax.ShapeDtypeStruct((1, HEAD_DIM), F32),
                   jax.ShapeDtypeStruct((1, HEAD_DIM), F32), jax.ShapeDtypeStruct((1, SWA_HEADS), F32)],
        compiler_params=_params("arbitrary"),
    )(proj, proj, proj, proj, proj, qg, kg, sinks, dout)


CONV_CB = 512
CONV_TM = 512
HALO = 8


def _conv_pre(x_ref, h_ref, w_ref, i):
    halo = jnp.where(i > 0, h_ref[...], 0.0)
    xe = jnp.concatenate([halo, x_ref[...]], axis=0)
    tm = x_ref.shape[0]
    w = w_ref[...]
    c = sum(w[k:k + 1, :] * xe[HALO - (GDN_CONV - 1) + k:HALO - (GDN_CONV - 1) + k + tm] for k in range(GDN_CONV))
    return c, xe


def _conv_specs(tm, cb):
    return [pl.BlockSpec((tm, cb), lambda c, i: (i, c)),
            pl.BlockSpec((HALO, cb), lambda c, i: (jnp.maximum(i * (tm // HALO) - 1, 0), c)),
            pl.BlockSpec((GDN_CONV, cb), lambda c, i: (0, c))]


def _conv_fwd(x, w, *, name):
    t, ch = x.shape
    tm, cb = _tile(t, CONV_TM), _tile(ch, CONV_CB)

    def body(x_ref, h_ref, w_ref, o_ref):
        c, _ = _conv_pre(x_ref, h_ref, w_ref, pl.program_id(1))
        o_ref[...] = c * jax.nn.sigmoid(c)

    tile = pl.BlockSpec((tm, cb), lambda c, i: (i, c))
    return pl.pallas_call(
        body, name=name, grid=(ch // cb, t // tm), in_specs=_conv_specs(tm, cb), out_specs=tile,
        out_shape=jax.ShapeDtypeStruct((t, ch), F32),
        compiler_params=_params("parallel", "parallel"),
    )(x, x, w)


def _conv_bwd(x, w, dact, *, name):
    t, ch = x.shape
    tm, cb = _tile(t, CONV_TM), _tile(ch, CONV_CB)
    nt = t // tm
    last = GDN_CONV - 1

    def body(x_ref, h_ref, w_ref, xn_ref, da_ref, dan_ref, dx_ref, dw_ref):
        i = pl.program_id(1)

        @pl.when(i == 0)
        def _():
            dw_ref[...] = jnp.zeros_like(dw_ref)

        w = w_ref[...]
        xe = jnp.concatenate([jnp.where(i > 0, h_ref[...], 0.0), x_ref[...], xn_ref[...]], axis=0)
        rows = tm + HALO
        c = sum(w[k:k + 1, :] * xe[HALO - last + k:HALO - last + k + rows] for k in range(GDN_CONV))
        sig = jax.nn.sigmoid(c)
        da = jnp.concatenate([da_ref[...], jnp.where(i < nt - 1, dan_ref[...], 0.0)], axis=0)
        dce = da * (sig * (1.0 + c * (1.0 - sig)))
        dc = dce[:tm]
        dx_ref[...] = sum(w[k:k + 1, :] * dce[last - k:last - k + tm] for k in range(GDN_CONV))
        dw_ref[...] += jnp.concatenate(
            [jnp.sum(dc * xe[HALO - last + k:HALO - last + k + tm], axis=0, keepdims=True) for k in range(GDN_CONV)],
            axis=0)

    tile = pl.BlockSpec((tm, cb), lambda c, i: (i, c))
    nxt = pl.BlockSpec((HALO, cb), lambda c, i: (jnp.minimum((i + 1) * (tm // HALO), t // HALO - 1), c))
    return pl.pallas_call(
        body, name=name, grid=(ch // cb, nt),
        in_specs=_conv_specs(tm, cb) + [nxt, tile, nxt],
        out_specs=[tile, pl.BlockSpec((GDN_CONV, cb), lambda c, i: (0, c))],
        out_shape=[jax.ShapeDtypeStruct((t, ch), F32), jax.ShapeDtypeStruct((GDN_CONV, ch), F32)],
        compiler_params=_params("parallel", "arbitrary"),
    )(x, x, w, x, dact, dact)


def _gdn_chunk(qraw, kraw, v, bl, a, alog, dtb, state, inverse=None, keep_inverse=False):
    c, d = GDN_CHUNK, GDN_HEAD_DIM
    nh = qraw.shape[0]
    ri = lax.broadcasted_iota(jnp.int32, (nh, c, c), 1)
    ci = lax.broadcasted_iota(jnp.int32, (nh, c, c), 2)
    incl, strict = ri >= ci, ri > ci
    q = qraw * lax.rsqrt(jnp.sum(qraw * qraw, axis=-1, keepdims=True) + EPS) * (d ** -0.5)
    k = kraw * lax.rsqrt(jnp.sum(kraw * kraw, axis=-1, keepdims=True) + EPS)
    beta = jax.nn.sigmoid(bl)
    g = -jnp.exp(alog) * jax.nn.softplus(a + dtb)
    gc = _ldot(incl.astype(F32), jnp.broadcast_to(g, (nh, c, d)))
    gcm = gc[:, :, :c]
    decay = jnp.exp(jnp.where(incl, gcm - jnp.swapaxes(gcm, 1, 2), NEG))
    eg = jnp.exp(gc)
    kbeta = k * beta
    x = -jnp.where(strict, _bdot_nt(kbeta, k) * decay, 0.0)
    tinv = _unit_lower_inverse(x) if inverse is None else _known_inverse(x, inverse)
    u = _hdot(tinv, v * beta)
    w = _hdot(tinv, kbeta * eg)
    attn = jnp.where(incl, _bdot_nt(q, k) * decay, 0.0)
    glast = gc[:, c - 1:c, :]
    v_new = u - _bdot(w, state)
    o = _bdot(q * eg, state) + _bdot(attn, v_new)
    state = state * jnp.exp(glast) + _bdot_tn(k * jnp.exp(glast - gc), v_new)
    return (o, state, tinv) if keep_inverse else (o, state)


GDN_REP = GDN_V_HEADS // GDN_K_HEADS
GDN_HB = 8


def _gdn_pick(vals, kh, r):
    ba, alog, dtb = vals
    lane = lax.broadcasted_iota(jnp.int32, ba.shape, 1)
    hv = kh * GDN_REP + r
    bl = jnp.sum(jnp.where(lane == hv, ba, 0.0), axis=1, keepdims=True)
    a = jnp.sum(jnp.where(lane == GDN_V_HEADS + hv, ba, 0.0), axis=1, keepdims=True)
    lane1 = lax.broadcasted_iota(jnp.int32, alog.shape, 1)
    al = jnp.sum(jnp.where(lane1 == hv, alog, 0.0), axis=1, keepdims=True)
    db = jnp.sum(jnp.where(lane1 == hv, dtb, 0.0), axis=1, keepdims=True)
    return bl, a, al, db


def _gdn_stack(qs, ks, vs, small, j):
    d = GDN_HEAD_DIM
    per = [[], [], [], [], [], [], []]
    for hh in range(GDN_HB):
        q, k = qs[:, hh * d:(hh + 1) * d], ks[:, hh * d:(hh + 1) * d]
        for r in range(GDN_REP):
            col = (hh * GDN_REP + r) * d
            for lst, val in zip(per, (q, k, vs[:, col:col + d]) + _gdn_pick(small, j * GDN_HB + hh, r)):
                lst.append(val)
    return tuple(jnp.stack(lst) for lst in per)


def _gdn_specs(nchunk, rev):
    c, d = GDN_CHUNK, GDN_HEAD_DIM
    at = (lambda n: nchunk - 1 - n) if rev else (lambda n: n)
    ng = GDN_K_HEADS // GDN_HB
    return at, [pl.BlockSpec((c, GDN_HB * d), lambda n, j: (at(n), j)),
                pl.BlockSpec((c, GDN_HB * d), lambda n, j: (at(n), ng + j)),
                pl.BlockSpec((c, GDN_HB * GDN_REP * d), lambda n, j: (at(n), ng + j)),
                pl.BlockSpec((c, 2 * GDN_V_HEADS), lambda n, j: (at(n), 0)),
                pl.BlockSpec((1, GDN_V_HEADS), lambda n, j: (0, 0)),
                pl.BlockSpec((1, GDN_V_HEADS), lambda n, j: (0, 0))]


def _gdn_fwd(act, ba, alog, dtb, *, name):
    t = act.shape[0]
    c, d = GDN_CHUNK, GDN_HEAD_DIM
    nchunk = t // c
    at, specs = _gdn_specs(nchunk, False)

    def body(q_ref, k_ref, v_ref, ba_ref, al_ref, db_ref, o_ref, s_ref, inv_ref, state):
        n, j = pl.program_id(0), pl.program_id(1)
        heads = pl.ds(j * GDN_HB, GDN_HB)

        @pl.when(n == 0)
        def _():
            state[heads] = jnp.zeros((GDN_HB, GDN_REP, d, d), F32)

        s_in = state[heads]
        s_ref[...] = s_in
        args = _gdn_stack(q_ref[...], k_ref[...], v_ref[...], (ba_ref[...], al_ref[...], db_ref[...]), j)
        o, s_new, inv_ref[...] = _gdn_chunk(*args, s_in.reshape(GDN_HB * GDN_REP, d, d), keep_inverse=True)
        o_ref[...] = jnp.concatenate([o[b] for b in range(GDN_HB * GDN_REP)], axis=1)
        state[heads] = s_new.reshape(GDN_HB, GDN_REP, d, d)

    return pl.pallas_call(
        body, name=name, grid=(nchunk, GDN_K_HEADS // GDN_HB), in_specs=specs,
        out_specs=[pl.BlockSpec((c, GDN_HB * GDN_REP * d), lambda n, j: (n, j)),
                   pl.BlockSpec((None, GDN_HB, GDN_REP, d, d), lambda n, j: (n, j, 0, 0, 0)),
                   pl.BlockSpec((None, GDN_HB * GDN_REP, c, c), lambda n, j: (n, j, 0, 0))],
        out_shape=[jax.ShapeDtypeStruct((t, GDN_VW), F32),
                   jax.ShapeDtypeStruct((nchunk, GDN_K_HEADS, GDN_REP, d, d), F32),
                   jax.ShapeDtypeStruct((nchunk, GDN_V_HEADS, c, c), F32)],
        scratch_shapes=[pltpu.VMEM((GDN_K_HEADS, GDN_REP, d, d), F32)],
        compiler_params=_params("arbitrary", "arbitrary"),
    )(act, act, act, ba, alog, dtb)


def _gdn_bwd(act, ba, alog, dtb, states, inverses, dout, *, name):
    assert GDN_HB == GDN_K_HEADS
    t = act.shape[0]
    c, d = GDN_CHUNK, GDN_HEAD_DIM
    nchunk = t // c
    at, specs = _gdn_specs(nchunk, True)

    def body(q_ref, k_ref, v_ref, ba_ref, al_ref, db_ref, s_ref, inv_ref, do_ref,
             dact_ref, dba_ref, dal_ref, ddb_ref, dstate):
        n, j = pl.program_id(0), pl.program_id(1)

        @pl.when(n == 0)
        def _():
            dstate[pl.ds(j * GDN_HB, GDN_HB)] = jnp.zeros((GDN_HB, GDN_REP, d, d), F32)

        @pl.when((n == 0) & (j == 0))
        def _():
            dal_ref[...] = jnp.zeros_like(dal_ref)
            ddb_ref[...] = jnp.zeros_like(ddb_ref)

        @pl.when(j == 0)
        def _():
            dba_ref[...] = jnp.zeros_like(dba_ref)

        heads = pl.ds(j * GDN_HB, GDN_HB)
        nh = GDN_HB * GDN_REP
        args = _gdn_stack(q_ref[...], k_ref[...], v_ref[...], (ba_ref[...], al_ref[...], db_ref[...]), j)
        _, vjp = jax.vjp(functools.partial(_gdn_chunk, inverse=inv_ref[...]), *args, s_ref[...].reshape(nh, d, d))
        do = do_ref[...]
        do = jnp.stack([do[:, b * d:(b + 1) * d] for b in range(nh)])
        gq, gk, gv, gbl, ga, gal, gdb, gs = vjp((do, dstate[heads].reshape(nh, d, d)))
        dstate[heads] = gs.reshape(GDN_HB, GDN_REP, d, d)
        dact_ref[...] = jnp.concatenate([gq[GDN_REP * hh] + gq[GDN_REP * hh + 1] for hh in range(GDN_HB)]
                                        + [gk[GDN_REP * hh] + gk[GDN_REP * hh + 1] for hh in range(GDN_HB)]
                                        + [gv[b] for b in range(nh)], axis=1)
        lane = lax.broadcasted_iota(jnp.int32, (c, 2 * GDN_V_HEADS), 1)
        lane1 = lax.broadcasted_iota(jnp.int32, (1, GDN_V_HEADS), 1)
        dba = jnp.zeros((c, 2 * GDN_V_HEADS), F32)
        dal = jnp.zeros((1, GDN_V_HEADS), F32)
        ddb = jnp.zeros((1, GDN_V_HEADS), F32)
        for b in range(nh):
            hv = j * nh + b
            dba = dba + jnp.where(lane == hv, gbl[b], 0.0) + jnp.where(lane == GDN_V_HEADS + hv, ga[b], 0.0)
            dal = dal + jnp.where(lane1 == hv, gal[b], 0.0)
            ddb = ddb + jnp.where(lane1 == hv, gdb[b], 0.0)
        dba_ref[...] += dba
        dal_ref[...] += dal
        ddb_ref[...] += ddb

    small = pl.BlockSpec((1, GDN_V_HEADS), lambda n, j: (0, 0))
    return pl.pallas_call(
        body, name=name, grid=(nchunk, GDN_K_HEADS // GDN_HB),
        in_specs=specs + [pl.BlockSpec((None, GDN_HB, GDN_REP, d, d), lambda n, j: (at(n), j, 0, 0, 0)),
                          pl.BlockSpec((None, GDN_HB * GDN_REP, c, c), lambda n, j: (at(n), j, 0, 0)),
                          pl.BlockSpec((c, GDN_HB * GDN_REP * d), lambda n, j: (at(n), j))],
        out_specs=[pl.BlockSpec((c, GDN_CONV_W), lambda n, j: (at(n), 0)),
                   pl.BlockSpec((c, 2 * GDN_V_HEADS), lambda n, j: (at(n), 0)),
                   small, small],
        out_shape=[jax.ShapeDtypeStruct((t, GDN_CONV_W), F32), jax.ShapeDtypeStruct((t, 2 * GDN_V_HEADS), F32),
                   jax.ShapeDtypeStruct((1, GDN_V_HEADS), F32), jax.ShapeDtypeStruct((1, GDN_V_HEADS), F32)],
        scratch_shapes=[pltpu.VMEM((GDN_K_HEADS, GDN_REP, d, d), F32)],
        compiler_params=_params("arbitrary", "arbitrary"),
    )(act, act, act, ba, alog, dtb, states, inverses, dout)


N_DEV = 8
ANY = pl.BlockSpec(memory_space=pl.ANY)


def _coords():
    return lax.axis_index("x"), lax.axis_index("y"), lax.axis_index("c")


def _other_chips(x, y):
    return [(1 - x, y), (x, 1 - y), (1 - x, 1 - y)]


def _remote(src, dst, send_sems, recv_sems, k, to):
    return pltpu.make_async_remote_copy(src_ref=src, dst_ref=dst, send_sem=send_sems.at[k], recv_sem=recv_sems.at[k],
                                        device_id=to, device_id_type=MESH)


def _dma_sems(n):
    return [pltpu.SemaphoreType.DMA((n,)), pltpu.SemaphoreType.DMA((n,))]


def _gather_copies(ins, outs, send_sems, recv_sems, local_sems, only_first=False):
    x, y, c = _coords()
    sibling = (x, y, 1 - c)
    local, sends, arrivals, relays, relayed = [], [], [], [], []
    for a, (x_ref, out_ref) in enumerate(zip(ins, outs)):
        local.append(pltpu.make_async_copy(x_ref, out_ref.at[2 * x + y], local_sems.at[a]))
        for j, (cx, cy) in enumerate(_other_chips(x, y)):
            k, theirs = 6 * a + j, 2 * cx + cy
            sends.append(_remote(x_ref.at[c], out_ref.at[2 * x + y, c], send_sems, recv_sems, k, (cx, cy, c)))
            if only_first:
                continue
            arrivals.append(_remote(x_ref.at[c], out_ref.at[theirs, c], send_sems, recv_sems, k, (cx, cy, c)))
            relays.append(_remote(out_ref.at[theirs, c], out_ref.at[theirs, c], send_sems, recv_sems, k + 3, sibling))
            relayed.append(_remote(x_ref.at[c], out_ref.at[theirs, 1 - c], send_sems, recv_sems, k + 3, sibling))
    return local, sends, arrivals, relays, relayed


def _gather_start(copies):
    local, sends, _, _, _ = copies
    for cp in local + sends:
        cp.start()


def _gather_finish(copies):
    local, sends, arrivals, relays, relayed = copies
    for landed, relay in zip(arrivals, relays):
        landed.wait_recv()
        relay.start()
    for cp in relayed:
        cp.wait_recv()
    for cp in sends + relays:
        cp.wait_send()
    for cp in local:
        cp.wait()


def _gather_scratch(na):
    return _dma_sems(6 * na) + [pltpu.SemaphoreType.DMA((na,))]


def _gather_quarters(parts, *, name):
    na = len(parts)

    def body(*refs):
        copies = _gather_copies(refs[:na], refs[na:2 * na], *refs[2 * na:])
        _gather_start(copies)
        _gather_finish(copies)

    return pl.pallas_call(
        body, name=name, in_specs=[ANY] * na, out_specs=[ANY] * na,
        out_shape=[jax.ShapeDtypeStruct((N_CHIPS,) + p.shape, p.dtype) for p in parts],
        scratch_shapes=_gather_scratch(na),
    )(*parts)


def _swap_halves(grads, *, name):
    na = len(grads)

    def body(*refs):
        ins, outs = refs[:na], refs[na:2 * na]
        send_sems, recv_sems = refs[2 * na:]
        x, y, c = _coords()
        sends = [_remote(g_ref.at[j, 1 - c], o_ref.at[j], send_sems, recv_sems, N_CHIPS * a + j, (x, y, 1 - c))
                 for a, (g_ref, o_ref) in enumerate(zip(ins, outs)) for j in range(N_CHIPS)]
        for cp in sends:
            cp.start()
        for cp in sends:
            cp.wait()

    return pl.pallas_call(
        body, name=name, in_specs=[ANY] * na, out_specs=[ANY] * na,
        out_shape=[jax.ShapeDtypeStruct((N_CHIPS,) + g.shape[2:], g.dtype) for g in grads],
        scratch_shapes=_dma_sems(N_CHIPS * na),
    )(*grads)


def _scatter_copies(ins, outs, send_sems, recv_sems):
    x, y, c = _coords()
    return [_remote(p_ref.at[2 * cx + cy], outs[3 * a + j], send_sems, recv_sems, 3 * a + j, (cx, cy, c))
            for a, p_ref in enumerate(ins) for j, (cx, cy) in enumerate(_other_chips(x, y))]


def _scatter_shapes(pairs):
    return [jax.ShapeDtypeStruct(p.shape[1:], p.dtype) for p in pairs for _ in range(3)]


def _scatter_quarters(pairs, *, name):
    na = len(pairs)

    def body(*refs):
        sends = _scatter_copies(refs[:na], refs[na:4 * na], *refs[4 * na:])
        for cp in sends:
            cp.start()
        for cp in sends:
            cp.wait()

    out = pl.pallas_call(
        body, name=name, in_specs=[ANY] * na, out_specs=[ANY] * (3 * na), out_shape=_scatter_shapes(pairs),
        scratch_shapes=_dma_sems(3 * na),
    )(*pairs)
    return [out[3 * a:3 * a + 3] for a in range(na)]


def _share_halves(tots, *, name):
    na = len(tots)

    def body(*refs):
        ins, outs = refs[:na], refs[na:2 * na]
        send_sems, recv_sems = refs[2 * na:]
        x, y, c = _coords()
        sends = [_remote(t_ref, o_ref, send_sems, recv_sems, a, (x, y, 1 - c))
                 for a, (t_ref, o_ref) in enumerate(zip(ins, outs))]
        for cp in sends:
            cp.start()
        for cp in sends:
            cp.wait()

    return pl.pallas_call(
        body, name=name, in_specs=[ANY] * na, out_specs=[ANY] * na,
        out_shape=[jax.ShapeDtypeStruct(t.shape, t.dtype) for t in tots],
        scratch_shapes=_dma_sems(na),
    )(*tots)


def _gather_all(vec, *, name):
    m, w = vec.shape

    def body(x_ref, out_ref, send_sems, recv_sems, local_sem):
        x, y, c = _coords()
        me, sibling = (x, y, c), (x, y, 1 - c)
        chips = _other_chips(x, y)

        def rows(px, py, pc):
            return out_ref.at[pl.ds((4 * px + 2 * py + pc) * m, m), :]

        def copy(k, block, to, src=None):
            return _remote(rows(*block) if src is None else src, rows(*block), send_sems, recv_sems, k, to)

        mine = pltpu.make_async_copy(x_ref, rows(*me), local_sem)
        mine.start()
        first = [copy(0, me, sibling, src=x_ref)]
        first += [copy(1 + j, me, (*chip, c), src=x_ref) for j, chip in enumerate(chips)]
        for cp in first:
            cp.start()
        passed = [copy(4 + j, (*chip, c), sibling) for j, chip in enumerate(chips)]
        for j, chip in enumerate(chips):
            copy(1 + j, (*chip, c), me).wait_recv()
            passed[j].start()
        copy(0, sibling, me).wait_recv()
        for j, chip in enumerate(chips):
            copy(4 + j, (*chip, 1 - c), me).wait_recv()
        for cp in first + passed:
            cp.wait_send()
        mine.wait()

    vm = pl.BlockSpec(memory_space=pltpu.VMEM)
    return pl.pallas_call(
        body, name=name, in_specs=[vm], out_specs=vm, out_shape=jax.ShapeDtypeStruct((N_DEV * m, w), vec.dtype),
        scratch_shapes=_dma_sems(7) + [pltpu.SemaphoreType.DMA(())],
    )(vec)


def _sum_blocks(allv, n, *, name):
    m = allv.shape[0] // n

    def body(a_ref, o_ref):
        acc = a_ref[0:m, :]
        for d in range(1, n):
            acc = acc + a_ref[d * m:(d + 1) * m, :]
        o_ref[...] = acc

    return pl.pallas_call(body, name=name, out_shape=jax.ShapeDtypeStruct((m, allv.shape[1]), allv.dtype))(allv)


EW_BLOCK_BYTES = 1 << 20


def _ew_rows(rows, w):
    return _tile(rows, max(8, (EW_BLOCK_BYTES // (4 * w)) // 8 * 8), 8)


def _add_pair(g, got, c, *, name):
    _, _, rows, w = g.shape
    tr = _ew_rows(rows, w)

    def body(c_ref, g_ref, got_ref, o_ref):
        o_ref[...] = (g_ref[...] + got_ref[...]).astype(o_ref.dtype)

    blk = pl.BlockSpec((None, tr, w), lambda q, i, c_ref: (q, i, 0))
    return pl.pallas_call(
        body, name=name,
        grid_spec=pltpu.PrefetchScalarGridSpec(
            num_scalar_prefetch=1, grid=(N_CHIPS, rows // tr),
            in_specs=[pl.BlockSpec((None, None, tr, w), lambda q, i, c_ref: (q, c_ref[0], i, 0)), blk], out_specs=blk),
        out_shape=jax.ShapeDtypeStruct(got.shape, BF16),
        compiler_params=_params("parallel", "parallel"),
    )(c, g, got)


def _add_chips(pair, recv, chip, *, name):
    _, rows, w = pair.shape
    tr = _ew_rows(rows, w)

    def body(chip_ref, p_ref, r0_ref, r1_ref, r2_ref, o_ref):
        f = lambda r: r[...].astype(F32)
        o_ref[...] = ((f(p_ref) + f(r0_ref)) + f(r1_ref)) + f(r2_ref)

    blk = pl.BlockSpec((tr, w), lambda i, chip_ref: (i, 0))
    return pl.pallas_call(
        body, name=name,
        grid_spec=pltpu.PrefetchScalarGridSpec(
            num_scalar_prefetch=1, grid=(rows // tr,),
            in_specs=[pl.BlockSpec((None, tr, w), lambda i, chip_ref: (chip_ref[0], i, 0)), blk, blk, blk], out_specs=blk),
        out_shape=jax.ShapeDtypeStruct((rows, w), F32),
        compiler_params=_params("parallel"),
    )(chip, pair, *recv)


def _adamw_math(w, g, m, v):
    nm = ADAM_B1 * m + (1.0 - ADAM_B1) * g
    nv = ADAM_B2 * v + (1.0 - ADAM_B2) * (g * g)
    m_hat = nm / (1.0 - ADAM_B1 ** ADAM_STEP)
    v_hat = nv / (1.0 - ADAM_B2 ** ADAM_STEP)
    return -ADAM_LR * (m_hat / (jnp.sqrt(v_hat) + ADAM_EPS) + ADAM_WD * w), nm, nv


def _adamw(w, g, m, v, *, name):
    shape = w.shape
    last = shape[-1]
    w2, g2, m2, v2 = (a.reshape(-1, last) for a in (w, g, m, v))
    rows = w2.shape[0]
    tm = _ew_rows(rows, last)

    def body(w_ref, g_ref, m_ref, v_ref, d_ref, nm_ref, nv_ref):
        d_ref[...], nm_ref[...], nv_ref[...] = _adamw_math(w_ref[...], g_ref[...], m_ref[...], v_ref[...])

    spec = pl.BlockSpec((tm, last), lambda i: (i, 0))
    out = jax.ShapeDtypeStruct((rows, last), F32)
    d, nm, nv = pl.pallas_call(
        body, name=name, grid=(rows // tm,), in_specs=[spec] * 4, out_specs=[spec] * 3, out_shape=[out] * 3,
        compiler_params=_params("parallel"),
    )(w2, g2, m2, v2)
    return d.reshape(shape), nm.reshape(shape), nv.reshape(shape)


def _adamw_halves(w, m, v, mine, theirs, c, *, name, into=None):
    rows, wd = w.shape[-2:]
    tr = _ew_rows(rows, wd)
    bufs, at = into if into is not None else ((), ())

    def body(c_ref, w_ref, m_ref, v_ref, a_ref, b_ref, *rest):
        g_ref, d_ref, nm_ref, nv_ref = rest[len(bufs):]
        g = jnp.where(pl.program_id(0) == c_ref[0], a_ref[...], b_ref[...])
        g_ref[...] = g
        d_ref[...], nm_ref[...], nv_ref[...] = _adamw_math(w_ref[...], g, m_ref[...], v_ref[...])

    full = pl.BlockSpec((None,) * (1 + len(at)) + (tr, wd), lambda hf, i, c_ref: at + (hf, i, 0))
    half = pl.BlockSpec((tr, wd), lambda hf, i, c_ref: (i, 0))
    out = jax.ShapeDtypeStruct(w.shape, F32)
    return pl.pallas_call(
        body, name=name,
        grid_spec=pltpu.PrefetchScalarGridSpec(num_scalar_prefetch=1, grid=(2, rows // tr),
                                               in_specs=[full] * 3 + [half] * 2 + [ANY] * len(bufs),
                                               out_specs=[full] * 4),
        out_shape=[out] * 4, input_output_aliases={6 + b: b for b in range(len(bufs))},
        compiler_params=_params("parallel", "parallel"),
    )(c, w, m, v, mine, theirs, *bufs)


def _join_quarters(q, *, name):
    _, rows, n = q.shape
    tr = _tile(rows, 256, 16)

    def body(q_ref, o_ref):
        o_ref[...] = jnp.concatenate([q_ref[s] for s in range(N_CHIPS)], axis=1)

    return pl.pallas_call(
        body, name=name, grid=(rows // tr,),
        in_specs=[pl.BlockSpec((N_CHIPS, tr, n), lambda i: (0, i, 0))],
        out_specs=pl.BlockSpec((tr, N_CHIPS * n), lambda i: (i, 0)),
        out_shape=jax.ShapeDtypeStruct((rows, N_CHIPS * n), q.dtype),
        compiler_params=_params("parallel"),
    )(q)


def _split_quarters(pieces, *, name):
    rows = pieces[0].shape[0]
    n = sum(p.shape[1] for p in pieces) // N_CHIPS
    tr = _tile(rows, 256, 16)

    def body(*refs):
        x = jnp.concatenate([r[...] for r in refs[:-1]], axis=1)
        for s in range(N_CHIPS):
            refs[-1][s] = x[:, s * n:(s + 1) * n]

    return pl.pallas_call(
        body, name=name, grid=(rows // tr,),
        in_specs=[pl.BlockSpec((tr, p.shape[1]), lambda i: (i, 0)) for p in pieces],
        out_specs=pl.BlockSpec((N_CHIPS, tr, n), lambda i: (0, i, 0)),
        out_shape=jax.ShapeDtypeStruct((N_CHIPS, rows, n), pieces[0].dtype),
        compiler_params=_params("parallel"),
    )(*pieces)


_WEIGHTS = ['ffn_norm', 'ffn_w_gate', 'ffn_w_up', 'ffn_w_down', 'mix_norm', 'att_w_in', 'att_q_norm', 'att_k_norm',
            'att_sinks', 'att_w_out', 'gdn_w_in', 'gdn_conv_w', 'gdn_a_log', 'gdn_dt_bias', 'gdn_out_norm', 'gdn_w_out',
            'ple_norm', 'ple_w_gate', 'ple_w_proj']
_BIG = ['ffn_w_gate', 'ffn_w_up', 'ffn_w_down', 'att_w_in', 'att_w_out', 'gdn_w_in', 'gdn_w_out', 'ple_w_gate',
        'ple_w_proj']
_SMALL_CUT = {'ffn_norm': 2, 'gdn_conv_w': 2}
_WHOLE = ['mix_norm', 'att_q_norm', 'att_k_norm', 'att_sinks', 'gdn_a_log', 'gdn_dt_bias', 'gdn_out_norm', 'ple_norm']
PACK_W = 1024
SMALL_ROW_MULT = 8


def _halves(a):
    return a.reshape(2, -1, a.shape[-1])


def _from_quarters(blk, axis):
    full = jnp.moveaxis(blk, 0, axis)
    shp = list(full.shape)
    shp[axis:axis + 2] = [shp[axis] * shp[axis + 1]]
    return full.reshape(shp)


def _to_quarters(full, axis):
    shp = list(full.shape)
    shp[axis:axis + 1] = [N_CHIPS, shp[axis] // N_CHIPS]
    return jnp.moveaxis(full.reshape(shp), axis, 0)


def _pack(parts, row_mult):
    flat = jnp.concatenate(parts, axis=-1)
    n = flat.shape[-1]
    rows = -(-n // (PACK_W * row_mult)) * row_mult
    return jnp.pad(flat, [(0, rows * PACK_W - n)]).reshape(rows, PACK_W)


def _unpack(flat, shapes):
    lead = flat.shape[:-2]
    flat = flat.reshape(lead + (-1,))
    out, off = [], 0
    for shp in shapes:
        n = math.prod(shp)
        out.append(flat[..., off:off + n].reshape(lead + tuple(shp)))
        off += n
    return out


FFN_TM = 1024


def _ffn_up(hn, wg, wu, at, *, name):
    t, d = hn.shape
    fq = wg.shape[-1]
    tm = _tile(t, FFN_TM)

    def body(h_ref, wg_ref, wu_ref, g_ref, u_ref, a_ref):
        h = h_ref[...]
        g, u = _dg(h, _b(wg_ref[...]), 1, 0), _dg(h, _b(wu_ref[...]), 1, 0)
        g_ref[...] = g.astype(BF16)
        u_ref[...] = u.astype(BF16)
        a_ref[...] = _f_swiglu(g, u)[0].astype(BF16)

    w_spec = pl.BlockSpec((None,) * (1 + len(at)) + (d, fq), lambda s, i: (s,) + at + (0, 0))
    o_spec = pl.BlockSpec((None, tm, fq), lambda s, i: (s, i, 0))
    out = jax.ShapeDtypeStruct((N_CHIPS, t, fq), BF16)
    return pl.pallas_call(
        body, name=name, grid=(N_CHIPS, t // tm),
        in_specs=[pl.BlockSpec((tm, d), lambda s, i: (i, 0)), w_spec, w_spec], out_specs=[o_spec] * 3,
        out_shape=[out] * 3, compiler_params=_params("parallel", "parallel"),
    )(hn, wg, wu)


def _ffn_d_up(dout, wd, g, u, at, *, name):
    t, d = dout.shape
    fq = wd.shape[-2]
    tm = _tile(t, FFN_TM)

    def body(do_ref, wd_ref, g_ref, u_ref, dg_ref, du_ref):
        da = _dg(_b(do_ref[...]), _b(wd_ref[...]), 1, 1) * 0.5
        _, vjp = jax.vjp(_f_swiglu, g_ref[...].astype(F32), u_ref[...].astype(F32))
        dg, du = vjp((da,))
        dg_ref[...] = dg.astype(BF16)
        du_ref[...] = du.astype(BF16)

    w_spec = pl.BlockSpec((None,) * (1 + len(at)) + (fq, d), lambda i, s: (s,) + at + (0, 0))
    o_spec = pl.BlockSpec((None, tm, fq), lambda i, s: (s, i, 0))
    out = jax.ShapeDtypeStruct((N_CHIPS, t, fq), BF16)
    return pl.pallas_call(
        body, name=name, grid=(t // tm, N_CHIPS),
        in_specs=[pl.BlockSpec((tm, d), lambda i, s: (i, 0)), w_spec, o_spec, o_spec], out_specs=[o_spec] * 2,
        out_shape=[out] * 2, compiler_params=_params("parallel", "parallel"),
    )(dout, wd, g, u)


def _ffn_fwd(h, gain, wg, wu, wd, at, tag):
    lead = (Q,) + at
    hn, = _row_fwd(_f_rms, [h], [gain], [(D_MODEL, BF16)], name=f"{tag}_norm")
    g, u, a = _ffn_up(hn, wg, wu, at, name=f"{tag}_up")
    out = _mm((a, (Q,)), (wd, lead), res=h, scale=0.5, name=f"{tag}_down")
    return out, (h, hn, g, u, a)


def _ffn_bwd(dout, saved, gain, wg, wu, wd, at, grads, g_at, tag):
    h, hn, g, u, a = saved
    lead = (Q,) + at
    g_lead = (Q,) + g_at
    dg, du = _ffn_d_up(dout, wd, g, u, at, name=f"{tag}_d_up")
    g_gate, g_up, g_down = grads
    g_down = _mm((a, (Q,)), dout, ta=True, scale=0.5, into=(g_down, g_lead), name=f"{tag}_dw_down")
    g_gate = _mm((dg, (Q,)), hn, ta=True, into=(g_gate, g_lead), name=f"{tag}_dw_gate")
    g_up = _mm((du, (Q,)), hn, ta=True, into=(g_up, g_lead), name=f"{tag}_dw_up")
    dhn = _mm((dg, (Q,)), (wg, lead), tb=True, name=f"{tag}_d_norm_gate")
    dh, dgain = _mm((du, (Q,)), (wu, lead), tb=True, res=dhn, norm_bwd=(h, gain, dout), name=f"{tag}_d_in")
    return dh, dgain, (g_gate, g_up, g_down)


def _att_fwd(h, gain, w_in, qg, kg, sinks, w_out, gather):
    hn, = _row_fwd(_f_rms, [h], [gain], [(D_MODEL, BF16)], name="att_norm")
    proj = _mm(hn, w_in, out_dtype=BF16, name="att_in")
    a, rtot, *gathered = _sb_fwd(proj, name="att_sb", gather=gather)
    b = _swa_fwd(proj, qg, kg, sinks, name="att_swa")
    out = _mm(a, (w_out, (0,)), res=h, name="att_out_sb")
    out = _mm(b, (w_out, (1,)), res=out, name="att_out_swa")
    return out, (h, hn, proj, a, rtot, b), gathered


def _att_bwd(dout, saved, gain, w_in, qg, kg, sinks, w_out, scatter):
    h, hn, proj, a, rtot, b = saved
    da = _mm(dout, (w_out, (0,)), tb=True, name="att_d_sb")
    db = _mm(dout, (w_out, (1,)), tb=True, name="att_d_swa")
    dw_out = lax.empty(w_out.shape, F32)
    dw_out = _mm(a, dout, ta=True, into=(dw_out, (0,)), name="att_dw_out_sb")
    dw_out = _mm(b, dout, ta=True, into=(dw_out, (1,)), name="att_dw_out_swa")
    dq, dk, dv, *landed = _sb_bwd(proj, rtot, da, name="att_sb_bwd", scatter=scatter)
    dqb, dkb, dvb, dqg, dkg, dsk = _swa_bwd(proj, qg, kg, sinks, db, name="att_swa_bwd")
    dproj = jnp.concatenate([dq, dk, dv, dqb, dkb, dvb], axis=1)
    dw_in = _mm(hn, dproj, ta=True, name="att_dw_in")
    dh, dgain = _mm(dproj, w_in, tb=True, norm_bwd=(h, gain, dout), name="att_d_in")
    return dh, dgain, dw_in, dqg, dkg, dsk, dw_out, [landed[3 * a:3 * a + 3] for a in range(len(scatter))]


def _gdn_layer_fwd(h, gain, w_in, conv_w, alog, dtb, out_gain, w_out):
    w_qkv, w_z, w_ba = w_in[:, :GDN_CONV_W], w_in[:, GDN_CONV_W:GDN_CONV_W + GDN_VW], w_in[:, GDN_CONV_W + GDN_VW:]
    hn, = _row_fwd(_f_rms, [h], [gain], [(D_MODEL, BF16)], name="gdn_norm")
    pq = _mm(hn, w_qkv, name="gdn_in_qkv")
    pz = _mm(hn, w_z, name="gdn_in_z")
    ba = _mm(hn, w_ba, name="gdn_in_ba")
    act = _conv_fwd(pq, conv_w, name="gdn_conv")
    o, states, inverses = _gdn_fwd(act, ba, alog, dtb, name="gdn_rule")
    y, = _row_fwd(_f_gdn_out, [o, pz], [out_gain], [(GDN_VW, BF16)], name="gdn_gate")
    out = _mm(y, w_out, res=h, name="gdn_out")
    return out, (h, hn, pq, pz, ba, act, o, states, inverses, y, (w_qkv, w_z, w_ba))


def _gdn_layer_bwd(dout, saved, gain, conv_w, alog, dtb, out_gain, w_out):
    h, hn, pq, pz, ba, act, o, states, inverses, y, (w_qkv, w_z, w_ba) = saved
    dy = _mm(dout, w_out, tb=True, name="gdn_d_gate")
    dw_out = _mm(y, dout, ta=True, name="gdn_dw_out")
    do, dpz, dout_gain = _row_bwd(_f_gdn_out, [o, pz], [out_gain], [dy], [(0, F32), (1, F32)], [0], name="gdn_gate_bwd")
    dact, dba, dal, ddb = _gdn_bwd(act, ba, alog, dtb, states, inverses, do, name="gdn_rule_bwd")
    dpq, dconv = _conv_bwd(pq, conv_w, dact, name="gdn_conv_bwd")
    dw_in = [_mm(hn, dpq, ta=True, name="gdn_dw_qkv"), _mm(hn, dpz, ta=True, name="gdn_dw_z"),
             _mm(hn, dba, ta=True, name="gdn_dw_ba")]
    dhn = _mm(dpq, w_qkv, tb=True, name="gdn_d_norm_qkv")
    dhn = _mm(dpz, w_z, tb=True, res=dhn, name="gdn_d_norm_z")
    dh, dgain = _mm(dba, w_ba, tb=True, res=dhn, norm_bwd=(h, gain, dout), name="gdn_d_in")
    return dh, dgain, dw_in, dconv, dal, ddb, dout_gain, dw_out


def _ple_fwd(h, gain, w_gate, w_proj, pe, tag):
    hn, = _row_fwd(_f_rms, [h], [gain], [(D_MODEL, BF16)], name=f"{tag}_norm")
    gl = _mm(hn, w_gate, name=f"{tag}_gate")
    pp = _mm(pe, w_proj, name=f"{tag}_proj")
    out, = _row_fwd(_f_ple, [h, gl, pp], [], [(D_MODEL, F32)], name=f"{tag}_mix")
    return out, (h, hn, gl, pp)


def _ple_bwd(dout, saved, gain, w_gate, pe, tag):
    h, hn, gl, pp = saved
    dha, dgl, dpp = _row_bwd(_f_ple, [h, gl, pp], [], [dout], [(0, F32), (1, BF16), (2, BF16)], [], name=f"{tag}_mix_bwd")
    dw_gate = _mm(hn, dgl, ta=True, name=f"{tag}_dw_gate")
    dw_proj = _mm(pe, dpp, ta=True, name=f"{tag}_dw_proj")
    dh, dgain = _mm(dgl, w_gate, tb=True, norm_bwd=(h, gain, dha), name=f"{tag}_d_in")
    return dh, dgain, dw_gate, dw_proj


def kernel(x, p, ffn_norm, ffn_w_gate, ffn_w_up, ffn_w_down, mix_norm, att_w_in, att_q_norm, att_k_norm, att_sinks, att_w_out, gdn_w_in, gdn_conv_w, gdn_a_log, gdn_dt_bias, gdn_out_norm, gdn_w_out, ple_norm, ple_w_gate, ple_w_proj, loss_target, m_ffn_norm, m_ffn_w_gate, m_ffn_w_up, m_ffn_w_down, m_mix_norm, m_att_w_in, m_att_q_norm, m_att_k_norm, m_att_sinks, m_att_w_out, m_gdn_w_in, m_gdn_conv_w, m_gdn_a_log, m_gdn_dt_bias, m_gdn_out_norm, m_gdn_w_out, m_ple_norm, m_ple_w_gate, m_ple_w_proj, v_ffn_norm, v_ffn_w_gate, v_ffn_w_up, v_ffn_w_down, v_mix_norm, v_att_w_in, v_att_q_norm, v_att_k_norm, v_att_sinks, v_att_w_out, v_gdn_w_in, v_gdn_conv_w, v_gdn_a_log, v_gdn_dt_bias, v_gdn_out_norm, v_gdn_w_out, v_ple_norm, v_ple_w_gate, v_ple_w_proj):
    arg = dict(locals())
    cx, cy, cc = _coords()
    chip = (2 * cx + cy).astype(jnp.int32).reshape(1)
    core = cc.astype(jnp.int32).reshape(1)
    n_layers = ffn_norm.shape[0]

    quarter = lambda n, i=None: _halves((arg[n] if i is None else arg[n][i]).astype(BF16))
    ffn_names = ('ffn_w_gate', 'ffn_w_up', 'ffn_w_down')
    early = [quarter(n, 0) for n in ffn_names] + [quarter('att_w_in'), quarter('att_w_out')]
    late_names = ('gdn_w_in', 'gdn_w_out', 'ple_w_gate', 'ple_w_proj')
    late = [quarter(n, 1) for n in ffn_names] + [quarter(n) for n in late_names]
    *ffn_w0, att_in_q, att_out_q = _gather_quarters(early, name="gather_weights")
    wt = {'att_w_in': _join_quarters(att_in_q.reshape((N_CHIPS,) + att_w_in.shape[1:]), name="att_w_in_join"),
          'att_w_out': att_out_q.reshape(2, SB_W, D_MODEL)}

    small_names = list(_SMALL_CUT)
    small_shapes = [arg[n].shape for n in small_names]
    svec = _pack([arg[n].reshape(-1) for n in small_names], SMALL_ROW_MULT)
    srows = svec.shape[0]
    sall = _gather_all(svec, name="gather_gains").reshape(N_CHIPS, 2, srows, PACK_W)[:, 0]
    for n, q in zip(small_names, _unpack(sall, small_shapes)):
        wt[n] = _from_quarters(q, _SMALL_CUT[n])
    row = lambda v: v.reshape(1, -1)

    as_ffn = lambda g, n: g.reshape((N_CHIPS,) + arg[n].shape[1:])
    ffn_w = [tuple(as_ffn(g, n) for g, n in zip(ffn_w0, ffn_names)), None]
    h = x[0]
    tape = []
    for i in range(n_layers):
        j = i // 2
        h, s0 = _ffn_fwd(h, row(wt['ffn_norm'][i, 0]), *ffn_w[i], (0,), f"ffn{i}a")
        if i % 2 == 0:
            h, sm, gathered = _att_fwd(h, row(mix_norm[i]), wt['att_w_in'], att_q_norm[j:j + 1], att_k_norm[j:j + 1],
                                       att_sinks[j:j + 1], wt['att_w_out'], late)
            ffn_w[1] = tuple(as_ffn(g, n) for g, n in zip(gathered[:3], ffn_names))
            wq = {n: g.reshape((N_CHIPS,) + arg[n].shape) for n, g in zip(late_names, gathered[3:])}
            wt['gdn_w_in'] = _join_quarters(wq['gdn_w_in'][:, 0], name="gdn_w_in_join")
            wt['gdn_w_out'] = wq['gdn_w_out'].reshape(GDN_VW, D_MODEL)
            wt['ple_w_gate'] = _from_quarters(wq['ple_w_gate'], 1)
            wt['ple_w_proj'] = _from_quarters(wq['ple_w_proj'], 2)
        else:
            h, sm = _gdn_layer_fwd(h, row(mix_norm[i]), wt['gdn_w_in'], wt['gdn_conv_w'][j], gdn_a_log[j:j + 1],
                                   gdn_dt_bias[j:j + 1], gdn_out_norm[j:j + 1], wt['gdn_w_out'])
        h, s1 = _ffn_fwd(h, row(wt['ffn_norm'][i, 1]), *ffn_w[i], (1,), f"ffn{i}b")
        h, sp = _ple_fwd(h, row(ple_norm[i]), wt['ple_w_gate'][i], wt['ple_w_proj'][i], p[i, 0], f"ple{i}")
        tape.append((s0, sm, s1, sp))

    dh, loss_local = _loss_head(h, loss_target[0], name="loss_head")
    loss = lax.psum(loss_local, ("x", "y", "c"))

    gr = {}
    stored_t = ('ffn_w_gate', 'ffn_w_up')
    as_stored = lambda a, n: jnp.swapaxes(a, -1, -2) if n in stored_t else a
    ffn_g = [tuple(lax.empty((N_CHIPS,) + as_stored(arg[n], n).shape[1:], F32) for n in ffn_names)
             for _ in range(n_layers)]
    d_ffn_norm = [[None, None] for _ in range(n_layers)]
    d_mix, d_ple_norm, d_ple_gate, d_ple_proj = [None] * n_layers, [None] * n_layers, [None] * n_layers, [None] * n_layers

    def as_halves(g):
        return g.reshape((N_CHIPS, 2, -1, g.shape[-1]))

    def pair_up(keys, grads, tag):
        got = _swap_halves(grads, name=f"grad_swap_halves_{tag}")
        return [_add_pair(g, o, core, name=f"grad_add_pair_{k}") for k, g, o in zip(keys, grads, got)]

    for i in reversed(range(n_layers)):
        j = i // 2
        s0, sm, s1, sp = tape[i]
        dh, d_ple_norm[i], d_ple_gate[i], d_ple_proj[i] = _ple_bwd(dh, sp, row(ple_norm[i]), wt['ple_w_gate'][i], p[i, 0],
                                                                   f"ple{i}")
        dh, d_ffn_norm[i][1], ffn_g[i] = _ffn_bwd(dh, s1, row(wt['ffn_norm'][i, 1]), *ffn_w[i], (1,), ffn_g[i], (1,),
                                                  f"ffn{i}b")
        if i % 2 == 0:
            gr['ple_w_gate'] = _to_quarters(jnp.stack(d_ple_gate), 1)
            gr['ple_w_proj'] = _to_quarters(jnp.stack(d_ple_proj), 2)
            first_keys = [f"{n}_1" for n in ffn_names] + list(late_names)
            first_pairs = pair_up(first_keys, [as_halves(g) for g in ffn_g[1]] + [as_halves(gr[n]) for n in late_names], "a")
            (dh, d_mix[i], dw_in, gr['att_q_norm'], gr['att_k_norm'], gr['att_sinks'], dw_out,
             first_recv) = _att_bwd(dh, sm, row(mix_norm[i]), wt['att_w_in'], att_q_norm[j:j + 1],
                                    att_k_norm[j:j + 1], att_sinks[j:j + 1], wt['att_w_out'], first_pairs)
            gr['att_w_in'] = _split_quarters([dw_in], name="att_dw_in_split")
            gr['att_w_out'] = dw_out
        else:
            (dh, d_mix[i], dw_in, dconv, gr['gdn_a_log'], gr['gdn_dt_bias'], gr['gdn_out_norm'],
             dw_out) = _gdn_layer_bwd(dh, sm, row(mix_norm[i]), wt['gdn_conv_w'][j], gdn_a_log[j:j + 1],
                                      gdn_dt_bias[j:j + 1], gdn_out_norm[j:j + 1], wt['gdn_w_out'])
            gr['gdn_w_in'] = _split_quarters(dw_in, name="gdn_dw_in_split")
            gr['gdn_w_out'] = dw_out
            gr['gdn_conv_w'] = dconv[None]
        dh, d_ffn_norm[i][0], ffn_g[i] = _ffn_bwd(dh, s0, row(wt['ffn_norm'][i, 0]), *ffn_w[i], (0,), ffn_g[i], (0,),
                                                  f"ffn{i}a")
    grad_x = dh[None]

    gr['ffn_norm'] = jnp.stack([jnp.stack([d_ffn_norm[i][k][0] for k in range(2)]) for i in range(n_layers)])
    gr['mix_norm'] = jnp.concatenate(d_mix, axis=0)
    gr['ple_norm'] = jnp.concatenate(d_ple_norm, axis=0)

    last_keys = [f"{n}_0" for n in ffn_names] + ['att_w_in', 'att_w_out']
    last_pairs = pair_up(last_keys, [as_halves(g) for g in ffn_g[0]] + [as_halves(gr['att_w_in']), as_halves(gr['att_w_out'])],
                         "b")
    last_recv = _scatter_quarters(last_pairs, name="grad_scatter")
    keys = first_keys + last_keys
    tots = [_add_chips(pr, rc, chip, name=f"grad_add_chips_{k}")
            for k, pr, rc in zip(keys, first_pairs + last_pairs, first_recv + last_recv)]
    theirs = _share_halves(tots, name="grad_share")
    summed = dict(zip(keys, zip(tots, theirs)))

    whole_shapes = [arg[n].shape for n in _WHOLE]
    cut_full_shapes = [gr[n].shape for n in small_names]
    gvec = _pack([gr[n].reshape(-1) for n in _WHOLE + small_names], SMALL_ROW_MULT)
    gall = _sum_blocks(_gather_all(gvec, name="gather_small_grads"), N_DEV, name="sum_small_grads")
    parts = _unpack(gall, whole_shapes + cut_full_shapes)
    gsum = dict(zip(_WHOLE, parts))
    for n, g in zip(small_names, parts[len(_WHOLE):]):
        gsum[n] = lax.dynamic_index_in_dim(_to_quarters(g, _SMALL_CUT[n]), chip[0], axis=0, keepdims=False)

    delta, new_m, new_v = {}, {}, {}
    for n in ('att_w_in', 'att_w_out') + late_names:
        res = _adamw_halves(_halves(arg[n]), _halves(arg["m_" + n]), _halves(arg["v_" + n]), *summed[n], core,
                            name=f"adamw_{n}")
        gsum[n], delta[n], new_m[n], new_v[n] = (r.reshape(arg[n].shape) for r in res)
    for n in ffn_names:
        wmv = [as_stored(arg[k + n], n) for k in ("", "m_", "v_")]
        res = tuple(lax.empty(wmv[0].shape, F32) for _ in range(4))
        for i in range(n_layers):
            res = _adamw_halves(*wmv, *summed[f"{n}_{i}"], core, name=f"adamw_{n}_{i}", into=(res, (i,)))
        gsum[n], delta[n], new_m[n], new_v[n] = (as_stored(r, n) for r in res)
    for n in _WHOLE + small_names:
        delta[n], new_m[n], new_v[n] = _adamw(arg[n], gsum[n], arg["m_" + n], arg["v_" + n], name=f"adamw_{n}")
    return (loss, grad_x, *[gsum[n] for n in _WEIGHTS], *[delta[n] for n in _WEIGHTS],
            *[new_m[n] for n in _WEIGHTS], *[new_v[n] for n in _WEIGHTS])
```

```python
import functools
import math

import jax
import jax.numpy as jnp
from jax import lax
from jax.experimental import pallas as pl
from jax.experimental.pallas import tpu as pltpu

F32 = jnp.float32
BF16 = jnp.bfloat16
MESH = pl.DeviceIdType.MESH

LANES = 128
VMEM_LIMIT_BYTES = 56 * 1024 * 1024

EPS = 1e-6
D_MODEL = 1024
HEAD_DIM = 64
SB_HEADS = 8
SWA_HEADS = 8
SWA_KV_HEADS = 2
WINDOW = 128
GDN_K_HEADS = 8
GDN_V_HEADS = 16
GDN_HEAD_DIM = 128
GDN_CONV = 4
GDN_CHUNK = 64
SB_W = SB_HEADS * HEAD_DIM
SWA_QW = SWA_HEADS * HEAD_DIM
SWA_KVW = SWA_KV_HEADS * HEAD_DIM
GDN_KW = GDN_K_HEADS * GDN_HEAD_DIM
GDN_VW = GDN_V_HEADS * GDN_HEAD_DIM
GDN_CONV_W = 2 * GDN_KW + GDN_VW

ADAM_LR = 0.001
ADAM_B1 = 0.9
ADAM_B2 = 0.999
ADAM_EPS = 1e-08
ADAM_WD = 0.01
ADAM_STEP = 10

NEG = -1e30


def _params(*sem):
    return pltpu.CompilerParams(dimension_semantics=sem or None, vmem_limit_bytes=VMEM_LIMIT_BYTES)


def _tile(n, cap, align=LANES):
    if n <= cap:
        return n
    for t in range(cap - cap % align, 0, -align):
        if n % t == 0:
            return t
    return n


N_CHIPS = 4
MM_VMEM_BUDGET_BYTES = 40 * 1024 * 1024
Q = "q"


def _opnd(x):
    return x if isinstance(x, tuple) else (x, ())


def _mm(a, b, *, name, ta=False, tb=False, out_dtype=F32, res=None, scale=1.0, out_q=False, into=None, norm_bwd=None,
        tm=None, tn=1024, tk=1024):
    (a_arr, a_lead), (b_arr, b_lead) = _opnd(a), _opnd(b)
    (k_a, m) = a_arr.shape[-2:] if ta else a_arr.shape[-2:][::-1]
    (n, k_b) = b_arr.shape[-2:] if tb else b_arr.shape[-2:][::-1]
    if into is not None:
        out_arr, out_lead = into
        out_q, out_dtype = Q in out_lead, out_arr.dtype
    else:
        out_lead = (Q,) if out_q else ()
    red_q = (Q in a_lead or Q in b_lead) and not out_q
    kq = min(k_a, k_b)
    assert (k_a == k_b) or (red_q and max(k_a, k_b) == N_CHIPS * kq), (a_arr.shape, b_arr.shape)
    tn, tk = _tile(n, tn), _tile(kq, tk)
    if tm is None:
        r_item = _opnd(res)[0].dtype.itemsize if res is not None else 0
        per_row = 2 * (tk * a_arr.dtype.itemsize + tn * (jnp.dtype(out_dtype).itemsize + r_item)) + 4 * tn
        if norm_bwd is not None:
            per_row += (2 * 2 + 4) * 4 * tn
        room = MM_VMEM_BUDGET_BYTES - 2 * tk * tn * b_arr.dtype.itemsize
        tm = next(c for c in (4096, 2048, 1024, 512, 256, 128) if c * per_row <= room or c == 128)
    tm = _tile(m, tm)
    nk = kq // tk
    ksteps = nk * (N_CHIPS if red_q else 1)
    dims = (((0 if ta else 1,), (1 if tb else 0,)), ((), ()))
    has_res = res is not None
    n_out = 2 if norm_bwd is not None else 1

    def body(*refs):
        a_ref, b_ref = refs[0], refs[1]
        o_ref, acc_ref = refs[-1 - n_out], refs[-1]
        k = pl.program_id(3)
        first_rows = pl.program_id(1) == 0

        @pl.when(k == 0)
        def _():
            acc_ref[...] = jnp.zeros_like(acc_ref)

        acc_ref[...] += lax.dot_general(a_ref[...].astype(BF16), b_ref[...].astype(BF16), dims,
                                        preferred_element_type=F32)

        @pl.when(k == ksteps - 1)
        def _():
            r = acc_ref[...]
            if scale != 1.0:
                r = r * scale
            if has_res:
                r = r + refs[2][...].astype(F32)
            if norm_bwd is not None:
                h_ref, gain_ref, dres_ref = refs[2 + has_res:5 + has_res]
                dgain_ref = refs[-2]
                _, vjp = jax.vjp(_f_rms_res, h_ref[...], gain_ref[...])
                r, dgain = vjp((r, dres_ref[...]))

                @pl.when(first_rows)
                def _():
                    dgain_ref[...] = jnp.zeros_like(dgain_ref)

                dgain_ref[...] += dgain
            o_ref[...] = r.astype(o_ref.dtype)

    def spec(lead, blk, pos):
        def index(s, i, j, k):
            kk = k % nk if (red_q and Q in lead) else k
            quarter = s if out_q else k // nk
            return tuple(quarter if l == Q else l for l in lead) + pos(i, j, kk)
        return pl.BlockSpec((None,) * len(lead) + blk, index)

    a_spec = spec(a_lead, (tk, tm), lambda i, j, k: (k, i)) if ta else spec(a_lead, (tm, tk), lambda i, j, k: (i, k))
    b_spec = spec(b_lead, (tn, tk), lambda i, j, k: (j, k)) if tb else spec(b_lead, (tk, tn), lambda i, j, k: (k, j))
    o_spec = spec(out_lead, (tm, tn), lambda i, j, k: (i, j))
    in_specs, args = [a_spec, b_spec], [a_arr, b_arr]
    if has_res:
        r_arr, r_lead = _opnd(res)
        in_specs.append(spec(r_lead, (tm, tn), lambda i, j, k: (i, j)))
        args.append(r_arr)
    out_specs, out_shapes = [o_spec], []
    if norm_bwd is not None:
        assert tn == n and not out_q and into is None, "the norm's backward needs whole rows"
        h_arr, gain_arr, dres_arr = norm_bwd
        row_spec = spec((), (tm, tn), lambda i, j, k: (i, j))
        gain_spec = pl.BlockSpec((1, tn), lambda s, i, j, k: (0, 0))
        in_specs += [row_spec, gain_spec, row_spec]
        args += [h_arr, gain_arr, dres_arr]
        out_specs.append(gain_spec)
    aliases = {}
    if into is not None:
        in_specs.append(pl.BlockSpec(memory_space=pl.ANY))
        args.append(out_arr)
        aliases = {len(args) - 1: 0}
        out_shapes.append(jax.ShapeDtypeStruct(out_arr.shape, out_arr.dtype))
    else:
        out_shapes.append(jax.ShapeDtypeStruct(((N_CHIPS,) if out_q else ()) + (m, n), out_dtype))
    if norm_bwd is not None:
        out_shapes.append(jax.ShapeDtypeStruct((1, n), F32))
    out = pl.pallas_call(
        body, name=name, grid=(N_CHIPS if out_q else 1, m // tm, n // tn, ksteps), in_specs=in_specs,
        out_specs=out_specs, out_shape=out_shapes, scratch_shapes=[pltpu.VMEM((tm, tn), F32)],
        input_output_aliases=aliases,
        compiler_params=_params("parallel", *(("arbitrary",) * 3 if norm_bwd is not None else ("parallel", "parallel", "arbitrary"))),
    )(*args)
    return out if norm_bwd is not None else out[0]


def _row_spec(r, tm):
    if isinstance(r, tuple):
        arr, width, cb = r
        return arr, pl.BlockSpec((tm, width), lambda i, cb=cb: (i, cb))
    return r, pl.BlockSpec((tm, r.shape[1]), lambda i: (i, 0))


def _const_spec(c):
    return pl.BlockSpec(c.shape, lambda i: (0,) * c.ndim)


def _row_fwd(fn, rows, consts, outs, *, name, tm=256):
    tm = _tile(_row_spec(rows[0], tm)[0].shape[0], tm, 8)
    arrs, specs = zip(*[_row_spec(r, tm) for r in rows])
    t = arrs[0].shape[0]
    nr, nc = len(rows), len(consts)

    def body(*refs):
        vals = [r[...].astype(F32) for r in refs[:nr + nc]]
        res = fn(*vals)
        for o_ref, v in zip(refs[nr + nc:], res):
            o_ref[...] = v.astype(o_ref.dtype)

    out = pl.pallas_call(
        body, name=name, grid=(t // tm,),
        in_specs=list(specs) + [_const_spec(c) for c in consts],
        out_specs=[pl.BlockSpec((tm, w), lambda i: (i, 0)) for w, _ in outs],
        out_shape=[jax.ShapeDtypeStruct((t, w), dt) for w, dt in outs],
        compiler_params=_params("parallel"),
    )(*arrs, *consts)
    return list(out)


def _row_bwd(fn, rows, consts, cts, row_grads, const_grads, *, name, tm=256):
    tm = _tile(_row_spec(rows[0], tm)[0].shape[0], tm, 8)
    arrs, specs = zip(*[_row_spec(r, tm) for r in rows])
    ct_arrs, ct_specs = zip(*[_row_spec(r, tm) for r in cts])
    t = arrs[0].shape[0]
    nr, nc, nt = len(rows), len(consts), len(cts)
    n_in = nr + nc + nt

    def body(*refs):
        vals = [r[...].astype(F32) for r in refs[:nr + nc]]
        ctv = tuple(r[...].astype(F32) for r in refs[nr + nc:n_in])
        _, vjp = jax.vjp(fn, *vals)
        g = vjp(ctv)
        outs = refs[n_in:]
        for (idx, _), o_ref in zip(row_grads, outs[:len(row_grads)]):
            o_ref[...] = g[idx].astype(o_ref.dtype)
        first = pl.program_id(0) == 0
        for ci, o_ref in zip(const_grads, outs[len(row_grads):]):
            @pl.when(first)
            def _(o_ref=o_ref):
                o_ref[...] = jnp.zeros_like(o_ref)

            o_ref[...] += g[nr + ci]

    widths = [(_row_spec(rows[idx], tm)[1].block_shape[1], dt) for idx, dt in row_grads]
    out = pl.pallas_call(
        body, name=name, grid=(t // tm,),
        in_specs=list(specs) + [_const_spec(c) for c in consts] + list(ct_specs),
        out_specs=[pl.BlockSpec((tm, w), lambda i: (i, 0)) for w, _ in widths]
        + [_const_spec(consts[ci]) for ci in const_grads],
        out_shape=[jax.ShapeDtypeStruct((t, w), dt) for w, dt in widths]
        + [jax.ShapeDtypeStruct(consts[ci].shape, F32) for ci in const_grads],
        compiler_params=_params("arbitrary"),
    )(*arrs, *consts, *ct_arrs)
    return list(out)


def _rms(x, g):
    return x * lax.rsqrt(jnp.mean(x * x, axis=-1, keepdims=True) + EPS) * g


def _f_rms(h, g):
    return (_rms(h, g),)


def _f_rms_res(h, g):
    return (_rms(h, g), h)


def _f_swiglu(g, u):
    return (g * jax.nn.sigmoid(g) * u,)


def _f_ple(h, gl, pp):
    return (h + jax.nn.sigmoid(gl) * pp,)


def _f_gdn_out(o, z, gain):
    outs = []
    for hd in range(GDN_V_HEADS):
        sl = slice(hd * GDN_HEAD_DIM, (hd + 1) * GDN_HEAD_DIM)
        oh, zh = o[:, sl], z[:, sl]
        outs.append(_rms(oh, gain) * (zh * jax.nn.sigmoid(zh)))
    return (jnp.concatenate(outs, axis=1),)


def _loss_head(y, target, *, name, tm=512):
    t, d = y.shape
    tm = _tile(t, tm, 8)

    def body(y_ref, t_ref, dy_ref, l_ref):
        @pl.when(pl.program_id(0) == 0)
        def _():
            l_ref[...] = jnp.zeros_like(l_ref)

        e = y_ref[...] - t_ref[...]
        dy_ref[...] = e * (1.0 / d)
        l_ref[...] += jnp.sum(e * e) * (0.5 / d)

    dy, l = pl.pallas_call(
        body, name=name, grid=(t // tm,),
        in_specs=[pl.BlockSpec((tm, d), lambda i: (i, 0))] * 2,
        out_specs=[pl.BlockSpec((tm, d), lambda i: (i, 0)), pl.BlockSpec((8, LANES), lambda i: (0, 0))],
        out_shape=[jax.ShapeDtypeStruct((t, d), F32), jax.ShapeDtypeStruct((8, LANES), F32)],
        compiler_params=_params("arbitrary"),
    )(y, target)
    return dy, l[0, 0]


def _dg(a, b, ca, cb):
    nb = a.ndim - 2
    batch = tuple(range(nb))
    return lax.dot_general(a, b, (((ca + nb,), (cb + nb,)), (batch, batch)), preferred_element_type=F32)


def _b(x):
    return x.astype(BF16)


@jax.custom_vjp
def _bdot(a, b):
    return _dg(_b(a), _b(b), 1, 0)


def _bdot_fwd(a, b):
    return _bdot(a, b), (a, b)


def _bdot_bwd(r, ct):
    a, b = r
    return _dg(_b(ct), _b(b), 1, 1), _dg(_b(a), _b(ct), 0, 0)


_bdot.defvjp(_bdot_fwd, _bdot_bwd)


@jax.custom_vjp
def _bdot_nt(a, b):
    return _dg(_b(a), _b(b), 1, 1)


def _bdot_nt_fwd(a, b):
    return _bdot_nt(a, b), (a, b)


def _bdot_nt_bwd(r, ct):
    a, b = r
    return _dg(_b(ct), _b(b), 1, 0), _dg(_b(ct), _b(a), 0, 0)


_bdot_nt.defvjp(_bdot_nt_fwd, _bdot_nt_bwd)


@jax.custom_vjp
def _bdot_tn(a, b):
    return _dg(_b(a), _b(b), 0, 0)


def _bdot_tn_fwd(a, b):
    return _bdot_tn(a, b), (a, b)


def _bdot_tn_bwd(r, ct):
    a, b = r
    return _dg(_b(b), _b(ct), 1, 1), _dg(_b(a), _b(ct), 1, 0)


_bdot_tn.defvjp(_bdot_tn_fwd, _bdot_tn_bwd)


def _two(x):
    hi = x.astype(BF16)
    return hi, (x - hi.astype(F32)).astype(BF16)


def _dg3(a, b, ca, cb):
    (ah, al), (bh, bl) = _two(a), _two(b)
    return _dg(ah, bh, ca, cb) + (_dg(ah, bl, ca, cb) + _dg(al, bh, ca, cb))


@jax.custom_vjp
def _hdot(a, b):
    return _dg3(a, b, 1, 0)


def _hdot_fwd(a, b):
    return _hdot(a, b), (a, b)


def _hdot_bwd(r, ct):
    a, b = r
    return _dg3(ct, b, 1, 1), _dg3(a, ct, 0, 0)


_hdot.defvjp(_hdot_fwd, _hdot_bwd)


@jax.custom_vjp
def _unit_lower_inverse(x):
    c = x.shape[-1]
    eye = (lax.broadcasted_iota(jnp.int32, x.shape, 1) == lax.broadcasted_iota(jnp.int32, x.shape, 2)).astype(F32)
    inv, pw = eye + x, x
    for _ in range(int(math.log2(c)) - 1):
        pw = _dg3(pw, pw, 1, 0)
        inv = inv + _dg3(inv, pw, 1, 0)
    return inv


def _unit_lower_inverse_fwd(x):
    inv = _unit_lower_inverse(x)
    return inv, inv


def _unit_lower_inverse_bwd(inv, ct):
    return (_dg3(_dg3(inv, ct, 0, 0), inv, 1, 1),)


_unit_lower_inverse.defvjp(_unit_lower_inverse_fwd, _unit_lower_inverse_bwd)


@jax.custom_vjp
def _known_inverse(x, inv):
    return inv


def _known_inverse_fwd(x, inv):
    return inv, inv


def _known_inverse_bwd(inv, ct):
    return _dg3(_dg3(inv, ct, 0, 0), inv, 1, 1), jnp.zeros_like(inv)


_known_inverse.defvjp(_known_inverse_fwd, _known_inverse_bwd)


def _split_dot(x, u):
    hi, lo = _two(x)
    return _dg(hi, u, 1, 0) + _dg(lo, u, 1, 0)


@jax.custom_vjp
def _ldot(l01, x):
    hi, lo = _two(x)
    l01 = l01.astype(BF16)
    return _dg(l01, hi, 1, 0) + _dg(l01, lo, 1, 0)


def _ldot_fwd(l01, x):
    return _ldot(l01, x), l01


def _ldot_bwd(l01, ct):
    hi, lo = _two(ct)
    l01b = l01.astype(BF16)
    return jnp.zeros_like(l01), _dg(l01b, hi, 0, 0) + _dg(l01b, lo, 0, 0)


_ldot.defvjp(_ldot_fwd, _ldot_bwd)


SB_BLK = 128
SB_BWD_BLK = 256
SB_KEYS = 512
SB_PAIRS = 2
SB_SCALE = HEAD_DIM ** -0.5


def _log_sigmoid(z):
    return jnp.minimum(z, 0.0) - jnp.log(1.0 + jnp.exp(-jnp.abs(z)))


def _sb_consts(t, blk):
    kb = min(SB_KEYS, t)
    nh = 2 * SB_PAIRS
    lane = lax.broadcasted_iota(jnp.int32, (nh, blk, kb), 2)
    row = lax.broadcasted_iota(jnp.int32, (nh, blk, kb), 1)
    ur = lax.broadcasted_iota(jnp.int32, (kb, kb), 0)
    uc = lax.broadcasted_iota(jnp.int32, (kb, kb), 1)
    return kb, nh, lane, row, ur, uc


def _sb_heads(x):
    head0 = lax.broadcasted_iota(jnp.int32, (x.shape[0], LANES), 1) < HEAD_DIM
    out = []
    for p in range(SB_PAIRS):
        blk = x[:, p * LANES:(p + 1) * LANES]
        out += [jnp.where(head0, blk, 0.0), jnp.where(head0, 0.0, blk)]
    return jnp.stack(out)


def _sb_pairs(x):
    return jnp.stack([x[:, (h // 2) * LANES:(h // 2 + 1) * LANES] for h in range(2 * SB_PAIRS)])


def _sb_merge(x):
    head0 = lax.broadcasted_iota(jnp.int32, (x.shape[1], LANES), 1) < HEAD_DIM
    return jnp.concatenate([jnp.where(head0, x[2 * p], x[2 * p + 1]) for p in range(SB_PAIRS)], axis=1)


def _sb_rows_dot(x, u):
    nh, rows, k = x.shape
    return _split_dot(x.reshape(nh * rows, k), u).reshape(nh, rows, k)


def _sb_fwd(proj, *, name, gather=()):
    t = proj.shape[0]
    nb = t // SB_BLK
    width = SB_PAIRS * LANES
    ng = SB_W // width
    na = len(gather)

    def body(q_ref, k_ref, v_ref, *rest):
        o_ref, r_ref = rest[na:na + 2]
        i = pl.program_id(1)
        if na:
            step = pl.program_id(0) * nb + i
            copies = lambda **kw: _gather_copies(rest[:na], rest[na + 2:2 * na + 2], *rest[2 * na + 2:], **kw)
            pl.when(step == 0)(lambda: _gather_start(copies(only_first=True)))
        kb, nh, lane, row, ur, uc = _sb_consts(t, SB_BLK)
        u_suffix = (ur >= uc).astype(BF16)
        qh = _b(_sb_heads(q_ref[...]) * SB_SCALE)
        diag = (i * SB_BLK) // kb

        def block(j, carry, masked):
            acc, car = carry
            keys = pl.ds(pl.multiple_of(j * kb, kb), kb)
            kj, vj = _b(_sb_pairs(k_ref[keys, :])), _b(_sb_pairs(v_ref[keys, :]))
            z = _dg(qh, kj, 1, 1)
            lk = _log_sigmoid(-z)
            if masked:
                causal = (j * kb + lane) < (i * SB_BLK + row)
                lk = jnp.where(causal, lk, 0.0)
            suf = _sb_rows_dot(lk, u_suffix) + car
            w = jnp.exp(z + suf)
            if masked:
                w = jnp.where(causal, w, 0.0)
            return acc + _dg(_b(w), vj, 1, 0), suf[:, :, 0:1]

        zero = (jnp.zeros((nh, SB_BLK, LANES), F32), jnp.zeros((nh, SB_BLK, 1), F32))
        carry = block(diag, zero, True)
        acc, car = lax.fori_loop(0, diag, lambda s, c: block(diag - 1 - s, c, False), carry)
        o_ref[...] = _sb_merge(acc)
        r_ref[...] = _sb_merge(jnp.broadcast_to(car, (nh, SB_BLK, LANES)))
        if na:
            pl.when(step == ng * nb - 1)(lambda: _gather_finish(copies()))

    return pl.pallas_call(
        body, name=name, grid=(ng, nb),
        in_specs=[pl.BlockSpec((SB_BLK, width), lambda p, i: (i, p)),
                  pl.BlockSpec((t, width), lambda p, i: (0, ng + p)),
                  pl.BlockSpec((t, width), lambda p, i: (0, 2 * ng + p))] + [ANY] * na,
        out_specs=[pl.BlockSpec((SB_BLK, width), lambda p, i: (i, p))] * 2 + [ANY] * na,
        out_shape=[jax.ShapeDtypeStruct((t, SB_W), F32)] * 2
        + [jax.ShapeDtypeStruct((N_CHIPS,) + g.shape, g.dtype) for g in gather],
        scratch_shapes=_gather_scratch(na) if na else [],
        compiler_params=_params("arbitrary", "arbitrary"),
    )(proj, proj, proj, *gather)


def _sb_bwd(proj, rtot, dout, *, name, scatter=()):
    t = proj.shape[0]
    nb = t // SB_BWD_BLK
    width = SB_PAIRS * LANES
    ng = SB_W // width
    na = len(scatter)

    def body(q_ref, k_ref, v_ref, r_ref, do_ref, *rest):
        dq_ref, dk_ref, dv_ref = rest[na:na + 3]
        i = pl.program_id(1)
        if na:
            step = pl.program_id(0) * nb + i
            copies = lambda: _scatter_copies(rest[:na], rest[na + 3:4 * na + 3], *rest[4 * na + 3:])
            pl.when(step == 0)(lambda: [cp.start() for cp in copies()] and None)
        kb, nh, lane, row, ur, uc = _sb_consts(t, SB_BWD_BLK)
        u_incl = (ur <= uc).astype(BF16)
        u_excl = (ur < uc).astype(BF16)
        q, do = q_ref[...], do_ref[...]
        qh, doh = _b(_sb_heads(q) * SB_SCALE), _b(_sb_heads(do))
        qb, dob = _b(_sb_pairs(q) * SB_SCALE), _b(_sb_pairs(do))
        rh = jnp.min(_sb_heads(r_ref[...]), axis=2, keepdims=True)
        diag = (i * SB_BWD_BLK) // kb

        @pl.when(i == 0)
        def _():
            dk_ref[...] = jnp.zeros_like(dk_ref)
            dv_ref[...] = jnp.zeros_like(dv_ref)

        def block(j, carry, masked):
            dq_acc, clk, ce = carry
            keys = pl.ds(pl.multiple_of(j * kb, kb), kb)
            kj, vj = _b(_sb_pairs(k_ref[keys, :])), _b(_sb_pairs(v_ref[keys, :]))
            z = _dg(qh, kj, 1, 1)
            lk = _log_sigmoid(-z)
            ls = z + lk
            if masked:
                causal = (j * kb + lane) < (i * SB_BWD_BLK + row)
                lk = jnp.where(causal, lk, 0.0)
            pre = _sb_rows_dot(lk, u_incl) + clk
            w = jnp.exp(ls + (rh - pre))
            if masked:
                w = jnp.where(causal, w, 0.0)
            e = _dg(doh, vj, 1, 1) * w
            pre_e = _sb_rows_dot(e, u_excl) + ce
            sig = jnp.exp(ls)
            dz = e - sig * (e + pre_e)
            if masked:
                dz = jnp.where(causal, dz, 0.0)
            dzb = _b(dz)
            dk_ref[keys, :] += _sb_merge(_dg(dzb, qb, 0, 0))
            dv_ref[keys, :] += _sb_merge(_dg(_b(w), dob, 0, 0))
            return dq_acc + _dg(dzb, kj, 1, 0), pre[:, :, kb - 1:], pre_e[:, :, kb - 1:] + e[:, :, kb - 1:]

        zero = (jnp.zeros((nh, SB_BWD_BLK, LANES), F32), jnp.zeros((nh, SB_BWD_BLK, 1), F32), jnp.zeros((nh, SB_BWD_BLK, 1), F32))
        carry = lax.fori_loop(0, diag, lambda j, c: block(j, c, False), zero)
        dq_acc, _, _ = block(diag, carry, True)
        dq_ref[...] = _sb_merge(dq_acc) * SB_SCALE
        if na:
            pl.when(step == ng * nb - 1)(lambda: [cp.wait() for cp in copies()] and None)

    blk = pl.BlockSpec((SB_BWD_BLK, width), lambda p, i: (i, p))
    whole = pl.BlockSpec((t, width), lambda p, i: (0, p))
    return pl.pallas_call(
        body, name=name, grid=(ng, nb),
        in_specs=[blk,
                  pl.BlockSpec((t, width), lambda p, i: (0, ng + p)),
                  pl.BlockSpec((t, width), lambda p, i: (0, 2 * ng + p)),
                  blk, blk] + [ANY] * na,
        out_specs=[blk, whole, whole] + [ANY] * (3 * na),
        out_shape=[jax.ShapeDtypeStruct((t, SB_W), F32)] * 3 + _scatter_shapes(scatter),
        scratch_shapes=_dma_sems(3 * na) if na else [],
        compiler_params=_params("arbitrary", "arbitrary"),
    )(proj, proj, proj, rtot, dout, *scatter)


SWA_G = SWA_HEADS // SWA_KV_HEADS


def _swa_heads(first, qs, ks, vs, qg, kg, sinks):
    shape = (SWA_HEADS, WINDOW, 2 * WINDOW)
    qi = lax.broadcasted_iota(jnp.int32, shape, 1)
    kj = lax.broadcasted_iota(jnp.int32, shape, 2)
    dist = qi + WINDOW - kj
    valid = (dist >= 0) & (dist < WINDOW) & (jnp.logical_not(first) | (kj >= WINDOW))
    head = lax.broadcasted_iota(jnp.int32, (SWA_HEADS, 1, 1), 0)
    slope = sum(jnp.where(head == h, 2.0 ** (-8.0 * (h + 1) / SWA_HEADS), 0.0) for h in range(SWA_HEADS))
    kn = _rms(ks, kg)
    per_q_head = lambda x: jnp.concatenate([x[h // SWA_G:h // SWA_G + 1] for h in range(SWA_HEADS)], axis=0)
    k8, v8 = per_q_head(kn), per_q_head(vs)
    s = _bdot_nt(_rms(qs, qg), k8) * (HEAD_DIM ** -0.5)
    s = jnp.where(valid, s - slope * dist.astype(F32), NEG)
    m = lax.stop_gradient(jnp.maximum(jnp.max(s, axis=2, keepdims=True), sinks))
    p = jnp.exp(s - m)
    den = jnp.sum(p, axis=2, keepdims=True) + jnp.exp(sinks - m)
    return _bdot(p / den, v8)


def _swa_split(q, kp, kc, vp, vc, sk):
    lanes = lambda x, n: jnp.stack([x[:, h * HEAD_DIM:(h + 1) * HEAD_DIM].astype(F32) for h in range(n)])
    k2, v2 = jnp.concatenate([kp, kc], axis=0), jnp.concatenate([vp, vc], axis=0)
    sinks = jnp.stack([sk[:, h:h + 1] for h in range(SWA_HEADS)])
    return lanes(q, SWA_HEADS), lanes(k2, SWA_KV_HEADS), lanes(v2, SWA_KV_HEADS), sinks


def _swa_join(x):
    return jnp.concatenate([x[h] for h in range(x.shape[0])], axis=1)


def _swa_specs(t):
    qcb = (3 * SB_W) // SWA_QW
    kcb = (3 * SB_W + SWA_QW) // SWA_KVW
    prev = lambda i: jnp.maximum(i - 1, 0)
    return [pl.BlockSpec((WINDOW, SWA_QW), lambda i: (i, qcb)),
            pl.BlockSpec((WINDOW, SWA_KVW), lambda i: (prev(i), kcb)),
            pl.BlockSpec((WINDOW, SWA_KVW), lambda i: (i, kcb)),
            pl.BlockSpec((WINDOW, SWA_KVW), lambda i: (prev(i), kcb + 1)),
            pl.BlockSpec((WINDOW, SWA_KVW), lambda i: (i, kcb + 1)),
            pl.BlockSpec((1, HEAD_DIM), lambda i: (0, 0)),
            pl.BlockSpec((1, HEAD_DIM), lambda i: (0, 0)),
            pl.BlockSpec((1, SWA_HEADS), lambda i: (0, 0))]


def _swa_fwd(proj, qg, kg, sinks, *, name):
    t = proj.shape[0]

    def body(q_ref, kp_ref, kc_ref, vp_ref, vc_ref, qg_ref, kg_ref, sk_ref, o_ref):
        first = pl.program_id(0) == 0
        qs, ks, vs, sk = _swa_split(q_ref[...], kp_ref[...], kc_ref[...], vp_ref[...], vc_ref[...], sk_ref[...])
        o_ref[...] = _swa_join(_swa_heads(first, qs, ks, vs, qg_ref[...], kg_ref[...], sk))

    return pl.pallas_call(
        body, name=name, grid=(t // WINDOW,), in_specs=_swa_specs(t),
        out_specs=pl.BlockSpec((WINDOW, SWA_QW), lambda i: (i, 0)),
        out_shape=jax.ShapeDtypeStruct((t, SWA_QW), F32),
        compiler_params=_params("parallel"),
    )(proj, proj, proj, proj, proj, qg, kg, sinks)


def _swa_bwd(proj, qg, kg, sinks, dout, *, name):
    t = proj.shape[0]

    def body(q_ref, kp_ref, kc_ref, vp_ref, vc_ref, qg_ref, kg_ref, sk_ref, do_ref,
             dq_ref, dk_ref, dv_ref, dqg_ref, dkg_ref, dsk_ref):
        i = pl.program_id(0)
        first = i == 0

        @pl.when(first)
        def _():
            for r in (dk_ref, dv_ref, dqg_ref, dkg_ref, dsk_ref):
                r[...] = jnp.zeros_like(r)

        qs, ks, vs, sk = _swa_split(q_ref[...], kp_ref[...], kc_ref[...], vp_ref[...], vc_ref[...], sk_ref[...])
        do = do_ref[...]
        cts = jnp.stack([do[:, h * HEAD_DIM:(h + 1) * HEAD_DIM] for h in range(SWA_HEADS)])
        _, vjp = jax.vjp(functools.partial(_swa_heads, first), qs, ks, vs, qg_ref[...], kg_ref[...], sk)
        dqs, dks, dvs, dqg, dkg, dsk = vjp(cts)
        dq_ref[...] = _swa_join(dqs)
        dk2, dv2 = _swa_join(dks), _swa_join(dvs)
        cur = pl.ds(pl.multiple_of(i * WINDOW, WINDOW), WINDOW)
        prv = pl.ds(pl.multiple_of(jnp.maximum(i - 1, 0) * WINDOW, WINDOW), WINDOW)
        dk_ref[prv, :] += dk2[:WINDOW]
        dv_ref[prv, :] += dv2[:WINDOW]
        dk_ref[cur, :] += dk2[WINDOW:]
        dv_ref[cur, :] += dv2[WINDOW:]
        dqg_ref[...] += dqg
        dkg_ref[...] += dkg
        dsk_ref[...] += _swa_join(dsk)

    whole = lambda shape: pl.BlockSpec(shape, lambda i: (0, 0))
    return pl.pallas_call(
        body, name=name, grid=(t // WINDOW,),
        in_specs=_swa_specs(t) + [pl.BlockSpec((WINDOW, SWA_QW), lambda i: (i, 0))],
        out_specs=[pl.BlockSpec((WINDOW, SWA_QW), lambda i: (i, 0)), whole((t, SWA_KVW)), whole((t, SWA_KVW)),
                   whole((1, HEAD_DIM)), whole((1, HEAD_DIM)), whole((1, SWA_HEADS))],
        out_shape=[jax.ShapeDtypeStruct((t, SWA_QW), F32), jax.ShapeDtypeStruct((t, SWA_KVW), F32),
                   jax.ShapeDtypeStruct((t, SWA_KVW), F32), jax.ShapeDtypeStruct((1, HEAD_DIM), F32),
                   jax.ShapeDtypeStruct((1, HEAD_DIM), F32), jax.ShapeDtypeStruct((1, SWA_HEADS), F32)],
        compiler_params=_params("arbitrary"),
    )(proj, proj, proj, proj, proj, qg, kg, sinks, dout)


CONV_CB = 512
CONV_TM = 512
HALO = 8


def _conv_pre(x_ref, h_ref, w_ref, i):
    halo = jnp.where(i > 0, h_ref[...], 0.0)
    xe = jnp.concatenate([halo, x_ref[...]], axis=0)
    tm = x_ref.shape[0]
    w = w_ref[...]
    c = sum(w[k:k + 1, :] * xe[HALO - (GDN_CONV - 1) + k:HALO - (GDN_CONV - 1) + k + tm] for k in range(GDN_CONV))
    return c, xe


def _conv_specs(tm, cb):
    return [pl.BlockSpec((tm, cb), lambda c, i: (i, c)),
            pl.BlockSpec((HALO, cb), lambda c, i: (jnp.maximum(i * (tm // HALO) - 1, 0), c)),
            pl.BlockSpec((GDN_CONV, cb), lambda c, i: (0, c))]


def _conv_fwd(x, w, *, name):
    t, ch = x.shape
    tm, cb = _tile(t, CONV_TM), _tile(ch, CONV_CB)

    def body(x_ref, h_ref, w_ref, o_ref):
        c, _ = _conv_pre(x_ref, h_ref, w_ref, pl.program_id(1))
        o_ref[...] = c * jax.nn.sigmoid(c)

    tile = pl.BlockSpec((tm, cb), lambda c, i: (i, c))
    return pl.pallas_call(
        body, name=name, grid=(ch // cb, t // tm), in_specs=_conv_specs(tm, cb), out_specs=tile,
        out_shape=jax.ShapeDtypeStruct((t, ch), F32),
        compiler_params=_params("parallel", "parallel"),
    )(x, x, w)


def _conv_bwd(x, w, dact, *, name):
    t, ch = x.shape
    tm, cb = _tile(t, CONV_TM), _tile(ch, CONV_CB)
    nt = t // tm
    last = GDN_CONV - 1

    def body(x_ref, h_ref, w_ref, xn_ref, da_ref, dan_ref, dx_ref, dw_ref):
        i = pl.program_id(1)

        @pl.when(i == 0)
        def _():
            dw_ref[...] = jnp.zeros_like(dw_ref)

        w = w_ref[...]
        xe = jnp.concatenate([jnp.where(i > 0, h_ref[...], 0.0), x_ref[...], xn_ref[...]], axis=0)
        rows = tm + HALO
        c = sum(w[k:k + 1, :] * xe[HALO - last + k:HALO - last + k + rows] for k in range(GDN_CONV))
        sig = jax.nn.sigmoid(c)
        da = jnp.concatenate([da_ref[...], jnp.where(i < nt - 1, dan_ref[...], 0.0)], axis=0)
        dce = da * (sig * (1.0 + c * (1.0 - sig)))
        dc = dce[:tm]
        dx_ref[...] = sum(w[k:k + 1, :] * dce[last - k:last - k + tm] for k in range(GDN_CONV))
        dw_ref[...] += jnp.concatenate(
            [jnp.sum(dc * xe[HALO - last + k:HALO - last + k + tm], axis=0, keepdims=True) for k in range(GDN_CONV)],
            axis=0)

    tile = pl.BlockSpec((tm, cb), lambda c, i: (i, c))
    nxt = pl.BlockSpec((HALO, cb), lambda c, i: (jnp.minimum((i + 1) * (tm // HALO), t // HALO - 1), c))
    return pl.pallas_call(
        body, name=name, grid=(ch // cb, nt),
        in_specs=_conv_specs(tm, cb) + [nxt, tile, nxt],
        out_specs=[tile, pl.BlockSpec((GDN_CONV, cb), lambda c, i: (0, c))],
        out_shape=[jax.ShapeDtypeStruct((t, ch), F32), jax.ShapeDtypeStruct((GDN_CONV, ch), F32)],
        compiler_params=_params("parallel", "arbitrary"),
    )(x, x, w, x, dact, dact)


def _gdn_chunk(qraw, kraw, v, bl, a, alog, dtb, state, inverse=None, keep_inverse=False):
    c, d = GDN_CHUNK, GDN_HEAD_DIM
    nh = qraw.shape[0]
    ri = lax.broadcasted_iota(jnp.int32, (nh, c, c), 1)
    ci = lax.broadcasted_iota(jnp.int32, (nh, c, c), 2)
    incl, strict = ri >= ci, ri > ci
    q = qraw * lax.rsqrt(jnp.sum(qraw * qraw, axis=-1, keepdims=True) + EPS) * (d ** -0.5)
    k = kraw * lax.rsqrt(jnp.sum(kraw * kraw, axis=-1, keepdims=True) + EPS)
    beta = jax.nn.sigmoid(bl)
    g = -jnp.exp(alog) * jax.nn.softplus(a + dtb)
    gc = _ldot(incl.astype(F32), jnp.broadcast_to(g, (nh, c, d)))
    gcm = gc[:, :, :c]
    decay = jnp.exp(jnp.where(incl, gcm - jnp.swapaxes(gcm, 1, 2), NEG))
    eg = jnp.exp(gc)
    kbeta = k * beta
    x = -jnp.where(strict, _bdot_nt(kbeta, k) * decay, 0.0)
    tinv = _unit_lower_inverse(x) if inverse is None else _known_inverse(x, inverse)
    u = _hdot(tinv, v * beta)
    w = _hdot(tinv, kbeta * eg)
    attn = jnp.where(incl, _bdot_nt(q, k) * decay, 0.0)
    glast = gc[:, c - 1:c, :]
    v_new = u - _bdot(w, state)
    o = _bdot(q * eg, state) + _bdot(attn, v_new)
    state = state * jnp.exp(glast) + _bdot_tn(k * jnp.exp(glast - gc), v_new)
    return (o, state, tinv) if keep_inverse else (o, state)


GDN_REP = GDN_V_HEADS // GDN_K_HEADS
GDN_HB = 8


def _gdn_pick(vals, kh, r):
    ba, alog, dtb = vals
    lane = lax.broadcasted_iota(jnp.int32, ba.shape, 1)
    hv = kh * GDN_REP + r
    bl = jnp.sum(jnp.where(lane == hv, ba, 0.0), axis=1, keepdims=True)
    a = jnp.sum(jnp.where(lane == GDN_V_HEADS + hv, ba, 0.0), axis=1, keepdims=True)
    lane1 = lax.broadcasted_iota(jnp.int32, alog.shape, 1)
    al = jnp.sum(jnp.where(lane1 == hv, alog, 0.0), axis=1, keepdims=True)
    db = jnp.sum(jnp.where(lane1 == hv, dtb, 0.0), axis=1, keepdims=True)
    return bl, a, al, db


def _gdn_stack(qs, ks, vs, small, j):
    d = GDN_HEAD_DIM
    per = [[], [], [], [], [], [], []]
    for hh in range(GDN_HB):
        q, k = qs[:, hh * d:(hh + 1) * d], ks[:, hh * d:(hh + 1) * d]
        for r in range(GDN_REP):
            col = (hh * GDN_REP + r) * d
            for lst, val in zip(per, (q, k, vs[:, col:col + d]) + _gdn_pick(small, j * GDN_HB + hh, r)):
                lst.append(val)
    return tuple(jnp.stack(lst) for lst in per)


def _gdn_specs(nchunk, rev):
    c, d = GDN_CHUNK, GDN_HEAD_DIM
    at = (lambda n: nchunk - 1 - n) if rev else (lambda n: n)
    ng = GDN_K_HEADS // GDN_HB
    return at, [pl.BlockSpec((c, GDN_HB * d), lambda n, j: (at(n), j)),
                pl.BlockSpec((c, GDN_HB * d), lambda n, j: (at(n), ng + j)),
                pl.BlockSpec((c, GDN_HB * GDN_REP * d), lambda n, j: (at(n), ng + j)),
                pl.BlockSpec((c, 2 * GDN_V_HEADS), lambda n, j: (at(n), 0)),
                pl.BlockSpec((1, GDN_V_HEADS), lambda n, j: (0, 0)),
                pl.BlockSpec((1, GDN_V_HEADS), lambda n, j: (0, 0))]


def _gdn_fwd(act, ba, alog, dtb, *, name):
    t = act.shape[0]
    c, d = GDN_CHUNK, GDN_HEAD_DIM
    nchunk = t // c
    at, specs = _gdn_specs(nchunk, False)

    def body(q_ref, k_ref, v_ref, ba_ref, al_ref, db_ref, o_ref, s_ref, inv_ref, state):
        n, j = pl.program_id(0), pl.program_id(1)
        heads = pl.ds(j * GDN_HB, GDN_HB)

        @pl.when(n == 0)
        def _():
            state[heads] = jnp.zeros((GDN_HB, GDN_REP, d, d), F32)

        s_in = state[heads]
        s_ref[...] = s_in
        args = _gdn_stack(q_ref[...], k_ref[...], v_ref[...], (ba_ref[...], al_ref[...], db_ref[...]), j)
        o, s_new, inv_ref[...] = _gdn_chunk(*args, s_in.reshape(GDN_HB * GDN_REP, d, d), keep_inverse=True)
        o_ref[...] = jnp.concatenate([o[b] for b in range(GDN_HB * GDN_REP)], axis=1)
        state[heads] = s_new.reshape(GDN_HB, GDN_REP, d, d)

    return pl.pallas_call(
        body, name=name, grid=(nchunk, GDN_K_HEADS // GDN_HB), in_specs=specs,
        out_specs=[pl.BlockSpec((c, GDN_HB * GDN_REP * d), lambda n, j: (n, j)),
                   pl.BlockSpec((None, GDN_HB, GDN_REP, d, d), lambda n, j: (n, j, 0, 0, 0)),
                   pl.BlockSpec((None, GDN_HB * GDN_REP, c, c), lambda n, j: (n, j, 0, 0))],
        out_shape=[jax.ShapeDtypeStruct((t, GDN_VW), F32),
                   jax.ShapeDtypeStruct((nchunk, GDN_K_HEADS, GDN_REP, d, d), F32),
                   jax.ShapeDtypeStruct((nchunk, GDN_V_HEADS, c, c), F32)],
        scratch_shapes=[pltpu.VMEM((GDN_K_HEADS, GDN_REP, d, d), F32)],
        compiler_params=_params("arbitrary", "arbitrary"),
    )(act, act, act, ba, alog, dtb)


def _gdn_bwd(act, ba, alog, dtb, states, inverses, dout, *, name):
    assert GDN_HB == GDN_K_HEADS
    t = act.shape[0]
    c, d = GDN_CHUNK, GDN_HEAD_DIM
    nchunk = t // c
    at, specs = _gdn_specs(nchunk, True)

    def body(q_ref, k_ref, v_ref, ba_ref, al_ref, db_ref, s_ref, inv_ref, do_ref,
             dact_ref, dba_ref, dal_ref, ddb_ref, dstate):
        n, j = pl.program_id(0), pl.program_id(1)

        @pl.when(n == 0)
        def _():
            dstate[pl.ds(j * GDN_HB, GDN_HB)] = jnp.zeros((GDN_HB, GDN_REP, d, d), F32)

        @pl.when((n == 0) & (j == 0))
        def _():
            dal_ref[...] = jnp.zeros_like(dal_ref)
            ddb_ref[...] = jnp.zeros_like(ddb_ref)

        @pl.when(j == 0)
        def _():
            dba_ref[...] = jnp.zeros_like(dba_ref)

        heads = pl.ds(j * GDN_HB, GDN_HB)
        nh = GDN_HB * GDN_REP
        args = _gdn_stack(q_ref[...], k_ref[...], v_ref[...], (ba_ref[...], al_ref[...], db_ref[...]), j)
        _, vjp = jax.vjp(functools.partial(_gdn_chunk, inverse=inv_ref[...]), *args, s_ref[...].reshape(nh, d, d))
        do = do_ref[...]
        do = jnp.stack([do[:, b * d:(b + 1) * d] for b in range(nh)])
        gq, gk, gv, gbl, ga, gal, gdb, gs = vjp((do, dstate[heads].reshape(nh, d, d)))
        dstate[heads] = gs.reshape(GDN_HB, GDN_REP, d, d)
        dact_ref[...] = jnp.concatenate([gq[GDN_REP * hh] + gq[GDN_REP * hh + 1] for hh in range(GDN_HB)]
                                        + [gk[GDN_REP * hh] + gk[GDN_REP * hh + 1] for hh in range(GDN_HB)]
                                        + [gv[b] for b in range(nh)], axis=1)
        lane = lax.broadcasted_iota(jnp.int32, (c, 2 * GDN_V_HEADS), 1)
        lane1 = lax.broadcasted_iota(jnp.int32, (1, GDN_V_HEADS), 1)
        dba = jnp.zeros((c, 2 * GDN_V_HEADS), F32)
        dal = jnp.zeros((1, GDN_V_HEADS), F32)
        ddb = jnp.zeros((1, GDN_V_HEADS), F32)
        for b in range(nh):
            hv = j * nh + b
            dba = dba + jnp.where(lane == hv, gbl[b], 0.0) + jnp.where(lane == GDN_V_HEADS + hv, ga[b], 0.0)
            dal = dal + jnp.where(lane1 == hv, gal[b], 0.0)
            ddb = ddb + jnp.where(lane1 == hv, gdb[b], 0.0)
        dba_ref[...] += dba
        dal_ref[...] += dal
        ddb_ref[...] += ddb

    small = pl.BlockSpec((1, GDN_V_HEADS), lambda n, j: (0, 0))
    return pl.pallas_call(
        body, name=name, grid=(nchunk, GDN_K_HEADS // GDN_HB),
        in_specs=specs + [pl.BlockSpec((None, GDN_HB, GDN_REP, d, d), lambda n, j: (at(n), j, 0, 0, 0)),
                          pl.BlockSpec((None, GDN_HB * GDN_REP, c, c), lambda n, j: (at(n), j, 0, 0)),
                          pl.BlockSpec((c, GDN_HB * GDN_REP * d), lambda n, j: (at(n), j))],
        out_specs=[pl.BlockSpec((c, GDN_CONV_W), lambda n, j: (at(n), 0)),
                   pl.BlockSpec((c, 2 * GDN_V_HEADS), lambda n, j: (at(n), 0)),
                   small, small],
        out_shape=[jax.ShapeDtypeStruct((t, GDN_CONV_W), F32), jax.ShapeDtypeStruct((t, 2 * GDN_V_HEADS), F32),
                   jax.ShapeDtypeStruct((1, GDN_V_HEADS), F32), jax.ShapeDtypeStruct((1, GDN_V_HEADS), F32)],
        scratch_shapes=[pltpu.VMEM((GDN_K_HEADS, GDN_REP, d, d), F32)],
        compiler_params=_params("arbitrary", "arbitrary"),
    )(act, act, act, ba, alog, dtb, states, inverses, dout)


N_DEV = 8
ANY = pl.BlockSpec(memory_space=pl.ANY)


def _coords():
    return lax.axis_index("x"), lax.axis_index("y"), lax.axis_index("c")


def _other_chips(x, y):
    return [(1 - x, y), (x, 1 - y), (1 - x, 1 - y)]


def _remote(src, dst, send_sems, recv_sems, k, to):
    return pltpu.make_async_remote_copy(src_ref=src, dst_ref=dst, send_sem=send_sems.at[k], recv_sem=recv_sems.at[k],
                                        device_id=to, device_id_type=MESH)


def _dma_sems(n):
    return [pltpu.SemaphoreType.DMA((n,)), pltpu.SemaphoreType.DMA((n,))]


def _gather_copies(ins, outs, send_sems, recv_sems, local_sems, only_first=False):
    x, y, c = _coords()
    sibling = (x, y, 1 - c)
    local, sends, arrivals, relays, relayed = [], [], [], [], []
    for a, (x_ref, out_ref) in enumerate(zip(ins, outs)):
        local.append(pltpu.make_async_copy(x_ref, out_ref.at[2 * x + y], local_sems.at[a]))
        for j, (cx, cy) in enumerate(_other_chips(x, y)):
            k, theirs = 6 * a + j, 2 * cx + cy
            sends.append(_remote(x_ref.at[c], out_ref.at[2 * x + y, c], send_sems, recv_sems, k, (cx, cy, c)))
            if only_first:
                continue
            arrivals.append(_remote(x_ref.at[c], out_ref.at[theirs, c], send_sems, recv_sems, k, (cx, cy, c)))
            relays.append(_remote(out_ref.at[theirs, c], out_ref.at[theirs, c], send_sems, recv_sems, k + 3, sibling))
            relayed.append(_remote(x_ref.at[c], out_ref.at[theirs, 1 - c], send_sems, recv_sems, k + 3, sibling))
    return local, sends, arrivals, relays, relayed


def _gather_start(copies):
    local, sends, _, _, _ = copies
    for cp in local + sends:
        cp.start()


def _gather_finish(copies):
    local, sends, arrivals, relays, relayed = copies
    for landed, relay in zip(arrivals, relays):
        landed.wait_recv()
        relay.start()
    for cp in relayed:
        cp.wait_recv()
    for cp in sends + relays:
        cp.wait_send()
    for cp in local:
        cp.wait()


def _gather_scratch(na):
    return _dma_sems(6 * na) + [pltpu.SemaphoreType.DMA((na,))]


def _gather_quarters(parts, *, name):
    na = len(parts)

    def body(*refs):
        copies = _gather_copies(refs[:na], refs[na:2 * na], *refs[2 * na:])
        _gather_start(copies)
        _gather_finish(copies)

    return pl.pallas_call(
        body, name=name, in_specs=[ANY] * na, out_specs=[ANY] * na,
        out_shape=[jax.ShapeDtypeStruct((N_CHIPS,) + p.shape, p.dtype) for p in parts],
        scratch_shapes=_gather_scratch(na),
    )(*parts)


def _swap_halves(grads, *, name):
    na = len(grads)

    def body(*refs):
        ins, outs = refs[:na], refs[na:2 * na]
        send_sems, recv_sems = refs[2 * na:]
        x, y, c = _coords()
        sends = [_remote(g_ref.at[j, 1 - c], o_ref.at[j], send_sems, recv_sems, N_CHIPS * a + j, (x, y, 1 - c))
                 for a, (g_ref, o_ref) in enumerate(zip(ins, outs)) for j in range(N_CHIPS)]
        for cp in sends:
            cp.start()
        for cp in sends:
            cp.wait()

    return pl.pallas_call(
        body, name=name, in_specs=[ANY] * na, out_specs=[ANY] * na,
        out_shape=[jax.ShapeDtypeStruct((N_CHIPS,) + g.shape[2:], g.dtype) for g in grads],
        scratch_shapes=_dma_sems(N_CHIPS * na),
    )(*grads)


def _scatter_copies(ins, outs, send_sems, recv_sems):
    x, y, c = _coords()
    return [_remote(p_ref.at[2 * cx + cy], outs[3 * a + j], send_sems, recv_sems, 3 * a + j, (cx, cy, c))
            for a, p_ref in enumerate(ins) for j, (cx, cy) in enumerate(_other_chips(x, y))]


def _scatter_shapes(pairs):
    return [jax.ShapeDtypeStruct(p.shape[1:], p.dtype) for p in pairs for _ in range(3)]


def _scatter_quarters(pairs, *, name):
    na = len(pairs)

    def body(*refs):
        sends = _scatter_copies(refs[:na], refs[na:4 * na], *refs[4 * na:])
        for cp in sends:
            cp.start()
        for cp in sends:
            cp.wait()

    out = pl.pallas_call(
        body, name=name, in_specs=[ANY] * na, out_specs=[ANY] * (3 * na), out_shape=_scatter_shapes(pairs),
        scratch_shapes=_dma_sems(3 * na),
    )(*pairs)
    return [out[3 * a:3 * a + 3] for a in range(na)]


def _share_halves(tots, *, name):
    na = len(tots)

    def body(*refs):
        ins, outs = refs[:na], refs[na:2 * na]
        send_sems, recv_sems = refs[2 * na:]
        x, y, c = _coords()
        sends = [_remote(t_ref, o_ref, send_sems, recv_sems, a, (x, y, 1 - c))
                 for a, (t_ref, o_ref) in enumerate(zip(ins, outs))]
        for cp in sends:
            cp.start()
        for cp in sends:
            cp.wait()

    return pl.pallas_call(
        body, name=name, in_specs=[ANY] * na, out_specs=[ANY] * na,
        out_shape=[jax.ShapeDtypeStruct(t.shape, t.dtype) for t in tots],
        scratch_shapes=_dma_sems(na),
    )(*tots)


def _gather_all(vec, *, name):
    m, w = vec.shape

    def body(x_ref, out_ref, send_sems, recv_sems, local_sem):
        x, y, c = _coords()
        me, sibling = (x, y, c), (x, y, 1 - c)
        chips = _other_chips(x, y)

        def rows(px, py, pc):
            return out_ref.at[pl.ds((4 * px + 2 * py + pc) * m, m), :]

        def copy(k, block, to, src=None):
            return _remote(rows(*block) if src is None else src, rows(*block), send_sems, recv_sems, k, to)

        mine = pltpu.make_async_copy(x_ref, rows(*me), local_sem)
        mine.start()
        first = [copy(0, me, sibling, src=x_ref)]
        first += [copy(1 + j, me, (*chip, c), src=x_ref) for j, chip in enumerate(chips)]
        for cp in first:
            cp.start()
        passed = [copy(4 + j, (*chip, c), sibling) for j, chip in enumerate(chips)]
        for j, chip in enumerate(chips):
            copy(1 + j, (*chip, c), me).wait_recv()
            passed[j].start()
        copy(0, sibling, me).wait_recv()
        for j, chip in enumerate(chips):
            copy(4 + j, (*chip, 1 - c), me).wait_recv()
        for cp in first + passed:
            cp.wait_send()
        mine.wait()

    vm = pl.BlockSpec(memory_space=pltpu.VMEM)
    return pl.pallas_call(
        body, name=name, in_specs=[vm], out_specs=vm, out_shape=jax.ShapeDtypeStruct((N_DEV * m, w), vec.dtype),
        scratch_shapes=_dma_sems(7) + [pltpu.SemaphoreType.DMA(())],
    )(vec)


def _sum_blocks(allv, n, *, name):
    m = allv.shape[0] // n

    def body(a_ref, o_ref):
        acc = a_ref[0:m, :]
        for d in range(1, n):
            acc = acc + a_ref[d * m:(d + 1) * m, :]
        o_ref[...] = acc

    return pl.pallas_call(body, name=name, out_shape=jax.ShapeDtypeStruct((m, allv.shape[1]), allv.dtype))(allv)


EW_BLOCK_BYTES = 1 << 20


def _ew_rows(rows, w):
    return _tile(rows, max(8, (EW_BLOCK_BYTES // (4 * w)) // 8 * 8), 8)


def _add_pair(g, got, c, *, name):
    _, _, rows, w = g.shape
    tr = _ew_rows(rows, w)

    def body(c_ref, g_ref, got_ref, o_ref):
        o_ref[...] = (g_ref[...] + got_ref[...]).astype(o_ref.dtype)

    blk = pl.BlockSpec((None, tr, w), lambda q, i, c_ref: (q, i, 0))
    return pl.pallas_call(
        body, name=name,
        grid_spec=pltpu.PrefetchScalarGridSpec(
            num_scalar_prefetch=1, grid=(N_CHIPS, rows // tr),
            in_specs=[pl.BlockSpec((None, None, tr, w), lambda q, i, c_ref: (q, c_ref[0], i, 0)), blk], out_specs=blk),
        out_shape=jax.ShapeDtypeStruct(got.shape, BF16),
        compiler_params=_params("parallel", "parallel"),
    )(c, g, got)


def _add_chips(pair, recv, chip, *, name):
    _, rows, w = pair.shape
    tr = _ew_rows(rows, w)

    def body(chip_ref, p_ref, r0_ref, r1_ref, r2_ref, o_ref):
        f = lambda r: r[...].astype(F32)
        o_ref[...] = ((f(p_ref) + f(r0_ref)) + f(r1_ref)) + f(r2_ref)

    blk = pl.BlockSpec((tr, w), lambda i, chip_ref: (i, 0))
    return pl.pallas_call(
        body, name=name,
        grid_spec=pltpu.PrefetchScalarGridSpec(
            num_scalar_prefetch=1, grid=(rows // tr,),
            in_specs=[pl.BlockSpec((None, tr, w), lambda i, chip_ref: (chip_ref[0], i, 0)), blk, blk, blk], out_specs=blk),
        out_shape=jax.ShapeDtypeStruct((rows, w), F32),
        compiler_params=_params("parallel"),
    )(chip, pair, *recv)


def _adamw_math(w, g, m, v):
    nm = ADAM_B1 * m + (1.0 - ADAM_B1) * g
    nv = ADAM_B2 * v + (1.0 - ADAM_B2) * (g * g)
    m_hat = nm / (1.0 - ADAM_B1 ** ADAM_STEP)
    v_hat = nv / (1.0 - ADAM_B2 ** ADAM_STEP)
    return -ADAM_LR * (m_hat / (jnp.sqrt(v_hat) + ADAM_EPS) + ADAM_WD * w), nm, nv


def _adamw(w, g, m, v, *, name):
    shape = w.shape
    last = shape[-1]
    w2, g2, m2, v2 = (a.reshape(-1, last) for a in (w, g, m, v))
    rows = w2.shape[0]
    tm = _ew_rows(rows, last)

    def body(w_ref, g_ref, m_ref, v_ref, d_ref, nm_ref, nv_ref):
        d_ref[...], nm_ref[...], nv_ref[...] = _adamw_math(w_ref[...], g_ref[...], m_ref[...], v_ref[...])

    spec = pl.BlockSpec((tm, last), lambda i: (i, 0))
    out = jax.ShapeDtypeStruct((rows, last), F32)
    d, nm, nv = pl.pallas_call(
        body, name=name, grid=(rows // tm,), in_specs=[spec] * 4, out_specs=[spec] * 3, out_shape=[out] * 3,
        compiler_params=_params("parallel"),
    )(w2, g2, m2, v2)
    return d.reshape(shape), nm.reshape(shape), nv.reshape(shape)


def _adamw_halves(w, m, v, mine, theirs, c, *, name, into=None):
    rows, wd = w.shape[-2:]
    tr = _ew_rows(rows, wd)
    bufs, at = into if into is not None else ((), ())

    def body(c_ref, w_ref, m_ref, v_ref, a_ref, b_ref, *rest):
        g_ref, d_ref, nm_ref, nv_ref = rest[len(bufs):]
        g = jnp.where(pl.program_id(0) == c_ref[0], a_ref[...], b_ref[...])
        g_ref[...] = g
        d_ref[...], nm_ref[...], nv_ref[...] = _adamw_math(w_ref[...], g, m_ref[...], v_ref[...])

    full = pl.BlockSpec((None,) * (1 + len(at)) + (tr, wd), lambda hf, i, c_ref: at + (hf, i, 0))
    half = pl.BlockSpec((tr, wd), lambda hf, i, c_ref: (i, 0))
    out = jax.ShapeDtypeStruct(w.shape, F32)
    return pl.pallas_call(
        body, name=name,
        grid_spec=pltpu.PrefetchScalarGridSpec(num_scalar_prefetch=1, grid=(2, rows // tr),
                                               in_specs=[full] * 3 + [half] * 2 + [ANY] * len(bufs),
                                               out_specs=[full] * 4),
        out_shape=[out] * 4, input_output_aliases={6 + b: b for b in range(len(bufs))},
        compiler_params=_params("parallel", "parallel"),
    )(c, w, m, v, mine, theirs, *bufs)


def _join_quarters(q, *, name):
    _, rows, n = q.shape
    tr = _tile(rows, 256, 16)

    def body(q_ref, o_ref):
        o_ref[...] = jnp.concatenate([q_ref[s] for s in range(N_CHIPS)], axis=1)

    return pl.pallas_call(
        body, name=name, grid=(rows // tr,),
        in_specs=[pl.BlockSpec((N_CHIPS, tr, n), lambda i: (0, i, 0))],
        out_specs=pl.BlockSpec((tr, N_CHIPS * n), lambda i: (i, 0)),
        out_shape=jax.ShapeDtypeStruct((rows, N_CHIPS * n), q.dtype),
        compiler_params=_params("parallel"),
    )(q)


def _split_quarters(pieces, *, name):
    rows = pieces[0].shape[0]
    n = sum(p.shape[1] for p in pieces) // N_CHIPS
    tr = _tile(rows, 256, 16)

    def body(*refs):
        x = jnp.concatenate([r[...] for r in refs[:-1]], axis=1)
        for s in range(N_CHIPS):
            refs[-1][s] = x[:, s * n:(s + 1) * n]

    return pl.pallas_call(
        body, name=name, grid=(rows // tr,),
        in_specs=[pl.BlockSpec((tr, p.shape[1]), lambda i: (i, 0)) for p in pieces],
        out_specs=pl.BlockSpec((N_CHIPS, tr, n), lambda i: (0, i, 0)),
        out_shape=jax.ShapeDtypeStruct((N_CHIPS, rows, n), pieces[0].dtype),
        compiler_params=_params("parallel"),
    )(*pieces)


_WEIGHTS = ['ffn_norm', 'ffn_w_gate', 'ffn_w_up', 'ffn_w_down', 'mix_norm', 'att_w_in', 'att_q_norm', 'att_k_norm',
            'att_sinks', 'att_w_out', 'gdn_w_in', 'gdn_conv_w', 'gdn_a_log', 'gdn_dt_bias', 'gdn_out_norm', 'gdn_w_out',
            'ple_norm', 'ple_w_gate', 'ple_w_proj']
_BIG = ['ffn_w_gate', 'ffn_w_up', 'ffn_w_down', 'att_w_in', 'att_w_out', 'gdn_w_in', 'gdn_w_out', 'ple_w_gate',
        'ple_w_proj']
_SMALL_CUT = {'ffn_norm': 2, 'gdn_conv_w': 2}
_WHOLE = ['mix_norm', 'att_q_norm', 'att_k_norm', 'att_sinks', 'gdn_a_log', 'gdn_dt_bias', 'gdn_out_norm', 'ple_norm']
PACK_W = 1024
SMALL_ROW_MULT = 8


def _halves(a):
    return a.reshape(2, -1, a.shape[-1])


def _from_quarters(blk, axis):
    full = jnp.moveaxis(blk, 0, axis)
    shp = list(full.shape)
    shp[axis:axis + 2] = [shp[axis] * shp[axis + 1]]
    return full.reshape(shp)


def _to_quarters(full, axis):
    shp = list(full.shape)
    shp[axis:axis + 1] = [N_CHIPS, shp[axis] // N_CHIPS]
    return jnp.moveaxis(full.reshape(shp), axis, 0)


def _pack(parts, row_mult):
    flat = jnp.concatenate(parts, axis=-1)
    n = flat.shape[-1]
    rows = -(-n // (PACK_W * row_mult)) * row_mult
    return jnp.pad(flat, [(0, rows * PACK_W - n)]).reshape(rows, PACK_W)


def _unpack(flat, shapes):
    lead = flat.shape[:-2]
    flat = flat.reshape(lead + (-1,))
    out, off = [], 0
    for shp in shapes:
        n = math.prod(shp)
        out.append(flat[..., off:off + n].reshape(lead + tuple(shp)))
        off += n
    return out


FFN_TM = 1024


def _ffn_up(hn, wg, wu, at, *, name):
    t, d = hn.shape
    fq = wg.shape[-1]
    tm = _tile(t, FFN_TM)

    def body(h_ref, wg_ref, wu_ref, g_ref, u_ref, a_ref):
        h = h_ref[...]
        g, u = _dg(h, _b(wg_ref[...]), 1, 0), _dg(h, _b(wu_ref[...]), 1, 0)
        g_ref[...] = g.astype(BF16)
        u_ref[...] = u.astype(BF16)
        a_ref[...] = _f_swiglu(g, u)[0].astype(BF16)

    w_spec = pl.BlockSpec((None,) * (1 + len(at)) + (d, fq), lambda s, i: (s,) + at + (0, 0))
    o_spec = pl.BlockSpec((None, tm, fq), lambda s, i: (s, i, 0))
    out = jax.ShapeDtypeStruct((N_CHIPS, t, fq), BF16)
    return pl.pallas_call(
        body, name=name, grid=(N_CHIPS, t // tm),
        in_specs=[pl.BlockSpec((tm, d), lambda s, i: (i, 0)), w_spec, w_spec], out_specs=[o_spec] * 3,
        out_shape=[out] * 3, compiler_params=_params("parallel", "parallel"),
    )(hn, wg, wu)


def _ffn_d_up(dout, wd, g, u, at, *, name):
    t, d = dout.shape
    fq = wd.shape[-2]
    tm = _tile(t, FFN_TM)

    def body(do_ref, wd_ref, g_ref, u_ref, dg_ref, du_ref):
        da = _dg(_b(do_ref[...]), _b(wd_ref[...]), 1, 1) * 0.5
        _, vjp = jax.vjp(_f_swiglu, g_ref[...].astype(F32), u_ref[...].astype(F32))
        dg, du = vjp((da,))
        dg_ref[...] = dg.astype(BF16)
        du_ref[...] = du.astype(BF16)

    w_spec = pl.BlockSpec((None,) * (1 + len(at)) + (fq, d), lambda i, s: (s,) + at + (0, 0))
    o_spec = pl.BlockSpec((None, tm, fq), lambda i, s: (s, i, 0))
    out = jax.ShapeDtypeStruct((N_CHIPS, t, fq), BF16)
    return pl.pallas_call(
        body, name=name, grid=(t // tm, N_CHIPS),
        in_specs=[pl.BlockSpec((tm, d), lambda i, s: (i, 0)), w_spec, o_spec, o_spec], out_specs=[o_spec] * 2,
        out_shape=[out] * 2, compiler_params=_params("parallel", "parallel"),
    )(dout, wd, g, u)


def _ffn_fwd(h, gain, wg, wu, wd, at, tag):
    lead = (Q,) + at
    hn, = _row_fwd(_f_rms, [h], [gain], [(D_MODEL, BF16)], name=f"{tag}_norm")
    g, u, a = _ffn_up(hn, wg, wu, at, name=f"{tag}_up")
    out = _mm((a, (Q,)), (wd, lead), res=h, scale=0.5, name=f"{tag}_down")
    return out, (h, hn, g, u, a)


def _ffn_bwd(dout, saved, gain, wg, wu, wd, at, grads, g_at, tag):
    h, hn, g, u, a = saved
    lead = (Q,) + at
    g_lead = (Q,) + g_at
    dg, du = _ffn_d_up(dout, wd, g, u, at, name=f"{tag}_d_up")
    g_gate, g_up, g_down = grads
    g_down = _mm((a, (Q,)), dout, ta=True, scale=0.5, into=(g_down, g_lead), name=f"{tag}_dw_down")
    g_gate = _mm((dg, (Q,)), hn, ta=True, into=(g_gate, g_lead), name=f"{tag}_dw_gate")
    g_up = _mm((du, (Q,)), hn, ta=True, into=(g_up, g_lead), name=f"{tag}_dw_up")
    dhn = _mm((dg, (Q,)), (wg, lead), tb=True, name=f"{tag}_d_norm_gate")
    dh, dgain = _mm((du, (Q,)), (wu, lead), tb=True, res=dhn, norm_bwd=(h, gain, dout), name=f"{tag}_d_in")
    return dh, dgain, (g_gate, g_up, g_down)


def _att_fwd(h, gain, w_in, qg, kg, sinks, w_out, gather):
    hn, = _row_fwd(_f_rms, [h], [gain], [(D_MODEL, BF16)], name="att_norm")
    proj = _mm(hn, w_in, out_dtype=BF16, name="att_in")
    a, rtot, *gathered = _sb_fwd(proj, name="att_sb", gather=gather)
    b = _swa_fwd(proj, qg, kg, sinks, name="att_swa")
    out = _mm(a, (w_out, (0,)), res=h, name="att_out_sb")
    out = _mm(b, (w_out, (1,)), res=out, name="att_out_swa")
    return out, (h, hn, proj, a, rtot, b), gathered


def _att_bwd(dout, saved, gain, w_in, qg, kg, sinks, w_out, scatter):
    h, hn, proj, a, rtot, b = saved
    da = _mm(dout, (w_out, (0,)), tb=True, name="att_d_sb")
    db = _mm(dout, (w_out, (1,)), tb=True, name="att_d_swa")
    dw_out = lax.empty(w_out.shape, F32)
    dw_out = _mm(a, dout, ta=True, into=(dw_out, (0,)), name="att_dw_out_sb")
    dw_out = _mm(b, dout, ta=True, into=(dw_out, (1,)), name="att_dw_out_swa")
    dq, dk, dv, *landed = _sb_bwd(proj, rtot, da, name="att_sb_bwd", scatter=scatter)
    dqb, dkb, dvb, dqg, dkg, dsk = _swa_bwd(proj, qg, kg, sinks, db, name="att_swa_bwd")
    dproj = jnp.concatenate([dq, dk, dv, dqb, dkb, dvb], axis=1)
    dw_in = _mm(hn, dproj, ta=True, name="att_dw_in")
    dh, dgain = _mm(dproj, w_in, tb=True, norm_bwd=(h, gain, dout), name="att_d_in")
    return dh, dgain, dw_in, dqg, dkg, dsk, dw_out, [landed[3 * a:3 * a + 3] for a in range(len(scatter))]


def _gdn_layer_fwd(h, gain, w_in, conv_w, alog, dtb, out_gain, w_out):
    w_qkv, w_z, w_ba = w_in[:, :GDN_CONV_W], w_in[:, GDN_CONV_W:GDN_CONV_W + GDN_VW], w_in[:, GDN_CONV_W + GDN_VW:]
    hn, = _row_fwd(_f_rms, [h], [gain], [(D_MODEL, BF16)], name="gdn_norm")
    pq = _mm(hn, w_qkv, name="gdn_in_qkv")
    pz = _mm(hn, w_z, name="gdn_in_z")
    ba = _mm(hn, w_ba, name="gdn_in_ba")
    act = _conv_fwd(pq, conv_w, name="gdn_conv")
    o, states, inverses = _gdn_fwd(act, ba, alog, dtb, name="gdn_rule")
    y, = _row_fwd(_f_gdn_out, [o, pz], [out_gain], [(GDN_VW, BF16)], name="gdn_gate")
    out = _mm(y, w_out, res=h, name="gdn_out")
    return out, (h, hn, pq, pz, ba, act, o, states, inverses, y, (w_qkv, w_z, w_ba))


def _gdn_layer_bwd(dout, saved, gain, conv_w, alog, dtb, out_gain, w_out):
    h, hn, pq, pz, ba, act, o, states, inverses, y, (w_qkv, w_z, w_ba) = saved
    dy = _mm(dout, w_out, tb=True, name="gdn_d_gate")
    dw_out = _mm(y, dout, ta=True, name="gdn_dw_out")
    do, dpz, dout_gain = _row_bwd(_f_gdn_out, [o, pz], [out_gain], [dy], [(0, F32), (1, F32)], [0], name="gdn_gate_bwd")
    dact, dba, dal, ddb = _gdn_bwd(act, ba, alog, dtb, states, inverses, do, name="gdn_rule_bwd")
    dpq, dconv = _conv_bwd(pq, conv_w, dact, name="gdn_conv_bwd")
    dw_in = [_mm(hn, dpq, ta=True, name="gdn_dw_qkv"), _mm(hn, dpz, ta=True, name="gdn_dw_z"),
             _mm(hn, dba, ta=True, name="gdn_dw_ba")]
    dhn = _mm(dpq, w_qkv, tb=True, name="gdn_d_norm_qkv")
    dhn = _mm(dpz, w_z, tb=True, res=dhn, name="gdn_d_norm_z")
    dh, dgain = _mm(dba, w_ba, tb=True, res=dhn, norm_bwd=(h, gain, dout), name="gdn_d_in")
    return dh, dgain, dw_in, dconv, dal, ddb, dout_gain, dw_out


def _ple_fwd(h, gain, w_gate, w_proj, pe, tag):
    hn, = _row_fwd(_f_rms, [h], [gain], [(D_MODEL, BF16)], name=f"{tag}_norm")
    gl = _mm(hn, w_gate, name=f"{tag}_gate")
    pp = _mm(pe, w_proj, name=f"{tag}_proj")
    out, = _row_fwd(_f_ple, [h, gl, pp], [], [(D_MODEL, F32)], name=f"{tag}_mix")
    return out, (h, hn, gl, pp)


def _ple_bwd(dout, saved, gain, w_gate, pe, tag):
    h, hn, gl, pp = saved
    dha, dgl, dpp = _row_bwd(_f_ple, [h, gl, pp], [], [dout], [(0, F32), (1, BF16), (2, BF16)], [], name=f"{tag}_mix_bwd")
    dw_gate = _mm(hn, dgl, ta=True, name=f"{tag}_dw_gate")
    dw_proj = _mm(pe, dpp, ta=True, name=f"{tag}_dw_proj")
    dh, dgain = _mm(dgl, w_gate, tb=True, norm_bwd=(h, gain, dha), name=f"{tag}_d_in")
    return dh, dgain, dw_gate, dw_proj


def kernel(x, p, ffn_norm, ffn_w_gate, ffn_w_up, ffn_w_down, mix_norm, att_w_in, att_q_norm, att_k_norm, att_sinks, att_w_out, gdn_w_in, gdn_conv_w, gdn_a_log, gdn_dt_bias, gdn_out_norm, gdn_w_out, ple_norm, ple_w_gate, ple_w_proj, loss_target, m_ffn_norm, m_ffn_w_gate, m_ffn_w_up, m_ffn_w_down, m_mix_norm, m_att_w_in, m_att_q_norm, m_att_k_norm, m_att_sinks, m_att_w_out, m_gdn_w_in, m_gdn_conv_w, m_gdn_a_log, m_gdn_dt_bias, m_gdn_out_norm, m_gdn_w_out, m_ple_norm, m_ple_w_gate, m_ple_w_proj, v_ffn_norm, v_ffn_w_gate, v_ffn_w_up, v_ffn_w_down, v_mix_norm, v_att_w_in, v_att_q_norm, v_att_k_norm, v_att_sinks, v_att_w_out, v_gdn_w_in, v_gdn_conv_w, v_gdn_a_log, v_gdn_dt_bias, v_gdn_out_norm, v_gdn_w_out, v_ple_norm, v_ple_w_gate, v_ple_w_proj):
    arg = dict(locals())
    cx, cy, cc = _coords()
    chip = (2 * cx + cy).astype(jnp.int32).reshape(1)
    core = cc.astype(jnp.int32).reshape(1)
    n_layers = ffn_norm.shape[0]

    quarter = lambda n, i=None: _halves((arg[n] if i is None else arg[n][i]).astype(BF16))
    ffn_names = ('ffn_w_gate', 'ffn_w_up', 'ffn_w_down')
    early = [quarter(n, 0) for n in ffn_names] + [quarter('att_w_in'), quarter('att_w_out')]
    late_names = ('gdn_w_in', 'gdn_w_out', 'ple_w_gate', 'ple_w_proj')
    late = [quarter(n, 1) for n in ffn_names] + [quarter(n) for n in late_names]
    *ffn_w0, att_in_q, att_out_q = _gather_quarters(early, name="gather_weights")
    wt = {'att_w_in': _join_quarters(att_in_q.reshape((N_CHIPS,) + att_w_in.shape[1:]), name="att_w_in_join"),
          'att_w_out': att_out_q.reshape(2, SB_W, D_MODEL)}

    small_names = list(_SMALL_CUT)
    small_shapes = [arg[n].shape for n in small_names]
    svec = _pack([arg[n].reshape(-1) for n in small_names], SMALL_ROW_MULT)
    srows = svec.shape[0]
    sall = _gather_all(svec, name="gather_gains").reshape(N_CHIPS, 2, srows, PACK_W)[:, 0]
    for n, q in zip(small_names, _unpack(sall, small_shapes)):
        wt[n] = _from_quarters(q, _SMALL_CUT[n])
    row = lambda v: v.reshape(1, -1)

    as_ffn = lambda g, n: g.reshape((N_CHIPS,) + arg[n].shape[1:])
    ffn_w = [tuple(as_ffn(g, n) for g, n in zip(ffn_w0, ffn_names)), None]
    h = x[0]
    tape = []
    for i in range(n_layers):
        j = i // 2
        h, s0 = _ffn_fwd(h, row(wt['ffn_norm'][i, 0]), *ffn_w[i], (0,), f"ffn{i}a")
        if i % 2 == 0:
            h, sm, gathered = _att_fwd(h, row(mix_norm[i]), wt['att_w_in'], att_q_norm[j:j + 1], att_k_norm[j:j + 1],
                                       att_sinks[j:j + 1], wt['att_w_out'], late)
            ffn_w[1] = tuple(as_ffn(g, n) for g, n in zip(gathered[:3], ffn_names))
            wq = {n: g.reshape((N_CHIPS,) + arg[n].shape) for n, g in zip(late_names, gathered[3:])}
            wt['gdn_w_in'] = _join_quarters(wq['gdn_w_in'][:, 0], name="gdn_w_in_join")
            wt['gdn_w_out'] = wq['gdn_w_out'].reshape(GDN_VW, D_MODEL)
            wt['ple_w_gate'] = _from_quarters(wq['ple_w_gate'], 1)
            wt['ple_w_proj'] = _from_quarters(wq['ple_w_proj'], 2)
        else:
            h, sm = _gdn_layer_fwd(h, row(mix_norm[i]), wt['gdn_w_in'], wt['gdn_conv_w'][j], gdn_a_log[j:j + 1],
                                   gdn_dt_bias[j:j + 1], gdn_out_norm[j:j + 1], wt['gdn_w_out'])
        h, s1 = _ffn_fwd(h, row(wt['ffn_norm'][i, 1]), *ffn_w[i], (1,), f"ffn{i}b")
        h, sp = _ple_fwd(h, row(ple_norm[i]), wt['ple_w_gate'][i], wt['ple_w_proj'][i], p[i, 0], f"ple{i}")
        tape.append((s0, sm, s1, sp))

    dh, loss_local = _loss_head(h, loss_target[0], name="loss_head")
    loss = lax.psum(loss_local, ("x", "y", "c"))

    gr = {}
    stored_t = ('ffn_w_gate', 'ffn_w_up')
    as_stored = lambda a, n: jnp.swapaxes(a, -1, -2) if n in stored_t else a
    ffn_g = [[tuple(lax.empty((N_CHIPS,) + as_stored(arg[n], n).shape[2:], F32) for n in ffn_names) for _ in range(2)]
             for _ in range(n_layers)]
    ffn_keys = lambda i, k: [f"{n}_{i}{k}" for n in ffn_names]
    d_ffn_norm = [[None, None] for _ in range(n_layers)]
    d_mix, d_ple_norm, d_ple_gate, d_ple_proj = [None] * n_layers, [None] * n_layers, [None] * n_layers, [None] * n_layers

    def as_halves(g):
        return g.reshape((N_CHIPS, 2, -1, g.shape[-1]))

    def pair_up(keys, grads, tag):
        got = _swap_halves(grads, name=f"grad_swap_halves_{tag}")
        return [_add_pair(g, o, core, name=f"grad_add_pair_{k}") for k, g, o in zip(keys, grads, got)]

    for i in reversed(range(n_layers)):
        j = i // 2
        s0, sm, s1, sp = tape[i]
        dh, d_ple_norm[i], d_ple_gate[i], d_ple_proj[i] = _ple_bwd(dh, sp, row(ple_norm[i]), wt['ple_w_gate'][i], p[i, 0],
                                                                   f"ple{i}")
        dh, d_ffn_norm[i][1], ffn_g[i][1] = _ffn_bwd(dh, s1, row(wt['ffn_norm'][i, 1]), *ffn_w[i], (1,), ffn_g[i][1], (),
                                                     f"ffn{i}b")
        if i % 2 == 0:
            gr['ple_w_gate'] = _to_quarters(jnp.stack(d_ple_gate), 1)
            gr['ple_w_proj'] = _to_quarters(jnp.stack(d_ple_proj), 2)
            first_keys = ffn_keys(1, 0) + ffn_keys(1, 1) + ffn_keys(0, 1) + list(late_names)
            first_pairs = pair_up(first_keys, [as_halves(g) for g in ffn_g[1][0] + ffn_g[1][1] + ffn_g[0][1]]
                                  + [as_halves(gr[n]) for n in late_names], "a")
            (dh, d_mix[i], dw_in, gr['att_q_norm'], gr['att_k_norm'], gr['att_sinks'], dw_out,
             first_recv) = _att_bwd(dh, sm, row(mix_norm[i]), wt['att_w_in'], att_q_norm[j:j + 1],
                                    att_k_norm[j:j + 1], att_sinks[j:j + 1], wt['att_w_out'], first_pairs)
            gr['att_w_in'] = _split_quarters([dw_in], name="att_dw_in_split")
            gr['att_w_out'] = dw_out
        else:
            (dh, d_mix[i], dw_in, dconv, gr['gdn_a_log'], gr['gdn_dt_bias'], gr['gdn_out_norm'],
             dw_out) = _gdn_layer_bwd(dh, sm, row(mix_norm[i]), wt['gdn_conv_w'][j], gdn_a_log[j:j + 1],
                                      gdn_dt_bias[j:j + 1], gdn_out_norm[j:j + 1], wt['gdn_w_out'])
            gr['gdn_w_in'] = _split_quarters(dw_in, name="gdn_dw_in_split")
            gr['gdn_w_out'] = dw_out
            gr['gdn_conv_w'] = dconv[None]
        dh, d_ffn_norm[i][0], ffn_g[i][0] = _ffn_bwd(dh, s0, row(wt['ffn_norm'][i, 0]), *ffn_w[i], (0,), ffn_g[i][0], (),
                                                     f"ffn{i}a")
    grad_x = dh[None]

    gr['ffn_norm'] = jnp.stack([jnp.stack([d_ffn_norm[i][k][0] for k in range(2)]) for i in range(n_layers)])
    gr['mix_norm'] = jnp.concatenate(d_mix, axis=0)
    gr['ple_norm'] = jnp.concatenate(d_ple_norm, axis=0)

    last_keys = ffn_keys(0, 0) + ['att_w_in', 'att_w_out']
    last_pairs = pair_up(last_keys, [as_halves(g) for g in ffn_g[0][0]] + [as_halves(gr['att_w_in']), as_halves(gr['att_w_out'])],
                         "b")
    last_recv = _scatter_quarters(last_pairs, name="grad_scatter")
    keys = first_keys + last_keys
    tots = [_add_chips(pr, rc, chip, name=f"grad_add_chips_{k}")
            for k, pr, rc in zip(keys, first_pairs + last_pairs, first_recv + last_recv)]
    theirs = _share_halves(tots, name="grad_share")
    summed = dict(zip(keys, zip(tots, theirs)))

    whole_shapes = [arg[n].shape for n in _WHOLE]
    cut_full_shapes = [gr[n].shape for n in small_names]
    gvec = _pack([gr[n].reshape(-1) for n in _WHOLE + small_names], SMALL_ROW_MULT)
    gall = _sum_blocks(_gather_all(gvec, name="gather_small_grads"), N_DEV, name="sum_small_grads")
    parts = _unpack(gall, whole_shapes + cut_full_shapes)
    gsum = dict(zip(_WHOLE, parts))
    for n, g in zip(small_names, parts[len(_WHOLE):]):
        gsum[n] = lax.dynamic_index_in_dim(_to_quarters(g, _SMALL_CUT[n]), chip[0], axis=0, keepdims=False)

    delta, new_m, new_v = {}, {}, {}
    for n in ('att_w_in', 'att_w_out') + late_names:
        res = _adamw_halves(_halves(arg[n]), _halves(arg["m_" + n]), _halves(arg["v_" + n]), *summed[n], core,
                            name=f"adamw_{n}")
        gsum[n], delta[n], new_m[n], new_v[n] = (r.reshape(arg[n].shape) for r in res)
    for n in ffn_names:
        stored = as_stored(arg[n], n).shape
        in_halves = lambda a: as_stored(a, n).reshape(stored[:2] + (2, stored[2] // 2, stored[3]))
        wmv = [in_halves(arg[k + n]) for k in ("", "m_", "v_")]
        res = tuple(lax.empty(wmv[0].shape, F32) for _ in range(4))
        for i in range(n_layers):
            for k in range(2):
                res = _adamw_halves(*wmv, *summed[f"{n}_{i}{k}"], core, name=f"adamw_{n}_{i}{k}", into=(res, (i, k)))
        gsum[n], delta[n], new_m[n], new_v[n] = (as_stored(r.reshape(stored), n) for r in res)
    for n in _WHOLE + small_names:
        delta[n], new_m[n], new_v[n] = _adamw(arg[n], gsum[n], arg["m_" + n], arg["v_" + n], name=f"adamw_{n}")
    return (loss, grad_x, *[gsum[n] for n in _WEIGHTS], *[delta[n] for n in _WEIGHTS],
            *[new_m[n] for n in _WEIGHTS], *[new_v[n] for n in _WEIGHTS])
```

```python
import functools
import math

import jax
import jax.numpy as jnp
from jax import lax
from jax.experimental import pallas as pl
from jax.experimental.pallas import tpu as pltpu

F32 = jnp.float32
BF16 = jnp.bfloat16
MESH = pl.DeviceIdType.MESH

LANES = 128
VMEM_LIMIT_BYTES = 56 * 1024 * 1024

EPS = 1e-6
D_MODEL = 1024
HEAD_DIM = 64
SB_HEADS = 8
SWA_HEADS = 8
SWA_KV_HEADS = 2
WINDOW = 128
GDN_K_HEADS = 8
GDN_V_HEADS = 16
GDN_HEAD_DIM = 128
GDN_CONV = 4
GDN_CHUNK = 64
SB_W = SB_HEADS * HEAD_DIM
SWA_QW = SWA_HEADS * HEAD_DIM
SWA_KVW = SWA_KV_HEADS * HEAD_DIM
GDN_KW = GDN_K_HEADS * GDN_HEAD_DIM
GDN_VW = GDN_V_HEADS * GDN_HEAD_DIM
GDN_CONV_W = 2 * GDN_KW + GDN_VW

ADAM_LR = 0.001
ADAM_B1 = 0.9
ADAM_B2 = 0.999
ADAM_EPS = 1e-08
ADAM_WD = 0.01
ADAM_STEP = 10

NEG = -1e30


def _params(*sem):
    return pltpu.CompilerParams(dimension_semantics=sem or None, vmem_limit_bytes=VMEM_LIMIT_BYTES)


def _tile(n, cap, align=LANES):
    if n <= cap:
        return n
    for t in range(cap - cap % align, 0, -align):
        if n % t == 0:
            return t
    return n


N_CHIPS = 4
MM_VMEM_BUDGET_BYTES = 40 * 1024 * 1024
Q = "q"


def _opnd(x):
    return x if isinstance(x, tuple) else (x, ())


def _mm(a, b, *, name, ta=False, tb=False, out_dtype=F32, res=None, scale=1.0, out_q=False, into=None, norm_bwd=None,
        tm=None, tn=1024, tk=1024):
    (a_arr, a_lead), (b_arr, b_lead) = _opnd(a), _opnd(b)
    (k_a, m) = a_arr.shape[-2:] if ta else a_arr.shape[-2:][::-1]
    (n, k_b) = b_arr.shape[-2:] if tb else b_arr.shape[-2:][::-1]
    if into is not None:
        out_arr, out_lead = into
        out_q, out_dtype = Q in out_lead, out_arr.dtype
    else:
        out_lead = (Q,) if out_q else ()
    red_q = (Q in a_lead or Q in b_lead) and not out_q
    kq = min(k_a, k_b)
    assert (k_a == k_b) or (red_q and max(k_a, k_b) == N_CHIPS * kq), (a_arr.shape, b_arr.shape)
    tn, tk = _tile(n, tn), _tile(kq, tk)
    if tm is None:
        r_item = _opnd(res)[0].dtype.itemsize if res is not None else 0
        per_row = 2 * (tk * a_arr.dtype.itemsize + tn * (jnp.dtype(out_dtype).itemsize + r_item)) + 4 * tn
        if norm_bwd is not None:
            per_row += (2 * 2 + 4) * 4 * tn
        room = MM_VMEM_BUDGET_BYTES - 2 * tk * tn * b_arr.dtype.itemsize
        tm = next(c for c in (4096, 2048, 1024, 512, 256, 128) if c * per_row <= room or c == 128)
    tm = _tile(m, tm)
    nk = kq // tk
    ksteps = nk * (N_CHIPS if red_q else 1)
    dims = (((0 if ta else 1,), (1 if tb else 0,)), ((), ()))
    has_res = res is not None
    n_out = 2 if norm_bwd is not None else 1

    def body(*refs):
        a_ref, b_ref = refs[0], refs[1]
        o_ref, acc_ref = refs[-1 - n_out], refs[-1]
        k = pl.program_id(3)
        first_rows = pl.program_id(1) == 0

        @pl.when(k == 0)
        def _():
            acc_ref[...] = jnp.zeros_like(acc_ref)

        acc_ref[...] += lax.dot_general(a_ref[...].astype(BF16), b_ref[...].astype(BF16), dims,
                                        preferred_element_type=F32)

        @pl.when(k == ksteps - 1)
        def _():
            r = acc_ref[...]
            if scale != 1.0:
                r = r * scale
            if has_res:
                r = r + refs[2][...].astype(F32)
            if norm_bwd is not None:
                h_ref, gain_ref, dres_ref = refs[2 + has_res:5 + has_res]
                dgain_ref = refs[-2]
                _, vjp = jax.vjp(_f_rms_res, h_ref[...], gain_ref[...])
                r, dgain = vjp((r, dres_ref[...]))

                @pl.when(first_rows)
                def _():
                    dgain_ref[...] = jnp.zeros_like(dgain_ref)

                dgain_ref[...] += dgain
            o_ref[...] = r.astype(o_ref.dtype)

    def spec(lead, blk, pos):
        def index(s, i, j, k):
            kk = k % nk if (red_q and Q in lead) else k
            quarter = s if out_q else k // nk
            return tuple(quarter if l == Q else l for l in lead) + pos(i, j, kk)
        return pl.BlockSpec((None,) * len(lead) + blk, index)

    a_spec = spec(a_lead, (tk, tm), lambda i, j, k: (k, i)) if ta else spec(a_lead, (tm, tk), lambda i, j, k: (i, k))
    b_spec = spec(b_lead, (tn, tk), lambda i, j, k: (j, k)) if tb else spec(b_lead, (tk, tn), lambda i, j, k: (k, j))
    o_spec = spec(out_lead, (tm, tn), lambda i, j, k: (i, j))
    in_specs, args = [a_spec, b_spec], [a_arr, b_arr]
    if has_res:
        r_arr, r_lead = _opnd(res)
        in_specs.append(spec(r_lead, (tm, tn), lambda i, j, k: (i, j)))
        args.append(r_arr)
    out_specs, out_shapes = [o_spec], []
    if norm_bwd is not None:
        assert tn == n and not out_q and into is None, "the norm's backward needs whole rows"
        h_arr, gain_arr, dres_arr = norm_bwd
        row_spec = spec((), (tm, tn), lambda i, j, k: (i, j))
        gain_spec = pl.BlockSpec((1, tn), lambda s, i, j, k: (0, 0))
        in_specs += [row_spec, gain_spec, row_spec]
        args += [h_arr, gain_arr, dres_arr]
        out_specs.append(gain_spec)
    aliases = {}
    if into is not None:
        in_specs.append(pl.BlockSpec(memory_space=pl.ANY))
        args.append(out_arr)
        aliases = {len(args) - 1: 0}
        out_shapes.append(jax.ShapeDtypeStruct(out_arr.shape, out_arr.dtype))
    else:
        out_shapes.append(jax.ShapeDtypeStruct(((N_CHIPS,) if out_q else ()) + (m, n), out_dtype))
    if norm_bwd is not None:
        out_shapes.append(jax.ShapeDtypeStruct((1, n), F32))
    out = pl.pallas_call(
        body, name=name, grid=(N_CHIPS if out_q else 1, m // tm, n // tn, ksteps), in_specs=in_specs,
        out_specs=out_specs, out_shape=out_shapes, scratch_shapes=[pltpu.VMEM((tm, tn), F32)],
        input_output_aliases=aliases,
        compiler_params=_params("parallel", *(("arbitrary",) * 3 if norm_bwd is not None else ("parallel", "parallel", "arbitrary"))),
    )(*args)
    return out if norm_bwd is not None else out[0]


def _row_spec(r, tm):
    if isinstance(r, tuple):
        arr, width, cb = r
        return arr, pl.BlockSpec((tm, width), lambda i, cb=cb: (i, cb))
    return r, pl.BlockSpec((tm, r.shape[1]), lambda i: (i, 0))


def _const_spec(c):
    return pl.BlockSpec(c.shape, lambda i: (0,) * c.ndim)


def _row_fwd(fn, rows, consts, outs, *, name, tm=256):
    tm = _tile(_row_spec(rows[0], tm)[0].shape[0], tm, 8)
    arrs, specs = zip(*[_row_spec(r, tm) for r in rows])
    t = arrs[0].shape[0]
    nr, nc = len(rows), len(consts)

    def body(*refs):
        vals = [r[...].astype(F32) for r in refs[:nr + nc]]
        res = fn(*vals)
        for o_ref, v in zip(refs[nr + nc:], res):
            o_ref[...] = v.astype(o_ref.dtype)

    out = pl.pallas_call(
        body, name=name, grid=(t // tm,),
        in_specs=list(specs) + [_const_spec(c) for c in consts],
        out_specs=[pl.BlockSpec((tm, w), lambda i: (i, 0)) for w, _ in outs],
        out_shape=[jax.ShapeDtypeStruct((t, w), dt) for w, dt in outs],
        compiler_params=_params("parallel"),
    )(*arrs, *consts)
    return list(out)


def _row_bwd(fn, rows, consts, cts, row_grads, const_grads, *, name, tm=256):
    tm = _tile(_row_spec(rows[0], tm)[0].shape[0], tm, 8)
    arrs, specs = zip(*[_row_spec(r, tm) for r in rows])
    ct_arrs, ct_specs = zip(*[_row_spec(r, tm) for r in cts])
    t = arrs[0].shape[0]
    nr, nc, nt = len(rows), len(consts), len(cts)
    n_in = nr + nc + nt

    def body(*refs):
        vals = [r[...].astype(F32) for r in refs[:nr + nc]]
        ctv = tuple(r[...].astype(F32) for r in refs[nr + nc:n_in])
        _, vjp = jax.vjp(fn, *vals)
        g = vjp(ctv)
        outs = refs[n_in:]
        for (idx, _), o_ref in zip(row_grads, outs[:len(row_grads)]):
            o_ref[...] = g[idx].astype(o_ref.dtype)
        first = pl.program_id(0) == 0
        for ci, o_ref in zip(const_grads, outs[len(row_grads):]):
            @pl.when(first)
            def _(o_ref=o_ref):
                o_ref[...] = jnp.zeros_like(o_ref)

            o_ref[...] += g[nr + ci]

    widths = [(_row_spec(rows[idx], tm)[1].block_shape[1], dt) for idx, dt in row_grads]
    out = pl.pallas_call(
        body, name=name, grid=(t // tm,),
        in_specs=list(specs) + [_const_spec(c) for c in consts] + list(ct_specs),
        out_specs=[pl.BlockSpec((tm, w), lambda i: (i, 0)) for w, _ in widths]
        + [_const_spec(consts[ci]) for ci in const_grads],
        out_shape=[jax.ShapeDtypeStruct((t, w), dt) for w, dt in widths]
        + [jax.ShapeDtypeStruct(consts[ci].shape, F32) for ci in const_grads],
        compiler_params=_params("arbitrary"),
    )(*arrs, *consts, *ct_arrs)
    return list(out)


def _rms(x, g):
    return x * lax.rsqrt(jnp.mean(x * x, axis=-1, keepdims=True) + EPS) * g


def _f_rms(h, g):
    return (_rms(h, g),)


def _f_rms_res(h, g):
    return (_rms(h, g), h)


def _f_swiglu(g, u):
    return (g * jax.nn.sigmoid(g) * u,)


def _f_ple(h, gl, pp):
    return (h + jax.nn.sigmoid(gl) * pp,)


def _f_gdn_out(o, z, gain):
    outs = []
    for hd in range(GDN_V_HEADS):
        sl = slice(hd * GDN_HEAD_DIM, (hd + 1) * GDN_HEAD_DIM)
        oh, zh = o[:, sl], z[:, sl]
        outs.append(_rms(oh, gain) * (zh * jax.nn.sigmoid(zh)))
    return (jnp.concatenate(outs, axis=1),)


def _loss_head(y, target, *, name, tm=512):
    t, d = y.shape
    tm = _tile(t, tm, 8)

    def body(y_ref, t_ref, dy_ref, l_ref):
        @pl.when(pl.program_id(0) == 0)
        def _():
            l_ref[...] = jnp.zeros_like(l_ref)

        e = y_ref[...] - t_ref[...]
        dy_ref[...] = e * (1.0 / d)
        l_ref[...] += jnp.sum(e * e) * (0.5 / d)

    dy, l = pl.pallas_call(
        body, name=name, grid=(t // tm,),
        in_specs=[pl.BlockSpec((tm, d), lambda i: (i, 0))] * 2,
        out_specs=[pl.BlockSpec((tm, d), lambda i: (i, 0)), pl.BlockSpec((8, LANES), lambda i: (0, 0))],
        out_shape=[jax.ShapeDtypeStruct((t, d), F32), jax.ShapeDtypeStruct((8, LANES), F32)],
        compiler_params=_params("arbitrary"),
    )(y, target)
    return dy, l[0, 0]


def _dg(a, b, ca, cb):
    nb = a.ndim - 2
    batch = tuple(range(nb))
    return lax.dot_general(a, b, (((ca + nb,), (cb + nb,)), (batch, batch)), preferred_element_type=F32)


def _b(x):
    return x.astype(BF16)


@jax.custom_vjp
def _bdot(a, b):
    return _dg(_b(a), _b(b), 1, 0)


def _bdot_fwd(a, b):
    return _bdot(a, b), (a, b)


def _bdot_bwd(r, ct):
    a, b = r
    return _dg(_b(ct), _b(b), 1, 1), _dg(_b(a), _b(ct), 0, 0)


_bdot.defvjp(_bdot_fwd, _bdot_bwd)


@jax.custom_vjp
def _bdot_nt(a, b):
    return _dg(_b(a), _b(b), 1, 1)


def _bdot_nt_fwd(a, b):
    return _bdot_nt(a, b), (a, b)


def _bdot_nt_bwd(r, ct):
    a, b = r
    return _dg(_b(ct), _b(b), 1, 0), _dg(_b(ct), _b(a), 0, 0)


_bdot_nt.defvjp(_bdot_nt_fwd, _bdot_nt_bwd)


@jax.custom_vjp
def _bdot_tn(a, b):
    return _dg(_b(a), _b(b), 0, 0)


def _bdot_tn_fwd(a, b):
    return _bdot_tn(a, b), (a, b)


def _bdot_tn_bwd(r, ct):
    a, b = r
    return _dg(_b(b), _b(ct), 1, 1), _dg(_b(a), _b(ct), 1, 0)


_bdot_tn.defvjp(_bdot_tn_fwd, _bdot_tn_bwd)


def _two(x):
    hi = x.astype(BF16)
    return hi, (x - hi.astype(F32)).astype(BF16)


def _dg3(a, b, ca, cb):
    (ah, al), (bh, bl) = _two(a), _two(b)
    return _dg(ah, bh, ca, cb) + (_dg(ah, bl, ca, cb) + _dg(al, bh, ca, cb))


@jax.custom_vjp
def _hdot(a, b):
    return _dg3(a, b, 1, 0)


def _hdot_fwd(a, b):
    return _hdot(a, b), (a, b)


def _hdot_bwd(r, ct):
    a, b = r
    return _dg3(ct, b, 1, 1), _dg3(a, ct, 0, 0)


_hdot.defvjp(_hdot_fwd, _hdot_bwd)


@jax.custom_vjp
def _unit_lower_inverse(x):
    c = x.shape[-1]
    eye = (lax.broadcasted_iota(jnp.int32, x.shape, 1) == lax.broadcasted_iota(jnp.int32, x.shape, 2)).astype(F32)
    inv, pw = eye + x, x
    for _ in range(int(math.log2(c)) - 1):
        pw = _dg3(pw, pw, 1, 0)
        inv = inv + _dg3(inv, pw, 1, 0)
    return inv


def _unit_lower_inverse_fwd(x):
    inv = _unit_lower_inverse(x)
    return inv, inv


def _unit_lower_inverse_bwd(inv, ct):
    return (_dg3(_dg3(inv, ct, 0, 0), inv, 1, 1),)


_unit_lower_inverse.defvjp(_unit_lower_inverse_fwd, _unit_lower_inverse_bwd)


@jax.custom_vjp
def _known_inverse(x, inv):
    return inv


def _known_inverse_fwd(x, inv):
    return inv, inv


def _known_inverse_bwd(inv, ct):
    return _dg3(_dg3(inv, ct, 0, 0), inv, 1, 1), jnp.zeros_like(inv)


_known_inverse.defvjp(_known_inverse_fwd, _known_inverse_bwd)


def _split_dot(x, u):
    hi, lo = _two(x)
    return _dg(hi, u, 1, 0) + _dg(lo, u, 1, 0)


@jax.custom_vjp
def _ldot(l01, x):
    hi, lo = _two(x)
    l01 = l01.astype(BF16)
    return _dg(l01, hi, 1, 0) + _dg(l01, lo, 1, 0)


def _ldot_fwd(l01, x):
    return _ldot(l01, x), l01


def _ldot_bwd(l01, ct):
    hi, lo = _two(ct)
    l01b = l01.astype(BF16)
    return jnp.zeros_like(l01), _dg(l01b, hi, 0, 0) + _dg(l01b, lo, 0, 0)


_ldot.defvjp(_ldot_fwd, _ldot_bwd)


SB_BLK = 128
SB_BWD_BLK = 256
SB_KEYS = 512
SB_PAIRS = 2
SB_SCALE = HEAD_DIM ** -0.5


def _log_sigmoid(z):
    return jnp.minimum(z, 0.0) - jnp.log(1.0 + jnp.exp(-jnp.abs(z)))


def _sb_consts(t, blk):
    kb = min(SB_KEYS, t)
    nh = 2 * SB_PAIRS
    lane = lax.broadcasted_iota(jnp.int32, (nh, blk, kb), 2)
    row = lax.broadcasted_iota(jnp.int32, (nh, blk, kb), 1)
    ur = lax.broadcasted_iota(jnp.int32, (kb, kb), 0)
    uc = lax.broadcasted_iota(jnp.int32, (kb, kb), 1)
    return kb, nh, lane, row, ur, uc


def _sb_heads(x):
    head0 = lax.broadcasted_iota(jnp.int32, (x.shape[0], LANES), 1) < HEAD_DIM
    out = []
    for p in range(SB_PAIRS):
        blk = x[:, p * LANES:(p + 1) * LANES]
        out += [jnp.where(head0, blk, 0.0), jnp.where(head0, 0.0, blk)]
    return jnp.stack(out)


def _sb_pairs(x):
    return jnp.stack([x[:, (h // 2) * LANES:(h // 2 + 1) * LANES] for h in range(2 * SB_PAIRS)])


def _sb_merge(x):
    head0 = lax.broadcasted_iota(jnp.int32, (x.shape[1], LANES), 1) < HEAD_DIM
    return jnp.concatenate([jnp.where(head0, x[2 * p], x[2 * p + 1]) for p in range(SB_PAIRS)], axis=1)


def _sb_rows_dot(x, u):
    nh, rows, k = x.shape
    return _split_dot(x.reshape(nh * rows, k), u).reshape(nh, rows, k)


def _sb_fwd(proj, *, name, gather=()):
    t = proj.shape[0]
    nb = t // SB_BLK
    width = SB_PAIRS * LANES
    ng = SB_W // width
    na = len(gather)

    def body(q_ref, k_ref, v_ref, *rest):
        o_ref, r_ref = rest[na:na + 2]
        i = pl.program_id(1)
        if na:
            step = pl.program_id(0) * nb + i
            copies = lambda **kw: _gather_copies(rest[:na], rest[na + 2:2 * na + 2], *rest[2 * na + 2:], **kw)
            pl.when(step == 0)(lambda: _gather_start(copies(only_first=True)))
        kb, nh, lane, row, ur, uc = _sb_consts(t, SB_BLK)
        u_suffix = (ur >= uc).astype(BF16)
        qh = _b(_sb_heads(q_ref[...]) * SB_SCALE)
        diag = (i * SB_BLK) // kb

        def block(j, carry, masked):
            acc, car = carry
            keys = pl.ds(pl.multiple_of(j * kb, kb), kb)
            kj, vj = _b(_sb_pairs(k_ref[keys, :])), _b(_sb_pairs(v_ref[keys, :]))
            z = _dg(qh, kj, 1, 1)
            lk = _log_sigmoid(-z)
            if masked:
                causal = (j * kb + lane) < (i * SB_BLK + row)
                lk = jnp.where(causal, lk, 0.0)
            suf = _sb_rows_dot(lk, u_suffix) + car
            w = jnp.exp(z + suf)
            if masked:
                w = jnp.where(causal, w, 0.0)
            return acc + _dg(_b(w), vj, 1, 0), suf[:, :, 0:1]

        zero = (jnp.zeros((nh, SB_BLK, LANES), F32), jnp.zeros((nh, SB_BLK, 1), F32))
        carry = block(diag, zero, True)
        acc, car = lax.fori_loop(0, diag, lambda s, c: block(diag - 1 - s, c, False), carry)
        o_ref[...] = _sb_merge(acc)
        r_ref[...] = _sb_merge(jnp.broadcast_to(car, (nh, SB_BLK, LANES)))
        if na:
            pl.when(step == ng * nb - 1)(lambda: _gather_finish(copies()))

    return pl.pallas_call(
        body, name=name, grid=(ng, nb),
        in_specs=[pl.BlockSpec((SB_BLK, width), lambda p, i: (i, p)),
                  pl.BlockSpec((t, width), lambda p, i: (0, ng + p)),
                  pl.BlockSpec((t, width), lambda p, i: (0, 2 * ng + p))] + [ANY] * na,
        out_specs=[pl.BlockSpec((SB_BLK, width), lambda p, i: (i, p))] * 2 + [ANY] * na,
        out_shape=[jax.ShapeDtypeStruct((t, SB_W), F32)] * 2
        + [jax.ShapeDtypeStruct((N_CHIPS,) + g.shape, g.dtype) for g in gather],
        scratch_shapes=_gather_scratch(na) if na else [],
        compiler_params=_params("arbitrary", "arbitrary"),
    )(proj, proj, proj, *gather)


def _sb_bwd(proj, rtot, dout, *, name, scatter=()):
    t = proj.shape[0]
    nb = t // SB_BWD_BLK
    width = SB_PAIRS * LANES
    ng = SB_W // width
    na = len(scatter)

    def body(q_ref, k_ref, v_ref, r_ref, do_ref, *rest):
        dq_ref, dk_ref, dv_ref = rest[na:na + 3]
        i = pl.program_id(1)
        if na:
            step = pl.program_id(0) * nb + i
            copies = lambda: _scatter_copies(rest[:na], rest[na + 3:4 * na + 3], *rest[4 * na + 3:])
            pl.when(step == 0)(lambda: [cp.start() for cp in copies()] and None)
        kb, nh, lane, row, ur, uc = _sb_consts(t, SB_BWD_BLK)
        u_incl = (ur <= uc).astype(BF16)
        u_excl = (ur < uc).astype(BF16)
        q, do = q_ref[...], do_ref[...]
        qh, doh = _b(_sb_heads(q) * SB_SCALE), _b(_sb_heads(do))
        qb, dob = _b(_sb_pairs(q) * SB_SCALE), _b(_sb_pairs(do))
        rh = jnp.min(_sb_heads(r_ref[...]), axis=2, keepdims=True)
        diag = (i * SB_BWD_BLK) // kb

        @pl.when(i == 0)
        def _():
            dk_ref[...] = jnp.zeros_like(dk_ref)
            dv_ref[...] = jnp.zeros_like(dv_ref)

        def block(j, carry, masked):
            dq_acc, clk, ce = carry
            keys = pl.ds(pl.multiple_of(j * kb, kb), kb)
            kj, vj = _b(_sb_pairs(k_ref[keys, :])), _b(_sb_pairs(v_ref[keys, :]))
            z = _dg(qh, kj, 1, 1)
            lk = _log_sigmoid(-z)
            ls = z + lk
            if masked:
                causal = (j * kb + lane) < (i * SB_BWD_BLK + row)
                lk = jnp.where(causal, lk, 0.0)
            pre = _sb_rows_dot(lk, u_incl) + clk
            w = jnp.exp(ls + (rh - pre))
            if masked:
                w = jnp.where(causal, w, 0.0)
            e = _dg(doh, vj, 1, 1) * w
            pre_e = _sb_rows_dot(e, u_excl) + ce
            sig = jnp.exp(ls)
            dz = e - sig * (e + pre_e)
            if masked:
                dz = jnp.where(causal, dz, 0.0)
            dzb = _b(dz)
            dk_ref[keys, :] += _sb_merge(_dg(dzb, qb, 0, 0))
            dv_ref[keys, :] += _sb_merge(_dg(_b(w), dob, 0, 0))
            return dq_acc + _dg(dzb, kj, 1, 0), pre[:, :, kb - 1:], pre_e[:, :, kb - 1:] + e[:, :, kb - 1:]

        zero = (jnp.zeros((nh, SB_BWD_BLK, LANES), F32), jnp.zeros((nh, SB_BWD_BLK, 1), F32), jnp.zeros((nh, SB_BWD_BLK, 1), F32))
        carry = lax.fori_loop(0, diag, lambda j, c: block(j, c, False), zero)
        dq_acc, _, _ = block(diag, carry, True)
        dq_ref[...] = _sb_merge(dq_acc) * SB_SCALE
        if na:
            pl.when(step == ng * nb - 1)(lambda: [cp.wait() for cp in copies()] and None)

    blk = pl.BlockSpec((SB_BWD_BLK, width), lambda p, i: (i, p))
    whole = pl.BlockSpec((t, width), lambda p, i: (0, p))
    return pl.pallas_call(
        body, name=name, grid=(ng, nb),
        in_specs=[blk,
                  pl.BlockSpec((t, width), lambda p, i: (0, ng + p)),
                  pl.BlockSpec((t, width), lambda p, i: (0, 2 * ng + p)),
                  blk, blk] + [ANY] * na,
        out_specs=[blk, whole, whole] + [ANY] * (3 * na),
        out_shape=[jax.ShapeDtypeStruct((t, SB_W), F32)] * 3 + _scatter_shapes(scatter),
        scratch_shapes=_dma_sems(3 * na) if na else [],
        compiler_params=_params("arbitrary", "arbitrary"),
    )(proj, proj, proj, rtot, dout, *scatter)


SWA_G = SWA_HEADS // SWA_KV_HEADS


def _swa_heads(first, qs, ks, vs, qg, kg, sinks):
    shape = (SWA_HEADS, WINDOW, 2 * WINDOW)
    qi = lax.broadcasted_iota(jnp.int32, shape, 1)
    kj = lax.broadcasted_iota(jnp.int32, shape, 2)
    dist = qi + WINDOW - kj
    valid = (dist >= 0) & (dist < WINDOW) & (jnp.logical_not(first) | (kj >= WINDOW))
    head = lax.broadcasted_iota(jnp.int32, (SWA_HEADS, 1, 1), 0)
    slope = sum(jnp.where(head == h, 2.0 ** (-8.0 * (h + 1) / SWA_HEADS), 0.0) for h in range(SWA_HEADS))
    kn = _rms(ks, kg)
    per_q_head = lambda x: jnp.concatenate([x[h // SWA_G:h // SWA_G + 1] for h in range(SWA_HEADS)], axis=0)
    k8, v8 = per_q_head(kn), per_q_head(vs)
    s = _bdot_nt(_rms(qs, qg), k8) * (HEAD_DIM ** -0.5)
    s = jnp.where(valid, s - slope * dist.astype(F32), NEG)
    m = lax.stop_gradient(jnp.maximum(jnp.max(s, axis=2, keepdims=True), sinks))
    p = jnp.exp(s - m)
    den = jnp.sum(p, axis=2, keepdims=True) + jnp.exp(sinks - m)
    return _bdot(p / den, v8)


def _swa_split(q, kp, kc, vp, vc, sk):
    lanes = lambda x, n: jnp.stack([x[:, h * HEAD_DIM:(h + 1) * HEAD_DIM].astype(F32) for h in range(n)])
    k2, v2 = jnp.concatenate([kp, kc], axis=0), jnp.concatenate([vp, vc], axis=0)
    sinks = jnp.stack([sk[:, h:h + 1] for h in range(SWA_HEADS)])
    return lanes(q, SWA_HEADS), lanes(k2, SWA_KV_HEADS), lanes(v2, SWA_KV_HEADS), sinks


def _swa_join(x):
    return jnp.concatenate([x[h] for h in range(x.shape[0])], axis=1)


def _swa_specs(t):
    qcb = (3 * SB_W) // SWA_QW
    kcb = (3 * SB_W + SWA_QW) // SWA_KVW
    prev = lambda i: jnp.maximum(i - 1, 0)
    return [pl.BlockSpec((WINDOW, SWA_QW), lambda i: (i, qcb)),
            pl.BlockSpec((WINDOW, SWA_KVW), lambda i: (prev(i), kcb)),
            pl.BlockSpec((WINDOW, SWA_KVW), lambda i: (i, kcb)),
            pl.BlockSpec((WINDOW, SWA_KVW), lambda i: (prev(i), kcb + 1)),
            pl.BlockSpec((WINDOW, SWA_KVW), lambda i: (i, kcb + 1)),
            pl.BlockSpec((1, HEAD_DIM), lambda i: (0, 0)),
            pl.BlockSpec((1, HEAD_DIM), lambda i: (0, 0)),
            pl.BlockSpec((1, SWA_HEADS), lambda i: (0, 0))]


def _swa_fwd(proj, qg, kg, sinks, *, name):
    t = proj.shape[0]

    def body(q_ref, kp_ref, kc_ref, vp_ref, vc_ref, qg_ref, kg_ref, sk_ref, o_ref):
        first = pl.program_id(0) == 0
        qs, ks, vs, sk = _swa_split(q_ref[...], kp_ref[...], kc_ref[...], vp_ref[...], vc_ref[...], sk_ref[...])
        o_ref[...] = _swa_join(_swa_heads(first, qs, ks, vs, qg_ref[...], kg_ref[...], sk))

    return pl.pallas_call(
        body, name=name, grid=(t // WINDOW,), in_specs=_swa_specs(t),
        out_specs=pl.BlockSpec((WINDOW, SWA_QW), lambda i: (i, 0)),
        out_shape=jax.ShapeDtypeStruct((t, SWA_QW), F32),
        compiler_params=_params("parallel"),
    )(proj, proj, proj, proj, proj, qg, kg, sinks)


def _swa_bwd(proj, qg, kg, sinks, dout, *, name):
    t = proj.shape[0]

    def body(q_ref, kp_ref, kc_ref, vp_ref, vc_ref, qg_ref, kg_ref, sk_ref, do_ref,
             dq_ref, dk_ref, dv_ref, dqg_ref, dkg_ref, dsk_ref):
        i = pl.program_id(0)
        first = i == 0

        @pl.when(first)
        def _():
            for r in (dk_ref, dv_ref, dqg_ref, dkg_ref, dsk_ref):
                r[...] = jnp.zeros_like(r)

        qs, ks, vs, sk = _swa_split(q_ref[...], kp_ref[...], kc_ref[...], vp_ref[...], vc_ref[...], sk_ref[...])
        do = do_ref[...]
        cts = jnp.stack([do[:, h * HEAD_DIM:(h + 1) * HEAD_DIM] for h in range(SWA_HEADS)])
        _, vjp = jax.vjp(functools.partial(_swa_heads, first), qs, ks, vs, qg_ref[...], kg_ref[...], sk)
        dqs, dks, dvs, dqg, dkg, dsk = vjp(cts)
        dq_ref[...] = _swa_join(dqs)
        dk2, dv2 = _swa_join(dks), _swa_join(dvs)
        cur = pl.ds(pl.multiple_of(i * WINDOW, WINDOW), WINDOW)
        prv = pl.ds(pl.multiple_of(jnp.maximum(i - 1, 0) * WINDOW, WINDOW), WINDOW)
        dk_ref[prv, :] += dk2[:WINDOW]
        dv_ref[prv, :] += dv2[:WINDOW]
        dk_ref[cur, :] += dk2[WINDOW:]
        dv_ref[cur, :] += dv2[WINDOW:]
        dqg_ref[...] += dqg
        dkg_ref[...] += dkg
        dsk_ref[...] += _swa_join(dsk)

    whole = lambda shape: pl.BlockSpec(shape, lambda i: (0, 0))
    return pl.pallas_call(
        body, name=name, grid=(t // WINDOW,),
        in_specs=_swa_specs(t) + [pl.BlockSpec((WINDOW, SWA_QW), lambda i: (i, 0))],
        out_specs=[pl.BlockSpec((WINDOW, SWA_QW), lambda i: (i, 0)), whole((t, SWA_KVW)), whole((t, SWA_KVW)),
                   whole((1, HEAD_DIM)), whole((1, HEAD_DIM)), whole((1, SWA_HEADS))],
        out_shape=[jax.ShapeDtypeStruct((t, SWA_QW), F32), jax.ShapeDtypeStruct((t, SWA_KVW), F32),
                   jax.ShapeDtypeStruct((t, SWA_KVW), F32), jax.ShapeDtypeStruct((1, HEAD_DIM), F32),
                   jax.ShapeDtypeStruct((1, HEAD_DIM), F32), jax.ShapeDtypeStruct((1, SWA_HEADS), F32)],
        compiler_params=_params("arbitrary"),
    )(proj, proj, proj, proj, proj, qg, kg, sinks, dout)


CONV_CB = 512
CONV_TM = 512
HALO = 8


def _conv_pre(x_ref, h_ref, w_ref, i):
    halo = jnp.where(i > 0, h_ref[...], 0.0)
    xe = jnp.concatenate([halo, x_ref[...]], axis=0)
    tm = x_ref.shape[0]
    w = w_ref[...]
    c = sum(w[k:k + 1, :] * xe[HALO - (GDN_CONV - 1) + k:HALO - (GDN_CONV - 1) + k + tm] for k in range(GDN_CONV))
    return c, xe


def _conv_specs(tm, cb):
    return [pl.BlockSpec((tm, cb), lambda c, i: (i, c)),
            pl.BlockSpec((HALO, cb), lambda c, i: (jnp.maximum(i * (tm // HALO) - 1, 0), c)),
            pl.BlockSpec((GDN_CONV, cb), lambda c, i: (0, c))]


def _conv_fwd(x, w, *, name):
    t, ch = x.shape
    tm, cb = _tile(t, CONV_TM), _tile(ch, CONV_CB)

    def body(x_ref, h_ref, w_ref, o_ref):
        c, _ = _conv_pre(x_ref, h_ref, w_ref, pl.program_id(1))
        o_ref[...] = c * jax.nn.sigmoid(c)

    tile = pl.BlockSpec((tm, cb), lambda c, i: (i, c))
    return pl.pallas_call(
        body, name=name, grid=(ch // cb, t // tm), in_specs=_conv_specs(tm, cb), out_specs=tile,
        out_shape=jax.ShapeDtypeStruct((t, ch), F32),
        compiler_params=_params("parallel", "parallel"),
    )(x, x, w)


def _conv_bwd(x, w, dact, *, name):
    t, ch = x.shape
    tm, cb = _tile(t, CONV_TM), _tile(ch, CONV_CB)
    nt = t // tm
    last = GDN_CONV - 1

    def body(x_ref, h_ref, w_ref, xn_ref, da_ref, dan_ref, dx_ref, dw_ref):
        i = pl.program_id(1)

        @pl.when(i == 0)
        def _():
            dw_ref[...] = jnp.zeros_like(dw_ref)

        w = w_ref[...]
        xe = jnp.concatenate([jnp.where(i > 0, h_ref[...], 0.0), x_ref[...], xn_ref[...]], axis=0)
        rows = tm + HALO
        c = sum(w[k:k + 1, :] * xe[HALO - last + k:HALO - last + k + rows] for k in range(GDN_CONV))
        sig = jax.nn.sigmoid(c)
        da = jnp.concatenate([da_ref[...], jnp.where(i < nt - 1, dan_ref[...], 0.0)], axis=0)
        dce = da * (sig * (1.0 + c * (1.0 - sig)))
        dc = dce[:tm]
        dx_ref[...] = sum(w[k:k + 1, :] * dce[last - k:last - k + tm] for k in range(GDN_CONV))
        dw_ref[...] += jnp.concatenate(
            [jnp.sum(dc * xe[HALO - last + k:HALO - last + k + tm], axis=0, keepdims=True) for k in range(GDN_CONV)],
            axis=0)

    tile = pl.BlockSpec((tm, cb), lambda c, i: (i, c))
    nxt = pl.BlockSpec((HALO, cb), lambda c, i: (jnp.minimum((i + 1) * (tm // HALO), t // HALO - 1), c))
    return pl.pallas_call(
        body, name=name, grid=(ch // cb, nt),
        in_specs=_conv_specs(tm, cb) + [nxt, tile, nxt],
        out_specs=[tile, pl.BlockSpec((GDN_CONV, cb), lambda c, i: (0, c))],
        out_shape=[jax.ShapeDtypeStruct((t, ch), F32), jax.ShapeDtypeStruct((GDN_CONV, ch), F32)],
        compiler_params=_params("parallel", "arbitrary"),
    )(x, x, w, x, dact, dact)


def _gdn_chunk(qraw, kraw, v, bl, a, alog, dtb, state, inverse=None, keep_inverse=False):
    c, d = GDN_CHUNK, GDN_HEAD_DIM
    nh = qraw.shape[0]
    ri = lax.broadcasted_iota(jnp.int32, (nh, c, c), 1)
    ci = lax.broadcasted_iota(jnp.int32, (nh, c, c), 2)
    incl, strict = ri >= ci, ri > ci
    q = qraw * lax.rsqrt(jnp.sum(qraw * qraw, axis=-1, keepdims=True) + EPS) * (d ** -0.5)
    k = kraw * lax.rsqrt(jnp.sum(kraw * kraw, axis=-1, keepdims=True) + EPS)
    beta = jax.nn.sigmoid(bl)
    g = -jnp.exp(alog) * jax.nn.softplus(a + dtb)
    gc = _ldot(incl.astype(F32), jnp.broadcast_to(g, (nh, c, d)))
    gcm = gc[:, :, :c]
    decay = jnp.exp(jnp.where(incl, gcm - jnp.swapaxes(gcm, 1, 2), NEG))
    eg = jnp.exp(gc)
    kbeta = k * beta
    x = -jnp.where(strict, _bdot_nt(kbeta, k) * decay, 0.0)
    tinv = _unit_lower_inverse(x) if inverse is None else _known_inverse(x, inverse)
    u = _hdot(tinv, v * beta)
    w = _hdot(tinv, kbeta * eg)
    attn = jnp.where(incl, _bdot_nt(q, k) * decay, 0.0)
    glast = gc[:, c - 1:c, :]
    v_new = u - _bdot(w, state)
    o = _bdot(q * eg, state) + _bdot(attn, v_new)
    state = state * jnp.exp(glast) + _bdot_tn(k * jnp.exp(glast - gc), v_new)
    return (o, state, tinv) if keep_inverse else (o, state)


GDN_REP = GDN_V_HEADS // GDN_K_HEADS
GDN_HB = 8


def _gdn_pick(vals, kh, r):
    ba, alog, dtb = vals
    lane = lax.broadcasted_iota(jnp.int32, ba.shape, 1)
    hv = kh * GDN_REP + r
    bl = jnp.sum(jnp.where(lane == hv, ba, 0.0), axis=1, keepdims=True)
    a = jnp.sum(jnp.where(lane == GDN_V_HEADS + hv, ba, 0.0), axis=1, keepdims=True)
    lane1 = lax.broadcasted_iota(jnp.int32, alog.shape, 1)
    al = jnp.sum(jnp.where(lane1 == hv, alog, 0.0), axis=1, keepdims=True)
    db = jnp.sum(jnp.where(lane1 == hv, dtb, 0.0), axis=1, keepdims=True)
    return bl, a, al, db


def _gdn_stack(qs, ks, vs, small, j):
    d = GDN_HEAD_DIM
    per = [[], [], [], [], [], [], []]
    for hh in range(GDN_HB):
        q, k = qs[:, hh * d:(hh + 1) * d], ks[:, hh * d:(hh + 1) * d]
        for r in range(GDN_REP):
            col = (hh * GDN_REP + r) * d
            for lst, val in zip(per, (q, k, vs[:, col:col + d]) + _gdn_pick(small, j * GDN_HB + hh, r)):
                lst.append(val)
    return tuple(jnp.stack(lst) for lst in per)


def _gdn_specs(nchunk, rev):
    c, d = GDN_CHUNK, GDN_HEAD_DIM
    at = (lambda n: nchunk - 1 - n) if rev else (lambda n: n)
    ng = GDN_K_HEADS // GDN_HB
    return at, [pl.BlockSpec((c, GDN_HB * d), lambda n, j: (at(n), j)),
                pl.BlockSpec((c, GDN_HB * d), lambda n, j: (at(n), ng + j)),
                pl.BlockSpec((c, GDN_HB * GDN_REP * d), lambda n, j: (at(n), ng + j)),
                pl.BlockSpec((c, 2 * GDN_V_HEADS), lambda n, j: (at(n), 0)),
                pl.BlockSpec((1, GDN_V_HEADS), lambda n, j: (0, 0)),
                pl.BlockSpec((1, GDN_V_HEADS), lambda n, j: (0, 0))]


def _gdn_fwd(act, ba, alog, dtb, *, name):
    t = act.shape[0]
    c, d = GDN_CHUNK, GDN_HEAD_DIM
    nchunk = t // c
    at, specs = _gdn_specs(nchunk, False)

    def body(q_ref, k_ref, v_ref, ba_ref, al_ref, db_ref, o_ref, s_ref, inv_ref, state):
        n, j = pl.program_id(0), pl.program_id(1)
        heads = pl.ds(j * GDN_HB, GDN_HB)

        @pl.when(n == 0)
        def _():
            state[heads] = jnp.zeros((GDN_HB, GDN_REP, d, d), F32)

        s_in = state[heads]
        s_ref[...] = s_in
        args = _gdn_stack(q_ref[...], k_ref[...], v_ref[...], (ba_ref[...], al_ref[...], db_ref[...]), j)
        o, s_new, inv_ref[...] = _gdn_chunk(*args, s_in.reshape(GDN_HB * GDN_REP, d, d), keep_inverse=True)
        o_ref[...] = jnp.concatenate([o[b] for b in range(GDN_HB * GDN_REP)], axis=1)
        state[heads] = s_new.reshape(GDN_HB, GDN_REP, d, d)

    return pl.pallas_call(
        body, name=name, grid=(nchunk, GDN_K_HEADS // GDN_HB), in_specs=specs,
        out_specs=[pl.BlockSpec((c, GDN_HB * GDN_REP * d), lambda n, j: (n, j)),
                   pl.BlockSpec((None, GDN_HB, GDN_REP, d, d), lambda n, j: (n, j, 0, 0, 0)),
                   pl.BlockSpec((None, GDN_HB * GDN_REP, c, c), lambda n, j: (n, j, 0, 0))],
        out_shape=[jax.ShapeDtypeStruct((t, GDN_VW), F32),
                   jax.ShapeDtypeStruct((nchunk, GDN_K_HEADS, GDN_REP, d, d), F32),
                   jax.ShapeDtypeStruct((nchunk, GDN_V_HEADS, c, c), F32)],
        scratch_shapes=[pltpu.VMEM((GDN_K_HEADS, GDN_REP, d, d), F32)],
        compiler_params=_params("arbitrary", "arbitrary"),
    )(act, act, act, ba, alog, dtb)


def _gdn_bwd(act, ba, alog, dtb, states, inverses, dout, *, name):
    assert GDN_HB == GDN_K_HEADS
    t = act.shape[0]
    c, d = GDN_CHUNK, GDN_HEAD_DIM
    nchunk = t // c
    at, specs = _gdn_specs(nchunk, True)

    def body(q_ref, k_ref, v_ref, ba_ref, al_ref, db_ref, s_ref, inv_ref, do_ref,
             dact_ref, dba_ref, dal_ref, ddb_ref, dstate):
        n, j = pl.program_id(0), pl.program_id(1)

        @pl.when(n == 0)
        def _():
            dstate[pl.ds(j * GDN_HB, GDN_HB)] = jnp.zeros((GDN_HB, GDN_REP, d, d), F32)

        @pl.when((n == 0) & (j == 0))
        def _():
            dal_ref[...] = jnp.zeros_like(dal_ref)
            ddb_ref[...] = jnp.zeros_like(ddb_ref)

        @pl.when(j == 0)
        def _():
            dba_ref[...] = jnp.zeros_like(dba_ref)

        heads = pl.ds(j * GDN_HB, GDN_HB)
        nh = GDN_HB * GDN_REP
        args = _gdn_stack(q_ref[...], k_ref[...], v_ref[...], (ba_ref[...], al_ref[...], db_ref[...]), j)
        _, vjp = jax.vjp(functools.partial(_gdn_chunk, inverse=inv_ref[...]), *args, s_ref[...].reshape(nh, d, d))
        do = do_ref[...]
        do = jnp.stack([do[:, b * d:(b + 1) * d] for b in range(nh)])
        gq, gk, gv, gbl, ga, gal, gdb, gs = vjp((do, dstate[heads].reshape(nh, d, d)))
        dstate[heads] = gs.reshape(GDN_HB, GDN_REP, d, d)
        dact_ref[...] = jnp.concatenate([gq[GDN_REP * hh] + gq[GDN_REP * hh + 1] for hh in range(GDN_HB)]
                                        + [gk[GDN_REP * hh] + gk[GDN_REP * hh + 1] for hh in range(GDN_HB)]
                                        + [gv[b] for b in range(nh)], axis=1)
        lane = lax.broadcasted_iota(jnp.int32, (c, 2 * GDN_V_HEADS), 1)
        lane1 = lax.broadcasted_iota(jnp.int32, (1, GDN_V_HEADS), 1)
        dba = jnp.zeros((c, 2 * GDN_V_HEADS), F32)
        dal = jnp.zeros((1, GDN_V_HEADS), F32)
        ddb = jnp.zeros((1, GDN_V_HEADS), F32)
        for b in range(nh):
            hv = j * nh + b
            dba = dba + jnp.where(lane == hv, gbl[b], 0.0) + jnp.where(lane == GDN_V_HEADS + hv, ga[b], 0.0)
            dal = dal + jnp.where(lane1 == hv, gal[b], 0.0)
            ddb = ddb + jnp.where(lane1 == hv, gdb[b], 0.0)
        dba_ref[...] += dba
        dal_ref[...] += dal
        ddb_ref[...] += ddb

    small = pl.BlockSpec((1, GDN_V_HEADS), lambda n, j: (0, 0))
    return pl.pallas_call(
        body, name=name, grid=(nchunk, GDN_K_HEADS // GDN_HB),
        in_specs=specs + [pl.BlockSpec((None, GDN_HB, GDN_REP, d, d), lambda n, j: (at(n), j, 0, 0, 0)),
                          pl.BlockSpec((None, GDN_HB * GDN_REP, c, c), lambda n, j: (at(n), j, 0, 0)),
                          pl.BlockSpec((c, GDN_HB * GDN_REP * d), lambda n, j: (at(n), j))],
        out_specs=[pl.BlockSpec((c, GDN_CONV_W), lambda n, j: (at(n), 0)),
                   pl.BlockSpec((c, 2 * GDN_V_HEADS), lambda n, j: (at(n), 0)),
                   small, small],
        out_shape=[jax.ShapeDtypeStruct((t, GDN_CONV_W), F32), jax.ShapeDtypeStruct((t, 2 * GDN_V_HEADS), F32),
                   jax.ShapeDtypeStruct((1, GDN_V_HEADS), F32), jax.ShapeDtypeStruct((1, GDN_V_HEADS), F32)],
        scratch_shapes=[pltpu.VMEM((GDN_K_HEADS, GDN_REP, d, d), F32)],
        compiler_params=_params("arbitrary", "arbitrary"),
    )(act, act, act, ba, alog, dtb, states, inverses, dout)


N_DEV = 8
ANY = pl.BlockSpec(memory_space=pl.ANY)


def _coords():
    return lax.axis_index("x"), lax.axis_index("y"), lax.axis_index("c")


def _other_chips(x, y):
    return [(1 - x, y), (x, 1 - y), (1 - x, 1 - y)]


def _remote(src, dst, send_sems, recv_sems, k, to):
    return pltpu.make_async_remote_copy(src_ref=src, dst_ref=dst, send_sem=send_sems.at[k], recv_sem=recv_sems.at[k],
                                        device_id=to, device_id_type=MESH)


def _dma_sems(n):
    return [pltpu.SemaphoreType.DMA((n,)), pltpu.SemaphoreType.DMA((n,))]


def _gather_copies(ins, outs, send_sems, recv_sems, local_sems, only_first=False):
    x, y, c = _coords()
    sibling = (x, y, 1 - c)
    local, sends, arrivals, relays, relayed = [], [], [], [], []
    for a, (x_ref, out_ref) in enumerate(zip(ins, outs)):
        local.append(pltpu.make_async_copy(x_ref, out_ref.at[2 * x + y], local_sems.at[a]))
        for j, (cx, cy) in enumerate(_other_chips(x, y)):
            k, theirs = 6 * a + j, 2 * cx + cy
            sends.append(_remote(x_ref.at[c], out_ref.at[2 * x + y, c], send_sems, recv_sems, k, (cx, cy, c)))
            if only_first:
                continue
            arrivals.append(_remote(x_ref.at[c], out_ref.at[theirs, c], send_sems, recv_sems, k, (cx, cy, c)))
            relays.append(_remote(out_ref.at[theirs, c], out_ref.at[theirs, c], send_sems, recv_sems, k + 3, sibling))
            relayed.append(_remote(x_ref.at[c], out_ref.at[theirs, 1 - c], send_sems, recv_sems, k + 3, sibling))
    return local, sends, arrivals, relays, relayed


def _gather_start(copies):
    local, sends, _, _, _ = copies
    for cp in local + sends:
        cp.start()


def _gather_finish(copies):
    local, sends, arrivals, relays, relayed = copies
    for landed, relay in zip(arrivals, relays):
        landed.wait_recv()
        relay.start()
    for cp in relayed:
        cp.wait_recv()
    for cp in sends + relays:
        cp.wait_send()
    for cp in local:
        cp.wait()


def _gather_scratch(na):
    return _dma_sems(6 * na) + [pltpu.SemaphoreType.DMA((na,))]


def _gather_quarters(parts, *, name):
    na = len(parts)

    def body(*refs):
        copies = _gather_copies(refs[:na], refs[na:2 * na], *refs[2 * na:])
        _gather_start(copies)
        _gather_finish(copies)

    return pl.pallas_call(
        body, name=name, in_specs=[ANY] * na, out_specs=[ANY] * na,
        out_shape=[jax.ShapeDtypeStruct((N_CHIPS,) + p.shape, p.dtype) for p in parts],
        scratch_shapes=_gather_scratch(na),
    )(*parts)


def _swap_halves(grads, *, name):
    na = len(grads)

    def body(*refs):
        ins, outs = refs[:na], refs[na:2 * na]
        send_sems, recv_sems = refs[2 * na:]
        x, y, c = _coords()
        sends = [_remote(g_ref.at[j, 1 - c], o_ref.at[j], send_sems, recv_sems, N_CHIPS * a + j, (x, y, 1 - c))
                 for a, (g_ref, o_ref) in enumerate(zip(ins, outs)) for j in range(N_CHIPS)]
        for cp in sends:
            cp.start()
        for cp in sends:
            cp.wait()

    return pl.pallas_call(
        body, name=name, in_specs=[ANY] * na, out_specs=[ANY] * na,
        out_shape=[jax.ShapeDtypeStruct((N_CHIPS,) + g.shape[2:], g.dtype) for g in grads],
        scratch_shapes=_dma_sems(N_CHIPS * na),
    )(*grads)


def _scatter_copies(ins, outs, send_sems, recv_sems):
    x, y, c = _coords()
    return [_remote(p_ref.at[2 * cx + cy], outs[3 * a + j], send_sems, recv_sems, 3 * a + j, (cx, cy, c))
            for a, p_ref in enumerate(ins) for j, (cx, cy) in enumerate(_other_chips(x, y))]


def _scatter_shapes(pairs):
    return [jax.ShapeDtypeStruct(p.shape[1:], p.dtype) for p in pairs for _ in range(3)]


def _scatter_quarters(pairs, *, name):
    na = len(pairs)

    def body(*refs):
        sends = _scatter_copies(refs[:na], refs[na:4 * na], *refs[4 * na:])
        for cp in sends:
            cp.start()
        for cp in sends:
            cp.wait()

    out = pl.pallas_call(
        body, name=name, in_specs=[ANY] * na, out_specs=[ANY] * (3 * na), out_shape=_scatter_shapes(pairs),
        scratch_shapes=_dma_sems(3 * na),
    )(*pairs)
    return [out[3 * a:3 * a + 3] for a in range(na)]


def _share_halves(tots, *, name):
    na = len(tots)

    def body(*refs):
        ins, outs = refs[:na], refs[na:2 * na]
        send_sems, recv_sems = refs[2 * na:]
        x, y, c = _coords()
        sends = [_remote(t_ref, o_ref, send_sems, recv_sems, a, (x, y, 1 - c))
                 for a, (t_ref, o_ref) in enumerate(zip(ins, outs))]
        for cp in sends:
            cp.start()
        for cp in sends:
            cp.wait()

    return pl.pallas_call(
        body, name=name, in_specs=[ANY] * na, out_specs=[ANY] * na,
        out_shape=[jax.ShapeDtypeStruct(t.shape, t.dtype) for t in tots],
        scratch_shapes=_dma_sems(na),
    )(*tots)


def _gather_all(vec, *, name):
    m, w = vec.shape

    def body(x_ref, out_ref, send_sems, recv_sems, local_sem):
        x, y, c = _coords()
        me, sibling = (x, y, c), (x, y, 1 - c)
        chips = _other_chips(x, y)

        def rows(px, py, pc):
            return out_ref.at[pl.ds((4 * px + 2 * py + pc) * m, m), :]

        def copy(k, block, to, src=None):
            return _remote(rows(*block) if src is None else src, rows(*block), send_sems, recv_sems, k, to)

        mine = pltpu.make_async_copy(x_ref, rows(*me), local_sem)
        mine.start()
        first = [copy(0, me, sibling, src=x_ref)]
        first += [copy(1 + j, me, (*chip, c), src=x_ref) for j, chip in enumerate(chips)]
        for cp in first:
            cp.start()
        passed = [copy(4 + j, (*chip, c), sibling) for j, chip in enumerate(chips)]
        for j, chip in enumerate(chips):
            copy(1 + j, (*chip, c), me).wait_recv()
            passed[j].start()
        copy(0, sibling, me).wait_recv()
        for j, chip in enumerate(chips):
            copy(4 + j, (*chip, 1 - c), me).wait_recv()
        for cp in first + passed:
            cp.wait_send()
        mine.wait()

    vm = pl.BlockSpec(memory_space=pltpu.VMEM)
    return pl.pallas_call(
        body, name=name, in_specs=[vm], out_specs=vm, out_shape=jax.ShapeDtypeStruct((N_DEV * m, w), vec.dtype),
        scratch_shapes=_dma_sems(7) + [pltpu.SemaphoreType.DMA(())],
    )(vec)


def _sum_blocks(allv, n, *, name):
    m = allv.shape[0] // n

    def body(a_ref, o_ref):
        acc = a_ref[0:m, :]
        for d in range(1, n):
            acc = acc + a_ref[d * m:(d + 1) * m, :]
        o_ref[...] = acc

    return pl.pallas_call(body, name=name, out_shape=jax.ShapeDtypeStruct((m, allv.shape[1]), allv.dtype))(allv)


EW_BLOCK_BYTES = 1 << 20


def _ew_rows(rows, w):
    return _tile(rows, max(8, (EW_BLOCK_BYTES // (4 * w)) // 8 * 8), 8)


def _add_pair(g, got, c, *, name):
    _, _, rows, w = g.shape
    tr = _ew_rows(rows, w)

    def body(c_ref, g_ref, got_ref, o_ref):
        o_ref[...] = (g_ref[...] + got_ref[...]).astype(o_ref.dtype)

    blk = pl.BlockSpec((None, tr, w), lambda q, i, c_ref: (q, i, 0))
    return pl.pallas_call(
        body, name=name,
        grid_spec=pltpu.PrefetchScalarGridSpec(
            num_scalar_prefetch=1, grid=(N_CHIPS, rows // tr),
            in_specs=[pl.BlockSpec((None, None, tr, w), lambda q, i, c_ref: (q, c_ref[0], i, 0)), blk], out_specs=blk),
        out_shape=jax.ShapeDtypeStruct(got.shape, BF16),
        compiler_params=_params("parallel", "parallel"),
    )(c, g, got)


def _add_chips(pair, recv, chip, *, name):
    _, rows, w = pair.shape
    tr = _ew_rows(rows, w)

    def body(chip_ref, p_ref, r0_ref, r1_ref, r2_ref, o_ref):
        f = lambda r: r[...].astype(F32)
        o_ref[...] = ((f(p_ref) + f(r0_ref)) + f(r1_ref)) + f(r2_ref)

    blk = pl.BlockSpec((tr, w), lambda i, chip_ref: (i, 0))
    return pl.pallas_call(
        body, name=name,
        grid_spec=pltpu.PrefetchScalarGridSpec(
            num_scalar_prefetch=1, grid=(rows // tr,),
            in_specs=[pl.BlockSpec((None, tr, w), lambda i, chip_ref: (chip_ref[0], i, 0)), blk, blk, blk], out_specs=blk),
        out_shape=jax.ShapeDtypeStruct((rows, w), F32),
        compiler_params=_params("parallel"),
    )(chip, pair, *recv)


def _adamw_math(w, g, m, v):
    nm = ADAM_B1 * m + (1.0 - ADAM_B1) * g
    nv = ADAM_B2 * v + (1.0 - ADAM_B2) * (g * g)
    m_hat = nm / (1.0 - ADAM_B1 ** ADAM_STEP)
    v_hat = nv / (1.0 - ADAM_B2 ** ADAM_STEP)
    return -ADAM_LR * (m_hat / (jnp.sqrt(v_hat) + ADAM_EPS) + ADAM_WD * w), nm, nv


def _adamw(w, g, m, v, *, name):
    shape = w.shape
    last = shape[-1]
    w2, g2, m2, v2 = (a.reshape(-1, last) for a in (w, g, m, v))
    rows = w2.shape[0]
    tm = _ew_rows(rows, last)

    def body(w_ref, g_ref, m_ref, v_ref, d_ref, nm_ref, nv_ref):
        d_ref[...], nm_ref[...], nv_ref[...] = _adamw_math(w_ref[...], g_ref[...], m_ref[...], v_ref[...])

    spec = pl.BlockSpec((tm, last), lambda i: (i, 0))
    out = jax.ShapeDtypeStruct((rows, last), F32)
    d, nm, nv = pl.pallas_call(
        body, name=name, grid=(rows // tm,), in_specs=[spec] * 4, out_specs=[spec] * 3, out_shape=[out] * 3,
        compiler_params=_params("parallel"),
    )(w2, g2, m2, v2)
    return d.reshape(shape), nm.reshape(shape), nv.reshape(shape)


def _adamw_halves(w, m, v, mine, theirs, c, *, name, into=None):
    rows, wd = w.shape[-2:]
    tr = _ew_rows(rows, wd)
    bufs, at = into if into is not None else ((), ())

    def body(c_ref, w_ref, m_ref, v_ref, a_ref, b_ref, *rest):
        g_ref, d_ref, nm_ref, nv_ref = rest[len(bufs):]
        g = jnp.where(pl.program_id(0) == c_ref[0], a_ref[...], b_ref[...])
        g_ref[...] = g
        d_ref[...], nm_ref[...], nv_ref[...] = _adamw_math(w_ref[...], g, m_ref[...], v_ref[...])

    full = pl.BlockSpec((None,) * (1 + len(at)) + (tr, wd), lambda hf, i, c_ref: at + (hf, i, 0))
    mine_spec = pl.BlockSpec((tr, wd), lambda hf, i, c_ref: (jnp.where(hf == c_ref[0], i, 0), 0))
    theirs_spec = pl.BlockSpec((tr, wd), lambda hf, i, c_ref: (jnp.where(hf == c_ref[0], 0, i), 0))
    out = jax.ShapeDtypeStruct(w.shape, F32)
    return pl.pallas_call(
        body, name=name,
        grid_spec=pltpu.PrefetchScalarGridSpec(num_scalar_prefetch=1, grid=(2, rows // tr),
                                               in_specs=[full] * 3 + [mine_spec, theirs_spec] + [ANY] * len(bufs),
                                               out_specs=[full] * 4),
        out_shape=[out] * 4, input_output_aliases={6 + b: b for b in range(len(bufs))},
        compiler_params=_params("parallel", "parallel"),
    )(c, w, m, v, mine, theirs, *bufs)


def _join_quarters(q, *, name):
    _, rows, n = q.shape
    tr = _tile(rows, 256, 16)

    def body(q_ref, o_ref):
        o_ref[...] = jnp.concatenate([q_ref[s] for s in range(N_CHIPS)], axis=1)

    return pl.pallas_call(
        body, name=name, grid=(rows // tr,),
        in_specs=[pl.BlockSpec((N_CHIPS, tr, n), lambda i: (0, i, 0))],
        out_specs=pl.BlockSpec((tr, N_CHIPS * n), lambda i: (i, 0)),
        out_shape=jax.ShapeDtypeStruct((rows, N_CHIPS * n), q.dtype),
        compiler_params=_params("parallel"),
    )(q)


def _split_quarters(pieces, *, name):
    rows = pieces[0].shape[0]
    n = sum(p.shape[1] for p in pieces) // N_CHIPS
    tr = _tile(rows, 256, 16)

    def body(*refs):
        x = jnp.concatenate([r[...] for r in refs[:-1]], axis=1)
        for s in range(N_CHIPS):
            refs[-1][s] = x[:, s * n:(s + 1) * n]

    return pl.pallas_call(
        body, name=name, grid=(rows // tr,),
        in_specs=[pl.BlockSpec((tr, p.shape[1]), lambda i: (i, 0)) for p in pieces],
        out_specs=pl.BlockSpec((N_CHIPS, tr, n), lambda i: (0, i, 0)),
        out_shape=jax.ShapeDtypeStruct((N_CHIPS, rows, n), pieces[0].dtype),
        compiler_params=_params("parallel"),
    )(*pieces)


_WEIGHTS = ['ffn_norm', 'ffn_w_gate', 'ffn_w_up', 'ffn_w_down', 'mix_norm', 'att_w_in', 'att_q_norm', 'att_k_norm',
            'att_sinks', 'att_w_out', 'gdn_w_in', 'gdn_conv_w', 'gdn_a_log', 'gdn_dt_bias', 'gdn_out_norm', 'gdn_w_out',
            'ple_norm', 'ple_w_gate', 'ple_w_proj']
_BIG = ['ffn_w_gate', 'ffn_w_up', 'ffn_w_down', 'att_w_in', 'att_w_out', 'gdn_w_in', 'gdn_w_out', 'ple_w_gate',
        'ple_w_proj']
_SMALL_CUT = {'ffn_norm': 2, 'gdn_conv_w': 2}
_WHOLE = ['mix_norm', 'att_q_norm', 'att_k_norm', 'att_sinks', 'gdn_a_log', 'gdn_dt_bias', 'gdn_out_norm', 'ple_norm']
PACK_W = 1024
SMALL_ROW_MULT = 8


def _halves(a):
    return a.reshape(2, -1, a.shape[-1])


def _from_quarters(blk, axis):
    full = jnp.moveaxis(blk, 0, axis)
    shp = list(full.shape)
    shp[axis:axis + 2] = [shp[axis] * shp[axis + 1]]
    return full.reshape(shp)


def _to_quarters(full, axis):
    shp = list(full.shape)
    shp[axis:axis + 1] = [N_CHIPS, shp[axis] // N_CHIPS]
    return jnp.moveaxis(full.reshape(shp), axis, 0)


def _pack(parts, row_mult):
    flat = jnp.concatenate(parts, axis=-1)
    n = flat.shape[-1]
    rows = -(-n // (PACK_W * row_mult)) * row_mult
    return jnp.pad(flat, [(0, rows * PACK_W - n)]).reshape(rows, PACK_W)


def _unpack(flat, shapes):
    lead = flat.shape[:-2]
    flat = flat.reshape(lead + (-1,))
    out, off = [], 0
    for shp in shapes:
        n = math.prod(shp)
        out.append(flat[..., off:off + n].reshape(lead + tuple(shp)))
        off += n
    return out


FFN_TM = 1024
FFN_DW_TK = 2048


def _ffn_up(hn, wg, wu, at, *, name):
    t, d = hn.shape
    fq = wg.shape[-1]
    tm = _tile(t, FFN_TM)

    def body(h_ref, wg_ref, wu_ref, g_ref, u_ref, a_ref):
        h = h_ref[...]
        g, u = _dg(h, _b(wg_ref[...]), 1, 0), _dg(h, _b(wu_ref[...]), 1, 0)
        g_ref[...] = g.astype(BF16)
        u_ref[...] = u.astype(BF16)
        a_ref[...] = _f_swiglu(g, u)[0].astype(BF16)

    w_spec = pl.BlockSpec((None,) * (1 + len(at)) + (d, fq), lambda s, i: (s,) + at + (0, 0))
    o_spec = pl.BlockSpec((None, tm, fq), lambda s, i: (s, i, 0))
    out = jax.ShapeDtypeStruct((N_CHIPS, t, fq), BF16)
    return pl.pallas_call(
        body, name=name, grid=(N_CHIPS, t // tm),
        in_specs=[pl.BlockSpec((tm, d), lambda s, i: (i, 0)), w_spec, w_spec], out_specs=[o_spec] * 3,
        out_shape=[out] * 3, compiler_params=_params("parallel", "parallel"),
    )(hn, wg, wu)


def _ffn_d_up(dout, wd, g, u, at, *, name):
    t, d = dout.shape
    fq = wd.shape[-2]
    tm = _tile(t, FFN_TM)

    def body(do_ref, wd_ref, g_ref, u_ref, dg_ref, du_ref):
        da = _dg(_b(do_ref[...]), _b(wd_ref[...]), 1, 1) * 0.5
        _, vjp = jax.vjp(_f_swiglu, g_ref[...].astype(F32), u_ref[...].astype(F32))
        dg, du = vjp((da,))
        dg_ref[...] = dg.astype(BF16)
        du_ref[...] = du.astype(BF16)

    w_spec = pl.BlockSpec((None,) * (1 + len(at)) + (fq, d), lambda i, s: (s,) + at + (0, 0))
    o_spec = pl.BlockSpec((None, tm, fq), lambda i, s: (s, i, 0))
    out = jax.ShapeDtypeStruct((N_CHIPS, t, fq), BF16)
    return pl.pallas_call(
        body, name=name, grid=(t // tm, N_CHIPS),
        in_specs=[pl.BlockSpec((tm, d), lambda i, s: (i, 0)), w_spec, o_spec, o_spec], out_specs=[o_spec] * 2,
        out_shape=[out] * 2, compiler_params=_params("parallel", "parallel"),
    )(dout, wd, g, u)


def _ffn_fwd(h, gain, wg, wu, wd, at, tag):
    lead = (Q,) + at
    hn, = _row_fwd(_f_rms, [h], [gain], [(D_MODEL, BF16)], name=f"{tag}_norm")
    g, u, a = _ffn_up(hn, wg, wu, at, name=f"{tag}_up")
    out = _mm((a, (Q,)), (wd, lead), res=h, scale=0.5, name=f"{tag}_down")
    return out, (h, hn, g, u, a)


def _ffn_bwd(dout, saved, gain, wg, wu, wd, at, grads, g_at, tag):
    h, hn, g, u, a = saved
    lead = (Q,) + at
    g_lead = (Q,) + g_at
    dg, du = _ffn_d_up(dout, wd, g, u, at, name=f"{tag}_d_up")
    g_gate, g_up, g_down = grads
    g_down = _mm((a, (Q,)), dout, ta=True, scale=0.5, into=(g_down, g_lead), tk=FFN_DW_TK, name=f"{tag}_dw_down")
    g_gate = _mm((dg, (Q,)), hn, ta=True, into=(g_gate, g_lead), tk=FFN_DW_TK, name=f"{tag}_dw_gate")
    g_up = _mm((du, (Q,)), hn, ta=True, into=(g_up, g_lead), tk=FFN_DW_TK, name=f"{tag}_dw_up")
    dhn = _mm((dg, (Q,)), (wg, lead), tb=True, name=f"{tag}_d_norm_gate")
    dh, dgain = _mm((du, (Q,)), (wu, lead), tb=True, res=dhn, norm_bwd=(h, gain, dout), name=f"{tag}_d_in")
    return dh, dgain, (g_gate, g_up, g_down)


def _att_fwd(h, gain, w_in, qg, kg, sinks, w_out, gather):
    hn, = _row_fwd(_f_rms, [h], [gain], [(D_MODEL, BF16)], name="att_norm")
    proj = _mm(hn, w_in, out_dtype=BF16, name="att_in")
    a, rtot, *gathered = _sb_fwd(proj, name="att_sb", gather=gather)
    b = _swa_fwd(proj, qg, kg, sinks, name="att_swa")
    out = _mm(a, (w_out, (0,)), res=h, name="att_out_sb")
    out = _mm(b, (w_out, (1,)), res=out, name="att_out_swa")
    return out, (h, hn, proj, a, rtot, b), gathered


def _att_bwd(dout, saved, gain, w_in, qg, kg, sinks, w_out, scatter):
    h, hn, proj, a, rtot, b = saved
    da = _mm(dout, (w_out, (0,)), tb=True, name="att_d_sb")
    db = _mm(dout, (w_out, (1,)), tb=True, name="att_d_swa")
    dw_out = lax.empty(w_out.shape, F32)
    dw_out = _mm(a, dout, ta=True, into=(dw_out, (0,)), name="att_dw_out_sb")
    dw_out = _mm(b, dout, ta=True, into=(dw_out, (1,)), name="att_dw_out_swa")
    dq, dk, dv, *landed = _sb_bwd(proj, rtot, da, name="att_sb_bwd", scatter=scatter)
    dqb, dkb, dvb, dqg, dkg, dsk = _swa_bwd(proj, qg, kg, sinks, db, name="att_swa_bwd")
    dproj = jnp.concatenate([dq, dk, dv, dqb, dkb, dvb], axis=1)
    dw_in = _mm(hn, dproj, ta=True, name="att_dw_in")
    dh, dgain = _mm(dproj, w_in, tb=True, norm_bwd=(h, gain, dout), name="att_d_in")
    return dh, dgain, dw_in, dqg, dkg, dsk, dw_out, [landed[3 * a:3 * a + 3] for a in range(len(scatter))]


def _gdn_layer_fwd(h, gain, w_in, conv_w, alog, dtb, out_gain, w_out):
    w_qkv, w_z, w_ba = w_in[:, :GDN_CONV_W], w_in[:, GDN_CONV_W:GDN_CONV_W + GDN_VW], w_in[:, GDN_CONV_W + GDN_VW:]
    hn, = _row_fwd(_f_rms, [h], [gain], [(D_MODEL, BF16)], name="gdn_norm")
    pq = _mm(hn, w_qkv, name="gdn_in_qkv")
    pz = _mm(hn, w_z, name="gdn_in_z")
    ba = _mm(hn, w_ba, name="gdn_in_ba")
    act = _conv_fwd(pq, conv_w, name="gdn_conv")
    o, states, inverses = _gdn_fwd(act, ba, alog, dtb, name="gdn_rule")
    y, = _row_fwd(_f_gdn_out, [o, pz], [out_gain], [(GDN_VW, BF16)], name="gdn_gate")
    out = _mm(y, w_out, res=h, name="gdn_out")
    return out, (h, hn, pq, pz, ba, act, o, states, inverses, y, (w_qkv, w_z, w_ba))


def _gdn_layer_bwd(dout, saved, gain, conv_w, alog, dtb, out_gain, w_out):
    h, hn, pq, pz, ba, act, o, states, inverses, y, (w_qkv, w_z, w_ba) = saved
    dy = _mm(dout, w_out, tb=True, name="gdn_d_gate")
    dw_out = _mm(y, dout, ta=True, name="gdn_dw_out")
    do, dpz, dout_gain = _row_bwd(_f_gdn_out, [o, pz], [out_gain], [dy], [(0, F32), (1, F32)], [0], name="gdn_gate_bwd")
    dact, dba, dal, ddb = _gdn_bwd(act, ba, alog, dtb, states, inverses, do, name="gdn_rule_bwd")
    dpq, dconv = _conv_bwd(pq, conv_w, dact, name="gdn_conv_bwd")
    dw_in = [_mm(hn, dpq, ta=True, name="gdn_dw_qkv"), _mm(hn, dpz, ta=True, name="gdn_dw_z"),
             _mm(hn, dba, ta=True, name="gdn_dw_ba")]
    dhn = _mm(dpq, w_qkv, tb=True, name="gdn_d_norm_qkv")
    dhn = _mm(dpz, w_z, tb=True, res=dhn, name="gdn_d_norm_z")
    dh, dgain = _mm(dba, w_ba, tb=True, res=dhn, norm_bwd=(h, gain, dout), name="gdn_d_in")
    return dh, dgain, dw_in, dconv, dal, ddb, dout_gain, dw_out


def _ple_fwd(h, gain, w_gate, w_proj, pe, tag):
    hn, = _row_fwd(_f_rms, [h], [gain], [(D_MODEL, BF16)], name=f"{tag}_norm")
    gl = _mm(hn, w_gate, name=f"{tag}_gate")
    pp = _mm(pe, w_proj, name=f"{tag}_proj")
    out, = _row_fwd(_f_ple, [h, gl, pp], [], [(D_MODEL, F32)], name=f"{tag}_mix")
    return out, (h, hn, gl, pp)


def _ple_bwd(dout, saved, gain, w_gate, pe, tag):
    h, hn, gl, pp = saved
    dha, dgl, dpp = _row_bwd(_f_ple, [h, gl, pp], [], [dout], [(0, F32), (1, BF16), (2, BF16)], [], name=f"{tag}_mix_bwd")
    dw_gate = _mm(hn, dgl, ta=True, name=f"{tag}_dw_gate")
    dw_proj = _mm(pe, dpp, ta=True, name=f"{tag}_dw_proj")
    dh, dgain = _mm(dgl, w_gate, tb=True, norm_bwd=(h, gain, dha), name=f"{tag}_d_in")
    return dh, dgain, dw_gate, dw_proj


def kernel(x, p, ffn_norm, ffn_w_gate, ffn_w_up, ffn_w_down, mix_norm, att_w_in, att_q_norm, att_k_norm, att_sinks, att_w_out, gdn_w_in, gdn_conv_w, gdn_a_log, gdn_dt_bias, gdn_out_norm, gdn_w_out, ple_norm, ple_w_gate, ple_w_proj, loss_target, m_ffn_norm, m_ffn_w_gate, m_ffn_w_up, m_ffn_w_down, m_mix_norm, m_att_w_in, m_att_q_norm, m_att_k_norm, m_att_sinks, m_att_w_out, m_gdn_w_in, m_gdn_conv_w, m_gdn_a_log, m_gdn_dt_bias, m_gdn_out_norm, m_gdn_w_out, m_ple_norm, m_ple_w_gate, m_ple_w_proj, v_ffn_norm, v_ffn_w_gate, v_ffn_w_up, v_ffn_w_down, v_mix_norm, v_att_w_in, v_att_q_norm, v_att_k_norm, v_att_sinks, v_att_w_out, v_gdn_w_in, v_gdn_conv_w, v_gdn_a_log, v_gdn_dt_bias, v_gdn_out_norm, v_gdn_w_out, v_ple_norm, v_ple_w_gate, v_ple_w_proj):
    arg = dict(locals())
    cx, cy, cc = _coords()
    chip = (2 * cx + cy).astype(jnp.int32).reshape(1)
    core = cc.astype(jnp.int32).reshape(1)
    n_layers = ffn_norm.shape[0]

    quarter = lambda n, i=None: _halves((arg[n] if i is None else arg[n][i]).astype(BF16))
    ffn_names = ('ffn_w_gate', 'ffn_w_up', 'ffn_w_down')
    early = [quarter(n, 0) for n in ffn_names] + [quarter('att_w_in'), quarter('att_w_out')]
    late_names = ('gdn_w_in', 'gdn_w_out', 'ple_w_gate', 'ple_w_proj')
    late = [quarter(n, 1) for n in ffn_names] + [quarter(n) for n in late_names]
    *ffn_w0, att_in_q, att_out_q = _gather_quarters(early, name="gather_weights")
    wt = {'att_w_in': _join_quarters(att_in_q.reshape((N_CHIPS,) + att_w_in.shape[1:]), name="att_w_in_join"),
          'att_w_out': att_out_q.reshape(2, SB_W, D_MODEL)}

    small_names = list(_SMALL_CUT)
    small_shapes = [arg[n].shape for n in small_names]
    svec = _pack([arg[n].reshape(-1) for n in small_names], SMALL_ROW_MULT)
    srows = svec.shape[0]
    sall = _gather_all(svec, name="gather_gains").reshape(N_CHIPS, 2, srows, PACK_W)[:, 0]
    for n, q in zip(small_names, _unpack(sall, small_shapes)):
        wt[n] = _from_quarters(q, _SMALL_CUT[n])
    row = lambda v: v.reshape(1, -1)

    as_ffn = lambda g, n: g.reshape((N_CHIPS,) + arg[n].shape[1:])
    ffn_w = [tuple(as_ffn(g, n) for g, n in zip(ffn_w0, ffn_names)), None]
    h = x[0]
    tape = []
    for i in range(n_layers):
        j = i // 2
        h, s0 = _ffn_fwd(h, row(wt['ffn_norm'][i, 0]), *ffn_w[i], (0,), f"ffn{i}a")
        if i % 2 == 0:
            h, sm, gathered = _att_fwd(h, row(mix_norm[i]), wt['att_w_in'], att_q_norm[j:j + 1], att_k_norm[j:j + 1],
                                       att_sinks[j:j + 1], wt['att_w_out'], late)
            ffn_w[1] = tuple(as_ffn(g, n) for g, n in zip(gathered[:3], ffn_names))
            wq = {n: g.reshape((N_CHIPS,) + arg[n].shape) for n, g in zip(late_names, gathered[3:])}
            wt['gdn_w_in'] = _join_quarters(wq['gdn_w_in'][:, 0], name="gdn_w_in_join")
            wt['gdn_w_out'] = wq['gdn_w_out'].reshape(GDN_VW, D_MODEL)
            wt['ple_w_gate'] = _from_quarters(wq['ple_w_gate'], 1)
            wt['ple_w_proj'] = _from_quarters(wq['ple_w_proj'], 2)
        else:
            h, sm = _gdn_layer_fwd(h, row(mix_norm[i]), wt['gdn_w_in'], wt['gdn_conv_w'][j], gdn_a_log[j:j + 1],
                                   gdn_dt_bias[j:j + 1], gdn_out_norm[j:j + 1], wt['gdn_w_out'])
        h, s1 = _ffn_fwd(h, row(wt['ffn_norm'][i, 1]), *ffn_w[i], (1,), f"ffn{i}b")
        h, sp = _ple_fwd(h, row(ple_norm[i]), wt['ple_w_gate'][i], wt['ple_w_proj'][i], p[i, 0], f"ple{i}")
        tape.append((s0, sm, s1, sp))

    dh, loss_local = _loss_head(h, loss_target[0], name="loss_head")
    loss = lax.psum(loss_local, ("x", "y", "c"))

    gr = {}
    stored_t = ('ffn_w_gate', 'ffn_w_up')
    as_stored = lambda a, n: jnp.swapaxes(a, -1, -2) if n in stored_t else a
    ffn_g = [[tuple(lax.empty((N_CHIPS,) + as_stored(arg[n], n).shape[2:], F32) for n in ffn_names) for _ in range(2)]
             for _ in range(n_layers)]
    ffn_keys = lambda i, k: [f"{n}_{i}{k}" for n in ffn_names]
    d_ffn_norm = [[None, None] for _ in range(n_layers)]
    d_mix, d_ple_norm, d_ple_gate, d_ple_proj = [None] * n_layers, [None] * n_layers, [None] * n_layers, [None] * n_layers

    def as_halves(g):
        return g.reshape((N_CHIPS, 2, -1, g.shape[-1]))

    def pair_up(keys, grads, tag):
        got = _swap_halves(grads, name=f"grad_swap_halves_{tag}")
        return [_add_pair(g, o, core, name=f"grad_add_pair_{k}") for k, g, o in zip(keys, grads, got)]

    for i in reversed(range(n_layers)):
        j = i // 2
        s0, sm, s1, sp = tape[i]
        dh, d_ple_norm[i], d_ple_gate[i], d_ple_proj[i] = _ple_bwd(dh, sp, row(ple_norm[i]), wt['ple_w_gate'][i], p[i, 0],
                                                                   f"ple{i}")
        dh, d_ffn_norm[i][1], ffn_g[i][1] = _ffn_bwd(dh, s1, row(wt['ffn_norm'][i, 1]), *ffn_w[i], (1,), ffn_g[i][1], (),
                                                     f"ffn{i}b")
        if i % 2 == 0:
            gr['ple_w_gate'] = _to_quarters(jnp.stack(d_ple_gate), 1)
            gr['ple_w_proj'] = _to_quarters(jnp.stack(d_ple_proj), 2)
            first_keys = ffn_keys(1, 0) + ffn_keys(1, 1) + ffn_keys(0, 1) + list(late_names)
            first_pairs = pair_up(first_keys, [as_halves(g) for g in ffn_g[1][0] + ffn_g[1][1] + ffn_g[0][1]]
                                  + [as_halves(gr[n]) for n in late_names], "a")
            (dh, d_mix[i], dw_in, gr['att_q_norm'], gr['att_k_norm'], gr['att_sinks'], dw_out,
             first_recv) = _att_bwd(dh, sm, row(mix_norm[i]), wt['att_w_in'], att_q_norm[j:j + 1],
                                    att_k_norm[j:j + 1], att_sinks[j:j + 1], wt['att_w_out'], first_pairs)
            gr['att_w_in'] = _split_quarters([dw_in], name="att_dw_in_split")
            gr['att_w_out'] = dw_out
        else:
            (dh, d_mix[i], dw_in, dconv, gr['gdn_a_log'], gr['gdn_dt_bias'], gr['gdn_out_norm'],
             dw_out) = _gdn_layer_bwd(dh, sm, row(mix_norm[i]), wt['gdn_conv_w'][j], gdn_a_log[j:j + 1],
                                      gdn_dt_bias[j:j + 1], gdn_out_norm[j:j + 1], wt['gdn_w_out'])
            gr['gdn_w_in'] = _split_quarters(dw_in, name="gdn_dw_in_split")
            gr['gdn_w_out'] = dw_out
            gr['gdn_conv_w'] = dconv[None]
        dh, d_ffn_norm[i][0], ffn_g[i][0] = _ffn_bwd(dh, s0, row(wt['ffn_norm'][i, 0]), *ffn_w[i], (0,), ffn_g[i][0], (),
                                                     f"ffn{i}a")
    grad_x = dh[None]

    gr['ffn_norm'] = jnp.stack([jnp.stack([d_ffn_norm[i][k][0] for k in range(2)]) for i in range(n_layers)])
    gr['mix_norm'] = jnp.concatenate(d_mix, axis=0)
    gr['ple_norm'] = jnp.concatenate(d_ple_norm, axis=0)

    last_keys = ffn_keys(0, 0) + ['att_w_in', 'att_w_out']
    last_pairs = pair_up(last_keys, [as_halves(g) for g in ffn_g[0][0]] + [as_halves(gr['att_w_in']), as_halves(gr['att_w_out'])],
                         "b")
    last_recv = _scatter_quarters(last_pairs, name="grad_scatter")
    keys = first_keys + last_keys
    tots = [_add_chips(pr, rc, chip, name=f"grad_add_chips_{k}")
            for k, pr, rc in zip(keys, first_pairs + last_pairs, first_recv + last_recv)]
    theirs = _share_halves(tots, name="grad_share")
    summed = dict(zip(keys, zip(tots, theirs)))

    whole_shapes = [arg[n].shape for n in _WHOLE]
    cut_full_shapes = [gr[n].shape for n in small_names]
    gvec = _pack([gr[n].reshape(-1) for n in _WHOLE + small_names], SMALL_ROW_MULT)
    gall = _sum_blocks(_gather_all(gvec, name="gather_small_grads"), N_DEV, name="sum_small_grads")
    parts = _unpack(gall, whole_shapes + cut_full_shapes)
    gsum = dict(zip(_WHOLE, parts))
    for n, g in zip(small_names, parts[len(_WHOLE):]):
        gsum[n] = lax.dynamic_index_in_dim(_to_quarters(g, _SMALL_CUT[n]), chip[0], axis=0, keepdims=False)

    delta, new_m, new_v = {}, {}, {}
    for n in ('att_w_in', 'att_w_out') + late_names:
        res = _adamw_halves(_halves(arg[n]), _halves(arg["m_" + n]), _halves(arg["v_" + n]), *summed[n], core,
                            name=f"adamw_{n}")
        gsum[n], delta[n], new_m[n], new_v[n] = (r.reshape(arg[n].shape) for r in res)
    for n in ffn_names:
        stored = as_stored(arg[n], n).shape
        in_halves = lambda a: as_stored(a, n).reshape(stored[:2] + (2, stored[2] // 2, stored[3]))
        wmv = [in_halves(arg[k + n]) for k in ("", "m_", "v_")]
        res = tuple(lax.empty(wmv[0].shape, F32) for _ in range(4))
        for i in range(n_layers):
            for k in range(2):
                res = _adamw_halves(*wmv, *summed[f"{n}_{i}{k}"], core, name=f"adamw_{n}_{i}{k}", into=(res, (i, k)))
        gsum[n], delta[n], new_m[n], new_v[n] = (as_stored(r.reshape(stored), n) for r in res)
    for n in _WHOLE + small_names:
        delta[n], new_m[n], new_v[n] = _adamw(arg[n], gsum[n], arg["m_" + n], arg["v_" + n], name=f"adamw_{n}")
    return (loss, grad_x, *[gsum[n] for n in _WEIGHTS], *[delta[n] for n in _WEIGHTS],
            *[new_m[n] for n in _WEIGHTS], *[new_v[n] for n in _WEIGHTS])
```

```python
import functools
import math

import jax
import jax.numpy as jnp
from jax import lax
from jax.experimental import pallas as pl
from jax.experimental.pallas import tpu as pltpu

F32 = jnp.float32
BF16 = jnp.bfloat16
MESH = pl.DeviceIdType.MESH

LANES = 128
VMEM_LIMIT_BYTES = 56 * 1024 * 1024

EPS = 1e-6
D_MODEL = 1024
HEAD_DIM = 64
SB_HEADS = 8
SWA_HEADS = 8
SWA_KV_HEADS = 2
WINDOW = 128
GDN_K_HEADS = 8
GDN_V_HEADS = 16
GDN_HEAD_DIM = 128
GDN_CONV = 4
GDN_CHUNK = 64
SB_W = SB_HEADS * HEAD_DIM
SWA_QW = SWA_HEADS * HEAD_DIM
SWA_KVW = SWA_KV_HEADS * HEAD_DIM
GDN_KW = GDN_K_HEADS * GDN_HEAD_DIM
GDN_VW = GDN_V_HEADS * GDN_HEAD_DIM
GDN_CONV_W = 2 * GDN_KW + GDN_VW

ADAM_LR = 0.001
ADAM_B1 = 0.9
ADAM_B2 = 0.999
ADAM_EPS = 1e-08
ADAM_WD = 0.01
ADAM_STEP = 10

NEG = -1e30


def _params(*sem):
    return pltpu.CompilerParams(dimension_semantics=sem or None, vmem_limit_bytes=VMEM_LIMIT_BYTES)


def _tile(n, cap, align=LANES):
    if n <= cap:
        return n
    for t in range(cap - cap % align, 0, -align):
        if n % t == 0:
            return t
    return n


N_CHIPS = 4
MM_VMEM_BUDGET_BYTES = 40 * 1024 * 1024
Q = "q"


def _opnd(x):
    return x if isinstance(x, tuple) else (x, ())


def _mm(a, b, *, name, ta=False, tb=False, out_dtype=F32, res=None, scale=1.0, out_q=False, into=None, norm_bwd=None,
        tm=None, tn=1024, tk=None):
    (a_arr, a_lead), (b_arr, b_lead) = _opnd(a), _opnd(b)
    (k_a, m) = a_arr.shape[-2:] if ta else a_arr.shape[-2:][::-1]
    (n, k_b) = b_arr.shape[-2:] if tb else b_arr.shape[-2:][::-1]
    if into is not None:
        out_arr, out_lead = into
        out_q, out_dtype = Q in out_lead, out_arr.dtype
    else:
        out_lead = (Q,) if out_q else ()
    red_q = (Q in a_lead or Q in b_lead) and not out_q
    kq = min(k_a, k_b)
    assert (k_a == k_b) or (red_q and max(k_a, k_b) == N_CHIPS * kq), (a_arr.shape, b_arr.shape)
    tn, tk = _tile(n, tn), _tile(kq, tk or (FFN_DW_TK if ta else 1024))
    if tm is None:
        r_item = _opnd(res)[0].dtype.itemsize if res is not None else 0
        per_row = 2 * (tk * a_arr.dtype.itemsize + tn * (jnp.dtype(out_dtype).itemsize + r_item)) + 4 * tn
        if norm_bwd is not None:
            per_row += (2 * 2 + 4) * 4 * tn
        room = MM_VMEM_BUDGET_BYTES - 2 * tk * tn * b_arr.dtype.itemsize
        tm = next(c for c in (4096, 2048, 1024, 512, 256, 128) if c * per_row <= room or c == 128)
    tm = _tile(m, tm)
    nk = kq // tk
    ksteps = nk * (N_CHIPS if red_q else 1)
    dims = (((0 if ta else 1,), (1 if tb else 0,)), ((), ()))
    has_res = res is not None
    n_out = 2 if norm_bwd is not None else 1

    def body(*refs):
        a_ref, b_ref = refs[0], refs[1]
        o_ref, acc_ref = refs[-1 - n_out], refs[-1]
        k = pl.program_id(3)
        first_rows = pl.program_id(1) == 0

        @pl.when(k == 0)
        def _():
            acc_ref[...] = jnp.zeros_like(acc_ref)

        acc_ref[...] += lax.dot_general(a_ref[...].astype(BF16), b_ref[...].astype(BF16), dims,
                                        preferred_element_type=F32)

        @pl.when(k == ksteps - 1)
        def _():
            r = acc_ref[...]
            if scale != 1.0:
                r = r * scale
            if has_res:
                r = r + refs[2][...].astype(F32)
            if norm_bwd is not None:
                h_ref, gain_ref, dres_ref = refs[2 + has_res:5 + has_res]
                dgain_ref = refs[-2]
                _, vjp = jax.vjp(_f_rms_res, h_ref[...], gain_ref[...])
                r, dgain = vjp((r, dres_ref[...]))

                @pl.when(first_rows)
                def _():
                    dgain_ref[...] = jnp.zeros_like(dgain_ref)

                dgain_ref[...] += dgain
            o_ref[...] = r.astype(o_ref.dtype)

    def spec(lead, blk, pos):
        def index(s, i, j, k):
            kk = k % nk if (red_q and Q in lead) else k
            quarter = s if out_q else k // nk
            return tuple(quarter if l == Q else l for l in lead) + pos(i, j, kk)
        return pl.BlockSpec((None,) * len(lead) + blk, index)

    a_spec = spec(a_lead, (tk, tm), lambda i, j, k: (k, i)) if ta else spec(a_lead, (tm, tk), lambda i, j, k: (i, k))
    b_spec = spec(b_lead, (tn, tk), lambda i, j, k: (j, k)) if tb else spec(b_lead, (tk, tn), lambda i, j, k: (k, j))
    o_spec = spec(out_lead, (tm, tn), lambda i, j, k: (i, j))
    in_specs, args = [a_spec, b_spec], [a_arr, b_arr]
    if has_res:
        r_arr, r_lead = _opnd(res)
        in_specs.append(spec(r_lead, (tm, tn), lambda i, j, k: (i, j)))
        args.append(r_arr)
    out_specs, out_shapes = [o_spec], []
    if norm_bwd is not None:
        assert tn == n and not out_q and into is None, "the norm's backward needs whole rows"
        h_arr, gain_arr, dres_arr = norm_bwd
        row_spec = spec((), (tm, tn), lambda i, j, k: (i, j))
        gain_spec = pl.BlockSpec((1, tn), lambda s, i, j, k: (0, 0))
        in_specs += [row_spec, gain_spec, row_spec]
        args += [h_arr, gain_arr, dres_arr]
        out_specs.append(gain_spec)
    aliases = {}
    if into is not None:
        in_specs.append(pl.BlockSpec(memory_space=pl.ANY))
        args.append(out_arr)
        aliases = {len(args) - 1: 0}
        out_shapes.append(jax.ShapeDtypeStruct(out_arr.shape, out_arr.dtype))
    else:
        out_shapes.append(jax.ShapeDtypeStruct(((N_CHIPS,) if out_q else ()) + (m, n), out_dtype))
    if norm_bwd is not None:
        out_shapes.append(jax.ShapeDtypeStruct((1, n), F32))
    out = pl.pallas_call(
        body, name=name, grid=(N_CHIPS if out_q else 1, m // tm, n // tn, ksteps), in_specs=in_specs,
        out_specs=out_specs, out_shape=out_shapes, scratch_shapes=[pltpu.VMEM((tm, tn), F32)],
        input_output_aliases=aliases,
        compiler_params=_params("parallel", *(("arbitrary",) * 3 if norm_bwd is not None else ("parallel", "parallel", "arbitrary"))),
    )(*args)
    return out if norm_bwd is not None else out[0]


def _row_spec(r, tm):
    if isinstance(r, tuple):
        arr, width, cb = r
        return arr, pl.BlockSpec((tm, width), lambda i, cb=cb: (i, cb))
    return r, pl.BlockSpec((tm, r.shape[1]), lambda i: (i, 0))


def _const_spec(c):
    return pl.BlockSpec(c.shape, lambda i: (0,) * c.ndim)


def _row_fwd(fn, rows, consts, outs, *, name, tm=256):
    tm = _tile(_row_spec(rows[0], tm)[0].shape[0], tm, 8)
    arrs, specs = zip(*[_row_spec(r, tm) for r in rows])
    t = arrs[0].shape[0]
    nr, nc = len(rows), len(consts)

    def body(*refs):
        vals = [r[...].astype(F32) for r in refs[:nr + nc]]
        res = fn(*vals)
        for o_ref, v in zip(refs[nr + nc:], res):
            o_ref[...] = v.astype(o_ref.dtype)

    out = pl.pallas_call(
        body, name=name, grid=(t // tm,),
        in_specs=list(specs) + [_const_spec(c) for c in consts],
        out_specs=[pl.BlockSpec((tm, w), lambda i: (i, 0)) for w, _ in outs],
        out_shape=[jax.ShapeDtypeStruct((t, w), dt) for w, dt in outs],
        compiler_params=_params("parallel"),
    )(*arrs, *consts)
    return list(out)


def _row_bwd(fn, rows, consts, cts, row_grads, const_grads, *, name, tm=256):
    tm = _tile(_row_spec(rows[0], tm)[0].shape[0], tm, 8)
    arrs, specs = zip(*[_row_spec(r, tm) for r in rows])
    ct_arrs, ct_specs = zip(*[_row_spec(r, tm) for r in cts])
    t = arrs[0].shape[0]
    nr, nc, nt = len(rows), len(consts), len(cts)
    n_in = nr + nc + nt

    def body(*refs):
        vals = [r[...].astype(F32) for r in refs[:nr + nc]]
        ctv = tuple(r[...].astype(F32) for r in refs[nr + nc:n_in])
        _, vjp = jax.vjp(fn, *vals)
        g = vjp(ctv)
        outs = refs[n_in:]
        for (idx, _), o_ref in zip(row_grads, outs[:len(row_grads)]):
            o_ref[...] = g[idx].astype(o_ref.dtype)
        first = pl.program_id(0) == 0
        for ci, o_ref in zip(const_grads, outs[len(row_grads):]):
            @pl.when(first)
            def _(o_ref=o_ref):
                o_ref[...] = jnp.zeros_like(o_ref)

            o_ref[...] += g[nr + ci]

    widths = [(_row_spec(rows[idx], tm)[1].block_shape[1], dt) for idx, dt in row_grads]
    out = pl.pallas_call(
        body, name=name, grid=(t // tm,),
        in_specs=list(specs) + [_const_spec(c) for c in consts] + list(ct_specs),
        out_specs=[pl.BlockSpec((tm, w), lambda i: (i, 0)) for w, _ in widths]
        + [_const_spec(consts[ci]) for ci in const_grads],
        out_shape=[jax.ShapeDtypeStruct((t, w), dt) for w, dt in widths]
        + [jax.ShapeDtypeStruct(consts[ci].shape, F32) for ci in const_grads],
        compiler_params=_params("arbitrary"),
    )(*arrs, *consts, *ct_arrs)
    return list(out)


def _rms(x, g):
    return x * lax.rsqrt(jnp.mean(x * x, axis=-1, keepdims=True) + EPS) * g


def _f_rms(h, g):
    return (_rms(h, g),)


def _f_rms_res(h, g):
    return (_rms(h, g), h)


def _f_swiglu(g, u):
    return (g * jax.nn.sigmoid(g) * u,)


def _f_ple(h, gl, pp):
    return (h + jax.nn.sigmoid(gl) * pp,)


def _f_gdn_out(o, z, gain):
    outs = []
    for hd in range(GDN_V_HEADS):
        sl = slice(hd * GDN_HEAD_DIM, (hd + 1) * GDN_HEAD_DIM)
        oh, zh = o[:, sl], z[:, sl]
        outs.append(_rms(oh, gain) * (zh * jax.nn.sigmoid(zh)))
    return (jnp.concatenate(outs, axis=1),)


def _loss_head(y, target, *, name, tm=512):
    t, d = y.shape
    tm = _tile(t, tm, 8)

    def body(y_ref, t_ref, dy_ref, l_ref):
        @pl.when(pl.program_id(0) == 0)
        def _():
            l_ref[...] = jnp.zeros_like(l_ref)

        e = y_ref[...] - t_ref[...]
        dy_ref[...] = e * (1.0 / d)
        l_ref[...] += jnp.sum(e * e) * (0.5 / d)

    dy, l = pl.pallas_call(
        body, name=name, grid=(t // tm,),
        in_specs=[pl.BlockSpec((tm, d), lambda i: (i, 0))] * 2,
        out_specs=[pl.BlockSpec((tm, d), lambda i: (i, 0)), pl.BlockSpec((8, LANES), lambda i: (0, 0))],
        out_shape=[jax.ShapeDtypeStruct((t, d), F32), jax.ShapeDtypeStruct((8, LANES), F32)],
        compiler_params=_params("arbitrary"),
    )(y, target)
    return dy, l[0, 0]


def _dg(a, b, ca, cb):
    nb = a.ndim - 2
    batch = tuple(range(nb))
    return lax.dot_general(a, b, (((ca + nb,), (cb + nb,)), (batch, batch)), preferred_element_type=F32)


def _b(x):
    return x.astype(BF16)


@jax.custom_vjp
def _bdot(a, b):
    return _dg(_b(a), _b(b), 1, 0)


def _bdot_fwd(a, b):
    return _bdot(a, b), (a, b)


def _bdot_bwd(r, ct):
    a, b = r
    return _dg(_b(ct), _b(b), 1, 1), _dg(_b(a), _b(ct), 0, 0)


_bdot.defvjp(_bdot_fwd, _bdot_bwd)


@jax.custom_vjp
def _bdot_nt(a, b):
    return _dg(_b(a), _b(b), 1, 1)


def _bdot_nt_fwd(a, b):
    return _bdot_nt(a, b), (a, b)


def _bdot_nt_bwd(r, ct):
    a, b = r
    return _dg(_b(ct), _b(b), 1, 0), _dg(_b(ct), _b(a), 0, 0)


_bdot_nt.defvjp(_bdot_nt_fwd, _bdot_nt_bwd)


@jax.custom_vjp
def _bdot_tn(a, b):
    return _dg(_b(a), _b(b), 0, 0)


def _bdot_tn_fwd(a, b):
    return _bdot_tn(a, b), (a, b)


def _bdot_tn_bwd(r, ct):
    a, b = r
    return _dg(_b(b), _b(ct), 1, 1), _dg(_b(a), _b(ct), 1, 0)


_bdot_tn.defvjp(_bdot_tn_fwd, _bdot_tn_bwd)


def _two(x):
    hi = x.astype(BF16)
    return hi, (x - hi.astype(F32)).astype(BF16)


def _dg3(a, b, ca, cb):
    (ah, al), (bh, bl) = _two(a), _two(b)
    return _dg(ah, bh, ca, cb) + (_dg(ah, bl, ca, cb) + _dg(al, bh, ca, cb))


@jax.custom_vjp
def _hdot(a, b):
    return _dg3(a, b, 1, 0)


def _hdot_fwd(a, b):
    return _hdot(a, b), (a, b)


def _hdot_bwd(r, ct):
    a, b = r
    return _dg3(ct, b, 1, 1), _dg3(a, ct, 0, 0)


_hdot.defvjp(_hdot_fwd, _hdot_bwd)


@jax.custom_vjp
def _unit_lower_inverse(x):
    c = x.shape[-1]
    eye = (lax.broadcasted_iota(jnp.int32, x.shape, 1) == lax.broadcasted_iota(jnp.int32, x.shape, 2)).astype(F32)
    inv, pw = eye + x, x
    for _ in range(int(math.log2(c)) - 1):
        pw = _dg3(pw, pw, 1, 0)
        inv = inv + _dg3(inv, pw, 1, 0)
    return inv


def _unit_lower_inverse_fwd(x):
    inv = _unit_lower_inverse(x)
    return inv, inv


def _unit_lower_inverse_bwd(inv, ct):
    return (_dg3(_dg3(inv, ct, 0, 0), inv, 1, 1),)


_unit_lower_inverse.defvjp(_unit_lower_inverse_fwd, _unit_lower_inverse_bwd)


@jax.custom_vjp
def _known_inverse(x, inv):
    return inv


def _known_inverse_fwd(x, inv):
    return inv, inv


def _known_inverse_bwd(inv, ct):
    return _dg3(_dg3(inv, ct, 0, 0), inv, 1, 1), jnp.zeros_like(inv)


_known_inverse.defvjp(_known_inverse_fwd, _known_inverse_bwd)


def _split_dot(x, u):
    hi, lo = _two(x)
    return _dg(hi, u, 1, 0) + _dg(lo, u, 1, 0)


@jax.custom_vjp
def _ldot(l01, x):
    hi, lo = _two(x)
    l01 = l01.astype(BF16)
    return _dg(l01, hi, 1, 0) + _dg(l01, lo, 1, 0)


def _ldot_fwd(l01, x):
    return _ldot(l01, x), l01


def _ldot_bwd(l01, ct):
    hi, lo = _two(ct)
    l01b = l01.astype(BF16)
    return jnp.zeros_like(l01), _dg(l01b, hi, 0, 0) + _dg(l01b, lo, 0, 0)


_ldot.defvjp(_ldot_fwd, _ldot_bwd)


SB_BLK = 128
SB_BWD_BLK = 256
SB_KEYS = 512
SB_PAIRS = 2
SB_SCALE = HEAD_DIM ** -0.5


def _log_sigmoid(z):
    return jnp.minimum(z, 0.0) - jnp.log(1.0 + jnp.exp(-jnp.abs(z)))


def _sb_consts(t, blk):
    kb = min(SB_KEYS, t)
    nh = 2 * SB_PAIRS
    lane = lax.broadcasted_iota(jnp.int32, (nh, blk, kb), 2)
    row = lax.broadcasted_iota(jnp.int32, (nh, blk, kb), 1)
    ur = lax.broadcasted_iota(jnp.int32, (kb, kb), 0)
    uc = lax.broadcasted_iota(jnp.int32, (kb, kb), 1)
    return kb, nh, lane, row, ur, uc


def _sb_heads(x):
    head0 = lax.broadcasted_iota(jnp.int32, (x.shape[0], LANES), 1) < HEAD_DIM
    out = []
    for p in range(SB_PAIRS):
        blk = x[:, p * LANES:(p + 1) * LANES]
        out += [jnp.where(head0, blk, 0.0), jnp.where(head0, 0.0, blk)]
    return jnp.stack(out)


def _sb_pairs(x):
    return jnp.stack([x[:, (h // 2) * LANES:(h // 2 + 1) * LANES] for h in range(2 * SB_PAIRS)])


def _sb_merge(x):
    head0 = lax.broadcasted_iota(jnp.int32, (x.shape[1], LANES), 1) < HEAD_DIM
    return jnp.concatenate([jnp.where(head0, x[2 * p], x[2 * p + 1]) for p in range(SB_PAIRS)], axis=1)


def _sb_rows_dot(x, u):
    nh, rows, k = x.shape
    return _split_dot(x.reshape(nh * rows, k), u).reshape(nh, rows, k)


def _sb_fwd(proj, *, name, gather=()):
    t = proj.shape[0]
    nb = t // SB_BLK
    width = SB_PAIRS * LANES
    ng = SB_W // width
    na = len(gather)

    def body(q_ref, k_ref, v_ref, *rest):
        o_ref, r_ref = rest[na:na + 2]
        i = pl.program_id(1)
        if na:
            step = pl.program_id(0) * nb + i
            copies = lambda **kw: _gather_copies(rest[:na], rest[na + 2:2 * na + 2], *rest[2 * na + 2:], **kw)
            pl.when(step == 0)(lambda: _gather_start(copies(only_first=True)))
        kb, nh, lane, row, ur, uc = _sb_consts(t, SB_BLK)
        u_suffix = (ur >= uc).astype(BF16)
        qh = _b(_sb_heads(q_ref[...]) * SB_SCALE)
        diag = (i * SB_BLK) // kb

        def block(j, carry, masked):
            acc, car = carry
            keys = pl.ds(pl.multiple_of(j * kb, kb), kb)
            kj, vj = _b(_sb_pairs(k_ref[keys, :])), _b(_sb_pairs(v_ref[keys, :]))
            z = _dg(qh, kj, 1, 1)
            lk = _log_sigmoid(-z)
            if masked:
                causal = (j * kb + lane) < (i * SB_BLK + row)
                lk = jnp.where(causal, lk, 0.0)
            suf = _sb_rows_dot(lk, u_suffix) + car
            w = jnp.exp(z + suf)
            if masked:
                w = jnp.where(causal, w, 0.0)
            return acc + _dg(_b(w), vj, 1, 0), suf[:, :, 0:1]

        zero = (jnp.zeros((nh, SB_BLK, LANES), F32), jnp.zeros((nh, SB_BLK, 1), F32))
        carry = block(diag, zero, True)
        acc, car = lax.fori_loop(0, diag, lambda s, c: block(diag - 1 - s, c, False), carry)
        o_ref[...] = _sb_merge(acc)
        r_ref[...] = _sb_merge(jnp.broadcast_to(car, (nh, SB_BLK, LANES)))
        if na:
            pl.when(step == ng * nb - 1)(lambda: _gather_finish(copies()))

    return pl.pallas_call(
        body, name=name, grid=(ng, nb),
        in_specs=[pl.BlockSpec((SB_BLK, width), lambda p, i: (i, p)),
                  pl.BlockSpec((t, width), lambda p, i: (0, ng + p)),
                  pl.BlockSpec((t, width), lambda p, i: (0, 2 * ng + p))] + [ANY] * na,
        out_specs=[pl.BlockSpec((SB_BLK, width), lambda p, i: (i, p))] * 2 + [ANY] * na,
        out_shape=[jax.ShapeDtypeStruct((t, SB_W), F32)] * 2
        + [jax.ShapeDtypeStruct((N_CHIPS,) + g.shape, g.dtype) for g in gather],
        scratch_shapes=_gather_scratch(na) if na else [],
        compiler_params=_params("arbitrary", "arbitrary"),
    )(proj, proj, proj, *gather)


def _sb_bwd(proj, rtot, dout, *, name, scatter=()):
    t = proj.shape[0]
    nb = t // SB_BWD_BLK
    width = SB_PAIRS * LANES
    ng = SB_W // width
    na = len(scatter)

    def body(q_ref, k_ref, v_ref, r_ref, do_ref, *rest):
        dq_ref, dk_ref, dv_ref = rest[na:na + 3]
        i = pl.program_id(1)
        if na:
            step = pl.program_id(0) * nb + i
            copies = lambda: _scatter_copies(rest[:na], rest[na + 3:4 * na + 3], *rest[4 * na + 3:])
            pl.when(step == 0)(lambda: [cp.start() for cp in copies()] and None)
        kb, nh, lane, row, ur, uc = _sb_consts(t, SB_BWD_BLK)
        u_incl = (ur <= uc).astype(BF16)
        u_excl = (ur < uc).astype(BF16)
        q, do = q_ref[...], do_ref[...]
        qh, doh = _b(_sb_heads(q) * SB_SCALE), _b(_sb_heads(do))
        qb, dob = _b(_sb_pairs(q) * SB_SCALE), _b(_sb_pairs(do))
        rh = jnp.min(_sb_heads(r_ref[...]), axis=2, keepdims=True)
        diag = (i * SB_BWD_BLK) // kb

        @pl.when(i == 0)
        def _():
            dk_ref[...] = jnp.zeros_like(dk_ref)
            dv_ref[...] = jnp.zeros_like(dv_ref)

        def block(j, carry, masked):
            dq_acc, clk, ce = carry
            keys = pl.ds(pl.multiple_of(j * kb, kb), kb)
            kj, vj = _b(_sb_pairs(k_ref[keys, :])), _b(_sb_pairs(v_ref[keys, :]))
            z = _dg(qh, kj, 1, 1)
            lk = _log_sigmoid(-z)
            ls = z + lk
            if masked:
                causal = (j * kb + lane) < (i * SB_BWD_BLK + row)
                lk = jnp.where(causal, lk, 0.0)
            pre = _sb_rows_dot(lk, u_incl) + clk
            w = jnp.exp(ls + (rh - pre))
            if masked:
                w = jnp.where(causal, w, 0.0)
            e = _dg(doh, vj, 1, 1) * w
            pre_e = _sb_rows_dot(e, u_excl) + ce
            sig = jnp.exp(ls)
            dz = e - sig * (e + pre_e)
            if masked:
                dz = jnp.where(causal, dz, 0.0)
            dzb = _b(dz)
            dk_ref[keys, :] += _sb_merge(_dg(dzb, qb, 0, 0))
            dv_ref[keys, :] += _sb_merge(_dg(_b(w), dob, 0, 0))
            return dq_acc + _dg(dzb, kj, 1, 0), pre[:, :, kb - 1:], pre_e[:, :, kb - 1:] + e[:, :, kb - 1:]

        zero = (jnp.zeros((nh, SB_BWD_BLK, LANES), F32), jnp.zeros((nh, SB_BWD_BLK, 1), F32), jnp.zeros((nh, SB_BWD_BLK, 1), F32))
        carry = lax.fori_loop(0, diag, lambda j, c: block(j, c, False), zero)
        dq_acc, _, _ = block(diag, carry, True)
        dq_ref[...] = _sb_merge(dq_acc) * SB_SCALE
        if na:
            pl.when(step == ng * nb - 1)(lambda: [cp.wait() for cp in copies()] and None)

    blk = pl.BlockSpec((SB_BWD_BLK, width), lambda p, i: (i, p))
    whole = pl.BlockSpec((t, width), lambda p, i: (0, p))
    return pl.pallas_call(
        body, name=name, grid=(ng, nb),
        in_specs=[blk,
                  pl.BlockSpec((t, width), lambda p, i: (0, ng + p)),
                  pl.BlockSpec((t, width), lambda p, i: (0, 2 * ng + p)),
                  blk, blk] + [ANY] * na,
        out_specs=[blk, whole, whole] + [ANY] * (3 * na),
        out_shape=[jax.ShapeDtypeStruct((t, SB_W), F32)] * 3 + _scatter_shapes(scatter),
        scratch_shapes=_dma_sems(3 * na) if na else [],
        compiler_params=_params("arbitrary", "arbitrary"),
    )(proj, proj, proj, rtot, dout, *scatter)


SWA_G = SWA_HEADS // SWA_KV_HEADS


def _swa_heads(first, qs, ks, vs, qg, kg, sinks):
    shape = (SWA_HEADS, WINDOW, 2 * WINDOW)
    qi = lax.broadcasted_iota(jnp.int32, shape, 1)
    kj = lax.broadcasted_iota(jnp.int32, shape, 2)
    dist = qi + WINDOW - kj
    valid = (dist >= 0) & (dist < WINDOW) & (jnp.logical_not(first) | (kj >= WINDOW))
    head = lax.broadcasted_iota(jnp.int32, (SWA_HEADS, 1, 1), 0)
    slope = sum(jnp.where(head == h, 2.0 ** (-8.0 * (h + 1) / SWA_HEADS), 0.0) for h in range(SWA_HEADS))
    kn = _rms(ks, kg)
    per_q_head = lambda x: jnp.concatenate([x[h // SWA_G:h // SWA_G + 1] for h in range(SWA_HEADS)], axis=0)
    k8, v8 = per_q_head(kn), per_q_head(vs)
    s = _bdot_nt(_rms(qs, qg), k8) * (HEAD_DIM ** -0.5)
    s = jnp.where(valid, s - slope * dist.astype(F32), NEG)
    m = lax.stop_gradient(jnp.maximum(jnp.max(s, axis=2, keepdims=True), sinks))
    p = jnp.exp(s - m)
    den = jnp.sum(p, axis=2, keepdims=True) + jnp.exp(sinks - m)
    return _bdot(p / den, v8)


def _swa_split(q, kp, kc, vp, vc, sk):
    lanes = lambda x, n: jnp.stack([x[:, h * HEAD_DIM:(h + 1) * HEAD_DIM].astype(F32) for h in range(n)])
    k2, v2 = jnp.concatenate([kp, kc], axis=0), jnp.concatenate([vp, vc], axis=0)
    sinks = jnp.stack([sk[:, h:h + 1] for h in range(SWA_HEADS)])
    return lanes(q, SWA_HEADS), lanes(k2, SWA_KV_HEADS), lanes(v2, SWA_KV_HEADS), sinks


def _swa_join(x):
    return jnp.concatenate([x[h] for h in range(x.shape[0])], axis=1)


def _swa_specs(t):
    qcb = (3 * SB_W) // SWA_QW
    kcb = (3 * SB_W + SWA_QW) // SWA_KVW
    prev = lambda i: jnp.maximum(i - 1, 0)
    return [pl.BlockSpec((WINDOW, SWA_QW), lambda i: (i, qcb)),
            pl.BlockSpec((WINDOW, SWA_KVW), lambda i: (prev(i), kcb)),
            pl.BlockSpec((WINDOW, SWA_KVW), lambda i: (i, kcb)),
            pl.BlockSpec((WINDOW, SWA_KVW), lambda i: (prev(i), kcb + 1)),
            pl.BlockSpec((WINDOW, SWA_KVW), lambda i: (i, kcb + 1)),
            pl.BlockSpec((1, HEAD_DIM), lambda i: (0, 0)),
            pl.BlockSpec((1, HEAD_DIM), lambda i: (0, 0)),
            pl.BlockSpec((1, SWA_HEADS), lambda i: (0, 0))]


def _swa_fwd(proj, qg, kg, sinks, *, name):
    t = proj.shape[0]

    def body(q_ref, kp_ref, kc_ref, vp_ref, vc_ref, qg_ref, kg_ref, sk_ref, o_ref):
        first = pl.program_id(0) == 0
        qs, ks, vs, sk = _swa_split(q_ref[...], kp_ref[...], kc_ref[...], vp_ref[...], vc_ref[...], sk_ref[...])
        o_ref[...] = _swa_join(_swa_heads(first, qs, ks, vs, qg_ref[...], kg_ref[...], sk))

    return pl.pallas_call(
        body, name=name, grid=(t // WINDOW,), in_specs=_swa_specs(t),
        out_specs=pl.BlockSpec((WINDOW, SWA_QW), lambda i: (i, 0)),
        out_shape=jax.ShapeDtypeStruct((t, SWA_QW), F32),
        compiler_params=_params("parallel"),
    )(proj, proj, proj, proj, proj, qg, kg, sinks)


def _swa_bwd(proj, qg, kg, sinks, dout, *, name):
    t = proj.shape[0]

    def body(q_ref, kp_ref, kc_ref, vp_ref, vc_ref, qg_ref, kg_ref, sk_ref, do_ref,
             dq_ref, dk_ref, dv_ref, dqg_ref, dkg_ref, dsk_ref):
        i = pl.program_id(0)
        first = i == 0

        @pl.when(first)
        def _():
            for r in (dk_ref, dv_ref, dqg_ref, dkg_ref, dsk_ref):
                r[...] = jnp.zeros_like(r)

        qs, ks, vs, sk = _swa_split(q_ref[...], kp_ref[...], kc_ref[...], vp_ref[...], vc_ref[...], sk_ref[...])
        do = do_ref[...]
        cts = jnp.stack([do[:, h * HEAD_DIM:(h + 1) * HEAD_DIM] for h in range(SWA_HEADS)])
        _, vjp = jax.vjp(functools.partial(_swa_heads, first), qs, ks, vs, qg_ref[...], kg_ref[...], sk)
        dqs, dks, dvs, dqg, dkg, dsk = vjp(cts)
        dq_ref[...] = _swa_join(dqs)
        dk2, dv2 = _swa_join(dks), _swa_join(dvs)
        cur = pl.ds(pl.multiple_of(i * WINDOW, WINDOW), WINDOW)
        prv = pl.ds(pl.multiple_of(jnp.maximum(i - 1, 0) * WINDOW, WINDOW), WINDOW)
        dk_ref[prv, :] += dk2[:WINDOW]
        dv_ref[prv, :] += dv2[:WINDOW]
        dk_ref[cur, :] += dk2[WINDOW:]
        dv_ref[cur, :] += dv2[WINDOW:]
        dqg_ref[...] += dqg
        dkg_ref[...] += dkg
        dsk_ref[...] += _swa_join(dsk)

    whole = lambda shape: pl.BlockSpec(shape, lambda i: (0, 0))
    return pl.pallas_call(
        body, name=name, grid=(t // WINDOW,),
        in_specs=_swa_specs(t) + [pl.BlockSpec((WINDOW, SWA_QW), lambda i: (i, 0))],
        out_specs=[pl.BlockSpec((WINDOW, SWA_QW), lambda i: (i, 0)), whole((t, SWA_KVW)), whole((t, SWA_KVW)),
                   whole((1, HEAD_DIM)), whole((1, HEAD_DIM)), whole((1, SWA_HEADS))],
        out_shape=[jax.ShapeDtypeStruct((t, SWA_QW), F32), jax.ShapeDtypeStruct((t, SWA_KVW), F32),
                   jax.ShapeDtypeStruct((t, SWA_KVW), F32), jax.ShapeDtypeStruct((1, HEAD_DIM), F32),
                   jax.ShapeDtypeStruct((1, HEAD_DIM), F32), jax.ShapeDtypeStruct((1, SWA_HEADS), F32)],
        compiler_params=_params("arbitrary"),
    )(proj, proj, proj, proj, proj, qg, kg, sinks, dout)


CONV_CB = 512
CONV_TM = 512
HALO = 8


def _conv_pre(x_ref, h_ref, w_ref, i):
    halo = jnp.where(i > 0, h_ref[...], 0.0)
    xe = jnp.concatenate([halo, x_ref[...]], axis=0)
    tm = x_ref.shape[0]
    w = w_ref[...]
    c = sum(w[k:k + 1, :] * xe[HALO - (GDN_CONV - 1) + k:HALO - (GDN_CONV - 1) + k + tm] for k in range(GDN_CONV))
    return c, xe


def _conv_specs(tm, cb):
    return [pl.BlockSpec((tm, cb), lambda c, i: (i, c)),
            pl.BlockSpec((HALO, cb), lambda c, i: (jnp.maximum(i * (tm // HALO) - 1, 0), c)),
            pl.BlockSpec((GDN_CONV, cb), lambda c, i: (0, c))]


def _conv_fwd(x, w, *, name):
    t, ch = x.shape
    tm, cb = _tile(t, CONV_TM), _tile(ch, CONV_CB)

    def body(x_ref, h_ref, w_ref, o_ref):
        c, _ = _conv_pre(x_ref, h_ref, w_ref, pl.program_id(1))
        o_ref[...] = c * jax.nn.sigmoid(c)

    tile = pl.BlockSpec((tm, cb), lambda c, i: (i, c))
    return pl.pallas_call(
        body, name=name, grid=(ch // cb, t // tm), in_specs=_conv_specs(tm, cb), out_specs=tile,
        out_shape=jax.ShapeDtypeStruct((t, ch), F32),
        compiler_params=_params("parallel", "parallel"),
    )(x, x, w)


def _conv_bwd(x, w, dact, *, name):
    t, ch = x.shape
    tm, cb = _tile(t, CONV_TM), _tile(ch, CONV_CB)
    nt = t // tm
    last = GDN_CONV - 1

    def body(x_ref, h_ref, w_ref, xn_ref, da_ref, dan_ref, dx_ref, dw_ref):
        i = pl.program_id(1)

        @pl.when(i == 0)
        def _():
            dw_ref[...] = jnp.zeros_like(dw_ref)

        w = w_ref[...]
        xe = jnp.concatenate([jnp.where(i > 0, h_ref[...], 0.0), x_ref[...], xn_ref[...]], axis=0)
        rows = tm + HALO
        c = sum(w[k:k + 1, :] * xe[HALO - last + k:HALO - last + k + rows] for k in range(GDN_CONV))
        sig = jax.nn.sigmoid(c)
        da = jnp.concatenate([da_ref[...], jnp.where(i < nt - 1, dan_ref[...], 0.0)], axis=0)
        dce = da * (sig * (1.0 + c * (1.0 - sig)))
        dc = dce[:tm]
        dx_ref[...] = sum(w[k:k + 1, :] * dce[last - k:last - k + tm] for k in range(GDN_CONV))
        dw_ref[...] += jnp.concatenate(
            [jnp.sum(dc * xe[HALO - last + k:HALO - last + k + tm], axis=0, keepdims=True) for k in range(GDN_CONV)],
            axis=0)

    tile = pl.BlockSpec((tm, cb), lambda c, i: (i, c))
    nxt = pl.BlockSpec((HALO, cb), lambda c, i: (jnp.minimum((i + 1) * (tm // HALO), t // HALO - 1), c))
    return pl.pallas_call(
        body, name=name, grid=(ch // cb, nt),
        in_specs=_conv_specs(tm, cb) + [nxt, tile, nxt],
        out_specs=[tile, pl.BlockSpec((GDN_CONV, cb), lambda c, i: (0, c))],
        out_shape=[jax.ShapeDtypeStruct((t, ch), F32), jax.ShapeDtypeStruct((GDN_CONV, ch), F32)],
        compiler_params=_params("parallel", "arbitrary"),
    )(x, x, w, x, dact, dact)


def _gdn_chunk(qraw, kraw, v, bl, a, alog, dtb, state, inverse=None, keep_inverse=False):
    c, d = GDN_CHUNK, GDN_HEAD_DIM
    nh = qraw.shape[0]
    ri = lax.broadcasted_iota(jnp.int32, (nh, c, c), 1)
    ci = lax.broadcasted_iota(jnp.int32, (nh, c, c), 2)
    incl, strict = ri >= ci, ri > ci
    q = qraw * lax.rsqrt(jnp.sum(qraw * qraw, axis=-1, keepdims=True) + EPS) * (d ** -0.5)
    k = kraw * lax.rsqrt(jnp.sum(kraw * kraw, axis=-1, keepdims=True) + EPS)
    beta = jax.nn.sigmoid(bl)
    g = -jnp.exp(alog) * jax.nn.softplus(a + dtb)
    gc = _ldot(incl.astype(F32), jnp.broadcast_to(g, (nh, c, d)))
    gcm = gc[:, :, :c]
    decay = jnp.exp(jnp.where(incl, gcm - jnp.swapaxes(gcm, 1, 2), NEG))
    eg = jnp.exp(gc)
    kbeta = k * beta
    x = -jnp.where(strict, _bdot_nt(kbeta, k) * decay, 0.0)
    tinv = _unit_lower_inverse(x) if inverse is None else _known_inverse(x, inverse)
    u = _hdot(tinv, v * beta)
    w = _hdot(tinv, kbeta * eg)
    attn = jnp.where(incl, _bdot_nt(q, k) * decay, 0.0)
    glast = gc[:, c - 1:c, :]
    v_new = u - _bdot(w, state)
    o = _bdot(q * eg, state) + _bdot(attn, v_new)
    state = state * jnp.exp(glast) + _bdot_tn(k * jnp.exp(glast - gc), v_new)
    return (o, state, tinv) if keep_inverse else (o, state)


GDN_REP = GDN_V_HEADS // GDN_K_HEADS
GDN_HB = 8


def _gdn_pick(vals, kh, r):
    ba, alog, dtb = vals
    lane = lax.broadcasted_iota(jnp.int32, ba.shape, 1)
    hv = kh * GDN_REP + r
    bl = jnp.sum(jnp.where(lane == hv, ba, 0.0), axis=1, keepdims=True)
    a = jnp.sum(jnp.where(lane == GDN_V_HEADS + hv, ba, 0.0), axis=1, keepdims=True)
    lane1 = lax.broadcasted_iota(jnp.int32, alog.shape, 1)
    al = jnp.sum(jnp.where(lane1 == hv, alog, 0.0), axis=1, keepdims=True)
    db = jnp.sum(jnp.where(lane1 == hv, dtb, 0.0), axis=1, keepdims=True)
    return bl, a, al, db


def _gdn_stack(qs, ks, vs, small, j):
    d = GDN_HEAD_DIM
    per = [[], [], [], [], [], [], []]
    for hh in range(GDN_HB):
        q, k = qs[:, hh * d:(hh + 1) * d], ks[:, hh * d:(hh + 1) * d]
        for r in range(GDN_REP):
            col = (hh * GDN_REP + r) * d
            for lst, val in zip(per, (q, k, vs[:, col:col + d]) + _gdn_pick(small, j * GDN_HB + hh, r)):
                lst.append(val)
    return tuple(jnp.stack(lst) for lst in per)


def _gdn_specs(nchunk, rev):
    c, d = GDN_CHUNK, GDN_HEAD_DIM
    at = (lambda n: nchunk - 1 - n) if rev else (lambda n: n)
    ng = GDN_K_HEADS // GDN_HB
    return at, [pl.BlockSpec((c, GDN_HB * d), lambda n, j: (at(n), j)),
                pl.BlockSpec((c, GDN_HB * d), lambda n, j: (at(n), ng + j)),
                pl.BlockSpec((c, GDN_HB * GDN_REP * d), lambda n, j: (at(n), ng + j)),
                pl.BlockSpec((c, 2 * GDN_V_HEADS), lambda n, j: (at(n), 0)),
                pl.BlockSpec((1, GDN_V_HEADS), lambda n, j: (0, 0)),
                pl.BlockSpec((1, GDN_V_HEADS), lambda n, j: (0, 0))]


def _gdn_fwd(act, ba, alog, dtb, *, name):
    t = act.shape[0]
    c, d = GDN_CHUNK, GDN_HEAD_DIM
    nchunk = t // c
    at, specs = _gdn_specs(nchunk, False)

    def body(q_ref, k_ref, v_ref, ba_ref, al_ref, db_ref, o_ref, s_ref, inv_ref, state):
        n, j = pl.program_id(0), pl.program_id(1)
        heads = pl.ds(j * GDN_HB, GDN_HB)

        @pl.when(n == 0)
        def _():
            state[heads] = jnp.zeros((GDN_HB, GDN_REP, d, d), F32)

        s_in = state[heads]
        s_ref[...] = s_in
        args = _gdn_stack(q_ref[...], k_ref[...], v_ref[...], (ba_ref[...], al_ref[...], db_ref[...]), j)
        o, s_new, inv_ref[...] = _gdn_chunk(*args, s_in.reshape(GDN_HB * GDN_REP, d, d), keep_inverse=True)
        o_ref[...] = jnp.concatenate([o[b] for b in range(GDN_HB * GDN_REP)], axis=1)
        state[heads] = s_new.reshape(GDN_HB, GDN_REP, d, d)

    return pl.pallas_call(
        body, name=name, grid=(nchunk, GDN_K_HEADS // GDN_HB), in_specs=specs,
        out_specs=[pl.BlockSpec((c, GDN_HB * GDN_REP * d), lambda n, j: (n, j)),
                   pl.BlockSpec((None, GDN_HB, GDN_REP, d, d), lambda n, j: (n, j, 0, 0, 0)),
                   pl.BlockSpec((None, GDN_HB * GDN_REP, c, c), lambda n, j: (n, j, 0, 0))],
        out_shape=[jax.ShapeDtypeStruct((t, GDN_VW), F32),
                   jax.ShapeDtypeStruct((nchunk, GDN_K_HEADS, GDN_REP, d, d), F32),
                   jax.ShapeDtypeStruct((nchunk, GDN_V_HEADS, c, c), F32)],
        scratch_shapes=[pltpu.VMEM((GDN_K_HEADS, GDN_REP, d, d), F32)],
        compiler_params=_params("arbitrary", "arbitrary"),
    )(act, act, act, ba, alog, dtb)


def _gdn_bwd(act, ba, alog, dtb, states, inverses, dout, *, name):
    assert GDN_HB == GDN_K_HEADS
    t = act.shape[0]
    c, d = GDN_CHUNK, GDN_HEAD_DIM
    nchunk = t // c
    at, specs = _gdn_specs(nchunk, True)

    def body(q_ref, k_ref, v_ref, ba_ref, al_ref, db_ref, s_ref, inv_ref, do_ref,
             dact_ref, dba_ref, dal_ref, ddb_ref, dstate):
        n, j = pl.program_id(0), pl.program_id(1)

        @pl.when(n == 0)
        def _():
            dstate[pl.ds(j * GDN_HB, GDN_HB)] = jnp.zeros((GDN_HB, GDN_REP, d, d), F32)

        @pl.when((n == 0) & (j == 0))
        def _():
            dal_ref[...] = jnp.zeros_like(dal_ref)
            ddb_ref[...] = jnp.zeros_like(ddb_ref)

        @pl.when(j == 0)
        def _():
            dba_ref[...] = jnp.zeros_like(dba_ref)

        heads = pl.ds(j * GDN_HB, GDN_HB)
        nh = GDN_HB * GDN_REP
        args = _gdn_stack(q_ref[...], k_ref[...], v_ref[...], (ba_ref[...], al_ref[...], db_ref[...]), j)
        _, vjp = jax.vjp(functools.partial(_gdn_chunk, inverse=inv_ref[...]), *args, s_ref[...].reshape(nh, d, d))
        do = do_ref[...]
        do = jnp.stack([do[:, b * d:(b + 1) * d] for b in range(nh)])
        gq, gk, gv, gbl, ga, gal, gdb, gs = vjp((do, dstate[heads].reshape(nh, d, d)))
        dstate[heads] = gs.reshape(GDN_HB, GDN_REP, d, d)
        dact_ref[...] = jnp.concatenate([gq[GDN_REP * hh] + gq[GDN_REP * hh + 1] for hh in range(GDN_HB)]
                                        + [gk[GDN_REP * hh] + gk[GDN_REP * hh + 1] for hh in range(GDN_HB)]
                                        + [gv[b] for b in range(nh)], axis=1)
        lane = lax.broadcasted_iota(jnp.int32, (c, 2 * GDN_V_HEADS), 1)
        lane1 = lax.broadcasted_iota(jnp.int32, (1, GDN_V_HEADS), 1)
        dba = jnp.zeros((c, 2 * GDN_V_HEADS), F32)
        dal = jnp.zeros((1, GDN_V_HEADS), F32)
        ddb = jnp.zeros((1, GDN_V_HEADS), F32)
        for b in range(nh):
            hv = j * nh + b
            dba = dba + jnp.where(lane == hv, gbl[b], 0.0) + jnp.where(lane == GDN_V_HEADS + hv, ga[b], 0.0)
            dal = dal + jnp.where(lane1 == hv, gal[b], 0.0)
            ddb = ddb + jnp.where(lane1 == hv, gdb[b], 0.0)
        dba_ref[...] += dba
        dal_ref[...] += dal
        ddb_ref[...] += ddb

    small = pl.BlockSpec((1, GDN_V_HEADS), lambda n, j: (0, 0))
    return pl.pallas_call(
        body, name=name, grid=(nchunk, GDN_K_HEADS // GDN_HB),
        in_specs=specs + [pl.BlockSpec((None, GDN_HB, GDN_REP, d, d), lambda n, j: (at(n), j, 0, 0, 0)),
                          pl.BlockSpec((None, GDN_HB * GDN_REP, c, c), lambda n, j: (at(n), j, 0, 0)),
                          pl.BlockSpec((c, GDN_HB * GDN_REP * d), lambda n, j: (at(n), j))],
        out_specs=[pl.BlockSpec((c, GDN_CONV_W), lambda n, j: (at(n), 0)),
                   pl.BlockSpec((c, 2 * GDN_V_HEADS), lambda n, j: (at(n), 0)),
                   small, small],
        out_shape=[jax.ShapeDtypeStruct((t, GDN_CONV_W), F32), jax.ShapeDtypeStruct((t, 2 * GDN_V_HEADS), F32),
                   jax.ShapeDtypeStruct((1, GDN_V_HEADS), F32), jax.ShapeDtypeStruct((1, GDN_V_HEADS), F32)],
        scratch_shapes=[pltpu.VMEM((GDN_K_HEADS, GDN_REP, d, d), F32)],
        compiler_params=_params("arbitrary", "arbitrary"),
    )(act, act, act, ba, alog, dtb, states, inverses, dout)


N_DEV = 8
ANY = pl.BlockSpec(memory_space=pl.ANY)


def _coords():
    return lax.axis_index("x"), lax.axis_index("y"), lax.axis_index("c")


def _other_chips(x, y):
    return [(1 - x, y), (x, 1 - y), (1 - x, 1 - y)]


def _remote(src, dst, send_sems, recv_sems, k, to):
    return pltpu.make_async_remote_copy(src_ref=src, dst_ref=dst, send_sem=send_sems.at[k], recv_sem=recv_sems.at[k],
                                        device_id=to, device_id_type=MESH)


def _dma_sems(n):
    return [pltpu.SemaphoreType.DMA((n,)), pltpu.SemaphoreType.DMA((n,))]


def _gather_copies(ins, outs, send_sems, recv_sems, local_sems, only_first=False):
    x, y, c = _coords()
    sibling = (x, y, 1 - c)
    local, sends, arrivals, relays, relayed = [], [], [], [], []
    for a, (x_ref, out_ref) in enumerate(zip(ins, outs)):
        local.append(pltpu.make_async_copy(x_ref, out_ref.at[2 * x + y], local_sems.at[a]))
        for j, (cx, cy) in enumerate(_other_chips(x, y)):
            k, theirs = 6 * a + j, 2 * cx + cy
            sends.append(_remote(x_ref.at[c], out_ref.at[2 * x + y, c], send_sems, recv_sems, k, (cx, cy, c)))
            if only_first:
                continue
            arrivals.append(_remote(x_ref.at[c], out_ref.at[theirs, c], send_sems, recv_sems, k, (cx, cy, c)))
            relays.append(_remote(out_ref.at[theirs, c], out_ref.at[theirs, c], send_sems, recv_sems, k + 3, sibling))
            relayed.append(_remote(x_ref.at[c], out_ref.at[theirs, 1 - c], send_sems, recv_sems, k + 3, sibling))
    return local, sends, arrivals, relays, relayed


def _gather_start(copies):
    local, sends, _, _, _ = copies
    for cp in local + sends:
        cp.start()


def _gather_finish(copies):
    local, sends, arrivals, relays, relayed = copies
    for landed, relay in zip(arrivals, relays):
        landed.wait_recv()
        relay.start()
    for cp in relayed:
        cp.wait_recv()
    for cp in sends + relays:
        cp.wait_send()
    for cp in local:
        cp.wait()


def _gather_scratch(na):
    return _dma_sems(6 * na) + [pltpu.SemaphoreType.DMA((na,))]


def _gather_quarters(parts, *, name):
    na = len(parts)

    def body(*refs):
        copies = _gather_copies(refs[:na], refs[na:2 * na], *refs[2 * na:])
        _gather_start(copies)
        _gather_finish(copies)

    return pl.pallas_call(
        body, name=name, in_specs=[ANY] * na, out_specs=[ANY] * na,
        out_shape=[jax.ShapeDtypeStruct((N_CHIPS,) + p.shape, p.dtype) for p in parts],
        scratch_shapes=_gather_scratch(na),
    )(*parts)


def _swap_halves(grads, *, name):
    na = len(grads)

    def body(*refs):
        ins, outs = refs[:na], refs[na:2 * na]
        send_sems, recv_sems = refs[2 * na:]
        x, y, c = _coords()
        sends = [_remote(g_ref.at[j, 1 - c], o_ref.at[j], send_sems, recv_sems, N_CHIPS * a + j, (x, y, 1 - c))
                 for a, (g_ref, o_ref) in enumerate(zip(ins, outs)) for j in range(N_CHIPS)]
        for cp in sends:
            cp.start()
        for cp in sends:
            cp.wait()

    return pl.pallas_call(
        body, name=name, in_specs=[ANY] * na, out_specs=[ANY] * na,
        out_shape=[jax.ShapeDtypeStruct((N_CHIPS,) + g.shape[2:], g.dtype) for g in grads],
        scratch_shapes=_dma_sems(N_CHIPS * na),
    )(*grads)


def _scatter_copies(ins, outs, send_sems, recv_sems):
    x, y, c = _coords()
    return [_remote(p_ref.at[2 * cx + cy], outs[3 * a + j], send_sems, recv_sems, 3 * a + j, (cx, cy, c))
            for a, p_ref in enumerate(ins) for j, (cx, cy) in enumerate(_other_chips(x, y))]


def _scatter_shapes(pairs):
    return [jax.ShapeDtypeStruct(p.shape[1:], p.dtype) for p in pairs for _ in range(3)]


def _scatter_quarters(pairs, *, name):
    na = len(pairs)

    def body(*refs):
        sends = _scatter_copies(refs[:na], refs[na:4 * na], *refs[4 * na:])
        for cp in sends:
            cp.start()
        for cp in sends:
            cp.wait()

    out = pl.pallas_call(
        body, name=name, in_specs=[ANY] * na, out_specs=[ANY] * (3 * na), out_shape=_scatter_shapes(pairs),
        scratch_shapes=_dma_sems(3 * na),
    )(*pairs)
    return [out[3 * a:3 * a + 3] for a in range(na)]


def _share_halves(tots, *, name):
    na = len(tots)

    def body(*refs):
        ins, outs = refs[:na], refs[na:2 * na]
        send_sems, recv_sems = refs[2 * na:]
        x, y, c = _coords()
        sends = [_remote(t_ref, o_ref, send_sems, recv_sems, a, (x, y, 1 - c))
                 for a, (t_ref, o_ref) in enumerate(zip(ins, outs))]
        for cp in sends:
            cp.start()
        for cp in sends:
            cp.wait()

    return pl.pallas_call(
        body, name=name, in_specs=[ANY] * na, out_specs=[ANY] * na,
        out_shape=[jax.ShapeDtypeStruct(t.shape, t.dtype) for t in tots],
        scratch_shapes=_dma_sems(na),
    )(*tots)


def _gather_all(vec, *, name):
    m, w = vec.shape

    def body(x_ref, out_ref, send_sems, recv_sems, local_sem):
        x, y, c = _coords()
        me, sibling = (x, y, c), (x, y, 1 - c)
        chips = _other_chips(x, y)

        def rows(px, py, pc):
            return out_ref.at[pl.ds((4 * px + 2 * py + pc) * m, m), :]

        def copy(k, block, to, src=None):
            return _remote(rows(*block) if src is None else src, rows(*block), send_sems, recv_sems, k, to)

        mine = pltpu.make_async_copy(x_ref, rows(*me), local_sem)
        mine.start()
        first = [copy(0, me, sibling, src=x_ref)]
        first += [copy(1 + j, me, (*chip, c), src=x_ref) for j, chip in enumerate(chips)]
        for cp in first:
            cp.start()
        passed = [copy(4 + j, (*chip, c), sibling) for j, chip in enumerate(chips)]
        for j, chip in enumerate(chips):
            copy(1 + j, (*chip, c), me).wait_recv()
            passed[j].start()
        copy(0, sibling, me).wait_recv()
        for j, chip in enumerate(chips):
            copy(4 + j, (*chip, 1 - c), me).wait_recv()
        for cp in first + passed:
            cp.wait_send()
        mine.wait()

    vm = pl.BlockSpec(memory_space=pltpu.VMEM)
    return pl.pallas_call(
        body, name=name, in_specs=[vm], out_specs=vm, out_shape=jax.ShapeDtypeStruct((N_DEV * m, w), vec.dtype),
        scratch_shapes=_dma_sems(7) + [pltpu.SemaphoreType.DMA(())],
    )(vec)


def _sum_blocks(allv, n, *, name):
    m = allv.shape[0] // n

    def body(a_ref, o_ref):
        acc = a_ref[0:m, :]
        for d in range(1, n):
            acc = acc + a_ref[d * m:(d + 1) * m, :]
        o_ref[...] = acc

    return pl.pallas_call(body, name=name, out_shape=jax.ShapeDtypeStruct((m, allv.shape[1]), allv.dtype))(allv)


EW_BLOCK_BYTES = 1 << 20


def _ew_rows(rows, w):
    return _tile(rows, max(8, (EW_BLOCK_BYTES // (4 * w)) // 8 * 8), 8)


def _add_pair(g, got, c, *, name):
    _, _, rows, w = g.shape
    tr = _ew_rows(rows, w)

    def body(c_ref, g_ref, got_ref, o_ref):
        o_ref[...] = (g_ref[...] + got_ref[...]).astype(o_ref.dtype)

    blk = pl.BlockSpec((None, tr, w), lambda q, i, c_ref: (q, i, 0))
    return pl.pallas_call(
        body, name=name,
        grid_spec=pltpu.PrefetchScalarGridSpec(
            num_scalar_prefetch=1, grid=(N_CHIPS, rows // tr),
            in_specs=[pl.BlockSpec((None, None, tr, w), lambda q, i, c_ref: (q, c_ref[0], i, 0)), blk], out_specs=blk),
        out_shape=jax.ShapeDtypeStruct(got.shape, BF16),
        compiler_params=_params("parallel", "parallel"),
    )(c, g, got)


def _add_chips(pair, recv, chip, *, name):
    _, rows, w = pair.shape
    tr = _ew_rows(rows, w)

    def body(chip_ref, p_ref, r0_ref, r1_ref, r2_ref, o_ref):
        f = lambda r: r[...].astype(F32)
        o_ref[...] = ((f(p_ref) + f(r0_ref)) + f(r1_ref)) + f(r2_ref)

    blk = pl.BlockSpec((tr, w), lambda i, chip_ref: (i, 0))
    return pl.pallas_call(
        body, name=name,
        grid_spec=pltpu.PrefetchScalarGridSpec(
            num_scalar_prefetch=1, grid=(rows // tr,),
            in_specs=[pl.BlockSpec((None, tr, w), lambda i, chip_ref: (chip_ref[0], i, 0)), blk, blk, blk], out_specs=blk),
        out_shape=jax.ShapeDtypeStruct((rows, w), F32),
        compiler_params=_params("parallel"),
    )(chip, pair, *recv)


def _adamw_math(w, g, m, v):
    nm = ADAM_B1 * m + (1.0 - ADAM_B1) * g
    nv = ADAM_B2 * v + (1.0 - ADAM_B2) * (g * g)
    m_hat = nm / (1.0 - ADAM_B1 ** ADAM_STEP)
    v_hat = nv / (1.0 - ADAM_B2 ** ADAM_STEP)
    return -ADAM_LR * (m_hat / (jnp.sqrt(v_hat) + ADAM_EPS) + ADAM_WD * w), nm, nv


def _adamw(w, g, m, v, *, name):
    shape = w.shape
    last = shape[-1]
    w2, g2, m2, v2 = (a.reshape(-1, last) for a in (w, g, m, v))
    rows = w2.shape[0]
    tm = _ew_rows(rows, last)

    def body(w_ref, g_ref, m_ref, v_ref, d_ref, nm_ref, nv_ref):
        d_ref[...], nm_ref[...], nv_ref[...] = _adamw_math(w_ref[...], g_ref[...], m_ref[...], v_ref[...])

    spec = pl.BlockSpec((tm, last), lambda i: (i, 0))
    out = jax.ShapeDtypeStruct((rows, last), F32)
    d, nm, nv = pl.pallas_call(
        body, name=name, grid=(rows // tm,), in_specs=[spec] * 4, out_specs=[spec] * 3, out_shape=[out] * 3,
        compiler_params=_params("parallel"),
    )(w2, g2, m2, v2)
    return d.reshape(shape), nm.reshape(shape), nv.reshape(shape)


def _adamw_halves(w, m, v, mine, theirs, c, *, name, into=None):
    rows, wd = w.shape[-2:]
    tr = _ew_rows(rows, wd)
    bufs, at = into if into is not None else ((), ())

    def body(c_ref, w_ref, m_ref, v_ref, a_ref, b_ref, *rest):
        g_ref, d_ref, nm_ref, nv_ref = rest[len(bufs):]
        g = jnp.where(pl.program_id(0) == c_ref[0], a_ref[...], b_ref[...])
        g_ref[...] = g
        d_ref[...], nm_ref[...], nv_ref[...] = _adamw_math(w_ref[...], g, m_ref[...], v_ref[...])

    full = pl.BlockSpec((None,) * (1 + len(at)) + (tr, wd), lambda hf, i, c_ref: at + (hf, i, 0))
    mine_spec = pl.BlockSpec((tr, wd), lambda hf, i, c_ref: (jnp.where(hf == c_ref[0], i, 0), 0))
    theirs_spec = pl.BlockSpec((tr, wd), lambda hf, i, c_ref: (jnp.where(hf == c_ref[0], 0, i), 0))
    out = jax.ShapeDtypeStruct(w.shape, F32)
    return pl.pallas_call(
        body, name=name,
        grid_spec=pltpu.PrefetchScalarGridSpec(num_scalar_prefetch=1, grid=(2, rows // tr),
                                               in_specs=[full] * 3 + [mine_spec, theirs_spec] + [ANY] * len(bufs),
                                               out_specs=[full] * 4),
        out_shape=[out] * 4, input_output_aliases={6 + b: b for b in range(len(bufs))},
        compiler_params=_params("parallel", "parallel"),
    )(c, w, m, v, mine, theirs, *bufs)


def _join_quarters(q, *, name):
    _, rows, n = q.shape
    tr = _tile(rows, 256, 16)

    def body(q_ref, o_ref):
        o_ref[...] = jnp.concatenate([q_ref[s] for s in range(N_CHIPS)], axis=1)

    return pl.pallas_call(
        body, name=name, grid=(rows // tr,),
        in_specs=[pl.BlockSpec((N_CHIPS, tr, n), lambda i: (0, i, 0))],
        out_specs=pl.BlockSpec((tr, N_CHIPS * n), lambda i: (i, 0)),
        out_shape=jax.ShapeDtypeStruct((rows, N_CHIPS * n), q.dtype),
        compiler_params=_params("parallel"),
    )(q)


def _split_quarters(pieces, *, name):
    rows = pieces[0].shape[0]
    n = sum(p.shape[1] for p in pieces) // N_CHIPS
    tr = _tile(rows, 256, 16)

    def body(*refs):
        x = jnp.concatenate([r[...] for r in refs[:-1]], axis=1)
        for s in range(N_CHIPS):
            refs[-1][s] = x[:, s * n:(s + 1) * n]

    return pl.pallas_call(
        body, name=name, grid=(rows // tr,),
        in_specs=[pl.BlockSpec((tr, p.shape[1]), lambda i: (i, 0)) for p in pieces],
        out_specs=pl.BlockSpec((N_CHIPS, tr, n), lambda i: (0, i, 0)),
        out_shape=jax.ShapeDtypeStruct((N_CHIPS, rows, n), pieces[0].dtype),
        compiler_params=_params("parallel"),
    )(*pieces)


_WEIGHTS = ['ffn_norm', 'ffn_w_gate', 'ffn_w_up', 'ffn_w_down', 'mix_norm', 'att_w_in', 'att_q_norm', 'att_k_norm',
            'att_sinks', 'att_w_out', 'gdn_w_in', 'gdn_conv_w', 'gdn_a_log', 'gdn_dt_bias', 'gdn_out_norm', 'gdn_w_out',
            'ple_norm', 'ple_w_gate', 'ple_w_proj']
_BIG = ['ffn_w_gate', 'ffn_w_up', 'ffn_w_down', 'att_w_in', 'att_w_out', 'gdn_w_in', 'gdn_w_out', 'ple_w_gate',
        'ple_w_proj']
_SMALL_CUT = {'ffn_norm': 2, 'gdn_conv_w': 2}
_WHOLE = ['mix_norm', 'att_q_norm', 'att_k_norm', 'att_sinks', 'gdn_a_log', 'gdn_dt_bias', 'gdn_out_norm', 'ple_norm']
PACK_W = 1024
SMALL_ROW_MULT = 8


def _halves(a):
    return a.reshape(2, -1, a.shape[-1])


def _from_quarters(blk, axis):
    full = jnp.moveaxis(blk, 0, axis)
    shp = list(full.shape)
    shp[axis:axis + 2] = [shp[axis] * shp[axis + 1]]
    return full.reshape(shp)


def _to_quarters(full, axis):
    shp = list(full.shape)
    shp[axis:axis + 1] = [N_CHIPS, shp[axis] // N_CHIPS]
    return jnp.moveaxis(full.reshape(shp), axis, 0)


def _pack(parts, row_mult):
    flat = jnp.concatenate(parts, axis=-1)
    n = flat.shape[-1]
    rows = -(-n // (PACK_W * row_mult)) * row_mult
    return jnp.pad(flat, [(0, rows * PACK_W - n)]).reshape(rows, PACK_W)


def _unpack(flat, shapes):
    lead = flat.shape[:-2]
    flat = flat.reshape(lead + (-1,))
    out, off = [], 0
    for shp in shapes:
        n = math.prod(shp)
        out.append(flat[..., off:off + n].reshape(lead + tuple(shp)))
        off += n
    return out


FFN_TM = 1024
FFN_DW_TK = 2048


def _ffn_up(hn, wg, wu, at, *, name):
    t, d = hn.shape
    fq = wg.shape[-1]
    tm = _tile(t, FFN_TM)

    def body(h_ref, wg_ref, wu_ref, g_ref, u_ref, a_ref):
        h = h_ref[...]
        g, u = _dg(h, _b(wg_ref[...]), 1, 0), _dg(h, _b(wu_ref[...]), 1, 0)
        g_ref[...] = g.astype(BF16)
        u_ref[...] = u.astype(BF16)
        a_ref[...] = _f_swiglu(g, u)[0].astype(BF16)

    w_spec = pl.BlockSpec((None,) * (1 + len(at)) + (d, fq), lambda s, i: (s,) + at + (0, 0))
    o_spec = pl.BlockSpec((None, tm, fq), lambda s, i: (s, i, 0))
    out = jax.ShapeDtypeStruct((N_CHIPS, t, fq), BF16)
    return pl.pallas_call(
        body, name=name, grid=(N_CHIPS, t // tm),
        in_specs=[pl.BlockSpec((tm, d), lambda s, i: (i, 0)), w_spec, w_spec], out_specs=[o_spec] * 3,
        out_shape=[out] * 3, compiler_params=_params("parallel", "parallel"),
    )(hn, wg, wu)


def _ffn_d_up(dout, wd, g, u, at, *, name):
    t, d = dout.shape
    fq = wd.shape[-2]
    tm = _tile(t, FFN_TM)

    def body(do_ref, wd_ref, g_ref, u_ref, dg_ref, du_ref):
        da = _dg(_b(do_ref[...]), _b(wd_ref[...]), 1, 1) * 0.5
        _, vjp = jax.vjp(_f_swiglu, g_ref[...].astype(F32), u_ref[...].astype(F32))
        dg, du = vjp((da,))
        dg_ref[...] = dg.astype(BF16)
        du_ref[...] = du.astype(BF16)

    w_spec = pl.BlockSpec((None,) * (1 + len(at)) + (fq, d), lambda i, s: (s,) + at + (0, 0))
    o_spec = pl.BlockSpec((None, tm, fq), lambda i, s: (s, i, 0))
    out = jax.ShapeDtypeStruct((N_CHIPS, t, fq), BF16)
    return pl.pallas_call(
        body, name=name, grid=(t // tm, N_CHIPS),
        in_specs=[pl.BlockSpec((tm, d), lambda i, s: (i, 0)), w_spec, o_spec, o_spec], out_specs=[o_spec] * 2,
        out_shape=[out] * 2, compiler_params=_params("parallel", "parallel"),
    )(dout, wd, g, u)


def _ffn_fwd(h, gain, wg, wu, wd, at, tag):
    lead = (Q,) + at
    hn, = _row_fwd(_f_rms, [h], [gain], [(D_MODEL, BF16)], name=f"{tag}_norm")
    g, u, a = _ffn_up(hn, wg, wu, at, name=f"{tag}_up")
    out = _mm((a, (Q,)), (wd, lead), res=h, scale=0.5, name=f"{tag}_down")
    return out, (h, hn, g, u, a)


def _ffn_bwd(dout, saved, gain, wg, wu, wd, at, grads, g_at, tag):
    h, hn, g, u, a = saved
    lead = (Q,) + at
    g_lead = (Q,) + g_at
    dg, du = _ffn_d_up(dout, wd, g, u, at, name=f"{tag}_d_up")
    g_gate, g_up, g_down = grads
    g_down = _mm((a, (Q,)), dout, ta=True, scale=0.5, into=(g_down, g_lead), tk=FFN_DW_TK, name=f"{tag}_dw_down")
    g_gate = _mm((dg, (Q,)), hn, ta=True, into=(g_gate, g_lead), tk=FFN_DW_TK, name=f"{tag}_dw_gate")
    g_up = _mm((du, (Q,)), hn, ta=True, into=(g_up, g_lead), tk=FFN_DW_TK, name=f"{tag}_dw_up")
    dhn = _mm((dg, (Q,)), (wg, lead), tb=True, name=f"{tag}_d_norm_gate")
    dh, dgain = _mm((du, (Q,)), (wu, lead), tb=True, res=dhn, norm_bwd=(h, gain, dout), name=f"{tag}_d_in")
    return dh, dgain, (g_gate, g_up, g_down)


def _att_fwd(h, gain, w_in, qg, kg, sinks, w_out, gather):
    hn, = _row_fwd(_f_rms, [h], [gain], [(D_MODEL, BF16)], name="att_norm")
    proj = _mm(hn, w_in, out_dtype=BF16, name="att_in")
    a, rtot, *gathered = _sb_fwd(proj, name="att_sb", gather=gather)
    b = _swa_fwd(proj, qg, kg, sinks, name="att_swa")
    out = _mm(a, (w_out, (0,)), res=h, name="att_out_sb")
    out = _mm(b, (w_out, (1,)), res=out, name="att_out_swa")
    return out, (h, hn, proj, a, rtot, b), gathered


def _att_bwd(dout, saved, gain, w_in, qg, kg, sinks, w_out, scatter):
    h, hn, proj, a, rtot, b = saved
    da = _mm(dout, (w_out, (0,)), tb=True, name="att_d_sb")
    db = _mm(dout, (w_out, (1,)), tb=True, name="att_d_swa")
    dw_out = lax.empty(w_out.shape, F32)
    dw_out = _mm(a, dout, ta=True, into=(dw_out, (0,)), name="att_dw_out_sb")
    dw_out = _mm(b, dout, ta=True, into=(dw_out, (1,)), name="att_dw_out_swa")
    dq, dk, dv, *landed = _sb_bwd(proj, rtot, da, name="att_sb_bwd", scatter=scatter)
    dqb, dkb, dvb, dqg, dkg, dsk = _swa_bwd(proj, qg, kg, sinks, db, name="att_swa_bwd")
    dproj = jnp.concatenate([dq, dk, dv, dqb, dkb, dvb], axis=1)
    dw_in = _mm(hn, dproj, ta=True, name="att_dw_in")
    dh, dgain = _mm(dproj, w_in, tb=True, norm_bwd=(h, gain, dout), name="att_d_in")
    return dh, dgain, dw_in, dqg, dkg, dsk, dw_out, [landed[3 * a:3 * a + 3] for a in range(len(scatter))]


def _gdn_layer_fwd(h, gain, w_in, conv_w, alog, dtb, out_gain, w_out):
    w_qkv, w_z, w_ba = w_in[:, :GDN_CONV_W], w_in[:, GDN_CONV_W:GDN_CONV_W + GDN_VW], w_in[:, GDN_CONV_W + GDN_VW:]
    hn, = _row_fwd(_f_rms, [h], [gain], [(D_MODEL, BF16)], name="gdn_norm")
    pq = _mm(hn, w_qkv, name="gdn_in_qkv")
    pz = _mm(hn, w_z, name="gdn_in_z")
    ba = _mm(hn, w_ba, name="gdn_in_ba")
    act = _conv_fwd(pq, conv_w, name="gdn_conv")
    o, states, inverses = _gdn_fwd(act, ba, alog, dtb, name="gdn_rule")
    y, = _row_fwd(_f_gdn_out, [o, pz], [out_gain], [(GDN_VW, BF16)], name="gdn_gate")
    out = _mm(y, w_out, res=h, name="gdn_out")
    return out, (h, hn, pq, pz, ba, act, o, states, inverses, y, (w_qkv, w_z, w_ba))


def _gdn_layer_bwd(dout, saved, gain, conv_w, alog, dtb, out_gain, w_out):
    h, hn, pq, pz, ba, act, o, states, inverses, y, (w_qkv, w_z, w_ba) = saved
    dy = _mm(dout, w_out, tb=True, name="gdn_d_gate")
    dw_out = _mm(y, dout, ta=True, name="gdn_dw_out")
    do, dpz, dout_gain = _row_bwd(_f_gdn_out, [o, pz], [out_gain], [dy], [(0, F32), (1, F32)], [0], name="gdn_gate_bwd")
    dact, dba, dal, ddb = _gdn_bwd(act, ba, alog, dtb, states, inverses, do, name="gdn_rule_bwd")
    dpq, dconv = _conv_bwd(pq, conv_w, dact, name="gdn_conv_bwd")
    dw_in = [_mm(hn, dpq, ta=True, name="gdn_dw_qkv"), _mm(hn, dpz, ta=True, name="gdn_dw_z"),
             _mm(hn, dba, ta=True, name="gdn_dw_ba")]
    dhn = _mm(dpq, w_qkv, tb=True, name="gdn_d_norm_qkv")
    dhn = _mm(dpz, w_z, tb=True, res=dhn, name="gdn_d_norm_z")
    dh, dgain = _mm(dba, w_ba, tb=True, res=dhn, norm_bwd=(h, gain, dout), name="gdn_d_in")
    return dh, dgain, dw_in, dconv, dal, ddb, dout_gain, dw_out


def _ple_fwd(h, gain, w_gate, w_proj, pe, tag):
    hn, = _row_fwd(_f_rms, [h], [gain], [(D_MODEL, BF16)], name=f"{tag}_norm")
    gl = _mm(hn, w_gate, name=f"{tag}_gate")
    pp = _mm(pe, w_proj, name=f"{tag}_proj")
    out, = _row_fwd(_f_ple, [h, gl, pp], [], [(D_MODEL, F32)], name=f"{tag}_mix")
    return out, (h, hn, gl, pp)


def _ple_bwd(dout, saved, gain, w_gate, pe, tag):
    h, hn, gl, pp = saved
    dha, dgl, dpp = _row_bwd(_f_ple, [h, gl, pp], [], [dout], [(0, F32), (1, BF16), (2, BF16)], [], name=f"{tag}_mix_bwd")
    dw_gate = _mm(hn, dgl, ta=True, name=f"{tag}_dw_gate")
    dw_proj = _mm(pe, dpp, ta=True, name=f"{tag}_dw_proj")
    dh, dgain = _mm(dgl, w_gate, tb=True, norm_bwd=(h, gain, dha), name=f"{tag}_d_in")
    return dh, dgain, dw_gate, dw_proj


def kernel(x, p, ffn_norm, ffn_w_gate, ffn_w_up, ffn_w_down, mix_norm, att_w_in, att_q_norm, att_k_norm, att_sinks, att_w_out, gdn_w_in, gdn_conv_w, gdn_a_log, gdn_dt_bias, gdn_out_norm, gdn_w_out, ple_norm, ple_w_gate, ple_w_proj, loss_target, m_ffn_norm, m_ffn_w_gate, m_ffn_w_up, m_ffn_w_down, m_mix_norm, m_att_w_in, m_att_q_norm, m_att_k_norm, m_att_sinks, m_att_w_out, m_gdn_w_in, m_gdn_conv_w, m_gdn_a_log, m_gdn_dt_bias, m_gdn_out_norm, m_gdn_w_out, m_ple_norm, m_ple_w_gate, m_ple_w_proj, v_ffn_norm, v_ffn_w_gate, v_ffn_w_up, v_ffn_w_down, v_mix_norm, v_att_w_in, v_att_q_norm, v_att_k_norm, v_att_sinks, v_att_w_out, v_gdn_w_in, v_gdn_conv_w, v_gdn_a_log, v_gdn_dt_bias, v_gdn_out_norm, v_gdn_w_out, v_ple_norm, v_ple_w_gate, v_ple_w_proj):
    arg = dict(locals())
    cx, cy, cc = _coords()
    chip = (2 * cx + cy).astype(jnp.int32).reshape(1)
    core = cc.astype(jnp.int32).reshape(1)
    n_layers = ffn_norm.shape[0]

    quarter = lambda n, i=None: _halves((arg[n] if i is None else arg[n][i]).astype(BF16))
    ffn_names = ('ffn_w_gate', 'ffn_w_up', 'ffn_w_down')
    early = [quarter(n, 0) for n in ffn_names] + [quarter('att_w_in'), quarter('att_w_out')]
    late_names = ('gdn_w_in', 'gdn_w_out', 'ple_w_gate', 'ple_w_proj')
    late = [quarter(n, 1) for n in ffn_names] + [quarter(n) for n in late_names]
    *ffn_w0, att_in_q, att_out_q = _gather_quarters(early, name="gather_weights")
    wt = {'att_w_in': _join_quarters(att_in_q.reshape((N_CHIPS,) + att_w_in.shape[1:]), name="att_w_in_join"),
          'att_w_out': att_out_q.reshape(2, SB_W, D_MODEL)}

    small_names = list(_SMALL_CUT)
    small_shapes = [arg[n].shape for n in small_names]
    svec = _pack([arg[n].reshape(-1) for n in small_names], SMALL_ROW_MULT)
    srows = svec.shape[0]
    sall = _gather_all(svec, name="gather_gains").reshape(N_CHIPS, 2, srows, PACK_W)[:, 0]
    for n, q in zip(small_names, _unpack(sall, small_shapes)):
        wt[n] = _from_quarters(q, _SMALL_CUT[n])
    row = lambda v: v.reshape(1, -1)

    as_ffn = lambda g, n: g.reshape((N_CHIPS,) + arg[n].shape[1:])
    ffn_w = [tuple(as_ffn(g, n) for g, n in zip(ffn_w0, ffn_names)), None]
    h = x[0]
    tape = []
    for i in range(n_layers):
        j = i // 2
        h, s0 = _ffn_fwd(h, row(wt['ffn_norm'][i, 0]), *ffn_w[i], (0,), f"ffn{i}a")
        if i % 2 == 0:
            h, sm, gathered = _att_fwd(h, row(mix_norm[i]), wt['att_w_in'], att_q_norm[j:j + 1], att_k_norm[j:j + 1],
                                       att_sinks[j:j + 1], wt['att_w_out'], late)
            ffn_w[1] = tuple(as_ffn(g, n) for g, n in zip(gathered[:3], ffn_names))
            wq = {n: g.reshape((N_CHIPS,) + arg[n].shape) for n, g in zip(late_names, gathered[3:])}
            wt['gdn_w_in'] = _join_quarters(wq['gdn_w_in'][:, 0], name="gdn_w_in_join")
            wt['gdn_w_out'] = wq['gdn_w_out'].reshape(GDN_VW, D_MODEL)
            wt['ple_w_gate'] = _from_quarters(wq['ple_w_gate'], 1)
            wt['ple_w_proj'] = _from_quarters(wq['ple_w_proj'], 2)
        else:
            h, sm = _gdn_layer_fwd(h, row(mix_norm[i]), wt['gdn_w_in'], wt['gdn_conv_w'][j], gdn_a_log[j:j + 1],
                                   gdn_dt_bias[j:j + 1], gdn_out_norm[j:j + 1], wt['gdn_w_out'])
        h, s1 = _ffn_fwd(h, row(wt['ffn_norm'][i, 1]), *ffn_w[i], (1,), f"ffn{i}b")
        h, sp = _ple_fwd(h, row(ple_norm[i]), wt['ple_w_gate'][i], wt['ple_w_proj'][i], p[i, 0], f"ple{i}")
        tape.append((s0, sm, s1, sp))

    dh, loss_local = _loss_head(h, loss_target[0], name="loss_head")
    loss = lax.psum(loss_local, ("x", "y", "c"))

    gr = {}
    stored_t = ('ffn_w_gate', 'ffn_w_up')
    as_stored = lambda a, n: jnp.swapaxes(a, -1, -2) if n in stored_t else a
    ffn_g = [[tuple(lax.empty((N_CHIPS,) + as_stored(arg[n], n).shape[2:], F32) for n in ffn_names) for _ in range(2)]
             for _ in range(n_layers)]
    ffn_keys = lambda i, k: [f"{n}_{i}{k}" for n in ffn_names]
    d_ffn_norm = [[None, None] for _ in range(n_layers)]
    d_mix, d_ple_norm, d_ple_gate, d_ple_proj = [None] * n_layers, [None] * n_layers, [None] * n_layers, [None] * n_layers

    def as_halves(g):
        return g.reshape((N_CHIPS, 2, -1, g.shape[-1]))

    def pair_up(keys, grads, tag):
        got = _swap_halves(grads, name=f"grad_swap_halves_{tag}")
        return [_add_pair(g, o, core, name=f"grad_add_pair_{k}") for k, g, o in zip(keys, grads, got)]

    for i in reversed(range(n_layers)):
        j = i // 2
        s0, sm, s1, sp = tape[i]
        dh, d_ple_norm[i], d_ple_gate[i], d_ple_proj[i] = _ple_bwd(dh, sp, row(ple_norm[i]), wt['ple_w_gate'][i], p[i, 0],
                                                                   f"ple{i}")
        dh, d_ffn_norm[i][1], ffn_g[i][1] = _ffn_bwd(dh, s1, row(wt['ffn_norm'][i, 1]), *ffn_w[i], (1,), ffn_g[i][1], (),
                                                     f"ffn{i}b")
        if i % 2 == 0:
            gr['ple_w_gate'] = _to_quarters(jnp.stack(d_ple_gate), 1)
            gr['ple_w_proj'] = _to_quarters(jnp.stack(d_ple_proj), 2)
            first_keys = ffn_keys(1, 0) + ffn_keys(1, 1) + ffn_keys(0, 1) + list(late_names)
            first_pairs = pair_up(first_keys, [as_halves(g) for g in ffn_g[1][0] + ffn_g[1][1] + ffn_g[0][1]]
                                  + [as_halves(gr[n]) for n in late_names], "a")
            (dh, d_mix[i], dw_in, gr['att_q_norm'], gr['att_k_norm'], gr['att_sinks'], dw_out,
             first_recv) = _att_bwd(dh, sm, row(mix_norm[i]), wt['att_w_in'], att_q_norm[j:j + 1],
                                    att_k_norm[j:j + 1], att_sinks[j:j + 1], wt['att_w_out'], first_pairs)
            gr['att_w_in'] = _split_quarters([dw_in], name="att_dw_in_split")
            gr['att_w_out'] = dw_out
        else:
            (dh, d_mix[i], dw_in, dconv, gr['gdn_a_log'], gr['gdn_dt_bias'], gr['gdn_out_norm'],
             dw_out) = _gdn_layer_bwd(dh, sm, row(mix_norm[i]), wt['gdn_conv_w'][j], gdn_a_log[j:j + 1],
                                      gdn_dt_bias[j:j + 1], gdn_out_norm[j:j + 1], wt['gdn_w_out'])
            gr['gdn_w_in'] = _split_quarters(dw_in, name="gdn_dw_in_split")
            gr['gdn_w_out'] = dw_out
            gr['gdn_conv_w'] = dconv[None]
        dh, d_ffn_norm[i][0], ffn_g[i][0] = _ffn_bwd(dh, s0, row(wt['ffn_norm'][i, 0]), *ffn_w[i], (0,), ffn_g[i][0], (),
                                                     f"ffn{i}a")
    grad_x = dh[None]

    gr['ffn_norm'] = jnp.stack([jnp.stack([d_ffn_norm[i][k][0] for k in range(2)]) for i in range(n_layers)])
    gr['mix_norm'] = jnp.concatenate(d_mix, axis=0)
    gr['ple_norm'] = jnp.concatenate(d_ple_norm, axis=0)

    last_keys = ffn_keys(0, 0) + ['att_w_in', 'att_w_out']
    last_pairs = pair_up(last_keys, [as_halves(g) for g in ffn_g[0][0]] + [as_halves(gr['att_w_in']), as_halves(gr['att_w_out'])],
                         "b")
    last_recv = _scatter_quarters(last_pairs, name="grad_scatter")
    keys = first_keys + last_keys
    tots = [_add_chips(pr, rc, chip, name=f"grad_add_chips_{k}")
            for k, pr, rc in zip(keys, first_pairs + last_pairs, first_recv + last_recv)]
    theirs = _share_halves(tots, name="grad_share")
    summed = dict(zip(keys, zip(tots, theirs)))

    whole_shapes = [arg[n].shape for n in _WHOLE]
    cut_full_shapes = [gr[n].shape for n in small_names]
    gvec = _pack([gr[n].reshape(-1) for n in _WHOLE + small_names], SMALL_ROW_MULT)
    gall = _sum_blocks(_gather_all(gvec, name="gather_small_grads"), N_DEV, name="sum_small_grads")
    parts = _unpack(gall, whole_shapes + cut_full_shapes)
    gsum = dict(zip(_WHOLE, parts))
    for n, g in zip(small_names, parts[len(_WHOLE):]):
        gsum[n] = lax.dynamic_index_in_dim(_to_quarters(g, _SMALL_CUT[n]), chip[0], axis=0, keepdims=False)

    delta, new_m, new_v = {}, {}, {}
    for n in ('att_w_in', 'att_w_out') + late_names:
        res = _adamw_halves(_halves(arg[n]), _halves(arg["m_" + n]), _halves(arg["v_" + n]), *summed[n], core,
                            name=f"adamw_{n}")
        gsum[n], delta[n], new_m[n], new_v[n] = (r.reshape(arg[n].shape) for r in res)
    for n in ffn_names:
        stored = as_stored(arg[n], n).shape
        in_halves = lambda a: as_stored(a, n).reshape(stored[:2] + (2, stored[2] // 2, stored[3]))
        wmv = [in_halves(arg[k + n]) for k in ("", "m_", "v_")]
        res = tuple(lax.empty(wmv[0].shape, F32) for _ in range(4))
        for i in range(n_layers):
            for k in range(2):
                res = _adamw_halves(*wmv, *summed[f"{n}_{i}{k}"], core, name=f"adamw_{n}_{i}{k}", into=(res, (i, k)))
        gsum[n], delta[n], new_m[n], new_v[n] = (as_stored(r.reshape(stored), n) for r in res)
    for n in _WHOLE + small_names:
        delta[n], new_m[n], new_v[n] = _adamw(arg[n], gsum[n], arg["m_" + n], arg["v_" + n], name=f"adamw_{n}")
    return (loss, grad_x, *[gsum[n] for n in _WEIGHTS], *[delta[n] for n in _WEIGHTS],
            *[new_m[n] for n in _WEIGHTS], *[new_v[n] for n in _WEIGHTS])
```
